```python
import jax, jax.numpy as jnp
from jax import lax
import numpy as np

D_MODEL = 1024
BATCH = 32
SEQ = 2048
DEPTH = 1

N_META = 16
Q_BLOCK = 128
EPS = 1e-6
FOX_HEADS = 8
FOX_HEAD_DIM = 64
FOX_W = FOX_HEADS * FOX_HEAD_DIM
MLA_HEADS = 8
MLA_Q_RANK = 256
MLA_KV_RANK = 128
MLA_NOPE_DIM = 64
MLA_ROPE_DIM = 32
MLA_QK_DIM = MLA_NOPE_DIM + MLA_ROPE_DIM
MLA_V_DIM = 64
MLA_W = MLA_HEADS * MLA_V_DIM
ROPE_THETA = 10000.0
D_FF = 2816
N_BRANCH = 2
IN_SIZES = (FOX_W, FOX_W, FOX_W, FOX_HEADS, MLA_Q_RANK, MLA_KV_RANK, MLA_ROPE_DIM, N_BRANCH * D_MODEL)
IN_W = FOX_W * 3 + FOX_HEADS + MLA_Q_RANK + MLA_KV_RANK + MLA_ROPE_DIM + N_BRANCH * D_MODEL

kernel_name = "hybrid_fox_mla_macaron_meta"


def rms_norm(x, gain):
    xf = x.astype(jnp.float32)
    y = xf * lax.rsqrt(jnp.mean(xf * xf, axis=-1, keepdims=True) + EPS)
    return (y * gain.astype(jnp.float32)).astype(x.dtype)


def swiglu_ffn(u, w_gu, w_down):
    g, up = jnp.split(u @ w_gu, 2, axis=-1)
    return (jax.nn.silu(g) * up) @ w_down


def rope(x, cos, sin):
    xf = x.astype(jnp.float32)
    x1, x2 = jnp.split(xf, 2, axis=-1)
    out = jnp.concatenate([x1 * cos - x2 * sin, x1 * sin + x2 * cos], axis=-1)
    return out.astype(x.dtype)


def block_causal_attention(q, k, v, log_forget_cum=None):
    L = q.shape[2]
    scale = q.shape[-1] ** -0.5
    bounds = [(0, N_META)] + [(N_META + i * Q_BLOCK, N_META + (i + 1) * Q_BLOCK)
                              for i in range((L - N_META) // Q_BLOCK)]
    outs = []
    for qs, qe in bounds:
        qb = q[:, :, qs:qe].astype(jnp.float32)
        kb = k[:, :, :qe].astype(jnp.float32)
        s = jnp.einsum("bhqd,bhkd->bhqk", qb, kb) * scale
        if log_forget_cum is not None:
            s = s + log_forget_cum[:, :, qs:qe, None] - log_forget_cum[:, :, None, :qe]
        mask = jnp.arange(qs, qe)[:, None] >= jnp.arange(qe)[None, :]
        p = jax.nn.softmax(jnp.where(mask, s, -jnp.inf), axis=-1)
        outs.append(jnp.einsum("bhqk,bhkd->bhqd", p.astype(v.dtype), v[:, :, :qe]))
    return jnp.concatenate(outs, axis=2)


def split_heads(t, n_heads):
    B, L, _ = t.shape
    return t.reshape(B, L, n_heads, -1).transpose(0, 2, 1, 3)


def merge_heads(t):
    B, H, L, dh = t.shape
    return t.transpose(0, 2, 1, 3).reshape(B, L, H * dh)


def token_mixing(u, w_in, b_forget, b_gate, fox_q_norm, fox_k_norm, mla_cq_norm, mla_w_uq,
                 mla_ckv_norm, mla_w_ukv, mla_q_norm, mla_k_norm, w_branch_fox, w_branch_mla,
                 w_out):
    B, L, _ = u.shape
    cuts = list(np.cumsum(IN_SIZES)[:-1])
    fq, fk, fv, f_logit, c_q, c_kv, k_rope, gate_logit = jnp.split(u @ w_in, cuts, axis=-1)

    fq = rms_norm(split_heads(fq, FOX_HEADS), fox_q_norm)
    fk = rms_norm(split_heads(fk, FOX_HEADS), fox_k_norm)
    fv = split_heads(fv, FOX_HEADS)
    log_f = jax.nn.log_sigmoid(f_logit.astype(jnp.float32) + b_forget.astype(jnp.float32))
    log_f_cum = jnp.cumsum(log_f.transpose(0, 2, 1), axis=-1)
    o_fox = merge_heads(block_causal_attention(fq, fk, fv, log_f_cum)) @ w_branch_fox

    q = split_heads(rms_norm(c_q, mla_cq_norm) @ mla_w_uq, MLA_HEADS)
    kv = split_heads(rms_norm(c_kv, mla_ckv_norm) @ mla_w_ukv, MLA_HEADS)
    k_nope, v = jnp.split(kv, [MLA_NOPE_DIM], axis=-1)
    k_rope_h = jnp.broadcast_to(k_rope[:, None], (B, MLA_HEADS, L, MLA_ROPE_DIM))
    k = jnp.concatenate([k_nope, k_rope_h], axis=-1)
    q = rms_norm(q, mla_q_norm)
    k = rms_norm(k, mla_k_norm)
    pos = jnp.arange(L, dtype=jnp.float32)
    inv_freq = ROPE_THETA ** (-jnp.arange(0, MLA_ROPE_DIM, 2, dtype=jnp.float32) / MLA_ROPE_DIM)
    ang = pos[:, None] * inv_freq[None, :]
    cos, sin = jnp.cos(ang), jnp.sin(ang)
    q = jnp.concatenate([q[..., :MLA_NOPE_DIM], rope(q[..., MLA_NOPE_DIM:], cos, sin)], axis=-1)
    k = jnp.concatenate([k[..., :MLA_NOPE_DIM], rope(k[..., MLA_NOPE_DIM:], cos, sin)], axis=-1)
    o_mla = merge_heads(block_causal_attention(q, k, v)) @ w_branch_mla

    g_fox, g_mla = jnp.split(jax.nn.sigmoid(gate_logit + b_gate), N_BRANCH, axis=-1)
    return (g_fox * o_fox + g_mla * o_mla) @ w_out


def _fwd_setup_inputs(seed: int = 0) -> dict:
    key = jax.random.key(seed)
    ks = iter(jax.random.split(key, 32))

    def w(shape, fan_in):
        return jax.random.normal(next(ks), shape, jnp.float32) * fan_in ** -0.5

    def gain(shape):
        return 1.0 + 0.1 * jax.random.normal(next(ks), shape, jnp.float32)

    def bias(shape, scale=0.1, center=0.0):
        return center + scale * jax.random.normal(next(ks), shape, jnp.float32)

    Dp = DEPTH
    return {
        "x": jax.random.normal(next(ks), (BATCH, SEQ, D_MODEL), jnp.float32),
        "meta_tokens": jax.random.normal(next(ks), (N_META, D_MODEL), jnp.float32),
        "ffn1_norm": gain((Dp, D_MODEL)),
        "ffn1_w_gu": w((Dp, D_MODEL, 2 * D_FF), D_MODEL),
        "ffn1_w_down": w((Dp, D_FF, D_MODEL), D_FF),
        "mix_norm": gain((Dp, D_MODEL)),
        "w_in": w((Dp, D_MODEL, IN_W), D_MODEL),
        "b_forget": bias((Dp, FOX_HEADS), 0.1, 2.0),
        "b_gate": bias((Dp, N_BRANCH * D_MODEL)),
        "fox_q_norm": gain((Dp, FOX_HEAD_DIM)),
        "fox_k_norm": gain((Dp, FOX_HEAD_DIM)),
        "mla_cq_norm": gain((Dp, MLA_Q_RANK)),
        "mla_w_uq": w((Dp, MLA_Q_RANK, MLA_HEADS * MLA_QK_DIM), MLA_Q_RANK),
        "mla_ckv_norm": gain((Dp, MLA_KV_RANK)),
        "mla_w_ukv": w((Dp, MLA_KV_RANK, MLA_HEADS * (MLA_NOPE_DIM + MLA_V_DIM)), MLA_KV_RANK),
        "mla_q_norm": gain((Dp, MLA_QK_DIM)),
        "mla_k_norm": gain((Dp, MLA_QK_DIM)),
        "w_branch_fox": w((Dp, FOX_W, D_MODEL), FOX_W),
        "w_branch_mla": w((Dp, MLA_W, D_MODEL), MLA_W),
        "w_out": w((Dp, D_MODEL, D_MODEL), D_MODEL),
        "ffn2_norm": gain((Dp, D_MODEL)),
        "ffn2_w_gu": w((Dp, D_MODEL, 2 * D_FF), D_MODEL),
        "ffn2_w_down": w((Dp, D_FF, D_MODEL), D_FF),
    }


def _fwd_reference(x, meta_tokens, ffn1_norm, ffn1_w_gu, ffn1_w_down, mix_norm, w_in, b_forget, b_gate,
              fox_q_norm, fox_k_norm, mla_cq_norm, mla_w_uq, mla_ckv_norm, mla_w_ukv, mla_q_norm,
              mla_k_norm, w_branch_fox, w_branch_mla, w_out, ffn2_norm, ffn2_w_gu, ffn2_w_down):
    B = x.shape[0]
    meta = jnp.broadcast_to(meta_tokens.astype(x.dtype)[None], (B, N_META, D_MODEL))
    h = jnp.concatenate([meta, x], axis=1)
    for l in range(DEPTH):
        h = h + 0.5 * swiglu_ffn(rms_norm(h, ffn1_norm[l]), ffn1_w_gu[l], ffn1_w_down[l])
        h = h + token_mixing(rms_norm(h, mix_norm[l]), w_in[l], b_forget[l], b_gate[l],
                             fox_q_norm[l], fox_k_norm[l], mla_cq_norm[l], mla_w_uq[l],
                             mla_ckv_norm[l], mla_w_ukv[l], mla_q_norm[l], mla_k_norm[l],
                             w_branch_fox[l], w_branch_mla[l], w_out[l])
        h = h + 0.5 * swiglu_ffn(rms_norm(h, ffn2_norm[l]), ffn2_w_gu[l], ffn2_w_down[l])
    return h[:, N_META:]


import jax as _jax
import jax.numpy as _jnp

TWIN_FORMAT = 'train_step'
FWD_PARAMS = ['x', 'meta_tokens', 'ffn1_norm', 'ffn1_w_gu', 'ffn1_w_down', 'mix_norm', 'w_in', 'b_forget', 'b_gate', 'fox_q_norm', 'fox_k_norm', 'mla_cq_norm', 'mla_w_uq', 'mla_ckv_norm', 'mla_w_ukv', 'mla_q_norm', 'mla_k_norm', 'w_branch_fox', 'w_branch_mla', 'w_out', 'ffn2_norm', 'ffn2_w_gu', 'ffn2_w_down']
TWIN_WEIGHTS = ['meta_tokens', 'ffn1_norm', 'ffn1_w_gu', 'ffn1_w_down', 'mix_norm', 'w_in', 'b_forget', 'b_gate', 'fox_q_norm', 'fox_k_norm', 'mla_cq_norm', 'mla_w_uq', 'mla_ckv_norm', 'mla_w_ukv', 'mla_q_norm', 'mla_k_norm', 'w_branch_fox', 'w_branch_mla', 'w_out', 'ffn2_norm', 'ffn2_w_gu', 'ffn2_w_down']
TWIN_DIFF_INPUT = 'x'
TWIN_INPUTS = ['x', 'meta_tokens', 'ffn1_norm', 'ffn1_w_gu', 'ffn1_w_down', 'mix_norm', 'w_in', 'b_forget', 'b_gate', 'fox_q_norm', 'fox_k_norm', 'mla_cq_norm', 'mla_w_uq', 'mla_ckv_norm', 'mla_w_ukv', 'mla_q_norm', 'mla_k_norm', 'w_branch_fox', 'w_branch_mla', 'w_out', 'ffn2_norm', 'ffn2_w_gu', 'ffn2_w_down', 'loss_target', 'm_meta_tokens', 'm_ffn1_norm', 'm_ffn1_w_gu', 'm_ffn1_w_down', 'm_mix_norm', 'm_w_in', 'm_b_forget', 'm_b_gate', 'm_fox_q_norm', 'm_fox_k_norm', 'm_mla_cq_norm', 'm_mla_w_uq', 'm_mla_ckv_norm', 'm_mla_w_ukv', 'm_mla_q_norm', 'm_mla_k_norm', 'm_w_branch_fox', 'm_w_branch_mla', 'm_w_out', 'm_ffn2_norm', 'm_ffn2_w_gu', 'm_ffn2_w_down', 'v_meta_tokens', 'v_ffn1_norm', 'v_ffn1_w_gu', 'v_ffn1_w_down', 'v_mix_norm', 'v_w_in', 'v_b_forget', 'v_b_gate', 'v_fox_q_norm', 'v_fox_k_norm', 'v_mla_cq_norm', 'v_mla_w_uq', 'v_mla_ckv_norm', 'v_mla_w_ukv', 'v_mla_q_norm', 'v_mla_k_norm', 'v_w_branch_fox', 'v_w_branch_mla', 'v_w_out', 'v_ffn2_norm', 'v_ffn2_w_gu', 'v_ffn2_w_down']
TWIN_OUTPUTS = ['loss', 'grad_x', 'grad_meta_tokens', 'grad_ffn1_norm', 'grad_ffn1_w_gu', 'grad_ffn1_w_down', 'grad_mix_norm', 'grad_w_in', 'grad_b_forget', 'grad_b_gate', 'grad_fox_q_norm', 'grad_fox_k_norm', 'grad_mla_cq_norm', 'grad_mla_w_uq', 'grad_mla_ckv_norm', 'grad_mla_w_ukv', 'grad_mla_q_norm', 'grad_mla_k_norm', 'grad_w_branch_fox', 'grad_w_branch_mla', 'grad_w_out', 'grad_ffn2_norm', 'grad_ffn2_w_gu', 'grad_ffn2_w_down', 'delta_meta_tokens', 'delta_ffn1_norm', 'delta_ffn1_w_gu', 'delta_ffn1_w_down', 'delta_mix_norm', 'delta_w_in', 'delta_b_forget', 'delta_b_gate', 'delta_fox_q_norm', 'delta_fox_k_norm', 'delta_mla_cq_norm', 'delta_mla_w_uq', 'delta_mla_ckv_norm', 'delta_mla_w_ukv', 'delta_mla_q_norm', 'delta_mla_k_norm', 'delta_w_branch_fox', 'delta_w_branch_mla', 'delta_w_out', 'delta_ffn2_norm', 'delta_ffn2_w_gu', 'delta_ffn2_w_down', 'new_m_meta_tokens', 'new_m_ffn1_norm', 'new_m_ffn1_w_gu', 'new_m_ffn1_w_down', 'new_m_mix_norm', 'new_m_w_in', 'new_m_b_forget', 'new_m_b_gate', 'new_m_fox_q_norm', 'new_m_fox_k_norm', 'new_m_mla_cq_norm', 'new_m_mla_w_uq', 'new_m_mla_ckv_norm', 'new_m_mla_w_ukv', 'new_m_mla_q_norm', 'new_m_mla_k_norm', 'new_m_w_branch_fox', 'new_m_w_branch_mla', 'new_m_w_out', 'new_m_ffn2_norm', 'new_m_ffn2_w_gu', 'new_m_ffn2_w_down', 'new_v_meta_tokens', 'new_v_ffn1_norm', 'new_v_ffn1_w_gu', 'new_v_ffn1_w_down', 'new_v_mix_norm', 'new_v_w_in', 'new_v_b_forget', 'new_v_b_gate', 'new_v_fox_q_norm', 'new_v_fox_k_norm', 'new_v_mla_cq_norm', 'new_v_mla_w_uq', 'new_v_mla_ckv_norm', 'new_v_mla_w_ukv', 'new_v_mla_q_norm', 'new_v_mla_k_norm', 'new_v_w_branch_fox', 'new_v_w_branch_mla', 'new_v_w_out', 'new_v_ffn2_norm', 'new_v_ffn2_w_gu', 'new_v_ffn2_w_down']
TWIN_LEAF_KINDS = {'loss': 'loss', 'grad_x': 'grad_x', 'grad_meta_tokens': 'grad_w', 'grad_ffn1_norm': 'grad_w', 'grad_ffn1_w_gu': 'grad_w', 'grad_ffn1_w_down': 'grad_w', 'grad_mix_norm': 'grad_w', 'grad_w_in': 'grad_w', 'grad_b_forget': 'grad_w', 'grad_b_gate': 'grad_w', 'grad_fox_q_norm': 'grad_w', 'grad_fox_k_norm': 'grad_w', 'grad_mla_cq_norm': 'grad_w', 'grad_mla_w_uq': 'grad_w', 'grad_mla_ckv_norm': 'grad_w', 'grad_mla_w_ukv': 'grad_w', 'grad_mla_q_norm': 'grad_w', 'grad_mla_k_norm': 'grad_w', 'grad_w_branch_fox': 'grad_w', 'grad_w_branch_mla': 'grad_w', 'grad_w_out': 'grad_w', 'grad_ffn2_norm': 'grad_w', 'grad_ffn2_w_gu': 'grad_w', 'grad_ffn2_w_down': 'grad_w', 'delta_meta_tokens': 'delta_w', 'delta_ffn1_norm': 'delta_w', 'delta_ffn1_w_gu': 'delta_w', 'delta_ffn1_w_down': 'delta_w', 'delta_mix_norm': 'delta_w', 'delta_w_in': 'delta_w', 'delta_b_forget': 'delta_w', 'delta_b_gate': 'delta_w', 'delta_fox_q_norm': 'delta_w', 'delta_fox_k_norm': 'delta_w', 'delta_mla_cq_norm': 'delta_w', 'delta_mla_w_uq': 'delta_w', 'delta_mla_ckv_norm': 'delta_w', 'delta_mla_w_ukv': 'delta_w', 'delta_mla_q_norm': 'delta_w', 'delta_mla_k_norm': 'delta_w', 'delta_w_branch_fox': 'delta_w', 'delta_w_branch_mla': 'delta_w', 'delta_w_out': 'delta_w', 'delta_ffn2_norm': 'delta_w', 'delta_ffn2_w_gu': 'delta_w', 'delta_ffn2_w_down': 'delta_w', 'new_m_meta_tokens': 'new_m', 'new_m_ffn1_norm': 'new_m', 'new_m_ffn1_w_gu': 'new_m', 'new_m_ffn1_w_down': 'new_m', 'new_m_mix_norm': 'new_m', 'new_m_w_in': 'new_m', 'new_m_b_forget': 'new_m', 'new_m_b_gate': 'new_m', 'new_m_fox_q_norm': 'new_m', 'new_m_fox_k_norm': 'new_m', 'new_m_mla_cq_norm': 'new_m', 'new_m_mla_w_uq': 'new_m', 'new_m_mla_ckv_norm': 'new_m', 'new_m_mla_w_ukv': 'new_m', 'new_m_mla_q_norm': 'new_m', 'new_m_mla_k_norm': 'new_m', 'new_m_w_branch_fox': 'new_m', 'new_m_w_branch_mla': 'new_m', 'new_m_w_out': 'new_m', 'new_m_ffn2_norm': 'new_m', 'new_m_ffn2_w_gu': 'new_m', 'new_m_ffn2_w_down': 'new_m', 'new_v_meta_tokens': 'new_v', 'new_v_ffn1_norm': 'new_v', 'new_v_ffn1_w_gu': 'new_v', 'new_v_ffn1_w_down': 'new_v', 'new_v_mix_norm': 'new_v', 'new_v_w_in': 'new_v', 'new_v_b_forget': 'new_v', 'new_v_b_gate': 'new_v', 'new_v_fox_q_norm': 'new_v', 'new_v_fox_k_norm': 'new_v', 'new_v_mla_cq_norm': 'new_v', 'new_v_mla_w_uq': 'new_v', 'new_v_mla_ckv_norm': 'new_v', 'new_v_mla_w_ukv': 'new_v', 'new_v_mla_q_norm': 'new_v', 'new_v_mla_k_norm': 'new_v', 'new_v_w_branch_fox': 'new_v', 'new_v_w_branch_mla': 'new_v', 'new_v_w_out': 'new_v', 'new_v_ffn2_norm': 'new_v', 'new_v_ffn2_w_gu': 'new_v', 'new_v_ffn2_w_down': 'new_v'}


def _forward(args):
    return _fwd_reference(*[args[k] for k in FWD_PARAMS])


def _output_shape():
    out = _jax.eval_shape(lambda: _forward(_fwd_setup_inputs(0)))
    return out.shape, out.dtype

N_MICROBATCH = 1
ADAM_LR = 0.001
ADAM_B1 = 0.9
ADAM_B2 = 0.999
ADAM_EPS = 1e-08
ADAM_WD = 0.01
ADAM_STEP = 10
PER_EXAMPLE_BATCH_AXIS = {'x': 0, 'loss_target': 0}
SHARED_INPUTS = []
_WEIGHT_DTYPES = {'meta_tokens': _jnp.float32, 'ffn1_norm': _jnp.float32, 'ffn1_w_gu': _jnp.float32, 'ffn1_w_down': _jnp.float32, 'mix_norm': _jnp.float32, 'w_in': _jnp.float32, 'b_forget': _jnp.float32, 'b_gate': _jnp.float32, 'fox_q_norm': _jnp.float32, 'fox_k_norm': _jnp.float32, 'mla_cq_norm': _jnp.float32, 'mla_w_uq': _jnp.float32, 'mla_ckv_norm': _jnp.float32, 'mla_w_ukv': _jnp.float32, 'mla_q_norm': _jnp.float32, 'mla_k_norm': _jnp.float32, 'w_branch_fox': _jnp.float32, 'w_branch_mla': _jnp.float32, 'w_out': _jnp.float32, 'ffn2_norm': _jnp.float32, 'ffn2_w_gu': _jnp.float32, 'ffn2_w_down': _jnp.float32}
MOMENT_SCALE = {'meta_tokens': 1.341824e-02, 'ffn1_norm': 1.254851e+01, 'ffn1_w_gu': 9.991378e-02, 'ffn1_w_down': 1.723582e-01, 'mix_norm': 3.969793e+00, 'w_in': 1.194009e-01, 'b_forget': 1.323377e+02, 'b_gate': 7.499842e-01, 'fox_q_norm': 1.935699e+01, 'fox_k_norm': 1.934364e+01, 'mla_cq_norm': 8.397915e-02, 'mla_w_uq': 5.003927e-02, 'mla_ckv_norm': 7.758547e-01, 'mla_w_ukv': 6.598331e-02, 'mla_q_norm': 9.294710e-01, 'mla_k_norm': 9.113184e-01, 'w_branch_fox': 1.563801e-01, 'w_branch_mla': 4.897283e-02, 'w_out': 1.411476e-01, 'ffn2_norm': 1.283958e+01, 'ffn2_w_gu': 9.147620e-02, 'ffn2_w_down': 1.609867e-01}


def _to_microbatches(a, axis):
    t = _jnp.moveaxis(a, axis, 0)
    t = t.reshape((N_MICROBATCH, t.shape[0] // N_MICROBATCH) + t.shape[1:])
    return _jnp.moveaxis(t, 1, axis + 1)


def setup_inputs(seed: int = 0) -> dict:
    inp = _fwd_setup_inputs(seed)
    key = _jax.random.fold_in(_jax.random.key(seed), 7919)
    shape, _ = _output_shape()
    out = dict(inp)
    out["loss_target"] = _jax.random.normal(_jax.random.fold_in(key, 0), shape, _jnp.float32)
    for i, name in enumerate(TWIN_WEIGHTS):
        w = inp[name].astype(_jnp.float32)
        if MOMENT_SCALE is None:
            s = _jnp.sqrt(_jnp.mean(_jnp.square(w)) + 1e-30)
        else:
            s = MOMENT_SCALE[name]
        km, kv = _jax.random.split(_jax.random.fold_in(key, i + 1))
        out[name] = w
        out["m_" + name] = s * _jax.random.normal(km, w.shape, _jnp.float32)
        out["v_" + name] = (s * s) * _jax.random.uniform(kv, w.shape, _jnp.float32, 0.5, 1.5)
    if N_MICROBATCH > 1:
        for name, axis in PER_EXAMPLE_BATCH_AXIS.items():
            out[name] = _to_microbatches(out[name], axis)
    return {'x': out['x'], 'meta_tokens': out['meta_tokens'], 'ffn1_norm': out['ffn1_norm'], 'ffn1_w_gu': out['ffn1_w_gu'], 'ffn1_w_down': out['ffn1_w_down'], 'mix_norm': out['mix_norm'], 'w_in': out['w_in'], 'b_forget': out['b_forget'], 'b_gate': out['b_gate'], 'fox_q_norm': out['fox_q_norm'], 'fox_k_norm': out['fox_k_norm'], 'mla_cq_norm': out['mla_cq_norm'], 'mla_w_uq': out['mla_w_uq'], 'mla_ckv_norm': out['mla_ckv_norm'], 'mla_w_ukv': out['mla_w_ukv'], 'mla_q_norm': out['mla_q_norm'], 'mla_k_norm': out['mla_k_norm'], 'w_branch_fox': out['w_branch_fox'], 'w_branch_mla': out['w_branch_mla'], 'w_out': out['w_out'], 'ffn2_norm': out['ffn2_norm'], 'ffn2_w_gu': out['ffn2_w_gu'], 'ffn2_w_down': out['ffn2_w_down'], 'loss_target': out['loss_target'], 'm_meta_tokens': out['m_meta_tokens'], 'm_ffn1_norm': out['m_ffn1_norm'], 'm_ffn1_w_gu': out['m_ffn1_w_gu'], 'm_ffn1_w_down': out['m_ffn1_w_down'], 'm_mix_norm': out['m_mix_norm'], 'm_w_in': out['m_w_in'], 'm_b_forget': out['m_b_forget'], 'm_b_gate': out['m_b_gate'], 'm_fox_q_norm': out['m_fox_q_norm'], 'm_fox_k_norm': out['m_fox_k_norm'], 'm_mla_cq_norm': out['m_mla_cq_norm'], 'm_mla_w_uq': out['m_mla_w_uq'], 'm_mla_ckv_norm': out['m_mla_ckv_norm'], 'm_mla_w_ukv': out['m_mla_w_ukv'], 'm_mla_q_norm': out['m_mla_q_norm'], 'm_mla_k_norm': out['m_mla_k_norm'], 'm_w_branch_fox': out['m_w_branch_fox'], 'm_w_branch_mla': out['m_w_branch_mla'], 'm_w_out': out['m_w_out'], 'm_ffn2_norm': out['m_ffn2_norm'], 'm_ffn2_w_gu': out['m_ffn2_w_gu'], 'm_ffn2_w_down': out['m_ffn2_w_down'], 'v_meta_tokens': out['v_meta_tokens'], 'v_ffn1_norm': out['v_ffn1_norm'], 'v_ffn1_w_gu': out['v_ffn1_w_gu'], 'v_ffn1_w_down': out['v_ffn1_w_down'], 'v_mix_norm': out['v_mix_norm'], 'v_w_in': out['v_w_in'], 'v_b_forget': out['v_b_forget'], 'v_b_gate': out['v_b_gate'], 'v_fox_q_norm': out['v_fox_q_norm'], 'v_fox_k_norm': out['v_fox_k_norm'], 'v_mla_cq_norm': out['v_mla_cq_norm'], 'v_mla_w_uq': out['v_mla_w_uq'], 'v_mla_ckv_norm': out['v_mla_ckv_norm'], 'v_mla_w_ukv': out['v_mla_w_ukv'], 'v_mla_q_norm': out['v_mla_q_norm'], 'v_mla_k_norm': out['v_mla_k_norm'], 'v_w_branch_fox': out['v_w_branch_fox'], 'v_w_branch_mla': out['v_w_branch_mla'], 'v_w_out': out['v_w_out'], 'v_ffn2_norm': out['v_ffn2_norm'], 'v_ffn2_w_gu': out['v_ffn2_w_gu'], 'v_ffn2_w_down': out['v_ffn2_w_down']}


def _loss(weights, diff, rest, loss_target):
    with _jax.named_scope("forward"):
        args = {**rest, TWIN_DIFF_INPUT: diff, **{k: w.astype(_WEIGHT_DTYPES[k]) for k, w in weights.items()}}
        y = _forward(args)
    with _jax.named_scope("loss_head"):
        err = _jnp.square(y.astype(_jnp.float32) - loss_target)
        return 0.5 * _jnp.sum(_jnp.mean(err, axis=-1)) if err.ndim else 0.5 * err


def _adamw(w, g, m, v):
    m = ADAM_B1 * m + (1.0 - ADAM_B1) * g
    v = ADAM_B2 * v + (1.0 - ADAM_B2) * _jnp.square(g)
    m_hat = m / (1.0 - ADAM_B1 ** ADAM_STEP)
    v_hat = v / (1.0 - ADAM_B2 ** ADAM_STEP)
    delta = -ADAM_LR * (m_hat / (_jnp.sqrt(v_hat) + ADAM_EPS) + ADAM_WD * w)
    return delta, m, v


def reference(x, meta_tokens, ffn1_norm, ffn1_w_gu, ffn1_w_down, mix_norm, w_in, b_forget, b_gate, fox_q_norm, fox_k_norm, mla_cq_norm, mla_w_uq, mla_ckv_norm, mla_w_ukv, mla_q_norm, mla_k_norm, w_branch_fox, w_branch_mla, w_out, ffn2_norm, ffn2_w_gu, ffn2_w_down, loss_target, m_meta_tokens, m_ffn1_norm, m_ffn1_w_gu, m_ffn1_w_down, m_mix_norm, m_w_in, m_b_forget, m_b_gate, m_fox_q_norm, m_fox_k_norm, m_mla_cq_norm, m_mla_w_uq, m_mla_ckv_norm, m_mla_w_ukv, m_mla_q_norm, m_mla_k_norm, m_w_branch_fox, m_w_branch_mla, m_w_out, m_ffn2_norm, m_ffn2_w_gu, m_ffn2_w_down, v_meta_tokens, v_ffn1_norm, v_ffn1_w_gu, v_ffn1_w_down, v_mix_norm, v_w_in, v_b_forget, v_b_gate, v_fox_q_norm, v_fox_k_norm, v_mla_cq_norm, v_mla_w_uq, v_mla_ckv_norm, v_mla_w_ukv, v_mla_q_norm, v_mla_k_norm, v_w_branch_fox, v_w_branch_mla, v_w_out, v_ffn2_norm, v_ffn2_w_gu, v_ffn2_w_down):
    given = dict(x=x, meta_tokens=meta_tokens, ffn1_norm=ffn1_norm, ffn1_w_gu=ffn1_w_gu, ffn1_w_down=ffn1_w_down, mix_norm=mix_norm, w_in=w_in, b_forget=b_forget, b_gate=b_gate, fox_q_norm=fox_q_norm, fox_k_norm=fox_k_norm, mla_cq_norm=mla_cq_norm, mla_w_uq=mla_w_uq, mla_ckv_norm=mla_ckv_norm, mla_w_ukv=mla_w_ukv, mla_q_norm=mla_q_norm, mla_k_norm=mla_k_norm, w_branch_fox=w_branch_fox, w_branch_mla=w_branch_mla, w_out=w_out, ffn2_norm=ffn2_norm, ffn2_w_gu=ffn2_w_gu, ffn2_w_down=ffn2_w_down, loss_target=loss_target, m_meta_tokens=m_meta_tokens, m_ffn1_norm=m_ffn1_norm, m_ffn1_w_gu=m_ffn1_w_gu, m_ffn1_w_down=m_ffn1_w_down, m_mix_norm=m_mix_norm, m_w_in=m_w_in, m_b_forget=m_b_forget, m_b_gate=m_b_gate, m_fox_q_norm=m_fox_q_norm, m_fox_k_norm=m_fox_k_norm, m_mla_cq_norm=m_mla_cq_norm, m_mla_w_uq=m_mla_w_uq, m_mla_ckv_norm=m_mla_ckv_norm, m_mla_w_ukv=m_mla_w_ukv, m_mla_q_norm=m_mla_q_norm, m_mla_k_norm=m_mla_k_norm, m_w_branch_fox=m_w_branch_fox, m_w_branch_mla=m_w_branch_mla, m_w_out=m_w_out, m_ffn2_norm=m_ffn2_norm, m_ffn2_w_gu=m_ffn2_w_gu, m_ffn2_w_down=m_ffn2_w_down, v_meta_tokens=v_meta_tokens, v_ffn1_norm=v_ffn1_norm, v_ffn1_w_gu=v_ffn1_w_gu, v_ffn1_w_down=v_ffn1_w_down, v_mix_norm=v_mix_norm, v_w_in=v_w_in, v_b_forget=v_b_forget, v_b_gate=v_b_gate, v_fox_q_norm=v_fox_q_norm, v_fox_k_norm=v_fox_k_norm, v_mla_cq_norm=v_mla_cq_norm, v_mla_w_uq=v_mla_w_uq, v_mla_ckv_norm=v_mla_ckv_norm, v_mla_w_ukv=v_mla_w_ukv, v_mla_q_norm=v_mla_q_norm, v_mla_k_norm=v_mla_k_norm, v_w_branch_fox=v_w_branch_fox, v_w_branch_mla=v_w_branch_mla, v_w_out=v_w_out, v_ffn2_norm=v_ffn2_norm, v_ffn2_w_gu=v_ffn2_w_gu, v_ffn2_w_down=v_ffn2_w_down)
    weights = {n: given[n] for n in TWIN_WEIGHTS}
    shared = {n: given[n] for n in SHARED_INPUTS}
    per_example = {n: given[n] for n in ['x']}
    grad_fn = _jax.value_and_grad(_loss, argnums=(0, 1))

    def one_microbatch(ex, loss_target):
        ex = dict(ex)
        diff = ex.pop(TWIN_DIFF_INPUT)
        return grad_fn(weights, diff, {**shared, **ex}, loss_target)

    if N_MICROBATCH == 1:
        loss, (grad_w, grad_x) = one_microbatch(per_example, given["loss_target"])
    else:
        def body(carry, xs):
            loss_sum, grad_sum = carry
            l_k, (gw_k, gx_k) = one_microbatch(xs[0], xs[1])
            with _jax.named_scope("update"):
                return (loss_sum + l_k, _jax.tree.map(_jnp.add, grad_sum, gw_k)), gx_k

        init = (_jnp.zeros((), _jnp.float32), _jax.tree.map(_jnp.zeros_like, weights))
        (loss, grad_w), grad_x = _jax.lax.scan(body, init, (per_example, given["loss_target"]))
    with _jax.named_scope("update"):
        delta_w, new_m, new_v = {}, {}, {}
        for n in TWIN_WEIGHTS:
            delta_w[n], new_m[n], new_v[n] = _adamw(weights[n], grad_w[n], given["m_" + n], given["v_" + n])
    return (loss, grad_x, *[grad_w[n] for n in TWIN_WEIGHTS], *[delta_w[n] for n in TWIN_WEIGHTS],
            *[new_m[n] for n in TWIN_WEIGHTS], *[new_v[n] for n in TWIN_WEIGHTS])
```

```python
import functools

import numpy as np
import jax
import jax.numpy as jnp
from jax import lax
from jax.experimental import pallas as pl
from jax.experimental.pallas import tpu as pltpu

F32 = jnp.float32
BF16 = jnp.bfloat16

N_META = 16
EPS = 1e-6
HEADS = 8
FOX_HD = 64
FOX_W = HEADS * FOX_HD
MLA_Q_RANK = 256
MLA_KV_RANK = 128
MLA_NOPE = 64
MLA_ROPE = 32
MLA_QK = MLA_NOPE + MLA_ROPE
MLA_V = 64
ROPE_THETA = 10000.0
LANES = 128
META_BLK = 128
NEG = -1e30

ADAM_LR = 0.001
ADAM_B1 = 0.9
ADAM_B2 = 0.999
ADAM_EPS = 1e-08
ADAM_WD = 0.01
ADAM_STEP = 10

N_DEV = 8
AXES = ("x", "y", "c")
VMEM_LIMIT_BYTES = 56 * 1024 * 1024


def _tile(n, cap, mult):
    best = None
    for d in range(mult, min(n, cap) + 1, mult):
        if n % d == 0:
            best = d
    return n if best is None else best


def _params(sem=None):
    return pltpu.CompilerParams(dimension_semantics=sem, vmem_limit_bytes=VMEM_LIMIT_BYTES)


def _mm(a, b, mode, name, out_dtype=F32, scale=1.0, res=None):
    if mode == "nn":
        (M, K), (K2, N) = a.shape, b.shape
    elif mode == "nt":
        (M, K), (N, K2) = a.shape, b.shape
    else:
        (K, M), (K2, N) = a.shape, b.shape
    assert K == K2, (a.shape, b.shape, mode)
    if mode == "tn":
        tm, tk = _tile(M, 1408, 128), _tile(K, 640, 16)
    else:
        tm, tk = _tile(M, 640, 16), _tile(K, 1408, 128)
    tn = _tile(N, 1408, 128)
    nk = K // tk
    a_spec = {"nn": pl.BlockSpec((tm, tk), lambda i, j, k: (i, k)),
              "nt": pl.BlockSpec((tm, tk), lambda i, j, k: (i, k)),
              "tn": pl.BlockSpec((tk, tm), lambda i, j, k: (k, i))}[mode]
    b_spec = {"nn": pl.BlockSpec((tk, tn), lambda i, j, k: (k, j)),
              "nt": pl.BlockSpec((tn, tk), lambda i, j, k: (j, k)),
              "tn": pl.BlockSpec((tk, tn), lambda i, j, k: (k, j))}[mode]
    dims = {"nn": (((1,), (0,)), ((), ())), "nt": (((1,), (1,)), ((), ())),
            "tn": (((0,), (0,)), ((), ()))}[mode]
    o_spec = pl.BlockSpec((tm, tn), lambda i, j, k: (i, j))
    has_res = res is not None

    def body(*refs):
        if has_res:
            a_ref, b_ref, r_ref, o_ref, acc_ref = refs
        else:
            a_ref, b_ref, o_ref, acc_ref = refs
        k = pl.program_id(2)

        @pl.when(k == 0)
        def _():
            acc_ref[...] = jnp.zeros_like(acc_ref)

        acc_ref[...] += lax.dot_general(a_ref[...].astype(BF16), b_ref[...].astype(BF16), dims,
                                        preferred_element_type=F32)

        @pl.when(k == nk - 1)
        def _():
            o = acc_ref[...] * scale
            if has_res:
                o = o + r_ref[...]
            o_ref[...] = o.astype(out_dtype)

    ins = [a, b] + ([res] if has_res else [])
    specs = [a_spec, b_spec] + ([o_spec] if has_res else [])
    return pl.pallas_call(
        body, name=name, grid=(M // tm, N // tn, nk), in_specs=specs, out_specs=o_spec,
        out_shape=jax.ShapeDtypeStruct((M, N), out_dtype),
        scratch_shapes=[pltpu.VMEM((tm, tn), F32)],
        compiler_params=_params(("parallel", "parallel", "arbitrary")),
    )(*ins)


def _rope_fwd(y, c, s1, s2):
    return y * c + pltpu.roll(y, LANES - 16, 1) * s1 + pltpu.roll(y, 16, 1) * s2


def _rope_bwd(dy, c, s1, s2):
    return dy * c + pltpu.roll(dy * s1, 16, 1) + pltpu.roll(dy * s2, LANES - 16, 1)


def _rms_fwd(x, gain, d_true, name, rope=None, seq_rows=None, out_dtype=BF16):
    N, W = x.shape
    if rope is None:
        tr = _tile(N, 640, 16)
        nseq = 1
    else:
        tr = _tile(seq_rows, 1088, 16)
        nseq = seq_rows // tr
    inv_d = 1.0 / d_true

    def body(*refs):
        if rope is None:
            x_ref, g_ref, o_ref = refs
        else:
            x_ref, g_ref, c_ref, s1_ref, s2_ref, o_ref = refs
        xv = x_ref[...]
        r = lax.rsqrt(jnp.sum(xv * xv, axis=-1, keepdims=True) * inv_d + EPS)
        y = xv * r * g_ref[...]
        if rope is not None:
            y = _rope_fwd(y, c_ref[...], s1_ref[...], s2_ref[...])
        o_ref[...] = y.astype(out_dtype)

    row = pl.BlockSpec((tr, W), lambda i: (i, 0))
    specs = [row, pl.BlockSpec((1, W), lambda i: (0, 0))]
    ins = [x, gain]
    if rope is not None:
        tab = pl.BlockSpec((tr, W), lambda i: (i % nseq, 0))
        specs += [tab, tab, tab]
        ins += list(rope)
    return pl.pallas_call(
        body, name=name, grid=(N // tr,), in_specs=specs, out_specs=row,
        out_shape=jax.ShapeDtypeStruct((N, W), out_dtype),
        compiler_params=_params(("parallel",)),
    )(*ins)


def _rms_bwd(x, gain, dy, d_true, name, rope=None, seq_rows=None, res=None):
    N, W = x.shape
    if rope is None:
        tr = _tile(N, 640, 16)
        nseq = 1
    else:
        tr = _tile(seq_rows, 1088, 16)
        nseq = seq_rows // tr
    inv_d = 1.0 / d_true
    has_res = res is not None

    def body(*refs):
        refs = list(refs)
        x_ref, g_ref, dy_ref = refs[:3]
        pos = 3
        if rope is not None:
            c_ref, s1_ref, s2_ref = refs[3:6]
            pos = 6
        if has_res:
            r_ref = refs[pos]
            pos += 1
        dx_ref, dg_ref = refs[pos], refs[pos + 1]
        xv = x_ref[...]
        dyv = dy_ref[...].astype(F32)
        if rope is not None:
            dyv = _rope_bwd(dyv, c_ref[...], s1_ref[...], s2_ref[...])
        r = lax.rsqrt(jnp.sum(xv * xv, axis=-1, keepdims=True) * inv_d + EPS)
        gy = dyv * g_ref[...]
        dot = jnp.sum(gy * xv, axis=-1, keepdims=True)
        dx = r * gy - xv * (r * r * r * inv_d) * dot
        if has_res:
            dx = dx + r_ref[...]
        dx_ref[...] = dx

        @pl.when(pl.program_id(0) == 0)
        def _():
            dg_ref[...] = jnp.zeros_like(dg_ref)

        dg_ref[...] += jnp.sum(dyv * xv * r, axis=0, keepdims=True)

    row = pl.BlockSpec((tr, W), lambda i: (i, 0))
    one = pl.BlockSpec((1, W), lambda i: (0, 0))
    specs = [row, one, row]
    ins = [x, gain, dy]
    if rope is not None:
        tab = pl.BlockSpec((tr, W), lambda i: (i % nseq, 0))
        specs += [tab, tab, tab]
        ins += list(rope)
    if has_res:
        specs.append(row)
        ins.append(res)
    return pl.pallas_call(
        body, name=name, grid=(N // tr,), in_specs=specs, out_specs=(row, one),
        out_shape=(jax.ShapeDtypeStruct((N, W), F32), jax.ShapeDtypeStruct((1, W), F32)),
        compiler_params=_params(("arbitrary",)),
    )(*ins)


def _ffn_up(u, w_gu, name):
    T, D = u.shape
    F = w_gu.shape[1] // 2
    tm, tn = _tile(T, 640, 16), _tile(F, 1408, 128)
    nj = F // tn

    def body(u_ref, wg_ref, wu_ref, g_ref, up_ref, a_ref):
        uv = u_ref[...]
        g = jnp.dot(uv, wg_ref[...], preferred_element_type=F32)
        up = jnp.dot(uv, wu_ref[...], preferred_element_type=F32)
        g_ref[...] = g.astype(BF16)
        up_ref[...] = up.astype(BF16)
        a_ref[...] = (g * jax.nn.sigmoid(g) * up).astype(BF16)

    o_spec = pl.BlockSpec((tm, tn), lambda i, j: (i, j))
    sh = jax.ShapeDtypeStruct((T, F), BF16)
    return pl.pallas_call(
        body, name=name, grid=(T // tm, nj),
        in_specs=[pl.BlockSpec((tm, D), lambda i, j: (i, 0)),
                  pl.BlockSpec((D, tn), lambda i, j: (0, j)),
                  pl.BlockSpec((D, tn), lambda i, j: (0, j + nj))],
        out_specs=(o_spec, o_spec, o_spec), out_shape=(sh, sh, sh),
        compiler_params=_params(("parallel", "parallel")),
    )(u, w_gu, w_gu)


def _ffn_down_bwd(dh, w_down, g, up, name):
    T, D = dh.shape
    F = w_down.shape[0]
    tm, tn = _tile(T, 640, 16), _tile(F, 1408, 128)

    def body(dh_ref, w_ref, g_ref, up_ref, dg_ref, dup_ref):
        da = 0.5 * lax.dot_general(dh_ref[...].astype(BF16), w_ref[...], (((1,), (1,)), ((), ())),
                                   preferred_element_type=F32)
        gv = g_ref[...].astype(F32)
        sg = jax.nn.sigmoid(gv)
        silu = gv * sg
        dup_ref[...] = (da * silu).astype(BF16)
        dg_ref[...] = (da * up_ref[...].astype(F32) * (sg + silu * (1.0 - sg))).astype(BF16)

    t_spec = pl.BlockSpec((tm, tn), lambda i, j: (i, j))
    sh = jax.ShapeDtypeStruct((T, F), BF16)
    return pl.pallas_call(
        body, name=name, grid=(T // tm, F // tn),
        in_specs=[pl.BlockSpec((tm, D), lambda i, j: (i, 0)),
                  pl.BlockSpec((tn, D), lambda i, j: (j, 0)), t_spec, t_spec],
        out_specs=(t_spec, t_spec), out_shape=(sh, sh),
        compiler_params=_params(("parallel", "parallel")),
    )(dh, w_down, g, up)


def _ffn_up_bwd_dx(dg, dup, w_gu, name):
    T, F = dg.shape
    D = w_gu.shape[0]
    tm, tk = _tile(T, 640, 16), _tile(F, 1408, 128)
    nk = F // tk
    nt = (((1,), (1,)), ((), ()))

    def body(dg_ref, dup_ref, wg_ref, wu_ref, o_ref, acc_ref):
        k = pl.program_id(1)

        @pl.when(k == 0)
        def _():
            acc_ref[...] = jnp.zeros_like(acc_ref)

        acc_ref[...] += (lax.dot_general(dg_ref[...], wg_ref[...], nt, preferred_element_type=F32)
                         + lax.dot_general(dup_ref[...], wu_ref[...], nt, preferred_element_type=F32))

        @pl.when(k == nk - 1)
        def _():
            o_ref[...] = acc_ref[...]

    return pl.pallas_call(
        body, name=name, grid=(T // tm, nk),
        in_specs=[pl.BlockSpec((tm, tk), lambda i, k: (i, k)),
                  pl.BlockSpec((tm, tk), lambda i, k: (i, k)),
                  pl.BlockSpec((D, tk), lambda i, k: (0, k)),
                  pl.BlockSpec((D, tk), lambda i, k: (0, k + nk))],
        out_specs=pl.BlockSpec((tm, D), lambda i, k: (i, 0)),
        out_shape=jax.ShapeDtypeStruct((T, D), F32),
        scratch_shapes=[pltpu.VMEM((tm, D), F32)],
        compiler_params=_params(("parallel", "arbitrary")),
    )(dg, dup, w_gu, w_gu)


def _logsig(x):
    return jnp.minimum(x, 0.0) - jnp.log(1.0 + jnp.exp(-jnp.abs(x)))


def _cum_fwd(fl, flm, bf, name):
    B, S, _ = fl.shape
    nb = S // LANES

    def body(fl_ref, flm_ref, bf_ref, cum_ref, cumm_ref):
        rows = lax.broadcasted_iota(jnp.int32, (LANES, LANES), 0)
        cols = lax.broadcasted_iota(jnp.int32, (LANES, LANES), 1)
        tri = (rows >= cols).astype(F32)
        bias = bf_ref[...]
        lfm = jnp.where(rows < N_META, _logsig(flm_ref[...] + bias), 0.0)
        cm = jnp.dot(tri, lfm, precision=lax.Precision.HIGHEST, preferred_element_type=F32)
        cumm_ref[...] = cm
        base = cm[LANES - 1:LANES, :]
        for b in range(B):
            def blk(i, carry):
                r0 = pl.multiple_of(i * LANES, LANES)
                lf = _logsig(fl_ref[b, pl.ds(r0, LANES), :] + bias)
                c = jnp.dot(tri, lf, precision=lax.Precision.HIGHEST,
                            preferred_element_type=F32) + carry
                cum_ref[b, pl.ds(r0, LANES), :] = c
                return c[LANES - 1:LANES, :]

            lax.fori_loop(0, nb, blk, base)

    return pl.pallas_call(
        body, name=name,
        out_shape=(jax.ShapeDtypeStruct((B, S, LANES), F32),
                   jax.ShapeDtypeStruct((LANES, LANES), F32)),
        compiler_params=_params(),
    )(fl, flm, bf)


def _cum_bwd(dc, dcm, fl, flm, bf, name):
    B, S, _ = fl.shape
    nb = S // LANES

    def body(dc_ref, dcm_ref, fl_ref, flm_ref, bf_ref, dfl_ref, dflm_ref, dbf_ref):
        rows = lax.broadcasted_iota(jnp.int32, (LANES, LANES), 0)
        cols = lax.broadcasted_iota(jnp.int32, (LANES, LANES), 1)
        triu = (rows <= cols).astype(F32)
        bias = bf_ref[...]
        total = jnp.zeros((1, LANES), F32)
        dbf = jnp.zeros((1, LANES), F32)
        for b in range(B):
            tail = jnp.zeros((1, LANES), F32)
            for t in range(nb):
                r0 = (nb - 1 - t) * LANES
                rc = jnp.dot(triu, dc_ref[b, r0:r0 + LANES, :], precision=lax.Precision.HIGHEST,
                             preferred_element_type=F32) + tail
                xv = fl_ref[b, r0:r0 + LANES, :] + bias
                d = rc / (1.0 + jnp.exp(xv))
                dfl_ref[b, r0:r0 + LANES, :] = d
                tail = rc[0:1, :]
                dbf = dbf + jnp.sum(d, axis=0, keepdims=True)
            total = total + tail
        rcm = jnp.dot(triu, dcm_ref[...], precision=lax.Precision.HIGHEST,
                      preferred_element_type=F32) + total
        dm = jnp.where(rows < N_META, rcm / (1.0 + jnp.exp(flm_ref[...] + bias)), 0.0)
        dflm_ref[...] = dm
        dbf_ref[...] = dbf + jnp.sum(dm, axis=0, keepdims=True)

    return pl.pallas_call(
        body, name=name,
        out_shape=(jax.ShapeDtypeStruct((B, S, LANES), F32),
                   jax.ShapeDtypeStruct((LANES, LANES), F32),
                   jax.ShapeDtypeStruct((1, LANES), F32)),
        compiler_params=_params(),
    )(dc, dcm, fl, flm, bf)


_NT = (((1,), (1,)), ((), ()))


def _attn_fwd(q, k, v, scale, name, cq=None, ck=None, cmk=None):
    B, H, S, _ = q.shape
    SK = k.shape[2]
    TQ = min(512, S)
    TK = TQ
    forget = cq is not None

    def body(*refs):
        if forget:
            q_ref, k_ref, v_ref, cq_ref, ck_ref, cmk_ref, o_ref, lse_ref = refs
        else:
            q_ref, k_ref, v_ref, o_ref, lse_ref = refs
        km = k_ref[0, 0, S:S + META_BLK, :]
        vm = v_ref[0, 0, S:S + META_BLK, :]
        mcol = lax.broadcasted_iota(jnp.int32, (TQ, META_BLK), 1)
        rr = lax.broadcasted_iota(jnp.int32, (TQ, TK), 0)
        cc = lax.broadcasted_iota(jnp.int32, (TQ, TK), 1)
        for qi in range(S // TQ):
            q0 = qi * TQ
            qt = q_ref[0, 0, q0:q0 + TQ, :]
            s = lax.dot_general(qt, km, _NT, preferred_element_type=F32) * scale
            if forget:
                cqt = cq_ref[0, 0, q0:q0 + TQ, :]
                s = s + cqt - cmk_ref[0, 0]
            s = jnp.where(mcol < N_META, s, NEG)
            m = jnp.max(s, axis=1, keepdims=True)
            p = jnp.exp(s - m)
            l = jnp.sum(p, axis=1, keepdims=True)
            acc = jnp.dot(p.astype(BF16), vm, preferred_element_type=F32)

            def kblock(kj, carry):
                m, l, acc = carry
                k0 = pl.multiple_of(kj * TK, TK)
                kt = k_ref[0, 0, pl.ds(k0, TK), :]
                vt = v_ref[0, 0, pl.ds(k0, TK), :]
                s = lax.dot_general(qt, kt, _NT, preferred_element_type=F32) * scale
                if forget:
                    s = s + cqt - ck_ref[0, 0, kj]
                s = jnp.where(rr + q0 >= cc + k0, s, NEG)
                m2 = jnp.maximum(m, jnp.max(s, axis=1, keepdims=True))
                alpha = jnp.exp(m - m2)
                p = jnp.exp(s - m2)
                l2 = alpha * l + jnp.sum(p, axis=1, keepdims=True)
                acc2 = alpha * acc + jnp.dot(p.astype(BF16), vt, preferred_element_type=F32)
                return m2, l2, acc2

            m, l, acc = lax.fori_loop(0, (q0 + TQ) // TK, kblock, (m, l, acc))
            o_ref[0, 0, q0:q0 + TQ, :] = acc / l
            lse_ref[0, 0, q0:q0 + TQ, :] = m + jnp.log(l)

    qspec = pl.BlockSpec((1, 1, S, LANES), lambda b, h: (b, h, 0, 0))
    kspec = pl.BlockSpec((1, 1, SK, LANES), lambda b, h: (b, h, 0, 0))
    colspec = pl.BlockSpec((1, 1, S, 1), lambda b, h: (b, h, 0, 0))
    specs = [qspec, kspec, kspec]
    ins = [q, k, v]
    if forget:
        specs += [colspec,
                  pl.BlockSpec((1, 1, S // TK, 1, TK), lambda b, h: (b, h, 0, 0, 0)),
                  pl.BlockSpec((1, 1, 1, META_BLK), lambda b, h: (b, h, 0, 0))]
        ins += [cq, ck, cmk]
    return pl.pallas_call(
        body, name=name, grid=(B, H), in_specs=specs, out_specs=(qspec, colspec),
        out_shape=(jax.ShapeDtypeStruct((B, H, S, LANES), F32),
                   jax.ShapeDtypeStruct((B, H, S, 1), F32)),
        compiler_params=_params(("parallel", "parallel")),
    )(*ins)


def _attn_bwd(q, k, v, o, lse, do, scale, name, cq=None, ck=None, cmk=None):
    B, H, S, _ = q.shape
    SK = k.shape[2]
    TQ = min(512, S)
    TK = TQ
    forget = cq is not None

    def body(*refs):
        if forget:
            (q_ref, k_ref, v_ref, o_ref, lse_ref, do_ref, cq_ref, ck_ref, cmk_ref,
             dq_ref, dk_ref, dv_ref, dck_ref, dcm_ref) = refs
        else:
            q_ref, k_ref, v_ref, o_ref, lse_ref, do_ref, dq_ref, dk_ref, dv_ref = refs
        dk_ref[...] = jnp.zeros_like(dk_ref)
        dv_ref[...] = jnp.zeros_like(dv_ref)
        if forget:
            dck_ref[...] = jnp.zeros_like(dck_ref)
            dcm_ref[...] = jnp.zeros_like(dcm_ref)
        km = k_ref[0, 0, S:S + META_BLK, :]
        vm = v_ref[0, 0, S:S + META_BLK, :]
        mcol = lax.broadcasted_iota(jnp.int32, (TQ, META_BLK), 1)
        rr = lax.broadcasted_iota(jnp.int32, (TQ, TK), 0)
        cc = lax.broadcasted_iota(jnp.int32, (TQ, TK), 1)
        for qi in range(S // TQ):
            q0 = qi * TQ
            qt = q_ref[0, 0, q0:q0 + TQ, :]
            dof = do_ref[0, 0, q0:q0 + TQ, :]
            dot = dof.astype(BF16)
            delta = jnp.sum(dof * o_ref[0, 0, q0:q0 + TQ, :], axis=1, keepdims=True)
            lse_t = lse_ref[0, 0, q0:q0 + TQ, :]
            s = lax.dot_general(qt, km, _NT, preferred_element_type=F32) * scale
            if forget:
                cqt = cq_ref[0, 0, q0:q0 + TQ, :]
                s = s + cqt - cmk_ref[0, 0]
            p = jnp.where(mcol < N_META, jnp.exp(s - lse_t), 0.0)
            dp = lax.dot_general(dot, vm, _NT, preferred_element_type=F32)
            ds = p * (dp - delta)
            dsb = ds.astype(BF16)
            dq = jnp.dot(dsb, km, preferred_element_type=F32)
            dk_ref[0, 0, S:S + META_BLK, :] += jnp.dot(ds.T.astype(BF16), qt,
                                                       preferred_element_type=F32) * scale
            dv_ref[0, 0, S:S + META_BLK, :] += jnp.dot(p.T.astype(BF16), dot,
                                                       preferred_element_type=F32)
            if forget:
                dcm_ref[0, 0] += -jnp.sum(ds, axis=0, keepdims=True)

            def kblock(kj, dq):
                k0 = pl.multiple_of(kj * TK, TK)
                kt = k_ref[0, 0, pl.ds(k0, TK), :]
                vt = v_ref[0, 0, pl.ds(k0, TK), :]
                s = lax.dot_general(qt, kt, _NT, preferred_element_type=F32) * scale
                if forget:
                    s = s + cqt - ck_ref[0, 0, kj]
                p = jnp.where(rr + q0 >= cc + k0, jnp.exp(s - lse_t), 0.0)
                dp = lax.dot_general(dot, vt, _NT, preferred_element_type=F32)
                ds = p * (dp - delta)
                dk_ref[0, 0, pl.ds(k0, TK), :] += jnp.dot(ds.T.astype(BF16), qt,
                                                          preferred_element_type=F32) * scale
                dv_ref[0, 0, pl.ds(k0, TK), :] += jnp.dot(p.T.astype(BF16), dot,
                                                          preferred_element_type=F32)
                if forget:
                    dck_ref[0, 0, kj] += -jnp.sum(ds, axis=0, keepdims=True)
                return dq + jnp.dot(ds.astype(BF16), kt, preferred_element_type=F32)

            dq = lax.fori_loop(0, (q0 + TQ) // TK, kblock, dq)
            dq_ref[0, 0, q0:q0 + TQ, :] = dq * scale

    qspec = pl.BlockSpec((1, 1, S, LANES), lambda b, h: (b, h, 0, 0))
    kspec = pl.BlockSpec((1, 1, SK, LANES), lambda b, h: (b, h, 0, 0))
    colspec = pl.BlockSpec((1, 1, S, 1), lambda b, h: (b, h, 0, 0))
    specs = [qspec, kspec, kspec, qspec, colspec, qspec]
    ins = [q, k, v, o, lse, do]
    out_specs = [qspec, kspec, kspec]
    out_shape = [jax.ShapeDtypeStruct((B, H, S, LANES), F32),
                 jax.ShapeDtypeStruct((B, H, SK, LANES), F32),
                 jax.ShapeDtypeStruct((B, H, SK, LANES), F32)]
    if forget:
        ckspec = pl.BlockSpec((1, 1, S // TK, 1, TK), lambda b, h: (b, h, 0, 0, 0))
        cmspec = pl.BlockSpec((1, 1, 1, META_BLK), lambda b, h: (b, h, 0, 0))
        specs += [colspec, ckspec, cmspec]
        ins += [cq, ck, cmk]
        out_specs += [ckspec, cmspec]
        out_shape += [jax.ShapeDtypeStruct((B, H, S // TK, 1, TK), F32),
                      jax.ShapeDtypeStruct((B, H, 1, META_BLK), F32)]
    return pl.pallas_call(
        body, name=name, grid=(B, H), in_specs=specs, out_specs=tuple(out_specs),
        out_shape=tuple(out_shape),
        compiler_params=_params(("parallel", "parallel")),
    )(*ins)


def _gate_fwd(z, bg, of, om, name):
    T, D = of.shape
    tm = _tile(T, 640, 16)

    def body(z_ref, bg_ref, of_ref, om_ref, o_ref):
        gt = jax.nn.sigmoid(z_ref[...] + bg_ref[...])
        o_ref[...] = (gt[:, :D] * of_ref[...] + gt[:, D:] * om_ref[...]).astype(BF16)

    row = pl.BlockSpec((tm, D), lambda i: (i, 0))
    return pl.pallas_call(
        body, name=name, grid=(T // tm,),
        in_specs=[pl.BlockSpec((tm, 2 * D), lambda i: (i, 0)),
                  pl.BlockSpec((1, 2 * D), lambda i: (0, 0)), row, row],
        out_specs=row, out_shape=jax.ShapeDtypeStruct((T, D), BF16),
        compiler_params=_params(("parallel",)),
    )(z, bg, of, om)


def _gate_bwd(dmix, z, bg, of, om, name):
    T, D = of.shape
    tm = _tile(T, 640, 16)

    def body(dm_ref, z_ref, bg_ref, of_ref, om_ref, dgl_ref, dof_ref, dom_ref, dbg_ref):
        gt = jax.nn.sigmoid(z_ref[...] + bg_ref[...])
        dm = dm_ref[...]
        dof_ref[...] = (dm * gt[:, :D]).astype(BF16)
        dom_ref[...] = (dm * gt[:, D:]).astype(BF16)
        dgl = jnp.concatenate([dm * of_ref[...], dm * om_ref[...]], axis=1) * gt * (1.0 - gt)
        dgl_ref[...] = dgl

        @pl.when(pl.program_id(0) == 0)
        def _():
            dbg_ref[...] = jnp.zeros_like(dbg_ref)

        dbg_ref[...] += jnp.sum(dgl, axis=0, keepdims=True)

    row = pl.BlockSpec((tm, D), lambda i: (i, 0))
    wide = pl.BlockSpec((tm, 2 * D), lambda i: (i, 0))
    one = pl.BlockSpec((1, 2 * D), lambda i: (0, 0))
    return pl.pallas_call(
        body, name=name, grid=(T // tm,),
        in_specs=[row, wide, one, row, row], out_specs=(wide, row, row, one),
        out_shape=(jax.ShapeDtypeStruct((T, 2 * D), F32), jax.ShapeDtypeStruct((T, D), BF16),
                   jax.ShapeDtypeStruct((T, D), BF16), jax.ShapeDtypeStruct((1, 2 * D), F32)),
        compiler_params=_params(("arbitrary",)),
    )(dmix, z, bg, of, om)


def _loss(h, tgt, n_valid, name):
    T, D = h.shape
    tm = _tile(T, 640, 16)

    def body(h_ref, t_ref, dh_ref, l_ref):
        i = pl.program_id(0)
        rows = lax.broadcasted_iota(jnp.int32, (tm, D), 0) + i * tm
        err = jnp.where(rows < n_valid, h_ref[...] - t_ref[...], 0.0)
        dh_ref[...] = err * (1.0 / D)

        @pl.when(i == 0)
        def _():
            l_ref[...] = jnp.zeros_like(l_ref)

        l_ref[...] += 0.5 * jnp.sum(jnp.sum(err * err, axis=1, keepdims=True) * (1.0 / D))

    row = pl.BlockSpec((tm, D), lambda i: (i, 0))
    acc = pl.BlockSpec((8, LANES), lambda i: (0, 0))
    return pl.pallas_call(
        body, name=name, grid=(T // tm,), in_specs=[row, row], out_specs=(row, acc),
        out_shape=(jax.ShapeDtypeStruct((T, D), F32), jax.ShapeDtypeStruct((8, LANES), F32)),
        compiler_params=_params(("arbitrary",)),
    )(h, tgt)


def _adamw(parts, w, m, v, name):
    P, R, C = parts.shape
    tr = _tile(R, 256, 8)
    bc1 = 1.0 - ADAM_B1 ** ADAM_STEP
    bc2 = 1.0 - ADAM_B2 ** ADAM_STEP

    def body(p_ref, w_ref, m_ref, v_ref, g_ref, d_ref, m2_ref, v2_ref):
        g = p_ref[0].astype(F32)
        for j in range(1, P):
            g = g + p_ref[j].astype(F32)
        m2 = ADAM_B1 * m_ref[...] + (1.0 - ADAM_B1) * g
        v2 = ADAM_B2 * v_ref[...] + (1.0 - ADAM_B2) * (g * g)
        m_hat = m2 / bc1
        v_hat = v2 / bc2
        g_ref[...] = g
        d_ref[...] = -ADAM_LR * (m_hat / (jnp.sqrt(v_hat) + ADAM_EPS) + ADAM_WD * w_ref[...])
        m2_ref[...] = m2
        v2_ref[...] = v2

    row = pl.BlockSpec((tr, C), lambda i: (i, 0))
    sh = jax.ShapeDtypeStruct((R, C), F32)
    return pl.pallas_call(
        body, name=name, grid=(R // tr,),
        in_specs=[pl.BlockSpec((P, tr, C), lambda i: (0, i, 0)), row, row, row],
        out_specs=(row, row, row, row), out_shape=(sh, sh, sh, sh),
        compiler_params=_params(("parallel",)),
    )(parts, w, m, v)


def _peer(d):
    x, y, c = lax.axis_index("x"), lax.axis_index("y"), lax.axis_index("c")
    px = 1 - x if d & 4 else x
    py = 1 - y if d & 2 else y
    pc = 1 - c if d & 1 else c
    return (px, py, pc), 4 * px + 2 * py + pc


def _exchange(src, name, gather):
    R, C = src.shape[-2:]

    def body(src_ref, out_ref, send_sems, recv_sems, local_sem):
        _, me = _peer(0)
        mine = src_ref if gather else src_ref.at[me]
        own = pltpu.make_async_copy(mine, out_ref.at[me], local_sem)
        own.start()
        copies = []
        for d in range(1, N_DEV):
            dev, lin = _peer(d)
            cp = pltpu.make_async_remote_copy(
                src_ref=src_ref if gather else src_ref.at[lin], dst_ref=out_ref.at[me],
                send_sem=send_sems.at[d - 1], recv_sem=recv_sems.at[d - 1],
                device_id=dev, device_id_type=pl.DeviceIdType.MESH)
            cp.start()
            copies.append(cp)
        for d in range(1, N_DEV):
            dev, lin = _peer(d)
            pltpu.make_async_remote_copy(
                src_ref=src_ref if gather else src_ref.at[lin], dst_ref=out_ref.at[lin],
                send_sem=send_sems.at[d - 1], recv_sem=recv_sems.at[d - 1],
                device_id=dev, device_id_type=pl.DeviceIdType.MESH).wait_recv()
        for cp in copies:
            cp.wait_send()
        own.wait()

    return pl.pallas_call(
        body, name=name,
        in_specs=[pl.BlockSpec(memory_space=pl.ANY)], out_specs=pl.BlockSpec(memory_space=pl.ANY),
        out_shape=jax.ShapeDtypeStruct((N_DEV, R, C), src.dtype),
        scratch_shapes=[pltpu.SemaphoreType.DMA((N_DEV - 1,)), pltpu.SemaphoreType.DMA((N_DEV - 1,)),
                        pltpu.SemaphoreType.DMA],
    )(src)


def _pack(arrs, cols, row_mult):
    flat = jnp.concatenate([a.reshape(-1) for a in arrs])
    n = flat.shape[0]
    quantum = cols * row_mult
    total = -(-n // quantum) * quantum
    return jnp.pad(flat, (0, total - n)).reshape(total // cols, cols)


def _pack_rows(arrs, cols, row_mult):
    flat = jnp.concatenate(arrs, axis=1)
    n = flat.shape[1]
    quantum = cols * row_mult
    total = -(-n // quantum) * quantum
    return jnp.pad(flat, ((0, 0), (0, total - n))).reshape(N_DEV, total // cols, cols)


def _unpack(packed, shapes):
    flat = packed.reshape(-1)
    out, off = [], 0
    for s in shapes:
        n = int(np.prod(s))
        out.append(flat[off:off + n].reshape(s))
        off += n
    return out


def _rope_tables(positions):
    inv_freq = ROPE_THETA ** (-jnp.arange(0, MLA_ROPE, 2, dtype=F32) / MLA_ROPE)
    ang = positions.astype(F32)[:, None] * inv_freq[None, :]
    cos, sin = jnp.cos(ang), jnp.sin(ang)
    n = positions.shape[0]
    ones, zeros = jnp.ones((n, MLA_NOPE), F32), jnp.zeros((n, MLA_NOPE), F32)
    tail1, tail0 = jnp.ones((n, LANES - MLA_QK), F32), jnp.zeros((n, LANES - MLA_QK), F32)
    z16 = jnp.zeros((n, 16), F32)
    c = jnp.concatenate([ones, cos, cos, tail1], axis=1)
    s1 = jnp.concatenate([zeros, -sin, z16, tail0], axis=1)
    s2 = jnp.concatenate([zeros, z16, sin, tail0], axis=1)
    return c, s1, s2


def kernel(x, meta_tokens, ffn1_norm, ffn1_w_gu, ffn1_w_down, mix_norm, w_in, b_forget, b_gate, fox_q_norm, fox_k_norm, mla_cq_norm, mla_w_uq, mla_ckv_norm, mla_w_ukv, mla_q_norm, mla_k_norm, w_branch_fox, w_branch_mla, w_out, ffn2_norm, ffn2_w_gu, ffn2_w_down, loss_target, m_meta_tokens, m_ffn1_norm, m_ffn1_w_gu, m_ffn1_w_down, m_mix_norm, m_w_in, m_b_forget, m_b_gate, m_fox_q_norm, m_fox_k_norm, m_mla_cq_norm, m_mla_w_uq, m_mla_ckv_norm, m_mla_w_ukv, m_mla_q_norm, m_mla_k_norm, m_w_branch_fox, m_w_branch_mla, m_w_out, m_ffn2_norm, m_ffn2_w_gu, m_ffn2_w_down, v_meta_tokens, v_ffn1_norm, v_ffn1_w_gu, v_ffn1_w_down, v_mix_norm, v_w_in, v_b_forget, v_b_gate, v_fox_q_norm, v_fox_k_norm, v_mla_cq_norm, v_mla_w_uq, v_mla_ckv_norm, v_mla_w_ukv, v_mla_q_norm, v_mla_k_norm, v_w_branch_fox, v_w_branch_mla, v_w_out, v_ffn2_norm, v_ffn2_w_gu, v_ffn2_w_down):
    names = ["meta_tokens", "ffn1_norm", "ffn1_w_gu", "ffn1_w_down", "mix_norm", "w_in", "b_forget",
             "b_gate", "fox_q_norm", "fox_k_norm", "mla_cq_norm", "mla_w_uq", "mla_ckv_norm",
             "mla_w_ukv", "mla_q_norm", "mla_k_norm", "w_branch_fox", "w_branch_mla", "w_out",
             "ffn2_norm", "ffn2_w_gu", "ffn2_w_down"]
    W = dict(zip(names, [meta_tokens, ffn1_norm, ffn1_w_gu, ffn1_w_down, mix_norm, w_in, b_forget,
                         b_gate, fox_q_norm, fox_k_norm, mla_cq_norm, mla_w_uq, mla_ckv_norm,
                         mla_w_ukv, mla_q_norm, mla_k_norm, w_branch_fox, w_branch_mla, w_out,
                         ffn2_norm, ffn2_w_gu, ffn2_w_down]))
    Mo = dict(zip(names, [m_meta_tokens, m_ffn1_norm, m_ffn1_w_gu, m_ffn1_w_down, m_mix_norm, m_w_in,
                          m_b_forget, m_b_gate, m_fox_q_norm, m_fox_k_norm, m_mla_cq_norm,
                          m_mla_w_uq, m_mla_ckv_norm, m_mla_w_ukv, m_mla_q_norm, m_mla_k_norm,
                          m_w_branch_fox, m_w_branch_mla, m_w_out, m_ffn2_norm, m_ffn2_w_gu,
                          m_ffn2_w_down]))
    Vo = dict(zip(names, [v_meta_tokens, v_ffn1_norm, v_ffn1_w_gu, v_ffn1_w_down, v_mix_norm, v_w_in,
                          v_b_forget, v_b_gate, v_fox_q_norm, v_fox_k_norm, v_mla_cq_norm,
                          v_mla_w_uq, v_mla_ckv_norm, v_mla_w_ukv, v_mla_q_norm, v_mla_k_norm,
                          v_w_branch_fox, v_w_branch_mla, v_w_out, v_ffn2_norm, v_ffn2_w_gu,
                          v_ffn2_w_down]))

    B, S, D = x.shape
    NX = B * S
    T = -(-(NX + N_META) // LANES) * LANES
    SK = S + META_BLK
    H = HEADS
    me = 4 * lax.axis_index("x") + 2 * lax.axis_index("y") + lax.axis_index("c")

    big = [("ffn1_w_gu", 1), ("ffn1_w_down", 0), ("w_in", 1), ("mla_w_uq", 1), ("mla_w_ukv", 1),
           ("w_branch_fox", 1), ("w_branch_mla", 1), ("w_out", 0), ("ffn2_w_gu", 1), ("ffn2_w_down", 0)]
    shard_shapes = [W[n].shape[1:] for n, _ in big]
    wpack = _pack([W[n].astype(BF16) for n, _ in big], 1024, 16)
    gathered = _exchange(wpack, "gather_weights", gather=True)
    gflat = gathered.reshape(N_DEV, -1)
    full = {}
    off = 0
    for (n, ax), s in zip(big, shard_shapes):
        sz = s[0] * s[1]
        blk = gflat[:, off:off + sz].reshape(N_DEV, s[0], s[1])
        off += sz
        full[n] = (blk.transpose(1, 0, 2).reshape(s[0], N_DEV * s[1]) if ax == 1
                   else blk.reshape(N_DEV * s[0], s[1]))
    F = full["ffn1_w_down"].shape[0]

    wi = full["w_in"]
    o_fq, o_fk, o_fv = 0, FOX_W, 2 * FOX_W
    o_f = 3 * FOX_W
    o_cq = o_f + HEADS
    o_ckv = o_cq + MLA_Q_RANK
    o_kr = o_ckv + MLA_KV_RANK
    o_g = o_kr + MLA_ROPE
    w_in_p = jnp.concatenate([
        wi[:, o_g:o_g + 2 * D], wi[:, o_fq:o_f], wi[:, o_cq:o_kr],
        jnp.pad(wi[:, o_f:o_cq], ((0, 0), (0, LANES - HEADS))),
        jnp.pad(wi[:, o_kr:o_g], ((0, 0), (0, LANES - MLA_ROPE)))], axis=1)
    Z_G, Z_FQ = 0, 2 * D
    Z_FK, Z_FV = Z_FQ + FOX_W, Z_FQ + 2 * FOX_W
    Z_CQ = Z_FQ + 3 * FOX_W
    Z_CKV = Z_CQ + MLA_Q_RANK
    Z_F = Z_CKV + MLA_KV_RANK
    Z_KR = Z_F + LANES
    ZW = Z_KR + LANES

    def pad_lanes(a, w=LANES):
        return jnp.pad(a, [(0, 0)] * (a.ndim - 1) + [(0, w - a.shape[-1])])

    def rows_T(real, meta=None):
        n = real.shape[1]
        parts = [real, meta if meta is not None else jnp.zeros((N_META, n), real.dtype),
                 jnp.zeros((T - NX - N_META, n), real.dtype)]
        return jnp.concatenate(parts, axis=0)

    def to_heads(real, meta, dh):
        r = real.reshape(B, S, H, dh).transpose(0, 2, 1, 3)
        if meta is not None:
            mt = jnp.broadcast_to(meta.reshape(N_META, H, dh).transpose(1, 0, 2)[None],
                                  (B, H, N_META, dh))
            r = jnp.concatenate([r, mt, jnp.zeros((B, H, META_BLK - N_META, dh), r.dtype)], axis=2)
        return pad_lanes(r)

    def from_heads(g, dh):
        real = g[:, :, :S, :dh].transpose(0, 2, 1, 3).reshape(NX, H * dh)
        meta = None
        if g.shape[2] > S:
            meta = g[:, :, S:S + N_META, :dh].sum(0).transpose(1, 0, 2).reshape(N_META, H * dh)
        return real, meta

    small_in = _exchange(_pack([meta_tokens], LANES, 8), "gather_meta", gather=True)
    meta_full = small_in.reshape(N_DEV, -1)[:, :N_META * (D // N_DEV)].reshape(
        N_DEV, N_META, D // N_DEV).transpose(1, 0, 2).reshape(N_META, D)
    h0 = rows_T(x.reshape(NX, D), meta_full)
    tgt = rows_T(loss_target.reshape(NX, D))

    def ffn_fwd(h, norm, w_gu, w_down, tag):
        u = _rms_fwd(h, norm, D, tag + "_norm")
        g, up, a = _ffn_up(u, w_gu, tag + "_up")
        h_out = _mm(a, w_down, "nn", tag + "_down", scale=0.5, res=h)
        return h_out, (u, g, up, a)

    h1, ffn1_saved = ffn_fwd(h0, W["ffn1_norm"], full["ffn1_w_gu"], full["ffn1_w_down"], "ffn1")

    u2 = _rms_fwd(h1, W["mix_norm"], D, "mix_norm")
    z = _mm(u2, w_in_p, "nn", "w_in")
    zr, zm = z[:NX], z[NX:NX + N_META]

    gq_f, gk_f = pad_lanes(W["fox_q_norm"]), pad_lanes(W["fox_k_norm"])
    fq_h = to_heads(zr[:, Z_FQ:Z_FQ + FOX_W], None, FOX_HD).reshape(B * H * S, LANES)
    fk_h = to_heads(zr[:, Z_FK:Z_FK + FOX_W], zm[:, Z_FK:Z_FK + FOX_W], FOX_HD).reshape(B * H * SK, LANES)
    fv_h = to_heads(zr[:, Z_FV:Z_FV + FOX_W], zm[:, Z_FV:Z_FV + FOX_W], FOX_HD).astype(BF16)
    fqn = _rms_fwd(fq_h, gq_f, FOX_HD, "fox_q_norm").reshape(B, H, S, LANES)
    fkn = _rms_fwd(fk_h, gk_f, FOX_HD, "fox_k_norm").reshape(B, H, SK, LANES)
    fl = zr[:, Z_F:Z_F + LANES].reshape(B, S, LANES)
    flm = pad_lanes(zm[:, Z_F:Z_F + LANES].T).T
    bf = pad_lanes(W["b_forget"])
    cum, cumm = _cum_fwd(fl, flm, bf, "forget_cum")
    TK = min(512, S)
    cum_h = cum[:, :, :H].transpose(0, 2, 1)
    cq = cum_h[..., None]
    ck = cum_h.reshape(B, H, S // TK, 1, TK)
    cmk = jnp.broadcast_to(cumm[:, :H].T[None, :, None, :], (B, H, 1, META_BLK))
    o_fox, lse_fox = _attn_fwd(fqn, fkn, fv_h, FOX_HD ** -0.5, "fox_attn", cq, ck, cmk)
    of_m = rows_T(o_fox[..., :FOX_HD].transpose(0, 2, 1, 3).reshape(NX, FOX_W)).astype(BF16)
    of = _mm(of_m, full["w_branch_fox"], "nn", "branch_fox")

    pos_q = jnp.arange(S) + N_META
    pos_k = jnp.concatenate([pos_q, jnp.arange(META_BLK)])
    rope_q, rope_k = _rope_tables(pos_q), _rope_tables(pos_k)
    cqn = _rms_fwd(z[:, Z_CQ:Z_CQ + MLA_Q_RANK], W["mla_cq_norm"], MLA_Q_RANK, "mla_cq_norm")
    q_lin = _mm(cqn, full["mla_w_uq"], "nn", "mla_uq")
    ckvn = _rms_fwd(z[:, Z_CKV:Z_CKV + MLA_KV_RANK], W["mla_ckv_norm"], MLA_KV_RANK, "mla_ckv_norm")
    kv_lin = _mm(ckvn, full["mla_w_ukv"], "nn", "mla_ukv")
    mq_h = to_heads(q_lin[:NX], None, MLA_QK).reshape(B * H * S, LANES)

    def split_kv(rows):
        r = rows.reshape(-1, H, MLA_NOPE + MLA_V)
        return r[..., :MLA_NOPE], r[..., MLA_NOPE:]

    kn_r, v_r = split_kv(kv_lin[:NX])
    kn_m, v_m = split_kv(kv_lin[NX:NX + N_META])
    kr_r = jnp.broadcast_to(zr[:, None, Z_KR:Z_KR + MLA_ROPE], (NX, H, MLA_ROPE))
    kr_m = jnp.broadcast_to(zm[:, None, Z_KR:Z_KR + MLA_ROPE], (N_META, H, MLA_ROPE))
    mk_h = to_heads(jnp.concatenate([kn_r, kr_r], axis=-1).reshape(NX, H * MLA_QK),
                    jnp.concatenate([kn_m, kr_m], axis=-1).reshape(N_META, H * MLA_QK),
                    MLA_QK).reshape(B * H * SK, LANES)
    mv_h = to_heads(v_r.reshape(NX, H * MLA_V), v_m.reshape(N_META, H * MLA_V), MLA_V).astype(BF16)
    gq_m, gk_m = pad_lanes(W["mla_q_norm"]), pad_lanes(W["mla_k_norm"])
    mqn = _rms_fwd(mq_h, gq_m, MLA_QK, "mla_q_norm", rope=rope_q, seq_rows=S).reshape(B, H, S, LANES)
    mkn = _rms_fwd(mk_h, gk_m, MLA_QK, "mla_k_norm", rope=rope_k, seq_rows=SK).reshape(B, H, SK, LANES)
    o_mla, lse_mla = _attn_fwd(mqn, mkn, mv_h, MLA_QK ** -0.5, "mla_attn")
    om_m = rows_T(o_mla[..., :MLA_V].transpose(0, 2, 1, 3).reshape(NX, H * MLA_V)).astype(BF16)
    om = _mm(om_m, full["w_branch_mla"], "nn", "branch_mla")

    mix = _gate_fwd(z, W["b_gate"], of, om, "gate_mix")
    h2 = _mm(mix, full["w_out"], "nn", "w_out", res=h1)

    h3, ffn2_saved = ffn_fwd(h2, W["ffn2_norm"], full["ffn2_w_gu"], full["ffn2_w_down"], "ffn2")

    dh3, loss_acc = _loss(h3, tgt, NX, "loss")
    loss = lax.psum(loss_acc[0, 0], AXES)

    G = {}

    def ffn_bwd(dh, h, norm, w_gu, w_down, saved, tag):
        u, g, up, a = saved
        G[tag + "_w_down"] = _mm(a, dh, "tn", tag + "_dw_down", scale=0.5)
        dg, dup = _ffn_down_bwd(dh, w_down, g, up, tag + "_down_bwd")
        G[tag + "_w_gu"] = jnp.concatenate([_mm(u, dg, "tn", tag + "_dw_g"),
                                            _mm(u, dup, "tn", tag + "_dw_u")], axis=1)
        du = _ffn_up_bwd_dx(dg, dup, w_gu, tag + "_up_bwd")
        dh_in, G[tag + "_norm"] = _rms_bwd(h, norm, du, D, tag + "_norm_bwd", res=dh)
        return dh_in

    dh2 = ffn_bwd(dh3, h2, W["ffn2_norm"], full["ffn2_w_gu"], full["ffn2_w_down"], ffn2_saved, "ffn2")

    G["w_out"] = _mm(mix, dh2, "tn", "dw_out")
    dmix = _mm(dh2, full["w_out"], "nt", "w_out_bwd")
    dgl, dof, dom, G["b_gate"] = _gate_bwd(dmix, z, W["b_gate"], of, om, "gate_bwd")

    G["w_branch_fox"] = _mm(of_m, dof, "tn", "dw_branch_fox")
    do_fox = _mm(dof, full["w_branch_fox"], "nt", "branch_fox_bwd")
    do_fox_h = to_heads(do_fox[:NX], None, FOX_HD)
    dq_f, dk_f, dv_f, dck, dcmk = _attn_bwd(fqn, fkn, fv_h, o_fox, lse_fox, do_fox_h, FOX_HD ** -0.5,
                                            "fox_attn_bwd", cq, ck, cmk)
    dfq_h, G["fox_q_norm"] = _rms_bwd(fq_h, gq_f, dq_f.reshape(B * H * S, LANES), FOX_HD, "fox_q_norm_bwd")
    dfk_h, G["fox_k_norm"] = _rms_bwd(fk_h, gk_f, dk_f.reshape(B * H * SK, LANES), FOX_HD, "fox_k_norm_bwd")
    dfq_r, _ = from_heads(dfq_h.reshape(B, H, S, LANES), FOX_HD)
    dfk_r, dfk_m = from_heads(dfk_h.reshape(B, H, SK, LANES), FOX_HD)
    dfv_r, dfv_m = from_heads(dv_f, FOX_HD)
    dc = pad_lanes(dck.reshape(B, H, S).transpose(0, 2, 1))
    dcm = pad_lanes(dcmk.sum(0)[:, 0, :].T)
    dcm = jnp.where(jnp.arange(LANES)[:, None] < N_META, dcm, 0.0)
    dfl, dflm, dbf = _cum_bwd(dc, dcm, fl, flm, bf, "forget_cum_bwd")
    G["b_forget"] = dbf[:, :HEADS]
    lane_mask = (jnp.arange(LANES) < HEADS)[None, :]
    dfl_r = jnp.where(lane_mask, dfl.reshape(NX, LANES), 0.0)
    dfl_m = jnp.where(lane_mask, dflm[:N_META], 0.0)

    G["w_branch_mla"] = _mm(om_m, dom, "tn", "dw_branch_mla")
    do_mla = _mm(dom, full["w_branch_mla"], "nt", "branch_mla_bwd")
    do_mla_h = to_heads(do_mla[:NX], None, MLA_V)
    dq_m, dk_m, dv_m = _attn_bwd(mqn, mkn, mv_h, o_mla, lse_mla, do_mla_h, MLA_QK ** -0.5, "mla_attn_bwd")
    dmq_h, G["mla_q_norm"] = _rms_bwd(mq_h, gq_m, dq_m.reshape(B * H * S, LANES), MLA_QK,
                                      "mla_q_norm_bwd", rope=rope_q, seq_rows=S)
    dmk_h, G["mla_k_norm"] = _rms_bwd(mk_h, gk_m, dk_m.reshape(B * H * SK, LANES), MLA_QK,
                                      "mla_k_norm_bwd", rope=rope_k, seq_rows=SK)
    G["mla_q_norm"] = G["mla_q_norm"][:, :MLA_QK]
    G["mla_k_norm"] = G["mla_k_norm"][:, :MLA_QK]
    G["fox_q_norm"] = G["fox_q_norm"][:, :FOX_HD]
    G["fox_k_norm"] = G["fox_k_norm"][:, :FOX_HD]
    dq_lin_r, _ = from_heads(dmq_h.reshape(B, H, S, LANES), MLA_QK)
    dq_lin = rows_T(dq_lin_r)
    G["mla_w_uq"] = _mm(cqn, dq_lin, "tn", "dw_uq")
    dcqn = _mm(dq_lin, full["mla_w_uq"], "nt", "mla_uq_bwd")
    dcq, G["mla_cq_norm"] = _rms_bwd(z[:, Z_CQ:Z_CQ + MLA_Q_RANK], W["mla_cq_norm"], dcqn, MLA_Q_RANK,
                                     "mla_cq_norm_bwd")
    dmk4 = dmk_h.reshape(B, H, SK, LANES)
    dkn_r, dkn_m = from_heads(dmk4, MLA_NOPE)
    dkr_r = dmk4[:, :, :S, MLA_NOPE:MLA_QK].sum(1).reshape(NX, MLA_ROPE)
    dkr_m = dmk4[:, :, S:S + N_META, MLA_NOPE:MLA_QK].sum((0, 1))
    dv_r, dv_mt = from_heads(dv_m, MLA_V)

    def join_kv(kn, vv):
        n = kn.shape[0]
        return jnp.concatenate([kn.reshape(n, H, MLA_NOPE), vv.reshape(n, H, MLA_V)],
                               axis=-1).reshape(n, H * (MLA_NOPE + MLA_V))

    dkv_lin = rows_T(join_kv(dkn_r, dv_r), join_kv(dkn_m, dv_mt))
    G["mla_w_ukv"] = _mm(ckvn, dkv_lin, "tn", "dw_ukv")
    dckvn = _mm(dkv_lin, full["mla_w_ukv"], "nt", "mla_ukv_bwd")
    dckv, G["mla_ckv_norm"] = _rms_bwd(z[:, Z_CKV:Z_CKV + MLA_KV_RANK], W["mla_ckv_norm"], dckvn,
                                       MLA_KV_RANK, "mla_ckv_norm_bwd")

    dz = jnp.concatenate([
        dgl,
        rows_T(jnp.concatenate([dfq_r, dfk_r, dfv_r], axis=1),
               jnp.concatenate([jnp.zeros((N_META, FOX_W), F32), dfk_m, dfv_m], axis=1)),
        dcq, dckv,
        rows_T(dfl_r, dfl_m),
        rows_T(pad_lanes(dkr_r), pad_lanes(dkr_m))], axis=1).astype(BF16)
    dw_in_p = _mm(u2, dz, "tn", "dw_in")
    G["w_in"] = jnp.concatenate([
        dw_in_p[:, Z_FQ:Z_CQ], dw_in_p[:, Z_F:Z_F + HEADS], dw_in_p[:, Z_CQ:Z_F],
        dw_in_p[:, Z_KR:Z_KR + MLA_ROPE], dw_in_p[:, Z_G:Z_G + 2 * D]], axis=1)
    du2 = _mm(dz, w_in_p, "nt", "w_in_bwd")
    dh1, G["mix_norm"] = _rms_bwd(h1, W["mix_norm"], du2, D, "mix_norm_bwd", res=dh2)

    dh0 = ffn_bwd(dh1, h0, W["ffn1_norm"], full["ffn1_w_gu"], full["ffn1_w_down"], ffn1_saved, "ffn1")
    grad_x = dh0[:NX].reshape(B, S, D)
    G["meta_tokens"] = dh0[NX:NX + N_META]

    per_dest = []
    for (n, ax), s in zip(big, shard_shapes):
        g = G[n]
        if ax == 1:
            per_dest.append(g.reshape(g.shape[0], N_DEV, s[1]).transpose(1, 0, 2).reshape(N_DEV, -1))
        else:
            per_dest.append(g.reshape(N_DEV, -1))
    gpack = _pack_rows(per_dest, 1024, 16).astype(BF16)
    parts = _exchange(gpack, "scatter_grads", gather=False)
    pk = lambda src: _pack([src[n].astype(F32) for n, _ in big], 1024, 16)
    g_b, d_b, m_b, v_b = _adamw(parts, pk(W), pk(Mo), pk(Vo), "adamw_big")
    shapes_b = [W[n].shape for n, _ in big]
    res = {}
    for key, packed in (("g", g_b), ("d", d_b), ("m", m_b), ("v", v_b)):
        for (n, _), arr in zip(big, _unpack(packed, shapes_b)):
            res[key, n] = arr

    small = [n for n in names if n not in dict(big) and n != "meta_tokens"]
    small_shapes = [W[n].shape for n in small]
    spack = _pack([G["meta_tokens"]] + [G[n] for n in small], 1024, 8)
    sparts = _exchange(spack, "gather_small_grads", gather=True)
    sflat = sparts.reshape(N_DEV, -1)
    dsh = D // N_DEV
    meta_part = lax.dynamic_slice(sflat[:, :N_META * D].reshape(N_DEV, N_META, D),
                                  (0, 0, me * dsh), (N_DEV, N_META, dsh)).reshape(N_DEV, -1)
    rep_len = sum(int(np.prod(s)) for s in small_shapes)
    rep_part = sflat[:, N_META * D:N_META * D + rep_len]
    sp = _pack_rows([meta_part, rep_part], LANES, 8)
    pks = lambda src: _pack([src["meta_tokens"]] + [src[n] for n in small], LANES, 8)
    g_s, d_s, m_s, v_s = _adamw(sp, pks(W), pks(Mo), pks(Vo), "adamw_small")
    shapes_s = [W["meta_tokens"].shape] + small_shapes
    for key, packed in (("g", g_s), ("d", d_s), ("m", m_s), ("v", v_s)):
        for n, arr in zip(["meta_tokens"] + small, _unpack(packed, shapes_s)):
            res[key, n] = arr

    outs = [loss, grad_x]
    for key in ("g", "d", "m", "v"):
        outs += [res[key, n] for n in names]
    return tuple(outs)
```

```python
import numpy as np
import jax
import jax.numpy as jnp
from jax import lax
from jax.experimental import pallas as pl
from jax.experimental.pallas import tpu as pltpu

F32 = jnp.float32
BF16 = jnp.bfloat16

N_META = 16
EPS = 1e-6
HEADS = 8
FOX_HD = 64
FOX_W = HEADS * FOX_HD
MLA_Q_RANK = 256
MLA_KV_RANK = 128
MLA_NOPE = 64
MLA_ROPE = 32
MLA_QK = MLA_NOPE + MLA_ROPE
MLA_V = 64
ROPE_THETA = 10000.0
LANES = 128
HALF = LANES // 2
META_BLK = 128
NEG = -1e30

ADAM_LR = 0.001
ADAM_B1 = 0.9
ADAM_B2 = 0.999
ADAM_EPS = 1e-08
ADAM_WD = 0.01
ADAM_STEP = 10

N_DEV = 8
AXES = ("x", "y", "c")
VMEM_LIMIT_BYTES = 56 * 1024 * 1024


def _tile(n, cap, mult):
    best = None
    for d in range(mult, min(n, cap) + 1, mult):
        if n % d == 0:
            best = d
    return n if best is None else best


def _row_tile(rows, width):
    return _tile(rows, max(16, (1 << 19) // width), 16)


def _params(sem=None):
    return pltpu.CompilerParams(dimension_semantics=sem, vmem_limit_bytes=VMEM_LIMIT_BYTES)


def _mm(a, b, mode, name, out_dtype=F32, scale=1.0, res=None):
    if mode == "nn":
        (M, K), (K2, N) = a.shape, b.shape
    elif mode == "nt":
        (M, K), (N, K2) = a.shape, b.shape
    else:
        (K, M), (K2, N) = a.shape, b.shape
    assert K == K2, (a.shape, b.shape, mode)
    if mode == "tn":
        tm, tk = _tile(M, 1408, 128), _tile(K, 640, 16)
    else:
        tm, tk = _tile(M, 640, 16), _tile(K, 1408, 128)
    tn = _tile(N, 1408, 128)
    nk = K // tk
    a_spec = {"nn": pl.BlockSpec((tm, tk), lambda i, j, k: (i, k)),
              "nt": pl.BlockSpec((tm, tk), lambda i, j, k: (i, k)),
              "tn": pl.BlockSpec((tk, tm), lambda i, j, k: (k, i))}[mode]
    b_spec = {"nn": pl.BlockSpec((tk, tn), lambda i, j, k: (k, j)),
              "nt": pl.BlockSpec((tn, tk), lambda i, j, k: (j, k)),
              "tn": pl.BlockSpec((tk, tn), lambda i, j, k: (k, j))}[mode]
    dims = {"nn": (((1,), (0,)), ((), ())), "nt": (((1,), (1,)), ((), ())),
            "tn": (((0,), (0,)), ((), ()))}[mode]
    o_spec = pl.BlockSpec((tm, tn), lambda i, j, k: (i, j))
    has_res = res is not None

    def body(*refs):
        if has_res:
            a_ref, b_ref, r_ref, o_ref, acc_ref = refs
        else:
            a_ref, b_ref, o_ref, acc_ref = refs
        k = pl.program_id(2)

        @pl.when(k == 0)
        def _():
            acc_ref[...] = jnp.zeros_like(acc_ref)

        acc_ref[...] += lax.dot_general(a_ref[...].astype(BF16), b_ref[...].astype(BF16), dims,
                                        preferred_element_type=F32)

        @pl.when(k == nk - 1)
        def _():
            o = acc_ref[...] * scale
            if has_res:
                o = o + r_ref[...]
            o_ref[...] = o.astype(out_dtype)

    ins = [a, b] + ([res] if has_res else [])
    specs = [a_spec, b_spec] + ([o_spec] if has_res else [])
    return pl.pallas_call(
        body, name=name, grid=(M // tm, N // tn, nk), in_specs=specs, out_specs=o_spec,
        out_shape=jax.ShapeDtypeStruct((M, N), out_dtype),
        scratch_shapes=[pltpu.VMEM((tm, tn), F32)],
        compiler_params=_params(("parallel", "parallel", "arbitrary")),
    )(*ins)


def _rope_fwd(y, c, s1, s2):
    return y * c + pltpu.roll(y, LANES - 16, 1) * s1 + pltpu.roll(y, 16, 1) * s2


def _rope_bwd(dy, c, s1, s2):
    return dy * c + pltpu.roll(dy * s1, 16, 1) + pltpu.roll(dy * s2, LANES - 16, 1)


def _group_sum(v, seg):
    if seg == v.shape[-1]:
        return jnp.sum(v, axis=-1, keepdims=True)
    lo = lax.broadcasted_iota(jnp.int32, v.shape, 1) < seg
    s_lo = jnp.sum(jnp.where(lo, v, 0.0), axis=-1, keepdims=True)
    s_hi = jnp.sum(jnp.where(lo, 0.0, v), axis=-1, keepdims=True)
    return jnp.where(lo, s_lo, s_hi)


def _norm_fwd(src, col0, width, bw, gain, seg, d_true, name, tabs=None, out_dtype=BF16):
    T = src.shape[0]
    tr = _row_tile(T, bw)
    inv_d = 1.0 / d_true
    c0 = col0 // bw
    assert col0 % bw == 0 and width % bw == 0

    def body(*refs):
        if tabs is None:
            x_ref, g_ref, o_ref = refs
        else:
            x_ref, g_ref, c_ref, s1_ref, s2_ref, o_ref = refs
        xv = x_ref[...]
        r = lax.rsqrt(_group_sum(xv * xv, seg) * inv_d + EPS)
        y = xv * r * g_ref[...]
        if tabs is not None:
            y = _rope_fwd(y, c_ref[...], s1_ref[...], s2_ref[...])
        o_ref[...] = y.astype(out_dtype)

    specs = [pl.BlockSpec((tr, bw), lambda i, j: (i, c0 + j)), pl.BlockSpec((1, bw), lambda i, j: (0, 0))]
    ins = [src, gain]
    if tabs is not None:
        tab = pl.BlockSpec((tr, LANES), lambda i, j: (i, 0))
        specs += [tab, tab, tab]
        ins += list(tabs)
    return pl.pallas_call(
        body, name=name, grid=(T // tr, width // bw), in_specs=specs,
        out_specs=pl.BlockSpec((tr, bw), lambda i, j: (i, j)),
        out_shape=jax.ShapeDtypeStruct((T, width), out_dtype),
        compiler_params=_params(("parallel", "parallel")),
    )(*ins)


def _norm_bwd_math(xv, gain, dyv, seg, inv_d):
    r = lax.rsqrt(_group_sum(xv * xv, seg) * inv_d + EPS)
    gy = dyv * gain
    dot = _group_sum(gy * xv, seg)
    dx = r * gy - xv * (r * r * r * inv_d) * dot
    return dx, jnp.sum(dyv * xv * r, axis=0, keepdims=True)


def _norm_bwd(src, col0, width, bw, gain, seg, d_true, dy, name, tabs=None, res=None, out_dtype=F32):
    T = src.shape[0]
    tr = _row_tile(T, bw)
    inv_d = 1.0 / d_true
    c0 = col0 // bw
    has_res = res is not None

    def body(*refs):
        refs = list(refs)
        x_ref, g_ref, dy_ref = refs[:3]
        pos = 3
        if tabs is not None:
            c_ref, s1_ref, s2_ref = refs[3:6]
            pos = 6
        if has_res:
            r_ref = refs[pos]
            pos += 1
        dx_ref, dg_ref = refs[pos], refs[pos + 1]
        dyv = dy_ref[...].astype(F32)
        if tabs is not None:
            dyv = _rope_bwd(dyv, c_ref[...], s1_ref[...], s2_ref[...])
        dx, dg = _norm_bwd_math(x_ref[...], g_ref[...], dyv, seg, inv_d)
        if has_res:
            dx = dx + r_ref[...]
        dx_ref[...] = dx.astype(out_dtype)

        @pl.when((pl.program_id(0) == 0) & (pl.program_id(1) == 0))
        def _():
            dg_ref[...] = jnp.zeros_like(dg_ref)

        dg_ref[...] += dg

    blk = pl.BlockSpec((tr, bw), lambda i, j: (i, j))
    one = pl.BlockSpec((1, bw), lambda i, j: (0, 0))
    specs = [pl.BlockSpec((tr, bw), lambda i, j: (i, c0 + j)), one, blk]
    ins = [src, gain, dy]
    if tabs is not None:
        tab = pl.BlockSpec((tr, LANES), lambda i, j: (i, 0))
        specs += [tab, tab, tab]
        ins += list(tabs)
    if has_res:
        specs.append(blk)
        ins.append(res)
    return pl.pallas_call(
        body, name=name, grid=(T // tr, width // bw), in_specs=specs, out_specs=(blk, one),
        out_shape=(jax.ShapeDtypeStruct((T, width), out_dtype), jax.ShapeDtypeStruct((1, bw), F32)),
        compiler_params=_params(("arbitrary", "arbitrary")),
    )(*ins)


def _mla_k_raw(kv, kr):
    lane = lax.broadcasted_iota(jnp.int32, kv.shape, 1)
    return jnp.where(lane < MLA_NOPE, kv, jnp.where(lane < MLA_QK, pltpu.roll(kr, MLA_NOPE, 1), 0.0))


def _mla_k_fwd(kv_lin, z, kr_col, gain, tabs, name):
    T, W = kv_lin.shape
    tr = _row_tile(T, LANES)
    krb = kr_col // LANES
    inv_d = 1.0 / MLA_QK

    def body(kv_ref, kr_ref, g_ref, c_ref, s1_ref, s2_ref, o_ref):
        xv = _mla_k_raw(kv_ref[...], kr_ref[...])
        r = lax.rsqrt(jnp.sum(xv * xv, axis=-1, keepdims=True) * inv_d + EPS)
        o_ref[...] = _rope_fwd(xv * r * g_ref[...], c_ref[...], s1_ref[...], s2_ref[...]).astype(BF16)

    blk = pl.BlockSpec((tr, LANES), lambda i, h: (i, h))
    tab = pl.BlockSpec((tr, LANES), lambda i, h: (i, 0))
    return pl.pallas_call(
        body, name=name, grid=(T // tr, W // LANES),
        in_specs=[blk, pl.BlockSpec((tr, LANES), lambda i, h: (i, krb)),
                  pl.BlockSpec((1, LANES), lambda i, h: (0, 0)), tab, tab, tab],
        out_specs=blk, out_shape=jax.ShapeDtypeStruct((T, W), BF16),
        compiler_params=_params(("parallel", "parallel")),
    )(kv_lin, z, gain, *tabs)


def _mla_k_bwd(kv_lin, z, kr_col, gain, tabs, dk, dvk, name):
    T, W = kv_lin.shape
    tr = _row_tile(T, LANES)
    krb = kr_col // LANES
    inv_d = 1.0 / MLA_QK

    def body(kv_ref, kr_ref, g_ref, c_ref, s1_ref, s2_ref, dk_ref, dvk_ref, dkv_ref, dkr_ref, dg_ref):
        h = pl.program_id(1)
        xv = _mla_k_raw(kv_ref[...], kr_ref[...])
        dyv = _rope_bwd(dk_ref[...], c_ref[...], s1_ref[...], s2_ref[...])
        dx, dg = _norm_bwd_math(xv, g_ref[...], dyv, LANES, inv_d)
        lane = lax.broadcasted_iota(jnp.int32, dx.shape, 1)
        dkv_ref[...] = jnp.where(lane < MLA_NOPE, dx, dvk_ref[...])
        part = pltpu.roll(jnp.where((lane >= MLA_NOPE) & (lane < MLA_QK), dx, 0.0), LANES - MLA_NOPE, 1)

        @pl.when(h == 0)
        def _():
            dkr_ref[...] = jnp.zeros_like(dkr_ref)

        dkr_ref[...] += part

        @pl.when((pl.program_id(0) == 0) & (h == 0))
        def _():
            dg_ref[...] = jnp.zeros_like(dg_ref)

        dg_ref[...] += dg

    blk = pl.BlockSpec((tr, LANES), lambda i, h: (i, h))
    tab = pl.BlockSpec((tr, LANES), lambda i, h: (i, 0))
    one = pl.BlockSpec((1, LANES), lambda i, h: (0, 0))
    return pl.pallas_call(
        body, name=name, grid=(T // tr, W // LANES),
        in_specs=[blk, pl.BlockSpec((tr, LANES), lambda i, h: (i, krb)), one, tab, tab, tab, blk, blk],
        out_specs=(blk, tab, one),
        out_shape=(jax.ShapeDtypeStruct((T, W), F32), jax.ShapeDtypeStruct((T, LANES), F32),
                   jax.ShapeDtypeStruct((1, LANES), F32)),
        compiler_params=_params(("arbitrary", "arbitrary")),
    )(kv_lin, z, gain, *tabs, dk, dvk)


def _ffn_up(u, w_gu, name):
    T, D = u.shape
    F = w_gu.shape[1] // 2
    tm, tn = _tile(T, 640, 16), _tile(F, 1408, 128)
    nj = F // tn

    def body(u_ref, wg_ref, wu_ref, g_ref, up_ref, a_ref):
        uv = u_ref[...]
        g = jnp.dot(uv, wg_ref[...], preferred_element_type=F32)
        up = jnp.dot(uv, wu_ref[...], preferred_element_type=F32)
        g_ref[...] = g.astype(BF16)
        up_ref[...] = up.astype(BF16)
        a_ref[...] = (g * jax.nn.sigmoid(g) * up).astype(BF16)

    o_spec = pl.BlockSpec((tm, tn), lambda i, j: (i, j))
    sh = jax.ShapeDtypeStruct((T, F), BF16)
    return pl.pallas_call(
        body, name=name, grid=(T // tm, nj),
        in_specs=[pl.BlockSpec((tm, D), lambda i, j: (i, 0)),
                  pl.BlockSpec((D, tn), lambda i, j: (0, j)),
                  pl.BlockSpec((D, tn), lambda i, j: (0, j + nj))],
        out_specs=(o_spec, o_spec, o_spec), out_shape=(sh, sh, sh),
        compiler_params=_params(("parallel", "parallel")),
    )(u, w_gu, w_gu)


def _ffn_down_bwd(dh, w_down, g, up, name):
    T, D = dh.shape
    F = w_down.shape[0]
    tm, tn = _tile(T, 640, 16), _tile(F, 1408, 128)

    def body(dh_ref, w_ref, g_ref, up_ref, dg_ref, dup_ref):
        da = 0.5 * lax.dot_general(dh_ref[...].astype(BF16), w_ref[...], (((1,), (1,)), ((), ())),
                                   preferred_element_type=F32)
        gv = g_ref[...].astype(F32)
        sg = jax.nn.sigmoid(gv)
        silu = gv * sg
        dup_ref[...] = (da * silu).astype(BF16)
        dg_ref[...] = (da * up_ref[...].astype(F32) * (sg + silu * (1.0 - sg))).astype(BF16)

    t_spec = pl.BlockSpec((tm, tn), lambda i, j: (i, j))
    sh = jax.ShapeDtypeStruct((T, F), BF16)
    return pl.pallas_call(
        body, name=name, grid=(T // tm, F // tn),
        in_specs=[pl.BlockSpec((tm, D), lambda i, j: (i, 0)),
                  pl.BlockSpec((tn, D), lambda i, j: (j, 0)), t_spec, t_spec],
        out_specs=(t_spec, t_spec), out_shape=(sh, sh),
        compiler_params=_params(("parallel", "parallel")),
    )(dh, w_down, g, up)


def _ffn_up_bwd_dx(dg, dup, w_gu, name):
    T, F = dg.shape
    D = w_gu.shape[0]
    tm, tk = _tile(T, 640, 16), _tile(F, 1408, 128)
    nk = F // tk
    nt = (((1,), (1,)), ((), ()))

    def body(dg_ref, dup_ref, wg_ref, wu_ref, o_ref, acc_ref):
        k = pl.program_id(1)

        @pl.when(k == 0)
        def _():
            acc_ref[...] = jnp.zeros_like(acc_ref)

        acc_ref[...] += (lax.dot_general(dg_ref[...], wg_ref[...], nt, preferred_element_type=F32)
                         + lax.dot_general(dup_ref[...], wu_ref[...], nt, preferred_element_type=F32))

        @pl.when(k == nk - 1)
        def _():
            o_ref[...] = acc_ref[...]

    return pl.pallas_call(
        body, name=name, grid=(T // tm, nk),
        in_specs=[pl.BlockSpec((tm, tk), lambda i, k: (i, k)),
                  pl.BlockSpec((tm, tk), lambda i, k: (i, k)),
                  pl.BlockSpec((D, tk), lambda i, k: (0, k)),
                  pl.BlockSpec((D, tk), lambda i, k: (0, k + nk))],
        out_specs=pl.BlockSpec((tm, D), lambda i, k: (i, 0)),
        out_shape=jax.ShapeDtypeStruct((T, D), F32),
        scratch_shapes=[pltpu.VMEM((tm, D), F32)],
        compiler_params=_params(("parallel", "arbitrary")),
    )(dg, dup, w_gu, w_gu)


def _logsig(x):
    return jnp.minimum(x, 0.0) - jnp.log(1.0 + jnp.exp(-jnp.abs(x)))


def _cum_fwd(fl, flm, bf, name):
    B, S, _ = fl.shape
    nb = S // LANES

    def body(fl_ref, flm_ref, bf_ref, cum_ref, cumm_ref):
        rows = lax.broadcasted_iota(jnp.int32, (LANES, LANES), 0)
        cols = lax.broadcasted_iota(jnp.int32, (LANES, LANES), 1)
        tri = (rows >= cols).astype(F32)
        bias = bf_ref[...]
        lfm = jnp.where(rows < N_META, _logsig(flm_ref[...] + bias), 0.0)
        cm = jnp.dot(tri, lfm, precision=lax.Precision.HIGHEST, preferred_element_type=F32)
        cumm_ref[...] = cm
        base = cm[LANES - 1:LANES, :]
        for b in range(B):
            def blk(i, carry):
                r0 = pl.multiple_of(i * LANES, LANES)
                lf = _logsig(fl_ref[b, pl.ds(r0, LANES), :] + bias)
                c = jnp.dot(tri, lf, precision=lax.Precision.HIGHEST,
                            preferred_element_type=F32) + carry
                cum_ref[b, pl.ds(r0, LANES), :] = c
                return c[LANES - 1:LANES, :]

            lax.fori_loop(0, nb, blk, base)

    return pl.pallas_call(
        body, name=name,
        out_shape=(jax.ShapeDtypeStruct((B, S, LANES), F32),
                   jax.ShapeDtypeStruct((LANES, LANES), F32)),
        compiler_params=_params(),
    )(fl, flm, bf)


def _cum_bwd(dc, dcm, fl, flm, bf, name):
    B, S, _ = fl.shape
    nb = S // LANES

    def body(dc_ref, dcm_ref, fl_ref, flm_ref, bf_ref, dfl_ref, dflm_ref, dbf_ref):
        rows = lax.broadcasted_iota(jnp.int32, (LANES, LANES), 0)
        cols = lax.broadcasted_iota(jnp.int32, (LANES, LANES), 1)
        triu = (rows <= cols).astype(F32)
        bias = bf_ref[...]
        total = jnp.zeros((1, LANES), F32)
        dbf = jnp.zeros((1, LANES), F32)
        for b in range(B):
            tail = jnp.zeros((1, LANES), F32)
            for t in range(nb):
                r0 = (nb - 1 - t) * LANES
                rc = jnp.dot(triu, dc_ref[b, r0:r0 + LANES, :], precision=lax.Precision.HIGHEST,
                             preferred_element_type=F32) + tail
                xv = fl_ref[b, r0:r0 + LANES, :] + bias
                d = rc / (1.0 + jnp.exp(xv))
                dfl_ref[b, r0:r0 + LANES, :] = d
                tail = rc[0:1, :]
                dbf = dbf + jnp.sum(d, axis=0, keepdims=True)
            total = total + tail
        rcm = jnp.dot(triu, dcm_ref[...], precision=lax.Precision.HIGHEST,
                      preferred_element_type=F32) + total
        dm = jnp.where(rows < N_META, rcm / (1.0 + jnp.exp(flm_ref[...] + bias)), 0.0)
        dflm_ref[...] = dm
        dbf_ref[...] = dbf + jnp.sum(dm, axis=0, keepdims=True)

    return pl.pallas_call(
        body, name=name,
        out_shape=(jax.ShapeDtypeStruct((B, S, LANES), F32),
                   jax.ShapeDtypeStruct((LANES, LANES), F32),
                   jax.ShapeDtypeStruct((1, LANES), F32)),
        compiler_params=_params(),
    )(dc, dcm, fl, flm, bf)


_NT = (((1,), (1,)), ((), ()))


def _attn_specs(S, NX, qw, v_col0):
    mb = NX // META_BLK
    vb = v_col0 // qw
    return (pl.BlockSpec((S, qw), lambda b, p: (b, p)),
            pl.BlockSpec((META_BLK, qw), lambda b, p: (mb, p)),
            pl.BlockSpec((S, qw), lambda b, p: (b, vb + p)),
            pl.BlockSpec((META_BLK, qw), lambda b, p: (mb, vb + p)),
            pl.BlockSpec((S, LANES), lambda b, p: (b, p)))


def _cum_specs(S, TK):
    return [pl.BlockSpec((1, 2, S, 1), lambda b, p: (b, p, 0, 0)),
            pl.BlockSpec((1, 2, S // TK, 1, TK), lambda b, p: (b, p, 0, 0, 0)),
            pl.BlockSpec((1, 2, 1, META_BLK), lambda b, p: (b, p, 0, 0))]


def _attn_fwd(qn, kn, vsrc, v_col0, mla, scale, S, NX, name, cq=None, ck=None, cmk=None):
    T = qn.shape[0]
    B = NX // S
    qw = 2 * LANES if mla else LANES
    npair = qn.shape[1] // qw
    TQ = min(512, S)
    TK = TQ
    forget = cq is not None

    def body(*refs):
        if forget:
            q_ref, k_ref, km_ref, v_ref, vm_ref, cq_ref, ck_ref, cmk_ref, _, o_ref, lse_ref = refs
        else:
            q_ref, k_ref, km_ref, v_ref, vm_ref, _, o_ref, lse_ref = refs
        lo = lax.broadcasted_iota(jnp.int32, (1, LANES), 1) < HALF
        mcol = lax.broadcasted_iota(jnp.int32, (TQ, META_BLK), 1)
        rr = lax.broadcasted_iota(jnp.int32, (TQ, TK), 0)
        cc = lax.broadcasted_iota(jnp.int32, (TQ, TK), 1)
        two = lax.broadcasted_iota(jnp.int32, (TQ, 2), 1)
        for qi in range(S // TQ):
            q0 = qi * TQ
            outs, lses = [], []
            for e in range(2):
                sl = slice(e * LANES, (e + 1) * LANES) if mla else slice(None)
                if mla:
                    qt = q_ref[q0:q0 + TQ, sl]
                else:
                    qt = jnp.where(lo if e == 0 else ~lo, q_ref[q0:q0 + TQ, :], 0.0).astype(BF16)
                s = lax.dot_general(qt, km_ref[:, sl], _NT, preferred_element_type=F32) * scale
                if forget:
                    cqt = cq_ref[0, e, q0:q0 + TQ, :]
                    s = s + cqt - cmk_ref[0, e]
                s = jnp.where(mcol < N_META, s, NEG)
                m = jnp.max(s, axis=1, keepdims=True)
                p = jnp.exp(s - m)
                l = jnp.sum(p, axis=1, keepdims=True)
                acc = jnp.dot(p.astype(BF16), vm_ref[:, sl].astype(BF16), preferred_element_type=F32)

                def kblock(kj, carry):
                    m, l, acc = carry
                    k0 = pl.multiple_of(kj * TK, TK)
                    kt = k_ref[pl.ds(k0, TK), sl]
                    vt = v_ref[pl.ds(k0, TK), sl].astype(BF16)
                    s = lax.dot_general(qt, kt, _NT, preferred_element_type=F32) * scale
                    if forget:
                        s = s + cqt - ck_ref[0, e, kj]
                    s = jnp.where(rr + q0 >= cc + k0, s, NEG)
                    m2 = jnp.maximum(m, jnp.max(s, axis=1, keepdims=True))
                    alpha = jnp.exp(m - m2)
                    p = jnp.exp(s - m2)
                    l2 = alpha * l + jnp.sum(p, axis=1, keepdims=True)
                    acc2 = alpha * acc + jnp.dot(p.astype(BF16), vt, preferred_element_type=F32)
                    return m2, l2, acc2

                m, l, acc = lax.fori_loop(0, (q0 + TQ) // TK, kblock, (m, l, acc))
                outs.append(acc / l)
                lses.append(m + jnp.log(l))
            first = pltpu.roll(outs[0], HALF, 1) if mla else outs[0]
            o_ref[q0:q0 + TQ, :] = jnp.where(lo, first, outs[1])
            lse_ref[0, 0, q0:q0 + TQ, :] = jnp.where(two == 0, lses[0], lses[1])

    qk, kmeta, vv, vmeta, pair = _attn_specs(S, NX, qw, v_col0)
    specs = [qk, qk, kmeta, vv, vmeta]
    ins = [qn, kn, kn, vsrc, vsrc]
    if forget:
        specs += _cum_specs(S, TK)
        ins += [cq, ck, cmk]
    specs.append(pl.BlockSpec(memory_space=pl.ANY))
    ins.append(jnp.zeros((T, npair * LANES), F32))
    lse_spec = pl.BlockSpec((1, 1, S, 2), lambda b, p: (b, p, 0, 0))
    return pl.pallas_call(
        body, name=name, grid=(B, npair), in_specs=specs, out_specs=(pair, lse_spec),
        out_shape=(jax.ShapeDtypeStruct((T, npair * LANES), F32),
                   jax.ShapeDtypeStruct((B, npair, S, 2), F32)),
        input_output_aliases={len(ins) - 1: 0},
        compiler_params=_params(("parallel", "parallel")),
    )(*ins)


def _attn_bwd(qn, kn, vsrc, v_col0, o, lse, do, mla, scale, S, NX, name, cq=None, ck=None, cmk=None):
    T, W = qn.shape
    B = NX // S
    qw = 2 * LANES if mla else LANES
    npair = W // qw
    TQ = min(512, S)
    TK = TQ
    forget = cq is not None

    def body(*refs):
        refs = list(refs)
        q_ref, k_ref, km_ref, v_ref, vm_ref, o_ref, do_ref, lse_ref = refs[:8]
        pos = 8
        if forget:
            cq_ref, ck_ref, cmk_ref = refs[8:11]
            pos = 11
        pos += 3
        dq_ref, dk_ref, dv_ref, dkm_ref, dvm_ref = refs[pos:pos + 5]
        if forget:
            dck_ref, dcm_ref = refs[pos + 5:pos + 7]
            dck_ref[...] = jnp.zeros_like(dck_ref)
            dcm_ref[...] = jnp.zeros_like(dcm_ref)
        dk_ref[...] = jnp.zeros_like(dk_ref)
        dv_ref[...] = jnp.zeros_like(dv_ref)
        dkm_ref[...] = jnp.zeros_like(dkm_ref)
        dvm_ref[...] = jnp.zeros_like(dvm_ref)
        lo = lax.broadcasted_iota(jnp.int32, (1, LANES), 1) < HALF
        mcol = lax.broadcasted_iota(jnp.int32, (TQ, META_BLK), 1)
        rr = lax.broadcasted_iota(jnp.int32, (TQ, TK), 0)
        cc = lax.broadcasted_iota(jnp.int32, (TQ, TK), 1)
        two = lax.broadcasted_iota(jnp.int32, (TQ, 2), 1)
        for qi in range(S // TQ):
            q0 = qi * TQ
            dof = do_ref[q0:q0 + TQ, :]
            prod = dof * o_ref[q0:q0 + TQ, :]
            lse2 = lse_ref[0, 0, q0:q0 + TQ, :]
            dqs = []
            for e in range(2):
                sl = slice(e * LANES, (e + 1) * LANES) if mla else slice(None)
                mine = lo if e == 0 else ~lo
                if mla:
                    qt = q_ref[q0:q0 + TQ, sl]
                    dot = jnp.where(lo, 0.0, pltpu.roll(dof, HALF, 1) if e == 0 else dof).astype(BF16)
                else:
                    qt = jnp.where(mine, q_ref[q0:q0 + TQ, :], 0.0).astype(BF16)
                    dot = jnp.where(mine, dof, 0.0).astype(BF16)
                delta = jnp.sum(jnp.where(mine, prod, 0.0), axis=1, keepdims=True)
                lse_t = jnp.sum(jnp.where(two == e, lse2, 0.0), axis=1, keepdims=True)
                km = km_ref[:, sl]
                vm = vm_ref[:, sl].astype(BF16)
                s = lax.dot_general(qt, km, _NT, preferred_element_type=F32) * scale
                if forget:
                    cqt = cq_ref[0, e, q0:q0 + TQ, :]
                    s = s + cqt - cmk_ref[0, e]
                p = jnp.where(mcol < N_META, jnp.exp(s - lse_t), 0.0)
                dp = lax.dot_general(dot, vm, _NT, preferred_element_type=F32)
                ds = p * (dp - delta)
                dq = jnp.dot(ds.astype(BF16), km, preferred_element_type=F32)
                dkm_ref[0, :, sl] += jnp.dot(ds.T.astype(BF16), qt, preferred_element_type=F32) * scale
                dvm_ref[0, :, sl] += jnp.dot(p.T.astype(BF16), dot, preferred_element_type=F32)
                if forget:
                    dcm_ref[0, e] += -jnp.sum(ds, axis=0, keepdims=True)

                def kblock(kj, dq):
                    k0 = pl.multiple_of(kj * TK, TK)
                    kt = k_ref[pl.ds(k0, TK), sl]
                    vt = v_ref[pl.ds(k0, TK), sl].astype(BF16)
                    s = lax.dot_general(qt, kt, _NT, preferred_element_type=F32) * scale
                    if forget:
                        s = s + cqt - ck_ref[0, e, kj]
                    p = jnp.where(rr + q0 >= cc + k0, jnp.exp(s - lse_t), 0.0)
                    dp = lax.dot_general(dot, vt, _NT, preferred_element_type=F32)
                    ds = p * (dp - delta)
                    dk_ref[pl.ds(k0, TK), sl] += jnp.dot(ds.T.astype(BF16), qt,
                                                         preferred_element_type=F32) * scale
                    dv_ref[pl.ds(k0, TK), sl] += jnp.dot(p.T.astype(BF16), dot,
                                                         preferred_element_type=F32)
                    if forget:
                        dck_ref[0, e, kj] += -jnp.sum(ds, axis=0, keepdims=True)
                    return dq + jnp.dot(ds.astype(BF16), kt, preferred_element_type=F32)

                dq = lax.fori_loop(0, (q0 + TQ) // TK, kblock, dq) * scale
                if mla:
                    dq_ref[q0:q0 + TQ, sl] = dq
                else:
                    dqs.append(dq)
            if not mla:
                dq_ref[q0:q0 + TQ, :] = jnp.where(lo, dqs[0], dqs[1])

    qk, kmeta, vv, vmeta, pair = _attn_specs(S, NX, qw, v_col0)
    lse_spec = pl.BlockSpec((1, 1, S, 2), lambda b, p: (b, p, 0, 0))
    specs = [qk, qk, kmeta, vv, vmeta, pair, pair, lse_spec]
    ins = [qn, kn, kn, vsrc, vsrc, o, do, lse]
    if forget:
        specs += _cum_specs(S, TK)
        ins += [cq, ck, cmk]
    first_alias = len(ins)
    specs += [pl.BlockSpec(memory_space=pl.ANY)] * 3
    ins += [jnp.zeros((T, W), F32)] * 3
    mspec = pl.BlockSpec((1, META_BLK, qw), lambda b, p: (b, 0, p))
    out_specs = [qk, qk, qk, mspec, mspec]
    tok = jax.ShapeDtypeStruct((T, W), F32)
    met = jax.ShapeDtypeStruct((B, META_BLK, W), F32)
    out_shape = [tok, tok, tok, met, met]
    if forget:
        out_specs += _cum_specs(S, TK)[1:]
        out_shape += [jax.ShapeDtypeStruct((B, HEADS, S // TK, 1, TK), F32),
                      jax.ShapeDtypeStruct((B, HEADS, 1, META_BLK), F32)]
    return pl.pallas_call(
        body, name=name, grid=(B, npair), in_specs=specs, out_specs=tuple(out_specs),
        out_shape=tuple(out_shape),
        input_output_aliases={first_alias: 0, first_alias + 1: 1, first_alias + 2: 2},
        compiler_params=_params(("parallel", "parallel")),
    )(*ins)


def _gate_fwd(z, bg, of, om, name):
    T, D = of.shape
    tm = _tile(T, 640, 16)

    def body(z_ref, bg_ref, of_ref, om_ref, o_ref):
        gt = jax.nn.sigmoid(z_ref[...] + bg_ref[...])
        o_ref[...] = (gt[:, :D] * of_ref[...] + gt[:, D:] * om_ref[...]).astype(BF16)

    row = pl.BlockSpec((tm, D), lambda i: (i, 0))
    return pl.pallas_call(
        body, name=name, grid=(T // tm,),
        in_specs=[pl.BlockSpec((tm, 2 * D), lambda i: (i, 0)),
                  pl.BlockSpec((1, 2 * D), lambda i: (0, 0)), row, row],
        out_specs=row, out_shape=jax.ShapeDtypeStruct((T, D), BF16),
        compiler_params=_params(("parallel",)),
    )(z, bg, of, om)


def _gate_bwd(dmix, z, bg, of, om, name):
    T, D = of.shape
    tm = _tile(T, 640, 16)

    def body(dm_ref, z_ref, bg_ref, of_ref, om_ref, dgl_ref, dof_ref, dom_ref, dbg_ref):
        gt = jax.nn.sigmoid(z_ref[...] + bg_ref[...])
        dm = dm_ref[...]
        dof_ref[...] = (dm * gt[:, :D]).astype(BF16)
        dom_ref[...] = (dm * gt[:, D:]).astype(BF16)
        dgl = jnp.concatenate([dm * of_ref[...], dm * om_ref[...]], axis=1) * gt * (1.0 - gt)
        dgl_ref[...] = dgl.astype(BF16)

        @pl.when(pl.program_id(0) == 0)
        def _():
            dbg_ref[...] = jnp.zeros_like(dbg_ref)

        dbg_ref[...] += jnp.sum(dgl, axis=0, keepdims=True)

    row = pl.BlockSpec((tm, D), lambda i: (i, 0))
    wide = pl.BlockSpec((tm, 2 * D), lambda i: (i, 0))
    one = pl.BlockSpec((1, 2 * D), lambda i: (0, 0))
    return pl.pallas_call(
        body, name=name, grid=(T // tm,),
        in_specs=[row, wide, one, row, row], out_specs=(wide, row, row, one),
        out_shape=(jax.ShapeDtypeStruct((T, 2 * D), BF16), jax.ShapeDtypeStruct((T, D), BF16),
                   jax.ShapeDtypeStruct((T, D), BF16), jax.ShapeDtypeStruct((1, 2 * D), F32)),
        compiler_params=_params(("arbitrary",)),
    )(dmix, z, bg, of, om)


def _loss(h, tgt, n_valid, name):
    T, D = h.shape
    tm = _tile(T, 640, 16)

    def body(h_ref, t_ref, dh_ref, l_ref):
        i = pl.program_id(0)
        rows = lax.broadcasted_iota(jnp.int32, (tm, D), 0) + i * tm
        err = jnp.where(rows < n_valid, h_ref[...] - t_ref[...], 0.0)
        dh_ref[...] = err * (1.0 / D)

        @pl.when(i == 0)
        def _():
            l_ref[...] = jnp.zeros_like(l_ref)

        l_ref[...] += 0.5 * jnp.sum(jnp.sum(err * err, axis=1, keepdims=True) * (1.0 / D))

    row = pl.BlockSpec((tm, D), lambda i: (i, 0))
    acc = pl.BlockSpec((8, LANES), lambda i: (0, 0))
    return pl.pallas_call(
        body, name=name, grid=(T // tm,), in_specs=[row, row], out_specs=(row, acc),
        out_shape=(jax.ShapeDtypeStruct((T, D), F32), jax.ShapeDtypeStruct((8, LANES), F32)),
        compiler_params=_params(("arbitrary",)),
    )(h, tgt)


def _adamw(parts, w, m, v, name):
    P, R, C = parts.shape
    tr = _tile(R, max(8, (1 << 18) // C), 8)
    bc1 = 1.0 - ADAM_B1 ** ADAM_STEP
    bc2 = 1.0 - ADAM_B2 ** ADAM_STEP

    def body(p_ref, w_ref, m_ref, v_ref, g_ref, d_ref, m2_ref, v2_ref):
        g = p_ref[0].astype(F32)
        for j in range(1, P):
            g = g + p_ref[j].astype(F32)
        m2 = ADAM_B1 * m_ref[...] + (1.0 - ADAM_B1) * g
        v2 = ADAM_B2 * v_ref[...] + (1.0 - ADAM_B2) * (g * g)
        m_hat = m2 / bc1
        v_hat = v2 / bc2
        g_ref[...] = g
        d_ref[...] = -ADAM_LR * (m_hat / (jnp.sqrt(v_hat) + ADAM_EPS) + ADAM_WD * w_ref[...])
        m2_ref[...] = m2
        v2_ref[...] = v2

    row = pl.BlockSpec((tr, C), lambda i: (i, 0))
    sh = jax.ShapeDtypeStruct((R, C), F32)
    return pl.pallas_call(
        body, name=name, grid=(R // tr,),
        in_specs=[pl.BlockSpec((P, tr, C), lambda i: (0, i, 0)), row, row, row],
        out_specs=(row, row, row, row), out_shape=(sh, sh, sh, sh),
        compiler_params=_params(("parallel",)),
    )(parts, w, m, v)


def _peer(d):
    x, y, c = lax.axis_index("x"), lax.axis_index("y"), lax.axis_index("c")
    px = 1 - x if d & 4 else x
    py = 1 - y if d & 2 else y
    pc = 1 - c if d & 1 else c
    return (px, py, pc), 4 * px + 2 * py + pc


def _exchange(srcs, name, gather):
    n = len(srcs)

    def body(*refs):
        src_refs, out_refs = refs[:n], refs[n:2 * n]
        send_sems, recv_sems, local_sems = refs[2 * n:]
        _, me = _peer(0)

        def remote(w, d, landing):
            dev, lin = _peer(d)
            return pltpu.make_async_remote_copy(
                src_ref=src_refs[w] if gather else src_refs[w].at[lin],
                dst_ref=out_refs[w].at[lin if landing else me],
                send_sem=send_sems.at[d - 1, w], recv_sem=recv_sems.at[d - 1, w],
                device_id=dev, device_id_type=pl.DeviceIdType.MESH)

        own = [pltpu.make_async_copy(src_refs[w] if gather else src_refs[w].at[me],
                                     out_refs[w].at[me], local_sems.at[w]) for w in range(n)]
        for cp in own:
            cp.start()
        sent = [remote(w, d, False) for d in range(1, N_DEV) for w in range(n)]
        for cp in sent:
            cp.start()
        for d in range(1, N_DEV):
            for w in range(n):
                remote(w, d, True).wait_recv()
        for cp in sent:
            cp.wait_send()
        for cp in own:
            cp.wait()

    hbm = pl.BlockSpec(memory_space=pl.ANY)
    outs = pl.pallas_call(
        body, name=name, in_specs=[hbm] * n, out_specs=tuple([hbm] * n),
        out_shape=tuple(jax.ShapeDtypeStruct((N_DEV,) + s.shape[-2:], s.dtype) for s in srcs),
        scratch_shapes=[pltpu.SemaphoreType.DMA((N_DEV - 1, n)), pltpu.SemaphoreType.DMA((N_DEV - 1, n)),
                        pltpu.SemaphoreType.DMA((n,))],
    )(*srcs)
    return list(outs)


def _pack(arrs, cols, row_mult):
    flat = jnp.concatenate([a.reshape(-1) for a in arrs])
    n = flat.shape[0]
    quantum = cols * row_mult
    total = -(-n // quantum) * quantum
    return jnp.pad(flat, (0, total - n)).reshape(total // cols, cols)


def _pack_rows(arrs, cols, row_mult):
    flat = jnp.concatenate(arrs, axis=1)
    n = flat.shape[1]
    quantum = cols * row_mult
    total = -(-n // quantum) * quantum
    return jnp.pad(flat, ((0, 0), (0, total - n))).reshape(N_DEV, total // cols, cols)


def _unpack(packed, shapes):
    flat = packed.reshape(-1)
    out, off = [], 0
    for s in shapes:
        n = int(np.prod(s))
        out.append(flat[off:off + n].reshape(s))
        off += n
    return out


def _rope_tables(positions):
    inv_freq = ROPE_THETA ** (-jnp.arange(0, MLA_ROPE, 2, dtype=F32) / MLA_ROPE)
    ang = positions.astype(F32)[:, None] * inv_freq[None, :]
    cos, sin = jnp.cos(ang), jnp.sin(ang)
    n = positions.shape[0]
    ones, zeros = jnp.ones((n, MLA_NOPE), F32), jnp.zeros((n, MLA_NOPE), F32)
    tail1, tail0 = jnp.ones((n, LANES - MLA_QK), F32), jnp.zeros((n, LANES - MLA_QK), F32)
    z16 = jnp.zeros((n, 16), F32)
    c = jnp.concatenate([ones, cos, cos, tail1], axis=1)
    s1 = jnp.concatenate([zeros, -sin, z16, tail0], axis=1)
    s2 = jnp.concatenate([zeros, z16, sin, tail0], axis=1)
    return c, s1, s2


def kernel(x, meta_tokens, ffn1_norm, ffn1_w_gu, ffn1_w_down, mix_norm, w_in, b_forget, b_gate, fox_q_norm, fox_k_norm, mla_cq_norm, mla_w_uq, mla_ckv_norm, mla_w_ukv, mla_q_norm, mla_k_norm, w_branch_fox, w_branch_mla, w_out, ffn2_norm, ffn2_w_gu, ffn2_w_down, loss_target, m_meta_tokens, m_ffn1_norm, m_ffn1_w_gu, m_ffn1_w_down, m_mix_norm, m_w_in, m_b_forget, m_b_gate, m_fox_q_norm, m_fox_k_norm, m_mla_cq_norm, m_mla_w_uq, m_mla_ckv_norm, m_mla_w_ukv, m_mla_q_norm, m_mla_k_norm, m_w_branch_fox, m_w_branch_mla, m_w_out, m_ffn2_norm, m_ffn2_w_gu, m_ffn2_w_down, v_meta_tokens, v_ffn1_norm, v_ffn1_w_gu, v_ffn1_w_down, v_mix_norm, v_w_in, v_b_forget, v_b_gate, v_fox_q_norm, v_fox_k_norm, v_mla_cq_norm, v_mla_w_uq, v_mla_ckv_norm, v_mla_w_ukv, v_mla_q_norm, v_mla_k_norm, v_w_branch_fox, v_w_branch_mla, v_w_out, v_ffn2_norm, v_ffn2_w_gu, v_ffn2_w_down):
    names = ["meta_tokens", "ffn1_norm", "ffn1_w_gu", "ffn1_w_down", "mix_norm", "w_in", "b_forget",
             "b_gate", "fox_q_norm", "fox_k_norm", "mla_cq_norm", "mla_w_uq", "mla_ckv_norm",
             "mla_w_ukv", "mla_q_norm", "mla_k_norm", "w_branch_fox", "w_branch_mla", "w_out",
             "ffn2_norm", "ffn2_w_gu", "ffn2_w_down"]
    W = dict(zip(names, [meta_tokens, ffn1_norm, ffn1_w_gu, ffn1_w_down, mix_norm, w_in, b_forget,
                         b_gate, fox_q_norm, fox_k_norm, mla_cq_norm, mla_w_uq, mla_ckv_norm,
                         mla_w_ukv, mla_q_norm, mla_k_norm, w_branch_fox, w_branch_mla, w_out,
                         ffn2_norm, ffn2_w_gu, ffn2_w_down]))
    Mo = dict(zip(names, [m_meta_tokens, m_ffn1_norm, m_ffn1_w_gu, m_ffn1_w_down, m_mix_norm, m_w_in,
                          m_b_forget, m_b_gate, m_fox_q_norm, m_fox_k_norm, m_mla_cq_norm,
                          m_mla_w_uq, m_mla_ckv_norm, m_mla_w_ukv, m_mla_q_norm, m_mla_k_norm,
                          m_w_branch_fox, m_w_branch_mla, m_w_out, m_ffn2_norm, m_ffn2_w_gu,
                          m_ffn2_w_down]))
    Vo = dict(zip(names, [v_meta_tokens, v_ffn1_norm, v_ffn1_w_gu, v_ffn1_w_down, v_mix_norm, v_w_in,
                          v_b_forget, v_b_gate, v_fox_q_norm, v_fox_k_norm, v_mla_cq_norm,
                          v_mla_w_uq, v_mla_ckv_norm, v_mla_w_ukv, v_mla_q_norm, v_mla_k_norm,
                          v_w_branch_fox, v_w_branch_mla, v_w_out, v_ffn2_norm, v_ffn2_w_gu,
                          v_ffn2_w_down]))

    B, S, D = x.shape
    NX = B * S
    T = NX + META_BLK
    H = HEADS
    assert NX % META_BLK == 0 and S % LANES == 0
    me = 4 * lax.axis_index("x") + 2 * lax.axis_index("y") + lax.axis_index("c")

    big = [("ffn1_w_gu", 1), ("ffn1_w_down", 0), ("w_in", 1), ("mla_w_uq", 1), ("mla_w_ukv", 1),
           ("w_branch_fox", 1), ("w_branch_mla", 1), ("w_out", 0), ("ffn2_w_gu", 1), ("ffn2_w_down", 0)]
    gathered = _exchange([W[n][0].astype(BF16) for n, _ in big], "gather_weights", gather=True)
    full = {}
    for (n, ax), blk in zip(big, gathered):
        _, r, c = blk.shape
        full[n] = blk.transpose(1, 0, 2).reshape(r, N_DEV * c) if ax == 1 else blk.reshape(N_DEV * r, c)

    wi = full["w_in"]
    o_fq = 0
    o_f = 3 * FOX_W
    o_cq = o_f + HEADS
    o_kr = o_cq + MLA_Q_RANK + MLA_KV_RANK
    o_g = o_kr + MLA_ROPE
    w_in_p = jnp.concatenate([
        wi[:, o_g:o_g + 2 * D], wi[:, o_fq:o_f], wi[:, o_cq:o_kr],
        jnp.pad(wi[:, o_f:o_cq], ((0, 0), (0, LANES - HEADS))),
        jnp.pad(wi[:, o_kr:o_g], ((0, 0), (0, LANES - MLA_ROPE)))], axis=1)
    Z_G, Z_FQ = 0, 2 * D
    Z_FK, Z_FV = Z_FQ + FOX_W, Z_FQ + 2 * FOX_W
    Z_CQ = Z_FQ + 3 * FOX_W
    Z_CKV = Z_CQ + MLA_Q_RANK
    Z_F = Z_CKV + MLA_KV_RANK
    Z_KR = Z_F + LANES
    w_uq_p = jnp.pad(full["mla_w_uq"].reshape(MLA_Q_RANK, H, MLA_QK),
                     ((0, 0), (0, 0), (0, LANES - MLA_QK))).reshape(MLA_Q_RANK, H * LANES)

    def pad_lanes(a, w=LANES):
        return jnp.pad(a, [(0, 0)] * (a.ndim - 1) + [(0, w - a.shape[-1])])

    def rows_T(real, meta=None):
        n = real.shape[1]
        parts = [real]
        used = 0
        if meta is not None:
            parts.append(meta)
            used = meta.shape[0]
        if T - NX - used:
            parts.append(jnp.zeros((T - NX - used, n), real.dtype))
        return jnp.concatenate(parts, axis=0)

    def put_meta(tok, meta_per_seq):
        return lax.dynamic_update_slice(tok, meta_per_seq.sum(0), (NX, 0))

    (meta_g,) = _exchange([meta_tokens], "gather_meta", gather=True)
    meta_full = meta_g.transpose(1, 0, 2).reshape(N_META, D)
    h0 = rows_T(x.reshape(NX, D), meta_full)
    tgt = rows_T(loss_target.reshape(NX, D))

    def ffn_fwd(h, norm, w_gu, w_down, tag):
        u = _norm_fwd(h, 0, D, D, norm, D, D, tag + "_norm")
        g, up, a = _ffn_up(u, w_gu, tag + "_up")
        h_out = _mm(a, w_down, "nn", tag + "_down", scale=0.5, res=h)
        return h_out, (u, g, up, a)

    h1, ffn1_saved = ffn_fwd(h0, W["ffn1_norm"], full["ffn1_w_gu"], full["ffn1_w_down"], "ffn1")

    u2 = _norm_fwd(h1, 0, D, D, W["mix_norm"], D, D, "mix_norm")
    z = _mm(u2, w_in_p, "nn", "w_in")

    gq_f = jnp.tile(W["fox_q_norm"], (1, 2))
    gk_f = jnp.tile(W["fox_k_norm"], (1, 2))
    fqn = _norm_fwd(z, Z_FQ, FOX_W, LANES, gq_f, FOX_HD, FOX_HD, "fox_q_norm")
    fkn = _norm_fwd(z, Z_FK, FOX_W, LANES, gk_f, FOX_HD, FOX_HD, "fox_k_norm")
    fl = z[:NX, Z_F:Z_F + LANES].reshape(B, S, LANES)
    flm = z[NX:, Z_F:Z_F + LANES]
    bf = pad_lanes(W["b_forget"])
    cum, cumm = _cum_fwd(fl, flm, bf, "forget_cum")
    TK = min(512, S)
    cum_h = cum[:, :, :H].transpose(0, 2, 1)
    cq = cum_h[..., None]
    ck = cum_h.reshape(B, H, S // TK, 1, TK)
    cmk = jnp.broadcast_to(cumm[:, :H].T[None, :, None, :], (B, H, 1, META_BLK))
    o_fox, lse_fox = _attn_fwd(fqn, fkn, z, Z_FV, False, FOX_HD ** -0.5, S, NX, "fox_attn", cq, ck, cmk)
    of = _mm(o_fox, full["w_branch_fox"], "nn", "branch_fox")

    pos = jnp.concatenate([jnp.tile(jnp.arange(S) + N_META, B), jnp.arange(META_BLK)])
    tabs = _rope_tables(pos)
    cqn = _norm_fwd(z, Z_CQ, MLA_Q_RANK, MLA_Q_RANK, W["mla_cq_norm"], MLA_Q_RANK, MLA_Q_RANK, "mla_cq_norm")
    q_lin = _mm(cqn, w_uq_p, "nn", "mla_uq")
    ckvn = _norm_fwd(z, Z_CKV, MLA_KV_RANK, MLA_KV_RANK, W["mla_ckv_norm"], MLA_KV_RANK, MLA_KV_RANK,
                     "mla_ckv_norm")
    kv_lin = _mm(ckvn, full["mla_w_ukv"], "nn", "mla_ukv")
    gq_m, gk_m = pad_lanes(W["mla_q_norm"]), pad_lanes(W["mla_k_norm"])
    mqn = _norm_fwd(q_lin, 0, H * LANES, LANES, gq_m, LANES, MLA_QK, "mla_q_norm", tabs=tabs)
    mkn = _mla_k_fwd(kv_lin, z, Z_KR, gk_m, tabs, "mla_k_norm")
    o_mla, lse_mla = _attn_fwd(mqn, mkn, kv_lin, 0, True, MLA_QK ** -0.5, S, NX, "mla_attn")
    om = _mm(o_mla, full["w_branch_mla"], "nn", "branch_mla")

    mix = _gate_fwd(z, W["b_gate"], of, om, "gate_mix")
    h2 = _mm(mix, full["w_out"], "nn", "w_out", res=h1)

    h3, ffn2_saved = ffn_fwd(h2, W["ffn2_norm"], full["ffn2_w_gu"], full["ffn2_w_down"], "ffn2")

    dh3, loss_acc = _loss(h3, tgt, NX, "loss")
    loss = lax.psum(loss_acc[0, 0], AXES)

    G = {}

    def ffn_bwd(dh, h, norm, w_gu, w_down, saved, tag):
        u, g, up, a = saved
        G[tag + "_w_down"] = _mm(a, dh, "tn", tag + "_dw_down", scale=0.5)
        dg, dup = _ffn_down_bwd(dh, w_down, g, up, tag + "_down_bwd")
        G[tag + "_w_gu"] = jnp.concatenate([_mm(u, dg, "tn", tag + "_dw_g"),
                                            _mm(u, dup, "tn", tag + "_dw_u")], axis=1)
        du = _ffn_up_bwd_dx(dg, dup, w_gu, tag + "_up_bwd")
        dh_in, G[tag + "_norm"] = _norm_bwd(h, 0, D, D, norm, D, D, du, tag + "_norm_bwd", res=dh)
        return dh_in

    dh2 = ffn_bwd(dh3, h2, W["ffn2_norm"], full["ffn2_w_gu"], full["ffn2_w_down"], ffn2_saved, "ffn2")

    G["w_out"] = _mm(mix, dh2, "tn", "dw_out")
    dmix = _mm(dh2, full["w_out"], "nt", "w_out_bwd")
    dgl, dof, dom, G["b_gate"] = _gate_bwd(dmix, z, W["b_gate"], of, om, "gate_bwd")

    G["w_branch_fox"] = _mm(o_fox, dof, "tn", "dw_branch_fox")
    do_fox = _mm(dof, full["w_branch_fox"], "nt", "branch_fox_bwd")
    dq_f, dk_f, dv_f, dkm_f, dvm_f, dck, dcmk = _attn_bwd(
        fqn, fkn, z, Z_FV, o_fox, lse_fox, do_fox, False, FOX_HD ** -0.5, S, NX, "fox_attn_bwd", cq, ck, cmk)
    dk_f, dv_f = put_meta(dk_f, dkm_f), put_meta(dv_f, dvm_f)
    dfq, gq = _norm_bwd(z, Z_FQ, FOX_W, LANES, gq_f, FOX_HD, FOX_HD, dq_f, "fox_q_norm_bwd", out_dtype=BF16)
    dfk, gk = _norm_bwd(z, Z_FK, FOX_W, LANES, gk_f, FOX_HD, FOX_HD, dk_f, "fox_k_norm_bwd", out_dtype=BF16)
    G["fox_q_norm"] = gq[:, :FOX_HD] + gq[:, FOX_HD:]
    G["fox_k_norm"] = gk[:, :FOX_HD] + gk[:, FOX_HD:]
    dc = pad_lanes(dck.reshape(B, H, S).transpose(0, 2, 1))
    dcm = pad_lanes(dcmk.sum(0)[:, 0, :].T)
    dcm = jnp.where(jnp.arange(LANES)[:, None] < N_META, dcm, 0.0)
    dfl, dflm, dbf = _cum_bwd(dc, dcm, fl, flm, bf, "forget_cum_bwd")
    G["b_forget"] = dbf[:, :HEADS]
    dfl_t = rows_T(dfl.reshape(NX, LANES), dflm)

    G["w_branch_mla"] = _mm(o_mla, dom, "tn", "dw_branch_mla")
    do_mla = _mm(dom, full["w_branch_mla"], "nt", "branch_mla_bwd")
    dq_m, dk_m, dvk, dkm_m, dvkm = _attn_bwd(
        mqn, mkn, kv_lin, 0, o_mla, lse_mla, do_mla, True, MLA_QK ** -0.5, S, NX, "mla_attn_bwd")
    dk_m, dvk = put_meta(dk_m, dkm_m), put_meta(dvk, dvkm)
    dq_lin, gq = _norm_bwd(q_lin, 0, H * LANES, LANES, gq_m, LANES, MLA_QK, dq_m, "mla_q_norm_bwd", tabs=tabs)
    G["mla_q_norm"] = gq[:, :MLA_QK]
    G["mla_w_uq"] = _mm(cqn, dq_lin, "tn", "dw_uq").reshape(MLA_Q_RANK, H, LANES)[:, :, :MLA_QK].reshape(
        MLA_Q_RANK, H * MLA_QK)
    dcqn = _mm(dq_lin, w_uq_p, "nt", "mla_uq_bwd")
    dcq, G["mla_cq_norm"] = _norm_bwd(z, Z_CQ, MLA_Q_RANK, MLA_Q_RANK, W["mla_cq_norm"], MLA_Q_RANK,
                                      MLA_Q_RANK, dcqn, "mla_cq_norm_bwd", out_dtype=BF16)
    dkv_lin, dkr, gk = _mla_k_bwd(kv_lin, z, Z_KR, gk_m, tabs, dk_m, dvk, "mla_k_norm_bwd")
    G["mla_k_norm"] = gk[:, :MLA_QK]
    G["mla_w_ukv"] = _mm(ckvn, dkv_lin, "tn", "dw_ukv")
    dckvn = _mm(dkv_lin, full["mla_w_ukv"], "nt", "mla_ukv_bwd")
    dckv, G["mla_ckv_norm"] = _norm_bwd(z, Z_CKV, MLA_KV_RANK, MLA_KV_RANK, W["mla_ckv_norm"], MLA_KV_RANK,
                                        MLA_KV_RANK, dckvn, "mla_ckv_norm_bwd", out_dtype=BF16)

    dz = jnp.concatenate([dgl, dfq, dfk, dv_f.astype(BF16), dcq, dckv, dfl_t.astype(BF16),
                          dkr.astype(BF16)], axis=1)
    dw_in_p = _mm(u2, dz, "tn", "dw_in")
    G["w_in"] = jnp.concatenate([
        dw_in_p[:, Z_FQ:Z_CQ], dw_in_p[:, Z_F:Z_F + HEADS], dw_in_p[:, Z_CQ:Z_F],
        dw_in_p[:, Z_KR:Z_KR + MLA_ROPE], dw_in_p[:, Z_G:Z_G + 2 * D]], axis=1)
    du2 = _mm(dz, w_in_p, "nt", "w_in_bwd")
    dh1, G["mix_norm"] = _norm_bwd(h1, 0, D, D, W["mix_norm"], D, D, du2, "mix_norm_bwd", res=dh2)

    dh0 = ffn_bwd(dh1, h0, W["ffn1_norm"], full["ffn1_w_gu"], full["ffn1_w_down"], ffn1_saved, "ffn1")
    grad_x = dh0[:NX].reshape(B, S, D)
    G["meta_tokens"] = dh0[NX:NX + N_META]

    per_dest = []
    for n, ax in big:
        g = G[n]
        r, c = W[n].shape[1:]
        per_dest.append((g.reshape(r, N_DEV, c).transpose(1, 0, 2) if ax == 1
                         else g.reshape(N_DEV, r, c)).astype(BF16))
    parts = _exchange(per_dest, "scatter_grads", gather=False)
    res = {}
    for (n, _), p in zip(big, parts):
        outs4 = _adamw(p, W[n][0], Mo[n][0], Vo[n][0], "adamw_" + n)
        for key, arr in zip(("g", "d", "m", "v"), outs4):
            res[key, n] = arr[None]

    small = [n for n in names if n not in dict(big) and n != "meta_tokens"]
    small_shapes = [W[n].shape for n in small]
    spack = _pack([G["meta_tokens"]] + [G[n] for n in small], 1024, 8)
    (sparts,) = _exchange([spack], "gather_small_grads", gather=True)
    sflat = sparts.reshape(N_DEV, -1)
    dsh = D // N_DEV
    meta_part = lax.dynamic_slice(sflat[:, :N_META * D].reshape(N_DEV, N_META, D),
                                  (0, 0, me * dsh), (N_DEV, N_META, dsh)).reshape(N_DEV, -1)
    rep_len = sum(int(np.prod(s)) for s in small_shapes)
    rep_part = sflat[:, N_META * D:N_META * D + rep_len]
    sp = _pack_rows([meta_part, rep_part], LANES, 8)
    pks = lambda src: _pack([src["meta_tokens"]] + [src[n] for n in small], LANES, 8)
    g_s, d_s, m_s, v_s = _adamw(sp, pks(W), pks(Mo), pks(Vo), "adamw_small")
    shapes_s = [W["meta_tokens"].shape] + small_shapes
    for key, packed in (("g", g_s), ("d", d_s), ("m", m_s), ("v", v_s)):
        for n, arr in zip(["meta_tokens"] + small, _unpack(packed, shapes_s)):
            res[key, n] = arr

    outs = [loss, grad_x]
    for key in ("g", "d", "m", "v"):
        outs += [res[key, n] for n in names]
    return tuple(outs)
```

```python
import numpy as np
import jax
import jax.numpy as jnp
from jax import lax
from jax.experimental import pallas as pl
from jax.experimental.pallas import tpu as pltpu

F32 = jnp.float32
BF16 = jnp.bfloat16

N_META = 16
EPS = 1e-6
HEADS = 8
FOX_HD = 64
FOX_W = HEADS * FOX_HD
MLA_Q_RANK = 256
MLA_KV_RANK = 128
MLA_NOPE = 64
MLA_ROPE = 32
MLA_QK = MLA_NOPE + MLA_ROPE
MLA_V = 64
ROPE_THETA = 10000.0
LANES = 128
HALF = LANES // 2
META_BLK = 128
NEG = -1e30

ADAM_LR = 0.001
ADAM_B1 = 0.9
ADAM_B2 = 0.999
ADAM_EPS = 1e-08
ADAM_WD = 0.01
ADAM_STEP = 10

N_DEV = 8
AXES = ("x", "y", "c")
VMEM_LIMIT_BYTES = 56 * 1024 * 1024


def _tile(n, cap, mult):
    best = None
    for d in range(mult, min(n, cap) + 1, mult):
        if n % d == 0:
            best = d
    return n if best is None else best


def _row_tile(rows, width):
    return _tile(rows, max(16, (1 << 19) // width), 16)


def _params(sem=None):
    return pltpu.CompilerParams(dimension_semantics=sem, vmem_limit_bytes=VMEM_LIMIT_BYTES)


def _mm(a, b, mode, name, out_dtype=F32, scale=1.0, res=None):
    if mode == "nn":
        (M, K), (K2, N) = a.shape, b.shape
    elif mode == "nt":
        (M, K), (N, K2) = a.shape, b.shape
    else:
        (K, M), (K2, N) = a.shape, b.shape
    assert K == K2, (a.shape, b.shape, mode)
    if mode == "tn":
        tm, tk = _tile(M, 1408, 128), _tile(K, 640, 16)
    else:
        tm, tk = _tile(M, 640, 16), _tile(K, 1408, 128)
    tn = _tile(N, 1408, 128)
    nk = K // tk
    a_spec = {"nn": pl.BlockSpec((tm, tk), lambda i, j, k: (i, k)),
              "nt": pl.BlockSpec((tm, tk), lambda i, j, k: (i, k)),
              "tn": pl.BlockSpec((tk, tm), lambda i, j, k: (k, i))}[mode]
    b_spec = {"nn": pl.BlockSpec((tk, tn), lambda i, j, k: (k, j)),
              "nt": pl.BlockSpec((tn, tk), lambda i, j, k: (j, k)),
              "tn": pl.BlockSpec((tk, tn), lambda i, j, k: (k, j))}[mode]
    dims = {"nn": (((1,), (0,)), ((), ())), "nt": (((1,), (1,)), ((), ())),
            "tn": (((0,), (0,)), ((), ()))}[mode]
    o_spec = pl.BlockSpec((tm, tn), lambda i, j, k: (i, j))
    has_res = res is not None

    def body(*refs):
        if has_res:
            a_ref, b_ref, r_ref, o_ref, acc_ref = refs
        else:
            a_ref, b_ref, o_ref, acc_ref = refs
        k = pl.program_id(2)

        @pl.when(k == 0)
        def _():
            acc_ref[...] = jnp.zeros_like(acc_ref)

        acc_ref[...] += lax.dot_general(a_ref[...].astype(BF16), b_ref[...].astype(BF16), dims,
                                        preferred_element_type=F32)

        @pl.when(k == nk - 1)
        def _():
            o = acc_ref[...] * scale
            if has_res:
                o = o + r_ref[...]
            o_ref[...] = o.astype(out_dtype)

    ins = [a, b] + ([res] if has_res else [])
    specs = [a_spec, b_spec] + ([o_spec] if has_res else [])
    return pl.pallas_call(
        body, name=name, grid=(M // tm, N // tn, nk), in_specs=specs, out_specs=o_spec,
        out_shape=jax.ShapeDtypeStruct((M, N), out_dtype),
        scratch_shapes=[pltpu.VMEM((tm, tn), F32)],
        compiler_params=_params(("parallel", "parallel", "arbitrary")),
    )(*ins)


def _rope_fwd(y, c, s1, s2):
    return y * c + pltpu.roll(y, LANES - 16, 1) * s1 + pltpu.roll(y, 16, 1) * s2


def _rope_bwd(dy, c, s1, s2):
    return dy * c + pltpu.roll(dy * s1, 16, 1) + pltpu.roll(dy * s2, LANES - 16, 1)


def _group_sum(v, seg):
    if seg == v.shape[-1]:
        return jnp.sum(v, axis=-1, keepdims=True)
    lo = lax.broadcasted_iota(jnp.int32, v.shape, 1) < seg
    s_lo = jnp.sum(jnp.where(lo, v, 0.0), axis=-1, keepdims=True)
    s_hi = jnp.sum(jnp.where(lo, 0.0, v), axis=-1, keepdims=True)
    return jnp.where(lo, s_lo, s_hi)


def _norm_fwd(src, col0, width, bw, gain, seg, d_true, name, tabs=None, out_dtype=BF16):
    T = src.shape[0]
    tr = _row_tile(T, bw)
    inv_d = 1.0 / d_true
    c0 = col0 // bw
    assert col0 % bw == 0 and width % bw == 0

    def body(*refs):
        if tabs is None:
            x_ref, g_ref, o_ref = refs
        else:
            x_ref, g_ref, c_ref, s1_ref, s2_ref, o_ref = refs
        xv = x_ref[...]
        r = lax.rsqrt(_group_sum(xv * xv, seg) * inv_d + EPS)
        y = xv * r * g_ref[...]
        if tabs is not None:
            y = _rope_fwd(y, c_ref[...], s1_ref[...], s2_ref[...])
        o_ref[...] = y.astype(out_dtype)

    specs = [pl.BlockSpec((tr, bw), lambda i, j: (i, c0 + j)), pl.BlockSpec((1, bw), lambda i, j: (0, 0))]
    ins = [src, gain]
    if tabs is not None:
        tab = pl.BlockSpec((tr, LANES), lambda i, j: (i, 0))
        specs += [tab, tab, tab]
        ins += list(tabs)
    return pl.pallas_call(
        body, name=name, grid=(T // tr, width // bw), in_specs=specs,
        out_specs=pl.BlockSpec((tr, bw), lambda i, j: (i, j)),
        out_shape=jax.ShapeDtypeStruct((T, width), out_dtype),
        compiler_params=_params(("parallel", "parallel")),
    )(*ins)


def _norm_bwd_math(xv, gain, dyv, seg, inv_d):
    r = lax.rsqrt(_group_sum(xv * xv, seg) * inv_d + EPS)
    gy = dyv * gain
    dot = _group_sum(gy * xv, seg)
    dx = r * gy - xv * (r * r * r * inv_d) * dot
    return dx, jnp.sum(dyv * xv * r, axis=0, keepdims=True)


def _norm_bwd(src, col0, width, bw, gain, seg, d_true, dy, name, tabs=None, res=None, out_dtype=F32):
    T = src.shape[0]
    tr = _row_tile(T, bw)
    inv_d = 1.0 / d_true
    c0 = col0 // bw
    has_res = res is not None

    def body(*refs):
        refs = list(refs)
        x_ref, g_ref, dy_ref = refs[:3]
        pos = 3
        if tabs is not None:
            c_ref, s1_ref, s2_ref = refs[3:6]
            pos = 6
        if has_res:
            r_ref = refs[pos]
            pos += 1
        dx_ref, dg_ref = refs[pos], refs[pos + 1]
        dyv = dy_ref[...].astype(F32)
        if tabs is not None:
            dyv = _rope_bwd(dyv, c_ref[...], s1_ref[...], s2_ref[...])
        dx, dg = _norm_bwd_math(x_ref[...], g_ref[...], dyv, seg, inv_d)
        if has_res:
            dx = dx + r_ref[...]
        dx_ref[...] = dx.astype(out_dtype)

        @pl.when((pl.program_id(0) == 0) & (pl.program_id(1) == 0))
        def _():
            dg_ref[...] = jnp.zeros_like(dg_ref)

        dg_ref[...] += dg

    blk = pl.BlockSpec((tr, bw), lambda i, j: (i, j))
    one = pl.BlockSpec((1, bw), lambda i, j: (0, 0))
    specs = [pl.BlockSpec((tr, bw), lambda i, j: (i, c0 + j)), one, blk]
    ins = [src, gain, dy]
    if tabs is not None:
        tab = pl.BlockSpec((tr, LANES), lambda i, j: (i, 0))
        specs += [tab, tab, tab]
        ins += list(tabs)
    if has_res:
        specs.append(blk)
        ins.append(res)
    return pl.pallas_call(
        body, name=name, grid=(T // tr, width // bw), in_specs=specs, out_specs=(blk, one),
        out_shape=(jax.ShapeDtypeStruct((T, width), out_dtype), jax.ShapeDtypeStruct((1, bw), F32)),
        compiler_params=_params(("arbitrary", "arbitrary")),
    )(*ins)


def _mla_k_raw(kv, kr):
    lane = lax.broadcasted_iota(jnp.int32, kv.shape, 1)
    return jnp.where(lane < MLA_NOPE, kv, jnp.where(lane < MLA_QK, pltpu.roll(kr, MLA_NOPE, 1), 0.0))


def _mla_k_fwd(kv_lin, z, kr_col, gain, tabs, name):
    T, W = kv_lin.shape
    tr = _row_tile(T, LANES)
    krb = kr_col // LANES
    inv_d = 1.0 / MLA_QK

    def body(kv_ref, kr_ref, g_ref, c_ref, s1_ref, s2_ref, o_ref):
        xv = _mla_k_raw(kv_ref[...], kr_ref[...])
        r = lax.rsqrt(jnp.sum(xv * xv, axis=-1, keepdims=True) * inv_d + EPS)
        o_ref[...] = _rope_fwd(xv * r * g_ref[...], c_ref[...], s1_ref[...], s2_ref[...]).astype(BF16)

    blk = pl.BlockSpec((tr, LANES), lambda i, h: (i, h))
    tab = pl.BlockSpec((tr, LANES), lambda i, h: (i, 0))
    return pl.pallas_call(
        body, name=name, grid=(T // tr, W // LANES),
        in_specs=[blk, pl.BlockSpec((tr, LANES), lambda i, h: (i, krb)),
                  pl.BlockSpec((1, LANES), lambda i, h: (0, 0)), tab, tab, tab],
        out_specs=blk, out_shape=jax.ShapeDtypeStruct((T, W), BF16),
        compiler_params=_params(("parallel", "parallel")),
    )(kv_lin, z, gain, *tabs)


def _mla_k_bwd(kv_lin, z, kr_col, gain, tabs, dk, dvk, name):
    T, W = kv_lin.shape
    tr = _row_tile(T, LANES)
    krb = kr_col // LANES
    inv_d = 1.0 / MLA_QK

    def body(kv_ref, kr_ref, g_ref, c_ref, s1_ref, s2_ref, dk_ref, dvk_ref, dkv_ref, dkr_ref, dg_ref):
        h = pl.program_id(1)
        xv = _mla_k_raw(kv_ref[...], kr_ref[...])
        dyv = _rope_bwd(dk_ref[...], c_ref[...], s1_ref[...], s2_ref[...])
        dx, dg = _norm_bwd_math(xv, g_ref[...], dyv, LANES, inv_d)
        lane = lax.broadcasted_iota(jnp.int32, dx.shape, 1)
        dkv_ref[...] = jnp.where(lane < MLA_NOPE, dx, dvk_ref[...])
        part = pltpu.roll(jnp.where((lane >= MLA_NOPE) & (lane < MLA_QK), dx, 0.0), LANES - MLA_NOPE, 1)

        @pl.when(h == 0)
        def _():
            dkr_ref[...] = jnp.zeros_like(dkr_ref)

        dkr_ref[...] += part

        @pl.when((pl.program_id(0) == 0) & (h == 0))
        def _():
            dg_ref[...] = jnp.zeros_like(dg_ref)

        dg_ref[...] += dg

    blk = pl.BlockSpec((tr, LANES), lambda i, h: (i, h))
    tab = pl.BlockSpec((tr, LANES), lambda i, h: (i, 0))
    one = pl.BlockSpec((1, LANES), lambda i, h: (0, 0))
    return pl.pallas_call(
        body, name=name, grid=(T // tr, W // LANES),
        in_specs=[blk, pl.BlockSpec((tr, LANES), lambda i, h: (i, krb)), one, tab, tab, tab, blk, blk],
        out_specs=(blk, tab, one),
        out_shape=(jax.ShapeDtypeStruct((T, W), F32), jax.ShapeDtypeStruct((T, LANES), F32),
                   jax.ShapeDtypeStruct((1, LANES), F32)),
        compiler_params=_params(("arbitrary", "arbitrary")),
    )(kv_lin, z, gain, *tabs, dk, dvk)


def _ffn_up(u, w_gu, name, exch=None):
    T, D = u.shape
    F = w_gu.shape[1] // 2
    tm, tn = _tile(T, 640, 16), _tile(F, 1408, 128)
    nj = F // tn

    def body(u_ref, wg_ref, wu_ref, g_ref, up_ref, a_ref):
        uv = u_ref[...]
        g = jnp.dot(uv, wg_ref[...], preferred_element_type=F32)
        up = jnp.dot(uv, wu_ref[...], preferred_element_type=F32)
        g_ref[...] = g.astype(BF16)
        up_ref[...] = up.astype(BF16)
        a_ref[...] = (g * jax.nn.sigmoid(g) * up).astype(BF16)

    o_spec = pl.BlockSpec((tm, tn), lambda i, j: (i, j))
    sh = jax.ShapeDtypeStruct((T, F), BF16)
    return _call(
        body, name, (T // tm, nj),
        [pl.BlockSpec((tm, D), lambda i, j: (i, 0)),
         pl.BlockSpec((D, tn), lambda i, j: (0, j)),
         pl.BlockSpec((D, tn), lambda i, j: (0, j + nj))],
        [o_spec, o_spec, o_spec], [sh, sh, sh], [u, w_gu, w_gu],
        sem=("parallel", "parallel"), exch=exch)


def _ffn_down_bwd(dh, w_down, g, up, name, exch=None):
    T, D = dh.shape
    F = w_down.shape[0]
    tm, tn = _tile(T, 640, 16), _tile(F, 1408, 128)

    def body(dh_ref, w_ref, g_ref, up_ref, dg_ref, dup_ref):
        da = 0.5 * lax.dot_general(dh_ref[...].astype(BF16), w_ref[...], (((1,), (1,)), ((), ())),
                                   preferred_element_type=F32)
        gv = g_ref[...].astype(F32)
        sg = jax.nn.sigmoid(gv)
        silu = gv * sg
        dup_ref[...] = (da * silu).astype(BF16)
        dg_ref[...] = (da * up_ref[...].astype(F32) * (sg + silu * (1.0 - sg))).astype(BF16)

    t_spec = pl.BlockSpec((tm, tn), lambda i, j: (i, j))
    sh = jax.ShapeDtypeStruct((T, F), BF16)
    return _call(
        body, name, (T // tm, F // tn),
        [pl.BlockSpec((tm, D), lambda i, j: (i, 0)),
         pl.BlockSpec((tn, D), lambda i, j: (j, 0)), t_spec, t_spec],
        [t_spec, t_spec], [sh, sh], [dh, w_down, g, up],
        sem=("parallel", "parallel"), exch=exch)


def _ffn_up_bwd_dx(dg, dup, w_gu, name, exch=None):
    T, F = dg.shape
    D = w_gu.shape[0]
    tm, tk = _tile(T, 640, 16), _tile(F, 1408, 128)
    nk = F // tk
    nt = (((1,), (1,)), ((), ()))

    def body(dg_ref, dup_ref, wg_ref, wu_ref, o_ref, acc_ref):
        k = pl.program_id(1)

        @pl.when(k == 0)
        def _():
            acc_ref[...] = jnp.zeros_like(acc_ref)

        acc_ref[...] += (lax.dot_general(dg_ref[...], wg_ref[...], nt, preferred_element_type=F32)
                         + lax.dot_general(dup_ref[...], wu_ref[...], nt, preferred_element_type=F32))

        @pl.when(k == nk - 1)
        def _():
            o_ref[...] = acc_ref[...]

    return _call(
        body, name, (T // tm, nk),
        [pl.BlockSpec((tm, tk), lambda i, k: (i, k)),
         pl.BlockSpec((tm, tk), lambda i, k: (i, k)),
         pl.BlockSpec((D, tk), lambda i, k: (0, k)),
         pl.BlockSpec((D, tk), lambda i, k: (0, k + nk))],
        [pl.BlockSpec((tm, D), lambda i, k: (i, 0))], [jax.ShapeDtypeStruct((T, D), F32)],
        [dg, dup, w_gu, w_gu], scratch_shapes=[pltpu.VMEM((tm, D), F32)],
        sem=("parallel", "arbitrary"), exch=exch)


def _logsig(x):
    return jnp.minimum(x, 0.0) - jnp.log(1.0 + jnp.exp(-jnp.abs(x)))


def _cum_fwd(fl, flm, bf, name):
    B, S, _ = fl.shape
    nb = S // LANES

    def body(fl_ref, flm_ref, bf_ref, cum_ref, cumm_ref):
        rows = lax.broadcasted_iota(jnp.int32, (LANES, LANES), 0)
        cols = lax.broadcasted_iota(jnp.int32, (LANES, LANES), 1)
        tri = (rows >= cols).astype(F32)
        bias = bf_ref[...]
        lfm = jnp.where(rows < N_META, _logsig(flm_ref[...] + bias), 0.0)
        cm = jnp.dot(tri, lfm, precision=lax.Precision.HIGHEST, preferred_element_type=F32)
        cumm_ref[...] = cm
        base = cm[LANES - 1:LANES, :]
        for b in range(B):
            def blk(i, carry):
                r0 = pl.multiple_of(i * LANES, LANES)
                lf = _logsig(fl_ref[b, pl.ds(r0, LANES), :] + bias)
                c = jnp.dot(tri, lf, precision=lax.Precision.HIGHEST,
                            preferred_element_type=F32) + carry
                cum_ref[b, pl.ds(r0, LANES), :] = c
                return c[LANES - 1:LANES, :]

            lax.fori_loop(0, nb, blk, base)

    return pl.pallas_call(
        body, name=name,
        out_shape=(jax.ShapeDtypeStruct((B, S, LANES), F32),
                   jax.ShapeDtypeStruct((LANES, LANES), F32)),
        compiler_params=_params(),
    )(fl, flm, bf)


def _cum_bwd(dc, dcm, fl, flm, bf, name):
    B, S, _ = fl.shape
    nb = S // LANES

    def body(dc_ref, dcm_ref, fl_ref, flm_ref, bf_ref, dfl_ref, dflm_ref, dbf_ref):
        rows = lax.broadcasted_iota(jnp.int32, (LANES, LANES), 0)
        cols = lax.broadcasted_iota(jnp.int32, (LANES, LANES), 1)
        triu = (rows <= cols).astype(F32)
        bias = bf_ref[...]
        total = jnp.zeros((1, LANES), F32)
        dbf = jnp.zeros((1, LANES), F32)
        for b in range(B):
            tail = jnp.zeros((1, LANES), F32)
            for t in range(nb):
                r0 = (nb - 1 - t) * LANES
                rc = jnp.dot(triu, dc_ref[b, r0:r0 + LANES, :], precision=lax.Precision.HIGHEST,
                             preferred_element_type=F32) + tail
                xv = fl_ref[b, r0:r0 + LANES, :] + bias
                d = rc / (1.0 + jnp.exp(xv))
                dfl_ref[b, r0:r0 + LANES, :] = d
                tail = rc[0:1, :]
                dbf = dbf + jnp.sum(d, axis=0, keepdims=True)
            total = total + tail
        rcm = jnp.dot(triu, dcm_ref[...], precision=lax.Precision.HIGHEST,
                      preferred_element_type=F32) + total
        dm = jnp.where(rows < N_META, rcm / (1.0 + jnp.exp(flm_ref[...] + bias)), 0.0)
        dflm_ref[...] = dm
        dbf_ref[...] = dbf + jnp.sum(dm, axis=0, keepdims=True)

    return pl.pallas_call(
        body, name=name,
        out_shape=(jax.ShapeDtypeStruct((B, S, LANES), F32),
                   jax.ShapeDtypeStruct((LANES, LANES), F32),
                   jax.ShapeDtypeStruct((1, LANES), F32)),
        compiler_params=_params(),
    )(dc, dcm, fl, flm, bf)


_NT = (((1,), (1,)), ((), ()))


def _attn_specs(S, NX, qw, v_col0):
    mb = NX // META_BLK
    vb = v_col0 // qw
    return (pl.BlockSpec((S, qw), lambda b, p: (b, p)),
            pl.BlockSpec((META_BLK, qw), lambda b, p: (mb, p)),
            pl.BlockSpec((S, qw), lambda b, p: (b, vb + p)),
            pl.BlockSpec((META_BLK, qw), lambda b, p: (mb, vb + p)),
            pl.BlockSpec((S, LANES), lambda b, p: (b, p)))


def _cum_specs(S, TK):
    return [pl.BlockSpec((1, 2, S, 1), lambda b, p: (b, p, 0, 0)),
            pl.BlockSpec((1, 2, S // TK, 1, TK), lambda b, p: (b, p, 0, 0, 0)),
            pl.BlockSpec((1, 2, 1, META_BLK), lambda b, p: (b, p, 0, 0))]


def _attn_fwd(qn, kn, vsrc, v_col0, mla, scale, S, NX, name, cq=None, ck=None, cmk=None, exch=None):
    T = qn.shape[0]
    B = NX // S
    qw = 2 * LANES if mla else LANES
    npair = qn.shape[1] // qw
    TQ = min(512, S)
    TK = TQ
    forget = cq is not None

    def body(*refs):
        if forget:
            q_ref, k_ref, km_ref, v_ref, vm_ref, cq_ref, ck_ref, cmk_ref, _, o_ref, lse_ref = refs
        else:
            q_ref, k_ref, km_ref, v_ref, vm_ref, _, o_ref, lse_ref = refs
        lo = lax.broadcasted_iota(jnp.int32, (1, LANES), 1) < HALF
        mcol = lax.broadcasted_iota(jnp.int32, (TQ, META_BLK), 1)
        rr = lax.broadcasted_iota(jnp.int32, (TQ, TK), 0)
        cc = lax.broadcasted_iota(jnp.int32, (TQ, TK), 1)
        two = lax.broadcasted_iota(jnp.int32, (TQ, 2), 1)
        for qi in range(S // TQ):
            q0 = qi * TQ
            outs, lses = [], []
            for e in range(2):
                sl = slice(e * LANES, (e + 1) * LANES) if mla else slice(None)
                if mla:
                    qt = q_ref[q0:q0 + TQ, sl]
                else:
                    qt = jnp.where(lo if e == 0 else ~lo, q_ref[q0:q0 + TQ, :], 0.0).astype(BF16)
                s = lax.dot_general(qt, km_ref[:, sl], _NT, preferred_element_type=F32) * scale
                if forget:
                    cqt = cq_ref[0, e, q0:q0 + TQ, :]
                    s = s + cqt - cmk_ref[0, e]
                s = jnp.where(mcol < N_META, s, NEG)
                m = jnp.max(s, axis=1, keepdims=True)
                p = jnp.exp(s - m)
                l = jnp.sum(p, axis=1, keepdims=True)
                acc = jnp.dot(p.astype(BF16), vm_ref[:, sl].astype(BF16), preferred_element_type=F32)

                def kblock(kj, carry):
                    m, l, acc = carry
                    k0 = pl.multiple_of(kj * TK, TK)
                    kt = k_ref[pl.ds(k0, TK), sl]
                    vt = v_ref[pl.ds(k0, TK), sl].astype(BF16)
                    s = lax.dot_general(qt, kt, _NT, preferred_element_type=F32) * scale
                    if forget:
                        s = s + cqt - ck_ref[0, e, kj]
                    s = jnp.where(rr + q0 >= cc + k0, s, NEG)
                    m2 = jnp.maximum(m, jnp.max(s, axis=1, keepdims=True))
                    alpha = jnp.exp(m - m2)
                    p = jnp.exp(s - m2)
                    l2 = alpha * l + jnp.sum(p, axis=1, keepdims=True)
                    acc2 = alpha * acc + jnp.dot(p.astype(BF16), vt, preferred_element_type=F32)
                    return m2, l2, acc2

                m, l, acc = lax.fori_loop(0, (q0 + TQ) // TK, kblock, (m, l, acc))
                outs.append(acc / l)
                lses.append(m + jnp.log(l))
            first = pltpu.roll(outs[0], HALF, 1) if mla else outs[0]
            o_ref[q0:q0 + TQ, :] = jnp.where(lo, first, outs[1])
            lse_ref[0, 0, q0:q0 + TQ, :] = jnp.where(two == 0, lses[0], lses[1])

    qk, kmeta, vv, vmeta, pair = _attn_specs(S, NX, qw, v_col0)
    specs = [qk, qk, kmeta, vv, vmeta]
    ins = [qn, kn, kn, vsrc, vsrc]
    if forget:
        specs += _cum_specs(S, TK)
        ins += [cq, ck, cmk]
    specs.append(pl.BlockSpec(memory_space=pl.ANY))
    ins.append(jnp.zeros((T, npair * LANES), F32))
    lse_spec = pl.BlockSpec((1, 1, S, 2), lambda b, p: (b, p, 0, 0))
    return _call(
        body, name, (B, npair), specs, [pair, lse_spec],
        [jax.ShapeDtypeStruct((T, npair * LANES), F32), jax.ShapeDtypeStruct((B, npair, S, 2), F32)],
        ins, sem=("parallel", "parallel"), aliases={len(ins) - 1: 0}, exch=exch)


def _attn_bwd(qn, kn, vsrc, v_col0, o, lse, do, mla, scale, S, NX, name, cq=None, ck=None, cmk=None,
              exch=None):
    T, W = qn.shape
    B = NX // S
    qw = 2 * LANES if mla else LANES
    npair = W // qw
    TQ = min(512, S)
    TK = TQ
    forget = cq is not None

    def body(*refs):
        refs = list(refs)
        q_ref, k_ref, km_ref, v_ref, vm_ref, o_ref, do_ref, lse_ref = refs[:8]
        pos = 8
        if forget:
            cq_ref, ck_ref, cmk_ref = refs[8:11]
            pos = 11
        pos += 3
        dq_ref, dk_ref, dv_ref, dkm_ref, dvm_ref = refs[pos:pos + 5]
        if forget:
            dck_ref, dcm_ref = refs[pos + 5:pos + 7]
            dck_ref[...] = jnp.zeros_like(dck_ref)
            dcm_ref[...] = jnp.zeros_like(dcm_ref)
        dk_ref[...] = jnp.zeros_like(dk_ref)
        dv_ref[...] = jnp.zeros_like(dv_ref)
        dkm_ref[...] = jnp.zeros_like(dkm_ref)
        dvm_ref[...] = jnp.zeros_like(dvm_ref)
        lo = lax.broadcasted_iota(jnp.int32, (1, LANES), 1) < HALF
        mcol = lax.broadcasted_iota(jnp.int32, (TQ, META_BLK), 1)
        rr = lax.broadcasted_iota(jnp.int32, (TQ, TK), 0)
        cc = lax.broadcasted_iota(jnp.int32, (TQ, TK), 1)
        two = lax.broadcasted_iota(jnp.int32, (TQ, 2), 1)
        for qi in range(S // TQ):
            q0 = qi * TQ
            dof = do_ref[q0:q0 + TQ, :]
            prod = dof * o_ref[q0:q0 + TQ, :]
            lse2 = lse_ref[0, 0, q0:q0 + TQ, :]
            dqs = []
            for e in range(2):
                sl = slice(e * LANES, (e + 1) * LANES) if mla else slice(None)
                mine = lo if e == 0 else ~lo
                if mla:
                    qt = q_ref[q0:q0 + TQ, sl]
                    dot = jnp.where(lo, 0.0, pltpu.roll(dof, HALF, 1) if e == 0 else dof).astype(BF16)
                else:
                    qt = jnp.where(mine, q_ref[q0:q0 + TQ, :], 0.0).astype(BF16)
                    dot = jnp.where(mine, dof, 0.0).astype(BF16)
                delta = jnp.sum(jnp.where(mine, prod, 0.0), axis=1, keepdims=True)
                lse_t = jnp.sum(jnp.where(two == e, lse2, 0.0), axis=1, keepdims=True)
                km = km_ref[:, sl]
                vm = vm_ref[:, sl].astype(BF16)
                s = lax.dot_general(qt, km, _NT, preferred_element_type=F32) * scale
                if forget:
                    cqt = cq_ref[0, e, q0:q0 + TQ, :]
                    s = s + cqt - cmk_ref[0, e]
                p = jnp.where(mcol < N_META, jnp.exp(s - lse_t), 0.0)
                dp = lax.dot_general(dot, vm, _NT, preferred_element_type=F32)
                ds = p * (dp - delta)
                dq = jnp.dot(ds.astype(BF16), km, preferred_element_type=F32)
                dkm_ref[0, :, sl] += jnp.dot(ds.T.astype(BF16), qt, preferred_element_type=F32) * scale
                dvm_ref[0, :, sl] += jnp.dot(p.T.astype(BF16), dot, preferred_element_type=F32)
                if forget:
                    dcm_ref[0, e] += -jnp.sum(ds, axis=0, keepdims=True)

                def kblock(kj, dq):
                    k0 = pl.multiple_of(kj * TK, TK)
                    kt = k_ref[pl.ds(k0, TK), sl]
                    vt = v_ref[pl.ds(k0, TK), sl].astype(BF16)
                    s = lax.dot_general(qt, kt, _NT, preferred_element_type=F32) * scale
                    if forget:
                        s = s + cqt - ck_ref[0, e, kj]
                    p = jnp.where(rr + q0 >= cc + k0, jnp.exp(s - lse_t), 0.0)
                    dp = lax.dot_general(dot, vt, _NT, preferred_element_type=F32)
                    ds = p * (dp - delta)
                    dk_ref[pl.ds(k0, TK), sl] += jnp.dot(ds.T.astype(BF16), qt,
                                                         preferred_element_type=F32) * scale
                    dv_ref[pl.ds(k0, TK), sl] += jnp.dot(p.T.astype(BF16), dot,
                                                         preferred_element_type=F32)
                    if forget:
                        dck_ref[0, e, kj] += -jnp.sum(ds, axis=0, keepdims=True)
                    return dq + jnp.dot(ds.astype(BF16), kt, preferred_element_type=F32)

                dq = lax.fori_loop(0, (q0 + TQ) // TK, kblock, dq) * scale
                if mla:
                    dq_ref[q0:q0 + TQ, sl] = dq
                else:
                    dqs.append(dq)
            if not mla:
                dq_ref[q0:q0 + TQ, :] = jnp.where(lo, dqs[0], dqs[1])

    qk, kmeta, vv, vmeta, pair = _attn_specs(S, NX, qw, v_col0)
    lse_spec = pl.BlockSpec((1, 1, S, 2), lambda b, p: (b, p, 0, 0))
    specs = [qk, qk, kmeta, vv, vmeta, pair, pair, lse_spec]
    ins = [qn, kn, kn, vsrc, vsrc, o, do, lse]
    if forget:
        specs += _cum_specs(S, TK)
        ins += [cq, ck, cmk]
    first_alias = len(ins)
    specs += [pl.BlockSpec(memory_space=pl.ANY)] * 3
    ins += [jnp.zeros((T, W), F32)] * 3
    mspec = pl.BlockSpec((1, META_BLK, qw), lambda b, p: (b, 0, p))
    out_specs = [qk, qk, qk, mspec, mspec]
    tok = jax.ShapeDtypeStruct((T, W), F32)
    met = jax.ShapeDtypeStruct((B, META_BLK, W), F32)
    out_shape = [tok, tok, tok, met, met]
    if forget:
        out_specs += _cum_specs(S, TK)[1:]
        out_shape += [jax.ShapeDtypeStruct((B, HEADS, S // TK, 1, TK), F32),
                      jax.ShapeDtypeStruct((B, HEADS, 1, META_BLK), F32)]
    return _call(
        body, name, (B, npair), specs, out_specs, out_shape, ins, sem=("parallel", "parallel"),
        aliases={first_alias: 0, first_alias + 1: 1, first_alias + 2: 2}, exch=exch)


def _gate_fwd(z, bg, of, om, name):
    T, D = of.shape
    tm = _tile(T, 640, 16)

    def body(z_ref, bg_ref, of_ref, om_ref, o_ref):
        gt = jax.nn.sigmoid(z_ref[...] + bg_ref[...])
        o_ref[...] = (gt[:, :D] * of_ref[...] + gt[:, D:] * om_ref[...]).astype(BF16)

    row = pl.BlockSpec((tm, D), lambda i: (i, 0))
    return pl.pallas_call(
        body, name=name, grid=(T // tm,),
        in_specs=[pl.BlockSpec((tm, 2 * D), lambda i: (i, 0)),
                  pl.BlockSpec((1, 2 * D), lambda i: (0, 0)), row, row],
        out_specs=row, out_shape=jax.ShapeDtypeStruct((T, D), BF16),
        compiler_params=_params(("parallel",)),
    )(z, bg, of, om)


def _gate_bwd(dmix, z, bg, of, om, name):
    T, D = of.shape
    tm = _tile(T, 640, 16)

    def body(dm_ref, z_ref, bg_ref, of_ref, om_ref, dgl_ref, dof_ref, dom_ref, dbg_ref):
        gt = jax.nn.sigmoid(z_ref[...] + bg_ref[...])
        dm = dm_ref[...]
        dof_ref[...] = (dm * gt[:, :D]).astype(BF16)
        dom_ref[...] = (dm * gt[:, D:]).astype(BF16)
        dgl = jnp.concatenate([dm * of_ref[...], dm * om_ref[...]], axis=1) * gt * (1.0 - gt)
        dgl_ref[...] = dgl.astype(BF16)

        @pl.when(pl.program_id(0) == 0)
        def _():
            dbg_ref[...] = jnp.zeros_like(dbg_ref)

        dbg_ref[...] += jnp.sum(dgl, axis=0, keepdims=True)

    row = pl.BlockSpec((tm, D), lambda i: (i, 0))
    wide = pl.BlockSpec((tm, 2 * D), lambda i: (i, 0))
    one = pl.BlockSpec((1, 2 * D), lambda i: (0, 0))
    return pl.pallas_call(
        body, name=name, grid=(T // tm,),
        in_specs=[row, wide, one, row, row], out_specs=(wide, row, row, one),
        out_shape=(jax.ShapeDtypeStruct((T, 2 * D), BF16), jax.ShapeDtypeStruct((T, D), BF16),
                   jax.ShapeDtypeStruct((T, D), BF16), jax.ShapeDtypeStruct((1, 2 * D), F32)),
        compiler_params=_params(("arbitrary",)),
    )(dmix, z, bg, of, om)


def _loss(h, tgt, n_valid, name):
    T, D = h.shape
    tm = _tile(T, 640, 16)

    def body(h_ref, t_ref, dh_ref, l_ref):
        i = pl.program_id(0)
        rows = lax.broadcasted_iota(jnp.int32, (tm, D), 0) + i * tm
        err = jnp.where(rows < n_valid, h_ref[...] - t_ref[...], 0.0)
        dh_ref[...] = err * (1.0 / D)

        @pl.when(i == 0)
        def _():
            l_ref[...] = jnp.zeros_like(l_ref)

        l_ref[...] += 0.5 * jnp.sum(jnp.sum(err * err, axis=1, keepdims=True) * (1.0 / D))

    row = pl.BlockSpec((tm, D), lambda i: (i, 0))
    acc = pl.BlockSpec((8, LANES), lambda i: (0, 0))
    return pl.pallas_call(
        body, name=name, grid=(T // tm,), in_specs=[row, row], out_specs=(row, acc),
        out_shape=(jax.ShapeDtypeStruct((T, D), F32), jax.ShapeDtypeStruct((8, LANES), F32)),
        compiler_params=_params(("arbitrary",)),
    )(h, tgt)


def _adamw(parts, w, m, v, name):
    P, R, C = parts.shape
    tr = _tile(R, max(8, (1 << 18) // C), 8)
    bc1 = 1.0 - ADAM_B1 ** ADAM_STEP
    bc2 = 1.0 - ADAM_B2 ** ADAM_STEP

    def body(p_ref, w_ref, m_ref, v_ref, g_ref, d_ref, m2_ref, v2_ref):
        g = p_ref[0].astype(F32)
        for j in range(1, P):
            g = g + p_ref[j].astype(F32)
        m2 = ADAM_B1 * m_ref[...] + (1.0 - ADAM_B1) * g
        v2 = ADAM_B2 * v_ref[...] + (1.0 - ADAM_B2) * (g * g)
        m_hat = m2 / bc1
        v_hat = v2 / bc2
        g_ref[...] = g
        d_ref[...] = -ADAM_LR * (m_hat / (jnp.sqrt(v_hat) + ADAM_EPS) + ADAM_WD * w_ref[...])
        m2_ref[...] = m2
        v2_ref[...] = v2

    row = pl.BlockSpec((tr, C), lambda i: (i, 0))
    sh = jax.ShapeDtypeStruct((R, C), F32)
    return pl.pallas_call(
        body, name=name, grid=(R // tr,),
        in_specs=[pl.BlockSpec((P, tr, C), lambda i: (0, i, 0)), row, row, row],
        out_specs=(row, row, row, row), out_shape=(sh, sh, sh, sh),
        compiler_params=_params(("parallel",)),
    )(parts, w, m, v)


def _peer(d):
    x, y, c = lax.axis_index("x"), lax.axis_index("y"), lax.axis_index("c")
    px = 1 - x if d & 4 else x
    py = 1 - y if d & 2 else y
    pc = 1 - c if d & 1 else c
    return (px, py, pc), 4 * px + 2 * py + pc


class _Exchange:
    def __init__(self, srcs, gather):
        self.srcs, self.gather, self.n = list(srcs), gather, len(srcs)
        n = self.n
        hbm = pl.BlockSpec(memory_space=pl.ANY)
        self.in_specs = [hbm] * n
        self.out_specs = [hbm] * n
        self.out_shape = [jax.ShapeDtypeStruct((N_DEV,) + s.shape[-2:], s.dtype) for s in srcs]
        self.scratch = [pltpu.SemaphoreType.DMA((N_DEV - 1, n)), pltpu.SemaphoreType.DMA((N_DEV - 1, n)),
                        pltpu.SemaphoreType.DMA((n,))]

    def _copies(self, src_refs, out_refs, sems):
        send_sems, recv_sems, local_sems = sems
        _, me = _peer(0)

        def remote(w, d, landing):
            dev, lin = _peer(d)
            return pltpu.make_async_remote_copy(
                src_ref=src_refs[w] if self.gather else src_refs[w].at[lin],
                dst_ref=out_refs[w].at[lin if landing else me],
                send_sem=send_sems.at[d - 1, w], recv_sem=recv_sems.at[d - 1, w],
                device_id=dev, device_id_type=pl.DeviceIdType.MESH)

        pairs = [(w, d) for d in range(1, N_DEV) for w in range(self.n)]
        own = [pltpu.make_async_copy(src_refs[w] if self.gather else src_refs[w].at[me],
                                     out_refs[w].at[me], local_sems.at[w]) for w in range(self.n)]
        return own, [remote(w, d, False) for w, d in pairs], [remote(w, d, True) for w, d in pairs]

    def start(self, src_refs, out_refs, sems):
        own, sent, _ = self._copies(src_refs, out_refs, sems)
        for cp in own + sent:
            cp.start()

    def wait(self, src_refs, out_refs, sems):
        own, sent, landing = self._copies(src_refs, out_refs, sems)
        for cp in landing:
            cp.wait_recv()
        for cp in sent:
            cp.wait_send()
        for cp in own:
            cp.wait()


def _exchange(srcs, name, gather):
    ex = _Exchange(srcs, gather)
    n = ex.n

    def body(*refs):
        ex.start(refs[:n], refs[n:2 * n], refs[2 * n:])
        ex.wait(refs[:n], refs[n:2 * n], refs[2 * n:])

    outs = pl.pallas_call(
        body, name=name, in_specs=ex.in_specs, out_specs=tuple(ex.out_specs),
        out_shape=tuple(ex.out_shape), scratch_shapes=ex.scratch,
    )(*srcs)
    return list(outs)


def _call(body, name, grid, in_specs, out_specs, out_shape, ins, scratch_shapes=(), sem=None,
          aliases=None, exch=None):
    aliases = aliases or {}
    if exch is None:
        outs = pl.pallas_call(
            body, name=name, grid=grid, in_specs=list(in_specs), out_specs=tuple(out_specs),
            out_shape=tuple(out_shape), scratch_shapes=list(scratch_shapes),
            input_output_aliases=aliases, compiler_params=_params(sem),
        )(*ins)
        return list(outs), []
    ni, no, ns, n = len(in_specs), len(out_specs), len(scratch_shapes), exch.n
    last_ids = [g - 1 for g in grid]

    def hosted(*refs):
        cin, xin = refs[:ni], refs[ni:ni + n]
        cout, xout = refs[ni + n:ni + n + no], refs[ni + n + no:ni + 2 * n + no]
        cscr, xsem = refs[ni + 2 * n + no:ni + 2 * n + no + ns], refs[ni + 2 * n + no + ns:]
        ids = [pl.program_id(a) for a in range(len(grid))]
        first, last = ids[0] == 0, ids[0] == last_ids[0]
        for a in range(1, len(grid)):
            first, last = first & (ids[a] == 0), last & (ids[a] == last_ids[a])

        @pl.when(first)
        def _():
            exch.start(xin, xout, xsem)

        body(*cin, *cout, *cscr)

        @pl.when(last)
        def _():
            exch.wait(xin, xout, xsem)

    outs = pl.pallas_call(
        hosted, name=name, grid=grid, in_specs=list(in_specs) + exch.in_specs,
        out_specs=tuple(list(out_specs) + exch.out_specs),
        out_shape=tuple(list(out_shape) + exch.out_shape),
        scratch_shapes=list(scratch_shapes) + exch.scratch, input_output_aliases=aliases,
        compiler_params=_params(("arbitrary",) * len(grid)),
    )(*ins, *exch.srcs)
    return list(outs[:no]), list(outs[no:])


def _pack(arrs, cols, row_mult):
    flat = jnp.concatenate([a.reshape(-1) for a in arrs])
    n = flat.shape[0]
    quantum = cols * row_mult
    total = -(-n // quantum) * quantum
    return jnp.pad(flat, (0, total - n)).reshape(total // cols, cols)


def _pack_rows(arrs, cols, row_mult):
    flat = jnp.concatenate(arrs, axis=1)
    n = flat.shape[1]
    quantum = cols * row_mult
    total = -(-n // quantum) * quantum
    return jnp.pad(flat, ((0, 0), (0, total - n))).reshape(N_DEV, total // cols, cols)


def _unpack(packed, shapes):
    flat = packed.reshape(-1)
    out, off = [], 0
    for s in shapes:
        n = int(np.prod(s))
        out.append(flat[off:off + n].reshape(s))
        off += n
    return out


def _rope_tables(positions):
    inv_freq = ROPE_THETA ** (-jnp.arange(0, MLA_ROPE, 2, dtype=F32) / MLA_ROPE)
    ang = positions.astype(F32)[:, None] * inv_freq[None, :]
    cos, sin = jnp.cos(ang), jnp.sin(ang)
    n = positions.shape[0]
    ones, zeros = jnp.ones((n, MLA_NOPE), F32), jnp.zeros((n, MLA_NOPE), F32)
    tail1, tail0 = jnp.ones((n, LANES - MLA_QK), F32), jnp.zeros((n, LANES - MLA_QK), F32)
    z16 = jnp.zeros((n, 16), F32)
    c = jnp.concatenate([ones, cos, cos, tail1], axis=1)
    s1 = jnp.concatenate([zeros, -sin, z16, tail0], axis=1)
    s2 = jnp.concatenate([zeros, z16, sin, tail0], axis=1)
    return c, s1, s2


def kernel(x, meta_tokens, ffn1_norm, ffn1_w_gu, ffn1_w_down, mix_norm, w_in, b_forget, b_gate, fox_q_norm, fox_k_norm, mla_cq_norm, mla_w_uq, mla_ckv_norm, mla_w_ukv, mla_q_norm, mla_k_norm, w_branch_fox, w_branch_mla, w_out, ffn2_norm, ffn2_w_gu, ffn2_w_down, loss_target, m_meta_tokens, m_ffn1_norm, m_ffn1_w_gu, m_ffn1_w_down, m_mix_norm, m_w_in, m_b_forget, m_b_gate, m_fox_q_norm, m_fox_k_norm, m_mla_cq_norm, m_mla_w_uq, m_mla_ckv_norm, m_mla_w_ukv, m_mla_q_norm, m_mla_k_norm, m_w_branch_fox, m_w_branch_mla, m_w_out, m_ffn2_norm, m_ffn2_w_gu, m_ffn2_w_down, v_meta_tokens, v_ffn1_norm, v_ffn1_w_gu, v_ffn1_w_down, v_mix_norm, v_w_in, v_b_forget, v_b_gate, v_fox_q_norm, v_fox_k_norm, v_mla_cq_norm, v_mla_w_uq, v_mla_ckv_norm, v_mla_w_ukv, v_mla_q_norm, v_mla_k_norm, v_w_branch_fox, v_w_branch_mla, v_w_out, v_ffn2_norm, v_ffn2_w_gu, v_ffn2_w_down):
    names = ["meta_tokens", "ffn1_norm", "ffn1_w_gu", "ffn1_w_down", "mix_norm", "w_in", "b_forget",
             "b_gate", "fox_q_norm", "fox_k_norm", "mla_cq_norm", "mla_w_uq", "mla_ckv_norm",
             "mla_w_ukv", "mla_q_norm", "mla_k_norm", "w_branch_fox", "w_branch_mla", "w_out",
             "ffn2_norm", "ffn2_w_gu", "ffn2_w_down"]
    W = dict(zip(names, [meta_tokens, ffn1_norm, ffn1_w_gu, ffn1_w_down, mix_norm, w_in, b_forget,
                         b_gate, fox_q_norm, fox_k_norm, mla_cq_norm, mla_w_uq, mla_ckv_norm,
                         mla_w_ukv, mla_q_norm, mla_k_norm, w_branch_fox, w_branch_mla, w_out,
                         ffn2_norm, ffn2_w_gu, ffn2_w_down]))
    Mo = dict(zip(names, [m_meta_tokens, m_ffn1_norm, m_ffn1_w_gu, m_ffn1_w_down, m_mix_norm, m_w_in,
                          m_b_forget, m_b_gate, m_fox_q_norm, m_fox_k_norm, m_mla_cq_norm,
                          m_mla_w_uq, m_mla_ckv_norm, m_mla_w_ukv, m_mla_q_norm, m_mla_k_norm,
                          m_w_branch_fox, m_w_branch_mla, m_w_out, m_ffn2_norm, m_ffn2_w_gu,
                          m_ffn2_w_down]))
    Vo = dict(zip(names, [v_meta_tokens, v_ffn1_norm, v_ffn1_w_gu, v_ffn1_w_down, v_mix_norm, v_w_in,
                          v_b_forget, v_b_gate, v_fox_q_norm, v_fox_k_norm, v_mla_cq_norm,
                          v_mla_w_uq, v_mla_ckv_norm, v_mla_w_ukv, v_mla_q_norm, v_mla_k_norm,
                          v_w_branch_fox, v_w_branch_mla, v_w_out, v_ffn2_norm, v_ffn2_w_gu,
                          v_ffn2_w_down]))

    B, S, D = x.shape
    NX = B * S
    T = NX + META_BLK
    H = HEADS
    assert NX % META_BLK == 0 and S % LANES == 0
    me = 4 * lax.axis_index("x") + 2 * lax.axis_index("y") + lax.axis_index("c")

    big = [("ffn1_w_gu", 1), ("ffn1_w_down", 0), ("w_in", 1), ("mla_w_uq", 1), ("mla_w_ukv", 1),
           ("w_branch_fox", 1), ("w_branch_mla", 1), ("w_out", 0), ("ffn2_w_gu", 1), ("ffn2_w_down", 0)]
    first_group = ["ffn1_w_gu", "ffn1_w_down"]
    mix_group = ["w_in", "mla_w_uq", "mla_w_ukv", "w_branch_fox", "w_branch_mla", "w_out"]
    last_group = ["ffn2_w_gu", "ffn2_w_down"]
    axis_of = dict(big)
    full = {}

    def shards(group):
        return [W[n][0].astype(BF16) for n in group]

    def assemble(group, blks):
        for n, blk in zip(group, blks):
            _, r, c = blk.shape
            full[n] = (blk.transpose(1, 0, 2).reshape(r, N_DEV * c) if axis_of[n] == 1
                       else blk.reshape(N_DEV * r, c))

    got = _exchange(shards(first_group) + [meta_tokens], "gather_first", gather=True)
    assemble(first_group, got[:2])
    meta_full = got[2].transpose(1, 0, 2).reshape(N_META, D)

    Z_G, Z_FQ = 0, 2 * D
    Z_FK, Z_FV = Z_FQ + FOX_W, Z_FQ + 2 * FOX_W
    Z_CQ = Z_FQ + 3 * FOX_W
    Z_CKV = Z_CQ + MLA_Q_RANK
    Z_F = Z_CKV + MLA_KV_RANK
    Z_KR = Z_F + LANES

    def pad_lanes(a, w=LANES):
        return jnp.pad(a, [(0, 0)] * (a.ndim - 1) + [(0, w - a.shape[-1])])

    def rows_T(real, meta=None):
        n = real.shape[1]
        parts = [real]
        used = 0
        if meta is not None:
            parts.append(meta)
            used = meta.shape[0]
        if T - NX - used:
            parts.append(jnp.zeros((T - NX - used, n), real.dtype))
        return jnp.concatenate(parts, axis=0)

    def put_meta(tok, meta_per_seq):
        return lax.dynamic_update_slice(tok, meta_per_seq.sum(0), (NX, 0))

    h0 = rows_T(x.reshape(NX, D), meta_full)
    tgt = rows_T(loss_target.reshape(NX, D))

    def ffn_fwd(h, norm, w_gu, w_down, tag, exch=None):
        u = _norm_fwd(h, 0, D, D, norm, D, D, tag + "_norm")
        (g, up, a), carried = _ffn_up(u, w_gu, tag + "_up", exch=exch)
        h_out = _mm(a, w_down, "nn", tag + "_down", scale=0.5, res=h)
        return h_out, (u, g, up, a), carried

    h1, ffn1_saved, got = ffn_fwd(h0, W["ffn1_norm"], full["ffn1_w_gu"], full["ffn1_w_down"], "ffn1",
                                  exch=_Exchange(shards(mix_group), True))
    assemble(mix_group, got)
    wi = full["w_in"]
    o_fq = 0
    o_f = 3 * FOX_W
    o_cq = o_f + HEADS
    o_kr = o_cq + MLA_Q_RANK + MLA_KV_RANK
    o_g = o_kr + MLA_ROPE
    w_in_p = jnp.concatenate([
        wi[:, o_g:o_g + 2 * D], wi[:, o_fq:o_f], wi[:, o_cq:o_kr],
        jnp.pad(wi[:, o_f:o_cq], ((0, 0), (0, LANES - HEADS))),
        jnp.pad(wi[:, o_kr:o_g], ((0, 0), (0, LANES - MLA_ROPE)))], axis=1)
    w_uq_p = jnp.pad(full["mla_w_uq"].reshape(MLA_Q_RANK, H, MLA_QK),
                     ((0, 0), (0, 0), (0, LANES - MLA_QK))).reshape(MLA_Q_RANK, H * LANES)

    u2 = _norm_fwd(h1, 0, D, D, W["mix_norm"], D, D, "mix_norm")
    z = _mm(u2, w_in_p, "nn", "w_in")

    gq_f = jnp.tile(W["fox_q_norm"], (1, 2))
    gk_f = jnp.tile(W["fox_k_norm"], (1, 2))
    fqn = _norm_fwd(z, Z_FQ, FOX_W, LANES, gq_f, FOX_HD, FOX_HD, "fox_q_norm")
    fkn = _norm_fwd(z, Z_FK, FOX_W, LANES, gk_f, FOX_HD, FOX_HD, "fox_k_norm")
    fl = z[:NX, Z_F:Z_F + LANES].reshape(B, S, LANES)
    flm = z[NX:, Z_F:Z_F + LANES]
    bf = pad_lanes(W["b_forget"])
    cum, cumm = _cum_fwd(fl, flm, bf, "forget_cum")
    TK = min(512, S)
    cum_h = cum[:, :, :H].transpose(0, 2, 1)
    cq = cum_h[..., None]
    ck = cum_h.reshape(B, H, S // TK, 1, TK)
    cmk = jnp.broadcast_to(cumm[:, :H].T[None, :, None, :], (B, H, 1, META_BLK))
    (o_fox, lse_fox), got = _attn_fwd(fqn, fkn, z, Z_FV, False, FOX_HD ** -0.5, S, NX, "fox_attn", cq, ck, cmk,
                                      exch=_Exchange(shards(last_group), True))
    assemble(last_group, got)
    of = _mm(o_fox, full["w_branch_fox"], "nn", "branch_fox")

    pos = jnp.concatenate([jnp.tile(jnp.arange(S) + N_META, B), jnp.arange(META_BLK)])
    tabs = _rope_tables(pos)
    cqn = _norm_fwd(z, Z_CQ, MLA_Q_RANK, MLA_Q_RANK, W["mla_cq_norm"], MLA_Q_RANK, MLA_Q_RANK, "mla_cq_norm")
    q_lin = _mm(cqn, w_uq_p, "nn", "mla_uq")
    ckvn = _norm_fwd(z, Z_CKV, MLA_KV_RANK, MLA_KV_RANK, W["mla_ckv_norm"], MLA_KV_RANK, MLA_KV_RANK,
                     "mla_ckv_norm")
    kv_lin = _mm(ckvn, full["mla_w_ukv"], "nn", "mla_ukv")
    gq_m, gk_m = pad_lanes(W["mla_q_norm"]), pad_lanes(W["mla_k_norm"])
    mqn = _norm_fwd(q_lin, 0, H * LANES, LANES, gq_m, LANES, MLA_QK, "mla_q_norm", tabs=tabs)
    mkn = _mla_k_fwd(kv_lin, z, Z_KR, gk_m, tabs, "mla_k_norm")
    (o_mla, lse_mla), _ = _attn_fwd(mqn, mkn, kv_lin, 0, True, MLA_QK ** -0.5, S, NX, "mla_attn")
    om = _mm(o_mla, full["w_branch_mla"], "nn", "branch_mla")

    mix = _gate_fwd(z, W["b_gate"], of, om, "gate_mix")
    h2 = _mm(mix, full["w_out"], "nn", "w_out", res=h1)

    h3, ffn2_saved, _ = ffn_fwd(h2, W["ffn2_norm"], full["ffn2_w_gu"], full["ffn2_w_down"], "ffn2")

    dh3, loss_acc = _loss(h3, tgt, NX, "loss")
    loss = lax.psum(loss_acc[0, 0], AXES)

    G = {}
    parts = {}

    def scatter_of(group):
        per_dest = []
        for n in group:
            r, c = W[n].shape[1:]
            per_dest.append((G[n].reshape(r, N_DEV, c).transpose(1, 0, 2) if axis_of[n] == 1
                             else G[n].reshape(N_DEV, r, c)).astype(BF16))
        return _Exchange(per_dest, False)

    def ffn_bwd(dh, h, norm, w_gu, w_down, saved, tag, behind_down=None, behind_up=None):
        u, g, up, a = saved
        G[tag + "_w_down"] = _mm(a, dh, "tn", tag + "_dw_down", scale=0.5)
        (dg, dup), got = _ffn_down_bwd(dh, w_down, g, up, tag + "_down_bwd",
                                       exch=scatter_of(behind_down) if behind_down else None)
        parts.update(zip(behind_down or [], got))
        G[tag + "_w_gu"] = jnp.concatenate([_mm(u, dg, "tn", tag + "_dw_g"),
                                            _mm(u, dup, "tn", tag + "_dw_u")], axis=1)
        (du,), got = _ffn_up_bwd_dx(dg, dup, w_gu, tag + "_up_bwd",
                                    exch=scatter_of(behind_up) if behind_up else None)
        parts.update(zip(behind_up or [], got))
        dh_in, G[tag + "_norm"] = _norm_bwd(h, 0, D, D, norm, D, D, du, tag + "_norm_bwd", res=dh)
        return dh_in

    dh2 = ffn_bwd(dh3, h2, W["ffn2_norm"], full["ffn2_w_gu"], full["ffn2_w_down"], ffn2_saved, "ffn2")

    G["w_out"] = _mm(mix, dh2, "tn", "dw_out")
    dmix = _mm(dh2, full["w_out"], "nt", "w_out_bwd")
    dgl, dof, dom, G["b_gate"] = _gate_bwd(dmix, z, W["b_gate"], of, om, "gate_bwd")

    G["w_branch_fox"] = _mm(o_fox, dof, "tn", "dw_branch_fox")
    do_fox = _mm(dof, full["w_branch_fox"], "nt", "branch_fox_bwd")
    (dq_f, dk_f, dv_f, dkm_f, dvm_f, dck, dcmk), got = _attn_bwd(
        fqn, fkn, z, Z_FV, o_fox, lse_fox, do_fox, False, FOX_HD ** -0.5, S, NX, "fox_attn_bwd", cq, ck, cmk,
        exch=scatter_of(last_group))
    parts.update(zip(last_group, got))
    dk_f, dv_f = put_meta(dk_f, dkm_f), put_meta(dv_f, dvm_f)
    dfq, gq = _norm_bwd(z, Z_FQ, FOX_W, LANES, gq_f, FOX_HD, FOX_HD, dq_f, "fox_q_norm_bwd", out_dtype=BF16)
    dfk, gk = _norm_bwd(z, Z_FK, FOX_W, LANES, gk_f, FOX_HD, FOX_HD, dk_f, "fox_k_norm_bwd", out_dtype=BF16)
    G["fox_q_norm"] = gq[:, :FOX_HD] + gq[:, FOX_HD:]
    G["fox_k_norm"] = gk[:, :FOX_HD] + gk[:, FOX_HD:]
    dc = pad_lanes(dck.reshape(B, H, S).transpose(0, 2, 1))
    dcm = pad_lanes(dcmk.sum(0)[:, 0, :].T)
    dcm = jnp.where(jnp.arange(LANES)[:, None] < N_META, dcm, 0.0)
    dfl, dflm, dbf = _cum_bwd(dc, dcm, fl, flm, bf, "forget_cum_bwd")
    G["b_forget"] = dbf[:, :HEADS]
    dfl_t = rows_T(dfl.reshape(NX, LANES), dflm)

    G["w_branch_mla"] = _mm(o_mla, dom, "tn", "dw_branch_mla")
    do_mla = _mm(dom, full["w_branch_mla"], "nt", "branch_mla_bwd")
    (dq_m, dk_m, dvk, dkm_m, dvkm), _ = _attn_bwd(
        mqn, mkn, kv_lin, 0, o_mla, lse_mla, do_mla, True, MLA_QK ** -0.5, S, NX, "mla_attn_bwd")
    dk_m, dvk = put_meta(dk_m, dkm_m), put_meta(dvk, dvkm)
    dq_lin, gq = _norm_bwd(q_lin, 0, H * LANES, LANES, gq_m, LANES, MLA_QK, dq_m, "mla_q_norm_bwd", tabs=tabs)
    G["mla_q_norm"] = gq[:, :MLA_QK]
    G["mla_w_uq"] = _mm(cqn, dq_lin, "tn", "dw_uq").reshape(MLA_Q_RANK, H, LANES)[:, :, :MLA_QK].reshape(
        MLA_Q_RANK, H * MLA_QK)
    dcqn = _mm(dq_lin, w_uq_p, "nt", "mla_uq_bwd")
    dcq, G["mla_cq_norm"] = _norm_bwd(z, Z_CQ, MLA_Q_RANK, MLA_Q_RANK, W["mla_cq_norm"], MLA_Q_RANK,
                                      MLA_Q_RANK, dcqn, "mla_cq_norm_bwd", out_dtype=BF16)
    dkv_lin, dkr, gk = _mla_k_bwd(kv_lin, z, Z_KR, gk_m, tabs, dk_m, dvk, "mla_k_norm_bwd")
    G["mla_k_norm"] = gk[:, :MLA_QK]
    G["mla_w_ukv"] = _mm(ckvn, dkv_lin, "tn", "dw_ukv")
    dckvn = _mm(dkv_lin, full["mla_w_ukv"], "nt", "mla_ukv_bwd")
    dckv, G["mla_ckv_norm"] = _norm_bwd(z, Z_CKV, MLA_KV_RANK, MLA_KV_RANK, W["mla_ckv_norm"], MLA_KV_RANK,
                                        MLA_KV_RANK, dckvn, "mla_ckv_norm_bwd", out_dtype=BF16)

    dz = jnp.concatenate([dgl, dfq, dfk, dv_f.astype(BF16), dcq, dckv, dfl_t.astype(BF16),
                          dkr.astype(BF16)], axis=1)
    dw_in_p = _mm(u2, dz, "tn", "dw_in")
    G["w_in"] = jnp.concatenate([
        dw_in_p[:, Z_FQ:Z_CQ], dw_in_p[:, Z_F:Z_F + HEADS], dw_in_p[:, Z_CQ:Z_F],
        dw_in_p[:, Z_KR:Z_KR + MLA_ROPE], dw_in_p[:, Z_G:Z_G + 2 * D]], axis=1)
    du2 = _mm(dz, w_in_p, "nt", "w_in_bwd")
    dh1, G["mix_norm"] = _norm_bwd(h1, 0, D, D, W["mix_norm"], D, D, du2, "mix_norm_bwd", res=dh2)

    dh0 = ffn_bwd(dh1, h0, W["ffn1_norm"], full["ffn1_w_gu"], full["ffn1_w_down"], ffn1_saved, "ffn1",
                  behind_down=mix_group, behind_up=first_group)
    grad_x = dh0[:NX].reshape(B, S, D)
    G["meta_tokens"] = dh0[NX:NX + N_META]

    res = {}
    for n, _ in big:
        outs4 = _adamw(parts[n], W[n][0], Mo[n][0], Vo[n][0], "adamw_" + n)
        for key, arr in zip(("g", "d", "m", "v"), outs4):
            res[key, n] = arr[None]

    small = [n for n in names if n not in dict(big) and n != "meta_tokens"]
    small_shapes = [W[n].shape for n in small]
    spack = _pack([G["meta_tokens"]] + [G[n] for n in small], 1024, 8)
    (sparts,) = _exchange([spack], "gather_small_grads", gather=True)
    sflat = sparts.reshape(N_DEV, -1)
    dsh = D // N_DEV
    meta_part = lax.dynamic_slice(sflat[:, :N_META * D].reshape(N_DEV, N_META, D),
                                  (0, 0, me * dsh), (N_DEV, N_META, dsh)).reshape(N_DEV, -1)
    rep_len = sum(int(np.prod(s)) for s in small_shapes)
    rep_part = sflat[:, N_META * D:N_META * D + rep_len]
    sp = _pack_rows([meta_part, rep_part], LANES, 8)
    pks = lambda src: _pack([src["meta_tokens"]] + [src[n] for n in small], LANES, 8)
    g_s, d_s, m_s, v_s = _adamw(sp, pks(W), pks(Mo), pks(Vo), "adamw_small")
    shapes_s = [W["meta_tokens"].shape] + small_shapes
    for key, packed in (("g", g_s), ("d", d_s), ("m", m_s), ("v", v_s)):
        for n, arr in zip(["meta_tokens"] + small, _unpack(packed, shapes_s)):
            res[key, n] = arr

    outs = [loss, grad_x]
    for key in ("g", "d", "m", "v"):
        outs += [res[key, n] for n in names]
    return tuple(outs)
```

```python
import numpy as np
import jax
import jax.numpy as jnp
from jax import lax
from jax.experimental import pallas as pl
from jax.experimental.pallas import tpu as pltpu

F32 = jnp.float32
BF16 = jnp.bfloat16

N_META = 16
EPS = 1e-6
HEADS = 8
FOX_HD = 64
FOX_W = HEADS * FOX_HD
MLA_Q_RANK = 256
MLA_KV_RANK = 128
MLA_NOPE = 64
MLA_ROPE = 32
MLA_QK = MLA_NOPE + MLA_ROPE
MLA_V = 64
ROPE_THETA = 10000.0
LANES = 128
HALF = LANES // 2
META_BLK = 128
NEG = -1e30

ADAM_LR = 0.001
ADAM_B1 = 0.9
ADAM_B2 = 0.999
ADAM_EPS = 1e-08
ADAM_WD = 0.01
ADAM_STEP = 10

N_DEV = 8
AXES = ("x", "y", "c")
VMEM_LIMIT_BYTES = 56 * 1024 * 1024


def _tile(n, cap, mult):
    best = None
    for d in range(mult, min(n, cap) + 1, mult):
        if n % d == 0:
            best = d
    return n if best is None else best


def _row_tile(rows, width):
    return _tile(rows, max(16, (1 << 19) // width), 16)


def _params(sem=None):
    return pltpu.CompilerParams(dimension_semantics=sem, vmem_limit_bytes=VMEM_LIMIT_BYTES)


def _mm(a, b, mode, name, out_dtype=F32, scale=1.0, res=None):
    if mode == "nn":
        (M, K), (K2, N) = a.shape, b.shape
    elif mode == "nt":
        (M, K), (N, K2) = a.shape, b.shape
    else:
        (K, M), (K2, N) = a.shape, b.shape
    assert K == K2, (a.shape, b.shape, mode)
    if mode == "tn":
        tm, tk = _tile(M, 1408, 128), _tile(K, 640, 16)
    else:
        tm, tk = _tile(M, 640, 16), _tile(K, 1408, 128)
    tn = _tile(N, 1408, 128)
    nk = K // tk
    a_spec = {"nn": pl.BlockSpec((tm, tk), lambda i, j, k: (i, k)),
              "nt": pl.BlockSpec((tm, tk), lambda i, j, k: (i, k)),
              "tn": pl.BlockSpec((tk, tm), lambda i, j, k: (k, i))}[mode]
    b_spec = {"nn": pl.BlockSpec((tk, tn), lambda i, j, k: (k, j)),
              "nt": pl.BlockSpec((tn, tk), lambda i, j, k: (j, k)),
              "tn": pl.BlockSpec((tk, tn), lambda i, j, k: (k, j))}[mode]
    dims = {"nn": (((1,), (0,)), ((), ())), "nt": (((1,), (1,)), ((), ())),
            "tn": (((0,), (0,)), ((), ()))}[mode]
    o_spec = pl.BlockSpec((tm, tn), lambda i, j, k: (i, j))
    has_res = res is not None

    def body(*refs):
        if has_res:
            a_ref, b_ref, r_ref, o_ref, acc_ref = refs
        else:
            a_ref, b_ref, o_ref, acc_ref = refs
        k = pl.program_id(2)

        @pl.when(k == 0)
        def _():
            acc_ref[...] = jnp.zeros_like(acc_ref)

        acc_ref[...] += lax.dot_general(a_ref[...].astype(BF16), b_ref[...].astype(BF16), dims,
                                        preferred_element_type=F32)

        @pl.when(k == nk - 1)
        def _():
            o = acc_ref[...] * scale
            if has_res:
                o = o + r_ref[...]
            o_ref[...] = o.astype(out_dtype)

    ins = [a, b] + ([res] if has_res else [])
    specs = [a_spec, b_spec] + ([o_spec] if has_res else [])
    return pl.pallas_call(
        body, name=name, grid=(M // tm, N // tn, nk), in_specs=specs, out_specs=o_spec,
        out_shape=jax.ShapeDtypeStruct((M, N), out_dtype),
        scratch_shapes=[pltpu.VMEM((tm, tn), F32)],
        compiler_params=_params(("parallel", "parallel", "arbitrary")),
    )(*ins)


def _rope_fwd(y, c, s1, s2):
    return y * c + pltpu.roll(y, LANES - 16, 1) * s1 + pltpu.roll(y, 16, 1) * s2


def _rope_bwd(dy, c, s1, s2):
    return dy * c + pltpu.roll(dy * s1, 16, 1) + pltpu.roll(dy * s2, LANES - 16, 1)


def _group_sum(v, seg):
    if seg == v.shape[-1]:
        return jnp.sum(v, axis=-1, keepdims=True)
    lo = lax.broadcasted_iota(jnp.int32, v.shape, 1) < seg
    s_lo = jnp.sum(jnp.where(lo, v, 0.0), axis=-1, keepdims=True)
    s_hi = jnp.sum(jnp.where(lo, 0.0, v), axis=-1, keepdims=True)
    return jnp.where(lo, s_lo, s_hi)


def _norm_fwd(src, col0, width, bw, gain, seg, d_true, name, tabs=None, out_dtype=BF16):
    T = src.shape[0]
    tr = _row_tile(T, bw)
    inv_d = 1.0 / d_true
    c0 = col0 // bw
    assert col0 % bw == 0 and width % bw == 0

    def body(*refs):
        if tabs is None:
            x_ref, g_ref, o_ref = refs
        else:
            x_ref, g_ref, c_ref, s1_ref, s2_ref, o_ref = refs
        xv = x_ref[...]
        r = lax.rsqrt(_group_sum(xv * xv, seg) * inv_d + EPS)
        y = xv * r * g_ref[...]
        if tabs is not None:
            y = _rope_fwd(y, c_ref[...], s1_ref[...], s2_ref[...])
        o_ref[...] = y.astype(out_dtype)

    specs = [pl.BlockSpec((tr, bw), lambda i, j: (i, c0 + j)), pl.BlockSpec((1, bw), lambda i, j: (0, 0))]
    ins = [src, gain]
    if tabs is not None:
        tab = pl.BlockSpec((tr, LANES), lambda i, j: (i, 0))
        specs += [tab, tab, tab]
        ins += list(tabs)
    return pl.pallas_call(
        body, name=name, grid=(T // tr, width // bw), in_specs=specs,
        out_specs=pl.BlockSpec((tr, bw), lambda i, j: (i, j)),
        out_shape=jax.ShapeDtypeStruct((T, width), out_dtype),
        compiler_params=_params(("parallel", "parallel")),
    )(*ins)


def _norm_bwd_math(xv, gain, dyv, seg, inv_d):
    r = lax.rsqrt(_group_sum(xv * xv, seg) * inv_d + EPS)
    gy = dyv * gain
    dot = _group_sum(gy * xv, seg)
    dx = r * gy - xv * (r * r * r * inv_d) * dot
    return dx, jnp.sum(dyv * xv * r, axis=0, keepdims=True)


def _norm_bwd(src, col0, width, bw, gain, seg, d_true, dy, name, tabs=None, res=None, out_dtype=F32):
    T = src.shape[0]
    tr = _row_tile(T, bw)
    inv_d = 1.0 / d_true
    c0 = col0 // bw
    has_res = res is not None

    def body(*refs):
        refs = list(refs)
        x_ref, g_ref, dy_ref = refs[:3]
        pos = 3
        if tabs is not None:
            c_ref, s1_ref, s2_ref = refs[3:6]
            pos = 6
        if has_res:
            r_ref = refs[pos]
            pos += 1
        dx_ref, dg_ref = refs[pos], refs[pos + 1]
        dyv = dy_ref[...].astype(F32)
        if tabs is not None:
            dyv = _rope_bwd(dyv, c_ref[...], s1_ref[...], s2_ref[...])
        dx, dg = _norm_bwd_math(x_ref[...], g_ref[...], dyv, seg, inv_d)
        if has_res:
            dx = dx + r_ref[...]
        dx_ref[...] = dx.astype(out_dtype)

        @pl.when((pl.program_id(0) == 0) & (pl.program_id(1) == 0))
        def _():
            dg_ref[...] = jnp.zeros_like(dg_ref)

        dg_ref[...] += dg

    blk = pl.BlockSpec((tr, bw), lambda i, j: (i, j))
    one = pl.BlockSpec((1, bw), lambda i, j: (0, 0))
    specs = [pl.BlockSpec((tr, bw), lambda i, j: (i, c0 + j)), one, blk]
    ins = [src, gain, dy]
    if tabs is not None:
        tab = pl.BlockSpec((tr, LANES), lambda i, j: (i, 0))
        specs += [tab, tab, tab]
        ins += list(tabs)
    if has_res:
        specs.append(blk)
        ins.append(res)
    return pl.pallas_call(
        body, name=name, grid=(T // tr, width // bw), in_specs=specs, out_specs=(blk, one),
        out_shape=(jax.ShapeDtypeStruct((T, width), out_dtype), jax.ShapeDtypeStruct((1, bw), F32)),
        compiler_params=_params(("arbitrary", "arbitrary")),
    )(*ins)


def _mla_k_raw(kv, kr):
    lane = lax.broadcasted_iota(jnp.int32, kv.shape, 1)
    return jnp.where(lane < MLA_NOPE, kv, jnp.where(lane < MLA_QK, pltpu.roll(kr, MLA_NOPE, 1), 0.0))


def _mla_k_fwd(kv_lin, z, kr_col, gain, tabs, name):
    T, W = kv_lin.shape
    tr = _row_tile(T, LANES)
    krb = kr_col // LANES
    inv_d = 1.0 / MLA_QK

    def body(kv_ref, kr_ref, g_ref, c_ref, s1_ref, s2_ref, o_ref):
        xv = _mla_k_raw(kv_ref[...], kr_ref[...])
        r = lax.rsqrt(jnp.sum(xv * xv, axis=-1, keepdims=True) * inv_d + EPS)
        o_ref[...] = _rope_fwd(xv * r * g_ref[...], c_ref[...], s1_ref[...], s2_ref[...]).astype(BF16)

    blk = pl.BlockSpec((tr, LANES), lambda i, h: (i, h))
    tab = pl.BlockSpec((tr, LANES), lambda i, h: (i, 0))
    return pl.pallas_call(
        body, name=name, grid=(T // tr, W // LANES),
        in_specs=[blk, pl.BlockSpec((tr, LANES), lambda i, h: (i, krb)),
                  pl.BlockSpec((1, LANES), lambda i, h: (0, 0)), tab, tab, tab],
        out_specs=blk, out_shape=jax.ShapeDtypeStruct((T, W), BF16),
        compiler_params=_params(("parallel", "parallel")),
    )(kv_lin, z, gain, *tabs)


def _mla_k_bwd(kv_lin, z, kr_col, gain, tabs, dk, dvk, name):
    T, W = kv_lin.shape
    tr = _row_tile(T, LANES)
    krb = kr_col // LANES
    inv_d = 1.0 / MLA_QK

    def body(kv_ref, kr_ref, g_ref, c_ref, s1_ref, s2_ref, dk_ref, dvk_ref, dkv_ref, dkr_ref, dg_ref):
        h = pl.program_id(1)
        xv = _mla_k_raw(kv_ref[...], kr_ref[...])
        dyv = _rope_bwd(dk_ref[...], c_ref[...], s1_ref[...], s2_ref[...])
        dx, dg = _norm_bwd_math(xv, g_ref[...], dyv, LANES, inv_d)
        lane = lax.broadcasted_iota(jnp.int32, dx.shape, 1)
        dkv_ref[...] = jnp.where(lane < MLA_NOPE, dx, dvk_ref[...])
        part = pltpu.roll(jnp.where((lane >= MLA_NOPE) & (lane < MLA_QK), dx, 0.0), LANES - MLA_NOPE, 1)

        @pl.when(h == 0)
        def _():
            dkr_ref[...] = jnp.zeros_like(dkr_ref)

        dkr_ref[...] += part

        @pl.when((pl.program_id(0) == 0) & (h == 0))
        def _():
            dg_ref[...] = jnp.zeros_like(dg_ref)

        dg_ref[...] += dg

    blk = pl.BlockSpec((tr, LANES), lambda i, h: (i, h))
    tab = pl.BlockSpec((tr, LANES), lambda i, h: (i, 0))
    one = pl.BlockSpec((1, LANES), lambda i, h: (0, 0))
    return pl.pallas_call(
        body, name=name, grid=(T // tr, W // LANES),
        in_specs=[blk, pl.BlockSpec((tr, LANES), lambda i, h: (i, krb)), one, tab, tab, tab, blk, blk],
        out_specs=(blk, tab, one),
        out_shape=(jax.ShapeDtypeStruct((T, W), F32), jax.ShapeDtypeStruct((T, LANES), F32),
                   jax.ShapeDtypeStruct((1, LANES), F32)),
        compiler_params=_params(("arbitrary", "arbitrary")),
    )(kv_lin, z, gain, *tabs, dk, dvk)


def _ffn_up(u, w_gu, name, exch=None):
    T, D = u.shape
    F = w_gu.shape[1] // 2
    tm, tn = _tile(T, 640, 16), _tile(F, 1408, 128)
    nj = F // tn

    def body(u_ref, wg_ref, wu_ref, g_ref, up_ref, a_ref):
        uv = u_ref[...]
        g = jnp.dot(uv, wg_ref[...], preferred_element_type=F32)
        up = jnp.dot(uv, wu_ref[...], preferred_element_type=F32)
        g_ref[...] = g.astype(BF16)
        up_ref[...] = up.astype(BF16)
        a_ref[...] = (g * jax.nn.sigmoid(g) * up).astype(BF16)

    o_spec = pl.BlockSpec((tm, tn), lambda i, j: (i, j))
    sh = jax.ShapeDtypeStruct((T, F), BF16)
    return _call(
        body, name, (T // tm, nj),
        [pl.BlockSpec((tm, D), lambda i, j: (i, 0)),
         pl.BlockSpec((D, tn), lambda i, j: (0, j)),
         pl.BlockSpec((D, tn), lambda i, j: (0, j + nj))],
        [o_spec, o_spec, o_spec], [sh, sh, sh], [u, w_gu, w_gu],
        sem=("parallel", "parallel"), exch=exch)


def _ffn_down_bwd(dh, w_down, g, up, name, exch=None):
    T, D = dh.shape
    F = w_down.shape[0]
    tm, tn = _tile(T, 640, 16), _tile(F, 1408, 128)

    def body(dh_ref, w_ref, g_ref, up_ref, dg_ref, dup_ref):
        da = 0.5 * lax.dot_general(dh_ref[...].astype(BF16), w_ref[...], (((1,), (1,)), ((), ())),
                                   preferred_element_type=F32)
        gv = g_ref[...].astype(F32)
        sg = jax.nn.sigmoid(gv)
        silu = gv * sg
        dup_ref[...] = (da * silu).astype(BF16)
        dg_ref[...] = (da * up_ref[...].astype(F32) * (sg + silu * (1.0 - sg))).astype(BF16)

    t_spec = pl.BlockSpec((tm, tn), lambda i, j: (i, j))
    sh = jax.ShapeDtypeStruct((T, F), BF16)
    return _call(
        body, name, (T // tm, F // tn),
        [pl.BlockSpec((tm, D), lambda i, j: (i, 0)),
         pl.BlockSpec((tn, D), lambda i, j: (j, 0)), t_spec, t_spec],
        [t_spec, t_spec], [sh, sh], [dh, w_down, g, up],
        sem=("parallel", "parallel"), exch=exch)


def _ffn_up_bwd_dx(dg, dup, w_gu, name, exch=None):
    T, F = dg.shape
    D = w_gu.shape[0]
    tm, tk = _tile(T, 640, 16), _tile(F, 1408, 128)
    nk = F // tk
    nt = (((1,), (1,)), ((), ()))

    def body(dg_ref, dup_ref, wg_ref, wu_ref, o_ref, acc_ref):
        k = pl.program_id(1)

        @pl.when(k == 0)
        def _():
            acc_ref[...] = jnp.zeros_like(acc_ref)

        acc_ref[...] += (lax.dot_general(dg_ref[...], wg_ref[...], nt, preferred_element_type=F32)
                         + lax.dot_general(dup_ref[...], wu_ref[...], nt, preferred_element_type=F32))

        @pl.when(k == nk - 1)
        def _():
            o_ref[...] = acc_ref[...]

    return _call(
        body, name, (T // tm, nk),
        [pl.BlockSpec((tm, tk), lambda i, k: (i, k)),
         pl.BlockSpec((tm, tk), lambda i, k: (i, k)),
         pl.BlockSpec((D, tk), lambda i, k: (0, k)),
         pl.BlockSpec((D, tk), lambda i, k: (0, k + nk))],
        [pl.BlockSpec((tm, D), lambda i, k: (i, 0))], [jax.ShapeDtypeStruct((T, D), F32)],
        [dg, dup, w_gu, w_gu], scratch_shapes=[pltpu.VMEM((tm, D), F32)],
        sem=("parallel", "arbitrary"), exch=exch)


def _logsig(x):
    return jnp.minimum(x, 0.0) - jnp.log(1.0 + jnp.exp(-jnp.abs(x)))


def _cum_fwd(fl, flm, bf, name):
    B, S, _ = fl.shape
    nb = S // LANES

    def body(fl_ref, flm_ref, bf_ref, cum_ref, cumm_ref):
        rows = lax.broadcasted_iota(jnp.int32, (LANES, LANES), 0)
        cols = lax.broadcasted_iota(jnp.int32, (LANES, LANES), 1)
        tri = (rows >= cols).astype(F32)
        bias = bf_ref[...]
        lfm = jnp.where(rows < N_META, _logsig(flm_ref[...] + bias), 0.0)
        cm = jnp.dot(tri, lfm, precision=lax.Precision.HIGHEST, preferred_element_type=F32)
        cumm_ref[...] = cm * LOG2E
        base = cm[LANES - 1:LANES, :]
        for b in range(B):
            def blk(i, carry):
                r0 = pl.multiple_of(i * LANES, LANES)
                lf = _logsig(fl_ref[b, pl.ds(r0, LANES), :] + bias)
                c = jnp.dot(tri, lf, precision=lax.Precision.HIGHEST,
                            preferred_element_type=F32) + carry
                cum_ref[b, pl.ds(r0, LANES), :] = c * LOG2E
                return c[LANES - 1:LANES, :]

            lax.fori_loop(0, nb, blk, base)

    return pl.pallas_call(
        body, name=name,
        out_shape=(jax.ShapeDtypeStruct((B, S, LANES), F32),
                   jax.ShapeDtypeStruct((LANES, LANES), F32)),
        compiler_params=_params(),
    )(fl, flm, bf)


def _cum_bwd(dc, dcm, fl, flm, bf, name):
    B, S, _ = fl.shape
    nb = S // LANES

    def body(dc_ref, dcm_ref, fl_ref, flm_ref, bf_ref, dfl_ref, dflm_ref, dbf_ref):
        rows = lax.broadcasted_iota(jnp.int32, (LANES, LANES), 0)
        cols = lax.broadcasted_iota(jnp.int32, (LANES, LANES), 1)
        triu = (rows <= cols).astype(F32)
        bias = bf_ref[...]
        total = jnp.zeros((1, LANES), F32)
        dbf = jnp.zeros((1, LANES), F32)
        for b in range(B):
            tail = jnp.zeros((1, LANES), F32)
            for t in range(nb):
                r0 = (nb - 1 - t) * LANES
                rc = jnp.dot(triu, dc_ref[b, r0:r0 + LANES, :], precision=lax.Precision.HIGHEST,
                             preferred_element_type=F32) + tail
                xv = fl_ref[b, r0:r0 + LANES, :] + bias
                d = rc / (1.0 + jnp.exp(xv))
                dfl_ref[b, r0:r0 + LANES, :] = d
                tail = rc[0:1, :]
                dbf = dbf + jnp.sum(d, axis=0, keepdims=True)
            total = total + tail
        rcm = jnp.dot(triu, dcm_ref[...], precision=lax.Precision.HIGHEST,
                      preferred_element_type=F32) + total
        dm = jnp.where(rows < N_META, rcm / (1.0 + jnp.exp(flm_ref[...] + bias)), 0.0)
        dflm_ref[...] = dm
        dbf_ref[...] = dbf + jnp.sum(dm, axis=0, keepdims=True)

    return pl.pallas_call(
        body, name=name,
        out_shape=(jax.ShapeDtypeStruct((B, S, LANES), F32),
                   jax.ShapeDtypeStruct((LANES, LANES), F32),
                   jax.ShapeDtypeStruct((1, LANES), F32)),
        compiler_params=_params(),
    )(dc, dcm, fl, flm, bf)


_NT = (((1,), (1,)), ((), ()))


def _attn_specs(S, NX, qw, v_col0):
    mb = NX // META_BLK
    vb = v_col0 // qw
    return (pl.BlockSpec((S, qw), lambda b, p: (b, p)),
            pl.BlockSpec((META_BLK, qw), lambda b, p: (mb, p)),
            pl.BlockSpec((S, qw), lambda b, p: (b, vb + p)),
            pl.BlockSpec((META_BLK, qw), lambda b, p: (mb, vb + p)),
            pl.BlockSpec((S, LANES), lambda b, p: (b, p)))


def _cum_specs(S, TK):
    return [pl.BlockSpec((1, 2, S // TK, 1, TK), lambda b, p: (b, p, 0, 0, 0)),
            pl.BlockSpec((1, 2, 1, META_BLK), lambda b, p: (b, p, 0, 0))]


LOG2E = 1.4426950408889634


def _attn_fwd(qn, kn, vsrc, v_col0, mla, scale, S, NX, name, ck=None, cmk=None, exch=None):
    T = qn.shape[0]
    B = NX // S
    qw = 2 * LANES if mla else LANES
    npair = qn.shape[1] // qw
    TQ = min(512, S)
    TK = TQ
    forget = ck is not None
    a = scale * LOG2E

    def body(*refs):
        if forget:
            q_ref, k_ref, km_ref, v_ref, vm_ref, ck_ref, cmk_ref, _, o_ref, lse_ref = refs
        else:
            q_ref, k_ref, km_ref, v_ref, vm_ref, _, o_ref, lse_ref = refs
        lo = lax.broadcasted_iota(jnp.int32, (1, LANES), 1) < HALF
        mcol = lax.broadcasted_iota(jnp.int32, (TQ, META_BLK), 1)
        causal = (lax.broadcasted_iota(jnp.int32, (TQ, TK), 0)
                  >= lax.broadcasted_iota(jnp.int32, (TQ, TK), 1))
        two = lax.broadcasted_iota(jnp.int32, (TQ, 2), 1)
        for qi in range(S // TQ):
            q0 = qi * TQ
            sls = [slice(e * LANES, (e + 1) * LANES) if mla else slice(None) for e in range(2)]
            if mla:
                qts = [q_ref[q0:q0 + TQ, sl] for sl in sls]
            else:
                qts = [jnp.where(lo if e == 0 else ~lo, q_ref[q0:q0 + TQ, :], 0.0).astype(BF16)
                       for e in range(2)]

            def step(e, kt, vt, c2, mask, carry):
                m, l, acc = carry
                s = lax.dot_general(qts[e], kt, _NT, preferred_element_type=F32) * a
                if forget:
                    s = s - c2
                if mask is not None:
                    s = jnp.where(mask, s, NEG)
                m2 = jnp.max(s, axis=1, keepdims=True)
                if m is not None:
                    m2 = jnp.maximum(m, m2)
                p = jnp.exp2(s - m2)
                l2 = jnp.sum(p, axis=1, keepdims=True)
                acc2 = jnp.dot(p.astype(BF16), vt.astype(BF16), preferred_element_type=F32)
                if m is not None:
                    alpha = jnp.exp2(m - m2)
                    l2, acc2 = alpha * l + l2, alpha * acc + acc2
                return m2, l2, acc2

            def both(rows, kj, mask, carry):
                return tuple(step(e, k_ref[rows, sls[e]], v_ref[rows, sls[e]],
                                  ck_ref[0, e, kj] if forget else None, mask, carry[e]) for e in range(2))

            def below(kj, carry):
                return both(pl.ds(pl.multiple_of(kj * TK, TK), TK), kj, None, carry)

            carry = tuple(step(e, km_ref[:, sls[e]], vm_ref[:, sls[e]], cmk_ref[0, e] if forget else None,
                               mcol < N_META, (None, None, None)) for e in range(2))
            if qi:
                carry = lax.fori_loop(0, qi, below, carry)
            carry = both(slice(q0, q0 + TK), qi, causal, carry)
            outs = [acc / l for _, l, acc in carry]
            lses = [m + jnp.log2(l) for m, l, _ in carry]
            first = pltpu.roll(outs[0], HALF, 1) if mla else outs[0]
            o_ref[q0:q0 + TQ, :] = jnp.where(lo, first, outs[1])
            lse_ref[0, 0, q0:q0 + TQ, :] = jnp.where(two == 0, lses[0], lses[1])

    qk, kmeta, vv, vmeta, pair = _attn_specs(S, NX, qw, v_col0)
    specs = [qk, qk, kmeta, vv, vmeta]
    ins = [qn, kn, kn, vsrc, vsrc]
    if forget:
        specs += _cum_specs(S, TK)
        ins += [ck, cmk]
    specs.append(pl.BlockSpec(memory_space=pl.ANY))
    ins.append(jnp.zeros((T, npair * LANES), F32))
    lse_spec = pl.BlockSpec((1, 1, S, 2), lambda b, p: (b, p, 0, 0))
    return _call(
        body, name, (B, npair), specs, [pair, lse_spec],
        [jax.ShapeDtypeStruct((T, npair * LANES), F32), jax.ShapeDtypeStruct((B, npair, S, 2), F32)],
        ins, sem=("parallel", "parallel"), aliases={len(ins) - 1: 0}, exch=exch)


def _attn_bwd(qn, kn, vsrc, v_col0, o, lse, do, mla, scale, S, NX, name, ck=None, cmk=None, exch=None):
    T, W = qn.shape
    B = NX // S
    qw = 2 * LANES if mla else LANES
    npair = W // qw
    TQ = min(512, S)
    TK = TQ
    forget = ck is not None
    a = scale * LOG2E
    _TN = (((0,), (0,)), ((), ()))

    def body(*refs):
        refs = list(refs)
        q_ref, k_ref, km_ref, v_ref, vm_ref, o_ref, do_ref, lse_ref = refs[:8]
        pos = 8
        if forget:
            ck_ref, cmk_ref = refs[8:10]
            pos = 10
        pos += 3
        dq_ref, dk_ref, dv_ref, dkm_ref, dvm_ref = refs[pos:pos + 5]
        if forget:
            dck_ref, dcm_ref, dcq_ref = refs[pos + 5:pos + 8]
            dck_ref[...] = jnp.zeros_like(dck_ref)
            dcm_ref[...] = jnp.zeros_like(dcm_ref)
        dk_ref[...] = jnp.zeros_like(dk_ref)
        dv_ref[...] = jnp.zeros_like(dv_ref)
        dkm_ref[...] = jnp.zeros_like(dkm_ref)
        dvm_ref[...] = jnp.zeros_like(dvm_ref)
        lo = lax.broadcasted_iota(jnp.int32, (1, LANES), 1) < HALF
        mcol = lax.broadcasted_iota(jnp.int32, (TQ, META_BLK), 1)
        causal = (lax.broadcasted_iota(jnp.int32, (TQ, TK), 0)
                  >= lax.broadcasted_iota(jnp.int32, (TQ, TK), 1))
        two = lax.broadcasted_iota(jnp.int32, (TQ, 2), 1)
        for qi in range(S // TQ):
            q0 = qi * TQ
            dof = do_ref[q0:q0 + TQ, :]
            prod = dof * o_ref[q0:q0 + TQ, :]
            lse2 = lse_ref[0, 0, q0:q0 + TQ, :]
            sls = [slice(e * LANES, (e + 1) * LANES) if mla else slice(None) for e in range(2)]
            mine = [lo, ~lo]
            if mla:
                qts = [q_ref[q0:q0 + TQ, sl] for sl in sls]
                dots = [jnp.where(lo, 0.0, pltpu.roll(dof, HALF, 1) if e == 0 else dof).astype(BF16)
                        for e in range(2)]
            else:
                qts = [jnp.where(mine[e], q_ref[q0:q0 + TQ, :], 0.0).astype(BF16) for e in range(2)]
                dots = [jnp.where(mine[e], dof, 0.0).astype(BF16) for e in range(2)]
            deltas = [jnp.sum(jnp.where(mine[e], prod, 0.0), axis=1, keepdims=True) for e in range(2)]
            lse_ts = [jnp.sum(jnp.where(two == e, lse2, 0.0), axis=1, keepdims=True) for e in range(2)]

            def grads(e, kt, vt, c2, mask):
                s = lax.dot_general(qts[e], kt, _NT, preferred_element_type=F32) * a
                if forget:
                    s = s - c2
                p = jnp.exp2(s - lse_ts[e])
                if mask is not None:
                    p = jnp.where(mask, p, 0.0)
                dp = lax.dot_general(dots[e], vt, _NT, preferred_element_type=F32)
                ds = p * (dp - deltas[e])
                dsb = ds.astype(BF16)
                return (jnp.dot(dsb, kt, preferred_element_type=F32),
                        lax.dot_general(dsb, qts[e], _TN, preferred_element_type=F32) * scale,
                        lax.dot_general(p.astype(BF16), dots[e], _TN, preferred_element_type=F32),
                        -jnp.sum(ds, axis=0, keepdims=True) if forget else None,
                        jnp.sum(ds, axis=1, keepdims=True) if forget else None)

            def block(k_at, v_at, dk_at, dv_at, c_at, dc_at, mask, dqs):
                got = [grads(e, k_at(sls[e]), v_at(sls[e]).astype(BF16), c_at(e) if forget else None, mask)
                       for e in range(2)]
                if mla:
                    for e in range(2):
                        dk_at(sls[e], got[e][1])
                        dv_at(sls[e], got[e][2])
                else:
                    dk_at(sls[0], got[0][1] + got[1][1])
                    dv_at(sls[0], got[0][2] + got[1][2])
                if forget:
                    for e in range(2):
                        dc_at(e, got[e][3])
                picks = (0, 0, 4, 4) if forget else (0, 0)
                new = tuple(got[i % 2][k] for i, k in enumerate(picks))
                return new if dqs is None else tuple(x + y for x, y in zip(dqs, new))

            def add_to(ref, *lead):
                def add(*idx_and_val):
                    *idx, val = idx_and_val
                    ref[(*lead, *idx)] += val
                return add

            def token_block(rows, kj, mask, dqs):
                return block(lambda sl: k_ref[rows, sl], lambda sl: v_ref[rows, sl],
                             lambda sl, val: add_to(dk_ref)(rows, sl, val),
                             lambda sl, val: add_to(dv_ref)(rows, sl, val),
                             lambda e: ck_ref[0, e, kj], lambda e, val: add_to(dck_ref, 0)(e, kj, val),
                             mask, dqs)

            dqs = block(lambda sl: km_ref[:, sl], lambda sl: vm_ref[:, sl],
                        lambda sl, val: add_to(dkm_ref, 0)(slice(None), sl, val),
                        lambda sl, val: add_to(dvm_ref, 0)(slice(None), sl, val),
                        lambda e: cmk_ref[0, e], lambda e, val: add_to(dcm_ref, 0)(e, val),
                        mcol < N_META, None)

            def below(kj, dqs):
                return token_block(pl.ds(pl.multiple_of(kj * TK, TK), TK), kj, None, dqs)

            if qi:
                dqs = lax.fori_loop(0, qi, below, dqs)
            dqs = token_block(slice(q0, q0 + TK), qi, causal, dqs)
            if forget:
                dcq_ref[0, 0, q0:q0 + TQ, :] = jnp.where(two == 0, dqs[2], dqs[3])
            if mla:
                for e in range(2):
                    dq_ref[q0:q0 + TQ, sls[e]] = dqs[e] * scale
            else:
                dq_ref[q0:q0 + TQ, :] = jnp.where(lo, dqs[0], dqs[1]) * scale

    qk, kmeta, vv, vmeta, pair = _attn_specs(S, NX, qw, v_col0)
    lse_spec = pl.BlockSpec((1, 1, S, 2), lambda b, p: (b, p, 0, 0))
    specs = [qk, qk, kmeta, vv, vmeta, pair, pair, lse_spec]
    ins = [qn, kn, kn, vsrc, vsrc, o, do, lse]
    if forget:
        specs += _cum_specs(S, TK)
        ins += [ck, cmk]
    first_alias = len(ins)
    specs += [pl.BlockSpec(memory_space=pl.ANY)] * 3
    ins += [jnp.zeros((T, W), F32)] * 3
    mspec = pl.BlockSpec((1, META_BLK, qw), lambda b, p: (b, 0, p))
    out_specs = [qk, qk, qk, mspec, mspec]
    tok = jax.ShapeDtypeStruct((T, W), F32)
    met = jax.ShapeDtypeStruct((B, META_BLK, W), F32)
    out_shape = [tok, tok, tok, met, met]
    if forget:
        out_specs += _cum_specs(S, TK) + [lse_spec]
        out_shape += [jax.ShapeDtypeStruct((B, HEADS, S // TK, 1, TK), F32),
                      jax.ShapeDtypeStruct((B, HEADS, 1, META_BLK), F32),
                      jax.ShapeDtypeStruct((B, npair, S, 2), F32)]
    return _call(
        body, name, (B, npair), specs, out_specs, out_shape, ins, sem=("parallel", "parallel"),
        aliases={first_alias: 0, first_alias + 1: 1, first_alias + 2: 2}, exch=exch)


def _gate_fwd(z, bg, of, om, name):
    T, D = of.shape
    tm = _tile(T, 640, 16)

    def body(z_ref, bg_ref, of_ref, om_ref, o_ref):
        gt = jax.nn.sigmoid(z_ref[...] + bg_ref[...])
        o_ref[...] = (gt[:, :D] * of_ref[...] + gt[:, D:] * om_ref[...]).astype(BF16)

    row = pl.BlockSpec((tm, D), lambda i: (i, 0))
    return pl.pallas_call(
        body, name=name, grid=(T // tm,),
        in_specs=[pl.BlockSpec((tm, 2 * D), lambda i: (i, 0)),
                  pl.BlockSpec((1, 2 * D), lambda i: (0, 0)), row, row],
        out_specs=row, out_shape=jax.ShapeDtypeStruct((T, D), BF16),
        compiler_params=_params(("parallel",)),
    )(z, bg, of, om)


def _gate_bwd(dmix, z, bg, of, om, name):
    T, D = of.shape
    tm = _tile(T, 640, 16)

    def body(dm_ref, z_ref, bg_ref, of_ref, om_ref, dgl_ref, dof_ref, dom_ref, dbg_ref):
        gt = jax.nn.sigmoid(z_ref[...] + bg_ref[...])
        dm = dm_ref[...]
        dof_ref[...] = (dm * gt[:, :D]).astype(BF16)
        dom_ref[...] = (dm * gt[:, D:]).astype(BF16)
        dgl = jnp.concatenate([dm * of_ref[...], dm * om_ref[...]], axis=1) * gt * (1.0 - gt)
        dgl_ref[...] = dgl.astype(BF16)

        @pl.when(pl.program_id(0) == 0)
        def _():
            dbg_ref[...] = jnp.zeros_like(dbg_ref)

        dbg_ref[...] += jnp.sum(dgl, axis=0, keepdims=True)

    row = pl.BlockSpec((tm, D), lambda i: (i, 0))
    wide = pl.BlockSpec((tm, 2 * D), lambda i: (i, 0))
    one = pl.BlockSpec((1, 2 * D), lambda i: (0, 0))
    return pl.pallas_call(
        body, name=name, grid=(T // tm,),
        in_specs=[row, wide, one, row, row], out_specs=(wide, row, row, one),
        out_shape=(jax.ShapeDtypeStruct((T, 2 * D), BF16), jax.ShapeDtypeStruct((T, D), BF16),
                   jax.ShapeDtypeStruct((T, D), BF16), jax.ShapeDtypeStruct((1, 2 * D), F32)),
        compiler_params=_params(("arbitrary",)),
    )(dmix, z, bg, of, om)


def _loss(h, tgt, n_valid, name):
    T, D = h.shape
    tm = _tile(T, 640, 16)

    def body(h_ref, t_ref, dh_ref, l_ref):
        i = pl.program_id(0)
        rows = lax.broadcasted_iota(jnp.int32, (tm, D), 0) + i * tm
        err = jnp.where(rows < n_valid, h_ref[...] - t_ref[...], 0.0)
        dh_ref[...] = err * (1.0 / D)

        @pl.when(i == 0)
        def _():
            l_ref[...] = jnp.zeros_like(l_ref)

        l_ref[...] += 0.5 * jnp.sum(jnp.sum(err * err, axis=1, keepdims=True) * (1.0 / D))

    row = pl.BlockSpec((tm, D), lambda i: (i, 0))
    acc = pl.BlockSpec((8, LANES), lambda i: (0, 0))
    return pl.pallas_call(
        body, name=name, grid=(T // tm,), in_specs=[row, row], out_specs=(row, acc),
        out_shape=(jax.ShapeDtypeStruct((T, D), F32), jax.ShapeDtypeStruct((8, LANES), F32)),
        compiler_params=_params(("arbitrary",)),
    )(h, tgt)


def _adamw(parts, w, m, v, name):
    P, R, C = parts.shape
    tr = _tile(R, max(8, (1 << 18) // C), 8)
    bc1 = 1.0 - ADAM_B1 ** ADAM_STEP
    bc2 = 1.0 - ADAM_B2 ** ADAM_STEP

    def body(p_ref, w_ref, m_ref, v_ref, g_ref, d_ref, m2_ref, v2_ref):
        g = p_ref[0].astype(F32)
        for j in range(1, P):
            g = g + p_ref[j].astype(F32)
        m2 = ADAM_B1 * m_ref[...] + (1.0 - ADAM_B1) * g
        v2 = ADAM_B2 * v_ref[...] + (1.0 - ADAM_B2) * (g * g)
        m_hat = m2 / bc1
        v_hat = v2 / bc2
        g_ref[...] = g
        d_ref[...] = -ADAM_LR * (m_hat / (jnp.sqrt(v_hat) + ADAM_EPS) + ADAM_WD * w_ref[...])
        m2_ref[...] = m2
        v2_ref[...] = v2

    row = pl.BlockSpec((tr, C), lambda i: (i, 0))
    sh = jax.ShapeDtypeStruct((R, C), F32)
    return pl.pallas_call(
        body, name=name, grid=(R // tr,),
        in_specs=[pl.BlockSpec((P, tr, C), lambda i: (0, i, 0)), row, row, row],
        out_specs=(row, row, row, row), out_shape=(sh, sh, sh, sh),
        compiler_params=_params(("parallel",)),
    )(parts, w, m, v)


def _peer(d):
    x, y, c = lax.axis_index("x"), lax.axis_index("y"), lax.axis_index("c")
    px = 1 - x if d & 4 else x
    py = 1 - y if d & 2 else y
    pc = 1 - c if d & 1 else c
    return (px, py, pc), 4 * px + 2 * py + pc


class _Exchange:
    def __init__(self, srcs, gather):
        self.srcs, self.gather, self.n = list(srcs), gather, len(srcs)
        n = self.n
        hbm = pl.BlockSpec(memory_space=pl.ANY)
        self.in_specs = [hbm] * n
        self.out_specs = [hbm] * n
        self.out_shape = [jax.ShapeDtypeStruct((N_DEV,) + s.shape[-2:], s.dtype) for s in srcs]
        self.scratch = [pltpu.SemaphoreType.DMA((N_DEV - 1, n)), pltpu.SemaphoreType.DMA((N_DEV - 1, n)),
                        pltpu.SemaphoreType.DMA((n,))]

    def _copies(self, src_refs, out_refs, sems):
        send_sems, recv_sems, local_sems = sems
        _, me = _peer(0)

        def remote(w, d, landing):
            dev, lin = _peer(d)
            return pltpu.make_async_remote_copy(
                src_ref=src_refs[w] if self.gather else src_refs[w].at[lin],
                dst_ref=out_refs[w].at[lin if landing else me],
                send_sem=send_sems.at[d - 1, w], recv_sem=recv_sems.at[d - 1, w],
                device_id=dev, device_id_type=pl.DeviceIdType.MESH)

        pairs = [(w, d) for d in range(1, N_DEV) for w in range(self.n)]
        own = [pltpu.make_async_copy(src_refs[w] if self.gather else src_refs[w].at[me],
                                     out_refs[w].at[me], local_sems.at[w]) for w in range(self.n)]
        return own, [remote(w, d, False) for w, d in pairs], [remote(w, d, True) for w, d in pairs]

    def start(self, src_refs, out_refs, sems):
        own, sent, _ = self._copies(src_refs, out_refs, sems)
        for cp in own + sent:
            cp.start()

    def wait(self, src_refs, out_refs, sems):
        own, sent, landing = self._copies(src_refs, out_refs, sems)
        for cp in landing:
            cp.wait_recv()
        for cp in sent:
            cp.wait_send()
        for cp in own:
            cp.wait()


def _exchange(srcs, name, gather):
    ex = _Exchange(srcs, gather)
    n = ex.n

    def body(*refs):
        ex.start(refs[:n], refs[n:2 * n], refs[2 * n:])
        ex.wait(refs[:n], refs[n:2 * n], refs[2 * n:])

    outs = pl.pallas_call(
        body, name=name, in_specs=ex.in_specs, out_specs=tuple(ex.out_specs),
        out_shape=tuple(ex.out_shape), scratch_shapes=ex.scratch,
    )(*srcs)
    return list(outs)


def _call(body, name, grid, in_specs, out_specs, out_shape, ins, scratch_shapes=(), sem=None,
          aliases=None, exch=None):
    aliases = aliases or {}
    if exch is None:
        outs = pl.pallas_call(
            body, name=name, grid=grid, in_specs=list(in_specs), out_specs=tuple(out_specs),
            out_shape=tuple(out_shape), scratch_shapes=list(scratch_shapes),
            input_output_aliases=aliases, compiler_params=_params(sem),
        )(*ins)
        return list(outs), []
    ni, no, ns, n = len(in_specs), len(out_specs), len(scratch_shapes), exch.n
    last_ids = [g - 1 for g in grid]

    def hosted(*refs):
        cin, xin = refs[:ni], refs[ni:ni + n]
        cout, xout = refs[ni + n:ni + n + no], refs[ni + n + no:ni + 2 * n + no]
        cscr, xsem = refs[ni + 2 * n + no:ni + 2 * n + no + ns], refs[ni + 2 * n + no + ns:]
        ids = [pl.program_id(a) for a in range(len(grid))]
        first, last = ids[0] == 0, ids[0] == last_ids[0]
        for a in range(1, len(grid)):
            first, last = first & (ids[a] == 0), last & (ids[a] == last_ids[a])

        @pl.when(first)
        def _():
            exch.start(xin, xout, xsem)

        body(*cin, *cout, *cscr)

        @pl.when(last)
        def _():
            exch.wait(xin, xout, xsem)

    outs = pl.pallas_call(
        hosted, name=name, grid=grid, in_specs=list(in_specs) + exch.in_specs,
        out_specs=tuple(list(out_specs) + exch.out_specs),
        out_shape=tuple(list(out_shape) + exch.out_shape),
        scratch_shapes=list(scratch_shapes) + exch.scratch, input_output_aliases=aliases,
        compiler_params=_params(("arbitrary",) * len(grid)),
    )(*ins, *exch.srcs)
    return list(outs[:no]), list(outs[no:])


def _pack(arrs, cols, row_mult):
    flat = jnp.concatenate([a.reshape(-1) for a in arrs])
    n = flat.shape[0]
    quantum = cols * row_mult
    total = -(-n // quantum) * quantum
    return jnp.pad(flat, (0, total - n)).reshape(total // cols, cols)


def _pack_rows(arrs, cols, row_mult):
    flat = jnp.concatenate(arrs, axis=1)
    n = flat.shape[1]
    quantum = cols * row_mult
    total = -(-n // quantum) * quantum
    return jnp.pad(flat, ((0, 0), (0, total - n))).reshape(N_DEV, total // cols, cols)


def _unpack(packed, shapes):
    flat = packed.reshape(-1)
    out, off = [], 0
    for s in shapes:
        n = int(np.prod(s))
        out.append(flat[off:off + n].reshape(s))
        off += n
    return out


def _rope_tables(positions):
    inv_freq = ROPE_THETA ** (-jnp.arange(0, MLA_ROPE, 2, dtype=F32) / MLA_ROPE)
    ang = positions.astype(F32)[:, None] * inv_freq[None, :]
    cos, sin = jnp.cos(ang), jnp.sin(ang)
    n = positions.shape[0]
    ones, zeros = jnp.ones((n, MLA_NOPE), F32), jnp.zeros((n, MLA_NOPE), F32)
    tail1, tail0 = jnp.ones((n, LANES - MLA_QK), F32), jnp.zeros((n, LANES - MLA_QK), F32)
    z16 = jnp.zeros((n, 16), F32)
    c = jnp.concatenate([ones, cos, cos, tail1], axis=1)
    s1 = jnp.concatenate([zeros, -sin, z16, tail0], axis=1)
    s2 = jnp.concatenate([zeros, z16, sin, tail0], axis=1)
    return c, s1, s2


def kernel(x, meta_tokens, ffn1_norm, ffn1_w_gu, ffn1_w_down, mix_norm, w_in, b_forget, b_gate, fox_q_norm, fox_k_norm, mla_cq_norm, mla_w_uq, mla_ckv_norm, mla_w_ukv, mla_q_norm, mla_k_norm, w_branch_fox, w_branch_mla, w_out, ffn2_norm, ffn2_w_gu, ffn2_w_down, loss_target, m_meta_tokens, m_ffn1_norm, m_ffn1_w_gu, m_ffn1_w_down, m_mix_norm, m_w_in, m_b_forget, m_b_gate, m_fox_q_norm, m_fox_k_norm, m_mla_cq_norm, m_mla_w_uq, m_mla_ckv_norm, m_mla_w_ukv, m_mla_q_norm, m_mla_k_norm, m_w_branch_fox, m_w_branch_mla, m_w_out, m_ffn2_norm, m_ffn2_w_gu, m_ffn2_w_down, v_meta_tokens, v_ffn1_norm, v_ffn1_w_gu, v_ffn1_w_down, v_mix_norm, v_w_in, v_b_forget, v_b_gate, v_fox_q_norm, v_fox_k_norm, v_mla_cq_norm, v_mla_w_uq, v_mla_ckv_norm, v_mla_w_ukv, v_mla_q_norm, v_mla_k_norm, v_w_branch_fox, v_w_branch_mla, v_w_out, v_ffn2_norm, v_ffn2_w_gu, v_ffn2_w_down):
    names = ["meta_tokens", "ffn1_norm", "ffn1_w_gu", "ffn1_w_down", "mix_norm", "w_in", "b_forget",
             "b_gate", "fox_q_norm", "fox_k_norm", "mla_cq_norm", "mla_w_uq", "mla_ckv_norm",
             "mla_w_ukv", "mla_q_norm", "mla_k_norm", "w_branch_fox", "w_branch_mla", "w_out",
             "ffn2_norm", "ffn2_w_gu", "ffn2_w_down"]
    W = dict(zip(names, [meta_tokens, ffn1_norm, ffn1_w_gu, ffn1_w_down, mix_norm, w_in, b_forget,
                         b_gate, fox_q_norm, fox_k_norm, mla_cq_norm, mla_w_uq, mla_ckv_norm,
                         mla_w_ukv, mla_q_norm, mla_k_norm, w_branch_fox, w_branch_mla, w_out,
                         ffn2_norm, ffn2_w_gu, ffn2_w_down]))
    Mo = dict(zip(names, [m_meta_tokens, m_ffn1_norm, m_ffn1_w_gu, m_ffn1_w_down, m_mix_norm, m_w_in,
                          m_b_forget, m_b_gate, m_fox_q_norm, m_fox_k_norm, m_mla_cq_norm,
                          m_mla_w_uq, m_mla_ckv_norm, m_mla_w_ukv, m_mla_q_norm, m_mla_k_norm,
                          m_w_branch_fox, m_w_branch_mla, m_w_out, m_ffn2_norm, m_ffn2_w_gu,
                          m_ffn2_w_down]))
    Vo = dict(zip(names, [v_meta_tokens, v_ffn1_norm, v_ffn1_w_gu, v_ffn1_w_down, v_mix_norm, v_w_in,
                          v_b_forget, v_b_gate, v_fox_q_norm, v_fox_k_norm, v_mla_cq_norm,
                          v_mla_w_uq, v_mla_ckv_norm, v_mla_w_ukv, v_mla_q_norm, v_mla_k_norm,
                          v_w_branch_fox, v_w_branch_mla, v_w_out, v_ffn2_norm, v_ffn2_w_gu,
                          v_ffn2_w_down]))

    B, S, D = x.shape
    NX = B * S
    T = NX + META_BLK
    H = HEADS
    assert NX % META_BLK == 0 and S % LANES == 0
    me = 4 * lax.axis_index("x") + 2 * lax.axis_index("y") + lax.axis_index("c")

    big = [("ffn1_w_gu", 1), ("ffn1_w_down", 0), ("w_in", 1), ("mla_w_uq", 1), ("mla_w_ukv", 1),
           ("w_branch_fox", 1), ("w_branch_mla", 1), ("w_out", 0), ("ffn2_w_gu", 1), ("ffn2_w_down", 0)]
    first_group = ["ffn1_w_gu", "ffn1_w_down"]
    mix_group = ["w_in", "mla_w_uq", "mla_w_ukv", "w_branch_fox", "w_branch_mla", "w_out"]
    last_group = ["ffn2_w_gu", "ffn2_w_down"]
    axis_of = dict(big)
    full = {}

    def shards(group):
        return [W[n][0].astype(BF16) for n in group]

    def assemble(group, blks):
        for n, blk in zip(group, blks):
            _, r, c = blk.shape
            full[n] = (blk.transpose(1, 0, 2).reshape(r, N_DEV * c) if axis_of[n] == 1
                       else blk.reshape(N_DEV * r, c))

    got = _exchange(shards(first_group) + [meta_tokens], "gather_first", gather=True)
    assemble(first_group, got[:2])
    meta_full = got[2].transpose(1, 0, 2).reshape(N_META, D)

    Z_G, Z_FQ = 0, 2 * D
    Z_FK, Z_FV = Z_FQ + FOX_W, Z_FQ + 2 * FOX_W
    Z_CQ = Z_FQ + 3 * FOX_W
    Z_CKV = Z_CQ + MLA_Q_RANK
    Z_F = Z_CKV + MLA_KV_RANK
    Z_KR = Z_F + LANES

    def pad_lanes(a, w=LANES):
        return jnp.pad(a, [(0, 0)] * (a.ndim - 1) + [(0, w - a.shape[-1])])

    def rows_T(real, meta=None):
        n = real.shape[1]
        parts = [real]
        used = 0
        if meta is not None:
            parts.append(meta)
            used = meta.shape[0]
        if T - NX - used:
            parts.append(jnp.zeros((T - NX - used, n), real.dtype))
        return jnp.concatenate(parts, axis=0)

    def put_meta(tok, meta_per_seq):
        return lax.dynamic_update_slice(tok, meta_per_seq.sum(0), (NX, 0))

    h0 = rows_T(x.reshape(NX, D), meta_full)
    tgt = rows_T(loss_target.reshape(NX, D))

    def ffn_fwd(h, norm, w_gu, w_down, tag, exch=None):
        u = _norm_fwd(h, 0, D, D, norm, D, D, tag + "_norm")
        (g, up, a), carried = _ffn_up(u, w_gu, tag + "_up", exch=exch)
        h_out = _mm(a, w_down, "nn", tag + "_down", scale=0.5, res=h)
        return h_out, (u, g, up, a), carried

    h1, ffn1_saved, got = ffn_fwd(h0, W["ffn1_norm"], full["ffn1_w_gu"], full["ffn1_w_down"], "ffn1",
                                  exch=_Exchange(shards(mix_group), True))
    assemble(mix_group, got)
    wi = full["w_in"]
    o_fq = 0
    o_f = 3 * FOX_W
    o_cq = o_f + HEADS
    o_kr = o_cq + MLA_Q_RANK + MLA_KV_RANK
    o_g = o_kr + MLA_ROPE
    w_in_p = jnp.concatenate([
        wi[:, o_g:o_g + 2 * D], wi[:, o_fq:o_f], wi[:, o_cq:o_kr],
        jnp.pad(wi[:, o_f:o_cq], ((0, 0), (0, LANES - HEADS))),
        jnp.pad(wi[:, o_kr:o_g], ((0, 0), (0, LANES - MLA_ROPE)))], axis=1)
    w_uq_p = jnp.pad(full["mla_w_uq"].reshape(MLA_Q_RANK, H, MLA_QK),
                     ((0, 0), (0, 0), (0, LANES - MLA_QK))).reshape(MLA_Q_RANK, H * LANES)

    u2 = _norm_fwd(h1, 0, D, D, W["mix_norm"], D, D, "mix_norm")
    z = _mm(u2, w_in_p, "nn", "w_in")

    gq_f = jnp.tile(W["fox_q_norm"], (1, 2))
    gk_f = jnp.tile(W["fox_k_norm"], (1, 2))
    fqn = _norm_fwd(z, Z_FQ, FOX_W, LANES, gq_f, FOX_HD, FOX_HD, "fox_q_norm")
    fkn = _norm_fwd(z, Z_FK, FOX_W, LANES, gk_f, FOX_HD, FOX_HD, "fox_k_norm")
    fl = z[:NX, Z_F:Z_F + LANES].reshape(B, S, LANES)
    flm = z[NX:, Z_F:Z_F + LANES]
    bf = pad_lanes(W["b_forget"])
    cum, cumm = _cum_fwd(fl, flm, bf, "forget_cum")
    TK = min(512, S)
    ck = cum[:, :, :H].transpose(0, 2, 1).reshape(B, H, S // TK, 1, TK)
    cmk = jnp.broadcast_to(cumm[:, :H].T[None, :, None, :], (B, H, 1, META_BLK))
    (o_fox, lse_fox), got = _attn_fwd(fqn, fkn, z, Z_FV, False, FOX_HD ** -0.5, S, NX, "fox_attn", ck, cmk,
                                      exch=_Exchange(shards(last_group), True))
    assemble(last_group, got)
    of = _mm(o_fox, full["w_branch_fox"], "nn", "branch_fox")

    pos = jnp.concatenate([jnp.tile(jnp.arange(S) + N_META, B), jnp.arange(META_BLK)])
    tabs = _rope_tables(pos)
    cqn = _norm_fwd(z, Z_CQ, MLA_Q_RANK, MLA_Q_RANK, W["mla_cq_norm"], MLA_Q_RANK, MLA_Q_RANK, "mla_cq_norm")
    q_lin = _mm(cqn, w_uq_p, "nn", "mla_uq")
    ckvn = _norm_fwd(z, Z_CKV, MLA_KV_RANK, MLA_KV_RANK, W["mla_ckv_norm"], MLA_KV_RANK, MLA_KV_RANK,
                     "mla_ckv_norm")
    kv_lin = _mm(ckvn, full["mla_w_ukv"], "nn", "mla_ukv")
    gq_m, gk_m = pad_lanes(W["mla_q_norm"]), pad_lanes(W["mla_k_norm"])
    mqn = _norm_fwd(q_lin, 0, H * LANES, LANES, gq_m, LANES, MLA_QK, "mla_q_norm", tabs=tabs)
    mkn = _mla_k_fwd(kv_lin, z, Z_KR, gk_m, tabs, "mla_k_norm")
    (o_mla, lse_mla), _ = _attn_fwd(mqn, mkn, kv_lin, 0, True, MLA_QK ** -0.5, S, NX, "mla_attn")
    om = _mm(o_mla, full["w_branch_mla"], "nn", "branch_mla")

    mix = _gate_fwd(z, W["b_gate"], of, om, "gate_mix")
    h2 = _mm(mix, full["w_out"], "nn", "w_out", res=h1)

    h3, ffn2_saved, _ = ffn_fwd(h2, W["ffn2_norm"], full["ffn2_w_gu"], full["ffn2_w_down"], "ffn2")

    dh3, loss_acc = _loss(h3, tgt, NX, "loss")
    loss = lax.psum(loss_acc[0, 0], AXES)

    G = {}
    parts = {}

    def scatter_of(group):
        per_dest = []
        for n in group:
            r, c = W[n].shape[1:]
            per_dest.append((G[n].reshape(r, N_DEV, c).transpose(1, 0, 2) if axis_of[n] == 1
                             else G[n].reshape(N_DEV, r, c)).astype(BF16))
        return _Exchange(per_dest, False)

    def ffn_bwd(dh, h, norm, w_gu, w_down, saved, tag, behind_down=None, behind_up=None):
        u, g, up, a = saved
        G[tag + "_w_down"] = _mm(a, dh, "tn", tag + "_dw_down", scale=0.5)
        (dg, dup), got = _ffn_down_bwd(dh, w_down, g, up, tag + "_down_bwd",
                                       exch=scatter_of(behind_down) if behind_down else None)
        parts.update(zip(behind_down or [], got))
        G[tag + "_w_gu"] = jnp.concatenate([_mm(u, dg, "tn", tag + "_dw_g"),
                                            _mm(u, dup, "tn", tag + "_dw_u")], axis=1)
        (du,), got = _ffn_up_bwd_dx(dg, dup, w_gu, tag + "_up_bwd",
                                    exch=scatter_of(behind_up) if behind_up else None)
        parts.update(zip(behind_up or [], got))
        dh_in, G[tag + "_norm"] = _norm_bwd(h, 0, D, D, norm, D, D, du, tag + "_norm_bwd", res=dh)
        return dh_in

    dh2 = ffn_bwd(dh3, h2, W["ffn2_norm"], full["ffn2_w_gu"], full["ffn2_w_down"], ffn2_saved, "ffn2")

    G["w_out"] = _mm(mix, dh2, "tn", "dw_out")
    dmix = _mm(dh2, full["w_out"], "nt", "w_out_bwd")
    dgl, dof, dom, G["b_gate"] = _gate_bwd(dmix, z, W["b_gate"], of, om, "gate_bwd")

    G["w_branch_fox"] = _mm(o_fox, dof, "tn", "dw_branch_fox")
    do_fox = _mm(dof, full["w_branch_fox"], "nt", "branch_fox_bwd")
    (dq_f, dk_f, dv_f, dkm_f, dvm_f, dck, dcmk, dcq), got = _attn_bwd(
        fqn, fkn, z, Z_FV, o_fox, lse_fox, do_fox, False, FOX_HD ** -0.5, S, NX, "fox_attn_bwd", ck, cmk,
        exch=scatter_of(last_group))
    parts.update(zip(last_group, got))
    dk_f, dv_f = put_meta(dk_f, dkm_f), put_meta(dv_f, dvm_f)
    dfq, gq = _norm_bwd(z, Z_FQ, FOX_W, LANES, gq_f, FOX_HD, FOX_HD, dq_f, "fox_q_norm_bwd", out_dtype=BF16)
    dfk, gk = _norm_bwd(z, Z_FK, FOX_W, LANES, gk_f, FOX_HD, FOX_HD, dk_f, "fox_k_norm_bwd", out_dtype=BF16)
    G["fox_q_norm"] = gq[:, :FOX_HD] + gq[:, FOX_HD:]
    G["fox_k_norm"] = gk[:, :FOX_HD] + gk[:, FOX_HD:]
    dc = pad_lanes(dck.reshape(B, H, S).transpose(0, 2, 1)
                   + dcq.transpose(0, 2, 1, 3).reshape(B, S, H))
    dcm = pad_lanes(dcmk.sum(0)[:, 0, :].T)
    dcm = jnp.where(jnp.arange(LANES)[:, None] < N_META, dcm, 0.0)
    dfl, dflm, dbf = _cum_bwd(dc, dcm, fl, flm, bf, "forget_cum_bwd")
    G["b_forget"] = dbf[:, :HEADS]
    dfl_t = rows_T(dfl.reshape(NX, LANES), dflm)

    G["w_branch_mla"] = _mm(o_mla, dom, "tn", "dw_branch_mla")
    do_mla = _mm(dom, full["w_branch_mla"], "nt", "branch_mla_bwd")
    (dq_m, dk_m, dvk, dkm_m, dvkm), _ = _attn_bwd(
        mqn, mkn, kv_lin, 0, o_mla, lse_mla, do_mla, True, MLA_QK ** -0.5, S, NX, "mla_attn_bwd")
    dk_m, dvk = put_meta(dk_m, dkm_m), put_meta(dvk, dvkm)
    dq_lin, gq = _norm_bwd(q_lin, 0, H * LANES, LANES, gq_m, LANES, MLA_QK, dq_m, "mla_q_norm_bwd", tabs=tabs)
    G["mla_q_norm"] = gq[:, :MLA_QK]
    G["mla_w_uq"] = _mm(cqn, dq_lin, "tn", "dw_uq").reshape(MLA_Q_RANK, H, LANES)[:, :, :MLA_QK].reshape(
        MLA_Q_RANK, H * MLA_QK)
    dcqn = _mm(dq_lin, w_uq_p, "nt", "mla_uq_bwd")
    dcq, G["mla_cq_norm"] = _norm_bwd(z, Z_CQ, MLA_Q_RANK, MLA_Q_RANK, W["mla_cq_norm"], MLA_Q_RANK,
                                      MLA_Q_RANK, dcqn, "mla_cq_norm_bwd", out_dtype=BF16)
    dkv_lin, dkr, gk = _mla_k_bwd(kv_lin, z, Z_KR, gk_m, tabs, dk_m, dvk, "mla_k_norm_bwd")
    G["mla_k_norm"] = gk[:, :MLA_QK]
    G["mla_w_ukv"] = _mm(ckvn, dkv_lin, "tn", "dw_ukv")
    dckvn = _mm(dkv_lin, full["mla_w_ukv"], "nt", "mla_ukv_bwd")
    dckv, G["mla_ckv_norm"] = _norm_bwd(z, Z_CKV, MLA_KV_RANK, MLA_KV_RANK, W["mla_ckv_norm"], MLA_KV_RANK,
                                        MLA_KV_RANK, dckvn, "mla_ckv_norm_bwd", out_dtype=BF16)

    dz = jnp.concatenate([dgl, dfq, dfk, dv_f.astype(BF16), dcq, dckv, dfl_t.astype(BF16),
                          dkr.astype(BF16)], axis=1)
    dw_in_p = _mm(u2, dz, "tn", "dw_in")
    G["w_in"] = jnp.concatenate([
        dw_in_p[:, Z_FQ:Z_CQ], dw_in_p[:, Z_F:Z_F + HEADS], dw_in_p[:, Z_CQ:Z_F],
        dw_in_p[:, Z_KR:Z_KR + MLA_ROPE], dw_in_p[:, Z_G:Z_G + 2 * D]], axis=1)
    du2 = _mm(dz, w_in_p, "nt", "w_in_bwd")
    dh1, G["mix_norm"] = _norm_bwd(h1, 0, D, D, W["mix_norm"], D, D, du2, "mix_norm_bwd", res=dh2)

    dh0 = ffn_bwd(dh1, h0, W["ffn1_norm"], full["ffn1_w_gu"], full["ffn1_w_down"], ffn1_saved, "ffn1",
                  behind_down=mix_group, behind_up=first_group)
    grad_x = dh0[:NX].reshape(B, S, D)
    G["meta_tokens"] = dh0[NX:NX + N_META]

    res = {}
    for n, _ in big:
        outs4 = _adamw(parts[n], W[n][0], Mo[n][0], Vo[n][0], "adamw_" + n)
        for key, arr in zip(("g", "d", "m", "v"), outs4):
            res[key, n] = arr[None]

    small = [n for n in names if n not in dict(big) and n != "meta_tokens"]
    small_shapes = [W[n].shape for n in small]
    spack = _pack([G["meta_tokens"]] + [G[n] for n in small], 1024, 8)
    (sparts,) = _exchange([spack], "gather_small_grads", gather=True)
    sflat = sparts.reshape(N_DEV, -1)
    dsh = D // N_DEV
    meta_part = lax.dynamic_slice(sflat[:, :N_META * D].reshape(N_DEV, N_META, D),
                                  (0, 0, me * dsh), (N_DEV, N_META, dsh)).reshape(N_DEV, -1)
    rep_len = sum(int(np.prod(s)) for s in small_shapes)
    rep_part = sflat[:, N_META * D:N_META * D + rep_len]
    sp = _pack_rows([meta_part, rep_part], LANES, 8)
    pks = lambda src: _pack([src["meta_tokens"]] + [src[n] for n in small], LANES, 8)
    g_s, d_s, m_s, v_s = _adamw(sp, pks(W), pks(Mo), pks(Vo), "adamw_small")
    shapes_s = [W["meta_tokens"].shape] + small_shapes
    for key, packed in (("g", g_s), ("d", d_s), ("m", m_s), ("v", v_s)):
        for n, arr in zip(["meta_tokens"] + small, _unpack(packed, shapes_s)):
            res[key, n] = arr

    outs = [loss, grad_x]
    for key in ("g", "d", "m", "v"):
        outs += [res[key, n] for n in names]
    return tuple(outs)
```

```python
import numpy as np
import jax
import jax.numpy as jnp
from jax import lax
from jax.experimental import pallas as pl
from jax.experimental.pallas import tpu as pltpu

F32 = jnp.float32
BF16 = jnp.bfloat16

N_META = 16
EPS = 1e-6
HEADS = 8
FOX_HD = 64
FOX_W = HEADS * FOX_HD
MLA_Q_RANK = 256
MLA_KV_RANK = 128
MLA_NOPE = 64
MLA_ROPE = 32
MLA_QK = MLA_NOPE + MLA_ROPE
MLA_V = 64
ROPE_THETA = 10000.0
LANES = 128
HALF = LANES // 2
META_BLK = 128
NEG = -1e30

ADAM_LR = 0.001
ADAM_B1 = 0.9
ADAM_B2 = 0.999
ADAM_EPS = 1e-08
ADAM_WD = 0.01
ADAM_STEP = 10

N_DEV = 8
AXES = ("x", "y", "c")
VMEM_LIMIT_BYTES = 56 * 1024 * 1024


def _tile(n, cap, mult):
    best = None
    for d in range(mult, min(n, cap) + 1, mult):
        if n % d == 0:
            best = d
    return n if best is None else best


def _row_tile(rows, width):
    return _tile(rows, max(16, (1 << 19) // width), 16)


def _params(sem=None):
    return pltpu.CompilerParams(dimension_semantics=sem, vmem_limit_bytes=VMEM_LIMIT_BYTES)


def _mm(a, b, mode, name, out_dtype=F32, scale=1.0, res=None, exch=None):
    if mode == "nn":
        (M, K), (K2, N) = a.shape, b.shape
    elif mode == "nt":
        (M, K), (N, K2) = a.shape, b.shape
    else:
        (K, M), (K2, N) = a.shape, b.shape
    assert K == K2, (a.shape, b.shape, mode)
    if mode == "tn":
        tm, tk = _tile(M, 1408, 128), _tile(K, 640, 16)
    else:
        tm, tk = _tile(M, 640, 16), _tile(K, 1408, 128)
    tn = _tile(N, 1408, 128)
    nk = K // tk
    a_spec = {"nn": pl.BlockSpec((tm, tk), lambda i, j, k: (i, k)),
              "nt": pl.BlockSpec((tm, tk), lambda i, j, k: (i, k)),
              "tn": pl.BlockSpec((tk, tm), lambda i, j, k: (k, i))}[mode]
    b_spec = {"nn": pl.BlockSpec((tk, tn), lambda i, j, k: (k, j)),
              "nt": pl.BlockSpec((tn, tk), lambda i, j, k: (j, k)),
              "tn": pl.BlockSpec((tk, tn), lambda i, j, k: (k, j))}[mode]
    dims = {"nn": (((1,), (0,)), ((), ())), "nt": (((1,), (1,)), ((), ())),
            "tn": (((0,), (0,)), ((), ()))}[mode]
    o_spec = pl.BlockSpec((tm, tn), lambda i, j, k: (i, j))
    has_res = res is not None

    def body(*refs):
        if has_res:
            a_ref, b_ref, r_ref, o_ref, acc_ref = refs
        else:
            a_ref, b_ref, o_ref, acc_ref = refs
        k = pl.program_id(2)

        @pl.when(k == 0)
        def _():
            acc_ref[...] = jnp.zeros_like(acc_ref)

        acc_ref[...] += lax.dot_general(a_ref[...].astype(BF16), b_ref[...].astype(BF16), dims,
                                        preferred_element_type=F32)

        @pl.when(k == nk - 1)
        def _():
            o = acc_ref[...] * scale
            if has_res:
                o = o + r_ref[...]
            o_ref[...] = o.astype(out_dtype)

    ins = [a, b] + ([res] if has_res else [])
    specs = [a_spec, b_spec] + ([o_spec] if has_res else [])
    (out,), got = _call(
        body, name, (M // tm, N // tn, nk), specs, [o_spec], [jax.ShapeDtypeStruct((M, N), out_dtype)],
        ins, scratch_shapes=[pltpu.VMEM((tm, tn), F32)], sem=("parallel", "parallel", "arbitrary"),
        exch=exch)
    return out if exch is None else (out, got)


def _rope_fwd(y, c, s1, s2):
    return y * c + pltpu.roll(y, LANES - 16, 1) * s1 + pltpu.roll(y, 16, 1) * s2


def _rope_bwd(dy, c, s1, s2):
    return dy * c + pltpu.roll(dy * s1, 16, 1) + pltpu.roll(dy * s2, LANES - 16, 1)


def _group_sum(v, seg):
    if seg == v.shape[-1]:
        return jnp.sum(v, axis=-1, keepdims=True)
    lo = lax.broadcasted_iota(jnp.int32, v.shape, 1) < seg
    s_lo = jnp.sum(jnp.where(lo, v, 0.0), axis=-1, keepdims=True)
    s_hi = jnp.sum(jnp.where(lo, 0.0, v), axis=-1, keepdims=True)
    return jnp.where(lo, s_lo, s_hi)


def _norm_fwd(src, col0, width, bw, gain, seg, d_true, name, tabs=None, out_dtype=BF16):
    T = src.shape[0]
    tr = _row_tile(T, bw)
    inv_d = 1.0 / d_true
    c0 = col0 // bw
    assert col0 % bw == 0 and width % bw == 0

    def body(*refs):
        if tabs is None:
            x_ref, g_ref, o_ref = refs
        else:
            x_ref, g_ref, c_ref, s1_ref, s2_ref, o_ref = refs
        xv = x_ref[...]
        r = lax.rsqrt(_group_sum(xv * xv, seg) * inv_d + EPS)
        y = xv * r * g_ref[...]
        if tabs is not None:
            y = _rope_fwd(y, c_ref[...], s1_ref[...], s2_ref[...])
        o_ref[...] = y.astype(out_dtype)

    specs = [pl.BlockSpec((tr, bw), lambda i, j: (i, c0 + j)), pl.BlockSpec((1, bw), lambda i, j: (0, 0))]
    ins = [src, gain]
    if tabs is not None:
        tab = pl.BlockSpec((tr, LANES), lambda i, j: (i, 0))
        specs += [tab, tab, tab]
        ins += list(tabs)
    return pl.pallas_call(
        body, name=name, grid=(T // tr, width // bw), in_specs=specs,
        out_specs=pl.BlockSpec((tr, bw), lambda i, j: (i, j)),
        out_shape=jax.ShapeDtypeStruct((T, width), out_dtype),
        compiler_params=_params(("parallel", "parallel")),
    )(*ins)


def _norm_bwd_math(xv, gain, dyv, seg, inv_d):
    r = lax.rsqrt(_group_sum(xv * xv, seg) * inv_d + EPS)
    gy = dyv * gain
    dot = _group_sum(gy * xv, seg)
    dx = r * gy - xv * (r * r * r * inv_d) * dot
    return dx, jnp.sum(dyv * xv * r, axis=0, keepdims=True)


def _norm_bwd(src, col0, width, bw, gain, seg, d_true, dy, name, tabs=None, res=None, out_dtype=F32):
    T = src.shape[0]
    tr = _row_tile(T, bw)
    inv_d = 1.0 / d_true
    c0 = col0 // bw
    has_res = res is not None

    def body(*refs):
        refs = list(refs)
        x_ref, g_ref, dy_ref = refs[:3]
        pos = 3
        if tabs is not None:
            c_ref, s1_ref, s2_ref = refs[3:6]
            pos = 6
        if has_res:
            r_ref = refs[pos]
            pos += 1
        dx_ref, dg_ref = refs[pos], refs[pos + 1]
        dyv = dy_ref[...].astype(F32)
        if tabs is not None:
            dyv = _rope_bwd(dyv, c_ref[...], s1_ref[...], s2_ref[...])
        dx, dg = _norm_bwd_math(x_ref[...], g_ref[...], dyv, seg, inv_d)
        if has_res:
            dx = dx + r_ref[...]
        dx_ref[...] = dx.astype(out_dtype)

        @pl.when((pl.program_id(0) == 0) & (pl.program_id(1) == 0))
        def _():
            dg_ref[...] = jnp.zeros_like(dg_ref)

        dg_ref[...] += dg

    blk = pl.BlockSpec((tr, bw), lambda i, j: (i, j))
    one = pl.BlockSpec((1, bw), lambda i, j: (0, 0))
    specs = [pl.BlockSpec((tr, bw), lambda i, j: (i, c0 + j)), one, blk]
    ins = [src, gain, dy]
    if tabs is not None:
        tab = pl.BlockSpec((tr, LANES), lambda i, j: (i, 0))
        specs += [tab, tab, tab]
        ins += list(tabs)
    if has_res:
        specs.append(blk)
        ins.append(res)
    return pl.pallas_call(
        body, name=name, grid=(T // tr, width // bw), in_specs=specs, out_specs=(blk, one),
        out_shape=(jax.ShapeDtypeStruct((T, width), out_dtype), jax.ShapeDtypeStruct((1, bw), F32)),
        compiler_params=_params(("arbitrary", "arbitrary")),
    )(*ins)


def _mla_k_raw(kv, kr):
    lane = lax.broadcasted_iota(jnp.int32, kv.shape, 1)
    return jnp.where(lane < MLA_NOPE, kv, jnp.where(lane < MLA_QK, pltpu.roll(kr, MLA_NOPE, 1), 0.0))


def _mla_k_fwd(kv_lin, z, kr_col, gain, tabs, name):
    T, W = kv_lin.shape
    tr = _row_tile(T, LANES)
    krb = kr_col // LANES
    inv_d = 1.0 / MLA_QK

    def body(kv_ref, kr_ref, g_ref, c_ref, s1_ref, s2_ref, o_ref):
        xv = _mla_k_raw(kv_ref[...], kr_ref[...])
        r = lax.rsqrt(jnp.sum(xv * xv, axis=-1, keepdims=True) * inv_d + EPS)
        o_ref[...] = _rope_fwd(xv * r * g_ref[...], c_ref[...], s1_ref[...], s2_ref[...]).astype(BF16)

    blk = pl.BlockSpec((tr, LANES), lambda i, h: (i, h))
    tab = pl.BlockSpec((tr, LANES), lambda i, h: (i, 0))
    return pl.pallas_call(
        body, name=name, grid=(T // tr, W // LANES),
        in_specs=[blk, pl.BlockSpec((tr, LANES), lambda i, h: (i, krb)),
                  pl.BlockSpec((1, LANES), lambda i, h: (0, 0)), tab, tab, tab],
        out_specs=blk, out_shape=jax.ShapeDtypeStruct((T, W), BF16),
        compiler_params=_params(("parallel", "parallel")),
    )(kv_lin, z, gain, *tabs)


def _mla_k_bwd(kv_lin, z, kr_col, gain, tabs, dk, dvk, name):
    T, W = kv_lin.shape
    tr = _row_tile(T, LANES)
    krb = kr_col // LANES
    inv_d = 1.0 / MLA_QK

    def body(kv_ref, kr_ref, g_ref, c_ref, s1_ref, s2_ref, dk_ref, dvk_ref, dkv_ref, dkr_ref, dg_ref):
        h = pl.program_id(1)
        xv = _mla_k_raw(kv_ref[...], kr_ref[...])
        dyv = _rope_bwd(dk_ref[...], c_ref[...], s1_ref[...], s2_ref[...])
        dx, dg = _norm_bwd_math(xv, g_ref[...], dyv, LANES, inv_d)
        lane = lax.broadcasted_iota(jnp.int32, dx.shape, 1)
        dkv_ref[...] = jnp.where(lane < MLA_NOPE, dx, dvk_ref[...])
        part = pltpu.roll(jnp.where((lane >= MLA_NOPE) & (lane < MLA_QK), dx, 0.0), LANES - MLA_NOPE, 1)

        @pl.when(h == 0)
        def _():
            dkr_ref[...] = jnp.zeros_like(dkr_ref)

        dkr_ref[...] += part

        @pl.when((pl.program_id(0) == 0) & (h == 0))
        def _():
            dg_ref[...] = jnp.zeros_like(dg_ref)

        dg_ref[...] += dg

    blk = pl.BlockSpec((tr, LANES), lambda i, h: (i, h))
    tab = pl.BlockSpec((tr, LANES), lambda i, h: (i, 0))
    one = pl.BlockSpec((1, LANES), lambda i, h: (0, 0))
    return pl.pallas_call(
        body, name=name, grid=(T // tr, W // LANES),
        in_specs=[blk, pl.BlockSpec((tr, LANES), lambda i, h: (i, krb)), one, tab, tab, tab, blk, blk],
        out_specs=(blk, tab, one),
        out_shape=(jax.ShapeDtypeStruct((T, W), F32), jax.ShapeDtypeStruct((T, LANES), F32),
                   jax.ShapeDtypeStruct((1, LANES), F32)),
        compiler_params=_params(("arbitrary", "arbitrary")),
    )(kv_lin, z, gain, *tabs, dk, dvk)


def _ffn_up(u, w_gu, name, exch=None):
    T, D = u.shape
    F = w_gu.shape[1] // 2
    tm, tn = _tile(T, 640, 16), _tile(F, 1408, 128)
    nj = F // tn

    def body(u_ref, wg_ref, wu_ref, g_ref, up_ref, a_ref):
        uv = u_ref[...]
        g = jnp.dot(uv, wg_ref[...], preferred_element_type=F32)
        up = jnp.dot(uv, wu_ref[...], preferred_element_type=F32)
        g_ref[...] = g.astype(BF16)
        up_ref[...] = up.astype(BF16)
        a_ref[...] = (g * jax.nn.sigmoid(g) * up).astype(BF16)

    o_spec = pl.BlockSpec((tm, tn), lambda i, j: (i, j))
    sh = jax.ShapeDtypeStruct((T, F), BF16)
    return _call(
        body, name, (T // tm, nj),
        [pl.BlockSpec((tm, D), lambda i, j: (i, 0)),
         pl.BlockSpec((D, tn), lambda i, j: (0, j)),
         pl.BlockSpec((D, tn), lambda i, j: (0, j + nj))],
        [o_spec, o_spec, o_spec], [sh, sh, sh], [u, w_gu, w_gu],
        sem=("parallel", "parallel"), exch=exch)


def _ffn_down_bwd(dh, w_down, g, up, name, exch=None):
    T, D = dh.shape
    F = w_down.shape[0]
    tm, tn = _tile(T, 640, 16), _tile(F, 1408, 128)

    def body(dh_ref, w_ref, g_ref, up_ref, dg_ref, dup_ref):
        da = 0.5 * lax.dot_general(dh_ref[...].astype(BF16), w_ref[...], (((1,), (1,)), ((), ())),
                                   preferred_element_type=F32)
        gv = g_ref[...].astype(F32)
        sg = jax.nn.sigmoid(gv)
        silu = gv * sg
        dup_ref[...] = (da * silu).astype(BF16)
        dg_ref[...] = (da * up_ref[...].astype(F32) * (sg + silu * (1.0 - sg))).astype(BF16)

    t_spec = pl.BlockSpec((tm, tn), lambda i, j: (i, j))
    sh = jax.ShapeDtypeStruct((T, F), BF16)
    return _call(
        body, name, (T // tm, F // tn),
        [pl.BlockSpec((tm, D), lambda i, j: (i, 0)),
         pl.BlockSpec((tn, D), lambda i, j: (j, 0)), t_spec, t_spec],
        [t_spec, t_spec], [sh, sh], [dh, w_down, g, up],
        sem=("parallel", "parallel"), exch=exch)


def _ffn_up_bwd_dx(dg, dup, w_gu, name, exch=None):
    T, F = dg.shape
    D = w_gu.shape[0]
    tm, tk = _tile(T, 640, 16), _tile(F, 1408, 128)
    nk = F // tk
    nt = (((1,), (1,)), ((), ()))

    def body(dg_ref, dup_ref, wg_ref, wu_ref, o_ref, acc_ref):
        k = pl.program_id(1)

        @pl.when(k == 0)
        def _():
            acc_ref[...] = jnp.zeros_like(acc_ref)

        acc_ref[...] += (lax.dot_general(dg_ref[...], wg_ref[...], nt, preferred_element_type=F32)
                         + lax.dot_general(dup_ref[...], wu_ref[...], nt, preferred_element_type=F32))

        @pl.when(k == nk - 1)
        def _():
            o_ref[...] = acc_ref[...]

    return _call(
        body, name, (T // tm, nk),
        [pl.BlockSpec((tm, tk), lambda i, k: (i, k)),
         pl.BlockSpec((tm, tk), lambda i, k: (i, k)),
         pl.BlockSpec((D, tk), lambda i, k: (0, k)),
         pl.BlockSpec((D, tk), lambda i, k: (0, k + nk))],
        [pl.BlockSpec((tm, D), lambda i, k: (i, 0))], [jax.ShapeDtypeStruct((T, D), F32)],
        [dg, dup, w_gu, w_gu], scratch_shapes=[pltpu.VMEM((tm, D), F32)],
        sem=("parallel", "arbitrary"), exch=exch)


def _logsig(x):
    return jnp.minimum(x, 0.0) - jnp.log(1.0 + jnp.exp(-jnp.abs(x)))


def _cum_fwd(fl, flm, bf, name):
    B, S, _ = fl.shape
    nb = S // LANES

    def body(fl_ref, flm_ref, bf_ref, cum_ref, cumm_ref):
        rows = lax.broadcasted_iota(jnp.int32, (LANES, LANES), 0)
        cols = lax.broadcasted_iota(jnp.int32, (LANES, LANES), 1)
        tri = (rows >= cols).astype(F32)
        bias = bf_ref[...]
        lfm = jnp.where(rows < N_META, _logsig(flm_ref[...] + bias), 0.0)
        cm = jnp.dot(tri, lfm, precision=lax.Precision.HIGHEST, preferred_element_type=F32)
        cumm_ref[...] = cm * LOG2E
        base = cm[LANES - 1:LANES, :]
        for b in range(B):
            def blk(i, carry):
                r0 = pl.multiple_of(i * LANES, LANES)
                lf = _logsig(fl_ref[b, pl.ds(r0, LANES), :] + bias)
                c = jnp.dot(tri, lf, precision=lax.Precision.HIGHEST,
                            preferred_element_type=F32) + carry
                cum_ref[b, pl.ds(r0, LANES), :] = c * LOG2E
                return c[LANES - 1:LANES, :]

            lax.fori_loop(0, nb, blk, base)

    return pl.pallas_call(
        body, name=name,
        out_shape=(jax.ShapeDtypeStruct((B, S, LANES), F32),
                   jax.ShapeDtypeStruct((LANES, LANES), F32)),
        compiler_params=_params(),
    )(fl, flm, bf)


def _cum_bwd(dc, dcm, fl, flm, bf, name):
    B, S, _ = fl.shape
    nb = S // LANES

    def body(dc_ref, dcm_ref, fl_ref, flm_ref, bf_ref, dfl_ref, dflm_ref, dbf_ref):
        rows = lax.broadcasted_iota(jnp.int32, (LANES, LANES), 0)
        cols = lax.broadcasted_iota(jnp.int32, (LANES, LANES), 1)
        triu = (rows <= cols).astype(F32)
        bias = bf_ref[...]
        total = jnp.zeros((1, LANES), F32)
        dbf = jnp.zeros((1, LANES), F32)
        for b in range(B):
            tail = jnp.zeros((1, LANES), F32)
            for t in range(nb):
                r0 = (nb - 1 - t) * LANES
                rc = jnp.dot(triu, dc_ref[b, r0:r0 + LANES, :], precision=lax.Precision.HIGHEST,
                             preferred_element_type=F32) + tail
                xv = fl_ref[b, r0:r0 + LANES, :] + bias
                d = rc / (1.0 + jnp.exp(xv))
                dfl_ref[b, r0:r0 + LANES, :] = d
                tail = rc[0:1, :]
                dbf = dbf + jnp.sum(d, axis=0, keepdims=True)
            total = total + tail
        rcm = jnp.dot(triu, dcm_ref[...], precision=lax.Precision.HIGHEST,
                      preferred_element_type=F32) + total
        dm = jnp.where(rows < N_META, rcm / (1.0 + jnp.exp(flm_ref[...] + bias)), 0.0)
        dflm_ref[...] = dm
        dbf_ref[...] = dbf + jnp.sum(dm, axis=0, keepdims=True)

    return pl.pallas_call(
        body, name=name,
        out_shape=(jax.ShapeDtypeStruct((B, S, LANES), F32),
                   jax.ShapeDtypeStruct((LANES, LANES), F32),
                   jax.ShapeDtypeStruct((1, LANES), F32)),
        compiler_params=_params(),
    )(dc, dcm, fl, flm, bf)


_NT = (((1,), (1,)), ((), ()))


def _attn_specs(S, NX, qw, v_col0):
    mb = NX // META_BLK
    vb = v_col0 // qw
    return (pl.BlockSpec((S, qw), lambda b, p: (b, p)),
            pl.BlockSpec((META_BLK, qw), lambda b, p: (mb, p)),
            pl.BlockSpec((S, qw), lambda b, p: (b, vb + p)),
            pl.BlockSpec((META_BLK, qw), lambda b, p: (mb, vb + p)),
            pl.BlockSpec((S, LANES), lambda b, p: (b, p)))


def _cum_specs(S, TK):
    return [pl.BlockSpec((1, 2, S // TK, 1, TK), lambda b, p: (b, p, 0, 0, 0)),
            pl.BlockSpec((1, 2, 1, META_BLK), lambda b, p: (b, p, 0, 0))]


LOG2E = 1.4426950408889634


def _attn_fwd(qn, kn, vsrc, v_col0, mla, scale, S, NX, name, ck=None, cmk=None, exch=None):
    T = qn.shape[0]
    B = NX // S
    qw = 2 * LANES if mla else LANES
    npair = qn.shape[1] // qw
    TQ = min(512, S)
    TK = TQ
    forget = ck is not None
    a = scale * LOG2E

    def body(*refs):
        if forget:
            q_ref, k_ref, km_ref, v_ref, vm_ref, ck_ref, cmk_ref, _, o_ref, lse_ref = refs
        else:
            q_ref, k_ref, km_ref, v_ref, vm_ref, _, o_ref, lse_ref = refs
        lo = lax.broadcasted_iota(jnp.int32, (1, LANES), 1) < HALF
        mcol = lax.broadcasted_iota(jnp.int32, (TQ, META_BLK), 1)
        causal = (lax.broadcasted_iota(jnp.int32, (TQ, TK), 0)
                  >= lax.broadcasted_iota(jnp.int32, (TQ, TK), 1))
        two = lax.broadcasted_iota(jnp.int32, (TQ, 2), 1)
        for qi in range(S // TQ):
            q0 = qi * TQ
            sls = [slice(e * LANES, (e + 1) * LANES) if mla else slice(None) for e in range(2)]
            if mla:
                qts = [q_ref[q0:q0 + TQ, sl] for sl in sls]
            else:
                qts = [jnp.where(lo if e == 0 else ~lo, q_ref[q0:q0 + TQ, :], 0.0).astype(BF16)
                       for e in range(2)]

            def step(e, kt, vt, c2, mask, carry):
                m, l, acc = carry
                s = lax.dot_general(qts[e], kt, _NT, preferred_element_type=F32) * a
                if forget:
                    s = s - c2
                if mask is not None:
                    s = jnp.where(mask, s, NEG)
                m2 = jnp.max(s, axis=1, keepdims=True)
                if m is not None:
                    m2 = jnp.maximum(m, m2)
                p = jnp.exp2(s - m2)
                l2 = jnp.sum(p, axis=1, keepdims=True)
                acc2 = jnp.dot(p.astype(BF16), vt.astype(BF16), preferred_element_type=F32)
                if m is not None:
                    alpha = jnp.exp2(m - m2)
                    l2, acc2 = alpha * l + l2, alpha * acc + acc2
                return m2, l2, acc2

            def both(rows, kj, mask, carry):
                return tuple(step(e, k_ref[rows, sls[e]], v_ref[rows, sls[e]],
                                  ck_ref[0, e, kj] if forget else None, mask, carry[e]) for e in range(2))

            def below(kj, carry):
                return both(pl.ds(pl.multiple_of(kj * TK, TK), TK), kj, None, carry)

            carry = tuple(step(e, km_ref[:, sls[e]], vm_ref[:, sls[e]], cmk_ref[0, e] if forget else None,
                               mcol < N_META, (None, None, None)) for e in range(2))
            if qi:
                carry = lax.fori_loop(0, qi, below, carry)
            carry = both(slice(q0, q0 + TK), qi, causal, carry)
            outs = [acc / l for _, l, acc in carry]
            lses = [m + jnp.log2(l) for m, l, _ in carry]
            first = pltpu.roll(outs[0], HALF, 1) if mla else outs[0]
            o_ref[q0:q0 + TQ, :] = jnp.where(lo, first, outs[1])
            lse_ref[0, 0, q0:q0 + TQ, :] = jnp.where(two == 0, lses[0], lses[1])

    qk, kmeta, vv, vmeta, pair = _attn_specs(S, NX, qw, v_col0)
    specs = [qk, qk, kmeta, vv, vmeta]
    ins = [qn, kn, kn, vsrc, vsrc]
    if forget:
        specs += _cum_specs(S, TK)
        ins += [ck, cmk]
    specs.append(pl.BlockSpec(memory_space=pl.ANY))
    ins.append(jnp.zeros((T, npair * LANES), F32))
    lse_spec = pl.BlockSpec((1, 1, S, 2), lambda b, p: (b, p, 0, 0))
    return _call(
        body, name, (B, npair), specs, [pair, lse_spec],
        [jax.ShapeDtypeStruct((T, npair * LANES), F32), jax.ShapeDtypeStruct((B, npair, S, 2), F32)],
        ins, sem=("parallel", "parallel"), aliases={len(ins) - 1: 0}, exch=exch)


def _attn_bwd(qn, kn, vsrc, v_col0, o, lse, do, mla, scale, S, NX, name, ck=None, cmk=None, exch=None):
    T, W = qn.shape
    B = NX // S
    qw = 2 * LANES if mla else LANES
    npair = W // qw
    TQ = min(512, S)
    TK = TQ
    forget = ck is not None
    a = scale * LOG2E
    _TN = (((0,), (0,)), ((), ()))

    def body(*refs):
        refs = list(refs)
        q_ref, k_ref, km_ref, v_ref, vm_ref, o_ref, do_ref, lse_ref = refs[:8]
        pos = 8
        if forget:
            ck_ref, cmk_ref = refs[8:10]
            pos = 10
        pos += 3
        dq_ref, dk_ref, dv_ref, dkm_ref, dvm_ref = refs[pos:pos + 5]
        if forget:
            dck_ref, dcm_ref, dcq_ref = refs[pos + 5:pos + 8]
            dck_ref[...] = jnp.zeros_like(dck_ref)
            dcm_ref[...] = jnp.zeros_like(dcm_ref)
        dk_ref[...] = jnp.zeros_like(dk_ref)
        dv_ref[...] = jnp.zeros_like(dv_ref)
        dkm_ref[...] = jnp.zeros_like(dkm_ref)
        dvm_ref[...] = jnp.zeros_like(dvm_ref)
        lo = lax.broadcasted_iota(jnp.int32, (1, LANES), 1) < HALF
        mcol = lax.broadcasted_iota(jnp.int32, (TQ, META_BLK), 1)
        causal = (lax.broadcasted_iota(jnp.int32, (TQ, TK), 0)
                  >= lax.broadcasted_iota(jnp.int32, (TQ, TK), 1))
        two = lax.broadcasted_iota(jnp.int32, (TQ, 2), 1)
        for qi in range(S // TQ):
            q0 = qi * TQ
            dof = do_ref[q0:q0 + TQ, :]
            prod = dof * o_ref[q0:q0 + TQ, :]
            lse2 = lse_ref[0, 0, q0:q0 + TQ, :]
            sls = [slice(e * LANES, (e + 1) * LANES) if mla else slice(None) for e in range(2)]
            mine = [lo, ~lo]
            if mla:
                qts = [q_ref[q0:q0 + TQ, sl] for sl in sls]
                dots = [jnp.where(lo, 0.0, pltpu.roll(dof, HALF, 1) if e == 0 else dof).astype(BF16)
                        for e in range(2)]
            else:
                qts = [jnp.where(mine[e], q_ref[q0:q0 + TQ, :], 0.0).astype(BF16) for e in range(2)]
                dots = [jnp.where(mine[e], dof, 0.0).astype(BF16) for e in range(2)]
            deltas = [jnp.sum(jnp.where(mine[e], prod, 0.0), axis=1, keepdims=True) for e in range(2)]
            lse_ts = [jnp.sum(jnp.where(two == e, lse2, 0.0), axis=1, keepdims=True) for e in range(2)]

            def grads(e, kt, vt, c2, mask):
                s = lax.dot_general(qts[e], kt, _NT, preferred_element_type=F32) * a
                if forget:
                    s = s - c2
                p = jnp.exp2(s - lse_ts[e])
                if mask is not None:
                    p = jnp.where(mask, p, 0.0)
                dp = lax.dot_general(dots[e], vt, _NT, preferred_element_type=F32)
                ds = p * (dp - deltas[e])
                dsb = ds.astype(BF16)
                return (jnp.dot(dsb, kt, preferred_element_type=F32),
                        lax.dot_general(dsb, qts[e], _TN, preferred_element_type=F32) * scale,
                        lax.dot_general(p.astype(BF16), dots[e], _TN, preferred_element_type=F32),
                        -jnp.sum(ds, axis=0, keepdims=True) if forget else None,
                        jnp.sum(ds, axis=1, keepdims=True) if forget else None)

            def block(k_at, v_at, dk_at, dv_at, c_at, dc_at, mask, dqs):
                got = [grads(e, k_at(sls[e]), v_at(sls[e]).astype(BF16), c_at(e) if forget else None, mask)
                       for e in range(2)]
                if mla:
                    for e in range(2):
                        dk_at(sls[e], got[e][1])
                        dv_at(sls[e], got[e][2])
                else:
                    dk_at(sls[0], got[0][1] + got[1][1])
                    dv_at(sls[0], got[0][2] + got[1][2])
                if forget:
                    for e in range(2):
                        dc_at(e, got[e][3])
                picks = (0, 0, 4, 4) if forget else (0, 0)
                new = tuple(got[i % 2][k] for i, k in enumerate(picks))
                return new if dqs is None else tuple(x + y for x, y in zip(dqs, new))

            def add_to(ref, *lead):
                def add(*idx_and_val):
                    *idx, val = idx_and_val
                    ref[(*lead, *idx)] += val
                return add

            def token_block(rows, kj, mask, dqs):
                return block(lambda sl: k_ref[rows, sl], lambda sl: v_ref[rows, sl],
                             lambda sl, val: add_to(dk_ref)(rows, sl, val),
                             lambda sl, val: add_to(dv_ref)(rows, sl, val),
                             lambda e: ck_ref[0, e, kj], lambda e, val: add_to(dck_ref, 0)(e, kj, val),
                             mask, dqs)

            dqs = block(lambda sl: km_ref[:, sl], lambda sl: vm_ref[:, sl],
                        lambda sl, val: add_to(dkm_ref, 0)(slice(None), sl, val),
                        lambda sl, val: add_to(dvm_ref, 0)(slice(None), sl, val),
                        lambda e: cmk_ref[0, e], lambda e, val: add_to(dcm_ref, 0)(e, val),
                        mcol < N_META, None)

            def below(kj, dqs):
                return token_block(pl.ds(pl.multiple_of(kj * TK, TK), TK), kj, None, dqs)

            if qi:
                dqs = lax.fori_loop(0, qi, below, dqs)
            dqs = token_block(slice(q0, q0 + TK), qi, causal, dqs)
            if forget:
                dcq_ref[0, 0, q0:q0 + TQ, :] = jnp.where(two == 0, dqs[2], dqs[3])
            if mla:
                for e in range(2):
                    dq_ref[q0:q0 + TQ, sls[e]] = dqs[e] * scale
            else:
                dq_ref[q0:q0 + TQ, :] = jnp.where(lo, dqs[0], dqs[1]) * scale

    qk, kmeta, vv, vmeta, pair = _attn_specs(S, NX, qw, v_col0)
    lse_spec = pl.BlockSpec((1, 1, S, 2), lambda b, p: (b, p, 0, 0))
    specs = [qk, qk, kmeta, vv, vmeta, pair, pair, lse_spec]
    ins = [qn, kn, kn, vsrc, vsrc, o, do, lse]
    if forget:
        specs += _cum_specs(S, TK)
        ins += [ck, cmk]
    first_alias = len(ins)
    specs += [pl.BlockSpec(memory_space=pl.ANY)] * 3
    ins += [jnp.zeros((T, W), F32)] * 3
    mspec = pl.BlockSpec((1, META_BLK, qw), lambda b, p: (b, 0, p))
    out_specs = [qk, qk, qk, mspec, mspec]
    tok = jax.ShapeDtypeStruct((T, W), F32)
    met = jax.ShapeDtypeStruct((B, META_BLK, W), F32)
    out_shape = [tok, tok, tok, met, met]
    if forget:
        out_specs += _cum_specs(S, TK) + [lse_spec]
        out_shape += [jax.ShapeDtypeStruct((B, HEADS, S // TK, 1, TK), F32),
                      jax.ShapeDtypeStruct((B, HEADS, 1, META_BLK), F32),
                      jax.ShapeDtypeStruct((B, npair, S, 2), F32)]
    return _call(
        body, name, (B, npair), specs, out_specs, out_shape, ins, sem=("parallel", "parallel"),
        aliases={first_alias: 0, first_alias + 1: 1, first_alias + 2: 2}, exch=exch)


def _gate_fwd(z, bg, of, om, name):
    T, D = of.shape
    tm = _tile(T, 640, 16)

    def body(z_ref, bg_ref, of_ref, om_ref, o_ref):
        gt = jax.nn.sigmoid(z_ref[...] + bg_ref[...])
        o_ref[...] = (gt[:, :D] * of_ref[...] + gt[:, D:] * om_ref[...]).astype(BF16)

    row = pl.BlockSpec((tm, D), lambda i: (i, 0))
    return pl.pallas_call(
        body, name=name, grid=(T // tm,),
        in_specs=[pl.BlockSpec((tm, 2 * D), lambda i: (i, 0)),
                  pl.BlockSpec((1, 2 * D), lambda i: (0, 0)), row, row],
        out_specs=row, out_shape=jax.ShapeDtypeStruct((T, D), BF16),
        compiler_params=_params(("parallel",)),
    )(z, bg, of, om)


def _gate_bwd(dmix, z, bg, of, om, name):
    T, D = of.shape
    tm = _tile(T, 640, 16)

    def body(dm_ref, z_ref, bg_ref, of_ref, om_ref, dgl_ref, dof_ref, dom_ref, dbg_ref):
        gt = jax.nn.sigmoid(z_ref[...] + bg_ref[...])
        dm = dm_ref[...]
        dof_ref[...] = (dm * gt[:, :D]).astype(BF16)
        dom_ref[...] = (dm * gt[:, D:]).astype(BF16)
        dgl = jnp.concatenate([dm * of_ref[...], dm * om_ref[...]], axis=1) * gt * (1.0 - gt)
        dgl_ref[...] = dgl.astype(BF16)

        @pl.when(pl.program_id(0) == 0)
        def _():
            dbg_ref[...] = jnp.zeros_like(dbg_ref)

        dbg_ref[...] += jnp.sum(dgl, axis=0, keepdims=True)

    row = pl.BlockSpec((tm, D), lambda i: (i, 0))
    wide = pl.BlockSpec((tm, 2 * D), lambda i: (i, 0))
    one = pl.BlockSpec((1, 2 * D), lambda i: (0, 0))
    return pl.pallas_call(
        body, name=name, grid=(T // tm,),
        in_specs=[row, wide, one, row, row], out_specs=(wide, row, row, one),
        out_shape=(jax.ShapeDtypeStruct((T, 2 * D), BF16), jax.ShapeDtypeStruct((T, D), BF16),
                   jax.ShapeDtypeStruct((T, D), BF16), jax.ShapeDtypeStruct((1, 2 * D), F32)),
        compiler_params=_params(("arbitrary",)),
    )(dmix, z, bg, of, om)


def _loss(h, tgt, n_valid, name):
    T, D = h.shape
    tm = _tile(T, 640, 16)

    def body(h_ref, t_ref, dh_ref, l_ref):
        i = pl.program_id(0)
        rows = lax.broadcasted_iota(jnp.int32, (tm, D), 0) + i * tm
        err = jnp.where(rows < n_valid, h_ref[...] - t_ref[...], 0.0)
        dh_ref[...] = err * (1.0 / D)

        @pl.when(i == 0)
        def _():
            l_ref[...] = jnp.zeros_like(l_ref)

        l_ref[...] += 0.5 * jnp.sum(jnp.sum(err * err, axis=1, keepdims=True) * (1.0 / D))

    row = pl.BlockSpec((tm, D), lambda i: (i, 0))
    acc = pl.BlockSpec((8, LANES), lambda i: (0, 0))
    return pl.pallas_call(
        body, name=name, grid=(T // tm,), in_specs=[row, row], out_specs=(row, acc),
        out_shape=(jax.ShapeDtypeStruct((T, D), F32), jax.ShapeDtypeStruct((8, LANES), F32)),
        compiler_params=_params(("arbitrary",)),
    )(h, tgt)


def _adamw(parts, w, m, v, name):
    P, R, C = parts.shape
    tr = _tile(R, max(8, (1 << 18) // C), 8)
    bc1 = 1.0 - ADAM_B1 ** ADAM_STEP
    bc2 = 1.0 - ADAM_B2 ** ADAM_STEP

    def body(p_ref, w_ref, m_ref, v_ref, g_ref, d_ref, m2_ref, v2_ref):
        g = p_ref[0].astype(F32)
        for j in range(1, P):
            g = g + p_ref[j].astype(F32)
        m2 = ADAM_B1 * m_ref[...] + (1.0 - ADAM_B1) * g
        v2 = ADAM_B2 * v_ref[...] + (1.0 - ADAM_B2) * (g * g)
        m_hat = m2 / bc1
        v_hat = v2 / bc2
        g_ref[...] = g
        d_ref[...] = -ADAM_LR * (m_hat / (jnp.sqrt(v_hat) + ADAM_EPS) + ADAM_WD * w_ref[...])
        m2_ref[...] = m2
        v2_ref[...] = v2

    row = pl.BlockSpec((tr, C), lambda i: (i, 0))
    sh = jax.ShapeDtypeStruct((R, C), F32)
    return pl.pallas_call(
        body, name=name, grid=(R // tr,),
        in_specs=[pl.BlockSpec((P, tr, C), lambda i: (0, i, 0)), row, row, row],
        out_specs=(row, row, row, row), out_shape=(sh, sh, sh, sh),
        compiler_params=_params(("parallel",)),
    )(parts, w, m, v)


def _peer(d):
    x, y, c = lax.axis_index("x"), lax.axis_index("y"), lax.axis_index("c")
    px = 1 - x if d & 4 else x
    py = 1 - y if d & 2 else y
    pc = 1 - c if d & 1 else c
    return (px, py, pc), 4 * px + 2 * py + pc


class _Exchange:
    def __init__(self, srcs, gather):
        self.srcs, self.gather, self.n = list(srcs), gather, len(srcs)
        n = self.n
        hbm = pl.BlockSpec(memory_space=pl.ANY)
        self.in_specs = [hbm] * n
        self.out_specs = [hbm] * n
        self.out_shape = [jax.ShapeDtypeStruct((N_DEV,) + s.shape[-2:], s.dtype) for s in srcs]
        self.scratch = [pltpu.SemaphoreType.DMA((N_DEV - 1, n)), pltpu.SemaphoreType.DMA((N_DEV - 1, n)),
                        pltpu.SemaphoreType.DMA((n,))]

    def _copies(self, src_refs, out_refs, sems):
        send_sems, recv_sems, local_sems = sems
        _, me = _peer(0)

        def remote(w, d, landing):
            dev, lin = _peer(d)
            return pltpu.make_async_remote_copy(
                src_ref=src_refs[w] if self.gather else src_refs[w].at[lin],
                dst_ref=out_refs[w].at[lin if landing else me],
                send_sem=send_sems.at[d - 1, w], recv_sem=recv_sems.at[d - 1, w],
                device_id=dev, device_id_type=pl.DeviceIdType.MESH)

        pairs = [(w, d) for d in range(1, N_DEV) for w in range(self.n)]
        own = [pltpu.make_async_copy(src_refs[w] if self.gather else src_refs[w].at[me],
                                     out_refs[w].at[me], local_sems.at[w]) for w in range(self.n)]
        return own, [remote(w, d, False) for w, d in pairs], [remote(w, d, True) for w, d in pairs]

    def start(self, src_refs, out_refs, sems):
        own, sent, _ = self._copies(src_refs, out_refs, sems)
        for cp in own + sent:
            cp.start()

    def wait(self, src_refs, out_refs, sems):
        own, sent, landing = self._copies(src_refs, out_refs, sems)
        for cp in landing:
            cp.wait_recv()
        for cp in sent:
            cp.wait_send()
        for cp in own:
            cp.wait()


def _exchange(srcs, name, gather):
    ex = _Exchange(srcs, gather)
    n = ex.n

    def body(*refs):
        ex.start(refs[:n], refs[n:2 * n], refs[2 * n:])
        ex.wait(refs[:n], refs[n:2 * n], refs[2 * n:])

    outs = pl.pallas_call(
        body, name=name, in_specs=ex.in_specs, out_specs=tuple(ex.out_specs),
        out_shape=tuple(ex.out_shape), scratch_shapes=ex.scratch,
    )(*srcs)
    return list(outs)


def _call(body, name, grid, in_specs, out_specs, out_shape, ins, scratch_shapes=(), sem=None,
          aliases=None, exch=None):
    aliases = aliases or {}
    if exch is None:
        outs = pl.pallas_call(
            body, name=name, grid=grid, in_specs=list(in_specs), out_specs=tuple(out_specs),
            out_shape=tuple(out_shape), scratch_shapes=list(scratch_shapes),
            input_output_aliases=aliases, compiler_params=_params(sem),
        )(*ins)
        return list(outs), []
    ni, no, ns, n = len(in_specs), len(out_specs), len(scratch_shapes), exch.n
    last_ids = [g - 1 for g in grid]

    def hosted(*refs):
        cin, xin = refs[:ni], refs[ni:ni + n]
        cout, xout = refs[ni + n:ni + n + no], refs[ni + n + no:ni + 2 * n + no]
        cscr, xsem = refs[ni + 2 * n + no:ni + 2 * n + no + ns], refs[ni + 2 * n + no + ns:]
        ids = [pl.program_id(a) for a in range(len(grid))]
        first, last = ids[0] == 0, ids[0] == last_ids[0]
        for a in range(1, len(grid)):
            first, last = first & (ids[a] == 0), last & (ids[a] == last_ids[a])

        @pl.when(first)
        def _():
            exch.start(xin, xout, xsem)

        body(*cin, *cout, *cscr)

        @pl.when(last)
        def _():
            exch.wait(xin, xout, xsem)

    outs = pl.pallas_call(
        hosted, name=name, grid=grid, in_specs=list(in_specs) + exch.in_specs,
        out_specs=tuple(list(out_specs) + exch.out_specs),
        out_shape=tuple(list(out_shape) + exch.out_shape),
        scratch_shapes=list(scratch_shapes) + exch.scratch, input_output_aliases=aliases,
        compiler_params=_params(("arbitrary",) * len(grid)),
    )(*ins, *exch.srcs)
    return list(outs[:no]), list(outs[no:])


def _pack(arrs, cols, row_mult):
    flat = jnp.concatenate([a.reshape(-1) for a in arrs])
    n = flat.shape[0]
    quantum = cols * row_mult
    total = -(-n // quantum) * quantum
    return jnp.pad(flat, (0, total - n)).reshape(total // cols, cols)


def _pack_rows(arrs, cols, row_mult):
    flat = jnp.concatenate(arrs, axis=1)
    n = flat.shape[1]
    quantum = cols * row_mult
    total = -(-n // quantum) * quantum
    return jnp.pad(flat, ((0, 0), (0, total - n))).reshape(N_DEV, total // cols, cols)


def _unpack(packed, shapes):
    flat = packed.reshape(-1)
    out, off = [], 0
    for s in shapes:
        n = int(np.prod(s))
        out.append(flat[off:off + n].reshape(s))
        off += n
    return out


def _rope_tables(positions):
    inv_freq = ROPE_THETA ** (-jnp.arange(0, MLA_ROPE, 2, dtype=F32) / MLA_ROPE)
    ang = positions.astype(F32)[:, None] * inv_freq[None, :]
    cos, sin = jnp.cos(ang), jnp.sin(ang)
    n = positions.shape[0]
    ones, zeros = jnp.ones((n, MLA_NOPE), F32), jnp.zeros((n, MLA_NOPE), F32)
    tail1, tail0 = jnp.ones((n, LANES - MLA_QK), F32), jnp.zeros((n, LANES - MLA_QK), F32)
    z16 = jnp.zeros((n, 16), F32)
    c = jnp.concatenate([ones, cos, cos, tail1], axis=1)
    s1 = jnp.concatenate([zeros, -sin, z16, tail0], axis=1)
    s2 = jnp.concatenate([zeros, z16, sin, tail0], axis=1)
    return c, s1, s2


def kernel(x, meta_tokens, ffn1_norm, ffn1_w_gu, ffn1_w_down, mix_norm, w_in, b_forget, b_gate, fox_q_norm, fox_k_norm, mla_cq_norm, mla_w_uq, mla_ckv_norm, mla_w_ukv, mla_q_norm, mla_k_norm, w_branch_fox, w_branch_mla, w_out, ffn2_norm, ffn2_w_gu, ffn2_w_down, loss_target, m_meta_tokens, m_ffn1_norm, m_ffn1_w_gu, m_ffn1_w_down, m_mix_norm, m_w_in, m_b_forget, m_b_gate, m_fox_q_norm, m_fox_k_norm, m_mla_cq_norm, m_mla_w_uq, m_mla_ckv_norm, m_mla_w_ukv, m_mla_q_norm, m_mla_k_norm, m_w_branch_fox, m_w_branch_mla, m_w_out, m_ffn2_norm, m_ffn2_w_gu, m_ffn2_w_down, v_meta_tokens, v_ffn1_norm, v_ffn1_w_gu, v_ffn1_w_down, v_mix_norm, v_w_in, v_b_forget, v_b_gate, v_fox_q_norm, v_fox_k_norm, v_mla_cq_norm, v_mla_w_uq, v_mla_ckv_norm, v_mla_w_ukv, v_mla_q_norm, v_mla_k_norm, v_w_branch_fox, v_w_branch_mla, v_w_out, v_ffn2_norm, v_ffn2_w_gu, v_ffn2_w_down):
    names = ["meta_tokens", "ffn1_norm", "ffn1_w_gu", "ffn1_w_down", "mix_norm", "w_in", "b_forget",
             "b_gate", "fox_q_norm", "fox_k_norm", "mla_cq_norm", "mla_w_uq", "mla_ckv_norm",
             "mla_w_ukv", "mla_q_norm", "mla_k_norm", "w_branch_fox", "w_branch_mla", "w_out",
             "ffn2_norm", "ffn2_w_gu", "ffn2_w_down"]
    W = dict(zip(names, [meta_tokens, ffn1_norm, ffn1_w_gu, ffn1_w_down, mix_norm, w_in, b_forget,
                         b_gate, fox_q_norm, fox_k_norm, mla_cq_norm, mla_w_uq, mla_ckv_norm,
                         mla_w_ukv, mla_q_norm, mla_k_norm, w_branch_fox, w_branch_mla, w_out,
                         ffn2_norm, ffn2_w_gu, ffn2_w_down]))
    Mo = dict(zip(names, [m_meta_tokens, m_ffn1_norm, m_ffn1_w_gu, m_ffn1_w_down, m_mix_norm, m_w_in,
                          m_b_forget, m_b_gate, m_fox_q_norm, m_fox_k_norm, m_mla_cq_norm,
                          m_mla_w_uq, m_mla_ckv_norm, m_mla_w_ukv, m_mla_q_norm, m_mla_k_norm,
                          m_w_branch_fox, m_w_branch_mla, m_w_out, m_ffn2_norm, m_ffn2_w_gu,
                          m_ffn2_w_down]))
    Vo = dict(zip(names, [v_meta_tokens, v_ffn1_norm, v_ffn1_w_gu, v_ffn1_w_down, v_mix_norm, v_w_in,
                          v_b_forget, v_b_gate, v_fox_q_norm, v_fox_k_norm, v_mla_cq_norm,
                          v_mla_w_uq, v_mla_ckv_norm, v_mla_w_ukv, v_mla_q_norm, v_mla_k_norm,
                          v_w_branch_fox, v_w_branch_mla, v_w_out, v_ffn2_norm, v_ffn2_w_gu,
                          v_ffn2_w_down]))

    B, S, D = x.shape
    NX = B * S
    T = NX + META_BLK
    H = HEADS
    assert NX % META_BLK == 0 and S % LANES == 0
    me = 4 * lax.axis_index("x") + 2 * lax.axis_index("y") + lax.axis_index("c")

    big = [("ffn1_w_gu", 1), ("ffn1_w_down", 0), ("w_in", 1), ("mla_w_uq", 1), ("mla_w_ukv", 1),
           ("w_branch_fox", 1), ("w_branch_mla", 1), ("w_out", 0), ("ffn2_w_gu", 1), ("ffn2_w_down", 0)]
    mix_small = ["mla_w_uq", "mla_w_ukv", "w_branch_fox", "w_branch_mla", "w_out"]
    last_group = ["ffn2_w_gu", "ffn2_w_down"]
    axis_of = dict(big)
    full = {}

    def shards(group):
        return [W[n][0].astype(BF16) for n in group]

    def assemble(group, blks):
        for n, blk in zip(group, blks):
            _, r, c = blk.shape
            full[n] = (blk.transpose(1, 0, 2).reshape(r, N_DEV * c) if axis_of[n] == 1
                       else blk.reshape(N_DEV * r, c))

    got = _exchange(shards(["ffn1_w_gu"]) + [meta_tokens], "gather_first", gather=True)
    assemble(["ffn1_w_gu"], got[:1])
    meta_full = got[1].transpose(1, 0, 2).reshape(N_META, D)

    Z_G, Z_FQ = 0, 2 * D
    Z_FK, Z_FV = Z_FQ + FOX_W, Z_FQ + 2 * FOX_W
    Z_CQ = Z_FQ + 3 * FOX_W
    Z_CKV = Z_CQ + MLA_Q_RANK
    Z_F = Z_CKV + MLA_KV_RANK
    Z_KR = Z_F + LANES

    def pad_lanes(a, w=LANES):
        return jnp.pad(a, [(0, 0)] * (a.ndim - 1) + [(0, w - a.shape[-1])])

    def rows_T(real, meta=None):
        n = real.shape[1]
        parts = [real]
        used = 0
        if meta is not None:
            parts.append(meta)
            used = meta.shape[0]
        if T - NX - used:
            parts.append(jnp.zeros((T - NX - used, n), real.dtype))
        return jnp.concatenate(parts, axis=0)

    def put_meta(tok, meta_per_seq):
        return lax.dynamic_update_slice(tok, meta_per_seq.sum(0), (NX, 0))

    h0 = rows_T(x.reshape(NX, D), meta_full)
    tgt = rows_T(loss_target.reshape(NX, D))

    def ffn_fwd(h, norm, w_gu, tag, behind_up=None, behind_down=None):
        u = _norm_fwd(h, 0, D, D, norm, D, D, tag + "_norm")
        (g, up, a), got = _ffn_up(u, w_gu, tag + "_up",
                                  exch=_Exchange(shards(behind_up), True) if behind_up else None)
        assemble(behind_up or [], got)
        h_out = _mm(a, full[tag + "_w_down"], "nn", tag + "_down", scale=0.5, res=h,
                    exch=_Exchange(shards(behind_down), True) if behind_down else None)
        if behind_down:
            h_out, got = h_out
            assemble(behind_down, got)
        return h_out, (u, g, up, a)

    h1, ffn1_saved = ffn_fwd(h0, W["ffn1_norm"], full["ffn1_w_gu"], "ffn1",
                             behind_up=["ffn1_w_down"], behind_down=["w_in"])
    wi = full["w_in"]
    o_fq = 0
    o_f = 3 * FOX_W
    o_cq = o_f + HEADS
    o_kr = o_cq + MLA_Q_RANK + MLA_KV_RANK
    o_g = o_kr + MLA_ROPE
    w_in_p = jnp.concatenate([
        wi[:, o_g:o_g + 2 * D], wi[:, o_fq:o_f], wi[:, o_cq:o_kr],
        jnp.pad(wi[:, o_f:o_cq], ((0, 0), (0, LANES - HEADS))),
        jnp.pad(wi[:, o_kr:o_g], ((0, 0), (0, LANES - MLA_ROPE)))], axis=1)

    u2 = _norm_fwd(h1, 0, D, D, W["mix_norm"], D, D, "mix_norm")
    z, got = _mm(u2, w_in_p, "nn", "w_in", exch=_Exchange(shards(mix_small), True))
    assemble(mix_small, got)
    w_uq_p = jnp.pad(full["mla_w_uq"].reshape(MLA_Q_RANK, H, MLA_QK),
                     ((0, 0), (0, 0), (0, LANES - MLA_QK))).reshape(MLA_Q_RANK, H * LANES)

    gq_f = jnp.tile(W["fox_q_norm"], (1, 2))
    gk_f = jnp.tile(W["fox_k_norm"], (1, 2))
    fqn = _norm_fwd(z, Z_FQ, FOX_W, LANES, gq_f, FOX_HD, FOX_HD, "fox_q_norm")
    fkn = _norm_fwd(z, Z_FK, FOX_W, LANES, gk_f, FOX_HD, FOX_HD, "fox_k_norm")
    fl = z[:NX, Z_F:Z_F + LANES].reshape(B, S, LANES)
    flm = z[NX:, Z_F:Z_F + LANES]
    bf = pad_lanes(W["b_forget"])
    cum, cumm = _cum_fwd(fl, flm, bf, "forget_cum")
    TK = min(512, S)
    ck = cum[:, :, :H].transpose(0, 2, 1).reshape(B, H, S // TK, 1, TK)
    cmk = jnp.broadcast_to(cumm[:, :H].T[None, :, None, :], (B, H, 1, META_BLK))
    (o_fox, lse_fox), got = _attn_fwd(fqn, fkn, z, Z_FV, False, FOX_HD ** -0.5, S, NX, "fox_attn", ck, cmk,
                                      exch=_Exchange(shards(last_group), True))
    assemble(last_group, got)
    of = _mm(o_fox, full["w_branch_fox"], "nn", "branch_fox")

    pos = jnp.concatenate([jnp.tile(jnp.arange(S) + N_META, B), jnp.arange(META_BLK)])
    tabs = _rope_tables(pos)
    cqn = _norm_fwd(z, Z_CQ, MLA_Q_RANK, MLA_Q_RANK, W["mla_cq_norm"], MLA_Q_RANK, MLA_Q_RANK, "mla_cq_norm")
    q_lin = _mm(cqn, w_uq_p, "nn", "mla_uq")
    ckvn = _norm_fwd(z, Z_CKV, MLA_KV_RANK, MLA_KV_RANK, W["mla_ckv_norm"], MLA_KV_RANK, MLA_KV_RANK,
                     "mla_ckv_norm")
    kv_lin = _mm(ckvn, full["mla_w_ukv"], "nn", "mla_ukv")
    gq_m, gk_m = pad_lanes(W["mla_q_norm"]), pad_lanes(W["mla_k_norm"])
    mqn = _norm_fwd(q_lin, 0, H * LANES, LANES, gq_m, LANES, MLA_QK, "mla_q_norm", tabs=tabs)
    mkn = _mla_k_fwd(kv_lin, z, Z_KR, gk_m, tabs, "mla_k_norm")
    (o_mla, lse_mla), _ = _attn_fwd(mqn, mkn, kv_lin, 0, True, MLA_QK ** -0.5, S, NX, "mla_attn")
    om = _mm(o_mla, full["w_branch_mla"], "nn", "branch_mla")

    mix = _gate_fwd(z, W["b_gate"], of, om, "gate_mix")
    h2 = _mm(mix, full["w_out"], "nn", "w_out", res=h1)

    h3, ffn2_saved = ffn_fwd(h2, W["ffn2_norm"], full["ffn2_w_gu"], "ffn2")

    dh3, loss_acc = _loss(h3, tgt, NX, "loss")
    loss = lax.psum(loss_acc[0, 0], AXES)

    G = {}
    parts = {}

    def scatter_of(group):
        per_dest = []
        for n in group:
            r, c = W[n].shape[1:]
            per_dest.append((G[n].reshape(r, N_DEV, c).transpose(1, 0, 2) if axis_of[n] == 1
                             else G[n].reshape(N_DEV, r, c)).astype(BF16))
        return _Exchange(per_dest, False)

    def ffn_bwd(dh, h, norm, w_gu, w_down, saved, tag, behind_down=None, spread=False):
        u, g, up, a = saved
        G[tag + "_w_down"] = _mm(a, dh, "tn", tag + "_dw_down", scale=0.5)
        (dg, dup), got = _ffn_down_bwd(dh, w_down, g, up, tag + "_down_bwd",
                                       exch=scatter_of(behind_down) if behind_down else None)
        parts.update(zip(behind_down or [], got))
        dw_g = _mm(u, dg, "tn", tag + "_dw_g", exch=scatter_of([tag + "_w_down"]) if spread else None)
        if spread:
            dw_g, got = dw_g
            parts[tag + "_w_down"] = got[0]
        G[tag + "_w_gu"] = jnp.concatenate([dw_g, _mm(u, dup, "tn", tag + "_dw_u")], axis=1)
        (du,), got = _ffn_up_bwd_dx(dg, dup, w_gu, tag + "_up_bwd",
                                    exch=scatter_of([tag + "_w_gu"]) if spread else None)
        if spread:
            parts[tag + "_w_gu"] = got[0]
        dh_in, G[tag + "_norm"] = _norm_bwd(h, 0, D, D, norm, D, D, du, tag + "_norm_bwd", res=dh)
        return dh_in

    dh2 = ffn_bwd(dh3, h2, W["ffn2_norm"], full["ffn2_w_gu"], full["ffn2_w_down"], ffn2_saved, "ffn2")

    G["w_out"] = _mm(mix, dh2, "tn", "dw_out")
    dmix = _mm(dh2, full["w_out"], "nt", "w_out_bwd")
    dgl, dof, dom, G["b_gate"] = _gate_bwd(dmix, z, W["b_gate"], of, om, "gate_bwd")

    G["w_branch_fox"] = _mm(o_fox, dof, "tn", "dw_branch_fox")
    do_fox = _mm(dof, full["w_branch_fox"], "nt", "branch_fox_bwd")
    (dq_f, dk_f, dv_f, dkm_f, dvm_f, dck, dcmk, dcq), got = _attn_bwd(
        fqn, fkn, z, Z_FV, o_fox, lse_fox, do_fox, False, FOX_HD ** -0.5, S, NX, "fox_attn_bwd", ck, cmk,
        exch=scatter_of(last_group))
    parts.update(zip(last_group, got))
    dk_f, dv_f = put_meta(dk_f, dkm_f), put_meta(dv_f, dvm_f)
    dfq, gq = _norm_bwd(z, Z_FQ, FOX_W, LANES, gq_f, FOX_HD, FOX_HD, dq_f, "fox_q_norm_bwd", out_dtype=BF16)
    dfk, gk = _norm_bwd(z, Z_FK, FOX_W, LANES, gk_f, FOX_HD, FOX_HD, dk_f, "fox_k_norm_bwd", out_dtype=BF16)
    G["fox_q_norm"] = gq[:, :FOX_HD] + gq[:, FOX_HD:]
    G["fox_k_norm"] = gk[:, :FOX_HD] + gk[:, FOX_HD:]
    dc = pad_lanes(dck.reshape(B, H, S).transpose(0, 2, 1)
                   + dcq.transpose(0, 2, 1, 3).reshape(B, S, H))
    dcm = pad_lanes(dcmk.sum(0)[:, 0, :].T)
    dcm = jnp.where(jnp.arange(LANES)[:, None] < N_META, dcm, 0.0)
    dfl, dflm, dbf = _cum_bwd(dc, dcm, fl, flm, bf, "forget_cum_bwd")
    G["b_forget"] = dbf[:, :HEADS]
    dfl_t = rows_T(dfl.reshape(NX, LANES), dflm)

    G["w_branch_mla"] = _mm(o_mla, dom, "tn", "dw_branch_mla")
    do_mla = _mm(dom, full["w_branch_mla"], "nt", "branch_mla_bwd")
    (dq_m, dk_m, dvk, dkm_m, dvkm), _ = _attn_bwd(
        mqn, mkn, kv_lin, 0, o_mla, lse_mla, do_mla, True, MLA_QK ** -0.5, S, NX, "mla_attn_bwd")
    dk_m, dvk = put_meta(dk_m, dkm_m), put_meta(dvk, dvkm)
    dq_lin, gq = _norm_bwd(q_lin, 0, H * LANES, LANES, gq_m, LANES, MLA_QK, dq_m, "mla_q_norm_bwd", tabs=tabs)
    G["mla_q_norm"] = gq[:, :MLA_QK]
    G["mla_w_uq"] = _mm(cqn, dq_lin, "tn", "dw_uq").reshape(MLA_Q_RANK, H, LANES)[:, :, :MLA_QK].reshape(
        MLA_Q_RANK, H * MLA_QK)
    dcqn = _mm(dq_lin, w_uq_p, "nt", "mla_uq_bwd")
    dcq, G["mla_cq_norm"] = _norm_bwd(z, Z_CQ, MLA_Q_RANK, MLA_Q_RANK, W["mla_cq_norm"], MLA_Q_RANK,
                                      MLA_Q_RANK, dcqn, "mla_cq_norm_bwd", out_dtype=BF16)
    dkv_lin, dkr, gk = _mla_k_bwd(kv_lin, z, Z_KR, gk_m, tabs, dk_m, dvk, "mla_k_norm_bwd")
    G["mla_k_norm"] = gk[:, :MLA_QK]
    G["mla_w_ukv"] = _mm(ckvn, dkv_lin, "tn", "dw_ukv")
    dckvn = _mm(dkv_lin, full["mla_w_ukv"], "nt", "mla_ukv_bwd")
    dckv, G["mla_ckv_norm"] = _norm_bwd(z, Z_CKV, MLA_KV_RANK, MLA_KV_RANK, W["mla_ckv_norm"], MLA_KV_RANK,
                                        MLA_KV_RANK, dckvn, "mla_ckv_norm_bwd", out_dtype=BF16)

    dz = jnp.concatenate([dgl, dfq, dfk, dv_f.astype(BF16), dcq, dckv, dfl_t.astype(BF16),
                          dkr.astype(BF16)], axis=1)
    dw_in_p = _mm(u2, dz, "tn", "dw_in")
    G["w_in"] = jnp.concatenate([
        dw_in_p[:, Z_FQ:Z_CQ], dw_in_p[:, Z_F:Z_F + HEADS], dw_in_p[:, Z_CQ:Z_F],
        dw_in_p[:, Z_KR:Z_KR + MLA_ROPE], dw_in_p[:, Z_G:Z_G + 2 * D]], axis=1)
    du2, got = _mm(dz, w_in_p, "nt", "w_in_bwd", exch=scatter_of(mix_small))
    parts.update(zip(mix_small, got))
    dh1, G["mix_norm"] = _norm_bwd(h1, 0, D, D, W["mix_norm"], D, D, du2, "mix_norm_bwd", res=dh2)

    dh0 = ffn_bwd(dh1, h0, W["ffn1_norm"], full["ffn1_w_gu"], full["ffn1_w_down"], ffn1_saved, "ffn1",
                  behind_down=["w_in"], spread=True)
    grad_x = dh0[:NX].reshape(B, S, D)
    G["meta_tokens"] = dh0[NX:NX + N_META]

    res = {}
    for n, _ in big:
        outs4 = _adamw(parts[n], W[n][0], Mo[n][0], Vo[n][0], "adamw_" + n)
        for key, arr in zip(("g", "d", "m", "v"), outs4):
            res[key, n] = arr[None]

    small = [n for n in names if n not in dict(big) and n != "meta_tokens"]
    small_shapes = [W[n].shape for n in small]
    spack = _pack([G["meta_tokens"]] + [G[n] for n in small], 1024, 8)
    (sparts,) = _exchange([spack], "gather_small_grads", gather=True)
    sflat = sparts.reshape(N_DEV, -1)
    dsh = D // N_DEV
    meta_part = lax.dynamic_slice(sflat[:, :N_META * D].reshape(N_DEV, N_META, D),
                                  (0, 0, me * dsh), (N_DEV, N_META, dsh)).reshape(N_DEV, -1)
    rep_len = sum(int(np.prod(s)) for s in small_shapes)
    rep_part = sflat[:, N_META * D:N_META * D + rep_len]
    sp = _pack_rows([meta_part, rep_part], LANES, 8)
    pks = lambda src: _pack([src["meta_tokens"]] + [src[n] for n in small], LANES, 8)
    g_s, d_s, m_s, v_s = _adamw(sp, pks(W), pks(Mo), pks(Vo), "adamw_small")
    shapes_s = [W["meta_tokens"].shape] + small_shapes
    for key, packed in (("g", g_s), ("d", d_s), ("m", m_s), ("v", v_s)):
        for n, arr in zip(["meta_tokens"] + small, _unpack(packed, shapes_s)):
            res[key, n] = arr

    outs = [loss, grad_x]
    for key in ("g", "d", "m", "v"):
        outs += [res[key, n] for n in names]
    return tuple(outs)
```

```python
import numpy as np
import jax
import jax.numpy as jnp
from jax import lax
from jax.experimental import pallas as pl
from jax.experimental.pallas import tpu as pltpu

F32 = jnp.float32
BF16 = jnp.bfloat16

N_META = 16
EPS = 1e-6
HEADS = 8
FOX_HD = 64
FOX_W = HEADS * FOX_HD
MLA_Q_RANK = 256
MLA_KV_RANK = 128
MLA_NOPE = 64
MLA_ROPE = 32
MLA_QK = MLA_NOPE + MLA_ROPE
MLA_V = 64
ROPE_THETA = 10000.0
LANES = 128
HALF = LANES // 2
META_BLK = 128
NEG = -1e30

ADAM_LR = 0.001
ADAM_B1 = 0.9
ADAM_B2 = 0.999
ADAM_EPS = 1e-08
ADAM_WD = 0.01
ADAM_STEP = 10

N_DEV = 8
AXES = ("x", "y", "c")
VMEM_LIMIT_BYTES = 56 * 1024 * 1024


def _tile(n, cap, mult):
    best = None
    for d in range(mult, min(n, cap) + 1, mult):
        if n % d == 0:
            best = d
    return n if best is None else best


def _row_tile(rows, width):
    return _tile(rows, max(16, (1 << 19) // width), 16)


def _params(sem=None):
    return pltpu.CompilerParams(dimension_semantics=sem, vmem_limit_bytes=VMEM_LIMIT_BYTES)


def _mm(a, b, mode, name, out_dtype=F32, scale=1.0, res=None, exch=None):
    if mode == "nn":
        (M, K), (K2, N) = a.shape, b.shape
    elif mode == "nt":
        (M, K), (N, K2) = a.shape, b.shape
    else:
        (K, M), (K2, N) = a.shape, b.shape
    assert K == K2, (a.shape, b.shape, mode)
    if mode == "tn":
        tm, tk = _tile(M, 1408, 128), _tile(K, 2080, 16)
    else:
        tm, tk = _tile(M, 640, 16), _tile(K, 4224, 128)
    tn = _tile(N, 1408, 128)
    nk = K // tk
    a_spec = {"nn": pl.BlockSpec((tm, tk), lambda i, j, k: (i, k)),
              "nt": pl.BlockSpec((tm, tk), lambda i, j, k: (i, k)),
              "tn": pl.BlockSpec((tk, tm), lambda i, j, k: (k, i))}[mode]
    b_spec = {"nn": pl.BlockSpec((tk, tn), lambda i, j, k: (k, j)),
              "nt": pl.BlockSpec((tn, tk), lambda i, j, k: (j, k)),
              "tn": pl.BlockSpec((tk, tn), lambda i, j, k: (k, j))}[mode]
    dims = {"nn": (((1,), (0,)), ((), ())), "nt": (((1,), (1,)), ((), ())),
            "tn": (((0,), (0,)), ((), ()))}[mode]
    o_spec = pl.BlockSpec((tm, tn), lambda i, j, k: (i, j))
    has_res = res is not None

    def body(*refs):
        a_ref, b_ref = refs[:2]
        r_ref = refs[2] if has_res else None
        o_ref = refs[2 + has_res]

        def finish(acc):
            o = acc * scale
            if has_res:
                o = o + r_ref[...]
            o_ref[...] = o.astype(out_dtype)

        prod = lax.dot_general(a_ref[...].astype(BF16), b_ref[...].astype(BF16), dims,
                               preferred_element_type=F32)
        if nk == 1:
            finish(prod)
            return
        acc_ref = refs[3 + has_res]
        k = pl.program_id(2)

        @pl.when(k == 0)
        def _():
            acc_ref[...] = prod

        @pl.when((k > 0) & (k < nk - 1))
        def _():
            acc_ref[...] += prod

        @pl.when(k == nk - 1)
        def _():
            finish(acc_ref[...] + prod)

    ins = [a, b] + ([res] if has_res else [])
    specs = [a_spec, b_spec] + ([o_spec] if has_res else [])
    (out,), got = _call(
        body, name, (M // tm, N // tn, nk), specs, [o_spec], [jax.ShapeDtypeStruct((M, N), out_dtype)],
        ins, scratch_shapes=[pltpu.VMEM((tm, tn), F32)] if nk > 1 else [],
        sem=("parallel", "parallel", "arbitrary"), exch=exch)
    return out if exch is None else (out, got)


def _rope_fwd(y, c, s1, s2):
    return y * c + pltpu.roll(y, LANES - 16, 1) * s1 + pltpu.roll(y, 16, 1) * s2


def _rope_bwd(dy, c, s1, s2):
    return dy * c + pltpu.roll(dy * s1, 16, 1) + pltpu.roll(dy * s2, LANES - 16, 1)


def _group_sum(v, seg):
    if seg == v.shape[-1]:
        return jnp.sum(v, axis=-1, keepdims=True)
    lo = lax.broadcasted_iota(jnp.int32, v.shape, 1) < seg
    s_lo = jnp.sum(jnp.where(lo, v, 0.0), axis=-1, keepdims=True)
    s_hi = jnp.sum(jnp.where(lo, 0.0, v), axis=-1, keepdims=True)
    return jnp.where(lo, s_lo, s_hi)


def _norm_fwd(src, col0, width, bw, gain, seg, d_true, name, tabs=None, out_dtype=BF16):
    T = src.shape[0]
    tr = _row_tile(T, bw)
    inv_d = 1.0 / d_true
    c0 = col0 // bw
    assert col0 % bw == 0 and width % bw == 0

    def body(*refs):
        if tabs is None:
            x_ref, g_ref, o_ref = refs
        else:
            x_ref, g_ref, c_ref, s1_ref, s2_ref, o_ref = refs
        xv = x_ref[...]
        r = lax.rsqrt(_group_sum(xv * xv, seg) * inv_d + EPS)
        y = xv * r * g_ref[...]
        if tabs is not None:
            y = _rope_fwd(y, c_ref[...], s1_ref[...], s2_ref[...])
        o_ref[...] = y.astype(out_dtype)

    specs = [pl.BlockSpec((tr, bw), lambda i, j: (i, c0 + j)), pl.BlockSpec((1, bw), lambda i, j: (0, 0))]
    ins = [src, gain]
    if tabs is not None:
        tab = pl.BlockSpec((tr, LANES), lambda i, j: (i, 0))
        specs += [tab, tab, tab]
        ins += list(tabs)
    return pl.pallas_call(
        body, name=name, grid=(T // tr, width // bw), in_specs=specs,
        out_specs=pl.BlockSpec((tr, bw), lambda i, j: (i, j)),
        out_shape=jax.ShapeDtypeStruct((T, width), out_dtype),
        compiler_params=_params(("parallel", "parallel")),
    )(*ins)


def _norm_bwd_math(xv, gain, dyv, seg, inv_d):
    r = lax.rsqrt(_group_sum(xv * xv, seg) * inv_d + EPS)
    gy = dyv * gain
    dot = _group_sum(gy * xv, seg)
    dx = r * gy - xv * (r * r * r * inv_d) * dot
    return dx, jnp.sum(dyv * xv * r, axis=0, keepdims=True)


def _norm_bwd(src, col0, width, bw, gain, seg, d_true, dy, name, tabs=None, res=None, out_dtype=F32):
    T = src.shape[0]
    tr = _row_tile(T, bw)
    inv_d = 1.0 / d_true
    c0 = col0 // bw
    has_res = res is not None

    def body(*refs):
        refs = list(refs)
        x_ref, g_ref, dy_ref = refs[:3]
        pos = 3
        if tabs is not None:
            c_ref, s1_ref, s2_ref = refs[3:6]
            pos = 6
        if has_res:
            r_ref = refs[pos]
            pos += 1
        dx_ref, dg_ref = refs[pos], refs[pos + 1]
        dyv = dy_ref[...].astype(F32)
        if tabs is not None:
            dyv = _rope_bwd(dyv, c_ref[...], s1_ref[...], s2_ref[...])
        dx, dg = _norm_bwd_math(x_ref[...], g_ref[...], dyv, seg, inv_d)
        if has_res:
            dx = dx + r_ref[...]
        dx_ref[...] = dx.astype(out_dtype)

        @pl.when((pl.program_id(0) == 0) & (pl.program_id(1) == 0))
        def _():
            dg_ref[...] = jnp.zeros_like(dg_ref)

        dg_ref[...] += dg

    blk = pl.BlockSpec((tr, bw), lambda i, j: (i, j))
    one = pl.BlockSpec((1, bw), lambda i, j: (0, 0))
    specs = [pl.BlockSpec((tr, bw), lambda i, j: (i, c0 + j)), one, blk]
    ins = [src, gain, dy]
    if tabs is not None:
        tab = pl.BlockSpec((tr, LANES), lambda i, j: (i, 0))
        specs += [tab, tab, tab]
        ins += list(tabs)
    if has_res:
        specs.append(blk)
        ins.append(res)
    return pl.pallas_call(
        body, name=name, grid=(T // tr, width // bw), in_specs=specs, out_specs=(blk, one),
        out_shape=(jax.ShapeDtypeStruct((T, width), out_dtype), jax.ShapeDtypeStruct((1, bw), F32)),
        compiler_params=_params(("arbitrary", "arbitrary")),
    )(*ins)


def _mla_k_raw(kv, kr):
    lane = lax.broadcasted_iota(jnp.int32, kv.shape, 1)
    return jnp.where(lane < MLA_NOPE, kv, jnp.where(lane < MLA_QK, pltpu.roll(kr, MLA_NOPE, 1), 0.0))


def _mla_k_fwd(kv_lin, z, kr_col, gain, tabs, name):
    T, W = kv_lin.shape
    tr = _row_tile(T, LANES)
    krb = kr_col // LANES
    inv_d = 1.0 / MLA_QK

    def body(kv_ref, kr_ref, g_ref, c_ref, s1_ref, s2_ref, o_ref):
        xv = _mla_k_raw(kv_ref[...], kr_ref[...])
        r = lax.rsqrt(jnp.sum(xv * xv, axis=-1, keepdims=True) * inv_d + EPS)
        o_ref[...] = _rope_fwd(xv * r * g_ref[...], c_ref[...], s1_ref[...], s2_ref[...]).astype(BF16)

    blk = pl.BlockSpec((tr, LANES), lambda i, h: (i, h))
    tab = pl.BlockSpec((tr, LANES), lambda i, h: (i, 0))
    return pl.pallas_call(
        body, name=name, grid=(T // tr, W // LANES),
        in_specs=[blk, pl.BlockSpec((tr, LANES), lambda i, h: (i, krb)),
                  pl.BlockSpec((1, LANES), lambda i, h: (0, 0)), tab, tab, tab],
        out_specs=blk, out_shape=jax.ShapeDtypeStruct((T, W), BF16),
        compiler_params=_params(("parallel", "parallel")),
    )(kv_lin, z, gain, *tabs)


def _mla_k_bwd(kv_lin, z, kr_col, gain, tabs, dk, dvk, name):
    T, W = kv_lin.shape
    tr = _row_tile(T, LANES)
    krb = kr_col // LANES
    inv_d = 1.0 / MLA_QK

    def body(kv_ref, kr_ref, g_ref, c_ref, s1_ref, s2_ref, dk_ref, dvk_ref, dkv_ref, dkr_ref, dg_ref):
        h = pl.program_id(1)
        xv = _mla_k_raw(kv_ref[...], kr_ref[...])
        dyv = _rope_bwd(dk_ref[...], c_ref[...], s1_ref[...], s2_ref[...])
        dx, dg = _norm_bwd_math(xv, g_ref[...], dyv, LANES, inv_d)
        lane = lax.broadcasted_iota(jnp.int32, dx.shape, 1)
        dkv_ref[...] = jnp.where(lane < MLA_NOPE, dx, dvk_ref[...])
        part = pltpu.roll(jnp.where((lane >= MLA_NOPE) & (lane < MLA_QK), dx, 0.0), LANES - MLA_NOPE, 1)

        @pl.when(h == 0)
        def _():
            dkr_ref[...] = jnp.zeros_like(dkr_ref)

        dkr_ref[...] += part

        @pl.when((pl.program_id(0) == 0) & (h == 0))
        def _():
            dg_ref[...] = jnp.zeros_like(dg_ref)

        dg_ref[...] += dg

    blk = pl.BlockSpec((tr, LANES), lambda i, h: (i, h))
    tab = pl.BlockSpec((tr, LANES), lambda i, h: (i, 0))
    one = pl.BlockSpec((1, LANES), lambda i, h: (0, 0))
    return pl.pallas_call(
        body, name=name, grid=(T // tr, W // LANES),
        in_specs=[blk, pl.BlockSpec((tr, LANES), lambda i, h: (i, krb)), one, tab, tab, tab, blk, blk],
        out_specs=(blk, tab, one),
        out_shape=(jax.ShapeDtypeStruct((T, W), F32), jax.ShapeDtypeStruct((T, LANES), F32),
                   jax.ShapeDtypeStruct((1, LANES), F32)),
        compiler_params=_params(("arbitrary", "arbitrary")),
    )(kv_lin, z, gain, *tabs, dk, dvk)


def _ffn_up(u, w_gu, name, exch=None):
    T, D = u.shape
    F = w_gu.shape[1] // 2
    tm, tn = _tile(T, 640, 16), _tile(F, 1408, 128)
    nj = F // tn

    def body(u_ref, wg_ref, wu_ref, g_ref, up_ref, a_ref):
        uv = u_ref[...]
        g = jnp.dot(uv, wg_ref[...], preferred_element_type=F32)
        up = jnp.dot(uv, wu_ref[...], preferred_element_type=F32)
        g_ref[...] = g.astype(BF16)
        up_ref[...] = up.astype(BF16)
        a_ref[...] = (g * jax.nn.sigmoid(g) * up).astype(BF16)

    o_spec = pl.BlockSpec((tm, tn), lambda i, j: (i, j))
    sh = jax.ShapeDtypeStruct((T, F), BF16)
    return _call(
        body, name, (T // tm, nj),
        [pl.BlockSpec((tm, D), lambda i, j: (i, 0)),
         pl.BlockSpec((D, tn), lambda i, j: (0, j)),
         pl.BlockSpec((D, tn), lambda i, j: (0, j + nj))],
        [o_spec, o_spec, o_spec], [sh, sh, sh], [u, w_gu, w_gu],
        sem=("parallel", "parallel"), exch=exch)


def _ffn_down_bwd(dh, w_down, g, up, name, exch=None):
    T, D = dh.shape
    F = w_down.shape[0]
    tm, tn = _tile(T, 640, 16), _tile(F, 1408, 128)

    def body(dh_ref, w_ref, g_ref, up_ref, dg_ref, dup_ref):
        da = 0.5 * lax.dot_general(dh_ref[...].astype(BF16), w_ref[...], (((1,), (1,)), ((), ())),
                                   preferred_element_type=F32)
        gv = g_ref[...].astype(F32)
        sg = jax.nn.sigmoid(gv)
        silu = gv * sg
        dup_ref[...] = (da * silu).astype(BF16)
        dg_ref[...] = (da * up_ref[...].astype(F32) * (sg + silu * (1.0 - sg))).astype(BF16)

    t_spec = pl.BlockSpec((tm, tn), lambda i, j: (i, j))
    sh = jax.ShapeDtypeStruct((T, F), BF16)
    return _call(
        body, name, (T // tm, F // tn),
        [pl.BlockSpec((tm, D), lambda i, j: (i, 0)),
         pl.BlockSpec((tn, D), lambda i, j: (j, 0)), t_spec, t_spec],
        [t_spec, t_spec], [sh, sh], [dh, w_down, g, up],
        sem=("parallel", "parallel"), exch=exch)


def _ffn_up_bwd_dx(dg, dup, w_gu, name, exch=None):
    T, F = dg.shape
    D = w_gu.shape[0]
    tm, tk = _tile(T, 640, 16), _tile(F, 1408, 128)
    nk = F // tk
    nt = (((1,), (1,)), ((), ()))

    def body(dg_ref, dup_ref, wg_ref, wu_ref, o_ref, acc_ref):
        k = pl.program_id(1)
        prod = (lax.dot_general(dg_ref[...], wg_ref[...], nt, preferred_element_type=F32)
                + lax.dot_general(dup_ref[...], wu_ref[...], nt, preferred_element_type=F32))
        if nk == 1:
            o_ref[...] = prod
            return

        @pl.when(k == 0)
        def _():
            acc_ref[...] = prod

        @pl.when((k > 0) & (k < nk - 1))
        def _():
            acc_ref[...] += prod

        @pl.when(k == nk - 1)
        def _():
            o_ref[...] = acc_ref[...] + prod

    return _call(
        body, name, (T // tm, nk),
        [pl.BlockSpec((tm, tk), lambda i, k: (i, k)),
         pl.BlockSpec((tm, tk), lambda i, k: (i, k)),
         pl.BlockSpec((D, tk), lambda i, k: (0, k)),
         pl.BlockSpec((D, tk), lambda i, k: (0, k + nk))],
        [pl.BlockSpec((tm, D), lambda i, k: (i, 0))], [jax.ShapeDtypeStruct((T, D), F32)],
        [dg, dup, w_gu, w_gu], scratch_shapes=[pltpu.VMEM((tm, D), F32)],
        sem=("parallel", "arbitrary"), exch=exch)


def _logsig(x):
    return jnp.minimum(x, 0.0) - jnp.log(1.0 + jnp.exp(-jnp.abs(x)))


def _cum_fwd(fl, flm, bf, name):
    B, S, _ = fl.shape
    nb = S // LANES

    def body(fl_ref, flm_ref, bf_ref, cum_ref, cumm_ref):
        rows = lax.broadcasted_iota(jnp.int32, (LANES, LANES), 0)
        cols = lax.broadcasted_iota(jnp.int32, (LANES, LANES), 1)
        tri = (rows >= cols).astype(F32)
        bias = bf_ref[...]
        lfm = jnp.where(rows < N_META, _logsig(flm_ref[...] + bias), 0.0)
        cm = jnp.dot(tri, lfm, precision=lax.Precision.HIGHEST, preferred_element_type=F32)
        cumm_ref[...] = cm * LOG2E
        base = cm[LANES - 1:LANES, :]
        for b in range(B):
            def blk(i, carry):
                r0 = pl.multiple_of(i * LANES, LANES)
                lf = _logsig(fl_ref[b, pl.ds(r0, LANES), :] + bias)
                c = jnp.dot(tri, lf, precision=lax.Precision.HIGHEST,
                            preferred_element_type=F32) + carry
                cum_ref[b, pl.ds(r0, LANES), :] = c * LOG2E
                return c[LANES - 1:LANES, :]

            lax.fori_loop(0, nb, blk, base)

    return pl.pallas_call(
        body, name=name,
        out_shape=(jax.ShapeDtypeStruct((B, S, LANES), F32),
                   jax.ShapeDtypeStruct((LANES, LANES), F32)),
        compiler_params=_params(),
    )(fl, flm, bf)


def _cum_bwd(dc, dcm, fl, flm, bf, name):
    B, S, _ = fl.shape
    nb = S // LANES

    def body(dc_ref, dcm_ref, fl_ref, flm_ref, bf_ref, dfl_ref, dflm_ref, dbf_ref):
        rows = lax.broadcasted_iota(jnp.int32, (LANES, LANES), 0)
        cols = lax.broadcasted_iota(jnp.int32, (LANES, LANES), 1)
        triu = (rows <= cols).astype(F32)
        bias = bf_ref[...]
        total = jnp.zeros((1, LANES), F32)
        dbf = jnp.zeros((1, LANES), F32)
        for b in range(B):
            tail = jnp.zeros((1, LANES), F32)
            for t in range(nb):
                r0 = (nb - 1 - t) * LANES
                rc = jnp.dot(triu, dc_ref[b, r0:r0 + LANES, :], precision=lax.Precision.HIGHEST,
                             preferred_element_type=F32) + tail
                xv = fl_ref[b, r0:r0 + LANES, :] + bias
                d = rc / (1.0 + jnp.exp(xv))
                dfl_ref[b, r0:r0 + LANES, :] = d
                tail = rc[0:1, :]
                dbf = dbf + jnp.sum(d, axis=0, keepdims=True)
            total = total + tail
        rcm = jnp.dot(triu, dcm_ref[...], precision=lax.Precision.HIGHEST,
                      preferred_element_type=F32) + total
        dm = jnp.where(rows < N_META, rcm / (1.0 + jnp.exp(flm_ref[...] + bias)), 0.0)
        dflm_ref[...] = dm
        dbf_ref[...] = dbf + jnp.sum(dm, axis=0, keepdims=True)

    return pl.pallas_call(
        body, name=name,
        out_shape=(jax.ShapeDtypeStruct((B, S, LANES), F32),
                   jax.ShapeDtypeStruct((LANES, LANES), F32),
                   jax.ShapeDtypeStruct((1, LANES), F32)),
        compiler_params=_params(),
    )(dc, dcm, fl, flm, bf)


_NT = (((1,), (1,)), ((), ()))


def _attn_specs(S, NX, qw, v_col0):
    mb = NX // META_BLK
    vb = v_col0 // qw
    return (pl.BlockSpec((S, qw), lambda b, p: (b, p)),
            pl.BlockSpec((META_BLK, qw), lambda b, p: (mb, p)),
            pl.BlockSpec((S, qw), lambda b, p: (b, vb + p)),
            pl.BlockSpec((META_BLK, qw), lambda b, p: (mb, vb + p)),
            pl.BlockSpec((S, LANES), lambda b, p: (b, p)))


def _cum_specs(S, TK):
    return [pl.BlockSpec((1, 2, S // TK, 1, TK), lambda b, p: (b, p, 0, 0, 0)),
            pl.BlockSpec((1, 2, 1, META_BLK), lambda b, p: (b, p, 0, 0))]


LOG2E = 1.4426950408889634


def _attn_fwd(qn, kn, vsrc, v_col0, mla, scale, S, NX, name, ck=None, cmk=None, exch=None):
    T = qn.shape[0]
    B = NX // S
    qw = 2 * LANES if mla else LANES
    npair = qn.shape[1] // qw
    TQ = min(512, S)
    TK = TQ
    forget = ck is not None
    a = scale * LOG2E

    def body(*refs):
        if forget:
            q_ref, k_ref, km_ref, v_ref, vm_ref, ck_ref, cmk_ref, _, o_ref, lse_ref = refs
        else:
            q_ref, k_ref, km_ref, v_ref, vm_ref, _, o_ref, lse_ref = refs
        lo = lax.broadcasted_iota(jnp.int32, (1, LANES), 1) < HALF
        mcol = lax.broadcasted_iota(jnp.int32, (TQ, META_BLK), 1)
        causal = (lax.broadcasted_iota(jnp.int32, (TQ, TK), 0)
                  >= lax.broadcasted_iota(jnp.int32, (TQ, TK), 1))
        two = lax.broadcasted_iota(jnp.int32, (TQ, 2), 1)
        for qi in range(S // TQ):
            q0 = qi * TQ
            sls = [slice(e * LANES, (e + 1) * LANES) if mla else slice(None) for e in range(2)]
            if mla:
                qts = [q_ref[q0:q0 + TQ, sl] for sl in sls]
            else:
                qts = [jnp.where(lo if e == 0 else ~lo, q_ref[q0:q0 + TQ, :], 0.0).astype(BF16)
                       for e in range(2)]

            def step(e, kt, vt, c2, mask, carry):
                m, l, acc = carry
                s = lax.dot_general(qts[e], kt, _NT, preferred_element_type=F32) * a
                if forget:
                    s = s - c2
                if mask is not None:
                    s = jnp.where(mask, s, NEG)
                m2 = jnp.max(s, axis=1, keepdims=True)
                if m is not None:
                    m2 = jnp.maximum(m, m2)
                p = jnp.exp2(s - m2)
                l2 = jnp.sum(p, axis=1, keepdims=True)
                acc2 = jnp.dot(p.astype(BF16), vt.astype(BF16), preferred_element_type=F32)
                if m is not None:
                    alpha = jnp.exp2(m - m2)
                    l2, acc2 = alpha * l + l2, alpha * acc + acc2
                return m2, l2, acc2

            def both(rows, kj, mask, carry):
                return tuple(step(e, k_ref[rows, sls[e]], v_ref[rows, sls[e]],
                                  ck_ref[0, e, kj] if forget else None, mask, carry[e]) for e in range(2))

            def below(kj, carry):
                return both(pl.ds(pl.multiple_of(kj * TK, TK), TK), kj, None, carry)

            carry = tuple(step(e, km_ref[:, sls[e]], vm_ref[:, sls[e]], cmk_ref[0, e] if forget else None,
                               mcol < N_META, (None, None, None)) for e in range(2))
            if qi:
                carry = lax.fori_loop(0, qi, below, carry)
            carry = both(slice(q0, q0 + TK), qi, causal, carry)
            outs = [acc / l for _, l, acc in carry]
            lses = [m + jnp.log2(l) for m, l, _ in carry]
            first = pltpu.roll(outs[0], HALF, 1) if mla else outs[0]
            o_ref[q0:q0 + TQ, :] = jnp.where(lo, first, outs[1])
            lse_ref[0, 0, q0:q0 + TQ, :] = jnp.where(two == 0, lses[0], lses[1])

    qk, kmeta, vv, vmeta, pair = _attn_specs(S, NX, qw, v_col0)
    specs = [qk, qk, kmeta, vv, vmeta]
    ins = [qn, kn, kn, vsrc, vsrc]
    if forget:
        specs += _cum_specs(S, TK)
        ins += [ck, cmk]
    specs.append(pl.BlockSpec(memory_space=pl.ANY))
    ins.append(jnp.zeros((T, npair * LANES), F32))
    lse_spec = pl.BlockSpec((1, 1, S, 2), lambda b, p: (b, p, 0, 0))
    return _call(
        body, name, (B, npair), specs, [pair, lse_spec],
        [jax.ShapeDtypeStruct((T, npair * LANES), F32), jax.ShapeDtypeStruct((B, npair, S, 2), F32)],
        ins, sem=("parallel", "parallel"), aliases={len(ins) - 1: 0}, exch=exch)


def _attn_bwd(qn, kn, vsrc, v_col0, o, lse, do, mla, scale, S, NX, name, ck=None, cmk=None, exch=None):
    T, W = qn.shape
    B = NX // S
    qw = 2 * LANES if mla else LANES
    npair = W // qw
    TQ = min(512, S)
    TK = TQ
    forget = ck is not None
    a = scale * LOG2E
    _TN = (((0,), (0,)), ((), ()))

    def body(*refs):
        refs = list(refs)
        q_ref, k_ref, km_ref, v_ref, vm_ref, o_ref, do_ref, lse_ref = refs[:8]
        pos = 8
        if forget:
            ck_ref, cmk_ref = refs[8:10]
            pos = 10
        pos += 3
        dq_ref, dk_ref, dv_ref, dkm_ref, dvm_ref = refs[pos:pos + 5]
        if forget:
            dck_ref, dcm_ref, dcq_ref = refs[pos + 5:pos + 8]
            dck_ref[...] = jnp.zeros_like(dck_ref)
            dcm_ref[...] = jnp.zeros_like(dcm_ref)
        dk_ref[...] = jnp.zeros_like(dk_ref)
        dv_ref[...] = jnp.zeros_like(dv_ref)
        dkm_ref[...] = jnp.zeros_like(dkm_ref)
        dvm_ref[...] = jnp.zeros_like(dvm_ref)
        lo = lax.broadcasted_iota(jnp.int32, (1, LANES), 1) < HALF
        mcol = lax.broadcasted_iota(jnp.int32, (TQ, META_BLK), 1)
        causal = (lax.broadcasted_iota(jnp.int32, (TQ, TK), 0)
                  >= lax.broadcasted_iota(jnp.int32, (TQ, TK), 1))
        two = lax.broadcasted_iota(jnp.int32, (TQ, 2), 1)
        for qi in range(S // TQ):
            q0 = qi * TQ
            dof = do_ref[q0:q0 + TQ, :]
            prod = dof * o_ref[q0:q0 + TQ, :]
            lse2 = lse_ref[0, 0, q0:q0 + TQ, :]
            sls = [slice(e * LANES, (e + 1) * LANES) if mla else slice(None) for e in range(2)]
            mine = [lo, ~lo]
            if mla:
                qts = [q_ref[q0:q0 + TQ, sl] for sl in sls]
                dots = [jnp.where(lo, 0.0, pltpu.roll(dof, HALF, 1) if e == 0 else dof).astype(BF16)
                        for e in range(2)]
            else:
                qts = [jnp.where(mine[e], q_ref[q0:q0 + TQ, :], 0.0).astype(BF16) for e in range(2)]
                dots = [jnp.where(mine[e], dof, 0.0).astype(BF16) for e in range(2)]
            deltas = [jnp.sum(jnp.where(mine[e], prod, 0.0), axis=1, keepdims=True) for e in range(2)]
            lse_ts = [jnp.sum(jnp.where(two == e, lse2, 0.0), axis=1, keepdims=True) for e in range(2)]

            def grads(e, kt, vt, c2, mask):
                s = lax.dot_general(qts[e], kt, _NT, preferred_element_type=F32) * a
                if forget:
                    s = s - c2
                p = jnp.exp2(s - lse_ts[e])
                if mask is not None:
                    p = jnp.where(mask, p, 0.0)
                dp = lax.dot_general(dots[e], vt, _NT, preferred_element_type=F32)
                ds = p * (dp - deltas[e])
                dsb = ds.astype(BF16)
                return (jnp.dot(dsb, kt, preferred_element_type=F32),
                        lax.dot_general(dsb, qts[e], _TN, preferred_element_type=F32) * scale,
                        lax.dot_general(p.astype(BF16), dots[e], _TN, preferred_element_type=F32),
                        -jnp.sum(ds, axis=0, keepdims=True) if forget else None,
                        jnp.sum(ds, axis=1, keepdims=True) if forget else None)

            def block(k_at, v_at, dk_at, dv_at, c_at, dc_at, mask, dqs):
                got = [grads(e, k_at(sls[e]), v_at(sls[e]).astype(BF16), c_at(e) if forget else None, mask)
                       for e in range(2)]
                if mla:
                    for e in range(2):
                        dk_at(sls[e], got[e][1])
                        dv_at(sls[e], got[e][2])
                else:
                    dk_at(sls[0], got[0][1] + got[1][1])
                    dv_at(sls[0], got[0][2] + got[1][2])
                if forget:
                    for e in range(2):
                        dc_at(e, got[e][3])
                picks = (0, 0, 4, 4) if forget else (0, 0)
                new = tuple(got[i % 2][k] for i, k in enumerate(picks))
                return new if dqs is None else tuple(x + y for x, y in zip(dqs, new))

            def add_to(ref, *lead):
                def add(*idx_and_val):
                    *idx, val = idx_and_val
                    ref[(*lead, *idx)] += val
                return add

            def token_block(rows, kj, mask, dqs):
                return block(lambda sl: k_ref[rows, sl], lambda sl: v_ref[rows, sl],
                             lambda sl, val: add_to(dk_ref)(rows, sl, val),
                             lambda sl, val: add_to(dv_ref)(rows, sl, val),
                             lambda e: ck_ref[0, e, kj], lambda e, val: add_to(dck_ref, 0)(e, kj, val),
                             mask, dqs)

            dqs = block(lambda sl: km_ref[:, sl], lambda sl: vm_ref[:, sl],
                        lambda sl, val: add_to(dkm_ref, 0)(slice(None), sl, val),
                        lambda sl, val: add_to(dvm_ref, 0)(slice(None), sl, val),
                        lambda e: cmk_ref[0, e], lambda e, val: add_to(dcm_ref, 0)(e, val),
                        mcol < N_META, None)

            def below(kj, dqs):
                return token_block(pl.ds(pl.multiple_of(kj * TK, TK), TK), kj, None, dqs)

            if qi:
                dqs = lax.fori_loop(0, qi, below, dqs)
            dqs = token_block(slice(q0, q0 + TK), qi, causal, dqs)
            if forget:
                dcq_ref[0, 0, q0:q0 + TQ, :] = jnp.where(two == 0, dqs[2], dqs[3])
            if mla:
                for e in range(2):
                    dq_ref[q0:q0 + TQ, sls[e]] = dqs[e] * scale
            else:
                dq_ref[q0:q0 + TQ, :] = jnp.where(lo, dqs[0], dqs[1]) * scale

    qk, kmeta, vv, vmeta, pair = _attn_specs(S, NX, qw, v_col0)
    lse_spec = pl.BlockSpec((1, 1, S, 2), lambda b, p: (b, p, 0, 0))
    specs = [qk, qk, kmeta, vv, vmeta, pair, pair, lse_spec]
    ins = [qn, kn, kn, vsrc, vsrc, o, do, lse]
    if forget:
        specs += _cum_specs(S, TK)
        ins += [ck, cmk]
    first_alias = len(ins)
    specs += [pl.BlockSpec(memory_space=pl.ANY)] * 3
    ins += [jnp.zeros((T, W), F32)] * 3
    mspec = pl.BlockSpec((1, META_BLK, qw), lambda b, p: (b, 0, p))
    out_specs = [qk, qk, qk, mspec, mspec]
    tok = jax.ShapeDtypeStruct((T, W), F32)
    met = jax.ShapeDtypeStruct((B, META_BLK, W), F32)
    out_shape = [tok, tok, tok, met, met]
    if forget:
        out_specs += _cum_specs(S, TK) + [lse_spec]
        out_shape += [jax.ShapeDtypeStruct((B, HEADS, S // TK, 1, TK), F32),
                      jax.ShapeDtypeStruct((B, HEADS, 1, META_BLK), F32),
                      jax.ShapeDtypeStruct((B, npair, S, 2), F32)]
    return _call(
        body, name, (B, npair), specs, out_specs, out_shape, ins, sem=("parallel", "parallel"),
        aliases={first_alias: 0, first_alias + 1: 1, first_alias + 2: 2}, exch=exch)


def _gate_fwd(z, bg, of, om, name):
    T, D = of.shape
    tm = _tile(T, 640, 16)

    def body(z_ref, bg_ref, of_ref, om_ref, o_ref):
        gt = jax.nn.sigmoid(z_ref[...] + bg_ref[...])
        o_ref[...] = (gt[:, :D] * of_ref[...] + gt[:, D:] * om_ref[...]).astype(BF16)

    row = pl.BlockSpec((tm, D), lambda i: (i, 0))
    return pl.pallas_call(
        body, name=name, grid=(T // tm,),
        in_specs=[pl.BlockSpec((tm, 2 * D), lambda i: (i, 0)),
                  pl.BlockSpec((1, 2 * D), lambda i: (0, 0)), row, row],
        out_specs=row, out_shape=jax.ShapeDtypeStruct((T, D), BF16),
        compiler_params=_params(("parallel",)),
    )(z, bg, of, om)


def _gate_bwd(dmix, z, bg, of, om, name):
    T, D = of.shape
    tm = _tile(T, 640, 16)

    def body(dm_ref, z_ref, bg_ref, of_ref, om_ref, dgl_ref, dof_ref, dom_ref, dbg_ref):
        gt = jax.nn.sigmoid(z_ref[...] + bg_ref[...])
        dm = dm_ref[...]
        dof_ref[...] = (dm * gt[:, :D]).astype(BF16)
        dom_ref[...] = (dm * gt[:, D:]).astype(BF16)
        dgl = jnp.concatenate([dm * of_ref[...], dm * om_ref[...]], axis=1) * gt * (1.0 - gt)
        dgl_ref[...] = dgl.astype(BF16)

        @pl.when(pl.program_id(0) == 0)
        def _():
            dbg_ref[...] = jnp.zeros_like(dbg_ref)

        dbg_ref[...] += jnp.sum(dgl, axis=0, keepdims=True)

    row = pl.BlockSpec((tm, D), lambda i: (i, 0))
    wide = pl.BlockSpec((tm, 2 * D), lambda i: (i, 0))
    one = pl.BlockSpec((1, 2 * D), lambda i: (0, 0))
    return pl.pallas_call(
        body, name=name, grid=(T // tm,),
        in_specs=[row, wide, one, row, row], out_specs=(wide, row, row, one),
        out_shape=(jax.ShapeDtypeStruct((T, 2 * D), BF16), jax.ShapeDtypeStruct((T, D), BF16),
                   jax.ShapeDtypeStruct((T, D), BF16), jax.ShapeDtypeStruct((1, 2 * D), F32)),
        compiler_params=_params(("arbitrary",)),
    )(dmix, z, bg, of, om)


def _loss(h, tgt, n_valid, name):
    T, D = h.shape
    tm = _tile(T, 640, 16)

    def body(h_ref, t_ref, dh_ref, l_ref):
        i = pl.program_id(0)
        rows = lax.broadcasted_iota(jnp.int32, (tm, D), 0) + i * tm
        err = jnp.where(rows < n_valid, h_ref[...] - t_ref[...], 0.0)
        dh_ref[...] = err * (1.0 / D)

        @pl.when(i == 0)
        def _():
            l_ref[...] = jnp.zeros_like(l_ref)

        l_ref[...] += 0.5 * jnp.sum(jnp.sum(err * err, axis=1, keepdims=True) * (1.0 / D))

    row = pl.BlockSpec((tm, D), lambda i: (i, 0))
    acc = pl.BlockSpec((8, LANES), lambda i: (0, 0))
    return pl.pallas_call(
        body, name=name, grid=(T // tm,), in_specs=[row, row], out_specs=(row, acc),
        out_shape=(jax.ShapeDtypeStruct((T, D), F32), jax.ShapeDtypeStruct((8, LANES), F32)),
        compiler_params=_params(("arbitrary",)),
    )(h, tgt)


def _adamw(parts, w, m, v, name):
    P, R, C = parts.shape
    tr = _tile(R, max(8, (1 << 18) // C), 8)
    bc1 = 1.0 - ADAM_B1 ** ADAM_STEP
    bc2 = 1.0 - ADAM_B2 ** ADAM_STEP

    def body(p_ref, w_ref, m_ref, v_ref, g_ref, d_ref, m2_ref, v2_ref):
        g = p_ref[0].astype(F32)
        for j in range(1, P):
            g = g + p_ref[j].astype(F32)
        m2 = ADAM_B1 * m_ref[...] + (1.0 - ADAM_B1) * g
        v2 = ADAM_B2 * v_ref[...] + (1.0 - ADAM_B2) * (g * g)
        m_hat = m2 / bc1
        v_hat = v2 / bc2
        g_ref[...] = g
        d_ref[...] = -ADAM_LR * (m_hat / (jnp.sqrt(v_hat) + ADAM_EPS) + ADAM_WD * w_ref[...])
        m2_ref[...] = m2
        v2_ref[...] = v2

    row = pl.BlockSpec((tr, C), lambda i: (i, 0))
    sh = jax.ShapeDtypeStruct((R, C), F32)
    return pl.pallas_call(
        body, name=name, grid=(R // tr,),
        in_specs=[pl.BlockSpec((P, tr, C), lambda i: (0, i, 0)), row, row, row],
        out_specs=(row, row, row, row), out_shape=(sh, sh, sh, sh),
        compiler_params=_params(("parallel",)),
    )(parts, w, m, v)


def _peer(d):
    x, y, c = lax.axis_index("x"), lax.axis_index("y"), lax.axis_index("c")
    px = 1 - x if d & 4 else x
    py = 1 - y if d & 2 else y
    pc = 1 - c if d & 1 else c
    return (px, py, pc), 4 * px + 2 * py + pc


class _Exchange:
    def __init__(self, srcs, gather):
        self.srcs, self.gather, self.n = list(srcs), gather, len(srcs)
        n = self.n
        hbm = pl.BlockSpec(memory_space=pl.ANY)
        self.in_specs = [hbm] * n
        self.out_specs = [hbm] * n
        self.out_shape = [jax.ShapeDtypeStruct((N_DEV,) + s.shape[-2:], s.dtype) for s in srcs]
        self.scratch = [pltpu.SemaphoreType.DMA((N_DEV - 1, n)), pltpu.SemaphoreType.DMA((N_DEV - 1, n)),
                        pltpu.SemaphoreType.DMA((n,))]

    def _copies(self, src_refs, out_refs, sems):
        send_sems, recv_sems, local_sems = sems
        _, me = _peer(0)

        def remote(w, d, landing):
            dev, lin = _peer(d)
            return pltpu.make_async_remote_copy(
                src_ref=src_refs[w] if self.gather else src_refs[w].at[lin],
                dst_ref=out_refs[w].at[lin if landing else me],
                send_sem=send_sems.at[d - 1, w], recv_sem=recv_sems.at[d - 1, w],
                device_id=dev, device_id_type=pl.DeviceIdType.MESH)

        pairs = [(w, d) for d in range(1, N_DEV) for w in range(self.n)]
        own = [pltpu.make_async_copy(src_refs[w] if self.gather else src_refs[w].at[me],
                                     out_refs[w].at[me], local_sems.at[w]) for w in range(self.n)]
        return own, [remote(w, d, False) for w, d in pairs], [remote(w, d, True) for w, d in pairs]

    def start(self, src_refs, out_refs, sems):
        own, sent, _ = self._copies(src_refs, out_refs, sems)
        for cp in own + sent:
            cp.start()

    def wait(self, src_refs, out_refs, sems):
        own, sent, landing = self._copies(src_refs, out_refs, sems)
        for cp in landing:
            cp.wait_recv()
        for cp in sent:
            cp.wait_send()
        for cp in own:
            cp.wait()


def _exchange(srcs, name, gather):
    ex = _Exchange(srcs, gather)
    n = ex.n

    def body(*refs):
        ex.start(refs[:n], refs[n:2 * n], refs[2 * n:])
        ex.wait(refs[:n], refs[n:2 * n], refs[2 * n:])

    outs = pl.pallas_call(
        body, name=name, in_specs=ex.in_specs, out_specs=tuple(ex.out_specs),
        out_shape=tuple(ex.out_shape), scratch_shapes=ex.scratch,
    )(*srcs)
    return list(outs)


def _call(body, name, grid, in_specs, out_specs, out_shape, ins, scratch_shapes=(), sem=None,
          aliases=None, exch=None):
    aliases = aliases or {}
    if exch is None:
        outs = pl.pallas_call(
            body, name=name, grid=grid, in_specs=list(in_specs), out_specs=tuple(out_specs),
            out_shape=tuple(out_shape), scratch_shapes=list(scratch_shapes),
            input_output_aliases=aliases, compiler_params=_params(sem),
        )(*ins)
        return list(outs), []
    ni, no, ns, n = len(in_specs), len(out_specs), len(scratch_shapes), exch.n
    last_ids = [g - 1 for g in grid]

    def hosted(*refs):
        cin, xin = refs[:ni], refs[ni:ni + n]
        cout, xout = refs[ni + n:ni + n + no], refs[ni + n + no:ni + 2 * n + no]
        cscr, xsem = refs[ni + 2 * n + no:ni + 2 * n + no + ns], refs[ni + 2 * n + no + ns:]
        ids = [pl.program_id(a) for a in range(len(grid))]
        first, last = ids[0] == 0, ids[0] == last_ids[0]
        for a in range(1, len(grid)):
            first, last = first & (ids[a] == 0), last & (ids[a] == last_ids[a])

        @pl.when(first)
        def _():
            exch.start(xin, xout, xsem)

        body(*cin, *cout, *cscr)

        @pl.when(last)
        def _():
            exch.wait(xin, xout, xsem)

    outs = pl.pallas_call(
        hosted, name=name, grid=grid, in_specs=list(in_specs) + exch.in_specs,
        out_specs=tuple(list(out_specs) + exch.out_specs),
        out_shape=tuple(list(out_shape) + exch.out_shape),
        scratch_shapes=list(scratch_shapes) + exch.scratch, input_output_aliases=aliases,
        compiler_params=_params(("arbitrary",) * len(grid)),
    )(*ins, *exch.srcs)
    return list(outs[:no]), list(outs[no:])


def _pack(arrs, cols, row_mult):
    flat = jnp.concatenate([a.reshape(-1) for a in arrs])
    n = flat.shape[0]
    quantum = cols * row_mult
    total = -(-n // quantum) * quantum
    return jnp.pad(flat, (0, total - n)).reshape(total // cols, cols)


def _pack_rows(arrs, cols, row_mult):
    flat = jnp.concatenate(arrs, axis=1)
    n = flat.shape[1]
    quantum = cols * row_mult
    total = -(-n // quantum) * quantum
    return jnp.pad(flat, ((0, 0), (0, total - n))).reshape(N_DEV, total // cols, cols)


def _unpack(packed, shapes):
    flat = packed.reshape(-1)
    out, off = [], 0
    for s in shapes:
        n = int(np.prod(s))
        out.append(flat[off:off + n].reshape(s))
        off += n
    return out


def _rope_tables(positions):
    inv_freq = ROPE_THETA ** (-jnp.arange(0, MLA_ROPE, 2, dtype=F32) / MLA_ROPE)
    ang = positions.astype(F32)[:, None] * inv_freq[None, :]
    cos, sin = jnp.cos(ang), jnp.sin(ang)
    n = positions.shape[0]
    ones, zeros = jnp.ones((n, MLA_NOPE), F32), jnp.zeros((n, MLA_NOPE), F32)
    tail1, tail0 = jnp.ones((n, LANES - MLA_QK), F32), jnp.zeros((n, LANES - MLA_QK), F32)
    z16 = jnp.zeros((n, 16), F32)
    c = jnp.concatenate([ones, cos, cos, tail1], axis=1)
    s1 = jnp.concatenate([zeros, -sin, z16, tail0], axis=1)
    s2 = jnp.concatenate([zeros, z16, sin, tail0], axis=1)
    return c, s1, s2


def kernel(x, meta_tokens, ffn1_norm, ffn1_w_gu, ffn1_w_down, mix_norm, w_in, b_forget, b_gate, fox_q_norm, fox_k_norm, mla_cq_norm, mla_w_uq, mla_ckv_norm, mla_w_ukv, mla_q_norm, mla_k_norm, w_branch_fox, w_branch_mla, w_out, ffn2_norm, ffn2_w_gu, ffn2_w_down, loss_target, m_meta_tokens, m_ffn1_norm, m_ffn1_w_gu, m_ffn1_w_down, m_mix_norm, m_w_in, m_b_forget, m_b_gate, m_fox_q_norm, m_fox_k_norm, m_mla_cq_norm, m_mla_w_uq, m_mla_ckv_norm, m_mla_w_ukv, m_mla_q_norm, m_mla_k_norm, m_w_branch_fox, m_w_branch_mla, m_w_out, m_ffn2_norm, m_ffn2_w_gu, m_ffn2_w_down, v_meta_tokens, v_ffn1_norm, v_ffn1_w_gu, v_ffn1_w_down, v_mix_norm, v_w_in, v_b_forget, v_b_gate, v_fox_q_norm, v_fox_k_norm, v_mla_cq_norm, v_mla_w_uq, v_mla_ckv_norm, v_mla_w_ukv, v_mla_q_norm, v_mla_k_norm, v_w_branch_fox, v_w_branch_mla, v_w_out, v_ffn2_norm, v_ffn2_w_gu, v_ffn2_w_down):
    names = ["meta_tokens", "ffn1_norm", "ffn1_w_gu", "ffn1_w_down", "mix_norm", "w_in", "b_forget",
             "b_gate", "fox_q_norm", "fox_k_norm", "mla_cq_norm", "mla_w_uq", "mla_ckv_norm",
             "mla_w_ukv", "mla_q_norm", "mla_k_norm", "w_branch_fox", "w_branch_mla", "w_out",
             "ffn2_norm", "ffn2_w_gu", "ffn2_w_down"]
    W = dict(zip(names, [meta_tokens, ffn1_norm, ffn1_w_gu, ffn1_w_down, mix_norm, w_in, b_forget,
                         b_gate, fox_q_norm, fox_k_norm, mla_cq_norm, mla_w_uq, mla_ckv_norm,
                         mla_w_ukv, mla_q_norm, mla_k_norm, w_branch_fox, w_branch_mla, w_out,
                         ffn2_norm, ffn2_w_gu, ffn2_w_down]))
    Mo = dict(zip(names, [m_meta_tokens, m_ffn1_norm, m_ffn1_w_gu, m_ffn1_w_down, m_mix_norm, m_w_in,
                          m_b_forget, m_b_gate, m_fox_q_norm, m_fox_k_norm, m_mla_cq_norm,
                          m_mla_w_uq, m_mla_ckv_norm, m_mla_w_ukv, m_mla_q_norm, m_mla_k_norm,
                          m_w_branch_fox, m_w_branch_mla, m_w_out, m_ffn2_norm, m_ffn2_w_gu,
                          m_ffn2_w_down]))
    Vo = dict(zip(names, [v_meta_tokens, v_ffn1_norm, v_ffn1_w_gu, v_ffn1_w_down, v_mix_norm, v_w_in,
                          v_b_forget, v_b_gate, v_fox_q_norm, v_fox_k_norm, v_mla_cq_norm,
                          v_mla_w_uq, v_mla_ckv_norm, v_mla_w_ukv, v_mla_q_norm, v_mla_k_norm,
                          v_w_branch_fox, v_w_branch_mla, v_w_out, v_ffn2_norm, v_ffn2_w_gu,
                          v_ffn2_w_down]))

    B, S, D = x.shape
    NX = B * S
    T = NX + META_BLK
    H = HEADS
    assert NX % META_BLK == 0 and S % LANES == 0
    me = 4 * lax.axis_index("x") + 2 * lax.axis_index("y") + lax.axis_index("c")

    big = [("ffn1_w_gu", 1), ("ffn1_w_down", 0), ("w_in", 1), ("mla_w_uq", 1), ("mla_w_ukv", 1),
           ("w_branch_fox", 1), ("w_branch_mla", 1), ("w_out", 0), ("ffn2_w_gu", 1), ("ffn2_w_down", 0)]
    mix_small = ["mla_w_uq", "mla_w_ukv", "w_branch_fox", "w_branch_mla", "w_out"]
    last_group = ["ffn2_w_gu", "ffn2_w_down"]
    axis_of = dict(big)
    full = {}

    def shards(group):
        return [W[n][0].astype(BF16) for n in group]

    def assemble(group, blks):
        for n, blk in zip(group, blks):
            _, r, c = blk.shape
            full[n] = (blk.transpose(1, 0, 2).reshape(r, N_DEV * c) if axis_of[n] == 1
                       else blk.reshape(N_DEV * r, c))

    got = _exchange(shards(["ffn1_w_gu"]) + [meta_tokens], "gather_first", gather=True)
    assemble(["ffn1_w_gu"], got[:1])
    meta_full = got[1].transpose(1, 0, 2).reshape(N_META, D)

    Z_G, Z_FQ = 0, 2 * D
    Z_FK, Z_FV = Z_FQ + FOX_W, Z_FQ + 2 * FOX_W
    Z_CQ = Z_FQ + 3 * FOX_W
    Z_CKV = Z_CQ + MLA_Q_RANK
    Z_F = Z_CKV + MLA_KV_RANK
    Z_KR = Z_F + LANES

    def pad_lanes(a, w=LANES):
        return jnp.pad(a, [(0, 0)] * (a.ndim - 1) + [(0, w - a.shape[-1])])

    def rows_T(real, meta=None):
        n = real.shape[1]
        parts = [real]
        used = 0
        if meta is not None:
            parts.append(meta)
            used = meta.shape[0]
        if T - NX - used:
            parts.append(jnp.zeros((T - NX - used, n), real.dtype))
        return jnp.concatenate(parts, axis=0)

    def put_meta(tok, meta_per_seq):
        return lax.dynamic_update_slice(tok, meta_per_seq.sum(0), (NX, 0))

    h0 = rows_T(x.reshape(NX, D), meta_full)
    tgt = rows_T(loss_target.reshape(NX, D))

    def ffn_fwd(h, norm, w_gu, tag, behind_up=None, behind_down=None):
        u = _norm_fwd(h, 0, D, D, norm, D, D, tag + "_norm")
        (g, up, a), got = _ffn_up(u, w_gu, tag + "_up",
                                  exch=_Exchange(shards(behind_up), True) if behind_up else None)
        assemble(behind_up or [], got)
        h_out = _mm(a, full[tag + "_w_down"], "nn", tag + "_down", scale=0.5, res=h,
                    exch=_Exchange(shards(behind_down), True) if behind_down else None)
        if behind_down:
            h_out, got = h_out
            assemble(behind_down, got)
        return h_out, (u, g, up, a)

    h1, ffn1_saved = ffn_fwd(h0, W["ffn1_norm"], full["ffn1_w_gu"], "ffn1",
                             behind_up=["ffn1_w_down"], behind_down=["w_in"])
    wi = full["w_in"]
    o_fq = 0
    o_f = 3 * FOX_W
    o_cq = o_f + HEADS
    o_kr = o_cq + MLA_Q_RANK + MLA_KV_RANK
    o_g = o_kr + MLA_ROPE
    w_in_p = jnp.concatenate([
        wi[:, o_g:o_g + 2 * D], wi[:, o_fq:o_f], wi[:, o_cq:o_kr],
        jnp.pad(wi[:, o_f:o_cq], ((0, 0), (0, LANES - HEADS))),
        jnp.pad(wi[:, o_kr:o_g], ((0, 0), (0, LANES - MLA_ROPE)))], axis=1)

    u2 = _norm_fwd(h1, 0, D, D, W["mix_norm"], D, D, "mix_norm")
    z, got = _mm(u2, w_in_p, "nn", "w_in", exch=_Exchange(shards(mix_small), True))
    assemble(mix_small, got)
    w_uq_p = jnp.pad(full["mla_w_uq"].reshape(MLA_Q_RANK, H, MLA_QK),
                     ((0, 0), (0, 0), (0, LANES - MLA_QK))).reshape(MLA_Q_RANK, H * LANES)

    gq_f = jnp.tile(W["fox_q_norm"], (1, 2))
    gk_f = jnp.tile(W["fox_k_norm"], (1, 2))
    fqn = _norm_fwd(z, Z_FQ, FOX_W, LANES, gq_f, FOX_HD, FOX_HD, "fox_q_norm")
    fkn = _norm_fwd(z, Z_FK, FOX_W, LANES, gk_f, FOX_HD, FOX_HD, "fox_k_norm")
    fl = z[:NX, Z_F:Z_F + LANES].reshape(B, S, LANES)
    flm = z[NX:, Z_F:Z_F + LANES]
    bf = pad_lanes(W["b_forget"])
    cum, cumm = _cum_fwd(fl, flm, bf, "forget_cum")
    TK = min(512, S)
    ck = cum[:, :, :H].transpose(0, 2, 1).reshape(B, H, S // TK, 1, TK)
    cmk = jnp.broadcast_to(cumm[:, :H].T[None, :, None, :], (B, H, 1, META_BLK))
    (o_fox, lse_fox), got = _attn_fwd(fqn, fkn, z, Z_FV, False, FOX_HD ** -0.5, S, NX, "fox_attn", ck, cmk,
                                      exch=_Exchange(shards(last_group), True))
    assemble(last_group, got)
    of = _mm(o_fox, full["w_branch_fox"], "nn", "branch_fox")

    pos = jnp.concatenate([jnp.tile(jnp.arange(S) + N_META, B), jnp.arange(META_BLK)])
    tabs = _rope_tables(pos)
    cqn = _norm_fwd(z, Z_CQ, MLA_Q_RANK, MLA_Q_RANK, W["mla_cq_norm"], MLA_Q_RANK, MLA_Q_RANK, "mla_cq_norm")
    q_lin = _mm(cqn, w_uq_p, "nn", "mla_uq")
    ckvn = _norm_fwd(z, Z_CKV, MLA_KV_RANK, MLA_KV_RANK, W["mla_ckv_norm"], MLA_KV_RANK, MLA_KV_RANK,
                     "mla_ckv_norm")
    kv_lin = _mm(ckvn, full["mla_w_ukv"], "nn", "mla_ukv")
    gq_m, gk_m = pad_lanes(W["mla_q_norm"]), pad_lanes(W["mla_k_norm"])
    mqn = _norm_fwd(q_lin, 0, H * LANES, LANES, gq_m, LANES, MLA_QK, "mla_q_norm", tabs=tabs)
    mkn = _mla_k_fwd(kv_lin, z, Z_KR, gk_m, tabs, "mla_k_norm")
    (o_mla, lse_mla), _ = _attn_fwd(mqn, mkn, kv_lin, 0, True, MLA_QK ** -0.5, S, NX, "mla_attn")
    om = _mm(o_mla, full["w_branch_mla"], "nn", "branch_mla")

    mix = _gate_fwd(z, W["b_gate"], of, om, "gate_mix")
    h2 = _mm(mix, full["w_out"], "nn", "w_out", res=h1)

    h3, ffn2_saved = ffn_fwd(h2, W["ffn2_norm"], full["ffn2_w_gu"], "ffn2")

    dh3, loss_acc = _loss(h3, tgt, NX, "loss")
    loss = lax.psum(loss_acc[0, 0], AXES)

    G = {}
    parts = {}

    def scatter_of(group):
        per_dest = []
        for n in group:
            r, c = W[n].shape[1:]
            per_dest.append((G[n].reshape(r, N_DEV, c).transpose(1, 0, 2) if axis_of[n] == 1
                             else G[n].reshape(N_DEV, r, c)).astype(BF16))
        return _Exchange(per_dest, False)

    def ffn_bwd(dh, h, norm, w_gu, w_down, saved, tag, behind_down=None, spread=False):
        u, g, up, a = saved
        G[tag + "_w_down"] = _mm(a, dh, "tn", tag + "_dw_down", scale=0.5)
        (dg, dup), got = _ffn_down_bwd(dh, w_down, g, up, tag + "_down_bwd",
                                       exch=scatter_of(behind_down) if behind_down else None)
        parts.update(zip(behind_down or [], got))
        dw_g = _mm(u, dg, "tn", tag + "_dw_g", exch=scatter_of([tag + "_w_down"]) if spread else None)
        if spread:
            dw_g, got = dw_g
            parts[tag + "_w_down"] = got[0]
        G[tag + "_w_gu"] = jnp.concatenate([dw_g, _mm(u, dup, "tn", tag + "_dw_u")], axis=1)
        (du,), got = _ffn_up_bwd_dx(dg, dup, w_gu, tag + "_up_bwd",
                                    exch=scatter_of([tag + "_w_gu"]) if spread else None)
        if spread:
            parts[tag + "_w_gu"] = got[0]
        dh_in, G[tag + "_norm"] = _norm_bwd(h, 0, D, D, norm, D, D, du, tag + "_norm_bwd", res=dh)
        return dh_in

    dh2 = ffn_bwd(dh3, h2, W["ffn2_norm"], full["ffn2_w_gu"], full["ffn2_w_down"], ffn2_saved, "ffn2")

    G["w_out"] = _mm(mix, dh2, "tn", "dw_out")
    dmix = _mm(dh2, full["w_out"], "nt", "w_out_bwd")
    dgl, dof, dom, G["b_gate"] = _gate_bwd(dmix, z, W["b_gate"], of, om, "gate_bwd")

    G["w_branch_fox"] = _mm(o_fox, dof, "tn", "dw_branch_fox")
    do_fox = _mm(dof, full["w_branch_fox"], "nt", "branch_fox_bwd")
    (dq_f, dk_f, dv_f, dkm_f, dvm_f, dck, dcmk, dcq), got = _attn_bwd(
        fqn, fkn, z, Z_FV, o_fox, lse_fox, do_fox, False, FOX_HD ** -0.5, S, NX, "fox_attn_bwd", ck, cmk,
        exch=scatter_of(last_group))
    parts.update(zip(last_group, got))
    dk_f, dv_f = put_meta(dk_f, dkm_f), put_meta(dv_f, dvm_f)
    dfq, gq = _norm_bwd(z, Z_FQ, FOX_W, LANES, gq_f, FOX_HD, FOX_HD, dq_f, "fox_q_norm_bwd", out_dtype=BF16)
    dfk, gk = _norm_bwd(z, Z_FK, FOX_W, LANES, gk_f, FOX_HD, FOX_HD, dk_f, "fox_k_norm_bwd", out_dtype=BF16)
    G["fox_q_norm"] = gq[:, :FOX_HD] + gq[:, FOX_HD:]
    G["fox_k_norm"] = gk[:, :FOX_HD] + gk[:, FOX_HD:]
    dc = pad_lanes(dck.reshape(B, H, S).transpose(0, 2, 1)
                   + dcq.transpose(0, 2, 1, 3).reshape(B, S, H))
    dcm = pad_lanes(dcmk.sum(0)[:, 0, :].T)
    dcm = jnp.where(jnp.arange(LANES)[:, None] < N_META, dcm, 0.0)
    dfl, dflm, dbf = _cum_bwd(dc, dcm, fl, flm, bf, "forget_cum_bwd")
    G["b_forget"] = dbf[:, :HEADS]
    dfl_t = rows_T(dfl.reshape(NX, LANES), dflm)

    G["w_branch_mla"] = _mm(o_mla, dom, "tn", "dw_branch_mla")
    do_mla = _mm(dom, full["w_branch_mla"], "nt", "branch_mla_bwd")
    (dq_m, dk_m, dvk, dkm_m, dvkm), _ = _attn_bwd(
        mqn, mkn, kv_lin, 0, o_mla, lse_mla, do_mla, True, MLA_QK ** -0.5, S, NX, "mla_attn_bwd")
    dk_m, dvk = put_meta(dk_m, dkm_m), put_meta(dvk, dvkm)
    dq_lin, gq = _norm_bwd(q_lin, 0, H * LANES, LANES, gq_m, LANES, MLA_QK, dq_m, "mla_q_norm_bwd", tabs=tabs)
    G["mla_q_norm"] = gq[:, :MLA_QK]
    G["mla_w_uq"] = _mm(cqn, dq_lin, "tn", "dw_uq").reshape(MLA_Q_RANK, H, LANES)[:, :, :MLA_QK].reshape(
        MLA_Q_RANK, H * MLA_QK)
    dcqn = _mm(dq_lin, w_uq_p, "nt", "mla_uq_bwd")
    dcq, G["mla_cq_norm"] = _norm_bwd(z, Z_CQ, MLA_Q_RANK, MLA_Q_RANK, W["mla_cq_norm"], MLA_Q_RANK,
                                      MLA_Q_RANK, dcqn, "mla_cq_norm_bwd", out_dtype=BF16)
    dkv_lin, dkr, gk = _mla_k_bwd(kv_lin, z, Z_KR, gk_m, tabs, dk_m, dvk, "mla_k_norm_bwd")
    G["mla_k_norm"] = gk[:, :MLA_QK]
    G["mla_w_ukv"] = _mm(ckvn, dkv_lin, "tn", "dw_ukv")
    dckvn = _mm(dkv_lin, full["mla_w_ukv"], "nt", "mla_ukv_bwd")
    dckv, G["mla_ckv_norm"] = _norm_bwd(z, Z_CKV, MLA_KV_RANK, MLA_KV_RANK, W["mla_ckv_norm"], MLA_KV_RANK,
                                        MLA_KV_RANK, dckvn, "mla_ckv_norm_bwd", out_dtype=BF16)

    dz = jnp.concatenate([dgl, dfq, dfk, dv_f.astype(BF16), dcq, dckv, dfl_t.astype(BF16),
                          dkr.astype(BF16)], axis=1)
    dw_in_p = _mm(u2, dz, "tn", "dw_in")
    G["w_in"] = jnp.concatenate([
        dw_in_p[:, Z_FQ:Z_CQ], dw_in_p[:, Z_F:Z_F + HEADS], dw_in_p[:, Z_CQ:Z_F],
        dw_in_p[:, Z_KR:Z_KR + MLA_ROPE], dw_in_p[:, Z_G:Z_G + 2 * D]], axis=1)
    du2, got = _mm(dz, w_in_p, "nt", "w_in_bwd", exch=scatter_of(mix_small))
    parts.update(zip(mix_small, got))
    dh1, G["mix_norm"] = _norm_bwd(h1, 0, D, D, W["mix_norm"], D, D, du2, "mix_norm_bwd", res=dh2)

    dh0 = ffn_bwd(dh1, h0, W["ffn1_norm"], full["ffn1_w_gu"], full["ffn1_w_down"], ffn1_saved, "ffn1",
                  behind_down=["w_in"], spread=True)
    grad_x = dh0[:NX].reshape(B, S, D)
    G["meta_tokens"] = dh0[NX:NX + N_META]

    res = {}
    for n, _ in big:
        outs4 = _adamw(parts[n], W[n][0], Mo[n][0], Vo[n][0], "adamw_" + n)
        for key, arr in zip(("g", "d", "m", "v"), outs4):
            res[key, n] = arr[None]

    small = [n for n in names if n not in dict(big) and n != "meta_tokens"]
    small_shapes = [W[n].shape for n in small]
    spack = _pack([G["meta_tokens"]] + [G[n] for n in small], 1024, 8)
    (sparts,) = _exchange([spack], "gather_small_grads", gather=True)
    sflat = sparts.reshape(N_DEV, -1)
    dsh = D // N_DEV
    meta_part = lax.dynamic_slice(sflat[:, :N_META * D].reshape(N_DEV, N_META, D),
                                  (0, 0, me * dsh), (N_DEV, N_META, dsh)).reshape(N_DEV, -1)
    rep_len = sum(int(np.prod(s)) for s in small_shapes)
    rep_part = sflat[:, N_META * D:N_META * D + rep_len]
    sp = _pack_rows([meta_part, rep_part], LANES, 8)
    pks = lambda src: _pack([src["meta_tokens"]] + [src[n] for n in small], LANES, 8)
    g_s, d_s, m_s, v_s = _adamw(sp, pks(W), pks(Mo), pks(Vo), "adamw_small")
    shapes_s = [W["meta_tokens"].shape] + small_shapes
    for key, packed in (("g", g_s), ("d", d_s), ("m", m_s), ("v", v_s)):
        for n, arr in zip(["meta_tokens"] + small, _unpack(packed, shapes_s)):
            res[key, n] = arr

    outs = [loss, grad_x]
    for key in ("g", "d", "m", "v"):
        outs += [res[key, n] for n in names]
    return tuple(outs)
```

```python
import numpy as np
import jax
import jax.numpy as jnp
from jax import lax
from jax.experimental import pallas as pl
from jax.experimental.pallas import tpu as pltpu

F32 = jnp.float32
BF16 = jnp.bfloat16

N_META = 16
EPS = 1e-6
HEADS = 8
FOX_HD = 64
FOX_W = HEADS * FOX_HD
MLA_Q_RANK = 256
MLA_KV_RANK = 128
MLA_NOPE = 64
MLA_ROPE = 32
MLA_QK = MLA_NOPE + MLA_ROPE
MLA_V = 64
ROPE_THETA = 10000.0
LANES = 128
HALF = LANES // 2
META_BLK = 128
NEG = -1e30

ADAM_LR = 0.001
ADAM_B1 = 0.9
ADAM_B2 = 0.999
ADAM_EPS = 1e-08
ADAM_WD = 0.01
ADAM_STEP = 10

N_DEV = 8
AXES = ("x", "y", "c")
VMEM_LIMIT_BYTES = 56 * 1024 * 1024


def _tile(n, cap, mult):
    best = None
    for d in range(mult, min(n, cap) + 1, mult):
        if n % d == 0:
            best = d
    return n if best is None else best


VREG_ELEMS = 8 * LANES


def _row_tile(rows, width):
    return _tile(rows, max(64, (1 << 19) // width), 64)


def _chunk_rows(width):
    rows = 16
    while 2 * rows * width <= 8 * VREG_ELEMS:
        rows *= 2
    return rows


def _by_chunks(rows, width, step, init=()):
    if width <= LANES:
        return step(slice(None), init)
    ch = _chunk_rows(width)
    assert rows % ch == 0, (rows, ch)

    n = rows // ch
    per = next(u for u in (4, 3, 2, 1) if n % u == 0)

    def one(i, carry):
        for u in range(per):
            carry = step(pl.ds(pl.multiple_of((i * per + u) * ch, ch), ch), carry)
        return carry

    return lax.fori_loop(0, n // per, one, init)


def _params(sem=None):
    return pltpu.CompilerParams(dimension_semantics=sem, vmem_limit_bytes=VMEM_LIMIT_BYTES)


def _mm(a, b, mode, name, out_dtype=F32, scale=1.0, res=None, exch=None):
    if mode == "nn":
        (M, K), (K2, N) = a.shape, b.shape
    elif mode == "nt":
        (M, K), (N, K2) = a.shape, b.shape
    else:
        (K, M), (K2, N) = a.shape, b.shape
    assert K == K2, (a.shape, b.shape, mode)
    if mode == "tn":
        tm, tk = _tile(M, 1408, 128), _tile(K, 2080, 16)
    else:
        tm, tk = _tile(M, 640, 16), _tile(K, 4224, 128)
    tn = _tile(N, 1408, 128)
    nk = K // tk
    a_spec = {"nn": pl.BlockSpec((tm, tk), lambda i, j, k: (i, k)),
              "nt": pl.BlockSpec((tm, tk), lambda i, j, k: (i, k)),
              "tn": pl.BlockSpec((tk, tm), lambda i, j, k: (k, i))}[mode]
    b_spec = {"nn": pl.BlockSpec((tk, tn), lambda i, j, k: (k, j)),
              "nt": pl.BlockSpec((tn, tk), lambda i, j, k: (j, k)),
              "tn": pl.BlockSpec((tk, tn), lambda i, j, k: (k, j))}[mode]
    dims = {"nn": (((1,), (0,)), ((), ())), "nt": (((1,), (1,)), ((), ())),
            "tn": (((0,), (0,)), ((), ()))}[mode]
    o_spec = pl.BlockSpec((tm, tn), lambda i, j, k: (i, j))
    has_res = res is not None

    def body(*refs):
        a_ref, b_ref = refs[:2]
        r_ref = refs[2] if has_res else None
        o_ref = refs[2 + has_res]

        def finish(acc):
            o = acc * scale
            if has_res:
                o = o + r_ref[...]
            o_ref[...] = o.astype(out_dtype)

        prod = lax.dot_general(a_ref[...].astype(BF16), b_ref[...].astype(BF16), dims,
                               preferred_element_type=F32)
        if nk == 1:
            finish(prod)
            return
        acc_ref = refs[3 + has_res]
        k = pl.program_id(2)

        @pl.when(k == 0)
        def _():
            acc_ref[...] = prod

        @pl.when((k > 0) & (k < nk - 1))
        def _():
            acc_ref[...] += prod

        @pl.when(k == nk - 1)
        def _():
            finish(acc_ref[...] + prod)

    ins = [a, b] + ([res] if has_res else [])
    specs = [a_spec, b_spec] + ([o_spec] if has_res else [])
    (out,), got = _call(
        body, name, (M // tm, N // tn, nk), specs, [o_spec], [jax.ShapeDtypeStruct((M, N), out_dtype)],
        ins, scratch_shapes=[pltpu.VMEM((tm, tn), F32)] if nk > 1 else [],
        sem=("parallel", "parallel", "arbitrary"), exch=exch)
    return out if exch is None else (out, got)


def _rope_fwd(y, c, s1, s2):
    return y * c + pltpu.roll(y, LANES - 16, 1) * s1 + pltpu.roll(y, 16, 1) * s2


def _rope_bwd(dy, c, s1, s2):
    return dy * c + pltpu.roll(dy * s1, 16, 1) + pltpu.roll(dy * s2, LANES - 16, 1)


def _group_sum(v, seg):
    if seg == v.shape[-1]:
        return jnp.sum(v, axis=-1, keepdims=True)
    lo = lax.broadcasted_iota(jnp.int32, v.shape, 1) < seg
    s_lo = jnp.sum(jnp.where(lo, v, 0.0), axis=-1, keepdims=True)
    s_hi = jnp.sum(jnp.where(lo, 0.0, v), axis=-1, keepdims=True)
    return jnp.where(lo, s_lo, s_hi)


def _norm_fwd(src, col0, width, bw, gain, seg, d_true, name, tabs=None, out_dtype=BF16):
    T = src.shape[0]
    tr = _row_tile(T, bw)
    inv_d = 1.0 / d_true
    c0 = col0 // bw
    assert col0 % bw == 0 and width % bw == 0

    def body(*refs):
        if tabs is None:
            x_ref, g_ref, o_ref = refs
        else:
            x_ref, g_ref, c_ref, s1_ref, s2_ref, o_ref = refs
        gain_v = g_ref[...]

        def step(rows, carry):
            xv = x_ref[rows, :]
            r = lax.rsqrt(_group_sum(xv * xv, seg) * inv_d + EPS)
            y = xv * r * gain_v
            if tabs is not None:
                y = _rope_fwd(y, c_ref[rows, :], s1_ref[rows, :], s2_ref[rows, :])
            o_ref[rows, :] = y.astype(out_dtype)
            return carry

        _by_chunks(tr, bw, step)

    specs = [pl.BlockSpec((tr, bw), lambda i, j: (i, c0 + j)), pl.BlockSpec((1, bw), lambda i, j: (0, 0))]
    ins = [src, gain]
    if tabs is not None:
        tab = pl.BlockSpec((tr, LANES), lambda i, j: (i, 0))
        specs += [tab, tab, tab]
        ins += list(tabs)
    return pl.pallas_call(
        body, name=name, grid=(T // tr, width // bw), in_specs=specs,
        out_specs=pl.BlockSpec((tr, bw), lambda i, j: (i, j)),
        out_shape=jax.ShapeDtypeStruct((T, width), out_dtype),
        compiler_params=_params(("parallel", "parallel")),
    )(*ins)


def _norm_bwd_math(xv, gain, dyv, seg, inv_d):
    r = lax.rsqrt(_group_sum(xv * xv, seg) * inv_d + EPS)
    gy = dyv * gain
    dot = _group_sum(gy * xv, seg)
    dx = r * gy - xv * (r * r * r * inv_d) * dot
    return dx, jnp.sum(dyv * xv * r, axis=0, keepdims=True)


def _norm_bwd(src, col0, width, bw, gain, seg, d_true, dy, name, tabs=None, res=None, out_dtype=F32):
    T = src.shape[0]
    tr = _row_tile(T, bw)
    inv_d = 1.0 / d_true
    c0 = col0 // bw
    has_res = res is not None

    def body(*refs):
        refs = list(refs)
        x_ref, g_ref, dy_ref = refs[:3]
        pos = 3
        if tabs is not None:
            c_ref, s1_ref, s2_ref = refs[3:6]
            pos = 6
        if has_res:
            r_ref = refs[pos]
            pos += 1
        dx_ref, dg_ref = refs[pos], refs[pos + 1]
        gain_v = g_ref[...]

        def step(rows, dg_sum):
            dyv = dy_ref[rows, :].astype(F32)
            if tabs is not None:
                dyv = _rope_bwd(dyv, c_ref[rows, :], s1_ref[rows, :], s2_ref[rows, :])
            dx, dg = _norm_bwd_math(x_ref[rows, :], gain_v, dyv, seg, inv_d)
            if has_res:
                dx = dx + r_ref[rows, :]
            dx_ref[rows, :] = dx.astype(out_dtype)
            return dg_sum + dg

        dg = _by_chunks(tr, bw, step, jnp.zeros((1, bw), F32))

        @pl.when((pl.program_id(0) == 0) & (pl.program_id(1) == 0))
        def _():
            dg_ref[...] = jnp.zeros_like(dg_ref)

        dg_ref[...] += dg

    blk = pl.BlockSpec((tr, bw), lambda i, j: (i, j))
    one = pl.BlockSpec((1, bw), lambda i, j: (0, 0))
    specs = [pl.BlockSpec((tr, bw), lambda i, j: (i, c0 + j)), one, blk]
    ins = [src, gain, dy]
    if tabs is not None:
        tab = pl.BlockSpec((tr, LANES), lambda i, j: (i, 0))
        specs += [tab, tab, tab]
        ins += list(tabs)
    if has_res:
        specs.append(blk)
        ins.append(res)
    return pl.pallas_call(
        body, name=name, grid=(T // tr, width // bw), in_specs=specs, out_specs=(blk, one),
        out_shape=(jax.ShapeDtypeStruct((T, width), out_dtype), jax.ShapeDtypeStruct((1, bw), F32)),
        compiler_params=_params(("arbitrary", "arbitrary")),
    )(*ins)


def _mla_k_raw(kv, kr):
    lane = lax.broadcasted_iota(jnp.int32, kv.shape, 1)
    return jnp.where(lane < MLA_NOPE, kv, jnp.where(lane < MLA_QK, pltpu.roll(kr, MLA_NOPE, 1), 0.0))


def _mla_k_fwd(kv_lin, z, kr_col, gain, tabs, name):
    T, W = kv_lin.shape
    tr = _row_tile(T, LANES)
    krb = kr_col // LANES
    inv_d = 1.0 / MLA_QK

    def body(kv_ref, kr_ref, g_ref, c_ref, s1_ref, s2_ref, o_ref):
        gain_v = g_ref[...]

        def step(rows, carry):
            xv = _mla_k_raw(kv_ref[rows, :], kr_ref[rows, :])
            r = lax.rsqrt(jnp.sum(xv * xv, axis=-1, keepdims=True) * inv_d + EPS)
            o_ref[rows, :] = _rope_fwd(xv * r * gain_v, c_ref[rows, :], s1_ref[rows, :],
                                       s2_ref[rows, :]).astype(BF16)
            return carry

        _by_chunks(tr, LANES, step)

    blk = pl.BlockSpec((tr, LANES), lambda i, h: (i, h))
    tab = pl.BlockSpec((tr, LANES), lambda i, h: (i, 0))
    return pl.pallas_call(
        body, name=name, grid=(T // tr, W // LANES),
        in_specs=[blk, pl.BlockSpec((tr, LANES), lambda i, h: (i, krb)),
                  pl.BlockSpec((1, LANES), lambda i, h: (0, 0)), tab, tab, tab],
        out_specs=blk, out_shape=jax.ShapeDtypeStruct((T, W), BF16),
        compiler_params=_params(("parallel", "parallel")),
    )(kv_lin, z, gain, *tabs)


def _mla_k_bwd(kv_lin, z, kr_col, gain, tabs, dk, dvk, name):
    T, W = kv_lin.shape
    tr = _row_tile(T, LANES)
    krb = kr_col // LANES
    inv_d = 1.0 / MLA_QK

    def body(kv_ref, kr_ref, g_ref, c_ref, s1_ref, s2_ref, dk_ref, dvk_ref, dkv_ref, dkr_ref, dg_ref):
        h = pl.program_id(1)
        gain_v = g_ref[...]

        @pl.when(h == 0)
        def _():
            dkr_ref[...] = jnp.zeros_like(dkr_ref)

        def step(rows, dg_sum):
            xv = _mla_k_raw(kv_ref[rows, :], kr_ref[rows, :])
            dyv = _rope_bwd(dk_ref[rows, :], c_ref[rows, :], s1_ref[rows, :], s2_ref[rows, :])
            dx, dg = _norm_bwd_math(xv, gain_v, dyv, LANES, inv_d)
            lane = lax.broadcasted_iota(jnp.int32, dx.shape, 1)
            dkv_ref[rows, :] = jnp.where(lane < MLA_NOPE, dx, dvk_ref[rows, :])
            dkr_ref[rows, :] += pltpu.roll(jnp.where((lane >= MLA_NOPE) & (lane < MLA_QK), dx, 0.0),
                                           LANES - MLA_NOPE, 1)
            return dg_sum + dg

        dg = _by_chunks(tr, LANES, step, jnp.zeros((1, LANES), F32))

        @pl.when((pl.program_id(0) == 0) & (h == 0))
        def _():
            dg_ref[...] = jnp.zeros_like(dg_ref)

        dg_ref[...] += dg

    blk = pl.BlockSpec((tr, LANES), lambda i, h: (i, h))
    tab = pl.BlockSpec((tr, LANES), lambda i, h: (i, 0))
    one = pl.BlockSpec((1, LANES), lambda i, h: (0, 0))
    return pl.pallas_call(
        body, name=name, grid=(T // tr, W // LANES),
        in_specs=[blk, pl.BlockSpec((tr, LANES), lambda i, h: (i, krb)), one, tab, tab, tab, blk, blk],
        out_specs=(blk, tab, one),
        out_shape=(jax.ShapeDtypeStruct((T, W), F32), jax.ShapeDtypeStruct((T, LANES), F32),
                   jax.ShapeDtypeStruct((1, LANES), F32)),
        compiler_params=_params(("arbitrary", "arbitrary")),
    )(kv_lin, z, gain, *tabs, dk, dvk)


def _ffn_up(u, w_gu, name, exch=None):
    T, D = u.shape
    F = w_gu.shape[1] // 2
    tm, tn = _tile(T, 640, 16), _tile(F, 1408, 128)
    nj = F // tn

    def body(u_ref, wg_ref, wu_ref, g_ref, up_ref, a_ref):
        uv = u_ref[...]
        g = jnp.dot(uv, wg_ref[...], preferred_element_type=F32)
        up = jnp.dot(uv, wu_ref[...], preferred_element_type=F32)
        g_ref[...] = g.astype(BF16)
        up_ref[...] = up.astype(BF16)
        a_ref[...] = (g * jax.nn.sigmoid(g) * up).astype(BF16)

    o_spec = pl.BlockSpec((tm, tn), lambda i, j: (i, j))
    sh = jax.ShapeDtypeStruct((T, F), BF16)
    return _call(
        body, name, (T // tm, nj),
        [pl.BlockSpec((tm, D), lambda i, j: (i, 0)),
         pl.BlockSpec((D, tn), lambda i, j: (0, j)),
         pl.BlockSpec((D, tn), lambda i, j: (0, j + nj))],
        [o_spec, o_spec, o_spec], [sh, sh, sh], [u, w_gu, w_gu],
        sem=("parallel", "parallel"), exch=exch)


def _ffn_down_bwd(dh, w_down, g, up, name, exch=None):
    T, D = dh.shape
    F = w_down.shape[0]
    tm, tn = _tile(T, 640, 16), _tile(F, 1408, 128)

    def body(dh_ref, w_ref, g_ref, up_ref, dg_ref, dup_ref):
        da = 0.5 * lax.dot_general(dh_ref[...].astype(BF16), w_ref[...], (((1,), (1,)), ((), ())),
                                   preferred_element_type=F32)
        gv = g_ref[...].astype(F32)
        sg = jax.nn.sigmoid(gv)
        silu = gv * sg
        dup_ref[...] = (da * silu).astype(BF16)
        dg_ref[...] = (da * up_ref[...].astype(F32) * (sg + silu * (1.0 - sg))).astype(BF16)

    t_spec = pl.BlockSpec((tm, tn), lambda i, j: (i, j))
    sh = jax.ShapeDtypeStruct((T, F), BF16)
    return _call(
        body, name, (T // tm, F // tn),
        [pl.BlockSpec((tm, D), lambda i, j: (i, 0)),
         pl.BlockSpec((tn, D), lambda i, j: (j, 0)), t_spec, t_spec],
        [t_spec, t_spec], [sh, sh], [dh, w_down, g, up],
        sem=("parallel", "parallel"), exch=exch)


def _ffn_up_bwd_dx(dg, dup, w_gu, name, exch=None):
    T, F = dg.shape
    D = w_gu.shape[0]
    tm, tk = _tile(T, 640, 16), _tile(F, 1408, 128)
    nk = F // tk
    nt = (((1,), (1,)), ((), ()))

    def body(dg_ref, dup_ref, wg_ref, wu_ref, o_ref, acc_ref):
        k = pl.program_id(1)
        prod = (lax.dot_general(dg_ref[...], wg_ref[...], nt, preferred_element_type=F32)
                + lax.dot_general(dup_ref[...], wu_ref[...], nt, preferred_element_type=F32))
        if nk == 1:
            o_ref[...] = prod
            return

        @pl.when(k == 0)
        def _():
            acc_ref[...] = prod

        @pl.when((k > 0) & (k < nk - 1))
        def _():
            acc_ref[...] += prod

        @pl.when(k == nk - 1)
        def _():
            o_ref[...] = acc_ref[...] + prod

    return _call(
        body, name, (T // tm, nk),
        [pl.BlockSpec((tm, tk), lambda i, k: (i, k)),
         pl.BlockSpec((tm, tk), lambda i, k: (i, k)),
         pl.BlockSpec((D, tk), lambda i, k: (0, k)),
         pl.BlockSpec((D, tk), lambda i, k: (0, k + nk))],
        [pl.BlockSpec((tm, D), lambda i, k: (i, 0))], [jax.ShapeDtypeStruct((T, D), F32)],
        [dg, dup, w_gu, w_gu], scratch_shapes=[pltpu.VMEM((tm, D), F32)],
        sem=("parallel", "arbitrary"), exch=exch)


def _logsig(x):
    return jnp.minimum(x, 0.0) - jnp.log(1.0 + jnp.exp(-jnp.abs(x)))


def _cum_fwd(fl, flm, bf, name):
    B, S, _ = fl.shape
    nb = S // LANES

    def body(fl_ref, flm_ref, bf_ref, cum_ref, cumm_ref):
        rows = lax.broadcasted_iota(jnp.int32, (LANES, LANES), 0)
        cols = lax.broadcasted_iota(jnp.int32, (LANES, LANES), 1)
        tri = (rows >= cols).astype(F32)
        bias = bf_ref[...]
        lfm = jnp.where(rows < N_META, _logsig(flm_ref[...] + bias), 0.0)
        cm = jnp.dot(tri, lfm, precision=lax.Precision.HIGHEST, preferred_element_type=F32)
        cumm_ref[...] = cm * LOG2E
        base = cm[LANES - 1:LANES, :]
        for b in range(B):
            def blk(i, carry):
                r0 = pl.multiple_of(i * LANES, LANES)
                lf = _logsig(fl_ref[b, pl.ds(r0, LANES), :] + bias)
                c = jnp.dot(tri, lf, precision=lax.Precision.HIGHEST,
                            preferred_element_type=F32) + carry
                cum_ref[b, pl.ds(r0, LANES), :] = c * LOG2E
                return c[LANES - 1:LANES, :]

            lax.fori_loop(0, nb, blk, base)

    return pl.pallas_call(
        body, name=name,
        out_shape=(jax.ShapeDtypeStruct((B, S, LANES), F32),
                   jax.ShapeDtypeStruct((LANES, LANES), F32)),
        compiler_params=_params(),
    )(fl, flm, bf)


def _cum_bwd(dc, dcm, fl, flm, bf, name):
    B, S, _ = fl.shape
    nb = S // LANES

    def body(dc_ref, dcm_ref, fl_ref, flm_ref, bf_ref, dfl_ref, dflm_ref, dbf_ref):
        rows = lax.broadcasted_iota(jnp.int32, (LANES, LANES), 0)
        cols = lax.broadcasted_iota(jnp.int32, (LANES, LANES), 1)
        triu = (rows <= cols).astype(F32)
        bias = bf_ref[...]
        total = jnp.zeros((1, LANES), F32)
        dbf = jnp.zeros((1, LANES), F32)
        for b in range(B):
            tail = jnp.zeros((1, LANES), F32)
            for t in range(nb):
                r0 = (nb - 1 - t) * LANES
                rc = jnp.dot(triu, dc_ref[b, r0:r0 + LANES, :], precision=lax.Precision.HIGHEST,
                             preferred_element_type=F32) + tail
                xv = fl_ref[b, r0:r0 + LANES, :] + bias
                d = rc / (1.0 + jnp.exp(xv))
                dfl_ref[b, r0:r0 + LANES, :] = d
                tail = rc[0:1, :]
                dbf = dbf + jnp.sum(d, axis=0, keepdims=True)
            total = total + tail
        rcm = jnp.dot(triu, dcm_ref[...], precision=lax.Precision.HIGHEST,
                      preferred_element_type=F32) + total
        dm = jnp.where(rows < N_META, rcm / (1.0 + jnp.exp(flm_ref[...] + bias)), 0.0)
        dflm_ref[...] = dm
        dbf_ref[...] = dbf + jnp.sum(dm, axis=0, keepdims=True)

    return pl.pallas_call(
        body, name=name,
        out_shape=(jax.ShapeDtypeStruct((B, S, LANES), F32),
                   jax.ShapeDtypeStruct((LANES, LANES), F32),
                   jax.ShapeDtypeStruct((1, LANES), F32)),
        compiler_params=_params(),
    )(dc, dcm, fl, flm, bf)


_NT = (((1,), (1,)), ((), ()))


def _attn_specs(S, NX, qw, v_col0):
    mb = NX // META_BLK
    vb = v_col0 // qw
    return (pl.BlockSpec((S, qw), lambda b, p: (b, p)),
            pl.BlockSpec((META_BLK, qw), lambda b, p: (mb, p)),
            pl.BlockSpec((S, qw), lambda b, p: (b, vb + p)),
            pl.BlockSpec((META_BLK, qw), lambda b, p: (mb, vb + p)),
            pl.BlockSpec((S, LANES), lambda b, p: (b, p)))


def _cum_specs(S, TK):
    return [pl.BlockSpec((1, 2, S // TK, 1, TK), lambda b, p: (b, p, 0, 0, 0)),
            pl.BlockSpec((1, 2, 1, META_BLK), lambda b, p: (b, p, 0, 0))]


LOG2E = 1.4426950408889634


def _attn_fwd(qn, kn, vsrc, v_col0, mla, scale, S, NX, name, ck=None, cmk=None, exch=None):
    T = qn.shape[0]
    B = NX // S
    qw = 2 * LANES if mla else LANES
    npair = qn.shape[1] // qw
    TQ = min(512, S)
    TK = TQ
    forget = ck is not None
    a = scale * LOG2E

    def body(*refs):
        if forget:
            q_ref, k_ref, km_ref, v_ref, vm_ref, ck_ref, cmk_ref, _, o_ref, lse_ref = refs
        else:
            q_ref, k_ref, km_ref, v_ref, vm_ref, _, o_ref, lse_ref = refs
        lo = lax.broadcasted_iota(jnp.int32, (1, LANES), 1) < HALF
        mcol = lax.broadcasted_iota(jnp.int32, (TQ, META_BLK), 1)
        causal = (lax.broadcasted_iota(jnp.int32, (TQ, TK), 0)
                  >= lax.broadcasted_iota(jnp.int32, (TQ, TK), 1))
        two = lax.broadcasted_iota(jnp.int32, (TQ, 2), 1)
        for qi in range(S // TQ):
            q0 = qi * TQ
            sls = [slice(e * LANES, (e + 1) * LANES) if mla else slice(None) for e in range(2)]
            if mla:
                qts = [q_ref[q0:q0 + TQ, sl] for sl in sls]
            else:
                qts = [jnp.where(lo if e == 0 else ~lo, q_ref[q0:q0 + TQ, :], 0.0).astype(BF16)
                       for e in range(2)]

            def step(e, kt, vt, c2, mask, carry):
                m, l, acc = carry
                s = lax.dot_general(qts[e], kt, _NT, preferred_element_type=F32) * a
                if forget:
                    s = s - c2
                if mask is not None:
                    s = jnp.where(mask, s, NEG)
                m2 = jnp.max(s, axis=1, keepdims=True)
                if m is not None:
                    m2 = jnp.maximum(m, m2)
                p = jnp.exp2(s - m2)
                l2 = jnp.sum(p, axis=1, keepdims=True)
                acc2 = jnp.dot(p.astype(BF16), vt.astype(BF16), preferred_element_type=F32)
                if m is not None:
                    alpha = jnp.exp2(m - m2)
                    l2, acc2 = alpha * l + l2, alpha * acc + acc2
                return m2, l2, acc2

            def both(rows, kj, mask, carry):
                return tuple(step(e, k_ref[rows, sls[e]], v_ref[rows, sls[e]],
                                  ck_ref[0, e, kj] if forget else None, mask, carry[e]) for e in range(2))

            def below(kj, carry):
                return both(pl.ds(pl.multiple_of(kj * TK, TK), TK), kj, None, carry)

            carry = tuple(step(e, km_ref[:, sls[e]], vm_ref[:, sls[e]], cmk_ref[0, e] if forget else None,
                               mcol < N_META, (None, None, None)) for e in range(2))
            if qi:
                carry = lax.fori_loop(0, qi, below, carry)
            carry = both(slice(q0, q0 + TK), qi, causal, carry)
            outs = [acc / l for _, l, acc in carry]
            lses = [m + jnp.log2(l) for m, l, _ in carry]
            first = pltpu.roll(outs[0], HALF, 1) if mla else outs[0]
            o_ref[q0:q0 + TQ, :] = jnp.where(lo, first, outs[1])
            lse_ref[0, 0, q0:q0 + TQ, :] = jnp.where(two == 0, lses[0], lses[1])

    qk, kmeta, vv, vmeta, pair = _attn_specs(S, NX, qw, v_col0)
    specs = [qk, qk, kmeta, vv, vmeta]
    ins = [qn, kn, kn, vsrc, vsrc]
    if forget:
        specs += _cum_specs(S, TK)
        ins += [ck, cmk]
    specs.append(pl.BlockSpec(memory_space=pl.ANY))
    ins.append(jnp.zeros((T, npair * LANES), F32))
    lse_spec = pl.BlockSpec((1, 1, S, 2), lambda b, p: (b, p, 0, 0))
    return _call(
        body, name, (B, npair), specs, [pair, lse_spec],
        [jax.ShapeDtypeStruct((T, npair * LANES), F32), jax.ShapeDtypeStruct((B, npair, S, 2), F32)],
        ins, sem=("parallel", "parallel"), aliases={len(ins) - 1: 0}, exch=exch)


def _attn_bwd(qn, kn, vsrc, v_col0, o, lse, do, mla, scale, S, NX, name, ck=None, cmk=None, exch=None):
    T, W = qn.shape
    B = NX // S
    qw = 2 * LANES if mla else LANES
    npair = W // qw
    TQ = min(512, S)
    TK = TQ
    forget = ck is not None
    a = scale * LOG2E
    _TN = (((0,), (0,)), ((), ()))

    def body(*refs):
        refs = list(refs)
        q_ref, k_ref, km_ref, v_ref, vm_ref, o_ref, do_ref, lse_ref = refs[:8]
        pos = 8
        if forget:
            ck_ref, cmk_ref = refs[8:10]
            pos = 10
        pos += 3
        dq_ref, dk_ref, dv_ref, dkm_ref, dvm_ref = refs[pos:pos + 5]
        if forget:
            dck_ref, dcm_ref, dcq_ref = refs[pos + 5:pos + 8]
            dck_ref[...] = jnp.zeros_like(dck_ref)
            dcm_ref[...] = jnp.zeros_like(dcm_ref)
        dk_ref[...] = jnp.zeros_like(dk_ref)
        dv_ref[...] = jnp.zeros_like(dv_ref)
        dkm_ref[...] = jnp.zeros_like(dkm_ref)
        dvm_ref[...] = jnp.zeros_like(dvm_ref)
        lo = lax.broadcasted_iota(jnp.int32, (1, LANES), 1) < HALF
        mcol = lax.broadcasted_iota(jnp.int32, (TQ, META_BLK), 1)
        causal = (lax.broadcasted_iota(jnp.int32, (TQ, TK), 0)
                  >= lax.broadcasted_iota(jnp.int32, (TQ, TK), 1))
        two = lax.broadcasted_iota(jnp.int32, (TQ, 2), 1)
        for qi in range(S // TQ):
            q0 = qi * TQ
            dof = do_ref[q0:q0 + TQ, :]
            prod = dof * o_ref[q0:q0 + TQ, :]
            lse2 = lse_ref[0, 0, q0:q0 + TQ, :]
            sls = [slice(e * LANES, (e + 1) * LANES) if mla else slice(None) for e in range(2)]
            mine = [lo, ~lo]
            if mla:
                qts = [q_ref[q0:q0 + TQ, sl] for sl in sls]
                dots = [jnp.where(lo, 0.0, pltpu.roll(dof, HALF, 1) if e == 0 else dof).astype(BF16)
                        for e in range(2)]
            else:
                qts = [jnp.where(mine[e], q_ref[q0:q0 + TQ, :], 0.0).astype(BF16) for e in range(2)]
                dots = [jnp.where(mine[e], dof, 0.0).astype(BF16) for e in range(2)]
            deltas = [jnp.sum(jnp.where(mine[e], prod, 0.0), axis=1, keepdims=True) for e in range(2)]
            lse_ts = [jnp.sum(jnp.where(two == e, lse2, 0.0), axis=1, keepdims=True) for e in range(2)]

            def grads(e, kt, vt, c2, mask):
                s = lax.dot_general(qts[e], kt, _NT, preferred_element_type=F32) * a
                if forget:
                    s = s - c2
                p = jnp.exp2(s - lse_ts[e])
                if mask is not None:
                    p = jnp.where(mask, p, 0.0)
                dp = lax.dot_general(dots[e], vt, _NT, preferred_element_type=F32)
                ds = p * (dp - deltas[e])
                dsb = ds.astype(BF16)
                return (jnp.dot(dsb, kt, preferred_element_type=F32),
                        lax.dot_general(dsb, qts[e], _TN, preferred_element_type=F32) * scale,
                        lax.dot_general(p.astype(BF16), dots[e], _TN, preferred_element_type=F32),
                        -jnp.sum(ds, axis=0, keepdims=True) if forget else None,
                        jnp.sum(ds, axis=1, keepdims=True) if forget else None)

            def block(k_at, v_at, dk_at, dv_at, c_at, dc_at, mask, dqs):
                got = [grads(e, k_at(sls[e]), v_at(sls[e]).astype(BF16), c_at(e) if forget else None, mask)
                       for e in range(2)]
                if mla:
                    for e in range(2):
                        dk_at(sls[e], got[e][1])
                        dv_at(sls[e], got[e][2])
                else:
                    dk_at(sls[0], got[0][1] + got[1][1])
                    dv_at(sls[0], got[0][2] + got[1][2])
                if forget:
                    for e in range(2):
                        dc_at(e, got[e][3])
                picks = (0, 0, 4, 4) if forget else (0, 0)
                new = tuple(got[i % 2][k] for i, k in enumerate(picks))
                return new if dqs is None else tuple(x + y for x, y in zip(dqs, new))

            def add_to(ref, *lead):
                def add(*idx_and_val):
                    *idx, val = idx_and_val
                    ref[(*lead, *idx)] += val
                return add

            def token_block(rows, kj, mask, dqs):
                return block(lambda sl: k_ref[rows, sl], lambda sl: v_ref[rows, sl],
                             lambda sl, val: add_to(dk_ref)(rows, sl, val),
                             lambda sl, val: add_to(dv_ref)(rows, sl, val),
                             lambda e: ck_ref[0, e, kj], lambda e, val: add_to(dck_ref, 0)(e, kj, val),
                             mask, dqs)

            dqs = block(lambda sl: km_ref[:, sl], lambda sl: vm_ref[:, sl],
                        lambda sl, val: add_to(dkm_ref, 0)(slice(None), sl, val),
                        lambda sl, val: add_to(dvm_ref, 0)(slice(None), sl, val),
                        lambda e: cmk_ref[0, e], lambda e, val: add_to(dcm_ref, 0)(e, val),
                        mcol < N_META, None)

            def below(kj, dqs):
                return token_block(pl.ds(pl.multiple_of(kj * TK, TK), TK), kj, None, dqs)

            if qi:
                dqs = lax.fori_loop(0, qi, below, dqs)
            dqs = token_block(slice(q0, q0 + TK), qi, causal, dqs)
            if forget:
                dcq_ref[0, 0, q0:q0 + TQ, :] = jnp.where(two == 0, dqs[2], dqs[3])
            if mla:
                for e in range(2):
                    dq_ref[q0:q0 + TQ, sls[e]] = dqs[e] * scale
            else:
                dq_ref[q0:q0 + TQ, :] = jnp.where(lo, dqs[0], dqs[1]) * scale

    qk, kmeta, vv, vmeta, pair = _attn_specs(S, NX, qw, v_col0)
    lse_spec = pl.BlockSpec((1, 1, S, 2), lambda b, p: (b, p, 0, 0))
    specs = [qk, qk, kmeta, vv, vmeta, pair, pair, lse_spec]
    ins = [qn, kn, kn, vsrc, vsrc, o, do, lse]
    if forget:
        specs += _cum_specs(S, TK)
        ins += [ck, cmk]
    first_alias = len(ins)
    specs += [pl.BlockSpec(memory_space=pl.ANY)] * 3
    ins += [jnp.zeros((T, W), F32)] * 3
    mspec = pl.BlockSpec((1, META_BLK, qw), lambda b, p: (b, 0, p))
    out_specs = [qk, qk, qk, mspec, mspec]
    tok = jax.ShapeDtypeStruct((T, W), F32)
    met = jax.ShapeDtypeStruct((B, META_BLK, W), F32)
    out_shape = [tok, tok, tok, met, met]
    if forget:
        out_specs += _cum_specs(S, TK) + [lse_spec]
        out_shape += [jax.ShapeDtypeStruct((B, HEADS, S // TK, 1, TK), F32),
                      jax.ShapeDtypeStruct((B, HEADS, 1, META_BLK), F32),
                      jax.ShapeDtypeStruct((B, npair, S, 2), F32)]
    return _call(
        body, name, (B, npair), specs, out_specs, out_shape, ins, sem=("parallel", "parallel"),
        aliases={first_alias: 0, first_alias + 1: 1, first_alias + 2: 2}, exch=exch)


def _gate_fwd(z, bg, of, om, name):
    T, D = of.shape
    tm = _tile(T, 640, 16)

    def body(z_ref, bg_ref, of_ref, om_ref, o_ref):
        bias = bg_ref[...]

        def step(rows, carry):
            gt = jax.nn.sigmoid(z_ref[rows, :] + bias)
            o_ref[rows, :] = (gt[:, :D] * of_ref[rows, :] + gt[:, D:] * om_ref[rows, :]).astype(BF16)
            return carry

        _by_chunks(tm, D, step)

    row = pl.BlockSpec((tm, D), lambda i: (i, 0))
    return pl.pallas_call(
        body, name=name, grid=(T // tm,),
        in_specs=[pl.BlockSpec((tm, 2 * D), lambda i: (i, 0)),
                  pl.BlockSpec((1, 2 * D), lambda i: (0, 0)), row, row],
        out_specs=row, out_shape=jax.ShapeDtypeStruct((T, D), BF16),
        compiler_params=_params(("parallel",)),
    )(z, bg, of, om)


def _gate_bwd(dmix, z, bg, of, om, name):
    T, D = of.shape
    tm = _tile(T, 640, 16)

    def body(dm_ref, z_ref, bg_ref, of_ref, om_ref, dgl_ref, dof_ref, dom_ref, dbg_ref):
        bias = bg_ref[...]

        def step(rows, dbg_sum):
            gt = jax.nn.sigmoid(z_ref[rows, :] + bias)
            dm = dm_ref[rows, :]
            dof_ref[rows, :] = (dm * gt[:, :D]).astype(BF16)
            dom_ref[rows, :] = (dm * gt[:, D:]).astype(BF16)
            dgl = jnp.concatenate([dm * of_ref[rows, :], dm * om_ref[rows, :]], axis=1) * gt * (1.0 - gt)
            dgl_ref[rows, :] = dgl.astype(BF16)
            return dbg_sum + jnp.sum(dgl, axis=0, keepdims=True)

        dbg = _by_chunks(tm, D, step, jnp.zeros((1, 2 * D), F32))

        @pl.when(pl.program_id(0) == 0)
        def _():
            dbg_ref[...] = jnp.zeros_like(dbg_ref)

        dbg_ref[...] += dbg

    row = pl.BlockSpec((tm, D), lambda i: (i, 0))
    wide = pl.BlockSpec((tm, 2 * D), lambda i: (i, 0))
    one = pl.BlockSpec((1, 2 * D), lambda i: (0, 0))
    return pl.pallas_call(
        body, name=name, grid=(T // tm,),
        in_specs=[row, wide, one, row, row], out_specs=(wide, row, row, one),
        out_shape=(jax.ShapeDtypeStruct((T, 2 * D), BF16), jax.ShapeDtypeStruct((T, D), BF16),
                   jax.ShapeDtypeStruct((T, D), BF16), jax.ShapeDtypeStruct((1, 2 * D), F32)),
        compiler_params=_params(("arbitrary",)),
    )(dmix, z, bg, of, om)


def _loss(h, tgt, n_valid, name):
    T, D = h.shape
    tm = _tile(T, 640, 16)

    def body(h_ref, t_ref, dh_ref, l_ref):
        i = pl.program_id(0)
        ch = _chunk_rows(D)
        row_in_chunk = lax.broadcasted_iota(jnp.int32, (ch, D), 0)

        def step(rows, part):
            err = jnp.where(row_in_chunk + (rows.start + i * tm) < n_valid, h_ref[rows, :] - t_ref[rows, :], 0.0)
            dh_ref[rows, :] = err * (1.0 / D)
            return part + jnp.sum(err * err, axis=0, keepdims=True)

        part = _by_chunks(tm, D, step, jnp.zeros((1, D), F32))

        @pl.when(i == 0)
        def _():
            l_ref[...] = jnp.zeros_like(l_ref)

        l_ref[...] += 0.5 * jnp.sum(part) * (1.0 / D)

    row = pl.BlockSpec((tm, D), lambda i: (i, 0))
    acc = pl.BlockSpec((8, LANES), lambda i: (0, 0))
    return pl.pallas_call(
        body, name=name, grid=(T // tm,), in_specs=[row, row], out_specs=(row, acc),
        out_shape=(jax.ShapeDtypeStruct((T, D), F32), jax.ShapeDtypeStruct((8, LANES), F32)),
        compiler_params=_params(("arbitrary",)),
    )(h, tgt)


def _adamw(parts, w, m, v, name):
    P, R, C = parts.shape
    tr = _tile(R, max(_chunk_rows(C), (1 << 18) // C), _chunk_rows(C))
    bc1 = 1.0 - ADAM_B1 ** ADAM_STEP
    bc2 = 1.0 - ADAM_B2 ** ADAM_STEP

    def body(p_ref, w_ref, m_ref, v_ref, g_ref, d_ref, m2_ref, v2_ref):
        def step(rows, carry):
            g = p_ref[0, rows, :].astype(F32)
            for j in range(1, P):
                g = g + p_ref[j, rows, :].astype(F32)
            m2 = ADAM_B1 * m_ref[rows, :] + (1.0 - ADAM_B1) * g
            v2 = ADAM_B2 * v_ref[rows, :] + (1.0 - ADAM_B2) * (g * g)
            m_hat = m2 / bc1
            v_hat = v2 / bc2
            g_ref[rows, :] = g
            d_ref[rows, :] = -ADAM_LR * (m_hat / (jnp.sqrt(v_hat) + ADAM_EPS) + ADAM_WD * w_ref[rows, :])
            m2_ref[rows, :] = m2
            v2_ref[rows, :] = v2
            return carry

        _by_chunks(tr, C, step)

    row = pl.BlockSpec((tr, C), lambda i: (i, 0))
    sh = jax.ShapeDtypeStruct((R, C), F32)
    return pl.pallas_call(
        body, name=name, grid=(R // tr,),
        in_specs=[pl.BlockSpec((P, tr, C), lambda i: (0, i, 0)), row, row, row],
        out_specs=(row, row, row, row), out_shape=(sh, sh, sh, sh),
        compiler_params=_params(("parallel",)),
    )(parts, w, m, v)


def _peer(d):
    x, y, c = lax.axis_index("x"), lax.axis_index("y"), lax.axis_index("c")
    px = 1 - x if d & 4 else x
    py = 1 - y if d & 2 else y
    pc = 1 - c if d & 1 else c
    return (px, py, pc), 4 * px + 2 * py + pc


class _Exchange:
    def __init__(self, srcs, gather):
        self.srcs, self.gather, self.n = list(srcs), gather, len(srcs)
        n = self.n
        hbm = pl.BlockSpec(memory_space=pl.ANY)
        self.in_specs = [hbm] * n
        self.out_specs = [hbm] * n
        self.out_shape = [jax.ShapeDtypeStruct((N_DEV,) + s.shape[-2:], s.dtype) for s in srcs]
        self.scratch = [pltpu.SemaphoreType.DMA((N_DEV - 1, n)), pltpu.SemaphoreType.DMA((N_DEV - 1, n)),
                        pltpu.SemaphoreType.DMA((n,))]

    def _copies(self, src_refs, out_refs, sems):
        send_sems, recv_sems, local_sems = sems
        _, me = _peer(0)

        def remote(w, d, landing):
            dev, lin = _peer(d)
            return pltpu.make_async_remote_copy(
                src_ref=src_refs[w] if self.gather else src_refs[w].at[lin],
                dst_ref=out_refs[w].at[lin if landing else me],
                send_sem=send_sems.at[d - 1, w], recv_sem=recv_sems.at[d - 1, w],
                device_id=dev, device_id_type=pl.DeviceIdType.MESH)

        pairs = [(w, d) for d in range(1, N_DEV) for w in range(self.n)]
        own = [pltpu.make_async_copy(src_refs[w] if self.gather else src_refs[w].at[me],
                                     out_refs[w].at[me], local_sems.at[w]) for w in range(self.n)]
        return own, [remote(w, d, False) for w, d in pairs], [remote(w, d, True) for w, d in pairs]

    def start(self, src_refs, out_refs, sems):
        own, sent, _ = self._copies(src_refs, out_refs, sems)
        for cp in own + sent:
            cp.start()

    def wait(self, src_refs, out_refs, sems):
        own, sent, landing = self._copies(src_refs, out_refs, sems)
        for cp in landing:
            cp.wait_recv()
        for cp in sent:
            cp.wait_send()
        for cp in own:
            cp.wait()


def _exchange(srcs, name, gather):
    ex = _Exchange(srcs, gather)
    n = ex.n

    def body(*refs):
        ex.start(refs[:n], refs[n:2 * n], refs[2 * n:])
        ex.wait(refs[:n], refs[n:2 * n], refs[2 * n:])

    outs = pl.pallas_call(
        body, name=name, in_specs=ex.in_specs, out_specs=tuple(ex.out_specs),
        out_shape=tuple(ex.out_shape), scratch_shapes=ex.scratch,
    )(*srcs)
    return list(outs)


def _call(body, name, grid, in_specs, out_specs, out_shape, ins, scratch_shapes=(), sem=None,
          aliases=None, exch=None):
    aliases = aliases or {}
    if exch is None:
        outs = pl.pallas_call(
            body, name=name, grid=grid, in_specs=list(in_specs), out_specs=tuple(out_specs),
            out_shape=tuple(out_shape), scratch_shapes=list(scratch_shapes),
            input_output_aliases=aliases, compiler_params=_params(sem),
        )(*ins)
        return list(outs), []
    ni, no, ns, n = len(in_specs), len(out_specs), len(scratch_shapes), exch.n
    last_ids = [g - 1 for g in grid]

    def hosted(*refs):
        cin, xin = refs[:ni], refs[ni:ni + n]
        cout, xout = refs[ni + n:ni + n + no], refs[ni + n + no:ni + 2 * n + no]
        cscr, xsem = refs[ni + 2 * n + no:ni + 2 * n + no + ns], refs[ni + 2 * n + no + ns:]
        ids = [pl.program_id(a) for a in range(len(grid))]
        first, last = ids[0] == 0, ids[0] == last_ids[0]
        for a in range(1, len(grid)):
            first, last = first & (ids[a] == 0), last & (ids[a] == last_ids[a])

        @pl.when(first)
        def _():
            exch.start(xin, xout, xsem)

        body(*cin, *cout, *cscr)

        @pl.when(last)
        def _():
            exch.wait(xin, xout, xsem)

    outs = pl.pallas_call(
        hosted, name=name, grid=grid, in_specs=list(in_specs) + exch.in_specs,
        out_specs=tuple(list(out_specs) + exch.out_specs),
        out_shape=tuple(list(out_shape) + exch.out_shape),
        scratch_shapes=list(scratch_shapes) + exch.scratch, input_output_aliases=aliases,
        compiler_params=_params(("arbitrary",) * len(grid)),
    )(*ins, *exch.srcs)
    return list(outs[:no]), list(outs[no:])


def _pack(arrs, cols, row_mult):
    flat = jnp.concatenate([a.reshape(-1) for a in arrs])
    n = flat.shape[0]
    quantum = cols * row_mult
    total = -(-n // quantum) * quantum
    return jnp.pad(flat, (0, total - n)).reshape(total // cols, cols)


def _pack_rows(arrs, cols, row_mult):
    flat = jnp.concatenate(arrs, axis=1)
    n = flat.shape[1]
    quantum = cols * row_mult
    total = -(-n // quantum) * quantum
    return jnp.pad(flat, ((0, 0), (0, total - n))).reshape(N_DEV, total // cols, cols)


def _unpack(packed, shapes):
    flat = packed.reshape(-1)
    out, off = [], 0
    for s in shapes:
        n = int(np.prod(s))
        out.append(flat[off:off + n].reshape(s))
        off += n
    return out


def _rope_tables(positions):
    inv_freq = ROPE_THETA ** (-jnp.arange(0, MLA_ROPE, 2, dtype=F32) / MLA_ROPE)
    ang = positions.astype(F32)[:, None] * inv_freq[None, :]
    cos, sin = jnp.cos(ang), jnp.sin(ang)
    n = positions.shape[0]
    ones, zeros = jnp.ones((n, MLA_NOPE), F32), jnp.zeros((n, MLA_NOPE), F32)
    tail1, tail0 = jnp.ones((n, LANES - MLA_QK), F32), jnp.zeros((n, LANES - MLA_QK), F32)
    z16 = jnp.zeros((n, 16), F32)
    c = jnp.concatenate([ones, cos, cos, tail1], axis=1)
    s1 = jnp.concatenate([zeros, -sin, z16, tail0], axis=1)
    s2 = jnp.concatenate([zeros, z16, sin, tail0], axis=1)
    return c, s1, s2


def kernel(x, meta_tokens, ffn1_norm, ffn1_w_gu, ffn1_w_down, mix_norm, w_in, b_forget, b_gate, fox_q_norm, fox_k_norm, mla_cq_norm, mla_w_uq, mla_ckv_norm, mla_w_ukv, mla_q_norm, mla_k_norm, w_branch_fox, w_branch_mla, w_out, ffn2_norm, ffn2_w_gu, ffn2_w_down, loss_target, m_meta_tokens, m_ffn1_norm, m_ffn1_w_gu, m_ffn1_w_down, m_mix_norm, m_w_in, m_b_forget, m_b_gate, m_fox_q_norm, m_fox_k_norm, m_mla_cq_norm, m_mla_w_uq, m_mla_ckv_norm, m_mla_w_ukv, m_mla_q_norm, m_mla_k_norm, m_w_branch_fox, m_w_branch_mla, m_w_out, m_ffn2_norm, m_ffn2_w_gu, m_ffn2_w_down, v_meta_tokens, v_ffn1_norm, v_ffn1_w_gu, v_ffn1_w_down, v_mix_norm, v_w_in, v_b_forget, v_b_gate, v_fox_q_norm, v_fox_k_norm, v_mla_cq_norm, v_mla_w_uq, v_mla_ckv_norm, v_mla_w_ukv, v_mla_q_norm, v_mla_k_norm, v_w_branch_fox, v_w_branch_mla, v_w_out, v_ffn2_norm, v_ffn2_w_gu, v_ffn2_w_down):
    names = ["meta_tokens", "ffn1_norm", "ffn1_w_gu", "ffn1_w_down", "mix_norm", "w_in", "b_forget",
             "b_gate", "fox_q_norm", "fox_k_norm", "mla_cq_norm", "mla_w_uq", "mla_ckv_norm",
             "mla_w_ukv", "mla_q_norm", "mla_k_norm", "w_branch_fox", "w_branch_mla", "w_out",
             "ffn2_norm", "ffn2_w_gu", "ffn2_w_down"]
    W = dict(zip(names, [meta_tokens, ffn1_norm, ffn1_w_gu, ffn1_w_down, mix_norm, w_in, b_forget,
                         b_gate, fox_q_norm, fox_k_norm, mla_cq_norm, mla_w_uq, mla_ckv_norm,
                         mla_w_ukv, mla_q_norm, mla_k_norm, w_branch_fox, w_branch_mla, w_out,
                         ffn2_norm, ffn2_w_gu, ffn2_w_down]))
    Mo = dict(zip(names, [m_meta_tokens, m_ffn1_norm, m_ffn1_w_gu, m_ffn1_w_down, m_mix_norm, m_w_in,
                          m_b_forget, m_b_gate, m_fox_q_norm, m_fox_k_norm, m_mla_cq_norm,
                          m_mla_w_uq, m_mla_ckv_norm, m_mla_w_ukv, m_mla_q_norm, m_mla_k_norm,
                          m_w_branch_fox, m_w_branch_mla, m_w_out, m_ffn2_norm, m_ffn2_w_gu,
                          m_ffn2_w_down]))
    Vo = dict(zip(names, [v_meta_tokens, v_ffn1_norm, v_ffn1_w_gu, v_ffn1_w_down, v_mix_norm, v_w_in,
                          v_b_forget, v_b_gate, v_fox_q_norm, v_fox_k_norm, v_mla_cq_norm,
                          v_mla_w_uq, v_mla_ckv_norm, v_mla_w_ukv, v_mla_q_norm, v_mla_k_norm,
                          v_w_branch_fox, v_w_branch_mla, v_w_out, v_ffn2_norm, v_ffn2_w_gu,
                          v_ffn2_w_down]))

    B, S, D = x.shape
    NX = B * S
    T = NX + META_BLK
    H = HEADS
    assert NX % META_BLK == 0 and S % LANES == 0
    me = 4 * lax.axis_index("x") + 2 * lax.axis_index("y") + lax.axis_index("c")

    big = [("ffn1_w_gu", 1), ("ffn1_w_down", 0), ("w_in", 1), ("mla_w_uq", 1), ("mla_w_ukv", 1),
           ("w_branch_fox", 1), ("w_branch_mla", 1), ("w_out", 0), ("ffn2_w_gu", 1), ("ffn2_w_down", 0)]
    mix_small = ["mla_w_uq", "mla_w_ukv", "w_branch_fox", "w_branch_mla", "w_out"]
    last_group = ["ffn2_w_gu", "ffn2_w_down"]
    axis_of = dict(big)
    full = {}

    def shards(group):
        return [W[n][0].astype(BF16) for n in group]

    def assemble(group, blks):
        for n, blk in zip(group, blks):
            _, r, c = blk.shape
            full[n] = (blk.transpose(1, 0, 2).reshape(r, N_DEV * c) if axis_of[n] == 1
                       else blk.reshape(N_DEV * r, c))

    got = _exchange(shards(["ffn1_w_gu"]) + [meta_tokens], "gather_first", gather=True)
    assemble(["ffn1_w_gu"], got[:1])
    meta_full = got[1].transpose(1, 0, 2).reshape(N_META, D)

    Z_G, Z_FQ = 0, 2 * D
    Z_FK, Z_FV = Z_FQ + FOX_W, Z_FQ + 2 * FOX_W
    Z_CQ = Z_FQ + 3 * FOX_W
    Z_CKV = Z_CQ + MLA_Q_RANK
    Z_F = Z_CKV + MLA_KV_RANK
    Z_KR = Z_F + LANES

    def pad_lanes(a, w=LANES):
        return jnp.pad(a, [(0, 0)] * (a.ndim - 1) + [(0, w - a.shape[-1])])

    def rows_T(real, meta=None):
        n = real.shape[1]
        parts = [real]
        used = 0
        if meta is not None:
            parts.append(meta)
            used = meta.shape[0]
        if T - NX - used:
            parts.append(jnp.zeros((T - NX - used, n), real.dtype))
        return jnp.concatenate(parts, axis=0)

    def put_meta(tok, meta_per_seq):
        return lax.dynamic_update_slice(tok, meta_per_seq.sum(0), (NX, 0))

    h0 = rows_T(x.reshape(NX, D), meta_full)
    tgt = rows_T(loss_target.reshape(NX, D))

    def ffn_fwd(h, norm, w_gu, tag, behind_up=None, behind_down=None):
        u = _norm_fwd(h, 0, D, D, norm, D, D, tag + "_norm")
        (g, up, a), got = _ffn_up(u, w_gu, tag + "_up",
                                  exch=_Exchange(shards(behind_up), True) if behind_up else None)
        assemble(behind_up or [], got)
        h_out = _mm(a, full[tag + "_w_down"], "nn", tag + "_down", scale=0.5, res=h,
                    exch=_Exchange(shards(behind_down), True) if behind_down else None)
        if behind_down:
            h_out, got = h_out
            assemble(behind_down, got)
        return h_out, (u, g, up, a)

    h1, ffn1_saved = ffn_fwd(h0, W["ffn1_norm"], full["ffn1_w_gu"], "ffn1",
                             behind_up=["ffn1_w_down"], behind_down=["w_in"])
    wi = full["w_in"]
    o_fq = 0
    o_f = 3 * FOX_W
    o_cq = o_f + HEADS
    o_kr = o_cq + MLA_Q_RANK + MLA_KV_RANK
    o_g = o_kr + MLA_ROPE
    w_in_p = jnp.concatenate([
        wi[:, o_g:o_g + 2 * D], wi[:, o_fq:o_f], wi[:, o_cq:o_kr],
        jnp.pad(wi[:, o_f:o_cq], ((0, 0), (0, LANES - HEADS))),
        jnp.pad(wi[:, o_kr:o_g], ((0, 0), (0, LANES - MLA_ROPE)))], axis=1)

    u2 = _norm_fwd(h1, 0, D, D, W["mix_norm"], D, D, "mix_norm")
    z, got = _mm(u2, w_in_p, "nn", "w_in", exch=_Exchange(shards(mix_small), True))
    assemble(mix_small, got)
    w_uq_p = jnp.pad(full["mla_w_uq"].reshape(MLA_Q_RANK, H, MLA_QK),
                     ((0, 0), (0, 0), (0, LANES - MLA_QK))).reshape(MLA_Q_RANK, H * LANES)

    gq_f = jnp.tile(W["fox_q_norm"], (1, 2))
    gk_f = jnp.tile(W["fox_k_norm"], (1, 2))
    fqn = _norm_fwd(z, Z_FQ, FOX_W, LANES, gq_f, FOX_HD, FOX_HD, "fox_q_norm")
    fkn = _norm_fwd(z, Z_FK, FOX_W, LANES, gk_f, FOX_HD, FOX_HD, "fox_k_norm")
    fl = z[:NX, Z_F:Z_F + LANES].reshape(B, S, LANES)
    flm = z[NX:, Z_F:Z_F + LANES]
    bf = pad_lanes(W["b_forget"])
    cum, cumm = _cum_fwd(fl, flm, bf, "forget_cum")
    TK = min(512, S)
    ck = cum[:, :, :H].transpose(0, 2, 1).reshape(B, H, S // TK, 1, TK)
    cmk = jnp.broadcast_to(cumm[:, :H].T[None, :, None, :], (B, H, 1, META_BLK))
    (o_fox, lse_fox), got = _attn_fwd(fqn, fkn, z, Z_FV, False, FOX_HD ** -0.5, S, NX, "fox_attn", ck, cmk,
                                      exch=_Exchange(shards(last_group), True))
    assemble(last_group, got)
    of = _mm(o_fox, full["w_branch_fox"], "nn", "branch_fox")

    pos = jnp.concatenate([jnp.tile(jnp.arange(S) + N_META, B), jnp.arange(META_BLK)])
    tabs = _rope_tables(pos)
    cqn = _norm_fwd(z, Z_CQ, MLA_Q_RANK, MLA_Q_RANK, W["mla_cq_norm"], MLA_Q_RANK, MLA_Q_RANK, "mla_cq_norm")
    q_lin = _mm(cqn, w_uq_p, "nn", "mla_uq")
    ckvn = _norm_fwd(z, Z_CKV, MLA_KV_RANK, MLA_KV_RANK, W["mla_ckv_norm"], MLA_KV_RANK, MLA_KV_RANK,
                     "mla_ckv_norm")
    kv_lin = _mm(ckvn, full["mla_w_ukv"], "nn", "mla_ukv")
    gq_m, gk_m = pad_lanes(W["mla_q_norm"]), pad_lanes(W["mla_k_norm"])
    mqn = _norm_fwd(q_lin, 0, H * LANES, LANES, gq_m, LANES, MLA_QK, "mla_q_norm", tabs=tabs)
    mkn = _mla_k_fwd(kv_lin, z, Z_KR, gk_m, tabs, "mla_k_norm")
    (o_mla, lse_mla), _ = _attn_fwd(mqn, mkn, kv_lin, 0, True, MLA_QK ** -0.5, S, NX, "mla_attn")
    om = _mm(o_mla, full["w_branch_mla"], "nn", "branch_mla")

    mix = _gate_fwd(z, W["b_gate"], of, om, "gate_mix")
    h2 = _mm(mix, full["w_out"], "nn", "w_out", res=h1)

    h3, ffn2_saved = ffn_fwd(h2, W["ffn2_norm"], full["ffn2_w_gu"], "ffn2")

    dh3, loss_acc = _loss(h3, tgt, NX, "loss")
    loss = lax.psum(loss_acc[0, 0], AXES)

    G = {}
    parts = {}

    def scatter_of(group):
        per_dest = []
        for n in group:
            r, c = W[n].shape[1:]
            per_dest.append((G[n].reshape(r, N_DEV, c).transpose(1, 0, 2) if axis_of[n] == 1
                             else G[n].reshape(N_DEV, r, c)).astype(BF16))
        return _Exchange(per_dest, False)

    def ffn_bwd(dh, h, norm, w_gu, w_down, saved, tag, behind_down=None, spread=False):
        u, g, up, a = saved
        G[tag + "_w_down"] = _mm(a, dh, "tn", tag + "_dw_down", scale=0.5)
        (dg, dup), got = _ffn_down_bwd(dh, w_down, g, up, tag + "_down_bwd",
                                       exch=scatter_of(behind_down) if behind_down else None)
        parts.update(zip(behind_down or [], got))
        dw_g = _mm(u, dg, "tn", tag + "_dw_g", exch=scatter_of([tag + "_w_down"]) if spread else None)
        if spread:
            dw_g, got = dw_g
            parts[tag + "_w_down"] = got[0]
        G[tag + "_w_gu"] = jnp.concatenate([dw_g, _mm(u, dup, "tn", tag + "_dw_u")], axis=1)
        (du,), got = _ffn_up_bwd_dx(dg, dup, w_gu, tag + "_up_bwd",
                                    exch=scatter_of([tag + "_w_gu"]) if spread else None)
        if spread:
            parts[tag + "_w_gu"] = got[0]
        dh_in, G[tag + "_norm"] = _norm_bwd(h, 0, D, D, norm, D, D, du, tag + "_norm_bwd", res=dh)
        return dh_in

    dh2 = ffn_bwd(dh3, h2, W["ffn2_norm"], full["ffn2_w_gu"], full["ffn2_w_down"], ffn2_saved, "ffn2")

    G["w_out"] = _mm(mix, dh2, "tn", "dw_out")
    dmix = _mm(dh2, full["w_out"], "nt", "w_out_bwd")
    dgl, dof, dom, G["b_gate"] = _gate_bwd(dmix, z, W["b_gate"], of, om, "gate_bwd")

    G["w_branch_fox"] = _mm(o_fox, dof, "tn", "dw_branch_fox")
    do_fox = _mm(dof, full["w_branch_fox"], "nt", "branch_fox_bwd")
    (dq_f, dk_f, dv_f, dkm_f, dvm_f, dck, dcmk, dcq), got = _attn_bwd(
        fqn, fkn, z, Z_FV, o_fox, lse_fox, do_fox, False, FOX_HD ** -0.5, S, NX, "fox_attn_bwd", ck, cmk,
        exch=scatter_of(last_group))
    parts.update(zip(last_group, got))
    dk_f, dv_f = put_meta(dk_f, dkm_f), put_meta(dv_f, dvm_f)
    dfq, gq = _norm_bwd(z, Z_FQ, FOX_W, LANES, gq_f, FOX_HD, FOX_HD, dq_f, "fox_q_norm_bwd", out_dtype=BF16)
    dfk, gk = _norm_bwd(z, Z_FK, FOX_W, LANES, gk_f, FOX_HD, FOX_HD, dk_f, "fox_k_norm_bwd", out_dtype=BF16)
    G["fox_q_norm"] = gq[:, :FOX_HD] + gq[:, FOX_HD:]
    G["fox_k_norm"] = gk[:, :FOX_HD] + gk[:, FOX_HD:]
    dc = pad_lanes(dck.reshape(B, H, S).transpose(0, 2, 1)
                   + dcq.transpose(0, 2, 1, 3).reshape(B, S, H))
    dcm = pad_lanes(dcmk.sum(0)[:, 0, :].T)
    dcm = jnp.where(jnp.arange(LANES)[:, None] < N_META, dcm, 0.0)
    dfl, dflm, dbf = _cum_bwd(dc, dcm, fl, flm, bf, "forget_cum_bwd")
    G["b_forget"] = dbf[:, :HEADS]
    dfl_t = rows_T(dfl.reshape(NX, LANES), dflm)

    G["w_branch_mla"] = _mm(o_mla, dom, "tn", "dw_branch_mla")
    do_mla = _mm(dom, full["w_branch_mla"], "nt", "branch_mla_bwd")
    (dq_m, dk_m, dvk, dkm_m, dvkm), _ = _attn_bwd(
        mqn, mkn, kv_lin, 0, o_mla, lse_mla, do_mla, True, MLA_QK ** -0.5, S, NX, "mla_attn_bwd")
    dk_m, dvk = put_meta(dk_m, dkm_m), put_meta(dvk, dvkm)
    dq_lin, gq = _norm_bwd(q_lin, 0, H * LANES, LANES, gq_m, LANES, MLA_QK, dq_m, "mla_q_norm_bwd", tabs=tabs)
    G["mla_q_norm"] = gq[:, :MLA_QK]
    G["mla_w_uq"] = _mm(cqn, dq_lin, "tn", "dw_uq").reshape(MLA_Q_RANK, H, LANES)[:, :, :MLA_QK].reshape(
        MLA_Q_RANK, H * MLA_QK)
    dcqn = _mm(dq_lin, w_uq_p, "nt", "mla_uq_bwd")
    dcq, G["mla_cq_norm"] = _norm_bwd(z, Z_CQ, MLA_Q_RANK, MLA_Q_RANK, W["mla_cq_norm"], MLA_Q_RANK,
                                      MLA_Q_RANK, dcqn, "mla_cq_norm_bwd", out_dtype=BF16)
    dkv_lin, dkr, gk = _mla_k_bwd(kv_lin, z, Z_KR, gk_m, tabs, dk_m, dvk, "mla_k_norm_bwd")
    G["mla_k_norm"] = gk[:, :MLA_QK]
    G["mla_w_ukv"] = _mm(ckvn, dkv_lin, "tn", "dw_ukv")
    dckvn = _mm(dkv_lin, full["mla_w_ukv"], "nt", "mla_ukv_bwd")
    dckv, G["mla_ckv_norm"] = _norm_bwd(z, Z_CKV, MLA_KV_RANK, MLA_KV_RANK, W["mla_ckv_norm"], MLA_KV_RANK,
                                        MLA_KV_RANK, dckvn, "mla_ckv_norm_bwd", out_dtype=BF16)

    dz = jnp.concatenate([dgl, dfq, dfk, dv_f.astype(BF16), dcq, dckv, dfl_t.astype(BF16),
                          dkr.astype(BF16)], axis=1)
    dw_in_p = _mm(u2, dz, "tn", "dw_in")
    G["w_in"] = jnp.concatenate([
        dw_in_p[:, Z_FQ:Z_CQ], dw_in_p[:, Z_F:Z_F + HEADS], dw_in_p[:, Z_CQ:Z_F],
        dw_in_p[:, Z_KR:Z_KR + MLA_ROPE], dw_in_p[:, Z_G:Z_G + 2 * D]], axis=1)
    du2, got = _mm(dz, w_in_p, "nt", "w_in_bwd", exch=scatter_of(mix_small))
    parts.update(zip(mix_small, got))
    dh1, G["mix_norm"] = _norm_bwd(h1, 0, D, D, W["mix_norm"], D, D, du2, "mix_norm_bwd", res=dh2)

    dh0 = ffn_bwd(dh1, h0, W["ffn1_norm"], full["ffn1_w_gu"], full["ffn1_w_down"], ffn1_saved, "ffn1",
                  behind_down=["w_in"], spread=True)
    grad_x = dh0[:NX].reshape(B, S, D)
    G["meta_tokens"] = dh0[NX:NX + N_META]

    res = {}
    for n, _ in big:
        outs4 = _adamw(parts[n], W[n][0], Mo[n][0], Vo[n][0], "adamw_" + n)
        for key, arr in zip(("g", "d", "m", "v"), outs4):
            res[key, n] = arr[None]

    small = [n for n in names if n not in dict(big) and n != "meta_tokens"]
    small_shapes = [W[n].shape for n in small]
    spack = _pack([G["meta_tokens"]] + [G[n] for n in small], 1024, 8)
    (sparts,) = _exchange([spack], "gather_small_grads", gather=True)
    sflat = sparts.reshape(N_DEV, -1)
    dsh = D // N_DEV
    meta_part = lax.dynamic_slice(sflat[:, :N_META * D].reshape(N_DEV, N_META, D),
                                  (0, 0, me * dsh), (N_DEV, N_META, dsh)).reshape(N_DEV, -1)
    rep_len = sum(int(np.prod(s)) for s in small_shapes)
    rep_part = sflat[:, N_META * D:N_META * D + rep_len]
    sp = _pack_rows([meta_part, rep_part], LANES, _chunk_rows(LANES))
    pks = lambda src: _pack([src["meta_tokens"]] + [src[n] for n in small], LANES, _chunk_rows(LANES))
    g_s, d_s, m_s, v_s = _adamw(sp, pks(W), pks(Mo), pks(Vo), "adamw_small")
    shapes_s = [W["meta_tokens"].shape] + small_shapes
    for key, packed in (("g", g_s), ("d", d_s), ("m", m_s), ("v", v_s)):
        for n, arr in zip(["meta_tokens"] + small, _unpack(packed, shapes_s)):
            res[key, n] = arr

    outs = [loss, grad_x]
    for key in ("g", "d", "m", "v"):
        outs += [res[key, n] for n in names]
    return tuple(outs)
```

```python
import numpy as np
import jax
import jax.numpy as jnp
from jax import lax
from jax.experimental import pallas as pl
from jax.experimental.pallas import tpu as pltpu

F32 = jnp.float32
BF16 = jnp.bfloat16

N_META = 16
EPS = 1e-6
HEADS = 8
FOX_HD = 64
FOX_W = HEADS * FOX_HD
MLA_Q_RANK = 256
MLA_KV_RANK = 128
MLA_NOPE = 64
MLA_ROPE = 32
MLA_QK = MLA_NOPE + MLA_ROPE
MLA_V = 64
ROPE_THETA = 10000.0
LANES = 128
HALF = LANES // 2
META_BLK = 128
NEG = -1e30

ADAM_LR = 0.001
ADAM_B1 = 0.9
ADAM_B2 = 0.999
ADAM_EPS = 1e-08
ADAM_WD = 0.01
ADAM_STEP = 10

N_DEV = 8
AXES = ("x", "y", "c")
VMEM_LIMIT_BYTES = 56 * 1024 * 1024


def _tile(n, cap, mult):
    best = None
    for d in range(mult, min(n, cap) + 1, mult):
        if n % d == 0:
            best = d
    return n if best is None else best


VREG_ELEMS = 8 * LANES


def _row_tile(rows, width):
    return _tile(rows, max(16, (1 << 19) // width), 16)


def _chunk_rows(width):
    rows = 16
    while 2 * rows * width <= 8 * VREG_ELEMS:
        rows *= 2
    return rows


def _by_chunks(rows, width, step, init=()):
    return step(pl.ds(0, rows), init)


def _params(sem=None):
    return pltpu.CompilerParams(dimension_semantics=sem, vmem_limit_bytes=VMEM_LIMIT_BYTES)


def _mm(a, b, mode, name, out_dtype=F32, scale=1.0, res=None, exch=None):
    if mode == "nn":
        (M, K), (K2, N) = a.shape, b.shape
    elif mode == "nt":
        (M, K), (N, K2) = a.shape, b.shape
    else:
        (K, M), (K2, N) = a.shape, b.shape
    assert K == K2, (a.shape, b.shape, mode)
    if mode == "tn":
        tm, tk = _tile(M, 1408, 128), _tile(K, 2080, 16)
    else:
        tm, tk = _tile(M, 640, 16), _tile(K, 4224, 128)
    tn = _tile(N, 1408, 128)
    nk = K // tk
    a_spec = {"nn": pl.BlockSpec((tm, tk), lambda i, j, k: (i, k)),
              "nt": pl.BlockSpec((tm, tk), lambda i, j, k: (i, k)),
              "tn": pl.BlockSpec((tk, tm), lambda i, j, k: (k, i))}[mode]
    b_spec = {"nn": pl.BlockSpec((tk, tn), lambda i, j, k: (k, j)),
              "nt": pl.BlockSpec((tn, tk), lambda i, j, k: (j, k)),
              "tn": pl.BlockSpec((tk, tn), lambda i, j, k: (k, j))}[mode]
    dims = {"nn": (((1,), (0,)), ((), ())), "nt": (((1,), (1,)), ((), ())),
            "tn": (((0,), (0,)), ((), ()))}[mode]
    o_spec = pl.BlockSpec((tm, tn), lambda i, j, k: (i, j))
    has_res = res is not None

    def body(*refs):
        a_ref, b_ref = refs[:2]
        r_ref = refs[2] if has_res else None
        o_ref = refs[2 + has_res]

        def finish(acc):
            o = acc * scale
            if has_res:
                o = o + r_ref[...]
            o_ref[...] = o.astype(out_dtype)

        prod = lax.dot_general(a_ref[...].astype(BF16), b_ref[...].astype(BF16), dims,
                               preferred_element_type=F32)
        if nk == 1:
            finish(prod)
            return
        acc_ref = refs[3 + has_res]
        k = pl.program_id(2)

        @pl.when(k == 0)
        def _():
            acc_ref[...] = prod

        @pl.when((k > 0) & (k < nk - 1))
        def _():
            acc_ref[...] += prod

        @pl.when(k == nk - 1)
        def _():
            finish(acc_ref[...] + prod)

    ins = [a, b] + ([res] if has_res else [])
    specs = [a_spec, b_spec] + ([o_spec] if has_res else [])
    (out,), got = _call(
        body, name, (M // tm, N // tn, nk), specs, [o_spec], [jax.ShapeDtypeStruct((M, N), out_dtype)],
        ins, scratch_shapes=[pltpu.VMEM((tm, tn), F32)] if nk > 1 else [],
        sem=("parallel", "parallel", "arbitrary"), exch=exch)
    return out if exch is None else (out, got)


def _rope_fwd(y, c, s1, s2):
    return y * c + pltpu.roll(y, LANES - 16, 1) * s1 + pltpu.roll(y, 16, 1) * s2


def _rope_bwd(dy, c, s1, s2):
    return dy * c + pltpu.roll(dy * s1, 16, 1) + pltpu.roll(dy * s2, LANES - 16, 1)


def _group_sum(v, seg):
    if seg == v.shape[-1]:
        return jnp.sum(v, axis=-1, keepdims=True)
    lo = lax.broadcasted_iota(jnp.int32, v.shape, 1) < seg
    s_lo = jnp.sum(jnp.where(lo, v, 0.0), axis=-1, keepdims=True)
    s_hi = jnp.sum(jnp.where(lo, 0.0, v), axis=-1, keepdims=True)
    return jnp.where(lo, s_lo, s_hi)


def _norm_fwd(src, col0, width, bw, gain, seg, d_true, name, tabs=None, out_dtype=BF16):
    T = src.shape[0]
    tr = _row_tile(T, bw)
    inv_d = 1.0 / d_true
    c0 = col0 // bw
    assert col0 % bw == 0 and width % bw == 0

    def body(*refs):
        if tabs is None:
            x_ref, g_ref, o_ref = refs
        else:
            x_ref, g_ref, c_ref, s1_ref, s2_ref, o_ref = refs
        gain_v = g_ref[...]

        def step(rows, carry):
            xv = x_ref[rows, :]
            r = lax.rsqrt(_group_sum(xv * xv, seg) * inv_d + EPS)
            y = xv * r * gain_v
            if tabs is not None:
                y = _rope_fwd(y, c_ref[rows, :], s1_ref[rows, :], s2_ref[rows, :])
            o_ref[rows, :] = y.astype(out_dtype)
            return carry

        _by_chunks(tr, bw, step)

    specs = [pl.BlockSpec((tr, bw), lambda i, j: (i, c0 + j)), pl.BlockSpec((1, bw), lambda i, j: (0, 0))]
    ins = [src, gain]
    if tabs is not None:
        tab = pl.BlockSpec((tr, LANES), lambda i, j: (i, 0))
        specs += [tab, tab, tab]
        ins += list(tabs)
    return pl.pallas_call(
        body, name=name, grid=(T // tr, width // bw), in_specs=specs,
        out_specs=pl.BlockSpec((tr, bw), lambda i, j: (i, j)),
        out_shape=jax.ShapeDtypeStruct((T, width), out_dtype),
        compiler_params=_params(("parallel", "parallel")),
    )(*ins)


def _norm_bwd_math(xv, gain, dyv, seg, inv_d):
    r = lax.rsqrt(_group_sum(xv * xv, seg) * inv_d + EPS)
    gy = dyv * gain
    dot = _group_sum(gy * xv, seg)
    dx = r * gy - xv * (r * r * r * inv_d) * dot
    return dx, jnp.sum(dyv * xv * r, axis=0, keepdims=True)


def _norm_bwd(src, col0, width, bw, gain, seg, d_true, dy, name, tabs=None, res=None, out_dtype=F32):
    T = src.shape[0]
    tr = _row_tile(T, bw)
    inv_d = 1.0 / d_true
    c0 = col0 // bw
    has_res = res is not None

    def body(*refs):
        refs = list(refs)
        x_ref, g_ref, dy_ref = refs[:3]
        pos = 3
        if tabs is not None:
            c_ref, s1_ref, s2_ref = refs[3:6]
            pos = 6
        if has_res:
            r_ref = refs[pos]
            pos += 1
        dx_ref, dg_ref = refs[pos], refs[pos + 1]
        gain_v = g_ref[...]

        def step(rows, dg_sum):
            dyv = dy_ref[rows, :].astype(F32)
            if tabs is not None:
                dyv = _rope_bwd(dyv, c_ref[rows, :], s1_ref[rows, :], s2_ref[rows, :])
            dx, dg = _norm_bwd_math(x_ref[rows, :], gain_v, dyv, seg, inv_d)
            if has_res:
                dx = dx + r_ref[rows, :]
            dx_ref[rows, :] = dx.astype(out_dtype)
            return dg_sum + dg

        dg = _by_chunks(tr, bw, step, jnp.zeros((1, bw), F32))

        @pl.when((pl.program_id(0) == 0) & (pl.program_id(1) == 0))
        def _():
            dg_ref[...] = jnp.zeros_like(dg_ref)

        dg_ref[...] += dg

    blk = pl.BlockSpec((tr, bw), lambda i, j: (i, j))
    one = pl.BlockSpec((1, bw), lambda i, j: (0, 0))
    specs = [pl.BlockSpec((tr, bw), lambda i, j: (i, c0 + j)), one, blk]
    ins = [src, gain, dy]
    if tabs is not None:
        tab = pl.BlockSpec((tr, LANES), lambda i, j: (i, 0))
        specs += [tab, tab, tab]
        ins += list(tabs)
    if has_res:
        specs.append(blk)
        ins.append(res)
    return pl.pallas_call(
        body, name=name, grid=(T // tr, width // bw), in_specs=specs, out_specs=(blk, one),
        out_shape=(jax.ShapeDtypeStruct((T, width), out_dtype), jax.ShapeDtypeStruct((1, bw), F32)),
        compiler_params=_params(("arbitrary", "arbitrary")),
    )(*ins)


def _mla_k_raw(kv, kr):
    lane = lax.broadcasted_iota(jnp.int32, kv.shape, 1)
    return jnp.where(lane < MLA_NOPE, kv, jnp.where(lane < MLA_QK, pltpu.roll(kr, MLA_NOPE, 1), 0.0))


def _mla_k_fwd(kv_lin, z, kr_col, gain, tabs, name):
    T, W = kv_lin.shape
    tr = _row_tile(T, LANES)
    krb = kr_col // LANES
    inv_d = 1.0 / MLA_QK

    def body(kv_ref, kr_ref, g_ref, c_ref, s1_ref, s2_ref, o_ref):
        gain_v = g_ref[...]

        def step(rows, carry):
            xv = _mla_k_raw(kv_ref[rows, :], kr_ref[rows, :])
            r = lax.rsqrt(jnp.sum(xv * xv, axis=-1, keepdims=True) * inv_d + EPS)
            o_ref[rows, :] = _rope_fwd(xv * r * gain_v, c_ref[rows, :], s1_ref[rows, :],
                                       s2_ref[rows, :]).astype(BF16)
            return carry

        _by_chunks(tr, LANES, step)

    blk = pl.BlockSpec((tr, LANES), lambda i, h: (i, h))
    tab = pl.BlockSpec((tr, LANES), lambda i, h: (i, 0))
    return pl.pallas_call(
        body, name=name, grid=(T // tr, W // LANES),
        in_specs=[blk, pl.BlockSpec((tr, LANES), lambda i, h: (i, krb)),
                  pl.BlockSpec((1, LANES), lambda i, h: (0, 0)), tab, tab, tab],
        out_specs=blk, out_shape=jax.ShapeDtypeStruct((T, W), BF16),
        compiler_params=_params(("parallel", "parallel")),
    )(kv_lin, z, gain, *tabs)


def _mla_k_bwd(kv_lin, z, kr_col, gain, tabs, dk, dvk, name):
    T, W = kv_lin.shape
    tr = _row_tile(T, LANES)
    krb = kr_col // LANES
    inv_d = 1.0 / MLA_QK

    def body(kv_ref, kr_ref, g_ref, c_ref, s1_ref, s2_ref, dk_ref, dvk_ref, dkv_ref, dkr_ref, dg_ref):
        h = pl.program_id(1)
        gain_v = g_ref[...]

        @pl.when(h == 0)
        def _():
            dkr_ref[...] = jnp.zeros_like(dkr_ref)

        def step(rows, dg_sum):
            xv = _mla_k_raw(kv_ref[rows, :], kr_ref[rows, :])
            dyv = _rope_bwd(dk_ref[rows, :], c_ref[rows, :], s1_ref[rows, :], s2_ref[rows, :])
            dx, dg = _norm_bwd_math(xv, gain_v, dyv, LANES, inv_d)
            lane = lax.broadcasted_iota(jnp.int32, dx.shape, 1)
            dkv_ref[rows, :] = jnp.where(lane < MLA_NOPE, dx, dvk_ref[rows, :])
            dkr_ref[rows, :] += pltpu.roll(jnp.where((lane >= MLA_NOPE) & (lane < MLA_QK), dx, 0.0),
                                           LANES - MLA_NOPE, 1)
            return dg_sum + dg

        dg = _by_chunks(tr, LANES, step, jnp.zeros((1, LANES), F32))

        @pl.when((pl.program_id(0) == 0) & (h == 0))
        def _():
            dg_ref[...] = jnp.zeros_like(dg_ref)

        dg_ref[...] += dg

    blk = pl.BlockSpec((tr, LANES), lambda i, h: (i, h))
    tab = pl.BlockSpec((tr, LANES), lambda i, h: (i, 0))
    one = pl.BlockSpec((1, LANES), lambda i, h: (0, 0))
    return pl.pallas_call(
        body, name=name, grid=(T // tr, W // LANES),
        in_specs=[blk, pl.BlockSpec((tr, LANES), lambda i, h: (i, krb)), one, tab, tab, tab, blk, blk],
        out_specs=(blk, tab, one),
        out_shape=(jax.ShapeDtypeStruct((T, W), F32), jax.ShapeDtypeStruct((T, LANES), F32),
                   jax.ShapeDtypeStruct((1, LANES), F32)),
        compiler_params=_params(("arbitrary", "arbitrary")),
    )(kv_lin, z, gain, *tabs, dk, dvk)


def _ffn_up(u, w_gu, name, exch=None):
    T, D = u.shape
    F = w_gu.shape[1] // 2
    tm, tn = _tile(T, 640, 16), _tile(F, 1408, 128)
    nj = F // tn

    def body(u_ref, wg_ref, wu_ref, sa_ref, sb_ref, a_ref):
        uv = u_ref[...]
        g = jnp.dot(uv, wg_ref[...], preferred_element_type=F32)
        up = jnp.dot(uv, wu_ref[...], preferred_element_type=F32)
        sg = jax.nn.sigmoid(g)
        silu = g * sg
        sa_ref[...] = silu.astype(BF16)
        sb_ref[...] = (up * (sg + silu * (1.0 - sg))).astype(BF16)
        a_ref[...] = (silu * up).astype(BF16)

    o_spec = pl.BlockSpec((tm, tn), lambda i, j: (i, j))
    sh = jax.ShapeDtypeStruct((T, F), BF16)
    return _call(
        body, name, (T // tm, nj),
        [pl.BlockSpec((tm, D), lambda i, j: (i, 0)),
         pl.BlockSpec((D, tn), lambda i, j: (0, j)),
         pl.BlockSpec((D, tn), lambda i, j: (0, j + nj))],
        [o_spec, o_spec, o_spec], [sh, sh, sh], [u, w_gu, w_gu],
        sem=("parallel", "parallel"), exch=exch)


def _ffn_down_bwd(dh, w_down, sa, sb, name, exch=None):
    T, D = dh.shape
    F = w_down.shape[0]
    tm, tn = _tile(T, 640, 16), _tile(F, 1408, 128)

    def body(dh_ref, w_ref, sa_ref, sb_ref, dg_ref, dup_ref):
        da = 0.5 * lax.dot_general(dh_ref[...].astype(BF16), w_ref[...], (((1,), (1,)), ((), ())),
                                   preferred_element_type=F32)
        dup_ref[...] = (da * sa_ref[...].astype(F32)).astype(BF16)
        dg_ref[...] = (da * sb_ref[...].astype(F32)).astype(BF16)

    t_spec = pl.BlockSpec((tm, tn), lambda i, j: (i, j))
    sh = jax.ShapeDtypeStruct((T, F), BF16)
    return _call(
        body, name, (T // tm, F // tn),
        [pl.BlockSpec((tm, D), lambda i, j: (i, 0)),
         pl.BlockSpec((tn, D), lambda i, j: (j, 0)), t_spec, t_spec],
        [t_spec, t_spec], [sh, sh], [dh, w_down, sa, sb],
        sem=("parallel", "parallel"), exch=exch)


def _ffn_up_bwd_dx(dg, dup, w_gu, name, exch=None):
    T, F = dg.shape
    D = w_gu.shape[0]
    tm, tk = _tile(T, 640, 16), _tile(F, 1408, 128)
    nk = F // tk
    nt = (((1,), (1,)), ((), ()))

    def body(dg_ref, dup_ref, wg_ref, wu_ref, o_ref, acc_ref):
        k = pl.program_id(1)
        prod = (lax.dot_general(dg_ref[...], wg_ref[...], nt, preferred_element_type=F32)
                + lax.dot_general(dup_ref[...], wu_ref[...], nt, preferred_element_type=F32))
        if nk == 1:
            o_ref[...] = prod
            return

        @pl.when(k == 0)
        def _():
            acc_ref[...] = prod

        @pl.when((k > 0) & (k < nk - 1))
        def _():
            acc_ref[...] += prod

        @pl.when(k == nk - 1)
        def _():
            o_ref[...] = acc_ref[...] + prod

    return _call(
        body, name, (T // tm, nk),
        [pl.BlockSpec((tm, tk), lambda i, k: (i, k)),
         pl.BlockSpec((tm, tk), lambda i, k: (i, k)),
         pl.BlockSpec((D, tk), lambda i, k: (0, k)),
         pl.BlockSpec((D, tk), lambda i, k: (0, k + nk))],
        [pl.BlockSpec((tm, D), lambda i, k: (i, 0))], [jax.ShapeDtypeStruct((T, D), F32)],
        [dg, dup, w_gu, w_gu], scratch_shapes=[pltpu.VMEM((tm, D), F32)],
        sem=("parallel", "arbitrary"), exch=exch)


def _logsig(x):
    return jnp.minimum(x, 0.0) - jnp.log(1.0 + jnp.exp(-jnp.abs(x)))


def _cum_fwd(fl, flm, bf, name):
    B, S, _ = fl.shape
    nb = S // LANES

    def body(fl_ref, flm_ref, bf_ref, cum_ref, cumm_ref):
        rows = lax.broadcasted_iota(jnp.int32, (LANES, LANES), 0)
        cols = lax.broadcasted_iota(jnp.int32, (LANES, LANES), 1)
        tri = (rows >= cols).astype(F32)
        bias = bf_ref[...]
        lfm = jnp.where(rows < N_META, _logsig(flm_ref[...] + bias), 0.0)
        cm = jnp.dot(tri, lfm, precision=lax.Precision.HIGHEST, preferred_element_type=F32)
        cumm_ref[...] = cm * LOG2E
        base = cm[LANES - 1:LANES, :]
        for b in range(B):
            def blk(i, carry):
                r0 = pl.multiple_of(i * LANES, LANES)
                lf = _logsig(fl_ref[b, pl.ds(r0, LANES), :] + bias)
                c = jnp.dot(tri, lf, precision=lax.Precision.HIGHEST,
                            preferred_element_type=F32) + carry
                cum_ref[b, pl.ds(r0, LANES), :] = c * LOG2E
                return c[LANES - 1:LANES, :]

            lax.fori_loop(0, nb, blk, base)

    return pl.pallas_call(
        body, name=name,
        out_shape=(jax.ShapeDtypeStruct((B, S, LANES), F32),
                   jax.ShapeDtypeStruct((LANES, LANES), F32)),
        compiler_params=_params(),
    )(fl, flm, bf)


def _cum_bwd(dc, dcm, fl, flm, bf, name):
    B, S, _ = fl.shape
    nb = S // LANES

    def body(dc_ref, dcm_ref, fl_ref, flm_ref, bf_ref, dfl_ref, dflm_ref, dbf_ref):
        rows = lax.broadcasted_iota(jnp.int32, (LANES, LANES), 0)
        cols = lax.broadcasted_iota(jnp.int32, (LANES, LANES), 1)
        triu = (rows <= cols).astype(F32)
        bias = bf_ref[...]
        total = jnp.zeros((1, LANES), F32)
        dbf = jnp.zeros((1, LANES), F32)
        for b in range(B):
            tail = jnp.zeros((1, LANES), F32)
            for t in range(nb):
                r0 = (nb - 1 - t) * LANES
                rc = jnp.dot(triu, dc_ref[b, r0:r0 + LANES, :], precision=lax.Precision.HIGHEST,
                             preferred_element_type=F32) + tail
                xv = fl_ref[b, r0:r0 + LANES, :] + bias
                d = rc / (1.0 + jnp.exp(xv))
                dfl_ref[b, r0:r0 + LANES, :] = d
                tail = rc[0:1, :]
                dbf = dbf + jnp.sum(d, axis=0, keepdims=True)
            total = total + tail
        rcm = jnp.dot(triu, dcm_ref[...], precision=lax.Precision.HIGHEST,
                      preferred_element_type=F32) + total
        dm = jnp.where(rows < N_META, rcm / (1.0 + jnp.exp(flm_ref[...] + bias)), 0.0)
        dflm_ref[...] = dm
        dbf_ref[...] = dbf + jnp.sum(dm, axis=0, keepdims=True)

    return pl.pallas_call(
        body, name=name,
        out_shape=(jax.ShapeDtypeStruct((B, S, LANES), F32),
                   jax.ShapeDtypeStruct((LANES, LANES), F32),
                   jax.ShapeDtypeStruct((1, LANES), F32)),
        compiler_params=_params(),
    )(dc, dcm, fl, flm, bf)


_NT = (((1,), (1,)), ((), ()))


def _attn_specs(S, NX, qw, v_col0):
    mb = NX // META_BLK
    vb = v_col0 // qw
    return (pl.BlockSpec((S, qw), lambda b, p: (b, p)),
            pl.BlockSpec((META_BLK, qw), lambda b, p: (mb, p)),
            pl.BlockSpec((S, qw), lambda b, p: (b, vb + p)),
            pl.BlockSpec((META_BLK, qw), lambda b, p: (mb, vb + p)),
            pl.BlockSpec((S, LANES), lambda b, p: (b, p)))


def _cum_specs(S, TK):
    return [pl.BlockSpec((1, 2, S // TK, 1, TK), lambda b, p: (b, p, 0, 0, 0)),
            pl.BlockSpec((1, 2, 1, META_BLK), lambda b, p: (b, p, 0, 0))]


LOG2E = 1.4426950408889634


def _attn_fwd(qn, kn, vsrc, v_col0, mla, scale, S, NX, name, ck=None, cmk=None, exch=None):
    T = qn.shape[0]
    B = NX // S
    qw = 2 * LANES if mla else LANES
    npair = qn.shape[1] // qw
    TQ = min(512, S)
    TK = TQ
    forget = ck is not None
    a = scale * LOG2E

    def body(*refs):
        if forget:
            q_ref, k_ref, km_ref, v_ref, vm_ref, ck_ref, cmk_ref, _, o_ref, lse_ref = refs
        else:
            q_ref, k_ref, km_ref, v_ref, vm_ref, _, o_ref, lse_ref = refs
        lo = lax.broadcasted_iota(jnp.int32, (1, LANES), 1) < HALF
        mcol = lax.broadcasted_iota(jnp.int32, (TQ, META_BLK), 1)
        causal = (lax.broadcasted_iota(jnp.int32, (TQ, TK), 0)
                  >= lax.broadcasted_iota(jnp.int32, (TQ, TK), 1))
        two = lax.broadcasted_iota(jnp.int32, (TQ, 2), 1)
        for qi in range(S // TQ):
            q0 = qi * TQ
            sls = [slice(e * LANES, (e + 1) * LANES) if mla else slice(None) for e in range(2)]
            if mla:
                qts = [q_ref[q0:q0 + TQ, sl] for sl in sls]
            else:
                qts = [jnp.where(lo if e == 0 else ~lo, q_ref[q0:q0 + TQ, :], 0.0).astype(BF16)
                       for e in range(2)]

            def step(e, kt, vt, c2, mask, carry):
                m, l, acc = carry
                s = lax.dot_general(qts[e], kt, _NT, preferred_element_type=F32) * a
                if forget:
                    s = s - c2
                if mask is not None:
                    s = jnp.where(mask, s, NEG)
                m2 = jnp.max(s, axis=1, keepdims=True)
                if m is not None:
                    m2 = jnp.maximum(m, m2)
                p = jnp.exp2(s - m2)
                l2 = jnp.sum(p, axis=1, keepdims=True)
                acc2 = jnp.dot(p.astype(BF16), vt.astype(BF16), preferred_element_type=F32)
                if m is not None:
                    alpha = jnp.exp2(m - m2)
                    l2, acc2 = alpha * l + l2, alpha * acc + acc2
                return m2, l2, acc2

            def both(rows, kj, mask, carry):
                return tuple(step(e, k_ref[rows, sls[e]], v_ref[rows, sls[e]],
                                  ck_ref[0, e, kj] if forget else None, mask, carry[e]) for e in range(2))

            def below(kj, carry):
                return both(pl.ds(pl.multiple_of(kj * TK, TK), TK), kj, None, carry)

            carry = tuple(step(e, km_ref[:, sls[e]], vm_ref[:, sls[e]], cmk_ref[0, e] if forget else None,
                               mcol < N_META, (None, None, None)) for e in range(2))
            if qi:
                carry = lax.fori_loop(0, qi, below, carry)
            carry = both(slice(q0, q0 + TK), qi, causal, carry)
            outs = [acc / l for _, l, acc in carry]
            lses = [m + jnp.log2(l) for m, l, _ in carry]
            first = pltpu.roll(outs[0], HALF, 1) if mla else outs[0]
            o_ref[q0:q0 + TQ, :] = jnp.where(lo, first, outs[1])
            lse_ref[0, 0, q0:q0 + TQ, :] = jnp.where(two == 0, lses[0], lses[1])

    qk, kmeta, vv, vmeta, pair = _attn_specs(S, NX, qw, v_col0)
    specs = [qk, qk, kmeta, vv, vmeta]
    ins = [qn, kn, kn, vsrc, vsrc]
    if forget:
        specs += _cum_specs(S, TK)
        ins += [ck, cmk]
    specs.append(pl.BlockSpec(memory_space=pl.ANY))
    ins.append(jnp.zeros((T, npair * LANES), F32))
    lse_spec = pl.BlockSpec((1, 1, S, 2), lambda b, p: (b, p, 0, 0))
    return _call(
        body, name, (B, npair), specs, [pair, lse_spec],
        [jax.ShapeDtypeStruct((T, npair * LANES), F32), jax.ShapeDtypeStruct((B, npair, S, 2), F32)],
        ins, sem=("parallel", "parallel"), aliases={len(ins) - 1: 0}, exch=exch)


def _attn_bwd(qn, kn, vsrc, v_col0, o, lse, do, mla, scale, S, NX, name, ck=None, cmk=None, exch=None):
    T, W = qn.shape
    B = NX // S
    qw = 2 * LANES if mla else LANES
    npair = W // qw
    TQ = min(512, S)
    TK = TQ
    forget = ck is not None
    a = scale * LOG2E
    _TN = (((0,), (0,)), ((), ()))

    def body(*refs):
        refs = list(refs)
        q_ref, k_ref, km_ref, v_ref, vm_ref, o_ref, do_ref, lse_ref = refs[:8]
        pos = 8
        if forget:
            ck_ref, cmk_ref = refs[8:10]
            pos = 10
        pos += 3
        dq_ref, dk_ref, dv_ref, dkm_ref, dvm_ref = refs[pos:pos + 5]
        if forget:
            dck_ref, dcm_ref, dcq_ref = refs[pos + 5:pos + 8]
            dck_ref[...] = jnp.zeros_like(dck_ref)
            dcm_ref[...] = jnp.zeros_like(dcm_ref)
        dk_ref[...] = jnp.zeros_like(dk_ref)
        dv_ref[...] = jnp.zeros_like(dv_ref)
        dkm_ref[...] = jnp.zeros_like(dkm_ref)
        dvm_ref[...] = jnp.zeros_like(dvm_ref)
        lo = lax.broadcasted_iota(jnp.int32, (1, LANES), 1) < HALF
        mcol = lax.broadcasted_iota(jnp.int32, (TQ, META_BLK), 1)
        causal = (lax.broadcasted_iota(jnp.int32, (TQ, TK), 0)
                  >= lax.broadcasted_iota(jnp.int32, (TQ, TK), 1))
        two = lax.broadcasted_iota(jnp.int32, (TQ, 2), 1)
        for qi in range(S // TQ):
            q0 = qi * TQ
            dof = do_ref[q0:q0 + TQ, :]
            prod = dof * o_ref[q0:q0 + TQ, :]
            lse2 = lse_ref[0, 0, q0:q0 + TQ, :]
            sls = [slice(e * LANES, (e + 1) * LANES) if mla else slice(None) for e in range(2)]
            mine = [lo, ~lo]
            if mla:
                qts = [q_ref[q0:q0 + TQ, sl] for sl in sls]
                dots = [jnp.where(lo, 0.0, pltpu.roll(dof, HALF, 1) if e == 0 else dof).astype(BF16)
                        for e in range(2)]
            else:
                qts = [jnp.where(mine[e], q_ref[q0:q0 + TQ, :], 0.0).astype(BF16) for e in range(2)]
                dots = [jnp.where(mine[e], dof, 0.0).astype(BF16) for e in range(2)]
            deltas = [jnp.sum(jnp.where(mine[e], prod, 0.0), axis=1, keepdims=True) for e in range(2)]
            lse_ts = [jnp.sum(jnp.where(two == e, lse2, 0.0), axis=1, keepdims=True) for e in range(2)]

            def grads(e, kt, vt, c2, mask):
                s = lax.dot_general(qts[e], kt, _NT, preferred_element_type=F32) * a
                if forget:
                    s = s - c2
                p = jnp.exp2(s - lse_ts[e])
                if mask is not None:
                    p = jnp.where(mask, p, 0.0)
                dp = lax.dot_general(dots[e], vt, _NT, preferred_element_type=F32)
                ds = p * (dp - deltas[e])
                dsb = ds.astype(BF16)
                return (jnp.dot(dsb, kt, preferred_element_type=F32),
                        lax.dot_general(dsb, qts[e], _TN, preferred_element_type=F32) * scale,
                        lax.dot_general(p.astype(BF16), dots[e], _TN, preferred_element_type=F32),
                        -jnp.sum(ds, axis=0, keepdims=True) if forget else None,
                        jnp.sum(ds, axis=1, keepdims=True) if forget else None)

            def block(k_at, v_at, dk_at, dv_at, c_at, dc_at, mask, dqs):
                got = [grads(e, k_at(sls[e]), v_at(sls[e]).astype(BF16), c_at(e) if forget else None, mask)
                       for e in range(2)]
                if mla:
                    for e in range(2):
                        dk_at(sls[e], got[e][1])
                        dv_at(sls[e], got[e][2])
                else:
                    dk_at(sls[0], got[0][1] + got[1][1])
                    dv_at(sls[0], got[0][2] + got[1][2])
                if forget:
                    for e in range(2):
                        dc_at(e, got[e][3])
                picks = (0, 0, 4, 4) if forget else (0, 0)
                new = tuple(got[i % 2][k] for i, k in enumerate(picks))
                return new if dqs is None else tuple(x + y for x, y in zip(dqs, new))

            def add_to(ref, *lead):
                def add(*idx_and_val):
                    *idx, val = idx_and_val
                    ref[(*lead, *idx)] += val
                return add

            def token_block(rows, kj, mask, dqs):
                return block(lambda sl: k_ref[rows, sl], lambda sl: v_ref[rows, sl],
                             lambda sl, val: add_to(dk_ref)(rows, sl, val),
                             lambda sl, val: add_to(dv_ref)(rows, sl, val),
                             lambda e: ck_ref[0, e, kj], lambda e, val: add_to(dck_ref, 0)(e, kj, val),
                             mask, dqs)

            dqs = block(lambda sl: km_ref[:, sl], lambda sl: vm_ref[:, sl],
                        lambda sl, val: add_to(dkm_ref, 0)(slice(None), sl, val),
                        lambda sl, val: add_to(dvm_ref, 0)(slice(None), sl, val),
                        lambda e: cmk_ref[0, e], lambda e, val: add_to(dcm_ref, 0)(e, val),
                        mcol < N_META, None)

            def below(kj, dqs):
                return token_block(pl.ds(pl.multiple_of(kj * TK, TK), TK), kj, None, dqs)

            if qi:
                dqs = lax.fori_loop(0, qi, below, dqs)
            dqs = token_block(slice(q0, q0 + TK), qi, causal, dqs)
            if forget:
                dcq_ref[0, 0, q0:q0 + TQ, :] = jnp.where(two == 0, dqs[2], dqs[3])
            if mla:
                for e in range(2):
                    dq_ref[q0:q0 + TQ, sls[e]] = dqs[e] * scale
            else:
                dq_ref[q0:q0 + TQ, :] = jnp.where(lo, dqs[0], dqs[1]) * scale

    qk, kmeta, vv, vmeta, pair = _attn_specs(S, NX, qw, v_col0)
    lse_spec = pl.BlockSpec((1, 1, S, 2), lambda b, p: (b, p, 0, 0))
    specs = [qk, qk, kmeta, vv, vmeta, pair, pair, lse_spec]
    ins = [qn, kn, kn, vsrc, vsrc, o, do, lse]
    if forget:
        specs += _cum_specs(S, TK)
        ins += [ck, cmk]
    first_alias = len(ins)
    specs += [pl.BlockSpec(memory_space=pl.ANY)] * 3
    ins += [jnp.zeros((T, W), F32)] * 3
    mspec = pl.BlockSpec((1, META_BLK, qw), lambda b, p: (b, 0, p))
    out_specs = [qk, qk, qk, mspec, mspec]
    tok = jax.ShapeDtypeStruct((T, W), F32)
    met = jax.ShapeDtypeStruct((B, META_BLK, W), F32)
    out_shape = [tok, tok, tok, met, met]
    if forget:
        out_specs += _cum_specs(S, TK) + [lse_spec]
        out_shape += [jax.ShapeDtypeStruct((B, HEADS, S // TK, 1, TK), F32),
                      jax.ShapeDtypeStruct((B, HEADS, 1, META_BLK), F32),
                      jax.ShapeDtypeStruct((B, npair, S, 2), F32)]
    return _call(
        body, name, (B, npair), specs, out_specs, out_shape, ins, sem=("parallel", "parallel"),
        aliases={first_alias: 0, first_alias + 1: 1, first_alias + 2: 2}, exch=exch)


def _gate_fwd(z, bg, of, om, name):
    T, D = of.shape
    tm = _tile(T, 640, 16)

    def body(z_ref, bg_ref, of_ref, om_ref, o_ref):
        bias = bg_ref[...]

        def step(rows, carry):
            gt = jax.nn.sigmoid(z_ref[rows, :] + bias)
            o_ref[rows, :] = (gt[:, :D] * of_ref[rows, :] + gt[:, D:] * om_ref[rows, :]).astype(BF16)
            return carry

        _by_chunks(tm, D, step)

    row = pl.BlockSpec((tm, D), lambda i: (i, 0))
    return pl.pallas_call(
        body, name=name, grid=(T // tm,),
        in_specs=[pl.BlockSpec((tm, 2 * D), lambda i: (i, 0)),
                  pl.BlockSpec((1, 2 * D), lambda i: (0, 0)), row, row],
        out_specs=row, out_shape=jax.ShapeDtypeStruct((T, D), BF16),
        compiler_params=_params(("parallel",)),
    )(z, bg, of, om)


def _gate_bwd(dmix, z, bg, of, om, name):
    T, D = of.shape
    tm = _tile(T, 640, 16)

    def body(dm_ref, z_ref, bg_ref, of_ref, om_ref, dgl_ref, dof_ref, dom_ref, dbg_ref):
        bias = bg_ref[...]

        def step(rows, dbg_sum):
            gt = jax.nn.sigmoid(z_ref[rows, :] + bias)
            dm = dm_ref[rows, :]
            dof_ref[rows, :] = (dm * gt[:, :D]).astype(BF16)
            dom_ref[rows, :] = (dm * gt[:, D:]).astype(BF16)
            dgl = jnp.concatenate([dm * of_ref[rows, :], dm * om_ref[rows, :]], axis=1) * gt * (1.0 - gt)
            dgl_ref[rows, :] = dgl.astype(BF16)
            return dbg_sum + jnp.sum(dgl, axis=0, keepdims=True)

        dbg = _by_chunks(tm, D, step, jnp.zeros((1, 2 * D), F32))

        @pl.when(pl.program_id(0) == 0)
        def _():
            dbg_ref[...] = jnp.zeros_like(dbg_ref)

        dbg_ref[...] += dbg

    row = pl.BlockSpec((tm, D), lambda i: (i, 0))
    wide = pl.BlockSpec((tm, 2 * D), lambda i: (i, 0))
    one = pl.BlockSpec((1, 2 * D), lambda i: (0, 0))
    return pl.pallas_call(
        body, name=name, grid=(T // tm,),
        in_specs=[row, wide, one, row, row], out_specs=(wide, row, row, one),
        out_shape=(jax.ShapeDtypeStruct((T, 2 * D), BF16), jax.ShapeDtypeStruct((T, D), BF16),
                   jax.ShapeDtypeStruct((T, D), BF16), jax.ShapeDtypeStruct((1, 2 * D), F32)),
        compiler_params=_params(("arbitrary",)),
    )(dmix, z, bg, of, om)


def _loss(h, tgt, n_valid, name):
    T, D = h.shape
    tm = _tile(T, 640, 16)

    def body(h_ref, t_ref, dh_ref, l_ref):
        i = pl.program_id(0)
        row_in_block = lax.broadcasted_iota(jnp.int32, (tm, D), 0)

        def step(rows, part):
            err = jnp.where(row_in_block + i * tm < n_valid, h_ref[rows, :] - t_ref[rows, :], 0.0)
            dh_ref[rows, :] = err * (1.0 / D)
            return part + jnp.sum(err * err, axis=0, keepdims=True)

        part = _by_chunks(tm, D, step, jnp.zeros((1, D), F32))

        @pl.when(i == 0)
        def _():
            l_ref[...] = jnp.zeros_like(l_ref)

        l_ref[...] += 0.5 * jnp.sum(part) * (1.0 / D)

    row = pl.BlockSpec((tm, D), lambda i: (i, 0))
    acc = pl.BlockSpec((8, LANES), lambda i: (0, 0))
    return pl.pallas_call(
        body, name=name, grid=(T // tm,), in_specs=[row, row], out_specs=(row, acc),
        out_shape=(jax.ShapeDtypeStruct((T, D), F32), jax.ShapeDtypeStruct((8, LANES), F32)),
        compiler_params=_params(("arbitrary",)),
    )(h, tgt)


def _adamw(parts, w, m, v, name):
    P, R, C = parts.shape
    tr = _tile(R, max(8, (1 << 18) // C), 8)
    bc1 = 1.0 - ADAM_B1 ** ADAM_STEP
    bc2 = 1.0 - ADAM_B2 ** ADAM_STEP

    def body(p_ref, w_ref, m_ref, v_ref, g_ref, d_ref, m2_ref, v2_ref):
        def step(rows, carry):
            g = p_ref[0, rows, :].astype(F32)
            for j in range(1, P):
                g = g + p_ref[j, rows, :].astype(F32)
            m2 = ADAM_B1 * m_ref[rows, :] + (1.0 - ADAM_B1) * g
            v2 = ADAM_B2 * v_ref[rows, :] + (1.0 - ADAM_B2) * (g * g)
            m_hat = m2 / bc1
            v_hat = v2 / bc2
            g_ref[rows, :] = g
            d_ref[rows, :] = -ADAM_LR * (m_hat / (jnp.sqrt(v_hat) + ADAM_EPS) + ADAM_WD * w_ref[rows, :])
            m2_ref[rows, :] = m2
            v2_ref[rows, :] = v2
            return carry

        _by_chunks(tr, C, step)

    row = pl.BlockSpec((tr, C), lambda i: (i, 0))
    sh = jax.ShapeDtypeStruct((R, C), F32)
    return pl.pallas_call(
        body, name=name, grid=(R // tr,),
        in_specs=[pl.BlockSpec((P, tr, C), lambda i: (0, i, 0)), row, row, row],
        out_specs=(row, row, row, row), out_shape=(sh, sh, sh, sh),
        compiler_params=_params(("parallel",)),
    )(parts, w, m, v)


def _peer(d):
    x, y, c = lax.axis_index("x"), lax.axis_index("y"), lax.axis_index("c")
    px = 1 - x if d & 4 else x
    py = 1 - y if d & 2 else y
    pc = 1 - c if d & 1 else c
    return (px, py, pc), 4 * px + 2 * py + pc


class _Exchange:
    def __init__(self, srcs, gather):
        self.srcs, self.gather, self.n = list(srcs), gather, len(srcs)
        n = self.n
        hbm = pl.BlockSpec(memory_space=pl.ANY)
        self.in_specs = [hbm] * n
        self.out_specs = [hbm] * n
        self.out_shape = [jax.ShapeDtypeStruct((N_DEV,) + s.shape[-2:], s.dtype) for s in srcs]
        self.scratch = [pltpu.SemaphoreType.DMA((N_DEV - 1, n)), pltpu.SemaphoreType.DMA((N_DEV - 1, n)),
                        pltpu.SemaphoreType.DMA((n,))]

    def _copies(self, src_refs, out_refs, sems):
        send_sems, recv_sems, local_sems = sems
        _, me = _peer(0)

        def remote(w, d, landing):
            dev, lin = _peer(d)
            return pltpu.make_async_remote_copy(
                src_ref=src_refs[w] if self.gather else src_refs[w].at[lin],
                dst_ref=out_refs[w].at[lin if landing else me],
                send_sem=send_sems.at[d - 1, w], recv_sem=recv_sems.at[d - 1, w],
                device_id=dev, device_id_type=pl.DeviceIdType.MESH)

        pairs = [(w, d) for d in range(1, N_DEV) for w in range(self.n)]
        own = [pltpu.make_async_copy(src_refs[w] if self.gather else src_refs[w].at[me],
                                     out_refs[w].at[me], local_sems.at[w]) for w in range(self.n)]
        return own, [remote(w, d, False) for w, d in pairs], [remote(w, d, True) for w, d in pairs]

    def _gather_copies(self, src_refs, out_refs, sems):
        send_sems, recv_sems, local_sems = sems
        x, y, c = lax.axis_index("x"), lax.axis_index("y"), lax.axis_index("c")
        me, sibling = (x, y, c), (x, y, 1 - c)
        chips = [(1 - x, y), (x, 1 - y), (1 - x, 1 - y)]

        def copy(w, k, block, to, src=None):
            rows = out_refs[w].at[4 * block[0] + 2 * block[1] + block[2]]
            return pltpu.make_async_remote_copy(
                src_ref=rows if src is None else src, dst_ref=rows,
                send_sem=send_sems.at[k, w], recv_sem=recv_sems.at[k, w],
                device_id=to, device_id_type=pl.DeviceIdType.MESH)

        ws = range(self.n)
        own = [pltpu.make_async_copy(src_refs[w], out_refs[w].at[4 * x + 2 * y + c], local_sems.at[w])
               for w in ws]
        first = [copy(w, 0, me, sibling, src_refs[w]) for w in ws]
        first += [copy(w, 1 + j, me, (*chip, c), src_refs[w]) for j, chip in enumerate(chips) for w in ws]
        landed = [[copy(w, 1 + j, (*chip, c), me) for w in ws] for j, chip in enumerate(chips)]
        passed = [[copy(w, 4 + j, (*chip, c), sibling) for w in ws] for j, chip in enumerate(chips)]
        from_sibling = [copy(w, 0, sibling, me) for w in ws]
        from_sibling += [copy(w, 4 + j, (*chip, 1 - c), me) for j, chip in enumerate(chips) for w in ws]
        return own, first, landed, passed, from_sibling

    def start(self, src_refs, out_refs, sems):
        if self.gather:
            own, first = self._gather_copies(src_refs, out_refs, sems)[:2]
            sent = first
        else:
            own, sent, _ = self._copies(src_refs, out_refs, sems)
        for cp in own + sent:
            cp.start()

    def wait(self, src_refs, out_refs, sems):
        if self.gather:
            own, first, landed, passed, from_sibling = self._gather_copies(src_refs, out_refs, sems)
            for arrived, onward in zip(landed, passed):
                for cp in arrived:
                    cp.wait_recv()
                for cp in onward:
                    cp.start()
            for cp in from_sibling:
                cp.wait_recv()
            for cp in first + [cp for group in passed for cp in group]:
                cp.wait_send()
        else:
            own, sent, landing = self._copies(src_refs, out_refs, sems)
            for cp in landing:
                cp.wait_recv()
            for cp in sent:
                cp.wait_send()
        for cp in own:
            cp.wait()


def _exchange(srcs, name, gather):
    ex = _Exchange(srcs, gather)
    n = ex.n

    def body(*refs):
        ex.start(refs[:n], refs[n:2 * n], refs[2 * n:])
        ex.wait(refs[:n], refs[n:2 * n], refs[2 * n:])

    outs = pl.pallas_call(
        body, name=name, in_specs=ex.in_specs, out_specs=tuple(ex.out_specs),
        out_shape=tuple(ex.out_shape), scratch_shapes=ex.scratch,
    )(*srcs)
    return list(outs)


def _call(body, name, grid, in_specs, out_specs, out_shape, ins, scratch_shapes=(), sem=None,
          aliases=None, exch=None):
    aliases = aliases or {}
    if exch is None:
        outs = pl.pallas_call(
            body, name=name, grid=grid, in_specs=list(in_specs), out_specs=tuple(out_specs),
            out_shape=tuple(out_shape), scratch_shapes=list(scratch_shapes),
            input_output_aliases=aliases, compiler_params=_params(sem),
        )(*ins)
        return list(outs), []
    ni, no, ns, n = len(in_specs), len(out_specs), len(scratch_shapes), exch.n
    last_ids = [g - 1 for g in grid]

    def hosted(*refs):
        cin, xin = refs[:ni], refs[ni:ni + n]
        cout, xout = refs[ni + n:ni + n + no], refs[ni + n + no:ni + 2 * n + no]
        cscr, xsem = refs[ni + 2 * n + no:ni + 2 * n + no + ns], refs[ni + 2 * n + no + ns:]
        ids = [pl.program_id(a) for a in range(len(grid))]
        first, last = ids[0] == 0, ids[0] == last_ids[0]
        for a in range(1, len(grid)):
            first, last = first & (ids[a] == 0), last & (ids[a] == last_ids[a])

        @pl.when(first)
        def _():
            exch.start(xin, xout, xsem)

        body(*cin, *cout, *cscr)

        @pl.when(last)
        def _():
            exch.wait(xin, xout, xsem)

    outs = pl.pallas_call(
        hosted, name=name, grid=grid, in_specs=list(in_specs) + exch.in_specs,
        out_specs=tuple(list(out_specs) + exch.out_specs),
        out_shape=tuple(list(out_shape) + exch.out_shape),
        scratch_shapes=list(scratch_shapes) + exch.scratch, input_output_aliases=aliases,
        compiler_params=_params(("arbitrary",) * len(grid)),
    )(*ins, *exch.srcs)
    return list(outs[:no]), list(outs[no:])


def _pack(arrs, cols, row_mult):
    flat = jnp.concatenate([a.reshape(-1) for a in arrs])
    n = flat.shape[0]
    quantum = cols * row_mult
    total = -(-n // quantum) * quantum
    return jnp.pad(flat, (0, total - n)).reshape(total // cols, cols)


def _pack_rows(arrs, cols, row_mult):
    flat = jnp.concatenate(arrs, axis=1)
    n = flat.shape[1]
    quantum = cols * row_mult
    total = -(-n // quantum) * quantum
    return jnp.pad(flat, ((0, 0), (0, total - n))).reshape(N_DEV, total // cols, cols)


def _unpack(packed, shapes):
    flat = packed.reshape(-1)
    out, off = [], 0
    for s in shapes:
        n = int(np.prod(s))
        out.append(flat[off:off + n].reshape(s))
        off += n
    return out


def _rope_tables(positions):
    inv_freq = ROPE_THETA ** (-jnp.arange(0, MLA_ROPE, 2, dtype=F32) / MLA_ROPE)
    ang = positions.astype(F32)[:, None] * inv_freq[None, :]
    cos, sin = jnp.cos(ang), jnp.sin(ang)
    n = positions.shape[0]
    ones, zeros = jnp.ones((n, MLA_NOPE), F32), jnp.zeros((n, MLA_NOPE), F32)
    tail1, tail0 = jnp.ones((n, LANES - MLA_QK), F32), jnp.zeros((n, LANES - MLA_QK), F32)
    z16 = jnp.zeros((n, 16), F32)
    c = jnp.concatenate([ones, cos, cos, tail1], axis=1)
    s1 = jnp.concatenate([zeros, -sin, z16, tail0], axis=1)
    s2 = jnp.concatenate([zeros, z16, sin, tail0], axis=1)
    return c, s1, s2


def kernel(x, meta_tokens, ffn1_norm, ffn1_w_gu, ffn1_w_down, mix_norm, w_in, b_forget, b_gate, fox_q_norm, fox_k_norm, mla_cq_norm, mla_w_uq, mla_ckv_norm, mla_w_ukv, mla_q_norm, mla_k_norm, w_branch_fox, w_branch_mla, w_out, ffn2_norm, ffn2_w_gu, ffn2_w_down, loss_target, m_meta_tokens, m_ffn1_norm, m_ffn1_w_gu, m_ffn1_w_down, m_mix_norm, m_w_in, m_b_forget, m_b_gate, m_fox_q_norm, m_fox_k_norm, m_mla_cq_norm, m_mla_w_uq, m_mla_ckv_norm, m_mla_w_ukv, m_mla_q_norm, m_mla_k_norm, m_w_branch_fox, m_w_branch_mla, m_w_out, m_ffn2_norm, m_ffn2_w_gu, m_ffn2_w_down, v_meta_tokens, v_ffn1_norm, v_ffn1_w_gu, v_ffn1_w_down, v_mix_norm, v_w_in, v_b_forget, v_b_gate, v_fox_q_norm, v_fox_k_norm, v_mla_cq_norm, v_mla_w_uq, v_mla_ckv_norm, v_mla_w_ukv, v_mla_q_norm, v_mla_k_norm, v_w_branch_fox, v_w_branch_mla, v_w_out, v_ffn2_norm, v_ffn2_w_gu, v_ffn2_w_down):
    names = ["meta_tokens", "ffn1_norm", "ffn1_w_gu", "ffn1_w_down", "mix_norm", "w_in", "b_forget",
             "b_gate", "fox_q_norm", "fox_k_norm", "mla_cq_norm", "mla_w_uq", "mla_ckv_norm",
             "mla_w_ukv", "mla_q_norm", "mla_k_norm", "w_branch_fox", "w_branch_mla", "w_out",
             "ffn2_norm", "ffn2_w_gu", "ffn2_w_down"]
    W = dict(zip(names, [meta_tokens, ffn1_norm, ffn1_w_gu, ffn1_w_down, mix_norm, w_in, b_forget,
                         b_gate, fox_q_norm, fox_k_norm, mla_cq_norm, mla_w_uq, mla_ckv_norm,
                         mla_w_ukv, mla_q_norm, mla_k_norm, w_branch_fox, w_branch_mla, w_out,
                         ffn2_norm, ffn2_w_gu, ffn2_w_down]))
    Mo = dict(zip(names, [m_meta_tokens, m_ffn1_norm, m_ffn1_w_gu, m_ffn1_w_down, m_mix_norm, m_w_in,
                          m_b_forget, m_b_gate, m_fox_q_norm, m_fox_k_norm, m_mla_cq_norm,
                          m_mla_w_uq, m_mla_ckv_norm, m_mla_w_ukv, m_mla_q_norm, m_mla_k_norm,
                          m_w_branch_fox, m_w_branch_mla, m_w_out, m_ffn2_norm, m_ffn2_w_gu,
                          m_ffn2_w_down]))
    Vo = dict(zip(names, [v_meta_tokens, v_ffn1_norm, v_ffn1_w_gu, v_ffn1_w_down, v_mix_norm, v_w_in,
                          v_b_forget, v_b_gate, v_fox_q_norm, v_fox_k_norm, v_mla_cq_norm,
                          v_mla_w_uq, v_mla_ckv_norm, v_mla_w_ukv, v_mla_q_norm, v_mla_k_norm,
                          v_w_branch_fox, v_w_branch_mla, v_w_out, v_ffn2_norm, v_ffn2_w_gu,
                          v_ffn2_w_down]))

    B, S, D = x.shape
    NX = B * S
    T = NX + META_BLK
    H = HEADS
    assert NX % META_BLK == 0 and S % LANES == 0
    me = 4 * lax.axis_index("x") + 2 * lax.axis_index("y") + lax.axis_index("c")

    big = [("ffn1_w_gu", 1), ("ffn1_w_down", 0), ("w_in", 1), ("mla_w_uq", 1), ("mla_w_ukv", 1),
           ("w_branch_fox", 1), ("w_branch_mla", 1), ("w_out", 0), ("ffn2_w_gu", 1), ("ffn2_w_down", 0)]
    mix_small = ["mla_w_uq", "mla_w_ukv", "w_branch_fox", "w_branch_mla", "w_out"]
    last_group = ["ffn2_w_gu", "ffn2_w_down"]
    axis_of = dict(big)
    full = {}

    def shards(group):
        return [W[n][0].astype(BF16) for n in group]

    def assemble(group, blks):
        for n, blk in zip(group, blks):
            _, r, c = blk.shape
            full[n] = (blk.transpose(1, 0, 2).reshape(r, N_DEV * c) if axis_of[n] == 1
                       else blk.reshape(N_DEV * r, c))

    got = _exchange(shards(["ffn1_w_gu"]) + [meta_tokens], "gather_first", gather=True)
    assemble(["ffn1_w_gu"], got[:1])
    meta_full = got[1].transpose(1, 0, 2).reshape(N_META, D)

    Z_G, Z_FQ = 0, 2 * D
    Z_FK, Z_FV = Z_FQ + FOX_W, Z_FQ + 2 * FOX_W
    Z_CQ = Z_FQ + 3 * FOX_W
    Z_CKV = Z_CQ + MLA_Q_RANK
    Z_F = Z_CKV + MLA_KV_RANK
    Z_KR = Z_F + LANES

    def pad_lanes(a, w=LANES):
        return jnp.pad(a, [(0, 0)] * (a.ndim - 1) + [(0, w - a.shape[-1])])

    def rows_T(real, meta=None):
        n = real.shape[1]
        parts = [real]
        used = 0
        if meta is not None:
            parts.append(meta)
            used = meta.shape[0]
        if T - NX - used:
            parts.append(jnp.zeros((T - NX - used, n), real.dtype))
        return jnp.concatenate(parts, axis=0)

    def put_meta(tok, meta_per_seq):
        return lax.dynamic_update_slice(tok, meta_per_seq.sum(0), (NX, 0))

    h0 = rows_T(x.reshape(NX, D), meta_full)
    tgt = rows_T(loss_target.reshape(NX, D))

    def ffn_fwd(h, norm, w_gu, tag, behind_up=None, behind_down=None):
        u = _norm_fwd(h, 0, D, D, norm, D, D, tag + "_norm")
        (sa, sb, a), got = _ffn_up(u, w_gu, tag + "_up",
                                  exch=_Exchange(shards(behind_up), True) if behind_up else None)
        assemble(behind_up or [], got)
        h_out = _mm(a, full[tag + "_w_down"], "nn", tag + "_down", scale=0.5, res=h,
                    exch=_Exchange(shards(behind_down), True) if behind_down else None)
        if behind_down:
            h_out, got = h_out
            assemble(behind_down, got)
        return h_out, (u, sa, sb, a)

    h1, ffn1_saved = ffn_fwd(h0, W["ffn1_norm"], full["ffn1_w_gu"], "ffn1",
                             behind_up=["ffn1_w_down"], behind_down=["w_in"])
    wi = full["w_in"]
    o_fq = 0
    o_f = 3 * FOX_W
    o_cq = o_f + HEADS
    o_kr = o_cq + MLA_Q_RANK + MLA_KV_RANK
    o_g = o_kr + MLA_ROPE
    w_in_p = jnp.concatenate([
        wi[:, o_g:o_g + 2 * D], wi[:, o_fq:o_f], wi[:, o_cq:o_kr],
        jnp.pad(wi[:, o_f:o_cq], ((0, 0), (0, LANES - HEADS))),
        jnp.pad(wi[:, o_kr:o_g], ((0, 0), (0, LANES - MLA_ROPE)))], axis=1)

    u2 = _norm_fwd(h1, 0, D, D, W["mix_norm"], D, D, "mix_norm")
    z, got = _mm(u2, w_in_p, "nn", "w_in", exch=_Exchange(shards(mix_small), True))
    assemble(mix_small, got)
    w_uq_p = jnp.pad(full["mla_w_uq"].reshape(MLA_Q_RANK, H, MLA_QK),
                     ((0, 0), (0, 0), (0, LANES - MLA_QK))).reshape(MLA_Q_RANK, H * LANES)

    gq_f = jnp.tile(W["fox_q_norm"], (1, 2))
    gk_f = jnp.tile(W["fox_k_norm"], (1, 2))
    fqn = _norm_fwd(z, Z_FQ, FOX_W, LANES, gq_f, FOX_HD, FOX_HD, "fox_q_norm")
    fkn = _norm_fwd(z, Z_FK, FOX_W, LANES, gk_f, FOX_HD, FOX_HD, "fox_k_norm")
    fl = z[:NX, Z_F:Z_F + LANES].reshape(B, S, LANES)
    flm = z[NX:, Z_F:Z_F + LANES]
    bf = pad_lanes(W["b_forget"])
    cum, cumm = _cum_fwd(fl, flm, bf, "forget_cum")
    TK = min(512, S)
    ck = cum[:, :, :H].transpose(0, 2, 1).reshape(B, H, S // TK, 1, TK)
    cmk = jnp.broadcast_to(cumm[:, :H].T[None, :, None, :], (B, H, 1, META_BLK))
    (o_fox, lse_fox), got = _attn_fwd(fqn, fkn, z, Z_FV, False, FOX_HD ** -0.5, S, NX, "fox_attn", ck, cmk,
                                      exch=_Exchange(shards(last_group), True))
    assemble(last_group, got)
    of = _mm(o_fox, full["w_branch_fox"], "nn", "branch_fox")

    pos = jnp.concatenate([jnp.tile(jnp.arange(S) + N_META, B), jnp.arange(META_BLK)])
    tabs = _rope_tables(pos)
    cqn = _norm_fwd(z, Z_CQ, MLA_Q_RANK, MLA_Q_RANK, W["mla_cq_norm"], MLA_Q_RANK, MLA_Q_RANK, "mla_cq_norm")
    q_lin = _mm(cqn, w_uq_p, "nn", "mla_uq")
    ckvn = _norm_fwd(z, Z_CKV, MLA_KV_RANK, MLA_KV_RANK, W["mla_ckv_norm"], MLA_KV_RANK, MLA_KV_RANK,
                     "mla_ckv_norm")
    kv_lin = _mm(ckvn, full["mla_w_ukv"], "nn", "mla_ukv")
    gq_m, gk_m = pad_lanes(W["mla_q_norm"]), pad_lanes(W["mla_k_norm"])
    mqn = _norm_fwd(q_lin, 0, H * LANES, LANES, gq_m, LANES, MLA_QK, "mla_q_norm", tabs=tabs)
    mkn = _mla_k_fwd(kv_lin, z, Z_KR, gk_m, tabs, "mla_k_norm")
    (o_mla, lse_mla), _ = _attn_fwd(mqn, mkn, kv_lin, 0, True, MLA_QK ** -0.5, S, NX, "mla_attn")
    om = _mm(o_mla, full["w_branch_mla"], "nn", "branch_mla")

    mix = _gate_fwd(z, W["b_gate"], of, om, "gate_mix")
    h2 = _mm(mix, full["w_out"], "nn", "w_out", res=h1)

    h3, ffn2_saved = ffn_fwd(h2, W["ffn2_norm"], full["ffn2_w_gu"], "ffn2")

    dh3, loss_acc = _loss(h3, tgt, NX, "loss")
    loss = lax.psum(loss_acc[0, 0], AXES)

    G = {}
    parts = {}

    def scatter_of(group):
        per_dest = []
        for n in group:
            r, c = W[n].shape[1:]
            per_dest.append((G[n].reshape(r, N_DEV, c).transpose(1, 0, 2) if axis_of[n] == 1
                             else G[n].reshape(N_DEV, r, c)).astype(BF16))
        return _Exchange(per_dest, False)

    def ffn_bwd(dh, h, norm, w_gu, w_down, saved, tag, behind_down=None, spread=False):
        u, sa, sb, a = saved
        G[tag + "_w_down"] = _mm(a, dh, "tn", tag + "_dw_down", scale=0.5)
        (dg, dup), got = _ffn_down_bwd(dh, w_down, sa, sb, tag + "_down_bwd",
                                       exch=scatter_of(behind_down) if behind_down else None)
        parts.update(zip(behind_down or [], got))
        dw_g = _mm(u, dg, "tn", tag + "_dw_g", exch=scatter_of([tag + "_w_down"]) if spread else None)
        if spread:
            dw_g, got = dw_g
            parts[tag + "_w_down"] = got[0]
        G[tag + "_w_gu"] = jnp.concatenate([dw_g, _mm(u, dup, "tn", tag + "_dw_u")], axis=1)
        (du,), got = _ffn_up_bwd_dx(dg, dup, w_gu, tag + "_up_bwd",
                                    exch=scatter_of([tag + "_w_gu"]) if spread else None)
        if spread:
            parts[tag + "_w_gu"] = got[0]
        dh_in, G[tag + "_norm"] = _norm_bwd(h, 0, D, D, norm, D, D, du, tag + "_norm_bwd", res=dh)
        return dh_in

    dh2 = ffn_bwd(dh3, h2, W["ffn2_norm"], full["ffn2_w_gu"], full["ffn2_w_down"], ffn2_saved, "ffn2")

    G["w_out"] = _mm(mix, dh2, "tn", "dw_out")
    dmix = _mm(dh2, full["w_out"], "nt", "w_out_bwd")
    dgl, dof, dom, G["b_gate"] = _gate_bwd(dmix, z, W["b_gate"], of, om, "gate_bwd")

    G["w_branch_fox"] = _mm(o_fox, dof, "tn", "dw_branch_fox")
    do_fox = _mm(dof, full["w_branch_fox"], "nt", "branch_fox_bwd")
    (dq_f, dk_f, dv_f, dkm_f, dvm_f, dck, dcmk, dcq), got = _attn_bwd(
        fqn, fkn, z, Z_FV, o_fox, lse_fox, do_fox, False, FOX_HD ** -0.5, S, NX, "fox_attn_bwd", ck, cmk,
        exch=scatter_of(last_group))
    parts.update(zip(last_group, got))
    dk_f, dv_f = put_meta(dk_f, dkm_f), put_meta(dv_f, dvm_f)
    dfq, gq = _norm_bwd(z, Z_FQ, FOX_W, LANES, gq_f, FOX_HD, FOX_HD, dq_f, "fox_q_norm_bwd", out_dtype=BF16)
    dfk, gk = _norm_bwd(z, Z_FK, FOX_W, LANES, gk_f, FOX_HD, FOX_HD, dk_f, "fox_k_norm_bwd", out_dtype=BF16)
    G["fox_q_norm"] = gq[:, :FOX_HD] + gq[:, FOX_HD:]
    G["fox_k_norm"] = gk[:, :FOX_HD] + gk[:, FOX_HD:]
    dc = pad_lanes(dck.reshape(B, H, S).transpose(0, 2, 1)
                   + dcq.transpose(0, 2, 1, 3).reshape(B, S, H))
    dcm = pad_lanes(dcmk.sum(0)[:, 0, :].T)
    dcm = jnp.where(jnp.arange(LANES)[:, None] < N_META, dcm, 0.0)
    dfl, dflm, dbf = _cum_bwd(dc, dcm, fl, flm, bf, "forget_cum_bwd")
    G["b_forget"] = dbf[:, :HEADS]
    dfl_t = rows_T(dfl.reshape(NX, LANES), dflm)

    G["w_branch_mla"] = _mm(o_mla, dom, "tn", "dw_branch_mla")
    do_mla = _mm(dom, full["w_branch_mla"], "nt", "branch_mla_bwd")
    (dq_m, dk_m, dvk, dkm_m, dvkm), _ = _attn_bwd(
        mqn, mkn, kv_lin, 0, o_mla, lse_mla, do_mla, True, MLA_QK ** -0.5, S, NX, "mla_attn_bwd")
    dk_m, dvk = put_meta(dk_m, dkm_m), put_meta(dvk, dvkm)
    dq_lin, gq = _norm_bwd(q_lin, 0, H * LANES, LANES, gq_m, LANES, MLA_QK, dq_m, "mla_q_norm_bwd", tabs=tabs)
    G["mla_q_norm"] = gq[:, :MLA_QK]
    G["mla_w_uq"] = _mm(cqn, dq_lin, "tn", "dw_uq").reshape(MLA_Q_RANK, H, LANES)[:, :, :MLA_QK].reshape(
        MLA_Q_RANK, H * MLA_QK)
    dcqn = _mm(dq_lin, w_uq_p, "nt", "mla_uq_bwd")
    dcq, G["mla_cq_norm"] = _norm_bwd(z, Z_CQ, MLA_Q_RANK, MLA_Q_RANK, W["mla_cq_norm"], MLA_Q_RANK,
                                      MLA_Q_RANK, dcqn, "mla_cq_norm_bwd", out_dtype=BF16)
    dkv_lin, dkr, gk = _mla_k_bwd(kv_lin, z, Z_KR, gk_m, tabs, dk_m, dvk, "mla_k_norm_bwd")
    G["mla_k_norm"] = gk[:, :MLA_QK]
    G["mla_w_ukv"] = _mm(ckvn, dkv_lin, "tn", "dw_ukv")
    dckvn = _mm(dkv_lin, full["mla_w_ukv"], "nt", "mla_ukv_bwd")
    dckv, G["mla_ckv_norm"] = _norm_bwd(z, Z_CKV, MLA_KV_RANK, MLA_KV_RANK, W["mla_ckv_norm"], MLA_KV_RANK,
                                        MLA_KV_RANK, dckvn, "mla_ckv_norm_bwd", out_dtype=BF16)

    dz = jnp.concatenate([dgl, dfq, dfk, dv_f.astype(BF16), dcq, dckv, dfl_t.astype(BF16),
                          dkr.astype(BF16)], axis=1)
    dw_in_p = _mm(u2, dz, "tn", "dw_in")
    G["w_in"] = jnp.concatenate([
        dw_in_p[:, Z_FQ:Z_CQ], dw_in_p[:, Z_F:Z_F + HEADS], dw_in_p[:, Z_CQ:Z_F],
        dw_in_p[:, Z_KR:Z_KR + MLA_ROPE], dw_in_p[:, Z_G:Z_G + 2 * D]], axis=1)
    du2, got = _mm(dz, w_in_p, "nt", "w_in_bwd", exch=scatter_of(mix_small))
    parts.update(zip(mix_small, got))
    dh1, G["mix_norm"] = _norm_bwd(h1, 0, D, D, W["mix_norm"], D, D, du2, "mix_norm_bwd", res=dh2)

    dh0 = ffn_bwd(dh1, h0, W["ffn1_norm"], full["ffn1_w_gu"], full["ffn1_w_down"], ffn1_saved, "ffn1",
                  behind_down=["w_in"], spread=True)
    grad_x = dh0[:NX].reshape(B, S, D)
    G["meta_tokens"] = dh0[NX:NX + N_META]

    res = {}
    for n, _ in big:
        outs4 = _adamw(parts[n], W[n][0], Mo[n][0], Vo[n][0], "adamw_" + n)
        for key, arr in zip(("g", "d", "m", "v"), outs4):
            res[key, n] = arr[None]

    small = [n for n in names if n not in dict(big) and n != "meta_tokens"]
    small_shapes = [W[n].shape for n in small]
    spack = _pack([G["meta_tokens"]] + [G[n] for n in small], 1024, 8)
    (sparts,) = _exchange([spack], "gather_small_grads", gather=True)
    sflat = sparts.reshape(N_DEV, -1)
    dsh = D // N_DEV
    meta_part = lax.dynamic_slice(sflat[:, :N_META * D].reshape(N_DEV, N_META, D),
                                  (0, 0, me * dsh), (N_DEV, N_META, dsh)).reshape(N_DEV, -1)
    rep_len = sum(int(np.prod(s)) for s in small_shapes)
    rep_part = sflat[:, N_META * D:N_META * D + rep_len]
    sp = _pack_rows([meta_part, rep_part], LANES, _chunk_rows(LANES))
    pks = lambda src: _pack([src["meta_tokens"]] + [src[n] for n in small], LANES, _chunk_rows(LANES))
    g_s, d_s, m_s, v_s = _adamw(sp, pks(W), pks(Mo), pks(Vo), "adamw_small")
    shapes_s = [W["meta_tokens"].shape] + small_shapes
    for key, packed in (("g", g_s), ("d", d_s), ("m", m_s), ("v", v_s)):
        for n, arr in zip(["meta_tokens"] + small, _unpack(packed, shapes_s)):
            res[key, n] = arr

    outs = [loss, grad_x]
    for key in ("g", "d", "m", "v"):
        outs += [res[key, n] for n in names]
    return tuple(outs)
```

```python
import numpy as np
import jax
import jax.numpy as jnp
from jax import lax
from jax.experimental import pallas as pl
from jax.experimental.pallas import tpu as pltpu

F32 = jnp.float32
BF16 = jnp.bfloat16

N_META = 16
EPS = 1e-6
HEADS = 8
FOX_HD = 64
FOX_W = HEADS * FOX_HD
MLA_Q_RANK = 256
MLA_KV_RANK = 128
MLA_NOPE = 64
MLA_ROPE = 32
MLA_QK = MLA_NOPE + MLA_ROPE
MLA_V = 64
ROPE_THETA = 10000.0
LANES = 128
HALF = LANES // 2
META_BLK = 128
NEG = -1e30

ADAM_LR = 0.001
ADAM_B1 = 0.9
ADAM_B2 = 0.999
ADAM_EPS = 1e-08
ADAM_WD = 0.01
ADAM_STEP = 10

N_DEV = 8
AXES = ("x", "y", "c")
VMEM_LIMIT_BYTES = 56 * 1024 * 1024


def _tile(n, cap, mult):
    best = None
    for d in range(mult, min(n, cap) + 1, mult):
        if n % d == 0:
            best = d
    return n if best is None else best


VREG_ELEMS = 8 * LANES


def _row_tile(rows, width):
    return _tile(rows, max(16, (1 << 19) // width), 16)


def _chunk_rows(width):
    rows = 16
    while 2 * rows * width <= 8 * VREG_ELEMS:
        rows *= 2
    return rows


def _by_chunks(rows, width, step, init=()):
    return step(pl.ds(0, rows), init)


def _params(sem=None):
    return pltpu.CompilerParams(dimension_semantics=sem, vmem_limit_bytes=VMEM_LIMIT_BYTES)


def _mm(a, b, mode, name, out_dtype=F32, scale=1.0, res=None, exch=None):
    if mode == "nn":
        (M, K), (K2, N) = a.shape, b.shape
    elif mode == "nt":
        (M, K), (N, K2) = a.shape, b.shape
    else:
        (K, M), (K2, N) = a.shape, b.shape
    assert K == K2, (a.shape, b.shape, mode)
    if mode == "tn":
        tm, tk = _tile(M, 1408, 128), _tile(K, 2080, 16)
    else:
        tm, tk = _tile(M, 640, 16), _tile(K, 4224, 128)
    tn = _tile(N, 1408, 128)
    nk = K // tk
    ni, nj = M // tm, N // tn
    bytes_a, bytes_b = a.size * a.dtype.itemsize, b.size * b.dtype.itemsize
    j_outer = nk == 1 and bytes_a * nj + bytes_b < bytes_a + bytes_b * ni
    ij = (lambda g0, g1: (g1, g0)) if j_outer else (lambda g0, g1: (g0, g1))

    def spec(shape, at):
        return pl.BlockSpec(shape, lambda g0, g1, k: at(*ij(g0, g1), k))

    a_spec = {"nn": spec((tm, tk), lambda i, j, k: (i, k)),
              "nt": spec((tm, tk), lambda i, j, k: (i, k)),
              "tn": spec((tk, tm), lambda i, j, k: (k, i))}[mode]
    b_spec = {"nn": spec((tk, tn), lambda i, j, k: (k, j)),
              "nt": spec((tn, tk), lambda i, j, k: (j, k)),
              "tn": spec((tk, tn), lambda i, j, k: (k, j))}[mode]
    dims = {"nn": (((1,), (0,)), ((), ())), "nt": (((1,), (1,)), ((), ())),
            "tn": (((0,), (0,)), ((), ()))}[mode]
    o_spec = spec((tm, tn), lambda i, j, k: (i, j))
    has_res = res is not None

    def body(*refs):
        a_ref, b_ref = refs[:2]
        r_ref = refs[2] if has_res else None
        o_ref = refs[2 + has_res]

        def finish(acc):
            o = acc * scale
            if has_res:
                o = o + r_ref[...]
            o_ref[...] = o.astype(out_dtype)

        prod = lax.dot_general(a_ref[...].astype(BF16), b_ref[...].astype(BF16), dims,
                               preferred_element_type=F32)
        if nk == 1:
            finish(prod)
            return
        acc_ref = refs[3 + has_res]
        k = pl.program_id(2)

        @pl.when(k == 0)
        def _():
            acc_ref[...] = prod

        @pl.when((k > 0) & (k < nk - 1))
        def _():
            acc_ref[...] += prod

        @pl.when(k == nk - 1)
        def _():
            finish(acc_ref[...] + prod)

    ins = [a, b] + ([res] if has_res else [])
    specs = [a_spec, b_spec] + ([o_spec] if has_res else [])
    (out,), got = _call(
        body, name, (nj, ni, nk) if j_outer else (ni, nj, nk), specs, [o_spec],
        [jax.ShapeDtypeStruct((M, N), out_dtype)],
        ins, scratch_shapes=[pltpu.VMEM((tm, tn), F32)] if nk > 1 else [],
        sem=("parallel", "parallel", "arbitrary"), exch=exch)
    return out if exch is None else (out, got)


def _rope_fwd(y, c, s1, s2):
    return y * c + pltpu.roll(y, LANES - 16, 1) * s1 + pltpu.roll(y, 16, 1) * s2


def _rope_bwd(dy, c, s1, s2):
    return dy * c + pltpu.roll(dy * s1, 16, 1) + pltpu.roll(dy * s2, LANES - 16, 1)


def _group_sum(v, seg):
    if seg == v.shape[-1]:
        return jnp.sum(v, axis=-1, keepdims=True)
    lo = lax.broadcasted_iota(jnp.int32, v.shape, 1) < seg
    s_lo = jnp.sum(jnp.where(lo, v, 0.0), axis=-1, keepdims=True)
    s_hi = jnp.sum(jnp.where(lo, 0.0, v), axis=-1, keepdims=True)
    return jnp.where(lo, s_lo, s_hi)


def _norm_fwd(src, col0, width, bw, gain, seg, d_true, name, tabs=None, out_dtype=BF16):
    T = src.shape[0]
    tr = _row_tile(T, bw)
    inv_d = 1.0 / d_true
    c0 = col0 // bw
    assert col0 % bw == 0 and width % bw == 0

    def body(*refs):
        if tabs is None:
            x_ref, g_ref, o_ref = refs
        else:
            x_ref, g_ref, c_ref, s1_ref, s2_ref, o_ref = refs
        gain_v = g_ref[...]

        def step(rows, carry):
            xv = x_ref[rows, :]
            r = lax.rsqrt(_group_sum(xv * xv, seg) * inv_d + EPS)
            y = xv * r * gain_v
            if tabs is not None:
                y = _rope_fwd(y, c_ref[rows, :], s1_ref[rows, :], s2_ref[rows, :])
            o_ref[rows, :] = y.astype(out_dtype)
            return carry

        _by_chunks(tr, bw, step)

    specs = [pl.BlockSpec((tr, bw), lambda i, j: (i, c0 + j)), pl.BlockSpec((1, bw), lambda i, j: (0, 0))]
    ins = [src, gain]
    if tabs is not None:
        tab = pl.BlockSpec((tr, LANES), lambda i, j: (i, 0))
        specs += [tab, tab, tab]
        ins += list(tabs)
    return pl.pallas_call(
        body, name=name, grid=(T // tr, width // bw), in_specs=specs,
        out_specs=pl.BlockSpec((tr, bw), lambda i, j: (i, j)),
        out_shape=jax.ShapeDtypeStruct((T, width), out_dtype),
        compiler_params=_params(("parallel", "parallel")),
    )(*ins)


def _norm_bwd_math(xv, gain, dyv, seg, inv_d):
    r = lax.rsqrt(_group_sum(xv * xv, seg) * inv_d + EPS)
    gy = dyv * gain
    dot = _group_sum(gy * xv, seg)
    dx = r * gy - xv * (r * r * r * inv_d) * dot
    return dx, jnp.sum(dyv * xv * r, axis=0, keepdims=True)


def _norm_bwd(src, col0, width, bw, gain, seg, d_true, dy, name, tabs=None, res=None, out_dtype=F32):
    T = src.shape[0]
    tr = _row_tile(T, bw)
    inv_d = 1.0 / d_true
    c0 = col0 // bw
    has_res = res is not None

    def body(*refs):
        refs = list(refs)
        x_ref, g_ref, dy_ref = refs[:3]
        pos = 3
        if tabs is not None:
            c_ref, s1_ref, s2_ref = refs[3:6]
            pos = 6
        if has_res:
            r_ref = refs[pos]
            pos += 1
        dx_ref, dg_ref = refs[pos], refs[pos + 1]
        gain_v = g_ref[...]

        def step(rows, dg_sum):
            dyv = dy_ref[rows, :].astype(F32)
            if tabs is not None:
                dyv = _rope_bwd(dyv, c_ref[rows, :], s1_ref[rows, :], s2_ref[rows, :])
            dx, dg = _norm_bwd_math(x_ref[rows, :], gain_v, dyv, seg, inv_d)
            if has_res:
                dx = dx + r_ref[rows, :]
            dx_ref[rows, :] = dx.astype(out_dtype)
            return dg_sum + dg

        dg = _by_chunks(tr, bw, step, jnp.zeros((1, bw), F32))

        @pl.when((pl.program_id(0) == 0) & (pl.program_id(1) == 0))
        def _():
            dg_ref[...] = jnp.zeros_like(dg_ref)

        dg_ref[...] += dg

    blk = pl.BlockSpec((tr, bw), lambda i, j: (i, j))
    one = pl.BlockSpec((1, bw), lambda i, j: (0, 0))
    specs = [pl.BlockSpec((tr, bw), lambda i, j: (i, c0 + j)), one, blk]
    ins = [src, gain, dy]
    if tabs is not None:
        tab = pl.BlockSpec((tr, LANES), lambda i, j: (i, 0))
        specs += [tab, tab, tab]
        ins += list(tabs)
    if has_res:
        specs.append(blk)
        ins.append(res)
    return pl.pallas_call(
        body, name=name, grid=(T // tr, width // bw), in_specs=specs, out_specs=(blk, one),
        out_shape=(jax.ShapeDtypeStruct((T, width), out_dtype), jax.ShapeDtypeStruct((1, bw), F32)),
        compiler_params=_params(("arbitrary", "arbitrary")),
    )(*ins)


def _mla_k_raw(kv, kr):
    lane = lax.broadcasted_iota(jnp.int32, kv.shape, 1)
    return jnp.where(lane < MLA_NOPE, kv, jnp.where(lane < MLA_QK, pltpu.roll(kr, MLA_NOPE, 1), 0.0))


def _mla_k_fwd(kv_lin, z, kr_col, gain, tabs, name):
    T, W = kv_lin.shape
    tr = _row_tile(T, LANES)
    krb = kr_col // LANES
    inv_d = 1.0 / MLA_QK

    def body(kv_ref, kr_ref, g_ref, c_ref, s1_ref, s2_ref, o_ref):
        gain_v = g_ref[...]

        def step(rows, carry):
            xv = _mla_k_raw(kv_ref[rows, :], kr_ref[rows, :])
            r = lax.rsqrt(jnp.sum(xv * xv, axis=-1, keepdims=True) * inv_d + EPS)
            o_ref[rows, :] = _rope_fwd(xv * r * gain_v, c_ref[rows, :], s1_ref[rows, :],
                                       s2_ref[rows, :]).astype(BF16)
            return carry

        _by_chunks(tr, LANES, step)

    blk = pl.BlockSpec((tr, LANES), lambda i, h: (i, h))
    tab = pl.BlockSpec((tr, LANES), lambda i, h: (i, 0))
    return pl.pallas_call(
        body, name=name, grid=(T // tr, W // LANES),
        in_specs=[blk, pl.BlockSpec((tr, LANES), lambda i, h: (i, krb)),
                  pl.BlockSpec((1, LANES), lambda i, h: (0, 0)), tab, tab, tab],
        out_specs=blk, out_shape=jax.ShapeDtypeStruct((T, W), BF16),
        compiler_params=_params(("parallel", "parallel")),
    )(kv_lin, z, gain, *tabs)


def _mla_k_bwd(kv_lin, z, kr_col, gain, tabs, dk, dvk, name):
    T, W = kv_lin.shape
    tr = _row_tile(T, LANES)
    krb = kr_col // LANES
    inv_d = 1.0 / MLA_QK

    def body(kv_ref, kr_ref, g_ref, c_ref, s1_ref, s2_ref, dk_ref, dvk_ref, dkv_ref, dkr_ref, dg_ref):
        h = pl.program_id(1)
        gain_v = g_ref[...]

        @pl.when(h == 0)
        def _():
            dkr_ref[...] = jnp.zeros_like(dkr_ref)

        def step(rows, dg_sum):
            xv = _mla_k_raw(kv_ref[rows, :], kr_ref[rows, :])
            dyv = _rope_bwd(dk_ref[rows, :], c_ref[rows, :], s1_ref[rows, :], s2_ref[rows, :])
            dx, dg = _norm_bwd_math(xv, gain_v, dyv, LANES, inv_d)
            lane = lax.broadcasted_iota(jnp.int32, dx.shape, 1)
            dkv_ref[rows, :] = jnp.where(lane < MLA_NOPE, dx, dvk_ref[rows, :])
            dkr_ref[rows, :] += pltpu.roll(jnp.where((lane >= MLA_NOPE) & (lane < MLA_QK), dx, 0.0),
                                           LANES - MLA_NOPE, 1)
            return dg_sum + dg

        dg = _by_chunks(tr, LANES, step, jnp.zeros((1, LANES), F32))

        @pl.when((pl.program_id(0) == 0) & (h == 0))
        def _():
            dg_ref[...] = jnp.zeros_like(dg_ref)

        dg_ref[...] += dg

    blk = pl.BlockSpec((tr, LANES), lambda i, h: (i, h))
    tab = pl.BlockSpec((tr, LANES), lambda i, h: (i, 0))
    one = pl.BlockSpec((1, LANES), lambda i, h: (0, 0))
    return pl.pallas_call(
        body, name=name, grid=(T // tr, W // LANES),
        in_specs=[blk, pl.BlockSpec((tr, LANES), lambda i, h: (i, krb)), one, tab, tab, tab, blk, blk],
        out_specs=(blk, tab, one),
        out_shape=(jax.ShapeDtypeStruct((T, W), F32), jax.ShapeDtypeStruct((T, LANES), F32),
                   jax.ShapeDtypeStruct((1, LANES), F32)),
        compiler_params=_params(("arbitrary", "arbitrary")),
    )(kv_lin, z, gain, *tabs, dk, dvk)


def _ffn_up(u, w_gu, name, exch=None):
    T, D = u.shape
    F = w_gu.shape[1] // 2
    tm, tn = _tile(T, 640, 16), _tile(F, 1408, 128)
    nj = F // tn

    def body(u_ref, wg_ref, wu_ref, sa_ref, sb_ref, a_ref):
        uv = u_ref[...]
        g = jnp.dot(uv, wg_ref[...], preferred_element_type=F32)
        up = jnp.dot(uv, wu_ref[...], preferred_element_type=F32)
        sg = jax.nn.sigmoid(g)
        silu = g * sg
        sa_ref[...] = silu.astype(BF16)
        sb_ref[...] = (up * (sg + silu * (1.0 - sg))).astype(BF16)
        a_ref[...] = (silu * up).astype(BF16)

    o_spec = pl.BlockSpec((tm, tn), lambda j, i: (i, j))
    sh = jax.ShapeDtypeStruct((T, F), BF16)
    return _call(
        body, name, (nj, T // tm),
        [pl.BlockSpec((tm, D), lambda j, i: (i, 0)),
         pl.BlockSpec((D, tn), lambda j, i: (0, j)),
         pl.BlockSpec((D, tn), lambda j, i: (0, j + nj))],
        [o_spec, o_spec, o_spec], [sh, sh, sh], [u, w_gu, w_gu],
        sem=("parallel", "parallel"), exch=exch)


def _ffn_down_bwd(dh, w_down, sa, sb, name, exch=None):
    T, D = dh.shape
    F = w_down.shape[0]
    tm, tn = _tile(T, 640, 16), _tile(F, 1408, 128)

    def body(dh_ref, w_ref, sa_ref, sb_ref, dg_ref, dup_ref):
        da = 0.5 * lax.dot_general(dh_ref[...].astype(BF16), w_ref[...], (((1,), (1,)), ((), ())),
                                   preferred_element_type=F32)
        dup_ref[...] = (da * sa_ref[...].astype(F32)).astype(BF16)
        dg_ref[...] = (da * sb_ref[...].astype(F32)).astype(BF16)

    t_spec = pl.BlockSpec((tm, tn), lambda j, i: (i, j))
    sh = jax.ShapeDtypeStruct((T, F), BF16)
    return _call(
        body, name, (F // tn, T // tm),
        [pl.BlockSpec((tm, D), lambda j, i: (i, 0)),
         pl.BlockSpec((tn, D), lambda j, i: (j, 0)), t_spec, t_spec],
        [t_spec, t_spec], [sh, sh], [dh, w_down, sa, sb],
        sem=("parallel", "parallel"), exch=exch)


def _ffn_up_bwd_dx(dg, dup, w_gu, name, exch=None):
    T, F = dg.shape
    D = w_gu.shape[0]
    tm, tk = _tile(T, 640, 16), _tile(F, 2816, 128)
    nk = F // tk
    nt = (((1,), (1,)), ((), ()))

    def body(dg_ref, dup_ref, wg_ref, wu_ref, o_ref, acc_ref):
        k = pl.program_id(1)
        prod = (lax.dot_general(dg_ref[...], wg_ref[...], nt, preferred_element_type=F32)
                + lax.dot_general(dup_ref[...], wu_ref[...], nt, preferred_element_type=F32))
        if nk == 1:
            o_ref[...] = prod
            return

        @pl.when(k == 0)
        def _():
            acc_ref[...] = prod

        @pl.when((k > 0) & (k < nk - 1))
        def _():
            acc_ref[...] += prod

        @pl.when(k == nk - 1)
        def _():
            o_ref[...] = acc_ref[...] + prod

    return _call(
        body, name, (T // tm, nk),
        [pl.BlockSpec((tm, tk), lambda i, k: (i, k)),
         pl.BlockSpec((tm, tk), lambda i, k: (i, k)),
         pl.BlockSpec((D, tk), lambda i, k: (0, k)),
         pl.BlockSpec((D, tk), lambda i, k: (0, k + nk))],
        [pl.BlockSpec((tm, D), lambda i, k: (i, 0))], [jax.ShapeDtypeStruct((T, D), F32)],
        [dg, dup, w_gu, w_gu], scratch_shapes=[pltpu.VMEM((tm, D), F32)],
        sem=("parallel", "arbitrary"), exch=exch)


def _logsig(x):
    return jnp.minimum(x, 0.0) - jnp.log(1.0 + jnp.exp(-jnp.abs(x)))


def _cum_fwd(fl, flm, bf, name):
    B, S, _ = fl.shape
    nb = S // LANES

    def body(fl_ref, flm_ref, bf_ref, cum_ref, cumm_ref):
        rows = lax.broadcasted_iota(jnp.int32, (LANES, LANES), 0)
        cols = lax.broadcasted_iota(jnp.int32, (LANES, LANES), 1)
        tri = (rows >= cols).astype(F32)
        bias = bf_ref[...]
        lfm = jnp.where(rows < N_META, _logsig(flm_ref[...] + bias), 0.0)
        cm = jnp.dot(tri, lfm, precision=lax.Precision.HIGHEST, preferred_element_type=F32)
        cumm_ref[...] = cm * LOG2E
        base = cm[LANES - 1:LANES, :]
        for b in range(B):
            def blk(i, carry):
                r0 = pl.multiple_of(i * LANES, LANES)
                lf = _logsig(fl_ref[b, pl.ds(r0, LANES), :] + bias)
                c = jnp.dot(tri, lf, precision=lax.Precision.HIGHEST,
                            preferred_element_type=F32) + carry
                cum_ref[b, pl.ds(r0, LANES), :] = c * LOG2E
                return c[LANES - 1:LANES, :]

            lax.fori_loop(0, nb, blk, base)

    return pl.pallas_call(
        body, name=name,
        out_shape=(jax.ShapeDtypeStruct((B, S, LANES), F32),
                   jax.ShapeDtypeStruct((LANES, LANES), F32)),
        compiler_params=_params(),
    )(fl, flm, bf)


def _cum_bwd(dc, dcm, fl, flm, bf, name):
    B, S, _ = fl.shape
    nb = S // LANES

    def body(dc_ref, dcm_ref, fl_ref, flm_ref, bf_ref, dfl_ref, dflm_ref, dbf_ref):
        rows = lax.broadcasted_iota(jnp.int32, (LANES, LANES), 0)
        cols = lax.broadcasted_iota(jnp.int32, (LANES, LANES), 1)
        triu = (rows <= cols).astype(F32)
        bias = bf_ref[...]
        total = jnp.zeros((1, LANES), F32)
        dbf = jnp.zeros((1, LANES), F32)
        for b in range(B):
            tail = jnp.zeros((1, LANES), F32)
            for t in range(nb):
                r0 = (nb - 1 - t) * LANES
                rc = jnp.dot(triu, dc_ref[b, r0:r0 + LANES, :], precision=lax.Precision.HIGHEST,
                             preferred_element_type=F32) + tail
                xv = fl_ref[b, r0:r0 + LANES, :] + bias
                d = rc / (1.0 + jnp.exp(xv))
                dfl_ref[b, r0:r0 + LANES, :] = d
                tail = rc[0:1, :]
                dbf = dbf + jnp.sum(d, axis=0, keepdims=True)
            total = total + tail
        rcm = jnp.dot(triu, dcm_ref[...], precision=lax.Precision.HIGHEST,
                      preferred_element_type=F32) + total
        dm = jnp.where(rows < N_META, rcm / (1.0 + jnp.exp(flm_ref[...] + bias)), 0.0)
        dflm_ref[...] = dm
        dbf_ref[...] = dbf + jnp.sum(dm, axis=0, keepdims=True)

    return pl.pallas_call(
        body, name=name,
        out_shape=(jax.ShapeDtypeStruct((B, S, LANES), F32),
                   jax.ShapeDtypeStruct((LANES, LANES), F32),
                   jax.ShapeDtypeStruct((1, LANES), F32)),
        compiler_params=_params(),
    )(dc, dcm, fl, flm, bf)


_NT = (((1,), (1,)), ((), ()))


def _attn_specs(S, NX, qw, v_col0):
    mb = NX // META_BLK
    vb = v_col0 // qw
    return (pl.BlockSpec((S, qw), lambda b, p: (b, p)),
            pl.BlockSpec((META_BLK, qw), lambda b, p: (mb, p)),
            pl.BlockSpec((S, qw), lambda b, p: (b, vb + p)),
            pl.BlockSpec((META_BLK, qw), lambda b, p: (mb, vb + p)),
            pl.BlockSpec((S, LANES), lambda b, p: (b, p)))


def _cum_specs(S, TK):
    return [pl.BlockSpec((1, 2, S // TK, 1, TK), lambda b, p: (b, p, 0, 0, 0)),
            pl.BlockSpec((1, 2, 1, META_BLK), lambda b, p: (b, p, 0, 0))]


LOG2E = 1.4426950408889634


def _attn_fwd(qn, kn, vsrc, v_col0, mla, scale, S, NX, name, ck=None, cmk=None, exch=None):
    T = qn.shape[0]
    B = NX // S
    qw = 2 * LANES if mla else LANES
    npair = qn.shape[1] // qw
    TQ = min(512, S)
    TK = TQ
    forget = ck is not None
    a = scale * LOG2E

    def body(*refs):
        if forget:
            q_ref, k_ref, km_ref, v_ref, vm_ref, ck_ref, cmk_ref, _, o_ref, lse_ref = refs
        else:
            q_ref, k_ref, km_ref, v_ref, vm_ref, _, o_ref, lse_ref = refs
        lo = lax.broadcasted_iota(jnp.int32, (1, LANES), 1) < HALF
        mcol = lax.broadcasted_iota(jnp.int32, (TQ, META_BLK), 1)
        causal = (lax.broadcasted_iota(jnp.int32, (TQ, TK), 0)
                  >= lax.broadcasted_iota(jnp.int32, (TQ, TK), 1))
        two = lax.broadcasted_iota(jnp.int32, (TQ, 2), 1)
        for qi in range(S // TQ):
            q0 = qi * TQ
            sls = [slice(e * LANES, (e + 1) * LANES) if mla else slice(None) for e in range(2)]
            if mla:
                qts = [q_ref[q0:q0 + TQ, sl] for sl in sls]
            else:
                qts = [jnp.where(lo if e == 0 else ~lo, q_ref[q0:q0 + TQ, :], 0.0).astype(BF16)
                       for e in range(2)]

            def step(e, kt, vt, c2, mask, carry):
                m, l, acc = carry
                s = lax.dot_general(qts[e], kt, _NT, preferred_element_type=F32) * a
                if forget:
                    s = s - c2
                if mask is not None:
                    s = jnp.where(mask, s, NEG)
                m2 = jnp.max(s, axis=1, keepdims=True)
                if m is not None:
                    m2 = jnp.maximum(m, m2)
                p = jnp.exp2(s - m2)
                l2 = jnp.sum(p, axis=1, keepdims=True)
                acc2 = jnp.dot(p.astype(BF16), vt.astype(BF16), preferred_element_type=F32)
                if m is not None:
                    alpha = jnp.exp2(m - m2)
                    l2, acc2 = alpha * l + l2, alpha * acc + acc2
                return m2, l2, acc2

            def both(rows, kj, mask, carry):
                return tuple(step(e, k_ref[rows, sls[e]], v_ref[rows, sls[e]],
                                  ck_ref[0, e, kj] if forget else None, mask, carry[e]) for e in range(2))

            def below(kj, carry):
                return both(pl.ds(pl.multiple_of(kj * TK, TK), TK), kj, None, carry)

            carry = tuple(step(e, km_ref[:, sls[e]], vm_ref[:, sls[e]], cmk_ref[0, e] if forget else None,
                               mcol < N_META, (None, None, None)) for e in range(2))
            if qi:
                carry = lax.fori_loop(0, qi, below, carry)
            carry = both(slice(q0, q0 + TK), qi, causal, carry)
            outs = [acc / l for _, l, acc in carry]
            lses = [m + jnp.log2(l) for m, l, _ in carry]
            first = pltpu.roll(outs[0], HALF, 1) if mla else outs[0]
            o_ref[q0:q0 + TQ, :] = jnp.where(lo, first, outs[1])
            lse_ref[0, 0, q0:q0 + TQ, :] = jnp.where(two == 0, lses[0], lses[1])

    qk, kmeta, vv, vmeta, pair = _attn_specs(S, NX, qw, v_col0)
    specs = [qk, qk, kmeta, vv, vmeta]
    ins = [qn, kn, kn, vsrc, vsrc]
    if forget:
        specs += _cum_specs(S, TK)
        ins += [ck, cmk]
    specs.append(pl.BlockSpec(memory_space=pl.ANY))
    ins.append(jnp.zeros((T, npair * LANES), F32))
    lse_spec = pl.BlockSpec((1, 1, S, 2), lambda b, p: (b, p, 0, 0))
    return _call(
        body, name, (B, npair), specs, [pair, lse_spec],
        [jax.ShapeDtypeStruct((T, npair * LANES), F32), jax.ShapeDtypeStruct((B, npair, S, 2), F32)],
        ins, sem=("parallel", "parallel"), aliases={len(ins) - 1: 0}, exch=exch)


def _attn_bwd(qn, kn, vsrc, v_col0, o, lse, do, mla, scale, S, NX, name, ck=None, cmk=None, exch=None):
    T, W = qn.shape
    B = NX // S
    qw = 2 * LANES if mla else LANES
    npair = W // qw
    TQ = min(512, S)
    TK = TQ
    forget = ck is not None
    a = scale * LOG2E
    _TN = (((0,), (0,)), ((), ()))

    def body(*refs):
        refs = list(refs)
        q_ref, k_ref, km_ref, v_ref, vm_ref, o_ref, do_ref, lse_ref = refs[:8]
        pos = 8
        if forget:
            ck_ref, cmk_ref = refs[8:10]
            pos = 10
        pos += 3
        dq_ref, dk_ref, dv_ref, dkm_ref, dvm_ref = refs[pos:pos + 5]
        if forget:
            dck_ref, dcm_ref, dcq_ref = refs[pos + 5:pos + 8]
            dck_ref[...] = jnp.zeros_like(dck_ref)
            dcm_ref[...] = jnp.zeros_like(dcm_ref)
        dk_ref[...] = jnp.zeros_like(dk_ref)
        dv_ref[...] = jnp.zeros_like(dv_ref)
        dkm_ref[...] = jnp.zeros_like(dkm_ref)
        dvm_ref[...] = jnp.zeros_like(dvm_ref)
        lo = lax.broadcasted_iota(jnp.int32, (1, LANES), 1) < HALF
        mcol = lax.broadcasted_iota(jnp.int32, (TQ, META_BLK), 1)
        causal = (lax.broadcasted_iota(jnp.int32, (TQ, TK), 0)
                  >= lax.broadcasted_iota(jnp.int32, (TQ, TK), 1))
        two = lax.broadcasted_iota(jnp.int32, (TQ, 2), 1)
        for qi in range(S // TQ):
            q0 = qi * TQ
            dof = do_ref[q0:q0 + TQ, :]
            prod = dof * o_ref[q0:q0 + TQ, :]
            lse2 = lse_ref[0, 0, q0:q0 + TQ, :]
            sls = [slice(e * LANES, (e + 1) * LANES) if mla else slice(None) for e in range(2)]
            mine = [lo, ~lo]
            if mla:
                qts = [q_ref[q0:q0 + TQ, sl] for sl in sls]
                dots = [jnp.where(lo, 0.0, pltpu.roll(dof, HALF, 1) if e == 0 else dof).astype(BF16)
                        for e in range(2)]
            else:
                qts = [jnp.where(mine[e], q_ref[q0:q0 + TQ, :], 0.0).astype(BF16) for e in range(2)]
                dots = [jnp.where(mine[e], dof, 0.0).astype(BF16) for e in range(2)]
            deltas = [jnp.sum(jnp.where(mine[e], prod, 0.0), axis=1, keepdims=True) for e in range(2)]
            lse_ts = [jnp.sum(jnp.where(two == e, lse2, 0.0), axis=1, keepdims=True) for e in range(2)]

            def grads(e, kt, vt, c2, mask):
                s = lax.dot_general(qts[e], kt, _NT, preferred_element_type=F32) * a
                if forget:
                    s = s - c2
                p = jnp.exp2(s - lse_ts[e])
                if mask is not None:
                    p = jnp.where(mask, p, 0.0)
                dp = lax.dot_general(dots[e], vt, _NT, preferred_element_type=F32)
                ds = p * (dp - deltas[e])
                dsb = ds.astype(BF16)
                return (jnp.dot(dsb, kt, preferred_element_type=F32),
                        lax.dot_general(dsb, qts[e], _TN, preferred_element_type=F32) * scale,
                        lax.dot_general(p.astype(BF16), dots[e], _TN, preferred_element_type=F32),
                        -jnp.sum(ds, axis=0, keepdims=True) if forget else None,
                        jnp.sum(ds, axis=1, keepdims=True) if forget else None)

            def block(k_at, v_at, dk_at, dv_at, c_at, dc_at, mask, dqs):
                got = [grads(e, k_at(sls[e]), v_at(sls[e]).astype(BF16), c_at(e) if forget else None, mask)
                       for e in range(2)]
                if mla:
                    for e in range(2):
                        dk_at(sls[e], got[e][1])
                        dv_at(sls[e], got[e][2])
                else:
                    dk_at(sls[0], got[0][1] + got[1][1])
                    dv_at(sls[0], got[0][2] + got[1][2])
                if forget:
                    for e in range(2):
                        dc_at(e, got[e][3])
                picks = (0, 0, 4, 4) if forget else (0, 0)
                new = tuple(got[i % 2][k] for i, k in enumerate(picks))
                return new if dqs is None else tuple(x + y for x, y in zip(dqs, new))

            def add_to(ref, *lead):
                def add(*idx_and_val):
                    *idx, val = idx_and_val
                    ref[(*lead, *idx)] += val
                return add

            def token_block(rows, kj, mask, dqs):
                return block(lambda sl: k_ref[rows, sl], lambda sl: v_ref[rows, sl],
                             lambda sl, val: add_to(dk_ref)(rows, sl, val),
                             lambda sl, val: add_to(dv_ref)(rows, sl, val),
                             lambda e: ck_ref[0, e, kj], lambda e, val: add_to(dck_ref, 0)(e, kj, val),
                             mask, dqs)

            dqs = block(lambda sl: km_ref[:, sl], lambda sl: vm_ref[:, sl],
                        lambda sl, val: add_to(dkm_ref, 0)(slice(None), sl, val),
                        lambda sl, val: add_to(dvm_ref, 0)(slice(None), sl, val),
                        lambda e: cmk_ref[0, e], lambda e, val: add_to(dcm_ref, 0)(e, val),
                        mcol < N_META, None)

            def below(kj, dqs):
                return token_block(pl.ds(pl.multiple_of(kj * TK, TK), TK), kj, None, dqs)

            if qi:
                dqs = lax.fori_loop(0, qi, below, dqs)
            dqs = token_block(slice(q0, q0 + TK), qi, causal, dqs)
            if forget:
                dcq_ref[0, 0, q0:q0 + TQ, :] = jnp.where(two == 0, dqs[2], dqs[3])
            if mla:
                for e in range(2):
                    dq_ref[q0:q0 + TQ, sls[e]] = dqs[e] * scale
            else:
                dq_ref[q0:q0 + TQ, :] = jnp.where(lo, dqs[0], dqs[1]) * scale

    qk, kmeta, vv, vmeta, pair = _attn_specs(S, NX, qw, v_col0)
    lse_spec = pl.BlockSpec((1, 1, S, 2), lambda b, p: (b, p, 0, 0))
    specs = [qk, qk, kmeta, vv, vmeta, pair, pair, lse_spec]
    ins = [qn, kn, kn, vsrc, vsrc, o, do, lse]
    if forget:
        specs += _cum_specs(S, TK)
        ins += [ck, cmk]
    first_alias = len(ins)
    specs += [pl.BlockSpec(memory_space=pl.ANY)] * 3
    ins += [jnp.zeros((T, W), F32)] * 3
    mspec = pl.BlockSpec((1, META_BLK, qw), lambda b, p: (b, 0, p))
    out_specs = [qk, qk, qk, mspec, mspec]
    tok = jax.ShapeDtypeStruct((T, W), F32)
    met = jax.ShapeDtypeStruct((B, META_BLK, W), F32)
    out_shape = [tok, tok, tok, met, met]
    if forget:
        out_specs += _cum_specs(S, TK) + [lse_spec]
        out_shape += [jax.ShapeDtypeStruct((B, HEADS, S // TK, 1, TK), F32),
                      jax.ShapeDtypeStruct((B, HEADS, 1, META_BLK), F32),
                      jax.ShapeDtypeStruct((B, npair, S, 2), F32)]
    return _call(
        body, name, (B, npair), specs, out_specs, out_shape, ins, sem=("parallel", "parallel"),
        aliases={first_alias: 0, first_alias + 1: 1, first_alias + 2: 2}, exch=exch)


def _gate_fwd(z, bg, of, om, name):
    T, D = of.shape
    tm = _tile(T, 640, 16)

    def body(z_ref, bg_ref, of_ref, om_ref, o_ref):
        bias = bg_ref[...]

        def step(rows, carry):
            gt = jax.nn.sigmoid(z_ref[rows, :] + bias)
            o_ref[rows, :] = (gt[:, :D] * of_ref[rows, :] + gt[:, D:] * om_ref[rows, :]).astype(BF16)
            return carry

        _by_chunks(tm, D, step)

    row = pl.BlockSpec((tm, D), lambda i: (i, 0))
    return pl.pallas_call(
        body, name=name, grid=(T // tm,),
        in_specs=[pl.BlockSpec((tm, 2 * D), lambda i: (i, 0)),
                  pl.BlockSpec((1, 2 * D), lambda i: (0, 0)), row, row],
        out_specs=row, out_shape=jax.ShapeDtypeStruct((T, D), BF16),
        compiler_params=_params(("parallel",)),
    )(z, bg, of, om)


def _gate_bwd(dmix, z, bg, of, om, name):
    T, D = of.shape
    tm = _tile(T, 640, 16)

    def body(dm_ref, z_ref, bg_ref, of_ref, om_ref, dgl_ref, dof_ref, dom_ref, dbg_ref):
        bias = bg_ref[...]

        def step(rows, dbg_sum):
            gt = jax.nn.sigmoid(z_ref[rows, :] + bias)
            dm = dm_ref[rows, :]
            dof_ref[rows, :] = (dm * gt[:, :D]).astype(BF16)
            dom_ref[rows, :] = (dm * gt[:, D:]).astype(BF16)
            dgl = jnp.concatenate([dm * of_ref[rows, :], dm * om_ref[rows, :]], axis=1) * gt * (1.0 - gt)
            dgl_ref[rows, :] = dgl.astype(BF16)
            return dbg_sum + jnp.sum(dgl, axis=0, keepdims=True)

        dbg = _by_chunks(tm, D, step, jnp.zeros((1, 2 * D), F32))

        @pl.when(pl.program_id(0) == 0)
        def _():
            dbg_ref[...] = jnp.zeros_like(dbg_ref)

        dbg_ref[...] += dbg

    row = pl.BlockSpec((tm, D), lambda i: (i, 0))
    wide = pl.BlockSpec((tm, 2 * D), lambda i: (i, 0))
    one = pl.BlockSpec((1, 2 * D), lambda i: (0, 0))
    return pl.pallas_call(
        body, name=name, grid=(T // tm,),
        in_specs=[row, wide, one, row, row], out_specs=(wide, row, row, one),
        out_shape=(jax.ShapeDtypeStruct((T, 2 * D), BF16), jax.ShapeDtypeStruct((T, D), BF16),
                   jax.ShapeDtypeStruct((T, D), BF16), jax.ShapeDtypeStruct((1, 2 * D), F32)),
        compiler_params=_params(("arbitrary",)),
    )(dmix, z, bg, of, om)


def _loss(h, tgt, n_valid, name):
    T, D = h.shape
    tm = _tile(T, 640, 16)

    def body(h_ref, t_ref, dh_ref, l_ref):
        i = pl.program_id(0)
        row_in_block = lax.broadcasted_iota(jnp.int32, (tm, D), 0)

        def step(rows, part):
            err = jnp.where(row_in_block + i * tm < n_valid, h_ref[rows, :] - t_ref[rows, :], 0.0)
            dh_ref[rows, :] = err * (1.0 / D)
            return part + jnp.sum(err * err, axis=0, keepdims=True)

        part = _by_chunks(tm, D, step, jnp.zeros((1, D), F32))

        @pl.when(i == 0)
        def _():
            l_ref[...] = jnp.zeros_like(l_ref)

        l_ref[...] += 0.5 * jnp.sum(part) * (1.0 / D)

    row = pl.BlockSpec((tm, D), lambda i: (i, 0))
    acc = pl.BlockSpec((8, LANES), lambda i: (0, 0))
    return pl.pallas_call(
        body, name=name, grid=(T // tm,), in_specs=[row, row], out_specs=(row, acc),
        out_shape=(jax.ShapeDtypeStruct((T, D), F32), jax.ShapeDtypeStruct((8, LANES), F32)),
        compiler_params=_params(("arbitrary",)),
    )(h, tgt)


def _adamw(parts, w, m, v, name):
    P, R, C = parts.shape
    tr = _tile(R, max(8, (1 << 18) // C), 8)
    bc1 = 1.0 - ADAM_B1 ** ADAM_STEP
    bc2 = 1.0 - ADAM_B2 ** ADAM_STEP

    def body(p_ref, w_ref, m_ref, v_ref, g_ref, d_ref, m2_ref, v2_ref):
        def step(rows, carry):
            g = p_ref[0, rows, :].astype(F32)
            for j in range(1, P):
                g = g + p_ref[j, rows, :].astype(F32)
            m2 = ADAM_B1 * m_ref[rows, :] + (1.0 - ADAM_B1) * g
            v2 = ADAM_B2 * v_ref[rows, :] + (1.0 - ADAM_B2) * (g * g)
            m_hat = m2 / bc1
            v_hat = v2 / bc2
            g_ref[rows, :] = g
            d_ref[rows, :] = -ADAM_LR * (m_hat / (jnp.sqrt(v_hat) + ADAM_EPS) + ADAM_WD * w_ref[rows, :])
            m2_ref[rows, :] = m2
            v2_ref[rows, :] = v2
            return carry

        _by_chunks(tr, C, step)

    row = pl.BlockSpec((tr, C), lambda i: (i, 0))
    sh = jax.ShapeDtypeStruct((R, C), F32)
    return pl.pallas_call(
        body, name=name, grid=(R // tr,),
        in_specs=[pl.BlockSpec((P, tr, C), lambda i: (0, i, 0)), row, row, row],
        out_specs=(row, row, row, row), out_shape=(sh, sh, sh, sh),
        compiler_params=_params(("parallel",)),
    )(parts, w, m, v)


def _peer(d):
    x, y, c = lax.axis_index("x"), lax.axis_index("y"), lax.axis_index("c")
    px = 1 - x if d & 4 else x
    py = 1 - y if d & 2 else y
    pc = 1 - c if d & 1 else c
    return (px, py, pc), 4 * px + 2 * py + pc


class _Exchange:
    def __init__(self, srcs, gather):
        self.srcs, self.gather, self.n = list(srcs), gather, len(srcs)
        n = self.n
        hbm = pl.BlockSpec(memory_space=pl.ANY)
        self.in_specs = [hbm] * n
        self.out_specs = [hbm] * n
        self.out_shape = [jax.ShapeDtypeStruct((N_DEV,) + s.shape[-2:], s.dtype) for s in srcs]
        self.scratch = [pltpu.SemaphoreType.DMA((N_DEV - 1, n)), pltpu.SemaphoreType.DMA((N_DEV - 1, n)),
                        pltpu.SemaphoreType.DMA((n,))]

    def _copies(self, src_refs, out_refs, sems):
        send_sems, recv_sems, local_sems = sems
        _, me = _peer(0)

        def remote(w, d, landing):
            dev, lin = _peer(d)
            return pltpu.make_async_remote_copy(
                src_ref=src_refs[w] if self.gather else src_refs[w].at[lin],
                dst_ref=out_refs[w].at[lin if landing else me],
                send_sem=send_sems.at[d - 1, w], recv_sem=recv_sems.at[d - 1, w],
                device_id=dev, device_id_type=pl.DeviceIdType.MESH)

        pairs = [(w, d) for d in range(1, N_DEV) for w in range(self.n)]
        own = [pltpu.make_async_copy(src_refs[w] if self.gather else src_refs[w].at[me],
                                     out_refs[w].at[me], local_sems.at[w]) for w in range(self.n)]
        return own, [remote(w, d, False) for w, d in pairs], [remote(w, d, True) for w, d in pairs]

    def _gather_copies(self, src_refs, out_refs, sems):
        send_sems, recv_sems, local_sems = sems
        x, y, c = lax.axis_index("x"), lax.axis_index("y"), lax.axis_index("c")
        me, sibling = (x, y, c), (x, y, 1 - c)
        chips = [(1 - x, y), (x, 1 - y), (1 - x, 1 - y)]

        def copy(w, k, block, to, src=None):
            rows = out_refs[w].at[4 * block[0] + 2 * block[1] + block[2]]
            return pltpu.make_async_remote_copy(
                src_ref=rows if src is None else src, dst_ref=rows,
                send_sem=send_sems.at[k, w], recv_sem=recv_sems.at[k, w],
                device_id=to, device_id_type=pl.DeviceIdType.MESH)

        ws = range(self.n)
        own = [pltpu.make_async_copy(src_refs[w], out_refs[w].at[4 * x + 2 * y + c], local_sems.at[w])
               for w in ws]
        first = [copy(w, 0, me, sibling, src_refs[w]) for w in ws]
        first += [copy(w, 1 + j, me, (*chip, c), src_refs[w]) for j, chip in enumerate(chips) for w in ws]
        landed = [[copy(w, 1 + j, (*chip, c), me) for w in ws] for j, chip in enumerate(chips)]
        passed = [[copy(w, 4 + j, (*chip, c), sibling) for w in ws] for j, chip in enumerate(chips)]
        from_sibling = [copy(w, 0, sibling, me) for w in ws]
        from_sibling += [copy(w, 4 + j, (*chip, 1 - c), me) for j, chip in enumerate(chips) for w in ws]
        return own, first, landed, passed, from_sibling

    def start(self, src_refs, out_refs, sems):
        if self.gather:
            own, first = self._gather_copies(src_refs, out_refs, sems)[:2]
            sent = first
        else:
            own, sent, _ = self._copies(src_refs, out_refs, sems)
        for cp in own + sent:
            cp.start()

    def wait(self, src_refs, out_refs, sems):
        if self.gather:
            own, first, landed, passed, from_sibling = self._gather_copies(src_refs, out_refs, sems)
            for arrived, onward in zip(landed, passed):
                for cp in arrived:
                    cp.wait_recv()
                for cp in onward:
                    cp.start()
            for cp in from_sibling:
                cp.wait_recv()
            for cp in first + [cp for group in passed for cp in group]:
                cp.wait_send()
        else:
            own, sent, landing = self._copies(src_refs, out_refs, sems)
            for cp in landing:
                cp.wait_recv()
            for cp in sent:
                cp.wait_send()
        for cp in own:
            cp.wait()


def _exchange(srcs, name, gather):
    ex = _Exchange(srcs, gather)
    n = ex.n

    def body(*refs):
        ex.start(refs[:n], refs[n:2 * n], refs[2 * n:])
        ex.wait(refs[:n], refs[n:2 * n], refs[2 * n:])

    outs = pl.pallas_call(
        body, name=name, in_specs=ex.in_specs, out_specs=tuple(ex.out_specs),
        out_shape=tuple(ex.out_shape), scratch_shapes=ex.scratch,
    )(*srcs)
    return list(outs)


def _call(body, name, grid, in_specs, out_specs, out_shape, ins, scratch_shapes=(), sem=None,
          aliases=None, exch=None):
    aliases = aliases or {}
    if exch is None:
        outs = pl.pallas_call(
            body, name=name, grid=grid, in_specs=list(in_specs), out_specs=tuple(out_specs),
            out_shape=tuple(out_shape), scratch_shapes=list(scratch_shapes),
            input_output_aliases=aliases, compiler_params=_params(sem),
        )(*ins)
        return list(outs), []
    ni, no, ns, n = len(in_specs), len(out_specs), len(scratch_shapes), exch.n
    last_ids = [g - 1 for g in grid]

    def hosted(*refs):
        cin, xin = refs[:ni], refs[ni:ni + n]
        cout, xout = refs[ni + n:ni + n + no], refs[ni + n + no:ni + 2 * n + no]
        cscr, xsem = refs[ni + 2 * n + no:ni + 2 * n + no + ns], refs[ni + 2 * n + no + ns:]
        ids = [pl.program_id(a) for a in range(len(grid))]
        first, last = ids[0] == 0, ids[0] == last_ids[0]
        for a in range(1, len(grid)):
            first, last = first & (ids[a] == 0), last & (ids[a] == last_ids[a])

        @pl.when(first)
        def _():
            exch.start(xin, xout, xsem)

        body(*cin, *cout, *cscr)

        @pl.when(last)
        def _():
            exch.wait(xin, xout, xsem)

    outs = pl.pallas_call(
        hosted, name=name, grid=grid, in_specs=list(in_specs) + exch.in_specs,
        out_specs=tuple(list(out_specs) + exch.out_specs),
        out_shape=tuple(list(out_shape) + exch.out_shape),
        scratch_shapes=list(scratch_shapes) + exch.scratch, input_output_aliases=aliases,
        compiler_params=_params(("arbitrary",) * len(grid)),
    )(*ins, *exch.srcs)
    return list(outs[:no]), list(outs[no:])


def _pack(arrs, cols, row_mult):
    flat = jnp.concatenate([a.reshape(-1) for a in arrs])
    n = flat.shape[0]
    quantum = cols * row_mult
    total = -(-n // quantum) * quantum
    return jnp.pad(flat, (0, total - n)).reshape(total // cols, cols)


def _pack_rows(arrs, cols, row_mult):
    flat = jnp.concatenate(arrs, axis=1)
    n = flat.shape[1]
    quantum = cols * row_mult
    total = -(-n // quantum) * quantum
    return jnp.pad(flat, ((0, 0), (0, total - n))).reshape(N_DEV, total // cols, cols)


def _unpack(packed, shapes):
    flat = packed.reshape(-1)
    out, off = [], 0
    for s in shapes:
        n = int(np.prod(s))
        out.append(flat[off:off + n].reshape(s))
        off += n
    return out


def _rope_tables(positions):
    inv_freq = ROPE_THETA ** (-jnp.arange(0, MLA_ROPE, 2, dtype=F32) / MLA_ROPE)
    ang = positions.astype(F32)[:, None] * inv_freq[None, :]
    cos, sin = jnp.cos(ang), jnp.sin(ang)
    n = positions.shape[0]
    ones, zeros = jnp.ones((n, MLA_NOPE), F32), jnp.zeros((n, MLA_NOPE), F32)
    tail1, tail0 = jnp.ones((n, LANES - MLA_QK), F32), jnp.zeros((n, LANES - MLA_QK), F32)
    z16 = jnp.zeros((n, 16), F32)
    c = jnp.concatenate([ones, cos, cos, tail1], axis=1)
    s1 = jnp.concatenate([zeros, -sin, z16, tail0], axis=1)
    s2 = jnp.concatenate([zeros, z16, sin, tail0], axis=1)
    return c, s1, s2


def kernel(x, meta_tokens, ffn1_norm, ffn1_w_gu, ffn1_w_down, mix_norm, w_in, b_forget, b_gate, fox_q_norm, fox_k_norm, mla_cq_norm, mla_w_uq, mla_ckv_norm, mla_w_ukv, mla_q_norm, mla_k_norm, w_branch_fox, w_branch_mla, w_out, ffn2_norm, ffn2_w_gu, ffn2_w_down, loss_target, m_meta_tokens, m_ffn1_norm, m_ffn1_w_gu, m_ffn1_w_down, m_mix_norm, m_w_in, m_b_forget, m_b_gate, m_fox_q_norm, m_fox_k_norm, m_mla_cq_norm, m_mla_w_uq, m_mla_ckv_norm, m_mla_w_ukv, m_mla_q_norm, m_mla_k_norm, m_w_branch_fox, m_w_branch_mla, m_w_out, m_ffn2_norm, m_ffn2_w_gu, m_ffn2_w_down, v_meta_tokens, v_ffn1_norm, v_ffn1_w_gu, v_ffn1_w_down, v_mix_norm, v_w_in, v_b_forget, v_b_gate, v_fox_q_norm, v_fox_k_norm, v_mla_cq_norm, v_mla_w_uq, v_mla_ckv_norm, v_mla_w_ukv, v_mla_q_norm, v_mla_k_norm, v_w_branch_fox, v_w_branch_mla, v_w_out, v_ffn2_norm, v_ffn2_w_gu, v_ffn2_w_down):
    names = ["meta_tokens", "ffn1_norm", "ffn1_w_gu", "ffn1_w_down", "mix_norm", "w_in", "b_forget",
             "b_gate", "fox_q_norm", "fox_k_norm", "mla_cq_norm", "mla_w_uq", "mla_ckv_norm",
             "mla_w_ukv", "mla_q_norm", "mla_k_norm", "w_branch_fox", "w_branch_mla", "w_out",
             "ffn2_norm", "ffn2_w_gu", "ffn2_w_down"]
    W = dict(zip(names, [meta_tokens, ffn1_norm, ffn1_w_gu, ffn1_w_down, mix_norm, w_in, b_forget,
                         b_gate, fox_q_norm, fox_k_norm, mla_cq_norm, mla_w_uq, mla_ckv_norm,
                         mla_w_ukv, mla_q_norm, mla_k_norm, w_branch_fox, w_branch_mla, w_out,
                         ffn2_norm, ffn2_w_gu, ffn2_w_down]))
    Mo = dict(zip(names, [m_meta_tokens, m_ffn1_norm, m_ffn1_w_gu, m_ffn1_w_down, m_mix_norm, m_w_in,
                          m_b_forget, m_b_gate, m_fox_q_norm, m_fox_k_norm, m_mla_cq_norm,
                          m_mla_w_uq, m_mla_ckv_norm, m_mla_w_ukv, m_mla_q_norm, m_mla_k_norm,
                          m_w_branch_fox, m_w_branch_mla, m_w_out, m_ffn2_norm, m_ffn2_w_gu,
                          m_ffn2_w_down]))
    Vo = dict(zip(names, [v_meta_tokens, v_ffn1_norm, v_ffn1_w_gu, v_ffn1_w_down, v_mix_norm, v_w_in,
                          v_b_forget, v_b_gate, v_fox_q_norm, v_fox_k_norm, v_mla_cq_norm,
                          v_mla_w_uq, v_mla_ckv_norm, v_mla_w_ukv, v_mla_q_norm, v_mla_k_norm,
                          v_w_branch_fox, v_w_branch_mla, v_w_out, v_ffn2_norm, v_ffn2_w_gu,
                          v_ffn2_w_down]))

    B, S, D = x.shape
    NX = B * S
    T = NX + META_BLK
    H = HEADS
    assert NX % META_BLK == 0 and S % LANES == 0
    me = 4 * lax.axis_index("x") + 2 * lax.axis_index("y") + lax.axis_index("c")

    big = [("ffn1_w_gu", 1), ("ffn1_w_down", 0), ("w_in", 1), ("mla_w_uq", 1), ("mla_w_ukv", 1),
           ("w_branch_fox", 1), ("w_branch_mla", 1), ("w_out", 0), ("ffn2_w_gu", 1), ("ffn2_w_down", 0)]
    mix_small = ["mla_w_uq", "mla_w_ukv", "w_branch_fox", "w_branch_mla", "w_out"]
    last_group = ["ffn2_w_gu", "ffn2_w_down"]
    axis_of = dict(big)
    full = {}

    def shards(group):
        return [W[n][0].astype(BF16) for n in group]

    def assemble(group, blks):
        for n, blk in zip(group, blks):
            _, r, c = blk.shape
            full[n] = (blk.transpose(1, 0, 2).reshape(r, N_DEV * c) if axis_of[n] == 1
                       else blk.reshape(N_DEV * r, c))

    got = _exchange(shards(["ffn1_w_gu"]) + [meta_tokens], "gather_first", gather=True)
    assemble(["ffn1_w_gu"], got[:1])
    meta_full = got[1].transpose(1, 0, 2).reshape(N_META, D)

    Z_G, Z_FQ = 0, 2 * D
    Z_FK, Z_FV = Z_FQ + FOX_W, Z_FQ + 2 * FOX_W
    Z_CQ = Z_FQ + 3 * FOX_W
    Z_CKV = Z_CQ + MLA_Q_RANK
    Z_F = Z_CKV + MLA_KV_RANK
    Z_KR = Z_F + LANES

    def pad_lanes(a, w=LANES):
        return jnp.pad(a, [(0, 0)] * (a.ndim - 1) + [(0, w - a.shape[-1])])

    def rows_T(real, meta=None):
        n = real.shape[1]
        parts = [real]
        used = 0
        if meta is not None:
            parts.append(meta)
            used = meta.shape[0]
        if T - NX - used:
            parts.append(jnp.zeros((T - NX - used, n), real.dtype))
        return jnp.concatenate(parts, axis=0)

    def put_meta(tok, meta_per_seq):
        return lax.dynamic_update_slice(tok, meta_per_seq.sum(0), (NX, 0))

    h0 = rows_T(x.reshape(NX, D), meta_full)
    tgt = rows_T(loss_target.reshape(NX, D))

    def ffn_fwd(h, norm, w_gu, tag, behind_up=None, behind_down=None):
        u = _norm_fwd(h, 0, D, D, norm, D, D, tag + "_norm")
        (sa, sb, a), got = _ffn_up(u, w_gu, tag + "_up",
                                  exch=_Exchange(shards(behind_up), True) if behind_up else None)
        assemble(behind_up or [], got)
        h_out = _mm(a, full[tag + "_w_down"], "nn", tag + "_down", scale=0.5, res=h,
                    exch=_Exchange(shards(behind_down), True) if behind_down else None)
        if behind_down:
            h_out, got = h_out
            assemble(behind_down, got)
        return h_out, (u, sa, sb, a)

    h1, ffn1_saved = ffn_fwd(h0, W["ffn1_norm"], full["ffn1_w_gu"], "ffn1",
                             behind_up=["ffn1_w_down"], behind_down=["w_in"])
    wi = full["w_in"]
    o_fq = 0
    o_f = 3 * FOX_W
    o_cq = o_f + HEADS
    o_kr = o_cq + MLA_Q_RANK + MLA_KV_RANK
    o_g = o_kr + MLA_ROPE
    w_in_p = jnp.concatenate([
        wi[:, o_g:o_g + 2 * D], wi[:, o_fq:o_f], wi[:, o_cq:o_kr],
        jnp.pad(wi[:, o_f:o_cq], ((0, 0), (0, LANES - HEADS))),
        jnp.pad(wi[:, o_kr:o_g], ((0, 0), (0, LANES - MLA_ROPE)))], axis=1)

    u2 = _norm_fwd(h1, 0, D, D, W["mix_norm"], D, D, "mix_norm")
    z, got = _mm(u2, w_in_p, "nn", "w_in", exch=_Exchange(shards(mix_small), True))
    assemble(mix_small, got)
    w_uq_p = jnp.pad(full["mla_w_uq"].reshape(MLA_Q_RANK, H, MLA_QK),
                     ((0, 0), (0, 0), (0, LANES - MLA_QK))).reshape(MLA_Q_RANK, H * LANES)

    gq_f = jnp.tile(W["fox_q_norm"], (1, 2))
    gk_f = jnp.tile(W["fox_k_norm"], (1, 2))
    fqn = _norm_fwd(z, Z_FQ, FOX_W, LANES, gq_f, FOX_HD, FOX_HD, "fox_q_norm")
    fkn = _norm_fwd(z, Z_FK, FOX_W, LANES, gk_f, FOX_HD, FOX_HD, "fox_k_norm")
    fl = z[:NX, Z_F:Z_F + LANES].reshape(B, S, LANES)
    flm = z[NX:, Z_F:Z_F + LANES]
    bf = pad_lanes(W["b_forget"])
    cum, cumm = _cum_fwd(fl, flm, bf, "forget_cum")
    TK = min(512, S)
    ck = cum[:, :, :H].transpose(0, 2, 1).reshape(B, H, S // TK, 1, TK)
    cmk = jnp.broadcast_to(cumm[:, :H].T[None, :, None, :], (B, H, 1, META_BLK))
    (o_fox, lse_fox), got = _attn_fwd(fqn, fkn, z, Z_FV, False, FOX_HD ** -0.5, S, NX, "fox_attn", ck, cmk,
                                      exch=_Exchange(shards(last_group), True))
    assemble(last_group, got)
    of = _mm(o_fox, full["w_branch_fox"], "nn", "branch_fox")

    pos = jnp.concatenate([jnp.tile(jnp.arange(S) + N_META, B), jnp.arange(META_BLK)])
    tabs = _rope_tables(pos)
    cqn = _norm_fwd(z, Z_CQ, MLA_Q_RANK, MLA_Q_RANK, W["mla_cq_norm"], MLA_Q_RANK, MLA_Q_RANK, "mla_cq_norm")
    q_lin = _mm(cqn, w_uq_p, "nn", "mla_uq")
    ckvn = _norm_fwd(z, Z_CKV, MLA_KV_RANK, MLA_KV_RANK, W["mla_ckv_norm"], MLA_KV_RANK, MLA_KV_RANK,
                     "mla_ckv_norm")
    kv_lin = _mm(ckvn, full["mla_w_ukv"], "nn", "mla_ukv")
    gq_m, gk_m = pad_lanes(W["mla_q_norm"]), pad_lanes(W["mla_k_norm"])
    mqn = _norm_fwd(q_lin, 0, H * LANES, LANES, gq_m, LANES, MLA_QK, "mla_q_norm", tabs=tabs)
    mkn = _mla_k_fwd(kv_lin, z, Z_KR, gk_m, tabs, "mla_k_norm")
    (o_mla, lse_mla), _ = _attn_fwd(mqn, mkn, kv_lin, 0, True, MLA_QK ** -0.5, S, NX, "mla_attn")
    om = _mm(o_mla, full["w_branch_mla"], "nn", "branch_mla")

    mix = _gate_fwd(z, W["b_gate"], of, om, "gate_mix")
    h2 = _mm(mix, full["w_out"], "nn", "w_out", res=h1)

    h3, ffn2_saved = ffn_fwd(h2, W["ffn2_norm"], full["ffn2_w_gu"], "ffn2")

    dh3, loss_acc = _loss(h3, tgt, NX, "loss")
    loss = lax.psum(loss_acc[0, 0], AXES)

    G = {}
    parts = {}

    def scatter_of(group):
        per_dest = []
        for n in group:
            r, c = W[n].shape[1:]
            per_dest.append((G[n].reshape(r, N_DEV, c).transpose(1, 0, 2) if axis_of[n] == 1
                             else G[n].reshape(N_DEV, r, c)).astype(BF16))
        return _Exchange(per_dest, False)

    def ffn_bwd(dh, h, norm, w_gu, w_down, saved, tag, behind_down=None, spread=False):
        u, sa, sb, a = saved
        G[tag + "_w_down"] = _mm(a, dh, "tn", tag + "_dw_down", scale=0.5)
        (dg, dup), got = _ffn_down_bwd(dh, w_down, sa, sb, tag + "_down_bwd",
                                       exch=scatter_of(behind_down) if behind_down else None)
        parts.update(zip(behind_down or [], got))
        dw_g = _mm(u, dg, "tn", tag + "_dw_g", exch=scatter_of([tag + "_w_down"]) if spread else None)
        if spread:
            dw_g, got = dw_g
            parts[tag + "_w_down"] = got[0]
        G[tag + "_w_gu"] = jnp.concatenate([dw_g, _mm(u, dup, "tn", tag + "_dw_u")], axis=1)
        (du,), got = _ffn_up_bwd_dx(dg, dup, w_gu, tag + "_up_bwd",
                                    exch=scatter_of([tag + "_w_gu"]) if spread else None)
        if spread:
            parts[tag + "_w_gu"] = got[0]
        dh_in, G[tag + "_norm"] = _norm_bwd(h, 0, D, D, norm, D, D, du, tag + "_norm_bwd", res=dh)
        return dh_in

    dh2 = ffn_bwd(dh3, h2, W["ffn2_norm"], full["ffn2_w_gu"], full["ffn2_w_down"], ffn2_saved, "ffn2")

    G["w_out"] = _mm(mix, dh2, "tn", "dw_out")
    dmix = _mm(dh2, full["w_out"], "nt", "w_out_bwd")
    dgl, dof, dom, G["b_gate"] = _gate_bwd(dmix, z, W["b_gate"], of, om, "gate_bwd")

    G["w_branch_fox"] = _mm(o_fox, dof, "tn", "dw_branch_fox")
    do_fox = _mm(dof, full["w_branch_fox"], "nt", "branch_fox_bwd")
    (dq_f, dk_f, dv_f, dkm_f, dvm_f, dck, dcmk, dcq), got = _attn_bwd(
        fqn, fkn, z, Z_FV, o_fox, lse_fox, do_fox, False, FOX_HD ** -0.5, S, NX, "fox_attn_bwd", ck, cmk,
        exch=scatter_of(last_group))
    parts.update(zip(last_group, got))
    dk_f, dv_f = put_meta(dk_f, dkm_f), put_meta(dv_f, dvm_f)
    dfq, gq = _norm_bwd(z, Z_FQ, FOX_W, LANES, gq_f, FOX_HD, FOX_HD, dq_f, "fox_q_norm_bwd", out_dtype=BF16)
    dfk, gk = _norm_bwd(z, Z_FK, FOX_W, LANES, gk_f, FOX_HD, FOX_HD, dk_f, "fox_k_norm_bwd", out_dtype=BF16)
    G["fox_q_norm"] = gq[:, :FOX_HD] + gq[:, FOX_HD:]
    G["fox_k_norm"] = gk[:, :FOX_HD] + gk[:, FOX_HD:]
    dc = pad_lanes(dck.reshape(B, H, S).transpose(0, 2, 1)
                   + dcq.transpose(0, 2, 1, 3).reshape(B, S, H))
    dcm = pad_lanes(dcmk.sum(0)[:, 0, :].T)
    dcm = jnp.where(jnp.arange(LANES)[:, None] < N_META, dcm, 0.0)
    dfl, dflm, dbf = _cum_bwd(dc, dcm, fl, flm, bf, "forget_cum_bwd")
    G["b_forget"] = dbf[:, :HEADS]
    dfl_t = rows_T(dfl.reshape(NX, LANES), dflm)

    G["w_branch_mla"] = _mm(o_mla, dom, "tn", "dw_branch_mla")
    do_mla = _mm(dom, full["w_branch_mla"], "nt", "branch_mla_bwd")
    (dq_m, dk_m, dvk, dkm_m, dvkm), _ = _attn_bwd(
        mqn, mkn, kv_lin, 0, o_mla, lse_mla, do_mla, True, MLA_QK ** -0.5, S, NX, "mla_attn_bwd")
    dk_m, dvk = put_meta(dk_m, dkm_m), put_meta(dvk, dvkm)
    dq_lin, gq = _norm_bwd(q_lin, 0, H * LANES, LANES, gq_m, LANES, MLA_QK, dq_m, "mla_q_norm_bwd", tabs=tabs)
    G["mla_q_norm"] = gq[:, :MLA_QK]
    G["mla_w_uq"] = _mm(cqn, dq_lin, "tn", "dw_uq").reshape(MLA_Q_RANK, H, LANES)[:, :, :MLA_QK].reshape(
        MLA_Q_RANK, H * MLA_QK)
    dcqn = _mm(dq_lin, w_uq_p, "nt", "mla_uq_bwd")
    dcq, G["mla_cq_norm"] = _norm_bwd(z, Z_CQ, MLA_Q_RANK, MLA_Q_RANK, W["mla_cq_norm"], MLA_Q_RANK,
                                      MLA_Q_RANK, dcqn, "mla_cq_norm_bwd", out_dtype=BF16)
    dkv_lin, dkr, gk = _mla_k_bwd(kv_lin, z, Z_KR, gk_m, tabs, dk_m, dvk, "mla_k_norm_bwd")
    G["mla_k_norm"] = gk[:, :MLA_QK]
    G["mla_w_ukv"] = _mm(ckvn, dkv_lin, "tn", "dw_ukv")
    dckvn = _mm(dkv_lin, full["mla_w_ukv"], "nt", "mla_ukv_bwd")
    dckv, G["mla_ckv_norm"] = _norm_bwd(z, Z_CKV, MLA_KV_RANK, MLA_KV_RANK, W["mla_ckv_norm"], MLA_KV_RANK,
                                        MLA_KV_RANK, dckvn, "mla_ckv_norm_bwd", out_dtype=BF16)

    dz = jnp.concatenate([dgl, dfq, dfk, dv_f.astype(BF16), dcq, dckv, dfl_t.astype(BF16),
                          dkr.astype(BF16)], axis=1)
    dw_in_p = _mm(u2, dz, "tn", "dw_in")
    G["w_in"] = jnp.concatenate([
        dw_in_p[:, Z_FQ:Z_CQ], dw_in_p[:, Z_F:Z_F + HEADS], dw_in_p[:, Z_CQ:Z_F],
        dw_in_p[:, Z_KR:Z_KR + MLA_ROPE], dw_in_p[:, Z_G:Z_G + 2 * D]], axis=1)
    du2, got = _mm(dz, w_in_p, "nt", "w_in_bwd", exch=scatter_of(mix_small))
    parts.update(zip(mix_small, got))
    dh1, G["mix_norm"] = _norm_bwd(h1, 0, D, D, W["mix_norm"], D, D, du2, "mix_norm_bwd", res=dh2)

    dh0 = ffn_bwd(dh1, h0, W["ffn1_norm"], full["ffn1_w_gu"], full["ffn1_w_down"], ffn1_saved, "ffn1",
                  behind_down=["w_in"], spread=True)
    grad_x = dh0[:NX].reshape(B, S, D)
    G["meta_tokens"] = dh0[NX:NX + N_META]

    res = {}
    for n, _ in big:
        outs4 = _adamw(parts[n], W[n][0], Mo[n][0], Vo[n][0], "adamw_" + n)
        for key, arr in zip(("g", "d", "m", "v"), outs4):
            res[key, n] = arr[None]

    small = [n for n in names if n not in dict(big) and n != "meta_tokens"]
    small_shapes = [W[n].shape for n in small]
    spack = _pack([G["meta_tokens"]] + [G[n] for n in small], 1024, 8)
    (sparts,) = _exchange([spack], "gather_small_grads", gather=True)
    sflat = sparts.reshape(N_DEV, -1)
    dsh = D // N_DEV
    meta_part = lax.dynamic_slice(sflat[:, :N_META * D].reshape(N_DEV, N_META, D),
                                  (0, 0, me * dsh), (N_DEV, N_META, dsh)).reshape(N_DEV, -1)
    rep_len = sum(int(np.prod(s)) for s in small_shapes)
    rep_part = sflat[:, N_META * D:N_META * D + rep_len]
    sp = _pack_rows([meta_part, rep_part], LANES, _chunk_rows(LANES))
    pks = lambda src: _pack([src["meta_tokens"]] + [src[n] for n in small], LANES, _chunk_rows(LANES))
    g_s, d_s, m_s, v_s = _adamw(sp, pks(W), pks(Mo), pks(Vo), "adamw_small")
    shapes_s = [W["meta_tokens"].shape] + small_shapes
    for key, packed in (("g", g_s), ("d", d_s), ("m", m_s), ("v", v_s)):
        for n, arr in zip(["meta_tokens"] + small, _unpack(packed, shapes_s)):
            res[key, n] = arr

    outs = [loss, grad_x]
    for key in ("g", "d", "m", "v"):
        outs += [res[key, n] for n in names]
    return tuple(outs)
```

```python
import numpy as np
import jax
import jax.numpy as jnp
from jax import lax
from jax.experimental import pallas as pl
from jax.experimental.pallas import tpu as pltpu

F32 = jnp.float32
BF16 = jnp.bfloat16

N_META = 16
EPS = 1e-6
HEADS = 8
FOX_HD = 64
FOX_W = HEADS * FOX_HD
MLA_Q_RANK = 256
MLA_KV_RANK = 128
MLA_NOPE = 64
MLA_ROPE = 32
MLA_QK = MLA_NOPE + MLA_ROPE
MLA_V = 64
ROPE_THETA = 10000.0
LANES = 128
HALF = LANES // 2
META_BLK = 128
NEG = -1e30

ADAM_LR = 0.001
ADAM_B1 = 0.9
ADAM_B2 = 0.999
ADAM_EPS = 1e-08
ADAM_WD = 0.01
ADAM_STEP = 10

N_DEV = 8
AXES = ("x", "y", "c")
VMEM_LIMIT_BYTES = 56 * 1024 * 1024


def _tile(n, cap, mult):
    best = None
    for d in range(mult, min(n, cap) + 1, mult):
        if n % d == 0:
            best = d
    return n if best is None else best


VREG_ELEMS = 8 * LANES


def _row_tile(rows, width):
    return _tile(rows, max(16, (1 << 19) // width), 16)


def _chunk_rows(width):
    rows = 16
    while 2 * rows * width <= 8 * VREG_ELEMS:
        rows *= 2
    return rows


def _by_chunks(rows, width, step, init=()):
    return step(pl.ds(0, rows), init)


def _params(sem=None):
    return pltpu.CompilerParams(dimension_semantics=sem, vmem_limit_bytes=VMEM_LIMIT_BYTES)


def _mm(a, b, mode, name, out_dtype=F32, scale=1.0, res=None, exch=None):
    if mode == "nn":
        (M, K), (K2, N) = a.shape, b.shape
    elif mode == "nt":
        (M, K), (N, K2) = a.shape, b.shape
    else:
        (K, M), (K2, N) = a.shape, b.shape
    assert K == K2, (a.shape, b.shape, mode)
    if mode == "tn":
        tm, tk = _tile(M, 1408, 128), _tile(K, 2080, 16)
    else:
        tm, tk = _tile(M, 640, 16), _tile(K, 4224, 128)
    tn = _tile(N, 1408, 128)
    nk = K // tk
    ni, nj = M // tm, N // tn
    bytes_a, bytes_b = a.size * a.dtype.itemsize, b.size * b.dtype.itemsize
    j_outer = nk == 1 and bytes_a * nj + bytes_b < bytes_a + bytes_b * ni
    ij = (lambda g0, g1: (g1, g0)) if j_outer else (lambda g0, g1: (g0, g1))

    def spec(shape, at):
        return pl.BlockSpec(shape, lambda g0, g1, k: at(*ij(g0, g1), k))

    a_spec = {"nn": spec((tm, tk), lambda i, j, k: (i, k)),
              "nt": spec((tm, tk), lambda i, j, k: (i, k)),
              "tn": spec((tk, tm), lambda i, j, k: (k, i))}[mode]
    b_spec = {"nn": spec((tk, tn), lambda i, j, k: (k, j)),
              "nt": spec((tn, tk), lambda i, j, k: (j, k)),
              "tn": spec((tk, tn), lambda i, j, k: (k, j))}[mode]
    dims = {"nn": (((1,), (0,)), ((), ())), "nt": (((1,), (1,)), ((), ())),
            "tn": (((0,), (0,)), ((), ()))}[mode]
    o_spec = spec((tm, tn), lambda i, j, k: (i, j))
    has_res = res is not None

    def body(*refs):
        a_ref, b_ref = refs[:2]
        r_ref = refs[2] if has_res else None
        o_ref = refs[2 + has_res]

        def finish(acc):
            o = acc * scale
            if has_res:
                o = o + r_ref[...]
            o_ref[...] = o.astype(out_dtype)

        prod = lax.dot_general(a_ref[...].astype(BF16), b_ref[...].astype(BF16), dims,
                               preferred_element_type=F32)
        if nk == 1:
            finish(prod)
            return
        acc_ref = refs[3 + has_res]
        k = pl.program_id(2)

        @pl.when(k == 0)
        def _():
            acc_ref[...] = prod

        @pl.when((k > 0) & (k < nk - 1))
        def _():
            acc_ref[...] += prod

        @pl.when(k == nk - 1)
        def _():
            finish(acc_ref[...] + prod)

    ins = [a, b] + ([res] if has_res else [])
    specs = [a_spec, b_spec] + ([o_spec] if has_res else [])
    (out,), got = _call(
        body, name, (nj, ni, nk) if j_outer else (ni, nj, nk), specs, [o_spec],
        [jax.ShapeDtypeStruct((M, N), out_dtype)],
        ins, scratch_shapes=[pltpu.VMEM((tm, tn), F32)] if nk > 1 else [],
        sem=("parallel", "parallel", "arbitrary"), exch=exch)
    return out if exch is None else (out, got)


def _rope_fwd(y, c, s1, s2):
    return y * c + pltpu.roll(y, LANES - 16, 1) * s1 + pltpu.roll(y, 16, 1) * s2


def _rope_bwd(dy, c, s1, s2):
    return dy * c + pltpu.roll(dy * s1, 16, 1) + pltpu.roll(dy * s2, LANES - 16, 1)


def _group_sum(v, seg):
    if seg == v.shape[-1]:
        return jnp.sum(v, axis=-1, keepdims=True)
    lo = lax.broadcasted_iota(jnp.int32, v.shape, 1) < seg
    s_lo = jnp.sum(jnp.where(lo, v, 0.0), axis=-1, keepdims=True)
    s_hi = jnp.sum(jnp.where(lo, 0.0, v), axis=-1, keepdims=True)
    return jnp.where(lo, s_lo, s_hi)


def _norm_fwd(src, col0, width, bw, gain, seg, d_true, name, tabs=None, out_dtype=BF16):
    T = src.shape[0]
    tr = _row_tile(T, bw)
    inv_d = 1.0 / d_true
    c0 = col0 // bw
    assert col0 % bw == 0 and width % bw == 0

    def body(*refs):
        if tabs is None:
            x_ref, g_ref, o_ref = refs
        else:
            x_ref, g_ref, c_ref, s1_ref, s2_ref, o_ref = refs
        gain_v = g_ref[...]

        def step(rows, carry):
            xv = x_ref[rows, :]
            r = lax.rsqrt(_group_sum(xv * xv, seg) * inv_d + EPS)
            y = xv * r * gain_v
            if tabs is not None:
                y = _rope_fwd(y, c_ref[rows, :], s1_ref[rows, :], s2_ref[rows, :])
            o_ref[rows, :] = y.astype(out_dtype)
            return carry

        _by_chunks(tr, bw, step)

    specs = [pl.BlockSpec((tr, bw), lambda i, j: (i, c0 + j)), pl.BlockSpec((1, bw), lambda i, j: (0, 0))]
    ins = [src, gain]
    if tabs is not None:
        tab = pl.BlockSpec((tr, LANES), lambda i, j: (i, 0))
        specs += [tab, tab, tab]
        ins += list(tabs)
    return pl.pallas_call(
        body, name=name, grid=(T // tr, width // bw), in_specs=specs,
        out_specs=pl.BlockSpec((tr, bw), lambda i, j: (i, j)),
        out_shape=jax.ShapeDtypeStruct((T, width), out_dtype),
        compiler_params=_params(("parallel", "parallel")),
    )(*ins)


def _norm_bwd_math(xv, gain, dyv, seg, inv_d):
    r = lax.rsqrt(_group_sum(xv * xv, seg) * inv_d + EPS)
    gy = dyv * gain
    dot = _group_sum(gy * xv, seg)
    dx = r * gy - xv * (r * r * r * inv_d) * dot
    return dx, jnp.sum(dyv * xv * r, axis=0, keepdims=True)


def _norm_bwd(src, col0, width, bw, gain, seg, d_true, dy, name, tabs=None, res=None, out_dtype=F32):
    T = src.shape[0]
    tr = _row_tile(T, bw)
    inv_d = 1.0 / d_true
    c0 = col0 // bw
    has_res = res is not None

    def body(*refs):
        refs = list(refs)
        x_ref, g_ref, dy_ref = refs[:3]
        pos = 3
        if tabs is not None:
            c_ref, s1_ref, s2_ref = refs[3:6]
            pos = 6
        if has_res:
            r_ref = refs[pos]
            pos += 1
        dx_ref, dg_ref = refs[pos], refs[pos + 1]
        gain_v = g_ref[...]

        def step(rows, dg_sum):
            dyv = dy_ref[rows, :].astype(F32)
            if tabs is not None:
                dyv = _rope_bwd(dyv, c_ref[rows, :], s1_ref[rows, :], s2_ref[rows, :])
            dx, dg = _norm_bwd_math(x_ref[rows, :], gain_v, dyv, seg, inv_d)
            if has_res:
                dx = dx + r_ref[rows, :]
            dx_ref[rows, :] = dx.astype(out_dtype)
            return dg_sum + dg

        dg = _by_chunks(tr, bw, step, jnp.zeros((1, bw), F32))

        @pl.when((pl.program_id(0) == 0) & (pl.program_id(1) == 0))
        def _():
            dg_ref[...] = jnp.zeros_like(dg_ref)

        dg_ref[...] += dg

    blk = pl.BlockSpec((tr, bw), lambda i, j: (i, j))
    one = pl.BlockSpec((1, bw), lambda i, j: (0, 0))
    specs = [pl.BlockSpec((tr, bw), lambda i, j: (i, c0 + j)), one, blk]
    ins = [src, gain, dy]
    if tabs is not None:
        tab = pl.BlockSpec((tr, LANES), lambda i, j: (i, 0))
        specs += [tab, tab, tab]
        ins += list(tabs)
    if has_res:
        specs.append(blk)
        ins.append(res)
    return pl.pallas_call(
        body, name=name, grid=(T // tr, width // bw), in_specs=specs, out_specs=(blk, one),
        out_shape=(jax.ShapeDtypeStruct((T, width), out_dtype), jax.ShapeDtypeStruct((1, bw), F32)),
        compiler_params=_params(("arbitrary", "arbitrary")),
    )(*ins)


def _mla_k_raw(kv, kr):
    lane = lax.broadcasted_iota(jnp.int32, kv.shape, 1)
    return jnp.where(lane < MLA_NOPE, kv, jnp.where(lane < MLA_QK, pltpu.roll(kr, MLA_NOPE, 1), 0.0))


def _mla_k_fwd(kv_lin, z, kr_col, gain, tabs, name):
    T, W = kv_lin.shape
    tr = _row_tile(T, LANES)
    krb = kr_col // LANES
    inv_d = 1.0 / MLA_QK

    def body(kv_ref, kr_ref, g_ref, c_ref, s1_ref, s2_ref, o_ref):
        gain_v = g_ref[...]

        def step(rows, carry):
            xv = _mla_k_raw(kv_ref[rows, :], kr_ref[rows, :])
            r = lax.rsqrt(jnp.sum(xv * xv, axis=-1, keepdims=True) * inv_d + EPS)
            o_ref[rows, :] = _rope_fwd(xv * r * gain_v, c_ref[rows, :], s1_ref[rows, :],
                                       s2_ref[rows, :]).astype(BF16)
            return carry

        _by_chunks(tr, LANES, step)

    blk = pl.BlockSpec((tr, LANES), lambda i, h: (i, h))
    tab = pl.BlockSpec((tr, LANES), lambda i, h: (i, 0))
    return pl.pallas_call(
        body, name=name, grid=(T // tr, W // LANES),
        in_specs=[blk, pl.BlockSpec((tr, LANES), lambda i, h: (i, krb)),
                  pl.BlockSpec((1, LANES), lambda i, h: (0, 0)), tab, tab, tab],
        out_specs=blk, out_shape=jax.ShapeDtypeStruct((T, W), BF16),
        compiler_params=_params(("parallel", "parallel")),
    )(kv_lin, z, gain, *tabs)


def _mla_k_bwd(kv_lin, z, kr_col, gain, tabs, dk, dvk, name):
    T, W = kv_lin.shape
    tr = _row_tile(T, LANES)
    krb = kr_col // LANES
    inv_d = 1.0 / MLA_QK

    def body(kv_ref, kr_ref, g_ref, c_ref, s1_ref, s2_ref, dk_ref, dvk_ref, dkv_ref, dkr_ref, dg_ref):
        h = pl.program_id(1)
        gain_v = g_ref[...]

        @pl.when(h == 0)
        def _():
            dkr_ref[...] = jnp.zeros_like(dkr_ref)

        def step(rows, dg_sum):
            xv = _mla_k_raw(kv_ref[rows, :], kr_ref[rows, :])
            dyv = _rope_bwd(dk_ref[rows, :], c_ref[rows, :], s1_ref[rows, :], s2_ref[rows, :])
            dx, dg = _norm_bwd_math(xv, gain_v, dyv, LANES, inv_d)
            lane = lax.broadcasted_iota(jnp.int32, dx.shape, 1)
            dkv_ref[rows, :] = jnp.where(lane < MLA_NOPE, dx, dvk_ref[rows, :])
            dkr_ref[rows, :] += pltpu.roll(jnp.where((lane >= MLA_NOPE) & (lane < MLA_QK), dx, 0.0),
                                           LANES - MLA_NOPE, 1)
            return dg_sum + dg

        dg = _by_chunks(tr, LANES, step, jnp.zeros((1, LANES), F32))

        @pl.when((pl.program_id(0) == 0) & (h == 0))
        def _():
            dg_ref[...] = jnp.zeros_like(dg_ref)

        dg_ref[...] += dg

    blk = pl.BlockSpec((tr, LANES), lambda i, h: (i, h))
    tab = pl.BlockSpec((tr, LANES), lambda i, h: (i, 0))
    one = pl.BlockSpec((1, LANES), lambda i, h: (0, 0))
    return pl.pallas_call(
        body, name=name, grid=(T // tr, W // LANES),
        in_specs=[blk, pl.BlockSpec((tr, LANES), lambda i, h: (i, krb)), one, tab, tab, tab, blk, blk],
        out_specs=(blk, tab, one),
        out_shape=(jax.ShapeDtypeStruct((T, W), F32), jax.ShapeDtypeStruct((T, LANES), F32),
                   jax.ShapeDtypeStruct((1, LANES), F32)),
        compiler_params=_params(("arbitrary", "arbitrary")),
    )(kv_lin, z, gain, *tabs, dk, dvk)


def _ffn_up(u, w_gu, name, exch=None):
    T, D = u.shape
    F = w_gu.shape[1] // 2
    tm, tn = _tile(T, 640, 16), _tile(F, 1408, 128)
    nj = F // tn

    def body(u_ref, wg_ref, wu_ref, sa_ref, sb_ref, a_ref):
        uv = u_ref[...]
        g = jnp.dot(uv, wg_ref[...], preferred_element_type=F32)
        up = jnp.dot(uv, wu_ref[...], preferred_element_type=F32)
        sg = jax.nn.sigmoid(g)
        silu = g * sg
        sa_ref[...] = silu.astype(BF16)
        sb_ref[...] = (up * (sg + silu * (1.0 - sg))).astype(BF16)
        a_ref[...] = (silu * up).astype(BF16)

    o_spec = pl.BlockSpec((tm, tn), lambda j, i: (i, j))
    sh = jax.ShapeDtypeStruct((T, F), BF16)
    return _call(
        body, name, (nj, T // tm),
        [pl.BlockSpec((tm, D), lambda j, i: (i, 0)),
         pl.BlockSpec((D, tn), lambda j, i: (0, j)),
         pl.BlockSpec((D, tn), lambda j, i: (0, j + nj))],
        [o_spec, o_spec, o_spec], [sh, sh, sh], [u, w_gu, w_gu],
        sem=("parallel", "parallel"), exch=exch)


def _ffn_down_bwd(dh, w_down, sa, sb, name, exch=None):
    T, D = dh.shape
    F = w_down.shape[0]
    tm, tn = _tile(T, 640, 16), _tile(F, 1408, 128)

    def body(dh_ref, w_ref, sa_ref, sb_ref, dg_ref, dup_ref):
        da = 0.5 * lax.dot_general(dh_ref[...].astype(BF16), w_ref[...], (((1,), (1,)), ((), ())),
                                   preferred_element_type=F32)
        dup_ref[...] = (da * sa_ref[...].astype(F32)).astype(BF16)
        dg_ref[...] = (da * sb_ref[...].astype(F32)).astype(BF16)

    t_spec = pl.BlockSpec((tm, tn), lambda j, i: (i, j))
    sh = jax.ShapeDtypeStruct((T, F), BF16)
    return _call(
        body, name, (F // tn, T // tm),
        [pl.BlockSpec((tm, D), lambda j, i: (i, 0)),
         pl.BlockSpec((tn, D), lambda j, i: (j, 0)), t_spec, t_spec],
        [t_spec, t_spec], [sh, sh], [dh, w_down, sa, sb],
        sem=("parallel", "parallel"), exch=exch)


def _ffn_up_bwd_dx(dg, dup, w_gu, name, exch=None):
    T, F = dg.shape
    D = w_gu.shape[0]
    tm, tk = _tile(T, 640, 16), _tile(F, 2816, 128)
    nk = F // tk
    nt = (((1,), (1,)), ((), ()))

    def body(dg_ref, dup_ref, wg_ref, wu_ref, o_ref, acc_ref):
        k = pl.program_id(1)
        prod = (lax.dot_general(dg_ref[...], wg_ref[...], nt, preferred_element_type=F32)
                + lax.dot_general(dup_ref[...], wu_ref[...], nt, preferred_element_type=F32))
        if nk == 1:
            o_ref[...] = prod
            return

        @pl.when(k == 0)
        def _():
            acc_ref[...] = prod

        @pl.when((k > 0) & (k < nk - 1))
        def _():
            acc_ref[...] += prod

        @pl.when(k == nk - 1)
        def _():
            o_ref[...] = acc_ref[...] + prod

    return _call(
        body, name, (T // tm, nk),
        [pl.BlockSpec((tm, tk), lambda i, k: (i, k)),
         pl.BlockSpec((tm, tk), lambda i, k: (i, k)),
         pl.BlockSpec((D, tk), lambda i, k: (0, k)),
         pl.BlockSpec((D, tk), lambda i, k: (0, k + nk))],
        [pl.BlockSpec((tm, D), lambda i, k: (i, 0))], [jax.ShapeDtypeStruct((T, D), F32)],
        [dg, dup, w_gu, w_gu], scratch_shapes=[pltpu.VMEM((tm, D), F32)],
        sem=("parallel", "arbitrary"), exch=exch)


def _logsig(x):
    return jnp.minimum(x, 0.0) - jnp.log(1.0 + jnp.exp(-jnp.abs(x)))


def _cum_fwd(fl, flm, bf, name):
    B, S, _ = fl.shape
    nb = S // LANES

    def body(fl_ref, flm_ref, bf_ref, cum_ref, cumm_ref):
        rows = lax.broadcasted_iota(jnp.int32, (LANES, LANES), 0)
        cols = lax.broadcasted_iota(jnp.int32, (LANES, LANES), 1)
        tri = (rows >= cols).astype(F32)
        bias = bf_ref[...]
        lfm = jnp.where(rows < N_META, _logsig(flm_ref[...] + bias), 0.0)
        cm = jnp.dot(tri, lfm, precision=lax.Precision.HIGHEST, preferred_element_type=F32)
        cumm_ref[...] = cm * LOG2E
        base = cm[LANES - 1:LANES, :]
        for b in range(B):
            def blk(i, carry):
                r0 = pl.multiple_of(i * LANES, LANES)
                lf = _logsig(fl_ref[b, pl.ds(r0, LANES), :] + bias)
                c = jnp.dot(tri, lf, precision=lax.Precision.HIGHEST,
                            preferred_element_type=F32) + carry
                cum_ref[b, pl.ds(r0, LANES), :] = c * LOG2E
                return c[LANES - 1:LANES, :]

            lax.fori_loop(0, nb, blk, base)

    return pl.pallas_call(
        body, name=name,
        out_shape=(jax.ShapeDtypeStruct((B, S, LANES), F32),
                   jax.ShapeDtypeStruct((LANES, LANES), F32)),
        compiler_params=_params(),
    )(fl, flm, bf)


def _cum_bwd(dc, dcm, fl, flm, bf, name):
    B, S, _ = fl.shape
    nb = S // LANES

    def body(dc_ref, dcm_ref, fl_ref, flm_ref, bf_ref, dfl_ref, dflm_ref, dbf_ref):
        rows = lax.broadcasted_iota(jnp.int32, (LANES, LANES), 0)
        cols = lax.broadcasted_iota(jnp.int32, (LANES, LANES), 1)
        triu = (rows <= cols).astype(F32)
        bias = bf_ref[...]
        total = jnp.zeros((1, LANES), F32)
        dbf = jnp.zeros((1, LANES), F32)
        for b in range(B):
            tail = jnp.zeros((1, LANES), F32)
            for t in range(nb):
                r0 = (nb - 1 - t) * LANES
                rc = jnp.dot(triu, dc_ref[b, r0:r0 + LANES, :], precision=lax.Precision.HIGHEST,
                             preferred_element_type=F32) + tail
                xv = fl_ref[b, r0:r0 + LANES, :] + bias
                d = rc / (1.0 + jnp.exp(xv))
                dfl_ref[b, r0:r0 + LANES, :] = d
                tail = rc[0:1, :]
                dbf = dbf + jnp.sum(d, axis=0, keepdims=True)
            total = total + tail
        rcm = jnp.dot(triu, dcm_ref[...], precision=lax.Precision.HIGHEST,
                      preferred_element_type=F32) + total
        dm = jnp.where(rows < N_META, rcm / (1.0 + jnp.exp(flm_ref[...] + bias)), 0.0)
        dflm_ref[...] = dm
        dbf_ref[...] = dbf + jnp.sum(dm, axis=0, keepdims=True)

    return pl.pallas_call(
        body, name=name,
        out_shape=(jax.ShapeDtypeStruct((B, S, LANES), F32),
                   jax.ShapeDtypeStruct((LANES, LANES), F32),
                   jax.ShapeDtypeStruct((1, LANES), F32)),
        compiler_params=_params(),
    )(dc, dcm, fl, flm, bf)


_NT = (((1,), (1,)), ((), ()))


def _token_rows_buffer(T, NX, width):
    return lax.dynamic_update_slice(lax.empty((T, width), F32), jnp.zeros((T - NX, width), F32), (NX, 0))


def _attn_specs(S, NX, qw, v_col0):
    mb = NX // META_BLK
    vb = v_col0 // qw
    return (pl.BlockSpec((S, qw), lambda b, p: (b, p)),
            pl.BlockSpec((META_BLK, qw), lambda b, p: (mb, p)),
            pl.BlockSpec((S, qw), lambda b, p: (b, vb + p)),
            pl.BlockSpec((META_BLK, qw), lambda b, p: (mb, vb + p)),
            pl.BlockSpec((S, LANES), lambda b, p: (b, p)))


def _cum_specs(S, TK):
    return [pl.BlockSpec((1, 2, S // TK, 1, TK), lambda b, p: (b, p, 0, 0, 0)),
            pl.BlockSpec((1, 2, 1, META_BLK), lambda b, p: (b, p, 0, 0))]


LOG2E = 1.4426950408889634


def _attn_fwd(qn, kn, vsrc, v_col0, mla, scale, S, NX, name, ck=None, cmk=None, exch=None):
    T = qn.shape[0]
    B = NX // S
    qw = 2 * LANES if mla else LANES
    npair = qn.shape[1] // qw
    TQ = min(512, S)
    TK = TQ
    forget = ck is not None
    a = scale * LOG2E

    def body(*refs):
        if forget:
            q_ref, k_ref, km_ref, v_ref, vm_ref, ck_ref, cmk_ref, _, o_ref, lse_ref = refs
        else:
            q_ref, k_ref, km_ref, v_ref, vm_ref, _, o_ref, lse_ref = refs
        lo = lax.broadcasted_iota(jnp.int32, (1, LANES), 1) < HALF
        mcol = lax.broadcasted_iota(jnp.int32, (TQ, META_BLK), 1)
        causal = (lax.broadcasted_iota(jnp.int32, (TQ, TK), 0)
                  >= lax.broadcasted_iota(jnp.int32, (TQ, TK), 1))
        two = lax.broadcasted_iota(jnp.int32, (TQ, 2), 1)
        for qi in range(S // TQ):
            q0 = qi * TQ
            sls = [slice(e * LANES, (e + 1) * LANES) if mla else slice(None) for e in range(2)]
            if mla:
                qts = [q_ref[q0:q0 + TQ, sl] for sl in sls]
            else:
                qts = [jnp.where(lo if e == 0 else ~lo, q_ref[q0:q0 + TQ, :], 0.0).astype(BF16)
                       for e in range(2)]

            def step(e, kt, vt, c2, mask, carry):
                m, l, acc = carry
                s = lax.dot_general(qts[e], kt, _NT, preferred_element_type=F32) * a
                if forget:
                    s = s - c2
                if mask is not None:
                    s = jnp.where(mask, s, NEG)
                m2 = jnp.max(s, axis=1, keepdims=True)
                if m is not None:
                    m2 = jnp.maximum(m, m2)
                p = jnp.exp2(s - m2)
                l2 = jnp.sum(p, axis=1, keepdims=True)
                acc2 = jnp.dot(p.astype(BF16), vt.astype(BF16), preferred_element_type=F32)
                if m is not None:
                    alpha = jnp.exp2(m - m2)
                    l2, acc2 = alpha * l + l2, alpha * acc + acc2
                return m2, l2, acc2

            def both(rows, kj, mask, carry):
                return tuple(step(e, k_ref[rows, sls[e]], v_ref[rows, sls[e]],
                                  ck_ref[0, e, kj] if forget else None, mask, carry[e]) for e in range(2))

            def below(kj, carry):
                return both(pl.ds(pl.multiple_of(kj * TK, TK), TK), kj, None, carry)

            carry = tuple(step(e, km_ref[:, sls[e]], vm_ref[:, sls[e]], cmk_ref[0, e] if forget else None,
                               mcol < N_META, (None, None, None)) for e in range(2))
            if qi:
                carry = lax.fori_loop(0, qi, below, carry)
            carry = both(slice(q0, q0 + TK), qi, causal, carry)
            outs = [acc / l for _, l, acc in carry]
            lses = [m + jnp.log2(l) for m, l, _ in carry]
            first = pltpu.roll(outs[0], HALF, 1) if mla else outs[0]
            o_ref[q0:q0 + TQ, :] = jnp.where(lo, first, outs[1])
            lse_ref[0, 0, q0:q0 + TQ, :] = jnp.where(two == 0, lses[0], lses[1])

    qk, kmeta, vv, vmeta, pair = _attn_specs(S, NX, qw, v_col0)
    specs = [qk, qk, kmeta, vv, vmeta]
    ins = [qn, kn, kn, vsrc, vsrc]
    if forget:
        specs += _cum_specs(S, TK)
        ins += [ck, cmk]
    specs.append(pl.BlockSpec(memory_space=pl.ANY))
    ins.append(_token_rows_buffer(T, NX, npair * LANES))
    lse_spec = pl.BlockSpec((1, 1, S, 2), lambda b, p: (b, p, 0, 0))
    return _call(
        body, name, (B, npair), specs, [pair, lse_spec],
        [jax.ShapeDtypeStruct((T, npair * LANES), F32), jax.ShapeDtypeStruct((B, npair, S, 2), F32)],
        ins, sem=("parallel", "parallel"), aliases={len(ins) - 1: 0}, exch=exch)


def _attn_bwd(qn, kn, vsrc, v_col0, o, lse, do, mla, scale, S, NX, name, ck=None, cmk=None, exch=None):
    T, W = qn.shape
    B = NX // S
    qw = 2 * LANES if mla else LANES
    npair = W // qw
    TQ = min(512, S)
    TK = TQ
    forget = ck is not None
    a = scale * LOG2E
    _TN = (((0,), (0,)), ((), ()))

    def body(*refs):
        refs = list(refs)
        q_ref, k_ref, km_ref, v_ref, vm_ref, o_ref, do_ref, lse_ref = refs[:8]
        pos = 8
        if forget:
            ck_ref, cmk_ref = refs[8:10]
            pos = 10
        pos += 3
        dq_ref, dk_ref, dv_ref, dkm_ref, dvm_ref = refs[pos:pos + 5]
        if forget:
            dck_ref, dcm_ref, dcq_ref = refs[pos + 5:pos + 8]
            dck_ref[...] = jnp.zeros_like(dck_ref)
            dcm_ref[...] = jnp.zeros_like(dcm_ref)
        dk_ref[...] = jnp.zeros_like(dk_ref)
        dv_ref[...] = jnp.zeros_like(dv_ref)
        dkm_ref[...] = jnp.zeros_like(dkm_ref)
        dvm_ref[...] = jnp.zeros_like(dvm_ref)
        lo = lax.broadcasted_iota(jnp.int32, (1, LANES), 1) < HALF
        mcol = lax.broadcasted_iota(jnp.int32, (TQ, META_BLK), 1)
        causal = (lax.broadcasted_iota(jnp.int32, (TQ, TK), 0)
                  >= lax.broadcasted_iota(jnp.int32, (TQ, TK), 1))
        two = lax.broadcasted_iota(jnp.int32, (TQ, 2), 1)
        for qi in range(S // TQ):
            q0 = qi * TQ
            dof = do_ref[q0:q0 + TQ, :]
            prod = dof * o_ref[q0:q0 + TQ, :]
            lse2 = lse_ref[0, 0, q0:q0 + TQ, :]
            sls = [slice(e * LANES, (e + 1) * LANES) if mla else slice(None) for e in range(2)]
            mine = [lo, ~lo]
            if mla:
                qts = [q_ref[q0:q0 + TQ, sl] for sl in sls]
                dots = [jnp.where(lo, 0.0, pltpu.roll(dof, HALF, 1) if e == 0 else dof).astype(BF16)
                        for e in range(2)]
            else:
                qts = [jnp.where(mine[e], q_ref[q0:q0 + TQ, :], 0.0).astype(BF16) for e in range(2)]
                dots = [jnp.where(mine[e], dof, 0.0).astype(BF16) for e in range(2)]
            deltas = [jnp.sum(jnp.where(mine[e], prod, 0.0), axis=1, keepdims=True) for e in range(2)]
            lse_ts = [jnp.sum(jnp.where(two == e, lse2, 0.0), axis=1, keepdims=True) for e in range(2)]

            def grads(e, kt, vt, c2, mask):
                s = lax.dot_general(qts[e], kt, _NT, preferred_element_type=F32) * a
                if forget:
                    s = s - c2
                p = jnp.exp2(s - lse_ts[e])
                if mask is not None:
                    p = jnp.where(mask, p, 0.0)
                dp = lax.dot_general(dots[e], vt, _NT, preferred_element_type=F32)
                ds = p * (dp - deltas[e])
                dsb = ds.astype(BF16)
                return (jnp.dot(dsb, kt, preferred_element_type=F32),
                        lax.dot_general(dsb, qts[e], _TN, preferred_element_type=F32) * scale,
                        lax.dot_general(p.astype(BF16), dots[e], _TN, preferred_element_type=F32),
                        -jnp.sum(ds, axis=0, keepdims=True) if forget else None,
                        jnp.sum(ds, axis=1, keepdims=True) if forget else None)

            def block(k_at, v_at, dk_at, dv_at, c_at, dc_at, mask, dqs):
                got = [grads(e, k_at(sls[e]), v_at(sls[e]).astype(BF16), c_at(e) if forget else None, mask)
                       for e in range(2)]
                if mla:
                    for e in range(2):
                        dk_at(sls[e], got[e][1])
                        dv_at(sls[e], got[e][2])
                else:
                    dk_at(sls[0], got[0][1] + got[1][1])
                    dv_at(sls[0], got[0][2] + got[1][2])
                if forget:
                    for e in range(2):
                        dc_at(e, got[e][3])
                picks = (0, 0, 4, 4) if forget else (0, 0)
                new = tuple(got[i % 2][k] for i, k in enumerate(picks))
                return new if dqs is None else tuple(x + y for x, y in zip(dqs, new))

            def add_to(ref, *lead):
                def add(*idx_and_val):
                    *idx, val = idx_and_val
                    ref[(*lead, *idx)] += val
                return add

            def token_block(rows, kj, mask, dqs):
                return block(lambda sl: k_ref[rows, sl], lambda sl: v_ref[rows, sl],
                             lambda sl, val: add_to(dk_ref)(rows, sl, val),
                             lambda sl, val: add_to(dv_ref)(rows, sl, val),
                             lambda e: ck_ref[0, e, kj], lambda e, val: add_to(dck_ref, 0)(e, kj, val),
                             mask, dqs)

            dqs = block(lambda sl: km_ref[:, sl], lambda sl: vm_ref[:, sl],
                        lambda sl, val: add_to(dkm_ref, 0)(slice(None), sl, val),
                        lambda sl, val: add_to(dvm_ref, 0)(slice(None), sl, val),
                        lambda e: cmk_ref[0, e], lambda e, val: add_to(dcm_ref, 0)(e, val),
                        mcol < N_META, None)

            def below(kj, dqs):
                return token_block(pl.ds(pl.multiple_of(kj * TK, TK), TK), kj, None, dqs)

            if qi:
                dqs = lax.fori_loop(0, qi, below, dqs)
            dqs = token_block(slice(q0, q0 + TK), qi, causal, dqs)
            if forget:
                dcq_ref[0, 0, q0:q0 + TQ, :] = jnp.where(two == 0, dqs[2], dqs[3])
            if mla:
                for e in range(2):
                    dq_ref[q0:q0 + TQ, sls[e]] = dqs[e] * scale
            else:
                dq_ref[q0:q0 + TQ, :] = jnp.where(lo, dqs[0], dqs[1]) * scale

    qk, kmeta, vv, vmeta, pair = _attn_specs(S, NX, qw, v_col0)
    lse_spec = pl.BlockSpec((1, 1, S, 2), lambda b, p: (b, p, 0, 0))
    specs = [qk, qk, kmeta, vv, vmeta, pair, pair, lse_spec]
    ins = [qn, kn, kn, vsrc, vsrc, o, do, lse]
    if forget:
        specs += _cum_specs(S, TK)
        ins += [ck, cmk]
    first_alias = len(ins)
    specs += [pl.BlockSpec(memory_space=pl.ANY)] * 3
    ins += [_token_rows_buffer(T, NX, W) for _ in range(3)]
    mspec = pl.BlockSpec((1, META_BLK, qw), lambda b, p: (b, 0, p))
    out_specs = [qk, qk, qk, mspec, mspec]
    tok = jax.ShapeDtypeStruct((T, W), F32)
    met = jax.ShapeDtypeStruct((B, META_BLK, W), F32)
    out_shape = [tok, tok, tok, met, met]
    if forget:
        out_specs += _cum_specs(S, TK) + [lse_spec]
        out_shape += [jax.ShapeDtypeStruct((B, HEADS, S // TK, 1, TK), F32),
                      jax.ShapeDtypeStruct((B, HEADS, 1, META_BLK), F32),
                      jax.ShapeDtypeStruct((B, npair, S, 2), F32)]
    return _call(
        body, name, (B, npair), specs, out_specs, out_shape, ins, sem=("parallel", "parallel"),
        aliases={first_alias: 0, first_alias + 1: 1, first_alias + 2: 2}, exch=exch)


def _gate_fwd(z, bg, of, om, name):
    T, D = of.shape
    tm = _tile(T, 640, 16)

    def body(z_ref, bg_ref, of_ref, om_ref, o_ref):
        bias = bg_ref[...]

        def step(rows, carry):
            gt = jax.nn.sigmoid(z_ref[rows, :] + bias)
            o_ref[rows, :] = (gt[:, :D] * of_ref[rows, :] + gt[:, D:] * om_ref[rows, :]).astype(BF16)
            return carry

        _by_chunks(tm, D, step)

    row = pl.BlockSpec((tm, D), lambda i: (i, 0))
    return pl.pallas_call(
        body, name=name, grid=(T // tm,),
        in_specs=[pl.BlockSpec((tm, 2 * D), lambda i: (i, 0)),
                  pl.BlockSpec((1, 2 * D), lambda i: (0, 0)), row, row],
        out_specs=row, out_shape=jax.ShapeDtypeStruct((T, D), BF16),
        compiler_params=_params(("parallel",)),
    )(z, bg, of, om)


def _gate_bwd(dmix, z, bg, of, om, name):
    T, D = of.shape
    tm = _tile(T, 640, 16)

    def body(dm_ref, z_ref, bg_ref, of_ref, om_ref, dgl_ref, dof_ref, dom_ref, dbg_ref):
        bias = bg_ref[...]

        def step(rows, dbg_sum):
            gt = jax.nn.sigmoid(z_ref[rows, :] + bias)
            dm = dm_ref[rows, :]
            dof_ref[rows, :] = (dm * gt[:, :D]).astype(BF16)
            dom_ref[rows, :] = (dm * gt[:, D:]).astype(BF16)
            dgl = jnp.concatenate([dm * of_ref[rows, :], dm * om_ref[rows, :]], axis=1) * gt * (1.0 - gt)
            dgl_ref[rows, :] = dgl.astype(BF16)
            return dbg_sum + jnp.sum(dgl, axis=0, keepdims=True)

        dbg = _by_chunks(tm, D, step, jnp.zeros((1, 2 * D), F32))

        @pl.when(pl.program_id(0) == 0)
        def _():
            dbg_ref[...] = jnp.zeros_like(dbg_ref)

        dbg_ref[...] += dbg

    row = pl.BlockSpec((tm, D), lambda i: (i, 0))
    wide = pl.BlockSpec((tm, 2 * D), lambda i: (i, 0))
    one = pl.BlockSpec((1, 2 * D), lambda i: (0, 0))
    return pl.pallas_call(
        body, name=name, grid=(T // tm,),
        in_specs=[row, wide, one, row, row], out_specs=(wide, row, row, one),
        out_shape=(jax.ShapeDtypeStruct((T, 2 * D), BF16), jax.ShapeDtypeStruct((T, D), BF16),
                   jax.ShapeDtypeStruct((T, D), BF16), jax.ShapeDtypeStruct((1, 2 * D), F32)),
        compiler_params=_params(("arbitrary",)),
    )(dmix, z, bg, of, om)


def _loss(h, tgt, n_valid, name):
    T, D = h.shape
    tm = _tile(T, 640, 16)

    def body(h_ref, t_ref, dh_ref, l_ref):
        i = pl.program_id(0)
        row_in_block = lax.broadcasted_iota(jnp.int32, (tm, D), 0)

        def step(rows, part):
            err = jnp.where(row_in_block + i * tm < n_valid, h_ref[rows, :] - t_ref[rows, :], 0.0)
            dh_ref[rows, :] = err * (1.0 / D)
            return part + jnp.sum(err * err, axis=0, keepdims=True)

        part = _by_chunks(tm, D, step, jnp.zeros((1, D), F32))

        @pl.when(i == 0)
        def _():
            l_ref[...] = jnp.zeros_like(l_ref)

        l_ref[...] += 0.5 * jnp.sum(part) * (1.0 / D)

    row = pl.BlockSpec((tm, D), lambda i: (i, 0))
    acc = pl.BlockSpec((8, LANES), lambda i: (0, 0))
    return pl.pallas_call(
        body, name=name, grid=(T // tm,), in_specs=[row, row], out_specs=(row, acc),
        out_shape=(jax.ShapeDtypeStruct((T, D), F32), jax.ShapeDtypeStruct((8, LANES), F32)),
        compiler_params=_params(("arbitrary",)),
    )(h, tgt)


def _adamw(parts, w, m, v, name):
    P, R, C = parts.shape
    tr = _tile(R, max(8, (1 << 18) // C), 8)
    bc1 = 1.0 - ADAM_B1 ** ADAM_STEP
    bc2 = 1.0 - ADAM_B2 ** ADAM_STEP

    def body(p_ref, w_ref, m_ref, v_ref, g_ref, d_ref, m2_ref, v2_ref):
        def step(rows, carry):
            g = p_ref[0, rows, :].astype(F32)
            for j in range(1, P):
                g = g + p_ref[j, rows, :].astype(F32)
            m2 = ADAM_B1 * m_ref[rows, :] + (1.0 - ADAM_B1) * g
            v2 = ADAM_B2 * v_ref[rows, :] + (1.0 - ADAM_B2) * (g * g)
            m_hat = m2 / bc1
            v_hat = v2 / bc2
            g_ref[rows, :] = g
            d_ref[rows, :] = -ADAM_LR * (m_hat / (jnp.sqrt(v_hat) + ADAM_EPS) + ADAM_WD * w_ref[rows, :])
            m2_ref[rows, :] = m2
            v2_ref[rows, :] = v2
            return carry

        _by_chunks(tr, C, step)

    row = pl.BlockSpec((tr, C), lambda i: (i, 0))
    sh = jax.ShapeDtypeStruct((R, C), F32)
    return pl.pallas_call(
        body, name=name, grid=(R // tr,),
        in_specs=[pl.BlockSpec((P, tr, C), lambda i: (0, i, 0)), row, row, row],
        out_specs=(row, row, row, row), out_shape=(sh, sh, sh, sh),
        compiler_params=_params(("parallel",)),
    )(parts, w, m, v)


def _peer(d):
    x, y, c = lax.axis_index("x"), lax.axis_index("y"), lax.axis_index("c")
    px = 1 - x if d & 4 else x
    py = 1 - y if d & 2 else y
    pc = 1 - c if d & 1 else c
    return (px, py, pc), 4 * px + 2 * py + pc


class _Exchange:
    def __init__(self, srcs, gather):
        self.srcs, self.gather, self.n = list(srcs), gather, len(srcs)
        n = self.n
        hbm = pl.BlockSpec(memory_space=pl.ANY)
        self.in_specs = [hbm] * n
        self.out_specs = [hbm] * n
        self.out_shape = [jax.ShapeDtypeStruct((N_DEV,) + s.shape[-2:], s.dtype) for s in srcs]
        self.scratch = [pltpu.SemaphoreType.DMA((N_DEV - 1, n)), pltpu.SemaphoreType.DMA((N_DEV - 1, n)),
                        pltpu.SemaphoreType.DMA((n,))]

    def _copies(self, src_refs, out_refs, sems):
        send_sems, recv_sems, local_sems = sems
        _, me = _peer(0)

        def remote(w, d, landing):
            dev, lin = _peer(d)
            return pltpu.make_async_remote_copy(
                src_ref=src_refs[w] if self.gather else src_refs[w].at[lin],
                dst_ref=out_refs[w].at[lin if landing else me],
                send_sem=send_sems.at[d - 1, w], recv_sem=recv_sems.at[d - 1, w],
                device_id=dev, device_id_type=pl.DeviceIdType.MESH)

        pairs = [(w, d) for d in range(1, N_DEV) for w in range(self.n)]
        own = [pltpu.make_async_copy(src_refs[w] if self.gather else src_refs[w].at[me],
                                     out_refs[w].at[me], local_sems.at[w]) for w in range(self.n)]
        return own, [remote(w, d, False) for w, d in pairs], [remote(w, d, True) for w, d in pairs]

    def _gather_copies(self, src_refs, out_refs, sems):
        send_sems, recv_sems, local_sems = sems
        x, y, c = lax.axis_index("x"), lax.axis_index("y"), lax.axis_index("c")
        me, sibling = (x, y, c), (x, y, 1 - c)
        chips = [(1 - x, y), (x, 1 - y), (1 - x, 1 - y)]

        def copy(w, k, block, to, src=None):
            rows = out_refs[w].at[4 * block[0] + 2 * block[1] + block[2]]
            return pltpu.make_async_remote_copy(
                src_ref=rows if src is None else src, dst_ref=rows,
                send_sem=send_sems.at[k, w], recv_sem=recv_sems.at[k, w],
                device_id=to, device_id_type=pl.DeviceIdType.MESH)

        ws = range(self.n)
        own = [pltpu.make_async_copy(src_refs[w], out_refs[w].at[4 * x + 2 * y + c], local_sems.at[w])
               for w in ws]
        first = [copy(w, 0, me, sibling, src_refs[w]) for w in ws]
        first += [copy(w, 1 + j, me, (*chip, c), src_refs[w]) for j, chip in enumerate(chips) for w in ws]
        landed = [[copy(w, 1 + j, (*chip, c), me) for w in ws] for j, chip in enumerate(chips)]
        passed = [[copy(w, 4 + j, (*chip, c), sibling) for w in ws] for j, chip in enumerate(chips)]
        from_sibling = [copy(w, 0, sibling, me) for w in ws]
        from_sibling += [copy(w, 4 + j, (*chip, 1 - c), me) for j, chip in enumerate(chips) for w in ws]
        return own, first, landed, passed, from_sibling

    def start(self, src_refs, out_refs, sems):
        if self.gather:
            own, first = self._gather_copies(src_refs, out_refs, sems)[:2]
            sent = first
        else:
            own, sent, _ = self._copies(src_refs, out_refs, sems)
        for cp in own + sent:
            cp.start()

    def wait(self, src_refs, out_refs, sems):
        if self.gather:
            own, first, landed, passed, from_sibling = self._gather_copies(src_refs, out_refs, sems)
            for arrived, onward in zip(landed, passed):
                for cp in arrived:
                    cp.wait_recv()
                for cp in onward:
                    cp.start()
            for cp in from_sibling:
                cp.wait_recv()
            for cp in first + [cp for group in passed for cp in group]:
                cp.wait_send()
        else:
            own, sent, landing = self._copies(src_refs, out_refs, sems)
            for cp in landing:
                cp.wait_recv()
            for cp in sent:
                cp.wait_send()
        for cp in own:
            cp.wait()


def _exchange(srcs, name, gather):
    ex = _Exchange(srcs, gather)
    n = ex.n

    def body(*refs):
        ex.start(refs[:n], refs[n:2 * n], refs[2 * n:])
        ex.wait(refs[:n], refs[n:2 * n], refs[2 * n:])

    outs = pl.pallas_call(
        body, name=name, in_specs=ex.in_specs, out_specs=tuple(ex.out_specs),
        out_shape=tuple(ex.out_shape), scratch_shapes=ex.scratch,
    )(*srcs)
    return list(outs)


def _call(body, name, grid, in_specs, out_specs, out_shape, ins, scratch_shapes=(), sem=None,
          aliases=None, exch=None):
    aliases = aliases or {}
    if exch is None:
        outs = pl.pallas_call(
            body, name=name, grid=grid, in_specs=list(in_specs), out_specs=tuple(out_specs),
            out_shape=tuple(out_shape), scratch_shapes=list(scratch_shapes),
            input_output_aliases=aliases, compiler_params=_params(sem),
        )(*ins)
        return list(outs), []
    ni, no, ns, n = len(in_specs), len(out_specs), len(scratch_shapes), exch.n
    last_ids = [g - 1 for g in grid]

    def hosted(*refs):
        cin, xin = refs[:ni], refs[ni:ni + n]
        cout, xout = refs[ni + n:ni + n + no], refs[ni + n + no:ni + 2 * n + no]
        cscr, xsem = refs[ni + 2 * n + no:ni + 2 * n + no + ns], refs[ni + 2 * n + no + ns:]
        ids = [pl.program_id(a) for a in range(len(grid))]
        first, last = ids[0] == 0, ids[0] == last_ids[0]
        for a in range(1, len(grid)):
            first, last = first & (ids[a] == 0), last & (ids[a] == last_ids[a])

        @pl.when(first)
        def _():
            exch.start(xin, xout, xsem)

        body(*cin, *cout, *cscr)

        @pl.when(last)
        def _():
            exch.wait(xin, xout, xsem)

    outs = pl.pallas_call(
        hosted, name=name, grid=grid, in_specs=list(in_specs) + exch.in_specs,
        out_specs=tuple(list(out_specs) + exch.out_specs),
        out_shape=tuple(list(out_shape) + exch.out_shape),
        scratch_shapes=list(scratch_shapes) + exch.scratch, input_output_aliases=aliases,
        compiler_params=_params(("arbitrary",) * len(grid)),
    )(*ins, *exch.srcs)
    return list(outs[:no]), list(outs[no:])


def _pack(arrs, cols, row_mult):
    flat = jnp.concatenate([a.reshape(-1) for a in arrs])
    n = flat.shape[0]
    quantum = cols * row_mult
    total = -(-n // quantum) * quantum
    return jnp.pad(flat, (0, total - n)).reshape(total // cols, cols)


def _pack_rows(arrs, cols, row_mult):
    flat = jnp.concatenate(arrs, axis=1)
    n = flat.shape[1]
    quantum = cols * row_mult
    total = -(-n // quantum) * quantum
    return jnp.pad(flat, ((0, 0), (0, total - n))).reshape(N_DEV, total // cols, cols)


def _unpack(packed, shapes):
    flat = packed.reshape(-1)
    out, off = [], 0
    for s in shapes:
        n = int(np.prod(s))
        out.append(flat[off:off + n].reshape(s))
        off += n
    return out


def _rope_tables(positions):
    inv_freq = ROPE_THETA ** (-jnp.arange(0, MLA_ROPE, 2, dtype=F32) / MLA_ROPE)
    ang = positions.astype(F32)[:, None] * inv_freq[None, :]
    cos, sin = jnp.cos(ang), jnp.sin(ang)
    n = positions.shape[0]
    ones, zeros = jnp.ones((n, MLA_NOPE), F32), jnp.zeros((n, MLA_NOPE), F32)
    tail1, tail0 = jnp.ones((n, LANES - MLA_QK), F32), jnp.zeros((n, LANES - MLA_QK), F32)
    z16 = jnp.zeros((n, 16), F32)
    c = jnp.concatenate([ones, cos, cos, tail1], axis=1)
    s1 = jnp.concatenate([zeros, -sin, z16, tail0], axis=1)
    s2 = jnp.concatenate([zeros, z16, sin, tail0], axis=1)
    return c, s1, s2


def kernel(x, meta_tokens, ffn1_norm, ffn1_w_gu, ffn1_w_down, mix_norm, w_in, b_forget, b_gate, fox_q_norm, fox_k_norm, mla_cq_norm, mla_w_uq, mla_ckv_norm, mla_w_ukv, mla_q_norm, mla_k_norm, w_branch_fox, w_branch_mla, w_out, ffn2_norm, ffn2_w_gu, ffn2_w_down, loss_target, m_meta_tokens, m_ffn1_norm, m_ffn1_w_gu, m_ffn1_w_down, m_mix_norm, m_w_in, m_b_forget, m_b_gate, m_fox_q_norm, m_fox_k_norm, m_mla_cq_norm, m_mla_w_uq, m_mla_ckv_norm, m_mla_w_ukv, m_mla_q_norm, m_mla_k_norm, m_w_branch_fox, m_w_branch_mla, m_w_out, m_ffn2_norm, m_ffn2_w_gu, m_ffn2_w_down, v_meta_tokens, v_ffn1_norm, v_ffn1_w_gu, v_ffn1_w_down, v_mix_norm, v_w_in, v_b_forget, v_b_gate, v_fox_q_norm, v_fox_k_norm, v_mla_cq_norm, v_mla_w_uq, v_mla_ckv_norm, v_mla_w_ukv, v_mla_q_norm, v_mla_k_norm, v_w_branch_fox, v_w_branch_mla, v_w_out, v_ffn2_norm, v_ffn2_w_gu, v_ffn2_w_down):
    names = ["meta_tokens", "ffn1_norm", "ffn1_w_gu", "ffn1_w_down", "mix_norm", "w_in", "b_forget",
             "b_gate", "fox_q_norm", "fox_k_norm", "mla_cq_norm", "mla_w_uq", "mla_ckv_norm",
             "mla_w_ukv", "mla_q_norm", "mla_k_norm", "w_branch_fox", "w_branch_mla", "w_out",
             "ffn2_norm", "ffn2_w_gu", "ffn2_w_down"]
    W = dict(zip(names, [meta_tokens, ffn1_norm, ffn1_w_gu, ffn1_w_down, mix_norm, w_in, b_forget,
                         b_gate, fox_q_norm, fox_k_norm, mla_cq_norm, mla_w_uq, mla_ckv_norm,
                         mla_w_ukv, mla_q_norm, mla_k_norm, w_branch_fox, w_branch_mla, w_out,
                         ffn2_norm, ffn2_w_gu, ffn2_w_down]))
    Mo = dict(zip(names, [m_meta_tokens, m_ffn1_norm, m_ffn1_w_gu, m_ffn1_w_down, m_mix_norm, m_w_in,
                          m_b_forget, m_b_gate, m_fox_q_norm, m_fox_k_norm, m_mla_cq_norm,
                          m_mla_w_uq, m_mla_ckv_norm, m_mla_w_ukv, m_mla_q_norm, m_mla_k_norm,
                          m_w_branch_fox, m_w_branch_mla, m_w_out, m_ffn2_norm, m_ffn2_w_gu,
                          m_ffn2_w_down]))
    Vo = dict(zip(names, [v_meta_tokens, v_ffn1_norm, v_ffn1_w_gu, v_ffn1_w_down, v_mix_norm, v_w_in,
                          v_b_forget, v_b_gate, v_fox_q_norm, v_fox_k_norm, v_mla_cq_norm,
                          v_mla_w_uq, v_mla_ckv_norm, v_mla_w_ukv, v_mla_q_norm, v_mla_k_norm,
                          v_w_branch_fox, v_w_branch_mla, v_w_out, v_ffn2_norm, v_ffn2_w_gu,
                          v_ffn2_w_down]))

    B, S, D = x.shape
    NX = B * S
    T = NX + META_BLK
    H = HEADS
    assert NX % META_BLK == 0 and S % LANES == 0
    me = 4 * lax.axis_index("x") + 2 * lax.axis_index("y") + lax.axis_index("c")

    big = [("ffn1_w_gu", 1), ("ffn1_w_down", 0), ("w_in", 1), ("mla_w_uq", 1), ("mla_w_ukv", 1),
           ("w_branch_fox", 1), ("w_branch_mla", 1), ("w_out", 0), ("ffn2_w_gu", 1), ("ffn2_w_down", 0)]
    mix_small = ["mla_w_uq", "mla_w_ukv", "w_branch_fox", "w_branch_mla", "w_out"]
    last_group = ["ffn2_w_gu", "ffn2_w_down"]
    axis_of = dict(big)
    full = {}

    def shards(group):
        return [W[n][0].astype(BF16) for n in group]

    def assemble(group, blks):
        for n, blk in zip(group, blks):
            _, r, c = blk.shape
            full[n] = (blk.transpose(1, 0, 2).reshape(r, N_DEV * c) if axis_of[n] == 1
                       else blk.reshape(N_DEV * r, c))

    got = _exchange(shards(["ffn1_w_gu"]) + [meta_tokens], "gather_first", gather=True)
    assemble(["ffn1_w_gu"], got[:1])
    meta_full = got[1].transpose(1, 0, 2).reshape(N_META, D)

    Z_G, Z_FQ = 0, 2 * D
    Z_FK, Z_FV = Z_FQ + FOX_W, Z_FQ + 2 * FOX_W
    Z_CQ = Z_FQ + 3 * FOX_W
    Z_CKV = Z_CQ + MLA_Q_RANK
    Z_F = Z_CKV + MLA_KV_RANK
    Z_KR = Z_F + LANES

    def pad_lanes(a, w=LANES):
        return jnp.pad(a, [(0, 0)] * (a.ndim - 1) + [(0, w - a.shape[-1])])

    def rows_T(real, meta=None):
        n = real.shape[1]
        parts = [real]
        used = 0
        if meta is not None:
            parts.append(meta)
            used = meta.shape[0]
        if T - NX - used:
            parts.append(jnp.zeros((T - NX - used, n), real.dtype))
        return jnp.concatenate(parts, axis=0)

    def put_meta(tok, meta_per_seq):
        return lax.dynamic_update_slice(tok, meta_per_seq.sum(0), (NX, 0))

    h0 = rows_T(x.reshape(NX, D), meta_full)
    tgt = rows_T(loss_target.reshape(NX, D))

    def ffn_fwd(h, norm, w_gu, tag, behind_up=None, behind_down=None):
        u = _norm_fwd(h, 0, D, D, norm, D, D, tag + "_norm")
        (sa, sb, a), got = _ffn_up(u, w_gu, tag + "_up",
                                  exch=_Exchange(shards(behind_up), True) if behind_up else None)
        assemble(behind_up or [], got)
        h_out = _mm(a, full[tag + "_w_down"], "nn", tag + "_down", scale=0.5, res=h,
                    exch=_Exchange(shards(behind_down), True) if behind_down else None)
        if behind_down:
            h_out, got = h_out
            assemble(behind_down, got)
        return h_out, (u, sa, sb, a)

    h1, ffn1_saved = ffn_fwd(h0, W["ffn1_norm"], full["ffn1_w_gu"], "ffn1",
                             behind_up=["ffn1_w_down"], behind_down=["w_in"])
    wi = full["w_in"]
    o_fq = 0
    o_f = 3 * FOX_W
    o_cq = o_f + HEADS
    o_kr = o_cq + MLA_Q_RANK + MLA_KV_RANK
    o_g = o_kr + MLA_ROPE
    w_in_p = jnp.concatenate([
        wi[:, o_g:o_g + 2 * D], wi[:, o_fq:o_f], wi[:, o_cq:o_kr],
        jnp.pad(wi[:, o_f:o_cq], ((0, 0), (0, LANES - HEADS))),
        jnp.pad(wi[:, o_kr:o_g], ((0, 0), (0, LANES - MLA_ROPE)))], axis=1)

    u2 = _norm_fwd(h1, 0, D, D, W["mix_norm"], D, D, "mix_norm")
    z, got = _mm(u2, w_in_p, "nn", "w_in", exch=_Exchange(shards(mix_small), True))
    assemble(mix_small, got)
    w_uq_p = jnp.pad(full["mla_w_uq"].reshape(MLA_Q_RANK, H, MLA_QK),
                     ((0, 0), (0, 0), (0, LANES - MLA_QK))).reshape(MLA_Q_RANK, H * LANES)

    gq_f = jnp.tile(W["fox_q_norm"], (1, 2))
    gk_f = jnp.tile(W["fox_k_norm"], (1, 2))
    fqn = _norm_fwd(z, Z_FQ, FOX_W, LANES, gq_f, FOX_HD, FOX_HD, "fox_q_norm")
    fkn = _norm_fwd(z, Z_FK, FOX_W, LANES, gk_f, FOX_HD, FOX_HD, "fox_k_norm")
    fl = z[:NX, Z_F:Z_F + LANES].reshape(B, S, LANES)
    flm = z[NX:, Z_F:Z_F + LANES]
    bf = pad_lanes(W["b_forget"])
    cum, cumm = _cum_fwd(fl, flm, bf, "forget_cum")
    TK = min(512, S)
    ck = cum[:, :, :H].transpose(0, 2, 1).reshape(B, H, S // TK, 1, TK)
    cmk = jnp.broadcast_to(cumm[:, :H].T[None, :, None, :], (B, H, 1, META_BLK))
    (o_fox, lse_fox), got = _attn_fwd(fqn, fkn, z, Z_FV, False, FOX_HD ** -0.5, S, NX, "fox_attn", ck, cmk,
                                      exch=_Exchange(shards(last_group), True))
    assemble(last_group, got)
    of = _mm(o_fox, full["w_branch_fox"], "nn", "branch_fox")

    pos = jnp.concatenate([jnp.tile(jnp.arange(S) + N_META, B), jnp.arange(META_BLK)])
    tabs = _rope_tables(pos)
    cqn = _norm_fwd(z, Z_CQ, MLA_Q_RANK, MLA_Q_RANK, W["mla_cq_norm"], MLA_Q_RANK, MLA_Q_RANK, "mla_cq_norm")
    q_lin = _mm(cqn, w_uq_p, "nn", "mla_uq")
    ckvn = _norm_fwd(z, Z_CKV, MLA_KV_RANK, MLA_KV_RANK, W["mla_ckv_norm"], MLA_KV_RANK, MLA_KV_RANK,
                     "mla_ckv_norm")
    kv_lin = _mm(ckvn, full["mla_w_ukv"], "nn", "mla_ukv")
    gq_m, gk_m = pad_lanes(W["mla_q_norm"]), pad_lanes(W["mla_k_norm"])
    mqn = _norm_fwd(q_lin, 0, H * LANES, LANES, gq_m, LANES, MLA_QK, "mla_q_norm", tabs=tabs)
    mkn = _mla_k_fwd(kv_lin, z, Z_KR, gk_m, tabs, "mla_k_norm")
    (o_mla, lse_mla), _ = _attn_fwd(mqn, mkn, kv_lin, 0, True, MLA_QK ** -0.5, S, NX, "mla_attn")
    om = _mm(o_mla, full["w_branch_mla"], "nn", "branch_mla")

    mix = _gate_fwd(z, W["b_gate"], of, om, "gate_mix")
    h2 = _mm(mix, full["w_out"], "nn", "w_out", res=h1)

    h3, ffn2_saved = ffn_fwd(h2, W["ffn2_norm"], full["ffn2_w_gu"], "ffn2")

    dh3, loss_acc = _loss(h3, tgt, NX, "loss")
    loss = lax.psum(loss_acc[0, 0], AXES)

    G = {}
    parts = {}

    def scatter_of(group):
        per_dest = []
        for n in group:
            r, c = W[n].shape[1:]
            per_dest.append((G[n].reshape(r, N_DEV, c).transpose(1, 0, 2) if axis_of[n] == 1
                             else G[n].reshape(N_DEV, r, c)).astype(BF16))
        return _Exchange(per_dest, False)

    def ffn_bwd(dh, h, norm, w_gu, w_down, saved, tag, behind_down=None, spread=False):
        u, sa, sb, a = saved
        G[tag + "_w_down"] = _mm(a, dh, "tn", tag + "_dw_down", scale=0.5)
        (dg, dup), got = _ffn_down_bwd(dh, w_down, sa, sb, tag + "_down_bwd",
                                       exch=scatter_of(behind_down) if behind_down else None)
        parts.update(zip(behind_down or [], got))
        dw_g = _mm(u, dg, "tn", tag + "_dw_g", exch=scatter_of([tag + "_w_down"]) if spread else None)
        if spread:
            dw_g, got = dw_g
            parts[tag + "_w_down"] = got[0]
        G[tag + "_w_gu"] = jnp.concatenate([dw_g, _mm(u, dup, "tn", tag + "_dw_u")], axis=1)
        (du,), got = _ffn_up_bwd_dx(dg, dup, w_gu, tag + "_up_bwd",
                                    exch=scatter_of([tag + "_w_gu"]) if spread else None)
        if spread:
            parts[tag + "_w_gu"] = got[0]
        dh_in, G[tag + "_norm"] = _norm_bwd(h, 0, D, D, norm, D, D, du, tag + "_norm_bwd", res=dh)
        return dh_in

    dh2 = ffn_bwd(dh3, h2, W["ffn2_norm"], full["ffn2_w_gu"], full["ffn2_w_down"], ffn2_saved, "ffn2")

    G["w_out"] = _mm(mix, dh2, "tn", "dw_out")
    dmix = _mm(dh2, full["w_out"], "nt", "w_out_bwd")
    dgl, dof, dom, G["b_gate"] = _gate_bwd(dmix, z, W["b_gate"], of, om, "gate_bwd")

    G["w_branch_fox"] = _mm(o_fox, dof, "tn", "dw_branch_fox")
    do_fox = _mm(dof, full["w_branch_fox"], "nt", "branch_fox_bwd")
    (dq_f, dk_f, dv_f, dkm_f, dvm_f, dck, dcmk, dcq), got = _attn_bwd(
        fqn, fkn, z, Z_FV, o_fox, lse_fox, do_fox, False, FOX_HD ** -0.5, S, NX, "fox_attn_bwd", ck, cmk,
        exch=scatter_of(last_group))
    parts.update(zip(last_group, got))
    dk_f, dv_f = put_meta(dk_f, dkm_f), put_meta(dv_f, dvm_f)
    dfq, gq = _norm_bwd(z, Z_FQ, FOX_W, LANES, gq_f, FOX_HD, FOX_HD, dq_f, "fox_q_norm_bwd", out_dtype=BF16)
    dfk, gk = _norm_bwd(z, Z_FK, FOX_W, LANES, gk_f, FOX_HD, FOX_HD, dk_f, "fox_k_norm_bwd", out_dtype=BF16)
    G["fox_q_norm"] = gq[:, :FOX_HD] + gq[:, FOX_HD:]
    G["fox_k_norm"] = gk[:, :FOX_HD] + gk[:, FOX_HD:]
    dc = pad_lanes(dck.reshape(B, H, S).transpose(0, 2, 1)
                   + dcq.transpose(0, 2, 1, 3).reshape(B, S, H))
    dcm = pad_lanes(dcmk.sum(0)[:, 0, :].T)
    dcm = jnp.where(jnp.arange(LANES)[:, None] < N_META, dcm, 0.0)
    dfl, dflm, dbf = _cum_bwd(dc, dcm, fl, flm, bf, "forget_cum_bwd")
    G["b_forget"] = dbf[:, :HEADS]
    dfl_t = rows_T(dfl.reshape(NX, LANES), dflm)

    G["w_branch_mla"] = _mm(o_mla, dom, "tn", "dw_branch_mla")
    do_mla = _mm(dom, full["w_branch_mla"], "nt", "branch_mla_bwd")
    (dq_m, dk_m, dvk, dkm_m, dvkm), _ = _attn_bwd(
        mqn, mkn, kv_lin, 0, o_mla, lse_mla, do_mla, True, MLA_QK ** -0.5, S, NX, "mla_attn_bwd")
    dk_m, dvk = put_meta(dk_m, dkm_m), put_meta(dvk, dvkm)
    dq_lin, gq = _norm_bwd(q_lin, 0, H * LANES, LANES, gq_m, LANES, MLA_QK, dq_m, "mla_q_norm_bwd", tabs=tabs)
    G["mla_q_norm"] = gq[:, :MLA_QK]
    G["mla_w_uq"] = _mm(cqn, dq_lin, "tn", "dw_uq").reshape(MLA_Q_RANK, H, LANES)[:, :, :MLA_QK].reshape(
        MLA_Q_RANK, H * MLA_QK)
    dcqn = _mm(dq_lin, w_uq_p, "nt", "mla_uq_bwd")
    dcq, G["mla_cq_norm"] = _norm_bwd(z, Z_CQ, MLA_Q_RANK, MLA_Q_RANK, W["mla_cq_norm"], MLA_Q_RANK,
                                      MLA_Q_RANK, dcqn, "mla_cq_norm_bwd", out_dtype=BF16)
    dkv_lin, dkr, gk = _mla_k_bwd(kv_lin, z, Z_KR, gk_m, tabs, dk_m, dvk, "mla_k_norm_bwd")
    G["mla_k_norm"] = gk[:, :MLA_QK]
    G["mla_w_ukv"] = _mm(ckvn, dkv_lin, "tn", "dw_ukv")
    dckvn = _mm(dkv_lin, full["mla_w_ukv"], "nt", "mla_ukv_bwd")
    dckv, G["mla_ckv_norm"] = _norm_bwd(z, Z_CKV, MLA_KV_RANK, MLA_KV_RANK, W["mla_ckv_norm"], MLA_KV_RANK,
                                        MLA_KV_RANK, dckvn, "mla_ckv_norm_bwd", out_dtype=BF16)

    dz = jnp.concatenate([dgl, dfq, dfk, dv_f.astype(BF16), dcq, dckv, dfl_t.astype(BF16),
                          dkr.astype(BF16)], axis=1)
    dw_in_p = _mm(u2, dz, "tn", "dw_in")
    G["w_in"] = jnp.concatenate([
        dw_in_p[:, Z_FQ:Z_CQ], dw_in_p[:, Z_F:Z_F + HEADS], dw_in_p[:, Z_CQ:Z_F],
        dw_in_p[:, Z_KR:Z_KR + MLA_ROPE], dw_in_p[:, Z_G:Z_G + 2 * D]], axis=1)
    du2, got = _mm(dz, w_in_p, "nt", "w_in_bwd", exch=scatter_of(mix_small))
    parts.update(zip(mix_small, got))
    dh1, G["mix_norm"] = _norm_bwd(h1, 0, D, D, W["mix_norm"], D, D, du2, "mix_norm_bwd", res=dh2)

    dh0 = ffn_bwd(dh1, h0, W["ffn1_norm"], full["ffn1_w_gu"], full["ffn1_w_down"], ffn1_saved, "ffn1",
                  behind_down=["w_in"], spread=True)
    grad_x = dh0[:NX].reshape(B, S, D)
    G["meta_tokens"] = dh0[NX:NX + N_META]

    res = {}
    for n, _ in big:
        outs4 = _adamw(parts[n], W[n][0], Mo[n][0], Vo[n][0], "adamw_" + n)
        for key, arr in zip(("g", "d", "m", "v"), outs4):
            res[key, n] = arr[None]

    small = [n for n in names if n not in dict(big) and n != "meta_tokens"]
    small_shapes = [W[n].shape for n in small]
    spack = _pack([G["meta_tokens"]] + [G[n] for n in small], 1024, 8)
    (sparts,) = _exchange([spack], "gather_small_grads", gather=True)
    sflat = sparts.reshape(N_DEV, -1)
    dsh = D // N_DEV
    meta_part = lax.dynamic_slice(sflat[:, :N_META * D].reshape(N_DEV, N_META, D),
                                  (0, 0, me * dsh), (N_DEV, N_META, dsh)).reshape(N_DEV, -1)
    rep_len = sum(int(np.prod(s)) for s in small_shapes)
    rep_part = sflat[:, N_META * D:N_META * D + rep_len]
    sp = _pack_rows([meta_part, rep_part], LANES, _chunk_rows(LANES))
    pks = lambda src: _pack([src["meta_tokens"]] + [src[n] for n in small], LANES, _chunk_rows(LANES))
    g_s, d_s, m_s, v_s = _adamw(sp, pks(W), pks(Mo), pks(Vo), "adamw_small")
    shapes_s = [W["meta_tokens"].shape] + small_shapes
    for key, packed in (("g", g_s), ("d", d_s), ("m", m_s), ("v", v_s)):
        for n, arr in zip(["meta_tokens"] + small, _unpack(packed, shapes_s)):
            res[key, n] = arr

    outs = [loss, grad_x]
    for key in ("g", "d", "m", "v"):
        outs += [res[key, n] for n in names]
    return tuple(outs)
```

```python
import numpy as np
import jax
import jax.numpy as jnp
from jax import lax
from jax.experimental import pallas as pl
from jax.experimental.pallas import tpu as pltpu

F32 = jnp.float32
BF16 = jnp.bfloat16

N_META = 16
EPS = 1e-6
HEADS = 8
FOX_HD = 64
FOX_W = HEADS * FOX_HD
MLA_Q_RANK = 256
MLA_KV_RANK = 128
MLA_NOPE = 64
MLA_ROPE = 32
MLA_QK = MLA_NOPE + MLA_ROPE
MLA_V = 64
ROPE_THETA = 10000.0
LANES = 128
HALF = LANES // 2
META_BLK = 128
NEG = -1e30

ADAM_LR = 0.001
ADAM_B1 = 0.9
ADAM_B2 = 0.999
ADAM_EPS = 1e-08
ADAM_WD = 0.01
ADAM_STEP = 10

N_DEV = 8
AXES = ("x", "y", "c")
VMEM_LIMIT_BYTES = 56 * 1024 * 1024


def _tile(n, cap, mult):
    best = None
    for d in range(mult, min(n, cap) + 1, mult):
        if n % d == 0:
            best = d
    return n if best is None else best


VREG_ELEMS = 8 * LANES


def _row_tile(rows, width):
    return _tile(rows, max(16, (1 << 19) // width), 16)


def _chunk_rows(width):
    rows = 16
    while 2 * rows * width <= 8 * VREG_ELEMS:
        rows *= 2
    return rows


def _by_chunks(rows, width, step, init=()):
    return step(pl.ds(0, rows), init)


def _params(sem=None):
    return pltpu.CompilerParams(dimension_semantics=sem, vmem_limit_bytes=VMEM_LIMIT_BYTES)


def _mm(a, b, mode, name, out_dtype=F32, scale=1.0, res=None, exch=None):
    if mode == "nn":
        (M, K), (K2, N) = a.shape, b.shape
    elif mode == "nt":
        (M, K), (N, K2) = a.shape, b.shape
    else:
        (K, M), (K2, N) = a.shape, b.shape
    assert K == K2, (a.shape, b.shape, mode)
    if mode == "tn":
        tm, tk = _tile(M, 1408, 128), _tile(K, 2080, 16)
    else:
        tm, tk = _tile(M, 640, 16), _tile(K, 4224, 128)
    tn = _tile(N, 1408, 128)
    nk = K // tk
    ni, nj = M // tm, N // tn
    bytes_a, bytes_b = a.size * a.dtype.itemsize, b.size * b.dtype.itemsize
    j_outer = nk == 1 and bytes_a * nj + bytes_b < bytes_a + bytes_b * ni
    ij = (lambda g0, g1: (g1, g0)) if j_outer else (lambda g0, g1: (g0, g1))

    def spec(shape, at):
        return pl.BlockSpec(shape, lambda g0, g1, k: at(*ij(g0, g1), k))

    a_spec = {"nn": spec((tm, tk), lambda i, j, k: (i, k)),
              "nt": spec((tm, tk), lambda i, j, k: (i, k)),
              "tn": spec((tk, tm), lambda i, j, k: (k, i))}[mode]
    b_spec = {"nn": spec((tk, tn), lambda i, j, k: (k, j)),
              "nt": spec((tn, tk), lambda i, j, k: (j, k)),
              "tn": spec((tk, tn), lambda i, j, k: (k, j))}[mode]
    dims = {"nn": (((1,), (0,)), ((), ())), "nt": (((1,), (1,)), ((), ())),
            "tn": (((0,), (0,)), ((), ()))}[mode]
    o_spec = spec((tm, tn), lambda i, j, k: (i, j))
    has_res = res is not None

    def body(*refs):
        a_ref, b_ref = refs[:2]
        r_ref = refs[2] if has_res else None
        o_ref = refs[2 + has_res]

        def finish(acc):
            o = acc * scale
            if has_res:
                o = o + r_ref[...]
            o_ref[...] = o.astype(out_dtype)

        prod = lax.dot_general(a_ref[...].astype(BF16), b_ref[...].astype(BF16), dims,
                               preferred_element_type=F32)
        if nk == 1:
            finish(prod)
            return
        acc_ref = refs[3 + has_res]
        k = pl.program_id(2)

        @pl.when(k == 0)
        def _():
            acc_ref[...] = prod

        @pl.when((k > 0) & (k < nk - 1))
        def _():
            acc_ref[...] += prod

        @pl.when(k == nk - 1)
        def _():
            finish(acc_ref[...] + prod)

    ins = [a, b] + ([res] if has_res else [])
    specs = [a_spec, b_spec] + ([o_spec] if has_res else [])
    (out,), got = _call(
        body, name, (nj, ni, nk) if j_outer else (ni, nj, nk), specs, [o_spec],
        [jax.ShapeDtypeStruct((M, N), out_dtype)],
        ins, scratch_shapes=[pltpu.VMEM((tm, tn), F32)] if nk > 1 else [],
        sem=("parallel", "parallel", "arbitrary"), exch=exch)
    return out if exch is None else (out, got)


def _rope_fwd(y, c, s1, s2):
    return y * c + pltpu.roll(y, LANES - 16, 1) * s1 + pltpu.roll(y, 16, 1) * s2


def _rope_bwd(dy, c, s1, s2):
    return dy * c + pltpu.roll(dy * s1, 16, 1) + pltpu.roll(dy * s2, LANES - 16, 1)


def _group_sum(v, seg):
    if seg == v.shape[-1]:
        return jnp.sum(v, axis=-1, keepdims=True)
    lo = lax.broadcasted_iota(jnp.int32, v.shape, 1) < seg
    s_lo = jnp.sum(jnp.where(lo, v, 0.0), axis=-1, keepdims=True)
    s_hi = jnp.sum(jnp.where(lo, 0.0, v), axis=-1, keepdims=True)
    return jnp.where(lo, s_lo, s_hi)


def _norm_fwd(src, col0, width, bw, gain, seg, d_true, name, tabs=None, out_dtype=BF16):
    T = src.shape[0]
    tr = _row_tile(T, bw)
    inv_d = 1.0 / d_true
    c0 = col0 // bw
    assert col0 % bw == 0 and width % bw == 0

    def body(*refs):
        if tabs is None:
            x_ref, g_ref, o_ref = refs
        else:
            x_ref, g_ref, c_ref, s1_ref, s2_ref, o_ref = refs
        gain_v = g_ref[...]

        def step(rows, carry):
            xv = x_ref[rows, :]
            r = lax.rsqrt(_group_sum(xv * xv, seg) * inv_d + EPS)
            y = xv * r * gain_v
            if tabs is not None:
                y = _rope_fwd(y, c_ref[rows, :], s1_ref[rows, :], s2_ref[rows, :])
            o_ref[rows, :] = y.astype(out_dtype)
            return carry

        _by_chunks(tr, bw, step)

    specs = [pl.BlockSpec((tr, bw), lambda i, j: (i, c0 + j)), pl.BlockSpec((1, bw), lambda i, j: (0, 0))]
    ins = [src, gain]
    if tabs is not None:
        tab = pl.BlockSpec((tr, LANES), lambda i, j: (i, 0))
        specs += [tab, tab, tab]
        ins += list(tabs)
    return pl.pallas_call(
        body, name=name, grid=(T // tr, width // bw), in_specs=specs,
        out_specs=pl.BlockSpec((tr, bw), lambda i, j: (i, j)),
        out_shape=jax.ShapeDtypeStruct((T, width), out_dtype),
        compiler_params=_params(("parallel", "parallel")),
    )(*ins)


def _norm_bwd_math(xv, gain, dyv, seg, inv_d):
    r = lax.rsqrt(_group_sum(xv * xv, seg) * inv_d + EPS)
    gy = dyv * gain
    dot = _group_sum(gy * xv, seg)
    dx = r * gy - xv * (r * r * r * inv_d) * dot
    return dx, jnp.sum(dyv * xv * r, axis=0, keepdims=True)


def _norm_bwd(src, col0, width, bw, gain, seg, d_true, dy, name, tabs=None, res=None, out_dtype=F32,
              bf16_copy=False):
    T = src.shape[0]
    tr = _row_tile(T, bw)
    inv_d = 1.0 / d_true
    c0 = col0 // bw
    has_res = res is not None

    def body(*refs):
        refs = list(refs)
        x_ref, g_ref, dy_ref = refs[:3]
        pos = 3
        if tabs is not None:
            c_ref, s1_ref, s2_ref = refs[3:6]
            pos = 6
        if has_res:
            r_ref = refs[pos]
            pos += 1
        dx_ref = refs[pos]
        dxb_ref = refs[pos + 1] if bf16_copy else None
        dg_ref = refs[pos + 1 + bf16_copy]
        gain_v = g_ref[...]

        def step(rows, dg_sum):
            dyv = dy_ref[rows, :].astype(F32)
            if tabs is not None:
                dyv = _rope_bwd(dyv, c_ref[rows, :], s1_ref[rows, :], s2_ref[rows, :])
            dx, dg = _norm_bwd_math(x_ref[rows, :], gain_v, dyv, seg, inv_d)
            if has_res:
                dx = dx + r_ref[rows, :]
            dx_ref[rows, :] = dx.astype(out_dtype)
            if bf16_copy:
                dxb_ref[rows, :] = dx.astype(BF16)
            return dg_sum + dg

        dg = _by_chunks(tr, bw, step, jnp.zeros((1, bw), F32))

        @pl.when((pl.program_id(0) == 0) & (pl.program_id(1) == 0))
        def _():
            dg_ref[...] = jnp.zeros_like(dg_ref)

        dg_ref[...] += dg

    blk = pl.BlockSpec((tr, bw), lambda i, j: (i, j))
    one = pl.BlockSpec((1, bw), lambda i, j: (0, 0))
    specs = [pl.BlockSpec((tr, bw), lambda i, j: (i, c0 + j)), one, blk]
    ins = [src, gain, dy]
    if tabs is not None:
        tab = pl.BlockSpec((tr, LANES), lambda i, j: (i, 0))
        specs += [tab, tab, tab]
        ins += list(tabs)
    if has_res:
        specs.append(blk)
        ins.append(res)
    extra = bf16_copy * [blk]
    extra_shape = bf16_copy * [jax.ShapeDtypeStruct((T, width), BF16)]
    return pl.pallas_call(
        body, name=name, grid=(T // tr, width // bw), in_specs=specs, out_specs=(blk, *extra, one),
        out_shape=(jax.ShapeDtypeStruct((T, width), out_dtype), *extra_shape,
                   jax.ShapeDtypeStruct((1, bw), F32)),
        compiler_params=_params(("arbitrary", "arbitrary")),
    )(*ins)


def _mla_k_raw(kv, kr):
    lane = lax.broadcasted_iota(jnp.int32, kv.shape, 1)
    return jnp.where(lane < MLA_NOPE, kv, jnp.where(lane < MLA_QK, pltpu.roll(kr, MLA_NOPE, 1), 0.0))


def _mla_k_fwd(kv_lin, z, kr_col, gain, tabs, name):
    T, W = kv_lin.shape
    tr = _row_tile(T, LANES)
    krb = kr_col // LANES
    inv_d = 1.0 / MLA_QK

    def body(kv_ref, kr_ref, g_ref, c_ref, s1_ref, s2_ref, o_ref):
        gain_v = g_ref[...]

        def step(rows, carry):
            xv = _mla_k_raw(kv_ref[rows, :], kr_ref[rows, :])
            r = lax.rsqrt(jnp.sum(xv * xv, axis=-1, keepdims=True) * inv_d + EPS)
            o_ref[rows, :] = _rope_fwd(xv * r * gain_v, c_ref[rows, :], s1_ref[rows, :],
                                       s2_ref[rows, :]).astype(BF16)
            return carry

        _by_chunks(tr, LANES, step)

    blk = pl.BlockSpec((tr, LANES), lambda i, h: (i, h))
    tab = pl.BlockSpec((tr, LANES), lambda i, h: (i, 0))
    return pl.pallas_call(
        body, name=name, grid=(T // tr, W // LANES),
        in_specs=[blk, pl.BlockSpec((tr, LANES), lambda i, h: (i, krb)),
                  pl.BlockSpec((1, LANES), lambda i, h: (0, 0)), tab, tab, tab],
        out_specs=blk, out_shape=jax.ShapeDtypeStruct((T, W), BF16),
        compiler_params=_params(("parallel", "parallel")),
    )(kv_lin, z, gain, *tabs)


def _mla_k_bwd(kv_lin, z, kr_col, gain, tabs, dk, dvk, name):
    T, W = kv_lin.shape
    tr = _row_tile(T, LANES)
    krb = kr_col // LANES
    inv_d = 1.0 / MLA_QK

    def body(kv_ref, kr_ref, g_ref, c_ref, s1_ref, s2_ref, dk_ref, dvk_ref, dkv_ref, dkr_ref, dg_ref):
        h = pl.program_id(1)
        gain_v = g_ref[...]

        @pl.when(h == 0)
        def _():
            dkr_ref[...] = jnp.zeros_like(dkr_ref)

        def step(rows, dg_sum):
            xv = _mla_k_raw(kv_ref[rows, :], kr_ref[rows, :])
            dyv = _rope_bwd(dk_ref[rows, :], c_ref[rows, :], s1_ref[rows, :], s2_ref[rows, :])
            dx, dg = _norm_bwd_math(xv, gain_v, dyv, LANES, inv_d)
            lane = lax.broadcasted_iota(jnp.int32, dx.shape, 1)
            dkv_ref[rows, :] = jnp.where(lane < MLA_NOPE, dx, dvk_ref[rows, :]).astype(BF16)
            dkr_ref[rows, :] += pltpu.roll(jnp.where((lane >= MLA_NOPE) & (lane < MLA_QK), dx, 0.0),
                                           LANES - MLA_NOPE, 1)
            return dg_sum + dg

        dg = _by_chunks(tr, LANES, step, jnp.zeros((1, LANES), F32))

        @pl.when((pl.program_id(0) == 0) & (h == 0))
        def _():
            dg_ref[...] = jnp.zeros_like(dg_ref)

        dg_ref[...] += dg

    blk = pl.BlockSpec((tr, LANES), lambda i, h: (i, h))
    tab = pl.BlockSpec((tr, LANES), lambda i, h: (i, 0))
    one = pl.BlockSpec((1, LANES), lambda i, h: (0, 0))
    return pl.pallas_call(
        body, name=name, grid=(T // tr, W // LANES),
        in_specs=[blk, pl.BlockSpec((tr, LANES), lambda i, h: (i, krb)), one, tab, tab, tab, blk, blk],
        out_specs=(blk, tab, one),
        out_shape=(jax.ShapeDtypeStruct((T, W), BF16), jax.ShapeDtypeStruct((T, LANES), F32),
                   jax.ShapeDtypeStruct((1, LANES), F32)),
        compiler_params=_params(("arbitrary", "arbitrary")),
    )(kv_lin, z, gain, *tabs, dk, dvk)


def _ffn_up(u, w_gu, name, exch=None):
    T, D = u.shape
    F = w_gu.shape[1] // 2
    tm, tn = _tile(T, 640, 16), _tile(F, 1408, 128)
    nj = F // tn

    def body(u_ref, wg_ref, wu_ref, sa_ref, sb_ref, a_ref):
        uv = u_ref[...]
        g = jnp.dot(uv, wg_ref[...], preferred_element_type=F32)
        up = jnp.dot(uv, wu_ref[...], preferred_element_type=F32)
        sg = jax.nn.sigmoid(g)
        silu = g * sg
        sa_ref[...] = silu.astype(BF16)
        sb_ref[...] = (up * (sg + silu * (1.0 - sg))).astype(BF16)
        a_ref[...] = (silu * up).astype(BF16)

    o_spec = pl.BlockSpec((tm, tn), lambda j, i: (i, j))
    sh = jax.ShapeDtypeStruct((T, F), BF16)
    return _call(
        body, name, (nj, T // tm),
        [pl.BlockSpec((tm, D), lambda j, i: (i, 0)),
         pl.BlockSpec((D, tn), lambda j, i: (0, j)),
         pl.BlockSpec((D, tn), lambda j, i: (0, j + nj))],
        [o_spec, o_spec, o_spec], [sh, sh, sh], [u, w_gu, w_gu],
        sem=("parallel", "parallel"), exch=exch)


def _ffn_down_bwd(dh, w_down, sa, sb, name, exch=None):
    T, D = dh.shape
    F = w_down.shape[0]
    tm, tn = _tile(T, 640, 16), _tile(F, 1408, 128)

    def body(dh_ref, w_ref, sa_ref, sb_ref, dg_ref, dup_ref):
        da = 0.5 * lax.dot_general(dh_ref[...].astype(BF16), w_ref[...], (((1,), (1,)), ((), ())),
                                   preferred_element_type=F32)
        dup_ref[...] = (da * sa_ref[...].astype(F32)).astype(BF16)
        dg_ref[...] = (da * sb_ref[...].astype(F32)).astype(BF16)

    t_spec = pl.BlockSpec((tm, tn), lambda j, i: (i, j))
    sh = jax.ShapeDtypeStruct((T, F), BF16)
    return _call(
        body, name, (F // tn, T // tm),
        [pl.BlockSpec((tm, D), lambda j, i: (i, 0)),
         pl.BlockSpec((tn, D), lambda j, i: (j, 0)), t_spec, t_spec],
        [t_spec, t_spec], [sh, sh], [dh, w_down, sa, sb],
        sem=("parallel", "parallel"), exch=exch)


def _ffn_up_bwd_dx(dg, dup, w_gu, name, exch=None):
    T, F = dg.shape
    D = w_gu.shape[0]
    tm, tk = _tile(T, 640, 16), _tile(F, 2816, 128)
    nk = F // tk
    nt = (((1,), (1,)), ((), ()))

    def body(dg_ref, dup_ref, wg_ref, wu_ref, o_ref, acc_ref):
        k = pl.program_id(1)
        prod = (lax.dot_general(dg_ref[...], wg_ref[...], nt, preferred_element_type=F32)
                + lax.dot_general(dup_ref[...], wu_ref[...], nt, preferred_element_type=F32))
        if nk == 1:
            o_ref[...] = prod
            return

        @pl.when(k == 0)
        def _():
            acc_ref[...] = prod

        @pl.when((k > 0) & (k < nk - 1))
        def _():
            acc_ref[...] += prod

        @pl.when(k == nk - 1)
        def _():
            o_ref[...] = acc_ref[...] + prod

    return _call(
        body, name, (T // tm, nk),
        [pl.BlockSpec((tm, tk), lambda i, k: (i, k)),
         pl.BlockSpec((tm, tk), lambda i, k: (i, k)),
         pl.BlockSpec((D, tk), lambda i, k: (0, k)),
         pl.BlockSpec((D, tk), lambda i, k: (0, k + nk))],
        [pl.BlockSpec((tm, D), lambda i, k: (i, 0))], [jax.ShapeDtypeStruct((T, D), F32)],
        [dg, dup, w_gu, w_gu], scratch_shapes=[pltpu.VMEM((tm, D), F32)],
        sem=("parallel", "arbitrary"), exch=exch)


def _logsig(x):
    return jnp.minimum(x, 0.0) - jnp.log(1.0 + jnp.exp(-jnp.abs(x)))


def _cum_fwd(fl, flm, bf, name):
    B, S, _ = fl.shape
    nb = S // LANES

    def body(fl_ref, flm_ref, bf_ref, cum_ref, cumm_ref):
        rows = lax.broadcasted_iota(jnp.int32, (LANES, LANES), 0)
        cols = lax.broadcasted_iota(jnp.int32, (LANES, LANES), 1)
        tri = (rows >= cols).astype(F32)
        bias = bf_ref[...]
        lfm = jnp.where(rows < N_META, _logsig(flm_ref[...] + bias), 0.0)
        cm = jnp.dot(tri, lfm, precision=lax.Precision.HIGHEST, preferred_element_type=F32)
        cumm_ref[...] = cm * LOG2E
        base = cm[LANES - 1:LANES, :]
        for b in range(B):
            def blk(i, carry):
                r0 = pl.multiple_of(i * LANES, LANES)
                lf = _logsig(fl_ref[b, pl.ds(r0, LANES), :] + bias)
                c = jnp.dot(tri, lf, precision=lax.Precision.HIGHEST,
                            preferred_element_type=F32) + carry
                cum_ref[b, pl.ds(r0, LANES), :] = c * LOG2E
                return c[LANES - 1:LANES, :]

            lax.fori_loop(0, nb, blk, base)

    return pl.pallas_call(
        body, name=name,
        out_shape=(jax.ShapeDtypeStruct((B, S, LANES), F32),
                   jax.ShapeDtypeStruct((LANES, LANES), F32)),
        compiler_params=_params(),
    )(fl, flm, bf)


def _cum_bwd(dc, dcm, fl, flm, bf, name):
    B, S, _ = fl.shape
    nb = S // LANES

    def body(dc_ref, dcm_ref, fl_ref, flm_ref, bf_ref, dfl_ref, dflm_ref, dbf_ref):
        rows = lax.broadcasted_iota(jnp.int32, (LANES, LANES), 0)
        cols = lax.broadcasted_iota(jnp.int32, (LANES, LANES), 1)
        triu = (rows <= cols).astype(F32)
        bias = bf_ref[...]
        total = jnp.zeros((1, LANES), F32)
        dbf = jnp.zeros((1, LANES), F32)
        for b in range(B):
            tail = jnp.zeros((1, LANES), F32)
            for t in range(nb):
                r0 = (nb - 1 - t) * LANES
                rc = jnp.dot(triu, dc_ref[b, r0:r0 + LANES, :], precision=lax.Precision.HIGHEST,
                             preferred_element_type=F32) + tail
                xv = fl_ref[b, r0:r0 + LANES, :] + bias
                d = rc / (1.0 + jnp.exp(xv))
                dfl_ref[b, r0:r0 + LANES, :] = d
                tail = rc[0:1, :]
                dbf = dbf + jnp.sum(d, axis=0, keepdims=True)
            total = total + tail
        rcm = jnp.dot(triu, dcm_ref[...], precision=lax.Precision.HIGHEST,
                      preferred_element_type=F32) + total
        dm = jnp.where(rows < N_META, rcm / (1.0 + jnp.exp(flm_ref[...] + bias)), 0.0)
        dflm_ref[...] = dm
        dbf_ref[...] = dbf + jnp.sum(dm, axis=0, keepdims=True)

    return pl.pallas_call(
        body, name=name,
        out_shape=(jax.ShapeDtypeStruct((B, S, LANES), F32),
                   jax.ShapeDtypeStruct((LANES, LANES), F32),
                   jax.ShapeDtypeStruct((1, LANES), F32)),
        compiler_params=_params(),
    )(dc, dcm, fl, flm, bf)


_NT = (((1,), (1,)), ((), ()))


def _token_rows_buffer(T, NX, width):
    return lax.dynamic_update_slice(lax.empty((T, width), F32), jnp.zeros((T - NX, width), F32), (NX, 0))


def _attn_specs(S, NX, qw, v_col0):
    mb = NX // META_BLK
    vb = v_col0 // qw
    return (pl.BlockSpec((S, qw), lambda b, p: (b, p)),
            pl.BlockSpec((META_BLK, qw), lambda b, p: (mb, p)),
            pl.BlockSpec((S, qw), lambda b, p: (b, vb + p)),
            pl.BlockSpec((META_BLK, qw), lambda b, p: (mb, vb + p)),
            pl.BlockSpec((S, LANES), lambda b, p: (b, p)))


def _cum_specs(S, TK):
    return [pl.BlockSpec((1, 2, S // TK, 1, TK), lambda b, p: (b, p, 0, 0, 0)),
            pl.BlockSpec((1, 2, 1, META_BLK), lambda b, p: (b, p, 0, 0))]


LOG2E = 1.4426950408889634


def _attn_fwd(qn, kn, vsrc, v_col0, mla, scale, S, NX, name, ck=None, cmk=None, exch=None):
    T = qn.shape[0]
    B = NX // S
    qw = 2 * LANES if mla else LANES
    npair = qn.shape[1] // qw
    TQ = min(512, S)
    TK = TQ
    forget = ck is not None
    a = scale * LOG2E

    def body(*refs):
        if forget:
            q_ref, k_ref, km_ref, v_ref, vm_ref, ck_ref, cmk_ref, _, o_ref, lse_ref = refs
        else:
            q_ref, k_ref, km_ref, v_ref, vm_ref, _, o_ref, lse_ref = refs
        lo = lax.broadcasted_iota(jnp.int32, (1, LANES), 1) < HALF
        mcol = lax.broadcasted_iota(jnp.int32, (TQ, META_BLK), 1)
        causal = (lax.broadcasted_iota(jnp.int32, (TQ, TK), 0)
                  >= lax.broadcasted_iota(jnp.int32, (TQ, TK), 1))
        two = lax.broadcasted_iota(jnp.int32, (TQ, 2), 1)
        for qi in range(S // TQ):
            q0 = qi * TQ
            sls = [slice(e * LANES, (e + 1) * LANES) if mla else slice(None) for e in range(2)]
            if mla:
                qts = [q_ref[q0:q0 + TQ, sl] for sl in sls]
            else:
                qts = [jnp.where(lo if e == 0 else ~lo, q_ref[q0:q0 + TQ, :], 0.0).astype(BF16)
                       for e in range(2)]

            def step(e, kt, vt, c2, mask, carry):
                m, l, acc = carry
                s = lax.dot_general(qts[e], kt, _NT, preferred_element_type=F32) * a
                if forget:
                    s = s - c2
                if mask is not None:
                    s = jnp.where(mask, s, NEG)
                m2 = jnp.max(s, axis=1, keepdims=True)
                if m is not None:
                    m2 = jnp.maximum(m, m2)
                p = jnp.exp2(s - m2)
                l2 = jnp.sum(p, axis=1, keepdims=True)
                acc2 = jnp.dot(p.astype(BF16), vt.astype(BF16), preferred_element_type=F32)
                if m is not None:
                    alpha = jnp.exp2(m - m2)
                    l2, acc2 = alpha * l + l2, alpha * acc + acc2
                return m2, l2, acc2

            def both(rows, kj, mask, carry):
                return tuple(step(e, k_ref[rows, sls[e]], v_ref[rows, sls[e]],
                                  ck_ref[0, e, kj] if forget else None, mask, carry[e]) for e in range(2))

            def below(kj, carry):
                return both(pl.ds(pl.multiple_of(kj * TK, TK), TK), kj, None, carry)

            carry = tuple(step(e, km_ref[:, sls[e]], vm_ref[:, sls[e]], cmk_ref[0, e] if forget else None,
                               mcol < N_META, (None, None, None)) for e in range(2))
            if qi:
                carry = lax.fori_loop(0, qi, below, carry)
            carry = both(slice(q0, q0 + TK), qi, causal, carry)
            outs = [acc / l for _, l, acc in carry]
            lses = [m + jnp.log2(l) for m, l, _ in carry]
            first = pltpu.roll(outs[0], HALF, 1) if mla else outs[0]
            o_ref[q0:q0 + TQ, :] = jnp.where(lo, first, outs[1])
            lse_ref[0, 0, q0:q0 + TQ, :] = jnp.where(two == 0, lses[0], lses[1])

    qk, kmeta, vv, vmeta, pair = _attn_specs(S, NX, qw, v_col0)
    specs = [qk, qk, kmeta, vv, vmeta]
    ins = [qn, kn, kn, vsrc, vsrc]
    if forget:
        specs += _cum_specs(S, TK)
        ins += [ck, cmk]
    specs.append(pl.BlockSpec(memory_space=pl.ANY))
    ins.append(_token_rows_buffer(T, NX, npair * LANES))
    lse_spec = pl.BlockSpec((1, 1, S, 2), lambda b, p: (b, p, 0, 0))
    return _call(
        body, name, (B, npair), specs, [pair, lse_spec],
        [jax.ShapeDtypeStruct((T, npair * LANES), F32), jax.ShapeDtypeStruct((B, npair, S, 2), F32)],
        ins, sem=("parallel", "parallel"), aliases={len(ins) - 1: 0}, exch=exch)


def _attn_bwd(qn, kn, vsrc, v_col0, o, lse, do, mla, scale, S, NX, name, ck=None, cmk=None, exch=None):
    T, W = qn.shape
    B = NX // S
    qw = 2 * LANES if mla else LANES
    npair = W // qw
    TQ = min(512, S)
    TK = TQ
    forget = ck is not None
    a = scale * LOG2E
    _TN = (((0,), (0,)), ((), ()))

    def body(*refs):
        refs = list(refs)
        q_ref, k_ref, km_ref, v_ref, vm_ref, o_ref, do_ref, lse_ref = refs[:8]
        pos = 8
        if forget:
            ck_ref, cmk_ref = refs[8:10]
            pos = 10
        pos += 3
        dq_ref, dk_ref, dv_ref, dkm_ref, dvm_ref = refs[pos:pos + 5]
        if forget:
            dck_ref, dcm_ref, dcq_ref = refs[pos + 5:pos + 8]
            dck_ref[...] = jnp.zeros_like(dck_ref)
            dcm_ref[...] = jnp.zeros_like(dcm_ref)
        dk_ref[...] = jnp.zeros_like(dk_ref)
        dv_ref[...] = jnp.zeros_like(dv_ref)
        dkm_ref[...] = jnp.zeros_like(dkm_ref)
        dvm_ref[...] = jnp.zeros_like(dvm_ref)
        lo = lax.broadcasted_iota(jnp.int32, (1, LANES), 1) < HALF
        mcol = lax.broadcasted_iota(jnp.int32, (TQ, META_BLK), 1)
        causal = (lax.broadcasted_iota(jnp.int32, (TQ, TK), 0)
                  >= lax.broadcasted_iota(jnp.int32, (TQ, TK), 1))
        two = lax.broadcasted_iota(jnp.int32, (TQ, 2), 1)
        for qi in range(S // TQ):
            q0 = qi * TQ
            dof = do_ref[q0:q0 + TQ, :]
            prod = dof * o_ref[q0:q0 + TQ, :]
            lse2 = lse_ref[0, 0, q0:q0 + TQ, :]
            sls = [slice(e * LANES, (e + 1) * LANES) if mla else slice(None) for e in range(2)]
            mine = [lo, ~lo]
            if mla:
                qts = [q_ref[q0:q0 + TQ, sl] for sl in sls]
                dots = [jnp.where(lo, 0.0, pltpu.roll(dof, HALF, 1) if e == 0 else dof).astype(BF16)
                        for e in range(2)]
            else:
                qts = [jnp.where(mine[e], q_ref[q0:q0 + TQ, :], 0.0).astype(BF16) for e in range(2)]
                dots = [jnp.where(mine[e], dof, 0.0).astype(BF16) for e in range(2)]
            deltas = [jnp.sum(jnp.where(mine[e], prod, 0.0), axis=1, keepdims=True) for e in range(2)]
            lse_ts = [jnp.sum(jnp.where(two == e, lse2, 0.0), axis=1, keepdims=True) for e in range(2)]

            def grads(e, kt, vt, c2, mask):
                s = lax.dot_general(qts[e], kt, _NT, preferred_element_type=F32) * a
                if forget:
                    s = s - c2
                p = jnp.exp2(s - lse_ts[e])
                if mask is not None:
                    p = jnp.where(mask, p, 0.0)
                dp = lax.dot_general(dots[e], vt, _NT, preferred_element_type=F32)
                ds = p * (dp - deltas[e])
                dsb = ds.astype(BF16)
                return (jnp.dot(dsb, kt, preferred_element_type=F32),
                        lax.dot_general(dsb, qts[e], _TN, preferred_element_type=F32) * scale,
                        lax.dot_general(p.astype(BF16), dots[e], _TN, preferred_element_type=F32),
                        -jnp.sum(ds, axis=0, keepdims=True) if forget else None,
                        jnp.sum(ds, axis=1, keepdims=True) if forget else None)

            def block(k_at, v_at, dk_at, dv_at, c_at, dc_at, mask, dqs):
                got = [grads(e, k_at(sls[e]), v_at(sls[e]).astype(BF16), c_at(e) if forget else None, mask)
                       for e in range(2)]
                if mla:
                    for e in range(2):
                        dk_at(sls[e], got[e][1])
                        dv_at(sls[e], got[e][2])
                else:
                    dk_at(sls[0], got[0][1] + got[1][1])
                    dv_at(sls[0], got[0][2] + got[1][2])
                if forget:
                    for e in range(2):
                        dc_at(e, got[e][3])
                picks = (0, 0, 4, 4) if forget else (0, 0)
                new = tuple(got[i % 2][k] for i, k in enumerate(picks))
                return new if dqs is None else tuple(x + y for x, y in zip(dqs, new))

            def add_to(ref, *lead):
                def add(*idx_and_val):
                    *idx, val = idx_and_val
                    ref[(*lead, *idx)] += val
                return add

            def token_block(rows, kj, mask, dqs):
                return block(lambda sl: k_ref[rows, sl], lambda sl: v_ref[rows, sl],
                             lambda sl, val: add_to(dk_ref)(rows, sl, val),
                             lambda sl, val: add_to(dv_ref)(rows, sl, val),
                             lambda e: ck_ref[0, e, kj], lambda e, val: add_to(dck_ref, 0)(e, kj, val),
                             mask, dqs)

            dqs = block(lambda sl: km_ref[:, sl], lambda sl: vm_ref[:, sl],
                        lambda sl, val: add_to(dkm_ref, 0)(slice(None), sl, val),
                        lambda sl, val: add_to(dvm_ref, 0)(slice(None), sl, val),
                        lambda e: cmk_ref[0, e], lambda e, val: add_to(dcm_ref, 0)(e, val),
                        mcol < N_META, None)

            def below(kj, dqs):
                return token_block(pl.ds(pl.multiple_of(kj * TK, TK), TK), kj, None, dqs)

            if qi:
                dqs = lax.fori_loop(0, qi, below, dqs)
            dqs = token_block(slice(q0, q0 + TK), qi, causal, dqs)
            if forget:
                dcq_ref[0, 0, q0:q0 + TQ, :] = jnp.where(two == 0, dqs[2], dqs[3])
            if mla:
                for e in range(2):
                    dq_ref[q0:q0 + TQ, sls[e]] = dqs[e] * scale
            else:
                dq_ref[q0:q0 + TQ, :] = jnp.where(lo, dqs[0], dqs[1]) * scale

    qk, kmeta, vv, vmeta, pair = _attn_specs(S, NX, qw, v_col0)
    lse_spec = pl.BlockSpec((1, 1, S, 2), lambda b, p: (b, p, 0, 0))
    specs = [qk, qk, kmeta, vv, vmeta, pair, pair, lse_spec]
    ins = [qn, kn, kn, vsrc, vsrc, o, do, lse]
    if forget:
        specs += _cum_specs(S, TK)
        ins += [ck, cmk]
    first_alias = len(ins)
    specs += [pl.BlockSpec(memory_space=pl.ANY)] * 3
    ins += [_token_rows_buffer(T, NX, W) for _ in range(3)]
    mspec = pl.BlockSpec((1, META_BLK, qw), lambda b, p: (b, 0, p))
    out_specs = [qk, qk, qk, mspec, mspec]
    tok = jax.ShapeDtypeStruct((T, W), F32)
    met = jax.ShapeDtypeStruct((B, META_BLK, W), F32)
    out_shape = [tok, tok, tok, met, met]
    if forget:
        out_specs += _cum_specs(S, TK) + [lse_spec]
        out_shape += [jax.ShapeDtypeStruct((B, HEADS, S // TK, 1, TK), F32),
                      jax.ShapeDtypeStruct((B, HEADS, 1, META_BLK), F32),
                      jax.ShapeDtypeStruct((B, npair, S, 2), F32)]
    return _call(
        body, name, (B, npair), specs, out_specs, out_shape, ins, sem=("parallel", "parallel"),
        aliases={first_alias: 0, first_alias + 1: 1, first_alias + 2: 2}, exch=exch)


def _gate_fwd(z, bg, of, om, name):
    T, D = of.shape
    tm = _tile(T, 640, 16)

    def body(z_ref, bg_ref, of_ref, om_ref, o_ref):
        bias = bg_ref[...]

        def step(rows, carry):
            gt = jax.nn.sigmoid(z_ref[rows, :] + bias)
            o_ref[rows, :] = (gt[:, :D] * of_ref[rows, :] + gt[:, D:] * om_ref[rows, :]).astype(BF16)
            return carry

        _by_chunks(tm, D, step)

    row = pl.BlockSpec((tm, D), lambda i: (i, 0))
    return pl.pallas_call(
        body, name=name, grid=(T // tm,),
        in_specs=[pl.BlockSpec((tm, 2 * D), lambda i: (i, 0)),
                  pl.BlockSpec((1, 2 * D), lambda i: (0, 0)), row, row],
        out_specs=row, out_shape=jax.ShapeDtypeStruct((T, D), BF16),
        compiler_params=_params(("parallel",)),
    )(z, bg, of, om)


def _gate_bwd(dmix, z, bg, of, om, name):
    T, D = of.shape
    tm = _tile(T, 640, 16)

    def body(dm_ref, z_ref, bg_ref, of_ref, om_ref, dgl_ref, dof_ref, dom_ref, dbg_ref):
        bias = bg_ref[...]

        def step(rows, dbg_sum):
            gt = jax.nn.sigmoid(z_ref[rows, :] + bias)
            dm = dm_ref[rows, :]
            dof_ref[rows, :] = (dm * gt[:, :D]).astype(BF16)
            dom_ref[rows, :] = (dm * gt[:, D:]).astype(BF16)
            dgl = jnp.concatenate([dm * of_ref[rows, :], dm * om_ref[rows, :]], axis=1) * gt * (1.0 - gt)
            dgl_ref[rows, :] = dgl.astype(BF16)
            return dbg_sum + jnp.sum(dgl, axis=0, keepdims=True)

        dbg = _by_chunks(tm, D, step, jnp.zeros((1, 2 * D), F32))

        @pl.when(pl.program_id(0) == 0)
        def _():
            dbg_ref[...] = jnp.zeros_like(dbg_ref)

        dbg_ref[...] += dbg

    row = pl.BlockSpec((tm, D), lambda i: (i, 0))
    wide = pl.BlockSpec((tm, 2 * D), lambda i: (i, 0))
    one = pl.BlockSpec((1, 2 * D), lambda i: (0, 0))
    return pl.pallas_call(
        body, name=name, grid=(T // tm,),
        in_specs=[row, wide, one, row, row], out_specs=(wide, row, row, one),
        out_shape=(jax.ShapeDtypeStruct((T, 2 * D), BF16), jax.ShapeDtypeStruct((T, D), BF16),
                   jax.ShapeDtypeStruct((T, D), BF16), jax.ShapeDtypeStruct((1, 2 * D), F32)),
        compiler_params=_params(("arbitrary",)),
    )(dmix, z, bg, of, om)


def _loss(h, tgt, n_valid, name):
    T, D = h.shape
    tm = _tile(T, 640, 16)

    def body(h_ref, t_ref, dh_ref, dhb_ref, l_ref):
        i = pl.program_id(0)
        row_in_block = lax.broadcasted_iota(jnp.int32, (tm, D), 0)

        def step(rows, part):
            err = jnp.where(row_in_block + i * tm < n_valid, h_ref[rows, :] - t_ref[rows, :], 0.0)
            dh = err * (1.0 / D)
            dh_ref[rows, :] = dh
            dhb_ref[rows, :] = dh.astype(BF16)
            return part + jnp.sum(err * err, axis=0, keepdims=True)

        part = _by_chunks(tm, D, step, jnp.zeros((1, D), F32))

        @pl.when(i == 0)
        def _():
            l_ref[...] = jnp.zeros_like(l_ref)

        l_ref[...] += 0.5 * jnp.sum(part) * (1.0 / D)

    row = pl.BlockSpec((tm, D), lambda i: (i, 0))
    acc = pl.BlockSpec((8, LANES), lambda i: (0, 0))
    return pl.pallas_call(
        body, name=name, grid=(T // tm,), in_specs=[row, row], out_specs=(row, row, acc),
        out_shape=(jax.ShapeDtypeStruct((T, D), F32), jax.ShapeDtypeStruct((T, D), BF16),
                   jax.ShapeDtypeStruct((8, LANES), F32)),
        compiler_params=_params(("arbitrary",)),
    )(h, tgt)


def _adamw(parts, w, m, v, name):
    P, R, C = parts.shape
    tr = _tile(R, max(8, (1 << 18) // C), 8)
    bc1 = 1.0 - ADAM_B1 ** ADAM_STEP
    bc2 = 1.0 - ADAM_B2 ** ADAM_STEP

    def body(p_ref, w_ref, m_ref, v_ref, g_ref, d_ref, m2_ref, v2_ref):
        def step(rows, carry):
            g = p_ref[0, rows, :].astype(F32)
            for j in range(1, P):
                g = g + p_ref[j, rows, :].astype(F32)
            m2 = ADAM_B1 * m_ref[rows, :] + (1.0 - ADAM_B1) * g
            v2 = ADAM_B2 * v_ref[rows, :] + (1.0 - ADAM_B2) * (g * g)
            m_hat = m2 / bc1
            v_hat = v2 / bc2
            g_ref[rows, :] = g
            d_ref[rows, :] = -ADAM_LR * (m_hat / (jnp.sqrt(v_hat) + ADAM_EPS) + ADAM_WD * w_ref[rows, :])
            m2_ref[rows, :] = m2
            v2_ref[rows, :] = v2
            return carry

        _by_chunks(tr, C, step)

    row = pl.BlockSpec((tr, C), lambda i: (i, 0))
    sh = jax.ShapeDtypeStruct((R, C), F32)
    return pl.pallas_call(
        body, name=name, grid=(R // tr,),
        in_specs=[pl.BlockSpec((P, tr, C), lambda i: (0, i, 0)), row, row, row],
        out_specs=(row, row, row, row), out_shape=(sh, sh, sh, sh),
        compiler_params=_params(("parallel",)),
    )(parts, w, m, v)


def _peer(d):
    x, y, c = lax.axis_index("x"), lax.axis_index("y"), lax.axis_index("c")
    px = 1 - x if d & 4 else x
    py = 1 - y if d & 2 else y
    pc = 1 - c if d & 1 else c
    return (px, py, pc), 4 * px + 2 * py + pc


class _Exchange:
    def __init__(self, srcs, gather):
        self.srcs, self.gather, self.n = list(srcs), gather, len(srcs)
        n = self.n
        hbm = pl.BlockSpec(memory_space=pl.ANY)
        self.in_specs = [hbm] * n
        self.out_specs = [hbm] * n
        self.out_shape = [jax.ShapeDtypeStruct((N_DEV,) + s.shape[-2:], s.dtype) for s in srcs]
        self.scratch = [pltpu.SemaphoreType.DMA((N_DEV - 1, n)), pltpu.SemaphoreType.DMA((N_DEV - 1, n)),
                        pltpu.SemaphoreType.DMA((n,))]

    def _copies(self, src_refs, out_refs, sems):
        send_sems, recv_sems, local_sems = sems
        _, me = _peer(0)

        def remote(w, d, landing):
            dev, lin = _peer(d)
            return pltpu.make_async_remote_copy(
                src_ref=src_refs[w] if self.gather else src_refs[w].at[lin],
                dst_ref=out_refs[w].at[lin if landing else me],
                send_sem=send_sems.at[d - 1, w], recv_sem=recv_sems.at[d - 1, w],
                device_id=dev, device_id_type=pl.DeviceIdType.MESH)

        pairs = [(w, d) for d in range(1, N_DEV) for w in range(self.n)]
        own = [pltpu.make_async_copy(src_refs[w] if self.gather else src_refs[w].at[me],
                                     out_refs[w].at[me], local_sems.at[w]) for w in range(self.n)]
        return own, [remote(w, d, False) for w, d in pairs], [remote(w, d, True) for w, d in pairs]

    def _gather_copies(self, src_refs, out_refs, sems):
        send_sems, recv_sems, local_sems = sems
        x, y, c = lax.axis_index("x"), lax.axis_index("y"), lax.axis_index("c")
        me, sibling = (x, y, c), (x, y, 1 - c)
        chips = [(1 - x, y), (x, 1 - y), (1 - x, 1 - y)]

        def copy(w, k, block, to, src=None):
            rows = out_refs[w].at[4 * block[0] + 2 * block[1] + block[2]]
            return pltpu.make_async_remote_copy(
                src_ref=rows if src is None else src, dst_ref=rows,
                send_sem=send_sems.at[k, w], recv_sem=recv_sems.at[k, w],
                device_id=to, device_id_type=pl.DeviceIdType.MESH)

        ws = range(self.n)
        own = [pltpu.make_async_copy(src_refs[w], out_refs[w].at[4 * x + 2 * y + c], local_sems.at[w])
               for w in ws]
        first = [copy(w, 0, me, sibling, src_refs[w]) for w in ws]
        first += [copy(w, 1 + j, me, (*chip, c), src_refs[w]) for j, chip in enumerate(chips) for w in ws]
        landed = [[copy(w, 1 + j, (*chip, c), me) for w in ws] for j, chip in enumerate(chips)]
        passed = [[copy(w, 4 + j, (*chip, c), sibling) for w in ws] for j, chip in enumerate(chips)]
        from_sibling = [copy(w, 0, sibling, me) for w in ws]
        from_sibling += [copy(w, 4 + j, (*chip, 1 - c), me) for j, chip in enumerate(chips) for w in ws]
        return own, first, landed, passed, from_sibling

    def start(self, src_refs, out_refs, sems):
        if self.gather:
            own, first = self._gather_copies(src_refs, out_refs, sems)[:2]
            sent = first
        else:
            own, sent, _ = self._copies(src_refs, out_refs, sems)
        for cp in own + sent:
            cp.start()

    def wait(self, src_refs, out_refs, sems):
        if self.gather:
            own, first, landed, passed, from_sibling = self._gather_copies(src_refs, out_refs, sems)
            for arrived, onward in zip(landed, passed):
                for cp in arrived:
                    cp.wait_recv()
                for cp in onward:
                    cp.start()
            for cp in from_sibling:
                cp.wait_recv()
            for cp in first + [cp for group in passed for cp in group]:
                cp.wait_send()
        else:
            own, sent, landing = self._copies(src_refs, out_refs, sems)
            for cp in landing:
                cp.wait_recv()
            for cp in sent:
                cp.wait_send()
        for cp in own:
            cp.wait()


def _exchange(srcs, name, gather):
    ex = _Exchange(srcs, gather)
    n = ex.n

    def body(*refs):
        ex.start(refs[:n], refs[n:2 * n], refs[2 * n:])
        ex.wait(refs[:n], refs[n:2 * n], refs[2 * n:])

    outs = pl.pallas_call(
        body, name=name, in_specs=ex.in_specs, out_specs=tuple(ex.out_specs),
        out_shape=tuple(ex.out_shape), scratch_shapes=ex.scratch,
    )(*srcs)
    return list(outs)


def _call(body, name, grid, in_specs, out_specs, out_shape, ins, scratch_shapes=(), sem=None,
          aliases=None, exch=None):
    aliases = aliases or {}
    if exch is None:
        outs = pl.pallas_call(
            body, name=name, grid=grid, in_specs=list(in_specs), out_specs=tuple(out_specs),
            out_shape=tuple(out_shape), scratch_shapes=list(scratch_shapes),
            input_output_aliases=aliases, compiler_params=_params(sem),
        )(*ins)
        return list(outs), []
    ni, no, ns, n = len(in_specs), len(out_specs), len(scratch_shapes), exch.n
    last_ids = [g - 1 for g in grid]

    def hosted(*refs):
        cin, xin = refs[:ni], refs[ni:ni + n]
        cout, xout = refs[ni + n:ni + n + no], refs[ni + n + no:ni + 2 * n + no]
        cscr, xsem = refs[ni + 2 * n + no:ni + 2 * n + no + ns], refs[ni + 2 * n + no + ns:]
        ids = [pl.program_id(a) for a in range(len(grid))]
        first, last = ids[0] == 0, ids[0] == last_ids[0]
        for a in range(1, len(grid)):
            first, last = first & (ids[a] == 0), last & (ids[a] == last_ids[a])

        @pl.when(first)
        def _():
            exch.start(xin, xout, xsem)

        body(*cin, *cout, *cscr)

        @pl.when(last)
        def _():
            exch.wait(xin, xout, xsem)

    outs = pl.pallas_call(
        hosted, name=name, grid=grid, in_specs=list(in_specs) + exch.in_specs,
        out_specs=tuple(list(out_specs) + exch.out_specs),
        out_shape=tuple(list(out_shape) + exch.out_shape),
        scratch_shapes=list(scratch_shapes) + exch.scratch, input_output_aliases=aliases,
        compiler_params=_params(("arbitrary",) * len(grid)),
    )(*ins, *exch.srcs)
    return list(outs[:no]), list(outs[no:])


def _pack(arrs, cols, row_mult):
    flat = jnp.concatenate([a.reshape(-1) for a in arrs])
    n = flat.shape[0]
    quantum = cols * row_mult
    total = -(-n // quantum) * quantum
    return jnp.pad(flat, (0, total - n)).reshape(total // cols, cols)


def _pack_rows(arrs, cols, row_mult):
    flat = jnp.concatenate(arrs, axis=1)
    n = flat.shape[1]
    quantum = cols * row_mult
    total = -(-n // quantum) * quantum
    return jnp.pad(flat, ((0, 0), (0, total - n))).reshape(N_DEV, total // cols, cols)


def _unpack(packed, shapes):
    flat = packed.reshape(-1)
    out, off = [], 0
    for s in shapes:
        n = int(np.prod(s))
        out.append(flat[off:off + n].reshape(s))
        off += n
    return out


def _rope_tables(positions):
    inv_freq = ROPE_THETA ** (-jnp.arange(0, MLA_ROPE, 2, dtype=F32) / MLA_ROPE)
    ang = positions.astype(F32)[:, None] * inv_freq[None, :]
    cos, sin = jnp.cos(ang), jnp.sin(ang)
    n = positions.shape[0]
    ones, zeros = jnp.ones((n, MLA_NOPE), F32), jnp.zeros((n, MLA_NOPE), F32)
    tail1, tail0 = jnp.ones((n, LANES - MLA_QK), F32), jnp.zeros((n, LANES - MLA_QK), F32)
    z16 = jnp.zeros((n, 16), F32)
    c = jnp.concatenate([ones, cos, cos, tail1], axis=1)
    s1 = jnp.concatenate([zeros, -sin, z16, tail0], axis=1)
    s2 = jnp.concatenate([zeros, z16, sin, tail0], axis=1)
    return c, s1, s2


def kernel(x, meta_tokens, ffn1_norm, ffn1_w_gu, ffn1_w_down, mix_norm, w_in, b_forget, b_gate, fox_q_norm, fox_k_norm, mla_cq_norm, mla_w_uq, mla_ckv_norm, mla_w_ukv, mla_q_norm, mla_k_norm, w_branch_fox, w_branch_mla, w_out, ffn2_norm, ffn2_w_gu, ffn2_w_down, loss_target, m_meta_tokens, m_ffn1_norm, m_ffn1_w_gu, m_ffn1_w_down, m_mix_norm, m_w_in, m_b_forget, m_b_gate, m_fox_q_norm, m_fox_k_norm, m_mla_cq_norm, m_mla_w_uq, m_mla_ckv_norm, m_mla_w_ukv, m_mla_q_norm, m_mla_k_norm, m_w_branch_fox, m_w_branch_mla, m_w_out, m_ffn2_norm, m_ffn2_w_gu, m_ffn2_w_down, v_meta_tokens, v_ffn1_norm, v_ffn1_w_gu, v_ffn1_w_down, v_mix_norm, v_w_in, v_b_forget, v_b_gate, v_fox_q_norm, v_fox_k_norm, v_mla_cq_norm, v_mla_w_uq, v_mla_ckv_norm, v_mla_w_ukv, v_mla_q_norm, v_mla_k_norm, v_w_branch_fox, v_w_branch_mla, v_w_out, v_ffn2_norm, v_ffn2_w_gu, v_ffn2_w_down):
    names = ["meta_tokens", "ffn1_norm", "ffn1_w_gu", "ffn1_w_down", "mix_norm", "w_in", "b_forget",
             "b_gate", "fox_q_norm", "fox_k_norm", "mla_cq_norm", "mla_w_uq", "mla_ckv_norm",
             "mla_w_ukv", "mla_q_norm", "mla_k_norm", "w_branch_fox", "w_branch_mla", "w_out",
             "ffn2_norm", "ffn2_w_gu", "ffn2_w_down"]
    W = dict(zip(names, [meta_tokens, ffn1_norm, ffn1_w_gu, ffn1_w_down, mix_norm, w_in, b_forget,
                         b_gate, fox_q_norm, fox_k_norm, mla_cq_norm, mla_w_uq, mla_ckv_norm,
                         mla_w_ukv, mla_q_norm, mla_k_norm, w_branch_fox, w_branch_mla, w_out,
                         ffn2_norm, ffn2_w_gu, ffn2_w_down]))
    Mo = dict(zip(names, [m_meta_tokens, m_ffn1_norm, m_ffn1_w_gu, m_ffn1_w_down, m_mix_norm, m_w_in,
                          m_b_forget, m_b_gate, m_fox_q_norm, m_fox_k_norm, m_mla_cq_norm,
                          m_mla_w_uq, m_mla_ckv_norm, m_mla_w_ukv, m_mla_q_norm, m_mla_k_norm,
                          m_w_branch_fox, m_w_branch_mla, m_w_out, m_ffn2_norm, m_ffn2_w_gu,
                          m_ffn2_w_down]))
    Vo = dict(zip(names, [v_meta_tokens, v_ffn1_norm, v_ffn1_w_gu, v_ffn1_w_down, v_mix_norm, v_w_in,
                          v_b_forget, v_b_gate, v_fox_q_norm, v_fox_k_norm, v_mla_cq_norm,
                          v_mla_w_uq, v_mla_ckv_norm, v_mla_w_ukv, v_mla_q_norm, v_mla_k_norm,
                          v_w_branch_fox, v_w_branch_mla, v_w_out, v_ffn2_norm, v_ffn2_w_gu,
                          v_ffn2_w_down]))

    B, S, D = x.shape
    NX = B * S
    T = NX + META_BLK
    H = HEADS
    assert NX % META_BLK == 0 and S % LANES == 0
    me = 4 * lax.axis_index("x") + 2 * lax.axis_index("y") + lax.axis_index("c")

    big = [("ffn1_w_gu", 1), ("ffn1_w_down", 0), ("w_in", 1), ("mla_w_uq", 1), ("mla_w_ukv", 1),
           ("w_branch_fox", 1), ("w_branch_mla", 1), ("w_out", 0), ("ffn2_w_gu", 1), ("ffn2_w_down", 0)]
    mix_small = ["mla_w_uq", "mla_w_ukv", "w_branch_fox", "w_branch_mla", "w_out"]
    last_group = ["ffn2_w_gu", "ffn2_w_down"]
    axis_of = dict(big)
    full = {}

    def shards(group):
        return [W[n][0].astype(BF16) for n in group]

    def assemble(group, blks):
        for n, blk in zip(group, blks):
            _, r, c = blk.shape
            full[n] = (blk.transpose(1, 0, 2).reshape(r, N_DEV * c) if axis_of[n] == 1
                       else blk.reshape(N_DEV * r, c))

    got = _exchange(shards(["ffn1_w_gu"]) + [meta_tokens], "gather_first", gather=True)
    assemble(["ffn1_w_gu"], got[:1])
    meta_full = got[1].transpose(1, 0, 2).reshape(N_META, D)

    Z_G, Z_FQ = 0, 2 * D
    Z_FK, Z_FV = Z_FQ + FOX_W, Z_FQ + 2 * FOX_W
    Z_CQ = Z_FQ + 3 * FOX_W
    Z_CKV = Z_CQ + MLA_Q_RANK
    Z_F = Z_CKV + MLA_KV_RANK
    Z_KR = Z_F + LANES

    def pad_lanes(a, w=LANES):
        return jnp.pad(a, [(0, 0)] * (a.ndim - 1) + [(0, w - a.shape[-1])])

    def rows_T(real, meta=None):
        n = real.shape[1]
        parts = [real]
        used = 0
        if meta is not None:
            parts.append(meta)
            used = meta.shape[0]
        if T - NX - used:
            parts.append(jnp.zeros((T - NX - used, n), real.dtype))
        return jnp.concatenate(parts, axis=0)

    def put_meta(tok, meta_per_seq):
        return lax.dynamic_update_slice(tok, meta_per_seq.sum(0), (NX, 0))

    h0 = rows_T(x.reshape(NX, D), meta_full)
    tgt = rows_T(loss_target.reshape(NX, D))

    def ffn_fwd(h, norm, w_gu, tag, behind_up=None, behind_down=None):
        u = _norm_fwd(h, 0, D, D, norm, D, D, tag + "_norm")
        (sa, sb, a), got = _ffn_up(u, w_gu, tag + "_up",
                                  exch=_Exchange(shards(behind_up), True) if behind_up else None)
        assemble(behind_up or [], got)
        h_out = _mm(a, full[tag + "_w_down"], "nn", tag + "_down", scale=0.5, res=h,
                    exch=_Exchange(shards(behind_down), True) if behind_down else None)
        if behind_down:
            h_out, got = h_out
            assemble(behind_down, got)
        return h_out, (u, sa, sb, a)

    h1, ffn1_saved = ffn_fwd(h0, W["ffn1_norm"], full["ffn1_w_gu"], "ffn1",
                             behind_up=["ffn1_w_down"], behind_down=["w_in"])
    wi = full["w_in"]
    o_fq = 0
    o_f = 3 * FOX_W
    o_cq = o_f + HEADS
    o_kr = o_cq + MLA_Q_RANK + MLA_KV_RANK
    o_g = o_kr + MLA_ROPE
    w_in_p = jnp.concatenate([
        wi[:, o_g:o_g + 2 * D], wi[:, o_fq:o_f], wi[:, o_cq:o_kr],
        jnp.pad(wi[:, o_f:o_cq], ((0, 0), (0, LANES - HEADS))),
        jnp.pad(wi[:, o_kr:o_g], ((0, 0), (0, LANES - MLA_ROPE)))], axis=1)

    u2 = _norm_fwd(h1, 0, D, D, W["mix_norm"], D, D, "mix_norm")
    z, got = _mm(u2, w_in_p, "nn", "w_in", exch=_Exchange(shards(mix_small), True))
    assemble(mix_small, got)
    w_uq_p = jnp.pad(full["mla_w_uq"].reshape(MLA_Q_RANK, H, MLA_QK),
                     ((0, 0), (0, 0), (0, LANES - MLA_QK))).reshape(MLA_Q_RANK, H * LANES)

    gq_f = jnp.tile(W["fox_q_norm"], (1, 2))
    gk_f = jnp.tile(W["fox_k_norm"], (1, 2))
    fqn = _norm_fwd(z, Z_FQ, FOX_W, LANES, gq_f, FOX_HD, FOX_HD, "fox_q_norm")
    fkn = _norm_fwd(z, Z_FK, FOX_W, LANES, gk_f, FOX_HD, FOX_HD, "fox_k_norm")
    fl = z[:NX, Z_F:Z_F + LANES].reshape(B, S, LANES)
    flm = z[NX:, Z_F:Z_F + LANES]
    bf = pad_lanes(W["b_forget"])
    cum, cumm = _cum_fwd(fl, flm, bf, "forget_cum")
    TK = min(512, S)
    ck = cum[:, :, :H].transpose(0, 2, 1).reshape(B, H, S // TK, 1, TK)
    cmk = jnp.broadcast_to(cumm[:, :H].T[None, :, None, :], (B, H, 1, META_BLK))
    (o_fox, lse_fox), got = _attn_fwd(fqn, fkn, z, Z_FV, False, FOX_HD ** -0.5, S, NX, "fox_attn", ck, cmk,
                                      exch=_Exchange(shards(last_group), True))
    assemble(last_group, got)
    of = _mm(o_fox, full["w_branch_fox"], "nn", "branch_fox")

    pos = jnp.concatenate([jnp.tile(jnp.arange(S) + N_META, B), jnp.arange(META_BLK)])
    tabs = _rope_tables(pos)
    cqn = _norm_fwd(z, Z_CQ, MLA_Q_RANK, MLA_Q_RANK, W["mla_cq_norm"], MLA_Q_RANK, MLA_Q_RANK, "mla_cq_norm")
    q_lin = _mm(cqn, w_uq_p, "nn", "mla_uq")
    ckvn = _norm_fwd(z, Z_CKV, MLA_KV_RANK, MLA_KV_RANK, W["mla_ckv_norm"], MLA_KV_RANK, MLA_KV_RANK,
                     "mla_ckv_norm")
    kv_lin = _mm(ckvn, full["mla_w_ukv"], "nn", "mla_ukv")
    gq_m, gk_m = pad_lanes(W["mla_q_norm"]), pad_lanes(W["mla_k_norm"])
    mqn = _norm_fwd(q_lin, 0, H * LANES, LANES, gq_m, LANES, MLA_QK, "mla_q_norm", tabs=tabs)
    mkn = _mla_k_fwd(kv_lin, z, Z_KR, gk_m, tabs, "mla_k_norm")
    (o_mla, lse_mla), _ = _attn_fwd(mqn, mkn, kv_lin, 0, True, MLA_QK ** -0.5, S, NX, "mla_attn")
    om = _mm(o_mla, full["w_branch_mla"], "nn", "branch_mla")

    mix = _gate_fwd(z, W["b_gate"], of, om, "gate_mix")
    h2 = _mm(mix, full["w_out"], "nn", "w_out", res=h1)

    h3, ffn2_saved = ffn_fwd(h2, W["ffn2_norm"], full["ffn2_w_gu"], "ffn2")

    dh3, dh3_b, loss_acc = _loss(h3, tgt, NX, "loss")
    loss = lax.psum(loss_acc[0, 0], AXES)

    G = {}
    parts = {}

    def scatter_of(group):
        per_dest = []
        for n in group:
            r, c = W[n].shape[1:]
            per_dest.append((G[n].reshape(r, N_DEV, c).transpose(1, 0, 2) if axis_of[n] == 1
                             else G[n].reshape(N_DEV, r, c)).astype(BF16))
        return _Exchange(per_dest, False)

    def ffn_bwd(dh, dh_b, h, norm, w_gu, w_down, saved, tag, behind_down=None, spread=False):
        u, sa, sb, a = saved
        G[tag + "_w_down"] = _mm(a, dh_b, "tn", tag + "_dw_down", scale=0.5)
        (dg, dup), got = _ffn_down_bwd(dh_b, w_down, sa, sb, tag + "_down_bwd",
                                       exch=scatter_of(behind_down) if behind_down else None)
        parts.update(zip(behind_down or [], got))
        dw_g = _mm(u, dg, "tn", tag + "_dw_g", exch=scatter_of([tag + "_w_down"]) if spread else None)
        if spread:
            dw_g, got = dw_g
            parts[tag + "_w_down"] = got[0]
        G[tag + "_w_gu"] = jnp.concatenate([dw_g, _mm(u, dup, "tn", tag + "_dw_u")], axis=1)
        (du,), got = _ffn_up_bwd_dx(dg, dup, w_gu, tag + "_up_bwd",
                                    exch=scatter_of([tag + "_w_gu"]) if spread else None)
        if spread:
            parts[tag + "_w_gu"] = got[0]
        *dh_in, G[tag + "_norm"] = _norm_bwd(h, 0, D, D, norm, D, D, du, tag + "_norm_bwd", res=dh,
                                             bf16_copy=not spread)
        return dh_in

    dh2, dh2_b = ffn_bwd(dh3, dh3_b, h2, W["ffn2_norm"], full["ffn2_w_gu"], full["ffn2_w_down"], ffn2_saved,
                         "ffn2")

    G["w_out"] = _mm(mix, dh2_b, "tn", "dw_out")
    dmix = _mm(dh2_b, full["w_out"], "nt", "w_out_bwd")
    dgl, dof, dom, G["b_gate"] = _gate_bwd(dmix, z, W["b_gate"], of, om, "gate_bwd")

    G["w_branch_fox"] = _mm(o_fox, dof, "tn", "dw_branch_fox")
    do_fox = _mm(dof, full["w_branch_fox"], "nt", "branch_fox_bwd")
    (dq_f, dk_f, dv_f, dkm_f, dvm_f, dck, dcmk, dcq), got = _attn_bwd(
        fqn, fkn, z, Z_FV, o_fox, lse_fox, do_fox, False, FOX_HD ** -0.5, S, NX, "fox_attn_bwd", ck, cmk,
        exch=scatter_of(last_group))
    parts.update(zip(last_group, got))
    dk_f, dv_f = put_meta(dk_f, dkm_f), put_meta(dv_f, dvm_f)
    dfq, gq = _norm_bwd(z, Z_FQ, FOX_W, LANES, gq_f, FOX_HD, FOX_HD, dq_f, "fox_q_norm_bwd", out_dtype=BF16)
    dfk, gk = _norm_bwd(z, Z_FK, FOX_W, LANES, gk_f, FOX_HD, FOX_HD, dk_f, "fox_k_norm_bwd", out_dtype=BF16)
    G["fox_q_norm"] = gq[:, :FOX_HD] + gq[:, FOX_HD:]
    G["fox_k_norm"] = gk[:, :FOX_HD] + gk[:, FOX_HD:]
    dc = pad_lanes(dck.reshape(B, H, S).transpose(0, 2, 1)
                   + dcq.transpose(0, 2, 1, 3).reshape(B, S, H))
    dcm = pad_lanes(dcmk.sum(0)[:, 0, :].T)
    dcm = jnp.where(jnp.arange(LANES)[:, None] < N_META, dcm, 0.0)
    dfl, dflm, dbf = _cum_bwd(dc, dcm, fl, flm, bf, "forget_cum_bwd")
    G["b_forget"] = dbf[:, :HEADS]
    dfl_t = rows_T(dfl.reshape(NX, LANES), dflm)

    G["w_branch_mla"] = _mm(o_mla, dom, "tn", "dw_branch_mla")
    do_mla = _mm(dom, full["w_branch_mla"], "nt", "branch_mla_bwd")
    (dq_m, dk_m, dvk, dkm_m, dvkm), _ = _attn_bwd(
        mqn, mkn, kv_lin, 0, o_mla, lse_mla, do_mla, True, MLA_QK ** -0.5, S, NX, "mla_attn_bwd")
    dk_m, dvk = put_meta(dk_m, dkm_m), put_meta(dvk, dvkm)
    dq_lin, gq = _norm_bwd(q_lin, 0, H * LANES, LANES, gq_m, LANES, MLA_QK, dq_m, "mla_q_norm_bwd", tabs=tabs,
                           out_dtype=BF16)
    G["mla_q_norm"] = gq[:, :MLA_QK]
    G["mla_w_uq"] = _mm(cqn, dq_lin, "tn", "dw_uq").reshape(MLA_Q_RANK, H, LANES)[:, :, :MLA_QK].reshape(
        MLA_Q_RANK, H * MLA_QK)
    dcqn = _mm(dq_lin, w_uq_p, "nt", "mla_uq_bwd")
    dcq, G["mla_cq_norm"] = _norm_bwd(z, Z_CQ, MLA_Q_RANK, MLA_Q_RANK, W["mla_cq_norm"], MLA_Q_RANK,
                                      MLA_Q_RANK, dcqn, "mla_cq_norm_bwd", out_dtype=BF16)
    dkv_lin, dkr, gk = _mla_k_bwd(kv_lin, z, Z_KR, gk_m, tabs, dk_m, dvk, "mla_k_norm_bwd")
    G["mla_k_norm"] = gk[:, :MLA_QK]
    G["mla_w_ukv"] = _mm(ckvn, dkv_lin, "tn", "dw_ukv")
    dckvn = _mm(dkv_lin, full["mla_w_ukv"], "nt", "mla_ukv_bwd")
    dckv, G["mla_ckv_norm"] = _norm_bwd(z, Z_CKV, MLA_KV_RANK, MLA_KV_RANK, W["mla_ckv_norm"], MLA_KV_RANK,
                                        MLA_KV_RANK, dckvn, "mla_ckv_norm_bwd", out_dtype=BF16)

    dz = jnp.concatenate([dgl, dfq, dfk, dv_f.astype(BF16), dcq, dckv, dfl_t.astype(BF16),
                          dkr.astype(BF16)], axis=1)
    dw_in_p = _mm(u2, dz, "tn", "dw_in")
    G["w_in"] = jnp.concatenate([
        dw_in_p[:, Z_FQ:Z_CQ], dw_in_p[:, Z_F:Z_F + HEADS], dw_in_p[:, Z_CQ:Z_F],
        dw_in_p[:, Z_KR:Z_KR + MLA_ROPE], dw_in_p[:, Z_G:Z_G + 2 * D]], axis=1)
    du2, got = _mm(dz, w_in_p, "nt", "w_in_bwd", exch=scatter_of(mix_small))
    parts.update(zip(mix_small, got))
    dh1, dh1_b, G["mix_norm"] = _norm_bwd(h1, 0, D, D, W["mix_norm"], D, D, du2, "mix_norm_bwd", res=dh2,
                                          bf16_copy=True)

    (dh0,) = ffn_bwd(dh1, dh1_b, h0, W["ffn1_norm"], full["ffn1_w_gu"], full["ffn1_w_down"], ffn1_saved, "ffn1",
                     behind_down=["w_in"], spread=True)
    grad_x = dh0[:NX].reshape(B, S, D)
    G["meta_tokens"] = dh0[NX:NX + N_META]

    res = {}
    for n, _ in big:
        outs4 = _adamw(parts[n], W[n][0], Mo[n][0], Vo[n][0], "adamw_" + n)
        for key, arr in zip(("g", "d", "m", "v"), outs4):
            res[key, n] = arr[None]

    small = [n for n in names if n not in dict(big) and n != "meta_tokens"]
    small_shapes = [W[n].shape for n in small]
    spack = _pack([G["meta_tokens"]] + [G[n] for n in small], 1024, 8)
    (sparts,) = _exchange([spack], "gather_small_grads", gather=True)
    sflat = sparts.reshape(N_DEV, -1)
    dsh = D // N_DEV
    meta_part = lax.dynamic_slice(sflat[:, :N_META * D].reshape(N_DEV, N_META, D),
                                  (0, 0, me * dsh), (N_DEV, N_META, dsh)).reshape(N_DEV, -1)
    rep_len = sum(int(np.prod(s)) for s in small_shapes)
    rep_part = sflat[:, N_META * D:N_META * D + rep_len]
    sp = _pack_rows([meta_part, rep_part], LANES, _chunk_rows(LANES))
    pks = lambda src: _pack([src["meta_tokens"]] + [src[n] for n in small], LANES, _chunk_rows(LANES))
    g_s, d_s, m_s, v_s = _adamw(sp, pks(W), pks(Mo), pks(Vo), "adamw_small")
    shapes_s = [W["meta_tokens"].shape] + small_shapes
    for key, packed in (("g", g_s), ("d", d_s), ("m", m_s), ("v", v_s)):
        for n, arr in zip(["meta_tokens"] + small, _unpack(packed, shapes_s)):
            res[key, n] = arr

    outs = [loss, grad_x]
    for key in ("g", "d", "m", "v"):
        outs += [res[key, n] for n in names]
    return tuple(outs)
```

```python
import numpy as np
import jax
import jax.numpy as jnp
from jax import lax
from jax.experimental import pallas as pl
from jax.experimental.pallas import tpu as pltpu

F32 = jnp.float32
BF16 = jnp.bfloat16

N_META = 16
EPS = 1e-6
HEADS = 8
FOX_HD = 64
FOX_W = HEADS * FOX_HD
MLA_Q_RANK = 256
MLA_KV_RANK = 128
MLA_NOPE = 64
MLA_ROPE = 32
MLA_QK = MLA_NOPE + MLA_ROPE
MLA_V = 64
ROPE_THETA = 10000.0
LANES = 128
HALF = LANES // 2
META_BLK = 128
NEG = -1e30

ADAM_LR = 0.001
ADAM_B1 = 0.9
ADAM_B2 = 0.999
ADAM_EPS = 1e-08
ADAM_WD = 0.01
ADAM_STEP = 10

N_DEV = 8
AXES = ("x", "y", "c")
VMEM_LIMIT_BYTES = 56 * 1024 * 1024


def _tile(n, cap, mult):
    best = None
    for d in range(mult, min(n, cap) + 1, mult):
        if n % d == 0:
            best = d
    return n if best is None else best


VREG_ELEMS = 8 * LANES


def _row_tile(rows, width):
    return _tile(rows, max(16, (1 << 19) // width), 16)


def _chunk_rows(width):
    rows = 16
    while 2 * rows * width <= 8 * VREG_ELEMS:
        rows *= 2
    return rows


def _by_chunks(rows, width, step, init=()):
    return step(pl.ds(0, rows), init)


def _params(sem=None):
    return pltpu.CompilerParams(dimension_semantics=sem, vmem_limit_bytes=VMEM_LIMIT_BYTES)


def _mm(a, b, mode, name, out_dtype=F32, scale=1.0, res=None, exch=None):
    if mode == "nn":
        (M, K), (K2, N) = a.shape, b.shape
    elif mode == "nt":
        (M, K), (N, K2) = a.shape, b.shape
    else:
        (K, M), (K2, N) = a.shape, b.shape
    assert K == K2, (a.shape, b.shape, mode)
    if mode == "tn":
        tm, tk = _tile(M, 1408, 128), _tile(K, 2080, 16)
    else:
        tm, tk = _tile(M, 640, 16), _tile(K, 4224, 128)
    tn = _tile(N, 1408, 128)
    nk = K // tk
    ni, nj = M // tm, N // tn
    bytes_a, bytes_b = a.size * a.dtype.itemsize, b.size * b.dtype.itemsize
    j_outer = nk == 1 and bytes_a * nj + bytes_b < bytes_a + bytes_b * ni
    ij = (lambda g0, g1: (g1, g0)) if j_outer else (lambda g0, g1: (g0, g1))

    def spec(shape, at):
        return pl.BlockSpec(shape, lambda g0, g1, k: at(*ij(g0, g1), k))

    a_spec = {"nn": spec((tm, tk), lambda i, j, k: (i, k)),
              "nt": spec((tm, tk), lambda i, j, k: (i, k)),
              "tn": spec((tk, tm), lambda i, j, k: (k, i))}[mode]
    b_spec = {"nn": spec((tk, tn), lambda i, j, k: (k, j)),
              "nt": spec((tn, tk), lambda i, j, k: (j, k)),
              "tn": spec((tk, tn), lambda i, j, k: (k, j))}[mode]
    dims = {"nn": (((1,), (0,)), ((), ())), "nt": (((1,), (1,)), ((), ())),
            "tn": (((0,), (0,)), ((), ()))}[mode]
    o_spec = spec((tm, tn), lambda i, j, k: (i, j))
    has_res = res is not None

    def body(*refs):
        a_ref, b_ref = refs[:2]
        r_ref = refs[2] if has_res else None
        o_ref = refs[2 + has_res]

        def finish(acc):
            o = acc * scale
            if has_res:
                o = o + r_ref[...]
            o_ref[...] = o.astype(out_dtype)

        prod = lax.dot_general(a_ref[...].astype(BF16), b_ref[...].astype(BF16), dims,
                               preferred_element_type=F32)
        if nk == 1:
            finish(prod)
            return
        acc_ref = refs[3 + has_res]
        k = pl.program_id(2)

        @pl.when(k == 0)
        def _():
            acc_ref[...] = prod

        @pl.when((k > 0) & (k < nk - 1))
        def _():
            acc_ref[...] += prod

        @pl.when(k == nk - 1)
        def _():
            finish(acc_ref[...] + prod)

    ins = [a, b] + ([res] if has_res else [])
    specs = [a_spec, b_spec] + ([o_spec] if has_res else [])
    (out,), got = _call(
        body, name, (nj, ni, nk) if j_outer else (ni, nj, nk), specs, [o_spec],
        [jax.ShapeDtypeStruct((M, N), out_dtype)],
        ins, scratch_shapes=[pltpu.VMEM((tm, tn), F32)] if nk > 1 else [],
        sem=("parallel", "parallel", "arbitrary"), exch=exch)
    return out if exch is None else (out, got)


def _rope_fwd(y, c, s1, s2):
    return y * c + pltpu.roll(y, LANES - 16, 1) * s1 + pltpu.roll(y, 16, 1) * s2


def _rope_bwd(dy, c, s1, s2):
    return dy * c + pltpu.roll(dy * s1, 16, 1) + pltpu.roll(dy * s2, LANES - 16, 1)


def _group_sum(v, seg):
    if seg == v.shape[-1]:
        return jnp.sum(v, axis=-1, keepdims=True)
    lo = lax.broadcasted_iota(jnp.int32, v.shape, 1) < seg
    s_lo = jnp.sum(jnp.where(lo, v, 0.0), axis=-1, keepdims=True)
    s_hi = jnp.sum(jnp.where(lo, 0.0, v), axis=-1, keepdims=True)
    return jnp.where(lo, s_lo, s_hi)


def _norm_fwd(src, col0, width, bw, gain, seg, d_true, name, tabs=None, out_dtype=BF16):
    T = src.shape[0]
    tr = _row_tile(T, bw)
    inv_d = 1.0 / d_true
    c0 = col0 // bw
    assert col0 % bw == 0 and width % bw == 0

    def body(*refs):
        if tabs is None:
            x_ref, g_ref, o_ref = refs
        else:
            x_ref, g_ref, c_ref, s1_ref, s2_ref, o_ref = refs
        gain_v = g_ref[...]

        def step(rows, carry):
            xv = x_ref[rows, :]
            r = lax.rsqrt(_group_sum(xv * xv, seg) * inv_d + EPS)
            y = xv * r * gain_v
            if tabs is not None:
                y = _rope_fwd(y, c_ref[rows, :], s1_ref[rows, :], s2_ref[rows, :])
            o_ref[rows, :] = y.astype(out_dtype)
            return carry

        _by_chunks(tr, bw, step)

    specs = [pl.BlockSpec((tr, bw), lambda i, j: (i, c0 + j)), pl.BlockSpec((1, bw), lambda i, j: (0, 0))]
    ins = [src, gain]
    if tabs is not None:
        tab = pl.BlockSpec((tr, LANES), lambda i, j: (i, 0))
        specs += [tab, tab, tab]
        ins += list(tabs)
    return pl.pallas_call(
        body, name=name, grid=(T // tr, width // bw), in_specs=specs,
        out_specs=pl.BlockSpec((tr, bw), lambda i, j: (i, j)),
        out_shape=jax.ShapeDtypeStruct((T, width), out_dtype),
        compiler_params=_params(("parallel", "parallel")),
    )(*ins)


def _norm_bwd_math(xv, gain, dyv, seg, inv_d):
    r = lax.rsqrt(_group_sum(xv * xv, seg) * inv_d + EPS)
    gy = dyv * gain
    dot = _group_sum(gy * xv, seg)
    dx = r * gy - xv * (r * r * r * inv_d) * dot
    return dx, jnp.sum(dyv * xv * r, axis=0, keepdims=True)


def _norm_bwd(src, col0, width, bw, gain, seg, d_true, dy, name, tabs=None, res=None, out_dtype=F32,
              bf16_copy=False):
    T = src.shape[0]
    tr = _row_tile(T, bw)
    inv_d = 1.0 / d_true
    c0 = col0 // bw
    has_res = res is not None

    def body(*refs):
        refs = list(refs)
        x_ref, g_ref, dy_ref = refs[:3]
        pos = 3
        if tabs is not None:
            c_ref, s1_ref, s2_ref = refs[3:6]
            pos = 6
        if has_res:
            r_ref = refs[pos]
            pos += 1
        dx_ref = refs[pos]
        dxb_ref = refs[pos + 1] if bf16_copy else None
        dg_ref = refs[pos + 1 + bf16_copy]
        gain_v = g_ref[...]

        def step(rows, dg_sum):
            dyv = dy_ref[rows, :].astype(F32)
            if tabs is not None:
                dyv = _rope_bwd(dyv, c_ref[rows, :], s1_ref[rows, :], s2_ref[rows, :])
            dx, dg = _norm_bwd_math(x_ref[rows, :], gain_v, dyv, seg, inv_d)
            if has_res:
                dx = dx + r_ref[rows, :]
            dx_ref[rows, :] = dx.astype(out_dtype)
            if bf16_copy:
                dxb_ref[rows, :] = dx.astype(BF16)
            return dg_sum + dg

        dg = _by_chunks(tr, bw, step, jnp.zeros((1, bw), F32))

        @pl.when((pl.program_id(0) == 0) & (pl.program_id(1) == 0))
        def _():
            dg_ref[...] = jnp.zeros_like(dg_ref)

        dg_ref[...] += dg

    blk = pl.BlockSpec((tr, bw), lambda i, j: (i, j))
    one = pl.BlockSpec((1, bw), lambda i, j: (0, 0))
    specs = [pl.BlockSpec((tr, bw), lambda i, j: (i, c0 + j)), one, blk]
    ins = [src, gain, dy]
    if tabs is not None:
        tab = pl.BlockSpec((tr, LANES), lambda i, j: (i, 0))
        specs += [tab, tab, tab]
        ins += list(tabs)
    if has_res:
        specs.append(blk)
        ins.append(res)
    extra = bf16_copy * [blk]
    extra_shape = bf16_copy * [jax.ShapeDtypeStruct((T, width), BF16)]
    return pl.pallas_call(
        body, name=name, grid=(T // tr, width // bw), in_specs=specs, out_specs=(blk, *extra, one),
        out_shape=(jax.ShapeDtypeStruct((T, width), out_dtype), *extra_shape,
                   jax.ShapeDtypeStruct((1, bw), F32)),
        compiler_params=_params(("arbitrary", "arbitrary")),
    )(*ins)


def _mla_k_raw(kv, kr):
    lane = lax.broadcasted_iota(jnp.int32, kv.shape, 1)
    return jnp.where(lane < MLA_NOPE, kv, jnp.where(lane < MLA_QK, pltpu.roll(kr, MLA_NOPE, 1), 0.0))


def _mla_k_fwd(kv_lin, z, kr_col, gain, tabs, name):
    T, W = kv_lin.shape
    tr = _row_tile(T, LANES)
    krb = kr_col // LANES
    inv_d = 1.0 / MLA_QK

    def body(kv_ref, kr_ref, g_ref, c_ref, s1_ref, s2_ref, o_ref):
        gain_v = g_ref[...]

        def step(rows, carry):
            xv = _mla_k_raw(kv_ref[rows, :], kr_ref[rows, :])
            r = lax.rsqrt(jnp.sum(xv * xv, axis=-1, keepdims=True) * inv_d + EPS)
            o_ref[rows, :] = _rope_fwd(xv * r * gain_v, c_ref[rows, :], s1_ref[rows, :],
                                       s2_ref[rows, :]).astype(BF16)
            return carry

        _by_chunks(tr, LANES, step)

    blk = pl.BlockSpec((tr, LANES), lambda i, h: (i, h))
    tab = pl.BlockSpec((tr, LANES), lambda i, h: (i, 0))
    return pl.pallas_call(
        body, name=name, grid=(T // tr, W // LANES),
        in_specs=[blk, pl.BlockSpec((tr, LANES), lambda i, h: (i, krb)),
                  pl.BlockSpec((1, LANES), lambda i, h: (0, 0)), tab, tab, tab],
        out_specs=blk, out_shape=jax.ShapeDtypeStruct((T, W), BF16),
        compiler_params=_params(("parallel", "parallel")),
    )(kv_lin, z, gain, *tabs)


def _mla_k_bwd(kv_lin, z, kr_col, gain, tabs, dk, dvk, name):
    T, W = kv_lin.shape
    tr = _row_tile(T, LANES)
    krb = kr_col // LANES
    inv_d = 1.0 / MLA_QK

    def body(kv_ref, kr_ref, g_ref, c_ref, s1_ref, s2_ref, dk_ref, dvk_ref, dkv_ref, dkr_ref, dg_ref):
        h = pl.program_id(1)
        gain_v = g_ref[...]

        @pl.when(h == 0)
        def _():
            dkr_ref[...] = jnp.zeros_like(dkr_ref)

        def step(rows, dg_sum):
            xv = _mla_k_raw(kv_ref[rows, :], kr_ref[rows, :])
            dyv = _rope_bwd(dk_ref[rows, :], c_ref[rows, :], s1_ref[rows, :], s2_ref[rows, :])
            dx, dg = _norm_bwd_math(xv, gain_v, dyv, LANES, inv_d)
            lane = lax.broadcasted_iota(jnp.int32, dx.shape, 1)
            dkv_ref[rows, :] = jnp.where(lane < MLA_NOPE, dx, dvk_ref[rows, :]).astype(BF16)
            dkr_ref[rows, :] += pltpu.roll(jnp.where((lane >= MLA_NOPE) & (lane < MLA_QK), dx, 0.0),
                                           LANES - MLA_NOPE, 1)
            return dg_sum + dg

        dg = _by_chunks(tr, LANES, step, jnp.zeros((1, LANES), F32))

        @pl.when((pl.program_id(0) == 0) & (h == 0))
        def _():
            dg_ref[...] = jnp.zeros_like(dg_ref)

        dg_ref[...] += dg

    blk = pl.BlockSpec((tr, LANES), lambda i, h: (i, h))
    tab = pl.BlockSpec((tr, LANES), lambda i, h: (i, 0))
    one = pl.BlockSpec((1, LANES), lambda i, h: (0, 0))
    return pl.pallas_call(
        body, name=name, grid=(T // tr, W // LANES),
        in_specs=[blk, pl.BlockSpec((tr, LANES), lambda i, h: (i, krb)), one, tab, tab, tab, blk, blk],
        out_specs=(blk, tab, one),
        out_shape=(jax.ShapeDtypeStruct((T, W), BF16), jax.ShapeDtypeStruct((T, LANES), F32),
                   jax.ShapeDtypeStruct((1, LANES), F32)),
        compiler_params=_params(("arbitrary", "arbitrary")),
    )(kv_lin, z, gain, *tabs, dk, dvk)


def _ffn_up(u, w_gu, name, exch=None):
    T, D = u.shape
    F = w_gu.shape[1] // 2
    tm, tn = _tile(T, 640, 16), _tile(F, 1408, 128)
    nj = F // tn

    def body(u_ref, wg_ref, wu_ref, sa_ref, sb_ref, a_ref):
        uv = u_ref[...]
        g = jnp.dot(uv, wg_ref[...], preferred_element_type=F32)
        up = jnp.dot(uv, wu_ref[...], preferred_element_type=F32)
        sg = jax.nn.sigmoid(g)
        silu = g * sg
        sa_ref[...] = silu.astype(BF16)
        sb_ref[...] = (up * (sg + silu * (1.0 - sg))).astype(BF16)
        a_ref[...] = (silu * up).astype(BF16)

    o_spec = pl.BlockSpec((tm, tn), lambda j, i: (i, j))
    sh = jax.ShapeDtypeStruct((T, F), BF16)
    return _call(
        body, name, (nj, T // tm),
        [pl.BlockSpec((tm, D), lambda j, i: (i, 0)),
         pl.BlockSpec((D, tn), lambda j, i: (0, j)),
         pl.BlockSpec((D, tn), lambda j, i: (0, j + nj))],
        [o_spec, o_spec, o_spec], [sh, sh, sh], [u, w_gu, w_gu],
        sem=("parallel", "parallel"), exch=exch)


def _ffn_down_bwd(dh, w_down, sa, sb, name, exch=None):
    T, D = dh.shape
    F = w_down.shape[0]
    tm, tn = _tile(T, 640, 16), _tile(F, 1408, 128)

    def body(dh_ref, w_ref, sa_ref, sb_ref, dg_ref, dup_ref):
        da = 0.5 * lax.dot_general(dh_ref[...].astype(BF16), w_ref[...], (((1,), (1,)), ((), ())),
                                   preferred_element_type=F32)
        dup_ref[...] = (da * sa_ref[...].astype(F32)).astype(BF16)
        dg_ref[...] = (da * sb_ref[...].astype(F32)).astype(BF16)

    t_spec = pl.BlockSpec((tm, tn), lambda j, i: (i, j))
    sh = jax.ShapeDtypeStruct((T, F), BF16)
    return _call(
        body, name, (F // tn, T // tm),
        [pl.BlockSpec((tm, D), lambda j, i: (i, 0)),
         pl.BlockSpec((tn, D), lambda j, i: (j, 0)), t_spec, t_spec],
        [t_spec, t_spec], [sh, sh], [dh, w_down, sa, sb],
        sem=("parallel", "parallel"), exch=exch)


def _ffn_up_bwd_dx(dg, dup, w_gu, name, exch=None):
    T, F = dg.shape
    D = w_gu.shape[0]
    tm, tk = _tile(T, 640, 16), _tile(F, 2816, 128)
    nk = F // tk
    nt = (((1,), (1,)), ((), ()))

    def body(dg_ref, dup_ref, wg_ref, wu_ref, o_ref, acc_ref):
        k = pl.program_id(1)
        prod = (lax.dot_general(dg_ref[...], wg_ref[...], nt, preferred_element_type=F32)
                + lax.dot_general(dup_ref[...], wu_ref[...], nt, preferred_element_type=F32))
        if nk == 1:
            o_ref[...] = prod
            return

        @pl.when(k == 0)
        def _():
            acc_ref[...] = prod

        @pl.when((k > 0) & (k < nk - 1))
        def _():
            acc_ref[...] += prod

        @pl.when(k == nk - 1)
        def _():
            o_ref[...] = acc_ref[...] + prod

    return _call(
        body, name, (T // tm, nk),
        [pl.BlockSpec((tm, tk), lambda i, k: (i, k)),
         pl.BlockSpec((tm, tk), lambda i, k: (i, k)),
         pl.BlockSpec((D, tk), lambda i, k: (0, k)),
         pl.BlockSpec((D, tk), lambda i, k: (0, k + nk))],
        [pl.BlockSpec((tm, D), lambda i, k: (i, 0))], [jax.ShapeDtypeStruct((T, D), F32)],
        [dg, dup, w_gu, w_gu], scratch_shapes=[pltpu.VMEM((tm, D), F32)],
        sem=("parallel", "arbitrary"), exch=exch)


def _logsig(x):
    return jnp.minimum(x, 0.0) - jnp.log(1.0 + jnp.exp(-jnp.abs(x)))


def _cum_fwd(fl, flm, bf, name):
    B, S, _ = fl.shape
    nb = S // LANES

    def body(fl_ref, flm_ref, bf_ref, cum_ref, cumm_ref):
        rows = lax.broadcasted_iota(jnp.int32, (LANES, LANES), 0)
        cols = lax.broadcasted_iota(jnp.int32, (LANES, LANES), 1)
        tri = (rows >= cols).astype(F32)
        bias = bf_ref[...]
        lfm = jnp.where(rows < N_META, _logsig(flm_ref[...] + bias), 0.0)
        cm = jnp.dot(tri, lfm, precision=lax.Precision.HIGHEST, preferred_element_type=F32)
        cumm_ref[...] = cm * LOG2E
        base = cm[LANES - 1:LANES, :]
        for b in range(B):
            def blk(i, carry):
                r0 = pl.multiple_of(i * LANES, LANES)
                lf = _logsig(fl_ref[b, pl.ds(r0, LANES), :] + bias)
                c = jnp.dot(tri, lf, precision=lax.Precision.HIGHEST,
                            preferred_element_type=F32) + carry
                cum_ref[b, pl.ds(r0, LANES), :] = c * LOG2E
                return c[LANES - 1:LANES, :]

            lax.fori_loop(0, nb, blk, base)

    return pl.pallas_call(
        body, name=name,
        out_shape=(jax.ShapeDtypeStruct((B, S, LANES), F32),
                   jax.ShapeDtypeStruct((LANES, LANES), F32)),
        compiler_params=_params(),
    )(fl, flm, bf)


def _cum_bwd(dc, dcm, fl, flm, bf, name):
    B, S, _ = fl.shape
    nb = S // LANES

    def body(dc_ref, dcm_ref, fl_ref, flm_ref, bf_ref, dfl_ref, dflm_ref, dbf_ref):
        rows = lax.broadcasted_iota(jnp.int32, (LANES, LANES), 0)
        cols = lax.broadcasted_iota(jnp.int32, (LANES, LANES), 1)
        triu = (rows <= cols).astype(F32)
        bias = bf_ref[...]
        total = jnp.zeros((1, LANES), F32)
        dbf = jnp.zeros((1, LANES), F32)
        for b in range(B):
            tail = jnp.zeros((1, LANES), F32)
            for t in range(nb):
                r0 = (nb - 1 - t) * LANES
                rc = jnp.dot(triu, dc_ref[b, r0:r0 + LANES, :], precision=lax.Precision.HIGHEST,
                             preferred_element_type=F32) + tail
                xv = fl_ref[b, r0:r0 + LANES, :] + bias
                d = rc / (1.0 + jnp.exp(xv))
                dfl_ref[b, r0:r0 + LANES, :] = d
                tail = rc[0:1, :]
                dbf = dbf + jnp.sum(d, axis=0, keepdims=True)
            total = total + tail
        rcm = jnp.dot(triu, dcm_ref[...], precision=lax.Precision.HIGHEST,
                      preferred_element_type=F32) + total
        dm = jnp.where(rows < N_META, rcm / (1.0 + jnp.exp(flm_ref[...] + bias)), 0.0)
        dflm_ref[...] = dm
        dbf_ref[...] = dbf + jnp.sum(dm, axis=0, keepdims=True)

    return pl.pallas_call(
        body, name=name,
        out_shape=(jax.ShapeDtypeStruct((B, S, LANES), F32),
                   jax.ShapeDtypeStruct((LANES, LANES), F32),
                   jax.ShapeDtypeStruct((1, LANES), F32)),
        compiler_params=_params(),
    )(dc, dcm, fl, flm, bf)


_NT = (((1,), (1,)), ((), ()))


def _token_rows_buffer(T, NX, width, dtype=F32):
    return lax.dynamic_update_slice(lax.empty((T, width), dtype), jnp.zeros((T - NX, width), dtype), (NX, 0))


def _attn_specs(S, NX, qw, v_col0):
    mb = NX // META_BLK
    vb = v_col0 // qw
    return (pl.BlockSpec((S, qw), lambda b, p: (b, p)),
            pl.BlockSpec((META_BLK, qw), lambda b, p: (mb, p)),
            pl.BlockSpec((S, qw), lambda b, p: (b, vb + p)),
            pl.BlockSpec((META_BLK, qw), lambda b, p: (mb, vb + p)),
            pl.BlockSpec((S, LANES), lambda b, p: (b, p)))


def _cum_specs(S, TK):
    return [pl.BlockSpec((1, 2, S // TK, 1, TK), lambda b, p: (b, p, 0, 0, 0)),
            pl.BlockSpec((1, 2, 1, META_BLK), lambda b, p: (b, p, 0, 0))]


LOG2E = 1.4426950408889634


def _attn_fwd(qn, kn, vsrc, v_col0, mla, scale, S, NX, name, ck=None, cmk=None, exch=None):
    T = qn.shape[0]
    B = NX // S
    qw = 2 * LANES if mla else LANES
    npair = qn.shape[1] // qw
    TQ = min(512, S)
    TK = TQ
    forget = ck is not None
    a = scale * LOG2E

    def body(*refs):
        if forget:
            q_ref, k_ref, km_ref, v_ref, vm_ref, ck_ref, cmk_ref, _, o_ref, lse_ref = refs
        else:
            q_ref, k_ref, km_ref, v_ref, vm_ref, _, o_ref, lse_ref = refs
        lo = lax.broadcasted_iota(jnp.int32, (1, LANES), 1) < HALF
        mcol = lax.broadcasted_iota(jnp.int32, (TQ, META_BLK), 1)
        causal = (lax.broadcasted_iota(jnp.int32, (TQ, TK), 0)
                  >= lax.broadcasted_iota(jnp.int32, (TQ, TK), 1))
        two = lax.broadcasted_iota(jnp.int32, (TQ, 2), 1)
        for qi in range(S // TQ):
            q0 = qi * TQ
            sls = [slice(e * LANES, (e + 1) * LANES) if mla else slice(None) for e in range(2)]
            if mla:
                qts = [q_ref[q0:q0 + TQ, sl] for sl in sls]
            else:
                qts = [jnp.where(lo if e == 0 else ~lo, q_ref[q0:q0 + TQ, :], 0.0).astype(BF16)
                       for e in range(2)]

            def step(e, kt, vt, c2, mask, carry):
                m, l, acc = carry
                s = lax.dot_general(qts[e], kt, _NT, preferred_element_type=F32) * a
                if forget:
                    s = s - c2
                if mask is not None:
                    s = jnp.where(mask, s, NEG)
                m2 = jnp.max(s, axis=1, keepdims=True)
                if m is not None:
                    m2 = jnp.maximum(m, m2)
                p = jnp.exp2(s - m2)
                l2 = jnp.sum(p, axis=1, keepdims=True)
                acc2 = jnp.dot(p.astype(BF16), vt.astype(BF16), preferred_element_type=F32)
                if m is not None:
                    alpha = jnp.exp2(m - m2)
                    l2, acc2 = alpha * l + l2, alpha * acc + acc2
                return m2, l2, acc2

            def both(rows, kj, mask, carry):
                return tuple(step(e, k_ref[rows, sls[e]], v_ref[rows, sls[e]],
                                  ck_ref[0, e, kj] if forget else None, mask, carry[e]) for e in range(2))

            def below(kj, carry):
                return both(pl.ds(pl.multiple_of(kj * TK, TK), TK), kj, None, carry)

            carry = tuple(step(e, km_ref[:, sls[e]], vm_ref[:, sls[e]], cmk_ref[0, e] if forget else None,
                               mcol < N_META, (None, None, None)) for e in range(2))
            if qi:
                carry = lax.fori_loop(0, qi, below, carry)
            carry = both(slice(q0, q0 + TK), qi, causal, carry)
            outs = [acc / l for _, l, acc in carry]
            lses = [m + jnp.log2(l) for m, l, _ in carry]
            first = pltpu.roll(outs[0], HALF, 1) if mla else outs[0]
            o_ref[q0:q0 + TQ, :] = jnp.where(lo, first, outs[1])
            lse_ref[0, 0, q0:q0 + TQ, :] = jnp.where(two == 0, lses[0], lses[1])

    qk, kmeta, vv, vmeta, pair = _attn_specs(S, NX, qw, v_col0)
    specs = [qk, qk, kmeta, vv, vmeta]
    ins = [qn, kn, kn, vsrc, vsrc]
    if forget:
        specs += _cum_specs(S, TK)
        ins += [ck, cmk]
    specs.append(pl.BlockSpec(memory_space=pl.ANY))
    ins.append(_token_rows_buffer(T, NX, npair * LANES))
    lse_spec = pl.BlockSpec((1, 1, S, 2), lambda b, p: (b, p, 0, 0))
    return _call(
        body, name, (B, npair), specs, [pair, lse_spec],
        [jax.ShapeDtypeStruct((T, npair * LANES), F32), jax.ShapeDtypeStruct((B, npair, S, 2), F32)],
        ins, sem=("parallel", "parallel"), aliases={len(ins) - 1: 0}, exch=exch)


def _attn_bwd(qn, kn, vsrc, v_col0, o, lse, do, mla, scale, S, NX, name, ck=None, cmk=None, exch=None):
    T, W = qn.shape
    B = NX // S
    qw = 2 * LANES if mla else LANES
    npair = W // qw
    TQ = min(512, S)
    TK = TQ
    forget = ck is not None
    a = scale * LOG2E
    _TN = (((0,), (0,)), ((), ()))

    def body(*refs):
        refs = list(refs)
        q_ref, k_ref, km_ref, v_ref, vm_ref, o_ref, do_ref, lse_ref = refs[:8]
        pos = 8
        if forget:
            ck_ref, cmk_ref = refs[8:10]
            pos = 10
        pos += 3
        dq_ref, dk_ref, dv_ref, dkm_ref, dvm_ref = refs[pos:pos + 5]
        if forget:
            dck_ref, dcm_ref, dcq_ref = refs[pos + 5:pos + 8]
            dck_ref[...] = jnp.zeros_like(dck_ref)
            dcm_ref[...] = jnp.zeros_like(dcm_ref)
        dk_ref[...] = jnp.zeros_like(dk_ref)
        dv_ref[...] = jnp.zeros_like(dv_ref)
        dkm_ref[...] = jnp.zeros_like(dkm_ref)
        dvm_ref[...] = jnp.zeros_like(dvm_ref)
        lo = lax.broadcasted_iota(jnp.int32, (1, LANES), 1) < HALF
        mcol = lax.broadcasted_iota(jnp.int32, (TQ, META_BLK), 1)
        causal = (lax.broadcasted_iota(jnp.int32, (TQ, TK), 0)
                  >= lax.broadcasted_iota(jnp.int32, (TQ, TK), 1))
        two = lax.broadcasted_iota(jnp.int32, (TQ, 2), 1)
        for qi in range(S // TQ):
            q0 = qi * TQ
            dof = do_ref[q0:q0 + TQ, :]
            prod = dof * o_ref[q0:q0 + TQ, :]
            lse2 = lse_ref[0, 0, q0:q0 + TQ, :]
            sls = [slice(e * LANES, (e + 1) * LANES) if mla else slice(None) for e in range(2)]
            mine = [lo, ~lo]
            if mla:
                qts = [q_ref[q0:q0 + TQ, sl] for sl in sls]
                dots = [jnp.where(lo, 0.0, pltpu.roll(dof, HALF, 1) if e == 0 else dof).astype(BF16)
                        for e in range(2)]
            else:
                qts = [jnp.where(mine[e], q_ref[q0:q0 + TQ, :], 0.0).astype(BF16) for e in range(2)]
                dots = [jnp.where(mine[e], dof, 0.0).astype(BF16) for e in range(2)]
            deltas = [jnp.sum(jnp.where(mine[e], prod, 0.0), axis=1, keepdims=True) for e in range(2)]
            lse_ts = [jnp.sum(jnp.where(two == e, lse2, 0.0), axis=1, keepdims=True) for e in range(2)]

            def grads(e, kt, vt, c2, mask):
                s = lax.dot_general(qts[e], kt, _NT, preferred_element_type=F32) * a
                if forget:
                    s = s - c2
                p = jnp.exp2(s - lse_ts[e])
                if mask is not None:
                    p = jnp.where(mask, p, 0.0)
                dp = lax.dot_general(dots[e], vt, _NT, preferred_element_type=F32)
                ds = p * (dp - deltas[e])
                dsb = ds.astype(BF16)
                return (jnp.dot(dsb, kt, preferred_element_type=F32),
                        lax.dot_general(dsb, qts[e], _TN, preferred_element_type=F32) * scale,
                        lax.dot_general(p.astype(BF16), dots[e], _TN, preferred_element_type=F32),
                        -jnp.sum(ds, axis=0, keepdims=True) if forget else None,
                        jnp.sum(ds, axis=1, keepdims=True) if forget else None)

            def block(k_at, v_at, dk_at, dv_at, c_at, dc_at, mask, dqs):
                got = [grads(e, k_at(sls[e]), v_at(sls[e]).astype(BF16), c_at(e) if forget else None, mask)
                       for e in range(2)]
                if mla:
                    for e in range(2):
                        dk_at(sls[e], got[e][1])
                        dv_at(sls[e], got[e][2])
                else:
                    dk_at(sls[0], got[0][1] + got[1][1])
                    dv_at(sls[0], got[0][2] + got[1][2])
                if forget:
                    for e in range(2):
                        dc_at(e, got[e][3])
                picks = (0, 0, 4, 4) if forget else (0, 0)
                new = tuple(got[i % 2][k] for i, k in enumerate(picks))
                return new if dqs is None else tuple(x + y for x, y in zip(dqs, new))

            def add_to(ref, *lead):
                def add(*idx_and_val):
                    *idx, val = idx_and_val
                    ref[(*lead, *idx)] += val
                return add

            def token_block(rows, kj, mask, dqs):
                return block(lambda sl: k_ref[rows, sl], lambda sl: v_ref[rows, sl],
                             lambda sl, val: add_to(dk_ref)(rows, sl, val),
                             lambda sl, val: add_to(dv_ref)(rows, sl, val),
                             lambda e: ck_ref[0, e, kj], lambda e, val: add_to(dck_ref, 0)(e, kj, val),
                             mask, dqs)

            dqs = block(lambda sl: km_ref[:, sl], lambda sl: vm_ref[:, sl],
                        lambda sl, val: add_to(dkm_ref, 0)(slice(None), sl, val),
                        lambda sl, val: add_to(dvm_ref, 0)(slice(None), sl, val),
                        lambda e: cmk_ref[0, e], lambda e, val: add_to(dcm_ref, 0)(e, val),
                        mcol < N_META, None)

            def below(kj, dqs):
                return token_block(pl.ds(pl.multiple_of(kj * TK, TK), TK), kj, None, dqs)

            if qi:
                dqs = lax.fori_loop(0, qi, below, dqs)
            dqs = token_block(slice(q0, q0 + TK), qi, causal, dqs)
            if forget:
                dcq_ref[0, 0, q0:q0 + TQ, :] = jnp.where(two == 0, dqs[2], dqs[3])
            if mla:
                for e in range(2):
                    dq_ref[q0:q0 + TQ, sls[e]] = dqs[e] * scale
            else:
                dq_ref[q0:q0 + TQ, :] = jnp.where(lo, dqs[0], dqs[1]) * scale

    qk, kmeta, vv, vmeta, pair = _attn_specs(S, NX, qw, v_col0)
    lse_spec = pl.BlockSpec((1, 1, S, 2), lambda b, p: (b, p, 0, 0))
    specs = [qk, qk, kmeta, vv, vmeta, pair, pair, lse_spec]
    ins = [qn, kn, kn, vsrc, vsrc, o, do, lse]
    if forget:
        specs += _cum_specs(S, TK)
        ins += [ck, cmk]
    first_alias = len(ins)
    specs += [pl.BlockSpec(memory_space=pl.ANY)] * 3
    ins += [_token_rows_buffer(T, NX, W) for _ in range(3)]
    mspec = pl.BlockSpec((1, META_BLK, qw), lambda b, p: (b, 0, p))
    out_specs = [qk, qk, qk, mspec, mspec]
    tok = jax.ShapeDtypeStruct((T, W), F32)
    met = jax.ShapeDtypeStruct((B, META_BLK, W), F32)
    out_shape = [tok, tok, tok, met, met]
    if forget:
        out_specs += _cum_specs(S, TK) + [lse_spec]
        out_shape += [jax.ShapeDtypeStruct((B, HEADS, S // TK, 1, TK), F32),
                      jax.ShapeDtypeStruct((B, HEADS, 1, META_BLK), F32),
                      jax.ShapeDtypeStruct((B, npair, S, 2), F32)]
    return _call(
        body, name, (B, npair), specs, out_specs, out_shape, ins, sem=("parallel", "parallel"),
        aliases={first_alias: 0, first_alias + 1: 1, first_alias + 2: 2}, exch=exch)


def _gate_fwd(z, bg, of, om, name):
    T, D = of.shape
    tm = _tile(T, 640, 16)

    def body(z_ref, bg_ref, of_ref, om_ref, o_ref):
        bias = bg_ref[...]

        def step(rows, carry):
            gt = jax.nn.sigmoid(z_ref[rows, :] + bias)
            o_ref[rows, :] = (gt[:, :D] * of_ref[rows, :] + gt[:, D:] * om_ref[rows, :]).astype(BF16)
            return carry

        _by_chunks(tm, D, step)

    row = pl.BlockSpec((tm, D), lambda i: (i, 0))
    return pl.pallas_call(
        body, name=name, grid=(T // tm,),
        in_specs=[pl.BlockSpec((tm, 2 * D), lambda i: (i, 0)),
                  pl.BlockSpec((1, 2 * D), lambda i: (0, 0)), row, row],
        out_specs=row, out_shape=jax.ShapeDtypeStruct((T, D), BF16),
        compiler_params=_params(("parallel",)),
    )(z, bg, of, om)


def _gate_bwd(dmix, z, bg, of, om, name):
    T, D = of.shape
    tm = _tile(T, 640, 16)

    def body(dm_ref, z_ref, bg_ref, of_ref, om_ref, dgl_ref, dof_ref, dom_ref, dbg_ref):
        bias = bg_ref[...]

        def step(rows, dbg_sum):
            gt = jax.nn.sigmoid(z_ref[rows, :] + bias)
            dm = dm_ref[rows, :]
            dof_ref[rows, :] = (dm * gt[:, :D]).astype(BF16)
            dom_ref[rows, :] = (dm * gt[:, D:]).astype(BF16)
            dgl = jnp.concatenate([dm * of_ref[rows, :], dm * om_ref[rows, :]], axis=1) * gt * (1.0 - gt)
            dgl_ref[rows, :] = dgl.astype(BF16)
            return dbg_sum + jnp.sum(dgl, axis=0, keepdims=True)

        dbg = _by_chunks(tm, D, step, jnp.zeros((1, 2 * D), F32))

        @pl.when(pl.program_id(0) == 0)
        def _():
            dbg_ref[...] = jnp.zeros_like(dbg_ref)

        dbg_ref[...] += dbg

    row = pl.BlockSpec((tm, D), lambda i: (i, 0))
    wide = pl.BlockSpec((tm, 2 * D), lambda i: (i, 0))
    one = pl.BlockSpec((1, 2 * D), lambda i: (0, 0))
    return pl.pallas_call(
        body, name=name, grid=(T // tm,),
        in_specs=[row, wide, one, row, row], out_specs=(wide, row, row, one),
        out_shape=(jax.ShapeDtypeStruct((T, 2 * D), BF16), jax.ShapeDtypeStruct((T, D), BF16),
                   jax.ShapeDtypeStruct((T, D), BF16), jax.ShapeDtypeStruct((1, 2 * D), F32)),
        compiler_params=_params(("arbitrary",)),
    )(dmix, z, bg, of, om)


def _loss(h, tgt, name):
    T, D = h.shape
    NX = tgt.shape[0]
    tm = _tile(NX, 640, 16)

    def body(h_ref, t_ref, _, __, dh_ref, dhb_ref, l_ref):
        i = pl.program_id(0)

        def step(rows, part):
            err = h_ref[rows, :] - t_ref[rows, :]
            dh = err * (1.0 / D)
            dh_ref[rows, :] = dh
            dhb_ref[rows, :] = dh.astype(BF16)
            return part + jnp.sum(err * err, axis=0, keepdims=True)

        part = _by_chunks(tm, D, step, jnp.zeros((1, D), F32))

        @pl.when(i == 0)
        def _():
            l_ref[...] = jnp.zeros_like(l_ref)

        l_ref[...] += 0.5 * jnp.sum(part) * (1.0 / D)

    row = pl.BlockSpec((tm, D), lambda i: (i, 0))
    acc = pl.BlockSpec((8, LANES), lambda i: (0, 0))
    hbm = pl.BlockSpec(memory_space=pl.ANY)
    return pl.pallas_call(
        body, name=name, grid=(NX // tm,), in_specs=[row, row, hbm, hbm], out_specs=(row, row, acc),
        out_shape=(jax.ShapeDtypeStruct((T, D), F32), jax.ShapeDtypeStruct((T, D), BF16),
                   jax.ShapeDtypeStruct((8, LANES), F32)),
        input_output_aliases={2: 0, 3: 1},
        compiler_params=_params(("arbitrary",)),
    )(h, tgt, _token_rows_buffer(T, NX, D), _token_rows_buffer(T, NX, D, BF16))


def _adamw(parts, w, m, v, name):
    P, R, C = parts.shape
    tr = _tile(R, max(8, (1 << 18) // C), 8)
    bc1 = 1.0 - ADAM_B1 ** ADAM_STEP
    bc2 = 1.0 - ADAM_B2 ** ADAM_STEP

    def body(p_ref, w_ref, m_ref, v_ref, g_ref, d_ref, m2_ref, v2_ref):
        def step(rows, carry):
            g = p_ref[0, rows, :].astype(F32)
            for j in range(1, P):
                g = g + p_ref[j, rows, :].astype(F32)
            m2 = ADAM_B1 * m_ref[rows, :] + (1.0 - ADAM_B1) * g
            v2 = ADAM_B2 * v_ref[rows, :] + (1.0 - ADAM_B2) * (g * g)
            m_hat = m2 / bc1
            v_hat = v2 / bc2
            g_ref[rows, :] = g
            d_ref[rows, :] = -ADAM_LR * (m_hat / (jnp.sqrt(v_hat) + ADAM_EPS) + ADAM_WD * w_ref[rows, :])
            m2_ref[rows, :] = m2
            v2_ref[rows, :] = v2
            return carry

        _by_chunks(tr, C, step)

    row = pl.BlockSpec((tr, C), lambda i: (i, 0))
    sh = jax.ShapeDtypeStruct((R, C), F32)
    return pl.pallas_call(
        body, name=name, grid=(R // tr,),
        in_specs=[pl.BlockSpec((P, tr, C), lambda i: (0, i, 0)), row, row, row],
        out_specs=(row, row, row, row), out_shape=(sh, sh, sh, sh),
        compiler_params=_params(("parallel",)),
    )(parts, w, m, v)


def _peer(d):
    x, y, c = lax.axis_index("x"), lax.axis_index("y"), lax.axis_index("c")
    px = 1 - x if d & 4 else x
    py = 1 - y if d & 2 else y
    pc = 1 - c if d & 1 else c
    return (px, py, pc), 4 * px + 2 * py + pc


class _Exchange:
    def __init__(self, srcs, gather):
        self.srcs, self.gather, self.n = list(srcs), gather, len(srcs)
        n = self.n
        hbm = pl.BlockSpec(memory_space=pl.ANY)
        self.in_specs = [hbm] * n
        self.out_specs = [hbm] * n
        self.out_shape = [jax.ShapeDtypeStruct((N_DEV,) + s.shape[-2:], s.dtype) for s in srcs]
        self.scratch = [pltpu.SemaphoreType.DMA((N_DEV - 1, n)), pltpu.SemaphoreType.DMA((N_DEV - 1, n)),
                        pltpu.SemaphoreType.DMA((n,))]

    def _copies(self, src_refs, out_refs, sems):
        send_sems, recv_sems, local_sems = sems
        _, me = _peer(0)

        def remote(w, d, landing):
            dev, lin = _peer(d)
            return pltpu.make_async_remote_copy(
                src_ref=src_refs[w] if self.gather else src_refs[w].at[lin],
                dst_ref=out_refs[w].at[lin if landing else me],
                send_sem=send_sems.at[d - 1, w], recv_sem=recv_sems.at[d - 1, w],
                device_id=dev, device_id_type=pl.DeviceIdType.MESH)

        pairs = [(w, d) for d in range(1, N_DEV) for w in range(self.n)]
        own = [pltpu.make_async_copy(src_refs[w] if self.gather else src_refs[w].at[me],
                                     out_refs[w].at[me], local_sems.at[w]) for w in range(self.n)]
        return own, [remote(w, d, False) for w, d in pairs], [remote(w, d, True) for w, d in pairs]

    def _gather_copies(self, src_refs, out_refs, sems):
        send_sems, recv_sems, local_sems = sems
        x, y, c = lax.axis_index("x"), lax.axis_index("y"), lax.axis_index("c")
        me, sibling = (x, y, c), (x, y, 1 - c)
        chips = [(1 - x, y), (x, 1 - y), (1 - x, 1 - y)]

        def copy(w, k, block, to, src=None):
            rows = out_refs[w].at[4 * block[0] + 2 * block[1] + block[2]]
            return pltpu.make_async_remote_copy(
                src_ref=rows if src is None else src, dst_ref=rows,
                send_sem=send_sems.at[k, w], recv_sem=recv_sems.at[k, w],
                device_id=to, device_id_type=pl.DeviceIdType.MESH)

        ws = range(self.n)
        own = [pltpu.make_async_copy(src_refs[w], out_refs[w].at[4 * x + 2 * y + c], local_sems.at[w])
               for w in ws]
        first = [copy(w, 0, me, sibling, src_refs[w]) for w in ws]
        first += [copy(w, 1 + j, me, (*chip, c), src_refs[w]) for j, chip in enumerate(chips) for w in ws]
        landed = [[copy(w, 1 + j, (*chip, c), me) for w in ws] for j, chip in enumerate(chips)]
        passed = [[copy(w, 4 + j, (*chip, c), sibling) for w in ws] for j, chip in enumerate(chips)]
        from_sibling = [copy(w, 0, sibling, me) for w in ws]
        from_sibling += [copy(w, 4 + j, (*chip, 1 - c), me) for j, chip in enumerate(chips) for w in ws]
        return own, first, landed, passed, from_sibling

    def start(self, src_refs, out_refs, sems):
        if self.gather:
            own, first = self._gather_copies(src_refs, out_refs, sems)[:2]
            sent = first
        else:
            own, sent, _ = self._copies(src_refs, out_refs, sems)
        for cp in own + sent:
            cp.start()

    def wait(self, src_refs, out_refs, sems):
        if self.gather:
            own, first, landed, passed, from_sibling = self._gather_copies(src_refs, out_refs, sems)
            for arrived, onward in zip(landed, passed):
                for cp in arrived:
                    cp.wait_recv()
                for cp in onward:
                    cp.start()
            for cp in from_sibling:
                cp.wait_recv()
            for cp in first + [cp for group in passed for cp in group]:
                cp.wait_send()
        else:
            own, sent, landing = self._copies(src_refs, out_refs, sems)
            for cp in landing:
                cp.wait_recv()
            for cp in sent:
                cp.wait_send()
        for cp in own:
            cp.wait()


def _exchange(srcs, name, gather):
    ex = _Exchange(srcs, gather)
    n = ex.n

    def body(*refs):
        ex.start(refs[:n], refs[n:2 * n], refs[2 * n:])
        ex.wait(refs[:n], refs[n:2 * n], refs[2 * n:])

    outs = pl.pallas_call(
        body, name=name, in_specs=ex.in_specs, out_specs=tuple(ex.out_specs),
        out_shape=tuple(ex.out_shape), scratch_shapes=ex.scratch,
    )(*srcs)
    return list(outs)


def _call(body, name, grid, in_specs, out_specs, out_shape, ins, scratch_shapes=(), sem=None,
          aliases=None, exch=None):
    aliases = aliases or {}
    if exch is None:
        outs = pl.pallas_call(
            body, name=name, grid=grid, in_specs=list(in_specs), out_specs=tuple(out_specs),
            out_shape=tuple(out_shape), scratch_shapes=list(scratch_shapes),
            input_output_aliases=aliases, compiler_params=_params(sem),
        )(*ins)
        return list(outs), []
    ni, no, ns, n = len(in_specs), len(out_specs), len(scratch_shapes), exch.n
    last_ids = [g - 1 for g in grid]

    def hosted(*refs):
        cin, xin = refs[:ni], refs[ni:ni + n]
        cout, xout = refs[ni + n:ni + n + no], refs[ni + n + no:ni + 2 * n + no]
        cscr, xsem = refs[ni + 2 * n + no:ni + 2 * n + no + ns], refs[ni + 2 * n + no + ns:]
        ids = [pl.program_id(a) for a in range(len(grid))]
        first, last = ids[0] == 0, ids[0] == last_ids[0]
        for a in range(1, len(grid)):
            first, last = first & (ids[a] == 0), last & (ids[a] == last_ids[a])

        @pl.when(first)
        def _():
            exch.start(xin, xout, xsem)

        body(*cin, *cout, *cscr)

        @pl.when(last)
        def _():
            exch.wait(xin, xout, xsem)

    outs = pl.pallas_call(
        hosted, name=name, grid=grid, in_specs=list(in_specs) + exch.in_specs,
        out_specs=tuple(list(out_specs) + exch.out_specs),
        out_shape=tuple(list(out_shape) + exch.out_shape),
        scratch_shapes=list(scratch_shapes) + exch.scratch, input_output_aliases=aliases,
        compiler_params=_params(("arbitrary",) * len(grid)),
    )(*ins, *exch.srcs)
    return list(outs[:no]), list(outs[no:])


def _pack(arrs, cols, row_mult):
    flat = jnp.concatenate([a.reshape(-1) for a in arrs])
    n = flat.shape[0]
    quantum = cols * row_mult
    total = -(-n // quantum) * quantum
    return jnp.pad(flat, (0, total - n)).reshape(total // cols, cols)


def _pack_rows(arrs, cols, row_mult):
    flat = jnp.concatenate(arrs, axis=1)
    n = flat.shape[1]
    quantum = cols * row_mult
    total = -(-n // quantum) * quantum
    return jnp.pad(flat, ((0, 0), (0, total - n))).reshape(N_DEV, total // cols, cols)


def _unpack(packed, shapes):
    flat = packed.reshape(-1)
    out, off = [], 0
    for s in shapes:
        n = int(np.prod(s))
        out.append(flat[off:off + n].reshape(s))
        off += n
    return out


def _rope_tables(positions):
    inv_freq = ROPE_THETA ** (-jnp.arange(0, MLA_ROPE, 2, dtype=F32) / MLA_ROPE)
    ang = positions.astype(F32)[:, None] * inv_freq[None, :]
    cos, sin = jnp.cos(ang), jnp.sin(ang)
    n = positions.shape[0]
    ones, zeros = jnp.ones((n, MLA_NOPE), F32), jnp.zeros((n, MLA_NOPE), F32)
    tail1, tail0 = jnp.ones((n, LANES - MLA_QK), F32), jnp.zeros((n, LANES - MLA_QK), F32)
    z16 = jnp.zeros((n, 16), F32)
    c = jnp.concatenate([ones, cos, cos, tail1], axis=1)
    s1 = jnp.concatenate([zeros, -sin, z16, tail0], axis=1)
    s2 = jnp.concatenate([zeros, z16, sin, tail0], axis=1)
    return c, s1, s2


def kernel(x, meta_tokens, ffn1_norm, ffn1_w_gu, ffn1_w_down, mix_norm, w_in, b_forget, b_gate, fox_q_norm, fox_k_norm, mla_cq_norm, mla_w_uq, mla_ckv_norm, mla_w_ukv, mla_q_norm, mla_k_norm, w_branch_fox, w_branch_mla, w_out, ffn2_norm, ffn2_w_gu, ffn2_w_down, loss_target, m_meta_tokens, m_ffn1_norm, m_ffn1_w_gu, m_ffn1_w_down, m_mix_norm, m_w_in, m_b_forget, m_b_gate, m_fox_q_norm, m_fox_k_norm, m_mla_cq_norm, m_mla_w_uq, m_mla_ckv_norm, m_mla_w_ukv, m_mla_q_norm, m_mla_k_norm, m_w_branch_fox, m_w_branch_mla, m_w_out, m_ffn2_norm, m_ffn2_w_gu, m_ffn2_w_down, v_meta_tokens, v_ffn1_norm, v_ffn1_w_gu, v_ffn1_w_down, v_mix_norm, v_w_in, v_b_forget, v_b_gate, v_fox_q_norm, v_fox_k_norm, v_mla_cq_norm, v_mla_w_uq, v_mla_ckv_norm, v_mla_w_ukv, v_mla_q_norm, v_mla_k_norm, v_w_branch_fox, v_w_branch_mla, v_w_out, v_ffn2_norm, v_ffn2_w_gu, v_ffn2_w_down):
    names = ["meta_tokens", "ffn1_norm", "ffn1_w_gu", "ffn1_w_down", "mix_norm", "w_in", "b_forget",
             "b_gate", "fox_q_norm", "fox_k_norm", "mla_cq_norm", "mla_w_uq", "mla_ckv_norm",
             "mla_w_ukv", "mla_q_norm", "mla_k_norm", "w_branch_fox", "w_branch_mla", "w_out",
             "ffn2_norm", "ffn2_w_gu", "ffn2_w_down"]
    W = dict(zip(names, [meta_tokens, ffn1_norm, ffn1_w_gu, ffn1_w_down, mix_norm, w_in, b_forget,
                         b_gate, fox_q_norm, fox_k_norm, mla_cq_norm, mla_w_uq, mla_ckv_norm,
                         mla_w_ukv, mla_q_norm, mla_k_norm, w_branch_fox, w_branch_mla, w_out,
                         ffn2_norm, ffn2_w_gu, ffn2_w_down]))
    Mo = dict(zip(names, [m_meta_tokens, m_ffn1_norm, m_ffn1_w_gu, m_ffn1_w_down, m_mix_norm, m_w_in,
                          m_b_forget, m_b_gate, m_fox_q_norm, m_fox_k_norm, m_mla_cq_norm,
                          m_mla_w_uq, m_mla_ckv_norm, m_mla_w_ukv, m_mla_q_norm, m_mla_k_norm,
                          m_w_branch_fox, m_w_branch_mla, m_w_out, m_ffn2_norm, m_ffn2_w_gu,
                          m_ffn2_w_down]))
    Vo = dict(zip(names, [v_meta_tokens, v_ffn1_norm, v_ffn1_w_gu, v_ffn1_w_down, v_mix_norm, v_w_in,
                          v_b_forget, v_b_gate, v_fox_q_norm, v_fox_k_norm, v_mla_cq_norm,
                          v_mla_w_uq, v_mla_ckv_norm, v_mla_w_ukv, v_mla_q_norm, v_mla_k_norm,
                          v_w_branch_fox, v_w_branch_mla, v_w_out, v_ffn2_norm, v_ffn2_w_gu,
                          v_ffn2_w_down]))

    B, S, D = x.shape
    NX = B * S
    T = NX + META_BLK
    H = HEADS
    assert NX % META_BLK == 0 and S % LANES == 0
    me = 4 * lax.axis_index("x") + 2 * lax.axis_index("y") + lax.axis_index("c")

    big = [("ffn1_w_gu", 1), ("ffn1_w_down", 0), ("w_in", 1), ("mla_w_uq", 1), ("mla_w_ukv", 1),
           ("w_branch_fox", 1), ("w_branch_mla", 1), ("w_out", 0), ("ffn2_w_gu", 1), ("ffn2_w_down", 0)]
    mix_small = ["mla_w_uq", "mla_w_ukv", "w_branch_fox", "w_branch_mla", "w_out"]
    last_group = ["ffn2_w_gu", "ffn2_w_down"]
    axis_of = dict(big)
    full = {}

    def shards(group):
        return [W[n][0].astype(BF16) for n in group]

    def assemble(group, blks):
        for n, blk in zip(group, blks):
            _, r, c = blk.shape
            full[n] = (blk.transpose(1, 0, 2).reshape(r, N_DEV * c) if axis_of[n] == 1
                       else blk.reshape(N_DEV * r, c))

    got = _exchange(shards(["ffn1_w_gu"]) + [meta_tokens], "gather_first", gather=True)
    assemble(["ffn1_w_gu"], got[:1])
    meta_full = got[1].transpose(1, 0, 2).reshape(N_META, D)

    Z_G, Z_FQ = 0, 2 * D
    Z_FK, Z_FV = Z_FQ + FOX_W, Z_FQ + 2 * FOX_W
    Z_CQ = Z_FQ + 3 * FOX_W
    Z_CKV = Z_CQ + MLA_Q_RANK
    Z_F = Z_CKV + MLA_KV_RANK
    Z_KR = Z_F + LANES

    def pad_lanes(a, w=LANES):
        return jnp.pad(a, [(0, 0)] * (a.ndim - 1) + [(0, w - a.shape[-1])])

    def rows_T(real, meta=None):
        n = real.shape[1]
        parts = [real]
        used = 0
        if meta is not None:
            parts.append(meta)
            used = meta.shape[0]
        if T - NX - used:
            parts.append(jnp.zeros((T - NX - used, n), real.dtype))
        return jnp.concatenate(parts, axis=0)

    def put_meta(tok, meta_per_seq):
        return lax.dynamic_update_slice(tok, meta_per_seq.sum(0), (NX, 0))

    h0 = rows_T(x.reshape(NX, D), meta_full)
    tgt = loss_target.reshape(NX, D)

    def ffn_fwd(h, norm, w_gu, tag, behind_up=None, behind_down=None):
        u = _norm_fwd(h, 0, D, D, norm, D, D, tag + "_norm")
        (sa, sb, a), got = _ffn_up(u, w_gu, tag + "_up",
                                  exch=_Exchange(shards(behind_up), True) if behind_up else None)
        assemble(behind_up or [], got)
        h_out = _mm(a, full[tag + "_w_down"], "nn", tag + "_down", scale=0.5, res=h,
                    exch=_Exchange(shards(behind_down), True) if behind_down else None)
        if behind_down:
            h_out, got = h_out
            assemble(behind_down, got)
        return h_out, (u, sa, sb, a)

    h1, ffn1_saved = ffn_fwd(h0, W["ffn1_norm"], full["ffn1_w_gu"], "ffn1",
                             behind_up=["ffn1_w_down"], behind_down=["w_in"])
    wi = full["w_in"]
    o_fq = 0
    o_f = 3 * FOX_W
    o_cq = o_f + HEADS
    o_kr = o_cq + MLA_Q_RANK + MLA_KV_RANK
    o_g = o_kr + MLA_ROPE
    w_in_p = jnp.concatenate([
        wi[:, o_g:o_g + 2 * D], wi[:, o_fq:o_f], wi[:, o_cq:o_kr],
        jnp.pad(wi[:, o_f:o_cq], ((0, 0), (0, LANES - HEADS))),
        jnp.pad(wi[:, o_kr:o_g], ((0, 0), (0, LANES - MLA_ROPE)))], axis=1)

    u2 = _norm_fwd(h1, 0, D, D, W["mix_norm"], D, D, "mix_norm")
    z, got = _mm(u2, w_in_p, "nn", "w_in", exch=_Exchange(shards(mix_small), True))
    assemble(mix_small, got)
    w_uq_p = jnp.pad(full["mla_w_uq"].reshape(MLA_Q_RANK, H, MLA_QK),
                     ((0, 0), (0, 0), (0, LANES - MLA_QK))).reshape(MLA_Q_RANK, H * LANES)

    gq_f = jnp.tile(W["fox_q_norm"], (1, 2))
    gk_f = jnp.tile(W["fox_k_norm"], (1, 2))
    fqn = _norm_fwd(z, Z_FQ, FOX_W, LANES, gq_f, FOX_HD, FOX_HD, "fox_q_norm")
    fkn = _norm_fwd(z, Z_FK, FOX_W, LANES, gk_f, FOX_HD, FOX_HD, "fox_k_norm")
    fl = z[:NX, Z_F:Z_F + LANES].reshape(B, S, LANES)
    flm = z[NX:, Z_F:Z_F + LANES]
    bf = pad_lanes(W["b_forget"])
    cum, cumm = _cum_fwd(fl, flm, bf, "forget_cum")
    TK = min(512, S)
    ck = cum[:, :, :H].transpose(0, 2, 1).reshape(B, H, S // TK, 1, TK)
    cmk = jnp.broadcast_to(cumm[:, :H].T[None, :, None, :], (B, H, 1, META_BLK))
    (o_fox, lse_fox), got = _attn_fwd(fqn, fkn, z, Z_FV, False, FOX_HD ** -0.5, S, NX, "fox_attn", ck, cmk,
                                      exch=_Exchange(shards(last_group), True))
    assemble(last_group, got)
    of = _mm(o_fox, full["w_branch_fox"], "nn", "branch_fox")

    pos = jnp.concatenate([jnp.tile(jnp.arange(S) + N_META, B), jnp.arange(META_BLK)])
    tabs = _rope_tables(pos)
    cqn = _norm_fwd(z, Z_CQ, MLA_Q_RANK, MLA_Q_RANK, W["mla_cq_norm"], MLA_Q_RANK, MLA_Q_RANK, "mla_cq_norm")
    q_lin = _mm(cqn, w_uq_p, "nn", "mla_uq")
    ckvn = _norm_fwd(z, Z_CKV, MLA_KV_RANK, MLA_KV_RANK, W["mla_ckv_norm"], MLA_KV_RANK, MLA_KV_RANK,
                     "mla_ckv_norm")
    kv_lin = _mm(ckvn, full["mla_w_ukv"], "nn", "mla_ukv")
    gq_m, gk_m = pad_lanes(W["mla_q_norm"]), pad_lanes(W["mla_k_norm"])
    mqn = _norm_fwd(q_lin, 0, H * LANES, LANES, gq_m, LANES, MLA_QK, "mla_q_norm", tabs=tabs)
    mkn = _mla_k_fwd(kv_lin, z, Z_KR, gk_m, tabs, "mla_k_norm")
    (o_mla, lse_mla), _ = _attn_fwd(mqn, mkn, kv_lin, 0, True, MLA_QK ** -0.5, S, NX, "mla_attn")
    om = _mm(o_mla, full["w_branch_mla"], "nn", "branch_mla")

    mix = _gate_fwd(z, W["b_gate"], of, om, "gate_mix")
    h2 = _mm(mix, full["w_out"], "nn", "w_out", res=h1)

    h3, ffn2_saved = ffn_fwd(h2, W["ffn2_norm"], full["ffn2_w_gu"], "ffn2")

    dh3, dh3_b, loss_acc = _loss(h3, tgt, "loss")
    loss = lax.psum(loss_acc[0, 0], AXES)

    G = {}
    parts = {}

    def scatter_of(group):
        per_dest = []
        for n in group:
            r, c = W[n].shape[1:]
            per_dest.append((G[n].reshape(r, N_DEV, c).transpose(1, 0, 2) if axis_of[n] == 1
                             else G[n].reshape(N_DEV, r, c)).astype(BF16))
        return _Exchange(per_dest, False)

    def ffn_bwd(dh, dh_b, h, norm, w_gu, w_down, saved, tag, behind_down=None, spread=False):
        u, sa, sb, a = saved
        G[tag + "_w_down"] = _mm(a, dh_b, "tn", tag + "_dw_down", scale=0.5)
        (dg, dup), got = _ffn_down_bwd(dh_b, w_down, sa, sb, tag + "_down_bwd",
                                       exch=scatter_of(behind_down) if behind_down else None)
        parts.update(zip(behind_down or [], got))
        dw_g = _mm(u, dg, "tn", tag + "_dw_g", exch=scatter_of([tag + "_w_down"]) if spread else None)
        if spread:
            dw_g, got = dw_g
            parts[tag + "_w_down"] = got[0]
        G[tag + "_w_gu"] = jnp.concatenate([dw_g, _mm(u, dup, "tn", tag + "_dw_u")], axis=1)
        (du,), got = _ffn_up_bwd_dx(dg, dup, w_gu, tag + "_up_bwd",
                                    exch=scatter_of([tag + "_w_gu"]) if spread else None)
        if spread:
            parts[tag + "_w_gu"] = got[0]
        *dh_in, G[tag + "_norm"] = _norm_bwd(h, 0, D, D, norm, D, D, du, tag + "_norm_bwd", res=dh,
                                             bf16_copy=not spread)
        return dh_in

    dh2, dh2_b = ffn_bwd(dh3, dh3_b, h2, W["ffn2_norm"], full["ffn2_w_gu"], full["ffn2_w_down"], ffn2_saved,
                         "ffn2")

    G["w_out"] = _mm(mix, dh2_b, "tn", "dw_out")
    dmix = _mm(dh2_b, full["w_out"], "nt", "w_out_bwd")
    dgl, dof, dom, G["b_gate"] = _gate_bwd(dmix, z, W["b_gate"], of, om, "gate_bwd")

    G["w_branch_fox"] = _mm(o_fox, dof, "tn", "dw_branch_fox")
    do_fox = _mm(dof, full["w_branch_fox"], "nt", "branch_fox_bwd")
    (dq_f, dk_f, dv_f, dkm_f, dvm_f, dck, dcmk, dcq), got = _attn_bwd(
        fqn, fkn, z, Z_FV, o_fox, lse_fox, do_fox, False, FOX_HD ** -0.5, S, NX, "fox_attn_bwd", ck, cmk,
        exch=scatter_of(last_group))
    parts.update(zip(last_group, got))
    dk_f, dv_f = put_meta(dk_f, dkm_f), put_meta(dv_f, dvm_f)
    dfq, gq = _norm_bwd(z, Z_FQ, FOX_W, LANES, gq_f, FOX_HD, FOX_HD, dq_f, "fox_q_norm_bwd", out_dtype=BF16)
    dfk, gk = _norm_bwd(z, Z_FK, FOX_W, LANES, gk_f, FOX_HD, FOX_HD, dk_f, "fox_k_norm_bwd", out_dtype=BF16)
    G["fox_q_norm"] = gq[:, :FOX_HD] + gq[:, FOX_HD:]
    G["fox_k_norm"] = gk[:, :FOX_HD] + gk[:, FOX_HD:]
    dc = pad_lanes(dck.reshape(B, H, S).transpose(0, 2, 1)
                   + dcq.transpose(0, 2, 1, 3).reshape(B, S, H))
    dcm = pad_lanes(dcmk.sum(0)[:, 0, :].T)
    dcm = jnp.where(jnp.arange(LANES)[:, None] < N_META, dcm, 0.0)
    dfl, dflm, dbf = _cum_bwd(dc, dcm, fl, flm, bf, "forget_cum_bwd")
    G["b_forget"] = dbf[:, :HEADS]
    dfl_t = rows_T(dfl.reshape(NX, LANES), dflm)

    G["w_branch_mla"] = _mm(o_mla, dom, "tn", "dw_branch_mla")
    do_mla = _mm(dom, full["w_branch_mla"], "nt", "branch_mla_bwd")
    (dq_m, dk_m, dvk, dkm_m, dvkm), _ = _attn_bwd(
        mqn, mkn, kv_lin, 0, o_mla, lse_mla, do_mla, True, MLA_QK ** -0.5, S, NX, "mla_attn_bwd")
    dk_m, dvk = put_meta(dk_m, dkm_m), put_meta(dvk, dvkm)
    dq_lin, gq = _norm_bwd(q_lin, 0, H * LANES, LANES, gq_m, LANES, MLA_QK, dq_m, "mla_q_norm_bwd", tabs=tabs,
                           out_dtype=BF16)
    G["mla_q_norm"] = gq[:, :MLA_QK]
    G["mla_w_uq"] = _mm(cqn, dq_lin, "tn", "dw_uq").reshape(MLA_Q_RANK, H, LANES)[:, :, :MLA_QK].reshape(
        MLA_Q_RANK, H * MLA_QK)
    dcqn = _mm(dq_lin, w_uq_p, "nt", "mla_uq_bwd")
    dcq, G["mla_cq_norm"] = _norm_bwd(z, Z_CQ, MLA_Q_RANK, MLA_Q_RANK, W["mla_cq_norm"], MLA_Q_RANK,
                                      MLA_Q_RANK, dcqn, "mla_cq_norm_bwd", out_dtype=BF16)
    dkv_lin, dkr, gk = _mla_k_bwd(kv_lin, z, Z_KR, gk_m, tabs, dk_m, dvk, "mla_k_norm_bwd")
    G["mla_k_norm"] = gk[:, :MLA_QK]
    G["mla_w_ukv"] = _mm(ckvn, dkv_lin, "tn", "dw_ukv")
    dckvn = _mm(dkv_lin, full["mla_w_ukv"], "nt", "mla_ukv_bwd")
    dckv, G["mla_ckv_norm"] = _norm_bwd(z, Z_CKV, MLA_KV_RANK, MLA_KV_RANK, W["mla_ckv_norm"], MLA_KV_RANK,
                                        MLA_KV_RANK, dckvn, "mla_ckv_norm_bwd", out_dtype=BF16)

    dz = jnp.concatenate([dgl, dfq, dfk, dv_f.astype(BF16), dcq, dckv, dfl_t.astype(BF16),
                          dkr.astype(BF16)], axis=1)
    dw_in_p = _mm(u2, dz, "tn", "dw_in")
    G["w_in"] = jnp.concatenate([
        dw_in_p[:, Z_FQ:Z_CQ], dw_in_p[:, Z_F:Z_F + HEADS], dw_in_p[:, Z_CQ:Z_F],
        dw_in_p[:, Z_KR:Z_KR + MLA_ROPE], dw_in_p[:, Z_G:Z_G + 2 * D]], axis=1)
    du2, got = _mm(dz, w_in_p, "nt", "w_in_bwd", exch=scatter_of(mix_small))
    parts.update(zip(mix_small, got))
    dh1, dh1_b, G["mix_norm"] = _norm_bwd(h1, 0, D, D, W["mix_norm"], D, D, du2, "mix_norm_bwd", res=dh2,
                                          bf16_copy=True)

    (dh0,) = ffn_bwd(dh1, dh1_b, h0, W["ffn1_norm"], full["ffn1_w_gu"], full["ffn1_w_down"], ffn1_saved, "ffn1",
                     behind_down=["w_in"], spread=True)
    grad_x = dh0[:NX].reshape(B, S, D)
    G["meta_tokens"] = dh0[NX:NX + N_META]

    res = {}
    for n, _ in big:
        outs4 = _adamw(parts[n], W[n][0], Mo[n][0], Vo[n][0], "adamw_" + n)
        for key, arr in zip(("g", "d", "m", "v"), outs4):
            res[key, n] = arr[None]

    small = [n for n in names if n not in dict(big) and n != "meta_tokens"]
    small_shapes = [W[n].shape for n in small]
    spack = _pack([G["meta_tokens"]] + [G[n] for n in small], 1024, 8)
    (sparts,) = _exchange([spack], "gather_small_grads", gather=True)
    sflat = sparts.reshape(N_DEV, -1)
    dsh = D // N_DEV
    meta_part = lax.dynamic_slice(sflat[:, :N_META * D].reshape(N_DEV, N_META, D),
                                  (0, 0, me * dsh), (N_DEV, N_META, dsh)).reshape(N_DEV, -1)
    rep_len = sum(int(np.prod(s)) for s in small_shapes)
    rep_part = sflat[:, N_META * D:N_META * D + rep_len]
    sp = _pack_rows([meta_part, rep_part], LANES, _chunk_rows(LANES))
    pks = lambda src: _pack([src["meta_tokens"]] + [src[n] for n in small], LANES, _chunk_rows(LANES))
    g_s, d_s, m_s, v_s = _adamw(sp, pks(W), pks(Mo), pks(Vo), "adamw_small")
    shapes_s = [W["meta_tokens"].shape] + small_shapes
    for key, packed in (("g", g_s), ("d", d_s), ("m", m_s), ("v", v_s)):
        for n, arr in zip(["meta_tokens"] + small, _unpack(packed, shapes_s)):
            res[key, n] = arr

    outs = [loss, grad_x]
    for key in ("g", "d", "m", "v"):
        outs += [res[key, n] for n in names]
    return tuple(outs)
```

```python
import numpy as np
import jax
import jax.numpy as jnp
from jax import lax
from jax.experimental import pallas as pl
from jax.experimental.pallas import tpu as pltpu

F32 = jnp.float32
BF16 = jnp.bfloat16

N_META = 16
EPS = 1e-6
HEADS = 8
FOX_HD = 64
FOX_W = HEADS * FOX_HD
MLA_Q_RANK = 256
MLA_KV_RANK = 128
MLA_NOPE = 64
MLA_ROPE = 32
MLA_QK = MLA_NOPE + MLA_ROPE
MLA_V = 64
ROPE_THETA = 10000.0
LANES = 128
HALF = LANES // 2
META_BLK = 128
NEG = -1e30

ADAM_LR = 0.001
ADAM_B1 = 0.9
ADAM_B2 = 0.999
ADAM_EPS = 1e-08
ADAM_WD = 0.01
ADAM_STEP = 10

N_DEV = 8
AXES = ("x", "y", "c")
VMEM_LIMIT_BYTES = 56 * 1024 * 1024


def _tile(n, cap, mult):
    best = None
    for d in range(mult, min(n, cap) + 1, mult):
        if n % d == 0:
            best = d
    return n if best is None else best


VREG_ELEMS = 8 * LANES


def _row_tile(rows, width):
    return _tile(rows, max(16, (1 << 19) // width), 16)


def _chunk_rows(width):
    rows = 16
    while 2 * rows * width <= 8 * VREG_ELEMS:
        rows *= 2
    return rows


def _by_chunks(rows, width, step, init=()):
    return step(pl.ds(0, rows), init)


def _params(sem=None):
    return pltpu.CompilerParams(dimension_semantics=sem, vmem_limit_bytes=VMEM_LIMIT_BYTES)


def _mm(a, b, mode, name, out_dtype=F32, scale=1.0, res=None, exch=None):
    if mode == "nn":
        (M, K), (K2, N) = a.shape, b.shape
    elif mode == "nt":
        (M, K), (N, K2) = a.shape, b.shape
    else:
        (K, M), (K2, N) = a.shape, b.shape
    assert K == K2, (a.shape, b.shape, mode)
    if mode == "tn":
        tm, tk = _tile(M, 1408, 128), _tile(K, 2080, 16)
    else:
        tm, tk = _tile(M, 832, 16), _tile(K, 4224, 128)
    tn = _tile(N, 1408, 128)
    nk = K // tk
    ni, nj = M // tm, N // tn
    bytes_a, bytes_b = a.size * a.dtype.itemsize, b.size * b.dtype.itemsize
    j_outer = nk == 1 and bytes_a * nj + bytes_b < bytes_a + bytes_b * ni
    ij = (lambda g0, g1: (g1, g0)) if j_outer else (lambda g0, g1: (g0, g1))

    def spec(shape, at):
        return pl.BlockSpec(shape, lambda g0, g1, k: at(*ij(g0, g1), k))

    a_spec = {"nn": spec((tm, tk), lambda i, j, k: (i, k)),
              "nt": spec((tm, tk), lambda i, j, k: (i, k)),
              "tn": spec((tk, tm), lambda i, j, k: (k, i))}[mode]
    b_spec = {"nn": spec((tk, tn), lambda i, j, k: (k, j)),
              "nt": spec((tn, tk), lambda i, j, k: (j, k)),
              "tn": spec((tk, tn), lambda i, j, k: (k, j))}[mode]
    dims = {"nn": (((1,), (0,)), ((), ())), "nt": (((1,), (1,)), ((), ())),
            "tn": (((0,), (0,)), ((), ()))}[mode]
    o_spec = spec((tm, tn), lambda i, j, k: (i, j))
    has_res = res is not None

    def body(*refs):
        a_ref, b_ref = refs[:2]
        r_ref = refs[2] if has_res else None
        o_ref = refs[2 + has_res]

        def finish(acc):
            o = acc * scale
            if has_res:
                o = o + r_ref[...]
            o_ref[...] = o.astype(out_dtype)

        prod = lax.dot_general(a_ref[...].astype(BF16), b_ref[...].astype(BF16), dims,
                               preferred_element_type=F32)
        if nk == 1:
            finish(prod)
            return
        acc_ref = refs[3 + has_res]
        k = pl.program_id(2)

        @pl.when(k == 0)
        def _():
            acc_ref[...] = prod

        @pl.when((k > 0) & (k < nk - 1))
        def _():
            acc_ref[...] += prod

        @pl.when(k == nk - 1)
        def _():
            finish(acc_ref[...] + prod)

    ins = [a, b] + ([res] if has_res else [])
    specs = [a_spec, b_spec] + ([o_spec] if has_res else [])
    (out,), got = _call(
        body, name, (nj, ni, nk) if j_outer else (ni, nj, nk), specs, [o_spec],
        [jax.ShapeDtypeStruct((M, N), out_dtype)],
        ins, scratch_shapes=[pltpu.VMEM((tm, tn), F32)] if nk > 1 else [],
        sem=("parallel", "parallel", "arbitrary"), exch=exch)
    return out if exch is None else (out, got)


def _rope_fwd(y, c, s1, s2):
    return y * c + pltpu.roll(y, LANES - 16, 1) * s1 + pltpu.roll(y, 16, 1) * s2


def _rope_bwd(dy, c, s1, s2):
    return dy * c + pltpu.roll(dy * s1, 16, 1) + pltpu.roll(dy * s2, LANES - 16, 1)


def _group_sum(v, seg):
    if seg == v.shape[-1]:
        return jnp.sum(v, axis=-1, keepdims=True)
    lo = lax.broadcasted_iota(jnp.int32, v.shape, 1) < seg
    s_lo = jnp.sum(jnp.where(lo, v, 0.0), axis=-1, keepdims=True)
    s_hi = jnp.sum(jnp.where(lo, 0.0, v), axis=-1, keepdims=True)
    return jnp.where(lo, s_lo, s_hi)


def _norm_fwd(src, col0, width, bw, gain, seg, d_true, name, tabs=None, out_dtype=BF16):
    T = src.shape[0]
    tr = _row_tile(T, bw)
    inv_d = 1.0 / d_true
    c0 = col0 // bw
    assert col0 % bw == 0 and width % bw == 0

    def body(*refs):
        if tabs is None:
            x_ref, g_ref, o_ref = refs
        else:
            x_ref, g_ref, c_ref, s1_ref, s2_ref, o_ref = refs
        gain_v = g_ref[...]

        def step(rows, carry):
            xv = x_ref[rows, :]
            r = lax.rsqrt(_group_sum(xv * xv, seg) * inv_d + EPS)
            y = xv * r * gain_v
            if tabs is not None:
                y = _rope_fwd(y, c_ref[rows, :], s1_ref[rows, :], s2_ref[rows, :])
            o_ref[rows, :] = y.astype(out_dtype)
            return carry

        _by_chunks(tr, bw, step)

    specs = [pl.BlockSpec((tr, bw), lambda i, j: (i, c0 + j)), pl.BlockSpec((1, bw), lambda i, j: (0, 0))]
    ins = [src, gain]
    if tabs is not None:
        tab = pl.BlockSpec((tr, LANES), lambda i, j: (i, 0))
        specs += [tab, tab, tab]
        ins += list(tabs)
    return pl.pallas_call(
        body, name=name, grid=(T // tr, width // bw), in_specs=specs,
        out_specs=pl.BlockSpec((tr, bw), lambda i, j: (i, j)),
        out_shape=jax.ShapeDtypeStruct((T, width), out_dtype),
        compiler_params=_params(("parallel", "parallel")),
    )(*ins)


def _norm_bwd_math(xv, gain, dyv, seg, inv_d):
    r = lax.rsqrt(_group_sum(xv * xv, seg) * inv_d + EPS)
    gy = dyv * gain
    dot = _group_sum(gy * xv, seg)
    dx = r * gy - xv * (r * r * r * inv_d) * dot
    return dx, jnp.sum(dyv * xv * r, axis=0, keepdims=True)


def _norm_bwd(src, col0, width, bw, gain, seg, d_true, dy, name, tabs=None, res=None, out_dtype=F32,
              bf16_copy=False):
    T = src.shape[0]
    tr = _row_tile(T, bw)
    inv_d = 1.0 / d_true
    c0 = col0 // bw
    has_res = res is not None

    def body(*refs):
        refs = list(refs)
        x_ref, g_ref, dy_ref = refs[:3]
        pos = 3
        if tabs is not None:
            c_ref, s1_ref, s2_ref = refs[3:6]
            pos = 6
        if has_res:
            r_ref = refs[pos]
            pos += 1
        dx_ref = refs[pos]
        dxb_ref = refs[pos + 1] if bf16_copy else None
        dg_ref = refs[pos + 1 + bf16_copy]
        gain_v = g_ref[...]

        def step(rows, dg_sum):
            dyv = dy_ref[rows, :].astype(F32)
            if tabs is not None:
                dyv = _rope_bwd(dyv, c_ref[rows, :], s1_ref[rows, :], s2_ref[rows, :])
            dx, dg = _norm_bwd_math(x_ref[rows, :], gain_v, dyv, seg, inv_d)
            if has_res:
                dx = dx + r_ref[rows, :]
            dx_ref[rows, :] = dx.astype(out_dtype)
            if bf16_copy:
                dxb_ref[rows, :] = dx.astype(BF16)
            return dg_sum + dg

        dg = _by_chunks(tr, bw, step, jnp.zeros((1, bw), F32))

        @pl.when((pl.program_id(0) == 0) & (pl.program_id(1) == 0))
        def _():
            dg_ref[...] = jnp.zeros_like(dg_ref)

        dg_ref[...] += dg

    blk = pl.BlockSpec((tr, bw), lambda i, j: (i, j))
    one = pl.BlockSpec((1, bw), lambda i, j: (0, 0))
    specs = [pl.BlockSpec((tr, bw), lambda i, j: (i, c0 + j)), one, blk]
    ins = [src, gain, dy]
    if tabs is not None:
        tab = pl.BlockSpec((tr, LANES), lambda i, j: (i, 0))
        specs += [tab, tab, tab]
        ins += list(tabs)
    if has_res:
        specs.append(blk)
        ins.append(res)
    extra = bf16_copy * [blk]
    extra_shape = bf16_copy * [jax.ShapeDtypeStruct((T, width), BF16)]
    return pl.pallas_call(
        body, name=name, grid=(T // tr, width // bw), in_specs=specs, out_specs=(blk, *extra, one),
        out_shape=(jax.ShapeDtypeStruct((T, width), out_dtype), *extra_shape,
                   jax.ShapeDtypeStruct((1, bw), F32)),
        compiler_params=_params(("arbitrary", "arbitrary")),
    )(*ins)


def _mla_k_raw(kv, kr):
    lane = lax.broadcasted_iota(jnp.int32, kv.shape, 1)
    return jnp.where(lane < MLA_NOPE, kv, jnp.where(lane < MLA_QK, pltpu.roll(kr, MLA_NOPE, 1), 0.0))


def _mla_k_fwd(kv_lin, z, kr_col, gain, tabs, name):
    T, W = kv_lin.shape
    tr = _row_tile(T, LANES)
    krb = kr_col // LANES
    inv_d = 1.0 / MLA_QK

    def body(kv_ref, kr_ref, g_ref, c_ref, s1_ref, s2_ref, o_ref):
        gain_v = g_ref[...]

        def step(rows, carry):
            xv = _mla_k_raw(kv_ref[rows, :], kr_ref[rows, :])
            r = lax.rsqrt(jnp.sum(xv * xv, axis=-1, keepdims=True) * inv_d + EPS)
            o_ref[rows, :] = _rope_fwd(xv * r * gain_v, c_ref[rows, :], s1_ref[rows, :],
                                       s2_ref[rows, :]).astype(BF16)
            return carry

        _by_chunks(tr, LANES, step)

    blk = pl.BlockSpec((tr, LANES), lambda i, h: (i, h))
    tab = pl.BlockSpec((tr, LANES), lambda i, h: (i, 0))
    return pl.pallas_call(
        body, name=name, grid=(T // tr, W // LANES),
        in_specs=[blk, pl.BlockSpec((tr, LANES), lambda i, h: (i, krb)),
                  pl.BlockSpec((1, LANES), lambda i, h: (0, 0)), tab, tab, tab],
        out_specs=blk, out_shape=jax.ShapeDtypeStruct((T, W), BF16),
        compiler_params=_params(("parallel", "parallel")),
    )(kv_lin, z, gain, *tabs)


def _mla_k_bwd(kv_lin, z, kr_col, gain, tabs, dk, dvk, name):
    T, W = kv_lin.shape
    tr = _row_tile(T, LANES)
    krb = kr_col // LANES
    inv_d = 1.0 / MLA_QK

    def body(kv_ref, kr_ref, g_ref, c_ref, s1_ref, s2_ref, dk_ref, dvk_ref, dkv_ref, dkr_ref, dg_ref):
        h = pl.program_id(1)
        gain_v = g_ref[...]

        @pl.when(h == 0)
        def _():
            dkr_ref[...] = jnp.zeros_like(dkr_ref)

        def step(rows, dg_sum):
            xv = _mla_k_raw(kv_ref[rows, :], kr_ref[rows, :])
            dyv = _rope_bwd(dk_ref[rows, :], c_ref[rows, :], s1_ref[rows, :], s2_ref[rows, :])
            dx, dg = _norm_bwd_math(xv, gain_v, dyv, LANES, inv_d)
            lane = lax.broadcasted_iota(jnp.int32, dx.shape, 1)
            dkv_ref[rows, :] = jnp.where(lane < MLA_NOPE, dx, dvk_ref[rows, :]).astype(BF16)
            dkr_ref[rows, :] += pltpu.roll(jnp.where((lane >= MLA_NOPE) & (lane < MLA_QK), dx, 0.0),
                                           LANES - MLA_NOPE, 1)
            return dg_sum + dg

        dg = _by_chunks(tr, LANES, step, jnp.zeros((1, LANES), F32))

        @pl.when((pl.program_id(0) == 0) & (h == 0))
        def _():
            dg_ref[...] = jnp.zeros_like(dg_ref)

        dg_ref[...] += dg

    blk = pl.BlockSpec((tr, LANES), lambda i, h: (i, h))
    tab = pl.BlockSpec((tr, LANES), lambda i, h: (i, 0))
    one = pl.BlockSpec((1, LANES), lambda i, h: (0, 0))
    return pl.pallas_call(
        body, name=name, grid=(T // tr, W // LANES),
        in_specs=[blk, pl.BlockSpec((tr, LANES), lambda i, h: (i, krb)), one, tab, tab, tab, blk, blk],
        out_specs=(blk, tab, one),
        out_shape=(jax.ShapeDtypeStruct((T, W), BF16), jax.ShapeDtypeStruct((T, LANES), F32),
                   jax.ShapeDtypeStruct((1, LANES), F32)),
        compiler_params=_params(("arbitrary", "arbitrary")),
    )(kv_lin, z, gain, *tabs, dk, dvk)


def _ffn_up(u, w_gu, name, exch=None):
    T, D = u.shape
    F = w_gu.shape[1] // 2
    tm, tn = _tile(T, 640, 16), _tile(F, 1408, 128)
    nj = F // tn

    def body(u_ref, wg_ref, wu_ref, sa_ref, sb_ref, a_ref):
        uv = u_ref[...]
        g = jnp.dot(uv, wg_ref[...], preferred_element_type=F32)
        up = jnp.dot(uv, wu_ref[...], preferred_element_type=F32)
        sg = jax.nn.sigmoid(g)
        silu = g * sg
        sa_ref[...] = silu.astype(BF16)
        sb_ref[...] = (up * (sg + silu * (1.0 - sg))).astype(BF16)
        a_ref[...] = (silu * up).astype(BF16)

    o_spec = pl.BlockSpec((tm, tn), lambda j, i: (i, j))
    sh = jax.ShapeDtypeStruct((T, F), BF16)
    return _call(
        body, name, (nj, T // tm),
        [pl.BlockSpec((tm, D), lambda j, i: (i, 0)),
         pl.BlockSpec((D, tn), lambda j, i: (0, j)),
         pl.BlockSpec((D, tn), lambda j, i: (0, j + nj))],
        [o_spec, o_spec, o_spec], [sh, sh, sh], [u, w_gu, w_gu],
        sem=("parallel", "parallel"), exch=exch)


def _ffn_down_bwd(dh, w_down, sa, sb, name, exch=None):
    T, D = dh.shape
    F = w_down.shape[0]
    tm, tn = _tile(T, 640, 16), _tile(F, 1408, 128)

    def body(dh_ref, w_ref, sa_ref, sb_ref, dg_ref, dup_ref):
        da = 0.5 * lax.dot_general(dh_ref[...].astype(BF16), w_ref[...], (((1,), (1,)), ((), ())),
                                   preferred_element_type=F32)
        dup_ref[...] = (da * sa_ref[...].astype(F32)).astype(BF16)
        dg_ref[...] = (da * sb_ref[...].astype(F32)).astype(BF16)

    t_spec = pl.BlockSpec((tm, tn), lambda j, i: (i, j))
    sh = jax.ShapeDtypeStruct((T, F), BF16)
    return _call(
        body, name, (F // tn, T // tm),
        [pl.BlockSpec((tm, D), lambda j, i: (i, 0)),
         pl.BlockSpec((tn, D), lambda j, i: (j, 0)), t_spec, t_spec],
        [t_spec, t_spec], [sh, sh], [dh, w_down, sa, sb],
        sem=("parallel", "parallel"), exch=exch)


def _ffn_up_bwd_dx(dg, dup, w_gu, name, exch=None):
    T, F = dg.shape
    D = w_gu.shape[0]
    tm, tk = _tile(T, 640, 16), _tile(F, 2816, 128)
    nk = F // tk
    nt = (((1,), (1,)), ((), ()))

    def body(dg_ref, dup_ref, wg_ref, wu_ref, o_ref, acc_ref):
        k = pl.program_id(1)
        prod = (lax.dot_general(dg_ref[...], wg_ref[...], nt, preferred_element_type=F32)
                + lax.dot_general(dup_ref[...], wu_ref[...], nt, preferred_element_type=F32))
        if nk == 1:
            o_ref[...] = prod
            return

        @pl.when(k == 0)
        def _():
            acc_ref[...] = prod

        @pl.when((k > 0) & (k < nk - 1))
        def _():
            acc_ref[...] += prod

        @pl.when(k == nk - 1)
        def _():
            o_ref[...] = acc_ref[...] + prod

    return _call(
        body, name, (T // tm, nk),
        [pl.BlockSpec((tm, tk), lambda i, k: (i, k)),
         pl.BlockSpec((tm, tk), lambda i, k: (i, k)),
         pl.BlockSpec((D, tk), lambda i, k: (0, k)),
         pl.BlockSpec((D, tk), lambda i, k: (0, k + nk))],
        [pl.BlockSpec((tm, D), lambda i, k: (i, 0))], [jax.ShapeDtypeStruct((T, D), F32)],
        [dg, dup, w_gu, w_gu], scratch_shapes=[pltpu.VMEM((tm, D), F32)],
        sem=("parallel", "arbitrary"), exch=exch)


def _logsig(x):
    return jnp.minimum(x, 0.0) - jnp.log(1.0 + jnp.exp(-jnp.abs(x)))


def _cum_fwd(fl, flm, bf, name):
    B, S, _ = fl.shape
    nb = S // LANES

    def body(fl_ref, flm_ref, bf_ref, cum_ref, cumm_ref):
        rows = lax.broadcasted_iota(jnp.int32, (LANES, LANES), 0)
        cols = lax.broadcasted_iota(jnp.int32, (LANES, LANES), 1)
        tri = (rows >= cols).astype(F32)
        bias = bf_ref[...]
        lfm = jnp.where(rows < N_META, _logsig(flm_ref[...] + bias), 0.0)
        cm = jnp.dot(tri, lfm, precision=lax.Precision.HIGHEST, preferred_element_type=F32)
        cumm_ref[...] = cm * LOG2E
        base = cm[LANES - 1:LANES, :]
        for b in range(B):
            def blk(i, carry):
                r0 = pl.multiple_of(i * LANES, LANES)
                lf = _logsig(fl_ref[b, pl.ds(r0, LANES), :] + bias)
                c = jnp.dot(tri, lf, precision=lax.Precision.HIGHEST,
                            preferred_element_type=F32) + carry
                cum_ref[b, pl.ds(r0, LANES), :] = c * LOG2E
                return c[LANES - 1:LANES, :]

            lax.fori_loop(0, nb, blk, base)

    return pl.pallas_call(
        body, name=name,
        out_shape=(jax.ShapeDtypeStruct((B, S, LANES), F32),
                   jax.ShapeDtypeStruct((LANES, LANES), F32)),
        compiler_params=_params(),
    )(fl, flm, bf)


def _cum_bwd(dc, dcm, fl, flm, bf, name):
    B, S, _ = fl.shape
    nb = S // LANES

    def body(dc_ref, dcm_ref, fl_ref, flm_ref, bf_ref, dfl_ref, dflm_ref, dbf_ref):
        rows = lax.broadcasted_iota(jnp.int32, (LANES, LANES), 0)
        cols = lax.broadcasted_iota(jnp.int32, (LANES, LANES), 1)
        triu = (rows <= cols).astype(F32)
        bias = bf_ref[...]
        total = jnp.zeros((1, LANES), F32)
        dbf = jnp.zeros((1, LANES), F32)
        for b in range(B):
            tail = jnp.zeros((1, LANES), F32)
            for t in range(nb):
                r0 = (nb - 1 - t) * LANES
                rc = jnp.dot(triu, dc_ref[b, r0:r0 + LANES, :], precision=lax.Precision.HIGHEST,
                             preferred_element_type=F32) + tail
                xv = fl_ref[b, r0:r0 + LANES, :] + bias
                d = rc / (1.0 + jnp.exp(xv))
                dfl_ref[b, r0:r0 + LANES, :] = d
                tail = rc[0:1, :]
                dbf = dbf + jnp.sum(d, axis=0, keepdims=True)
            total = total + tail
        rcm = jnp.dot(triu, dcm_ref[...], precision=lax.Precision.HIGHEST,
                      preferred_element_type=F32) + total
        dm = jnp.where(rows < N_META, rcm / (1.0 + jnp.exp(flm_ref[...] + bias)), 0.0)
        dflm_ref[...] = dm
        dbf_ref[...] = dbf + jnp.sum(dm, axis=0, keepdims=True)

    return pl.pallas_call(
        body, name=name,
        out_shape=(jax.ShapeDtypeStruct((B, S, LANES), F32),
                   jax.ShapeDtypeStruct((LANES, LANES), F32),
                   jax.ShapeDtypeStruct((1, LANES), F32)),
        compiler_params=_params(),
    )(dc, dcm, fl, flm, bf)


_NT = (((1,), (1,)), ((), ()))


def _token_rows_buffer(T, NX, width, dtype=F32):
    return lax.dynamic_update_slice(lax.empty((T, width), dtype), jnp.zeros((T - NX, width), dtype), (NX, 0))


def _attn_specs(S, NX, qw, v_col0):
    mb = NX // META_BLK
    vb = v_col0 // qw
    return (pl.BlockSpec((S, qw), lambda b, p: (b, p)),
            pl.BlockSpec((META_BLK, qw), lambda b, p: (mb, p)),
            pl.BlockSpec((S, qw), lambda b, p: (b, vb + p)),
            pl.BlockSpec((META_BLK, qw), lambda b, p: (mb, vb + p)),
            pl.BlockSpec((S, LANES), lambda b, p: (b, p)))


def _cum_specs(S, TK):
    return [pl.BlockSpec((1, 2, S // TK, 1, TK), lambda b, p: (b, p, 0, 0, 0)),
            pl.BlockSpec((1, 2, 1, META_BLK), lambda b, p: (b, p, 0, 0))]


LOG2E = 1.4426950408889634


def _attn_fwd(qn, kn, vsrc, v_col0, mla, scale, S, NX, name, ck=None, cmk=None, exch=None):
    T = qn.shape[0]
    B = NX // S
    qw = 2 * LANES if mla else LANES
    npair = qn.shape[1] // qw
    TQ = min(512, S)
    TK = TQ
    forget = ck is not None
    a = scale * LOG2E

    def body(*refs):
        if forget:
            q_ref, k_ref, km_ref, v_ref, vm_ref, ck_ref, cmk_ref, _, o_ref, lse_ref = refs
        else:
            q_ref, k_ref, km_ref, v_ref, vm_ref, _, o_ref, lse_ref = refs
        lo = lax.broadcasted_iota(jnp.int32, (1, LANES), 1) < HALF
        mcol = lax.broadcasted_iota(jnp.int32, (TQ, META_BLK), 1)
        causal = (lax.broadcasted_iota(jnp.int32, (TQ, TK), 0)
                  >= lax.broadcasted_iota(jnp.int32, (TQ, TK), 1))
        two = lax.broadcasted_iota(jnp.int32, (TQ, 2), 1)
        for qi in range(S // TQ):
            q0 = qi * TQ
            sls = [slice(e * LANES, (e + 1) * LANES) if mla else slice(None) for e in range(2)]
            if mla:
                qts = [q_ref[q0:q0 + TQ, sl] for sl in sls]
            else:
                qts = [jnp.where(lo if e == 0 else ~lo, q_ref[q0:q0 + TQ, :], 0.0).astype(BF16)
                       for e in range(2)]

            def step(e, kt, vt, c2, mask, carry):
                m, l, acc = carry
                s = lax.dot_general(qts[e], kt, _NT, preferred_element_type=F32) * a
                if forget:
                    s = s - c2
                if mask is not None:
                    s = jnp.where(mask, s, NEG)
                m2 = jnp.max(s, axis=1, keepdims=True)
                if m is not None:
                    m2 = jnp.maximum(m, m2)
                p = jnp.exp2(s - m2)
                l2 = jnp.sum(p, axis=1, keepdims=True)
                acc2 = jnp.dot(p.astype(BF16), vt.astype(BF16), preferred_element_type=F32)
                if m is not None:
                    alpha = jnp.exp2(m - m2)
                    l2, acc2 = alpha * l + l2, alpha * acc + acc2
                return m2, l2, acc2

            def both(rows, kj, mask, carry):
                return tuple(step(e, k_ref[rows, sls[e]], v_ref[rows, sls[e]],
                                  ck_ref[0, e, kj] if forget else None, mask, carry[e]) for e in range(2))

            def below(kj, carry):
                return both(pl.ds(pl.multiple_of(kj * TK, TK), TK), kj, None, carry)

            carry = tuple(step(e, km_ref[:, sls[e]], vm_ref[:, sls[e]], cmk_ref[0, e] if forget else None,
                               mcol < N_META, (None, None, None)) for e in range(2))
            if qi:
                carry = lax.fori_loop(0, qi, below, carry)
            carry = both(slice(q0, q0 + TK), qi, causal, carry)
            outs = [acc / l for _, l, acc in carry]
            lses = [m + jnp.log2(l) for m, l, _ in carry]
            first = pltpu.roll(outs[0], HALF, 1) if mla else outs[0]
            o_ref[q0:q0 + TQ, :] = jnp.where(lo, first, outs[1])
            lse_ref[0, 0, q0:q0 + TQ, :] = jnp.where(two == 0, lses[0], lses[1])

    qk, kmeta, vv, vmeta, pair = _attn_specs(S, NX, qw, v_col0)
    specs = [qk, qk, kmeta, vv, vmeta]
    ins = [qn, kn, kn, vsrc, vsrc]
    if forget:
        specs += _cum_specs(S, TK)
        ins += [ck, cmk]
    specs.append(pl.BlockSpec(memory_space=pl.ANY))
    ins.append(_token_rows_buffer(T, NX, npair * LANES))
    lse_spec = pl.BlockSpec((1, 1, S, 2), lambda b, p: (b, p, 0, 0))
    return _call(
        body, name, (B, npair), specs, [pair, lse_spec],
        [jax.ShapeDtypeStruct((T, npair * LANES), F32), jax.ShapeDtypeStruct((B, npair, S, 2), F32)],
        ins, sem=("parallel", "parallel"), aliases={len(ins) - 1: 0}, exch=exch)


def _attn_bwd(qn, kn, vsrc, v_col0, o, lse, do, mla, scale, S, NX, name, ck=None, cmk=None, exch=None):
    T, W = qn.shape
    B = NX // S
    qw = 2 * LANES if mla else LANES
    npair = W // qw
    TQ = min(512, S)
    TK = TQ
    forget = ck is not None
    a = scale * LOG2E
    _TN = (((0,), (0,)), ((), ()))

    def body(*refs):
        refs = list(refs)
        q_ref, k_ref, km_ref, v_ref, vm_ref, o_ref, do_ref, lse_ref = refs[:8]
        pos = 8
        if forget:
            ck_ref, cmk_ref = refs[8:10]
            pos = 10
        pos += 3
        dq_ref, dk_ref, dv_ref, dkm_ref, dvm_ref = refs[pos:pos + 5]
        if forget:
            dck_ref, dcm_ref, dcq_ref = refs[pos + 5:pos + 8]
            dck_ref[...] = jnp.zeros_like(dck_ref)
            dcm_ref[...] = jnp.zeros_like(dcm_ref)
        dk_ref[...] = jnp.zeros_like(dk_ref)
        dv_ref[...] = jnp.zeros_like(dv_ref)
        dkm_ref[...] = jnp.zeros_like(dkm_ref)
        dvm_ref[...] = jnp.zeros_like(dvm_ref)
        lo = lax.broadcasted_iota(jnp.int32, (1, LANES), 1) < HALF
        mcol = lax.broadcasted_iota(jnp.int32, (TQ, META_BLK), 1)
        causal = (lax.broadcasted_iota(jnp.int32, (TQ, TK), 0)
                  >= lax.broadcasted_iota(jnp.int32, (TQ, TK), 1))
        two = lax.broadcasted_iota(jnp.int32, (TQ, 2), 1)
        for qi in range(S // TQ):
            q0 = qi * TQ
            dof = do_ref[q0:q0 + TQ, :]
            prod = dof * o_ref[q0:q0 + TQ, :]
            lse2 = lse_ref[0, 0, q0:q0 + TQ, :]
            sls = [slice(e * LANES, (e + 1) * LANES) if mla else slice(None) for e in range(2)]
            mine = [lo, ~lo]
            if mla:
                qts = [q_ref[q0:q0 + TQ, sl] for sl in sls]
                dots = [jnp.where(lo, 0.0, pltpu.roll(dof, HALF, 1) if e == 0 else dof).astype(BF16)
                        for e in range(2)]
            else:
                qts = [jnp.where(mine[e], q_ref[q0:q0 + TQ, :], 0.0).astype(BF16) for e in range(2)]
                dots = [jnp.where(mine[e], dof, 0.0).astype(BF16) for e in range(2)]
            deltas = [jnp.sum(jnp.where(mine[e], prod, 0.0), axis=1, keepdims=True) for e in range(2)]
            lse_ts = [jnp.sum(jnp.where(two == e, lse2, 0.0), axis=1, keepdims=True) for e in range(2)]

            def grads(e, kt, vt, c2, mask):
                s = lax.dot_general(qts[e], kt, _NT, preferred_element_type=F32) * a
                if forget:
                    s = s - c2
                p = jnp.exp2(s - lse_ts[e])
                if mask is not None:
                    p = jnp.where(mask, p, 0.0)
                dp = lax.dot_general(dots[e], vt, _NT, preferred_element_type=F32)
                ds = p * (dp - deltas[e])
                dsb = ds.astype(BF16)
                return (jnp.dot(dsb, kt, preferred_element_type=F32),
                        lax.dot_general(dsb, qts[e], _TN, preferred_element_type=F32) * scale,
                        lax.dot_general(p.astype(BF16), dots[e], _TN, preferred_element_type=F32),
                        -jnp.sum(ds, axis=0, keepdims=True) if forget else None,
                        jnp.sum(ds, axis=1, keepdims=True) if forget else None)

            def block(k_at, v_at, dk_at, dv_at, c_at, dc_at, mask, dqs):
                got = [grads(e, k_at(sls[e]), v_at(sls[e]).astype(BF16), c_at(e) if forget else None, mask)
                       for e in range(2)]
                if mla:
                    for e in range(2):
                        dk_at(sls[e], got[e][1])
                        dv_at(sls[e], got[e][2])
                else:
                    dk_at(sls[0], got[0][1] + got[1][1])
                    dv_at(sls[0], got[0][2] + got[1][2])
                if forget:
                    for e in range(2):
                        dc_at(e, got[e][3])
                picks = (0, 0, 4, 4) if forget else (0, 0)
                new = tuple(got[i % 2][k] for i, k in enumerate(picks))
                return new if dqs is None else tuple(x + y for x, y in zip(dqs, new))

            def add_to(ref, *lead):
                def add(*idx_and_val):
                    *idx, val = idx_and_val
                    ref[(*lead, *idx)] += val
                return add

            def token_block(rows, kj, mask, dqs):
                return block(lambda sl: k_ref[rows, sl], lambda sl: v_ref[rows, sl],
                             lambda sl, val: add_to(dk_ref)(rows, sl, val),
                             lambda sl, val: add_to(dv_ref)(rows, sl, val),
                             lambda e: ck_ref[0, e, kj], lambda e, val: add_to(dck_ref, 0)(e, kj, val),
                             mask, dqs)

            dqs = block(lambda sl: km_ref[:, sl], lambda sl: vm_ref[:, sl],
                        lambda sl, val: add_to(dkm_ref, 0)(slice(None), sl, val),
                        lambda sl, val: add_to(dvm_ref, 0)(slice(None), sl, val),
                        lambda e: cmk_ref[0, e], lambda e, val: add_to(dcm_ref, 0)(e, val),
                        mcol < N_META, None)

            def below(kj, dqs):
                return token_block(pl.ds(pl.multiple_of(kj * TK, TK), TK), kj, None, dqs)

            if qi:
                dqs = lax.fori_loop(0, qi, below, dqs)
            dqs = token_block(slice(q0, q0 + TK), qi, causal, dqs)
            if forget:
                dcq_ref[0, 0, q0:q0 + TQ, :] = jnp.where(two == 0, dqs[2], dqs[3])
            if mla:
                for e in range(2):
                    dq_ref[q0:q0 + TQ, sls[e]] = dqs[e] * scale
            else:
                dq_ref[q0:q0 + TQ, :] = jnp.where(lo, dqs[0], dqs[1]) * scale

    qk, kmeta, vv, vmeta, pair = _attn_specs(S, NX, qw, v_col0)
    lse_spec = pl.BlockSpec((1, 1, S, 2), lambda b, p: (b, p, 0, 0))
    specs = [qk, qk, kmeta, vv, vmeta, pair, pair, lse_spec]
    ins = [qn, kn, kn, vsrc, vsrc, o, do, lse]
    if forget:
        specs += _cum_specs(S, TK)
        ins += [ck, cmk]
    first_alias = len(ins)
    specs += [pl.BlockSpec(memory_space=pl.ANY)] * 3
    ins += [_token_rows_buffer(T, NX, W) for _ in range(3)]
    mspec = pl.BlockSpec((1, META_BLK, qw), lambda b, p: (b, 0, p))
    out_specs = [qk, qk, qk, mspec, mspec]
    tok = jax.ShapeDtypeStruct((T, W), F32)
    met = jax.ShapeDtypeStruct((B, META_BLK, W), F32)
    out_shape = [tok, tok, tok, met, met]
    if forget:
        out_specs += _cum_specs(S, TK) + [lse_spec]
        out_shape += [jax.ShapeDtypeStruct((B, HEADS, S // TK, 1, TK), F32),
                      jax.ShapeDtypeStruct((B, HEADS, 1, META_BLK), F32),
                      jax.ShapeDtypeStruct((B, npair, S, 2), F32)]
    return _call(
        body, name, (B, npair), specs, out_specs, out_shape, ins, sem=("parallel", "parallel"),
        aliases={first_alias: 0, first_alias + 1: 1, first_alias + 2: 2}, exch=exch)


def _gate_fwd(z, bg, of, om, name):
    T, D = of.shape
    tm = _tile(T, 640, 16)

    def body(z_ref, bg_ref, of_ref, om_ref, o_ref):
        bias = bg_ref[...]

        def step(rows, carry):
            gt = jax.nn.sigmoid(z_ref[rows, :] + bias)
            o_ref[rows, :] = (gt[:, :D] * of_ref[rows, :] + gt[:, D:] * om_ref[rows, :]).astype(BF16)
            return carry

        _by_chunks(tm, D, step)

    row = pl.BlockSpec((tm, D), lambda i: (i, 0))
    return pl.pallas_call(
        body, name=name, grid=(T // tm,),
        in_specs=[pl.BlockSpec((tm, 2 * D), lambda i: (i, 0)),
                  pl.BlockSpec((1, 2 * D), lambda i: (0, 0)), row, row],
        out_specs=row, out_shape=jax.ShapeDtypeStruct((T, D), BF16),
        compiler_params=_params(("parallel",)),
    )(z, bg, of, om)


def _gate_bwd(dmix, z, bg, of, om, name):
    T, D = of.shape
    tm = _tile(T, 640, 16)

    def body(dm_ref, z_ref, bg_ref, of_ref, om_ref, dgl_ref, dof_ref, dom_ref, dbg_ref):
        bias = bg_ref[...]

        def step(rows, dbg_sum):
            gt = jax.nn.sigmoid(z_ref[rows, :] + bias)
            dm = dm_ref[rows, :]
            dof_ref[rows, :] = (dm * gt[:, :D]).astype(BF16)
            dom_ref[rows, :] = (dm * gt[:, D:]).astype(BF16)
            dgl = jnp.concatenate([dm * of_ref[rows, :], dm * om_ref[rows, :]], axis=1) * gt * (1.0 - gt)
            dgl_ref[rows, :] = dgl.astype(BF16)
            return dbg_sum + jnp.sum(dgl, axis=0, keepdims=True)

        dbg = _by_chunks(tm, D, step, jnp.zeros((1, 2 * D), F32))

        @pl.when(pl.program_id(0) == 0)
        def _():
            dbg_ref[...] = jnp.zeros_like(dbg_ref)

        dbg_ref[...] += dbg

    row = pl.BlockSpec((tm, D), lambda i: (i, 0))
    wide = pl.BlockSpec((tm, 2 * D), lambda i: (i, 0))
    one = pl.BlockSpec((1, 2 * D), lambda i: (0, 0))
    return pl.pallas_call(
        body, name=name, grid=(T // tm,),
        in_specs=[row, wide, one, row, row], out_specs=(wide, row, row, one),
        out_shape=(jax.ShapeDtypeStruct((T, 2 * D), BF16), jax.ShapeDtypeStruct((T, D), BF16),
                   jax.ShapeDtypeStruct((T, D), BF16), jax.ShapeDtypeStruct((1, 2 * D), F32)),
        compiler_params=_params(("arbitrary",)),
    )(dmix, z, bg, of, om)


def _loss(h, tgt, name):
    T, D = h.shape
    NX = tgt.shape[0]
    tm = _tile(NX, 640, 16)

    def body(h_ref, t_ref, _, __, dh_ref, dhb_ref, l_ref):
        i = pl.program_id(0)

        def step(rows, part):
            err = h_ref[rows, :] - t_ref[rows, :]
            dh = err * (1.0 / D)
            dh_ref[rows, :] = dh
            dhb_ref[rows, :] = dh.astype(BF16)
            return part + jnp.sum(err * err, axis=0, keepdims=True)

        part = _by_chunks(tm, D, step, jnp.zeros((1, D), F32))

        @pl.when(i == 0)
        def _():
            l_ref[...] = jnp.zeros_like(l_ref)

        l_ref[...] += 0.5 * jnp.sum(part) * (1.0 / D)

    row = pl.BlockSpec((tm, D), lambda i: (i, 0))
    acc = pl.BlockSpec((8, LANES), lambda i: (0, 0))
    hbm = pl.BlockSpec(memory_space=pl.ANY)
    return pl.pallas_call(
        body, name=name, grid=(NX // tm,), in_specs=[row, row, hbm, hbm], out_specs=(row, row, acc),
        out_shape=(jax.ShapeDtypeStruct((T, D), F32), jax.ShapeDtypeStruct((T, D), BF16),
                   jax.ShapeDtypeStruct((8, LANES), F32)),
        input_output_aliases={2: 0, 3: 1},
        compiler_params=_params(("arbitrary",)),
    )(h, tgt, _token_rows_buffer(T, NX, D), _token_rows_buffer(T, NX, D, BF16))


def _adamw(parts, w, m, v, name):
    P, R, C = parts.shape
    tr = _tile(R, max(8, (1 << 18) // C), 8)
    bc1 = 1.0 - ADAM_B1 ** ADAM_STEP
    bc2 = 1.0 - ADAM_B2 ** ADAM_STEP

    def body(p_ref, w_ref, m_ref, v_ref, g_ref, d_ref, m2_ref, v2_ref):
        def step(rows, carry):
            g = p_ref[0, rows, :].astype(F32)
            for j in range(1, P):
                g = g + p_ref[j, rows, :].astype(F32)
            m2 = ADAM_B1 * m_ref[rows, :] + (1.0 - ADAM_B1) * g
            v2 = ADAM_B2 * v_ref[rows, :] + (1.0 - ADAM_B2) * (g * g)
            m_hat = m2 / bc1
            v_hat = v2 / bc2
            g_ref[rows, :] = g
            d_ref[rows, :] = -ADAM_LR * (m_hat / (jnp.sqrt(v_hat) + ADAM_EPS) + ADAM_WD * w_ref[rows, :])
            m2_ref[rows, :] = m2
            v2_ref[rows, :] = v2
            return carry

        _by_chunks(tr, C, step)

    row = pl.BlockSpec((tr, C), lambda i: (i, 0))
    sh = jax.ShapeDtypeStruct((R, C), F32)
    return pl.pallas_call(
        body, name=name, grid=(R // tr,),
        in_specs=[pl.BlockSpec((P, tr, C), lambda i: (0, i, 0)), row, row, row],
        out_specs=(row, row, row, row), out_shape=(sh, sh, sh, sh),
        compiler_params=_params(("parallel",)),
    )(parts, w, m, v)


def _peer(d):
    x, y, c = lax.axis_index("x"), lax.axis_index("y"), lax.axis_index("c")
    px = 1 - x if d & 4 else x
    py = 1 - y if d & 2 else y
    pc = 1 - c if d & 1 else c
    return (px, py, pc), 4 * px + 2 * py + pc


class _Exchange:
    def __init__(self, srcs, gather):
        self.srcs, self.gather, self.n = list(srcs), gather, len(srcs)
        n = self.n
        hbm = pl.BlockSpec(memory_space=pl.ANY)
        self.in_specs = [hbm] * n
        self.out_specs = [hbm] * n
        self.out_shape = [jax.ShapeDtypeStruct((N_DEV,) + s.shape[-2:], s.dtype) for s in srcs]
        self.scratch = [pltpu.SemaphoreType.DMA((N_DEV - 1, n)), pltpu.SemaphoreType.DMA((N_DEV - 1, n)),
                        pltpu.SemaphoreType.DMA((n,))]

    def _copies(self, src_refs, out_refs, sems):
        send_sems, recv_sems, local_sems = sems
        _, me = _peer(0)

        def remote(w, d, landing):
            dev, lin = _peer(d)
            return pltpu.make_async_remote_copy(
                src_ref=src_refs[w] if self.gather else src_refs[w].at[lin],
                dst_ref=out_refs[w].at[lin if landing else me],
                send_sem=send_sems.at[d - 1, w], recv_sem=recv_sems.at[d - 1, w],
                device_id=dev, device_id_type=pl.DeviceIdType.MESH)

        pairs = [(w, d) for d in range(1, N_DEV) for w in range(self.n)]
        own = [pltpu.make_async_copy(src_refs[w] if self.gather else src_refs[w].at[me],
                                     out_refs[w].at[me], local_sems.at[w]) for w in range(self.n)]
        return own, [remote(w, d, False) for w, d in pairs], [remote(w, d, True) for w, d in pairs]

    def _gather_copies(self, src_refs, out_refs, sems):
        send_sems, recv_sems, local_sems = sems
        x, y, c = lax.axis_index("x"), lax.axis_index("y"), lax.axis_index("c")
        me, sibling = (x, y, c), (x, y, 1 - c)
        chips = [(1 - x, y), (x, 1 - y), (1 - x, 1 - y)]

        def copy(w, k, block, to, src=None):
            rows = out_refs[w].at[4 * block[0] + 2 * block[1] + block[2]]
            return pltpu.make_async_remote_copy(
                src_ref=rows if src is None else src, dst_ref=rows,
                send_sem=send_sems.at[k, w], recv_sem=recv_sems.at[k, w],
                device_id=to, device_id_type=pl.DeviceIdType.MESH)

        ws = range(self.n)
        own = [pltpu.make_async_copy(src_refs[w], out_refs[w].at[4 * x + 2 * y + c], local_sems.at[w])
               for w in ws]
        first = [copy(w, 0, me, sibling, src_refs[w]) for w in ws]
        first += [copy(w, 1 + j, me, (*chip, c), src_refs[w]) for j, chip in enumerate(chips) for w in ws]
        landed = [[copy(w, 1 + j, (*chip, c), me) for w in ws] for j, chip in enumerate(chips)]
        passed = [[copy(w, 4 + j, (*chip, c), sibling) for w in ws] for j, chip in enumerate(chips)]
        from_sibling = [copy(w, 0, sibling, me) for w in ws]
        from_sibling += [copy(w, 4 + j, (*chip, 1 - c), me) for j, chip in enumerate(chips) for w in ws]
        return own, first, landed, passed, from_sibling

    def start(self, src_refs, out_refs, sems):
        if self.gather:
            own, first = self._gather_copies(src_refs, out_refs, sems)[:2]
            sent = first
        else:
            own, sent, _ = self._copies(src_refs, out_refs, sems)
        for cp in own + sent:
            cp.start()

    def wait(self, src_refs, out_refs, sems):
        if self.gather:
            own, first, landed, passed, from_sibling = self._gather_copies(src_refs, out_refs, sems)
            for arrived, onward in zip(landed, passed):
                for cp in arrived:
                    cp.wait_recv()
                for cp in onward:
                    cp.start()
            for cp in from_sibling:
                cp.wait_recv()
            for cp in first + [cp for group in passed for cp in group]:
                cp.wait_send()
        else:
            own, sent, landing = self._copies(src_refs, out_refs, sems)
            for cp in landing:
                cp.wait_recv()
            for cp in sent:
                cp.wait_send()
        for cp in own:
            cp.wait()


def _exchange(srcs, name, gather):
    ex = _Exchange(srcs, gather)
    n = ex.n

    def body(*refs):
        ex.start(refs[:n], refs[n:2 * n], refs[2 * n:])
        ex.wait(refs[:n], refs[n:2 * n], refs[2 * n:])

    outs = pl.pallas_call(
        body, name=name, in_specs=ex.in_specs, out_specs=tuple(ex.out_specs),
        out_shape=tuple(ex.out_shape), scratch_shapes=ex.scratch,
    )(*srcs)
    return list(outs)


def _call(body, name, grid, in_specs, out_specs, out_shape, ins, scratch_shapes=(), sem=None,
          aliases=None, exch=None):
    aliases = aliases or {}
    if exch is None:
        outs = pl.pallas_call(
            body, name=name, grid=grid, in_specs=list(in_specs), out_specs=tuple(out_specs),
            out_shape=tuple(out_shape), scratch_shapes=list(scratch_shapes),
            input_output_aliases=aliases, compiler_params=_params(sem),
        )(*ins)
        return list(outs), []
    ni, no, ns, n = len(in_specs), len(out_specs), len(scratch_shapes), exch.n
    last_ids = [g - 1 for g in grid]

    def hosted(*refs):
        cin, xin = refs[:ni], refs[ni:ni + n]
        cout, xout = refs[ni + n:ni + n + no], refs[ni + n + no:ni + 2 * n + no]
        cscr, xsem = refs[ni + 2 * n + no:ni + 2 * n + no + ns], refs[ni + 2 * n + no + ns:]
        ids = [pl.program_id(a) for a in range(len(grid))]
        first, last = ids[0] == 0, ids[0] == last_ids[0]
        for a in range(1, len(grid)):
            first, last = first & (ids[a] == 0), last & (ids[a] == last_ids[a])

        @pl.when(first)
        def _():
            exch.start(xin, xout, xsem)

        body(*cin, *cout, *cscr)

        @pl.when(last)
        def _():
            exch.wait(xin, xout, xsem)

    outs = pl.pallas_call(
        hosted, name=name, grid=grid, in_specs=list(in_specs) + exch.in_specs,
        out_specs=tuple(list(out_specs) + exch.out_specs),
        out_shape=tuple(list(out_shape) + exch.out_shape),
        scratch_shapes=list(scratch_shapes) + exch.scratch, input_output_aliases=aliases,
        compiler_params=_params(("arbitrary",) * len(grid)),
    )(*ins, *exch.srcs)
    return list(outs[:no]), list(outs[no:])


def _pack(arrs, cols, row_mult):
    flat = jnp.concatenate([a.reshape(-1) for a in arrs])
    n = flat.shape[0]
    quantum = cols * row_mult
    total = -(-n // quantum) * quantum
    return jnp.pad(flat, (0, total - n)).reshape(total // cols, cols)


def _pack_rows(arrs, cols, row_mult):
    flat = jnp.concatenate(arrs, axis=1)
    n = flat.shape[1]
    quantum = cols * row_mult
    total = -(-n // quantum) * quantum
    return jnp.pad(flat, ((0, 0), (0, total - n))).reshape(N_DEV, total // cols, cols)


def _unpack(packed, shapes):
    flat = packed.reshape(-1)
    out, off = [], 0
    for s in shapes:
        n = int(np.prod(s))
        out.append(flat[off:off + n].reshape(s))
        off += n
    return out


def _rope_tables(positions):
    inv_freq = ROPE_THETA ** (-jnp.arange(0, MLA_ROPE, 2, dtype=F32) / MLA_ROPE)
    ang = positions.astype(F32)[:, None] * inv_freq[None, :]
    cos, sin = jnp.cos(ang), jnp.sin(ang)
    n = positions.shape[0]
    ones, zeros = jnp.ones((n, MLA_NOPE), F32), jnp.zeros((n, MLA_NOPE), F32)
    tail1, tail0 = jnp.ones((n, LANES - MLA_QK), F32), jnp.zeros((n, LANES - MLA_QK), F32)
    z16 = jnp.zeros((n, 16), F32)
    c = jnp.concatenate([ones, cos, cos, tail1], axis=1)
    s1 = jnp.concatenate([zeros, -sin, z16, tail0], axis=1)
    s2 = jnp.concatenate([zeros, z16, sin, tail0], axis=1)
    return c, s1, s2


def kernel(x, meta_tokens, ffn1_norm, ffn1_w_gu, ffn1_w_down, mix_norm, w_in, b_forget, b_gate, fox_q_norm, fox_k_norm, mla_cq_norm, mla_w_uq, mla_ckv_norm, mla_w_ukv, mla_q_norm, mla_k_norm, w_branch_fox, w_branch_mla, w_out, ffn2_norm, ffn2_w_gu, ffn2_w_down, loss_target, m_meta_tokens, m_ffn1_norm, m_ffn1_w_gu, m_ffn1_w_down, m_mix_norm, m_w_in, m_b_forget, m_b_gate, m_fox_q_norm, m_fox_k_norm, m_mla_cq_norm, m_mla_w_uq, m_mla_ckv_norm, m_mla_w_ukv, m_mla_q_norm, m_mla_k_norm, m_w_branch_fox, m_w_branch_mla, m_w_out, m_ffn2_norm, m_ffn2_w_gu, m_ffn2_w_down, v_meta_tokens, v_ffn1_norm, v_ffn1_w_gu, v_ffn1_w_down, v_mix_norm, v_w_in, v_b_forget, v_b_gate, v_fox_q_norm, v_fox_k_norm, v_mla_cq_norm, v_mla_w_uq, v_mla_ckv_norm, v_mla_w_ukv, v_mla_q_norm, v_mla_k_norm, v_w_branch_fox, v_w_branch_mla, v_w_out, v_ffn2_norm, v_ffn2_w_gu, v_ffn2_w_down):
    names = ["meta_tokens", "ffn1_norm", "ffn1_w_gu", "ffn1_w_down", "mix_norm", "w_in", "b_forget",
             "b_gate", "fox_q_norm", "fox_k_norm", "mla_cq_norm", "mla_w_uq", "mla_ckv_norm",
             "mla_w_ukv", "mla_q_norm", "mla_k_norm", "w_branch_fox", "w_branch_mla", "w_out",
             "ffn2_norm", "ffn2_w_gu", "ffn2_w_down"]
    W = dict(zip(names, [meta_tokens, ffn1_norm, ffn1_w_gu, ffn1_w_down, mix_norm, w_in, b_forget,
                         b_gate, fox_q_norm, fox_k_norm, mla_cq_norm, mla_w_uq, mla_ckv_norm,
                         mla_w_ukv, mla_q_norm, mla_k_norm, w_branch_fox, w_branch_mla, w_out,
                         ffn2_norm, ffn2_w_gu, ffn2_w_down]))
    Mo = dict(zip(names, [m_meta_tokens, m_ffn1_norm, m_ffn1_w_gu, m_ffn1_w_down, m_mix_norm, m_w_in,
                          m_b_forget, m_b_gate, m_fox_q_norm, m_fox_k_norm, m_mla_cq_norm,
                          m_mla_w_uq, m_mla_ckv_norm, m_mla_w_ukv, m_mla_q_norm, m_mla_k_norm,
                          m_w_branch_fox, m_w_branch_mla, m_w_out, m_ffn2_norm, m_ffn2_w_gu,
                          m_ffn2_w_down]))
    Vo = dict(zip(names, [v_meta_tokens, v_ffn1_norm, v_ffn1_w_gu, v_ffn1_w_down, v_mix_norm, v_w_in,
                          v_b_forget, v_b_gate, v_fox_q_norm, v_fox_k_norm, v_mla_cq_norm,
                          v_mla_w_uq, v_mla_ckv_norm, v_mla_w_ukv, v_mla_q_norm, v_mla_k_norm,
                          v_w_branch_fox, v_w_branch_mla, v_w_out, v_ffn2_norm, v_ffn2_w_gu,
                          v_ffn2_w_down]))

    B, S, D = x.shape
    NX = B * S
    T = NX + META_BLK
    H = HEADS
    assert NX % META_BLK == 0 and S % LANES == 0
    me = 4 * lax.axis_index("x") + 2 * lax.axis_index("y") + lax.axis_index("c")

    big = [("ffn1_w_gu", 1), ("ffn1_w_down", 0), ("w_in", 1), ("mla_w_uq", 1), ("mla_w_ukv", 1),
           ("w_branch_fox", 1), ("w_branch_mla", 1), ("w_out", 0), ("ffn2_w_gu", 1), ("ffn2_w_down", 0)]
    mix_small = ["mla_w_uq", "mla_w_ukv", "w_branch_fox", "w_branch_mla", "w_out"]
    last_group = ["ffn2_w_gu", "ffn2_w_down"]
    axis_of = dict(big)
    full = {}

    def shards(group):
        return [W[n][0].astype(BF16) for n in group]

    def assemble(group, blks):
        for n, blk in zip(group, blks):
            _, r, c = blk.shape
            full[n] = (blk.transpose(1, 0, 2).reshape(r, N_DEV * c) if axis_of[n] == 1
                       else blk.reshape(N_DEV * r, c))

    got = _exchange(shards(["ffn1_w_gu"]) + [meta_tokens], "gather_first", gather=True)
    assemble(["ffn1_w_gu"], got[:1])
    meta_full = got[1].transpose(1, 0, 2).reshape(N_META, D)

    Z_G, Z_FQ = 0, 2 * D
    Z_FK, Z_FV = Z_FQ + FOX_W, Z_FQ + 2 * FOX_W
    Z_CQ = Z_FQ + 3 * FOX_W
    Z_CKV = Z_CQ + MLA_Q_RANK
    Z_F = Z_CKV + MLA_KV_RANK
    Z_KR = Z_F + LANES

    def pad_lanes(a, w=LANES):
        return jnp.pad(a, [(0, 0)] * (a.ndim - 1) + [(0, w - a.shape[-1])])

    def rows_T(real, meta=None):
        n = real.shape[1]
        parts = [real]
        used = 0
        if meta is not None:
            parts.append(meta)
            used = meta.shape[0]
        if T - NX - used:
            parts.append(jnp.zeros((T - NX - used, n), real.dtype))
        return jnp.concatenate(parts, axis=0)

    def put_meta(tok, meta_per_seq):
        return lax.dynamic_update_slice(tok, meta_per_seq.sum(0), (NX, 0))

    h0 = rows_T(x.reshape(NX, D), meta_full)
    tgt = loss_target.reshape(NX, D)

    def ffn_fwd(h, norm, w_gu, tag, behind_up=None, behind_down=None):
        u = _norm_fwd(h, 0, D, D, norm, D, D, tag + "_norm")
        (sa, sb, a), got = _ffn_up(u, w_gu, tag + "_up",
                                  exch=_Exchange(shards(behind_up), True) if behind_up else None)
        assemble(behind_up or [], got)
        h_out = _mm(a, full[tag + "_w_down"], "nn", tag + "_down", scale=0.5, res=h,
                    exch=_Exchange(shards(behind_down), True) if behind_down else None)
        if behind_down:
            h_out, got = h_out
            assemble(behind_down, got)
        return h_out, (u, sa, sb, a)

    h1, ffn1_saved = ffn_fwd(h0, W["ffn1_norm"], full["ffn1_w_gu"], "ffn1",
                             behind_up=["ffn1_w_down"], behind_down=["w_in"])
    wi = full["w_in"]
    o_fq = 0
    o_f = 3 * FOX_W
    o_cq = o_f + HEADS
    o_kr = o_cq + MLA_Q_RANK + MLA_KV_RANK
    o_g = o_kr + MLA_ROPE
    w_in_p = jnp.concatenate([
        wi[:, o_g:o_g + 2 * D], wi[:, o_fq:o_f], wi[:, o_cq:o_kr],
        jnp.pad(wi[:, o_f:o_cq], ((0, 0), (0, LANES - HEADS))),
        jnp.pad(wi[:, o_kr:o_g], ((0, 0), (0, LANES - MLA_ROPE)))], axis=1)

    u2 = _norm_fwd(h1, 0, D, D, W["mix_norm"], D, D, "mix_norm")
    z, got = _mm(u2, w_in_p, "nn", "w_in", exch=_Exchange(shards(mix_small), True))
    assemble(mix_small, got)
    w_uq_p = jnp.pad(full["mla_w_uq"].reshape(MLA_Q_RANK, H, MLA_QK),
                     ((0, 0), (0, 0), (0, LANES - MLA_QK))).reshape(MLA_Q_RANK, H * LANES)

    gq_f = jnp.tile(W["fox_q_norm"], (1, 2))
    gk_f = jnp.tile(W["fox_k_norm"], (1, 2))
    fqn = _norm_fwd(z, Z_FQ, FOX_W, LANES, gq_f, FOX_HD, FOX_HD, "fox_q_norm")
    fkn = _norm_fwd(z, Z_FK, FOX_W, LANES, gk_f, FOX_HD, FOX_HD, "fox_k_norm")
    fl = z[:NX, Z_F:Z_F + LANES].reshape(B, S, LANES)
    flm = z[NX:, Z_F:Z_F + LANES]
    bf = pad_lanes(W["b_forget"])
    cum, cumm = _cum_fwd(fl, flm, bf, "forget_cum")
    TK = min(512, S)
    ck = cum[:, :, :H].transpose(0, 2, 1).reshape(B, H, S // TK, 1, TK)
    cmk = jnp.broadcast_to(cumm[:, :H].T[None, :, None, :], (B, H, 1, META_BLK))
    (o_fox, lse_fox), got = _attn_fwd(fqn, fkn, z, Z_FV, False, FOX_HD ** -0.5, S, NX, "fox_attn", ck, cmk,
                                      exch=_Exchange(shards(last_group), True))
    assemble(last_group, got)
    of = _mm(o_fox, full["w_branch_fox"], "nn", "branch_fox")

    pos = jnp.concatenate([jnp.tile(jnp.arange(S) + N_META, B), jnp.arange(META_BLK)])
    tabs = _rope_tables(pos)
    cqn = _norm_fwd(z, Z_CQ, MLA_Q_RANK, MLA_Q_RANK, W["mla_cq_norm"], MLA_Q_RANK, MLA_Q_RANK, "mla_cq_norm")
    q_lin = _mm(cqn, w_uq_p, "nn", "mla_uq")
    ckvn = _norm_fwd(z, Z_CKV, MLA_KV_RANK, MLA_KV_RANK, W["mla_ckv_norm"], MLA_KV_RANK, MLA_KV_RANK,
                     "mla_ckv_norm")
    kv_lin = _mm(ckvn, full["mla_w_ukv"], "nn", "mla_ukv")
    gq_m, gk_m = pad_lanes(W["mla_q_norm"]), pad_lanes(W["mla_k_norm"])
    mqn = _norm_fwd(q_lin, 0, H * LANES, LANES, gq_m, LANES, MLA_QK, "mla_q_norm", tabs=tabs)
    mkn = _mla_k_fwd(kv_lin, z, Z_KR, gk_m, tabs, "mla_k_norm")
    (o_mla, lse_mla), _ = _attn_fwd(mqn, mkn, kv_lin, 0, True, MLA_QK ** -0.5, S, NX, "mla_attn")
    om = _mm(o_mla, full["w_branch_mla"], "nn", "branch_mla")

    mix = _gate_fwd(z, W["b_gate"], of, om, "gate_mix")
    h2 = _mm(mix, full["w_out"], "nn", "w_out", res=h1)

    h3, ffn2_saved = ffn_fwd(h2, W["ffn2_norm"], full["ffn2_w_gu"], "ffn2")

    dh3, dh3_b, loss_acc = _loss(h3, tgt, "loss")
    loss = lax.psum(loss_acc[0, 0], AXES)

    G = {}
    parts = {}

    def scatter_of(group):
        per_dest = []
        for n in group:
            r, c = W[n].shape[1:]
            per_dest.append((G[n].reshape(r, N_DEV, c).transpose(1, 0, 2) if axis_of[n] == 1
                             else G[n].reshape(N_DEV, r, c)).astype(BF16))
        return _Exchange(per_dest, False)

    def ffn_bwd(dh, dh_b, h, norm, w_gu, w_down, saved, tag, behind_down=None, spread=False):
        u, sa, sb, a = saved
        G[tag + "_w_down"] = _mm(a, dh_b, "tn", tag + "_dw_down", scale=0.5)
        (dg, dup), got = _ffn_down_bwd(dh_b, w_down, sa, sb, tag + "_down_bwd",
                                       exch=scatter_of(behind_down) if behind_down else None)
        parts.update(zip(behind_down or [], got))
        dw_g = _mm(u, dg, "tn", tag + "_dw_g", exch=scatter_of([tag + "_w_down"]) if spread else None)
        if spread:
            dw_g, got = dw_g
            parts[tag + "_w_down"] = got[0]
        G[tag + "_w_gu"] = jnp.concatenate([dw_g, _mm(u, dup, "tn", tag + "_dw_u")], axis=1)
        (du,), got = _ffn_up_bwd_dx(dg, dup, w_gu, tag + "_up_bwd",
                                    exch=scatter_of([tag + "_w_gu"]) if spread else None)
        if spread:
            parts[tag + "_w_gu"] = got[0]
        *dh_in, G[tag + "_norm"] = _norm_bwd(h, 0, D, D, norm, D, D, du, tag + "_norm_bwd", res=dh,
                                             bf16_copy=not spread)
        return dh_in

    dh2, dh2_b = ffn_bwd(dh3, dh3_b, h2, W["ffn2_norm"], full["ffn2_w_gu"], full["ffn2_w_down"], ffn2_saved,
                         "ffn2")

    G["w_out"] = _mm(mix, dh2_b, "tn", "dw_out")
    dmix = _mm(dh2_b, full["w_out"], "nt", "w_out_bwd")
    dgl, dof, dom, G["b_gate"] = _gate_bwd(dmix, z, W["b_gate"], of, om, "gate_bwd")

    G["w_branch_fox"] = _mm(o_fox, dof, "tn", "dw_branch_fox")
    do_fox = _mm(dof, full["w_branch_fox"], "nt", "branch_fox_bwd")
    (dq_f, dk_f, dv_f, dkm_f, dvm_f, dck, dcmk, dcq), got = _attn_bwd(
        fqn, fkn, z, Z_FV, o_fox, lse_fox, do_fox, False, FOX_HD ** -0.5, S, NX, "fox_attn_bwd", ck, cmk,
        exch=scatter_of(last_group))
    parts.update(zip(last_group, got))
    dk_f, dv_f = put_meta(dk_f, dkm_f), put_meta(dv_f, dvm_f)
    dfq, gq = _norm_bwd(z, Z_FQ, FOX_W, LANES, gq_f, FOX_HD, FOX_HD, dq_f, "fox_q_norm_bwd", out_dtype=BF16)
    dfk, gk = _norm_bwd(z, Z_FK, FOX_W, LANES, gk_f, FOX_HD, FOX_HD, dk_f, "fox_k_norm_bwd", out_dtype=BF16)
    G["fox_q_norm"] = gq[:, :FOX_HD] + gq[:, FOX_HD:]
    G["fox_k_norm"] = gk[:, :FOX_HD] + gk[:, FOX_HD:]
    dc = pad_lanes(dck.reshape(B, H, S).transpose(0, 2, 1)
                   + dcq.transpose(0, 2, 1, 3).reshape(B, S, H))
    dcm = pad_lanes(dcmk.sum(0)[:, 0, :].T)
    dcm = jnp.where(jnp.arange(LANES)[:, None] < N_META, dcm, 0.0)
    dfl, dflm, dbf = _cum_bwd(dc, dcm, fl, flm, bf, "forget_cum_bwd")
    G["b_forget"] = dbf[:, :HEADS]
    dfl_t = rows_T(dfl.reshape(NX, LANES), dflm)

    G["w_branch_mla"] = _mm(o_mla, dom, "tn", "dw_branch_mla")
    do_mla = _mm(dom, full["w_branch_mla"], "nt", "branch_mla_bwd")
    (dq_m, dk_m, dvk, dkm_m, dvkm), _ = _attn_bwd(
        mqn, mkn, kv_lin, 0, o_mla, lse_mla, do_mla, True, MLA_QK ** -0.5, S, NX, "mla_attn_bwd")
    dk_m, dvk = put_meta(dk_m, dkm_m), put_meta(dvk, dvkm)
    dq_lin, gq = _norm_bwd(q_lin, 0, H * LANES, LANES, gq_m, LANES, MLA_QK, dq_m, "mla_q_norm_bwd", tabs=tabs,
                           out_dtype=BF16)
    G["mla_q_norm"] = gq[:, :MLA_QK]
    G["mla_w_uq"] = _mm(cqn, dq_lin, "tn", "dw_uq").reshape(MLA_Q_RANK, H, LANES)[:, :, :MLA_QK].reshape(
        MLA_Q_RANK, H * MLA_QK)
    dcqn = _mm(dq_lin, w_uq_p, "nt", "mla_uq_bwd")
    dcq, G["mla_cq_norm"] = _norm_bwd(z, Z_CQ, MLA_Q_RANK, MLA_Q_RANK, W["mla_cq_norm"], MLA_Q_RANK,
                                      MLA_Q_RANK, dcqn, "mla_cq_norm_bwd", out_dtype=BF16)
    dkv_lin, dkr, gk = _mla_k_bwd(kv_lin, z, Z_KR, gk_m, tabs, dk_m, dvk, "mla_k_norm_bwd")
    G["mla_k_norm"] = gk[:, :MLA_QK]
    G["mla_w_ukv"] = _mm(ckvn, dkv_lin, "tn", "dw_ukv")
    dckvn = _mm(dkv_lin, full["mla_w_ukv"], "nt", "mla_ukv_bwd")
    dckv, G["mla_ckv_norm"] = _norm_bwd(z, Z_CKV, MLA_KV_RANK, MLA_KV_RANK, W["mla_ckv_norm"], MLA_KV_RANK,
                                        MLA_KV_RANK, dckvn, "mla_ckv_norm_bwd", out_dtype=BF16)

    dz = jnp.concatenate([dgl, dfq, dfk, dv_f.astype(BF16), dcq, dckv, dfl_t.astype(BF16),
                          dkr.astype(BF16)], axis=1)
    dw_in_p = _mm(u2, dz, "tn", "dw_in")
    G["w_in"] = jnp.concatenate([
        dw_in_p[:, Z_FQ:Z_CQ], dw_in_p[:, Z_F:Z_F + HEADS], dw_in_p[:, Z_CQ:Z_F],
        dw_in_p[:, Z_KR:Z_KR + MLA_ROPE], dw_in_p[:, Z_G:Z_G + 2 * D]], axis=1)
    du2, got = _mm(dz, w_in_p, "nt", "w_in_bwd", exch=scatter_of(mix_small))
    parts.update(zip(mix_small, got))
    dh1, dh1_b, G["mix_norm"] = _norm_bwd(h1, 0, D, D, W["mix_norm"], D, D, du2, "mix_norm_bwd", res=dh2,
                                          bf16_copy=True)

    (dh0,) = ffn_bwd(dh1, dh1_b, h0, W["ffn1_norm"], full["ffn1_w_gu"], full["ffn1_w_down"], ffn1_saved, "ffn1",
                     behind_down=["w_in"], spread=True)
    grad_x = dh0[:NX].reshape(B, S, D)
    G["meta_tokens"] = dh0[NX:NX + N_META]

    res = {}
    for n, _ in big:
        outs4 = _adamw(parts[n], W[n][0], Mo[n][0], Vo[n][0], "adamw_" + n)
        for key, arr in zip(("g", "d", "m", "v"), outs4):
            res[key, n] = arr[None]

    small = [n for n in names if n not in dict(big) and n != "meta_tokens"]
    small_shapes = [W[n].shape for n in small]
    spack = _pack([G["meta_tokens"]] + [G[n] for n in small], 1024, 8)
    (sparts,) = _exchange([spack], "gather_small_grads", gather=True)
    sflat = sparts.reshape(N_DEV, -1)
    dsh = D // N_DEV
    meta_part = lax.dynamic_slice(sflat[:, :N_META * D].reshape(N_DEV, N_META, D),
                                  (0, 0, me * dsh), (N_DEV, N_META, dsh)).reshape(N_DEV, -1)
    rep_len = sum(int(np.prod(s)) for s in small_shapes)
    rep_part = sflat[:, N_META * D:N_META * D + rep_len]
    sp = _pack_rows([meta_part, rep_part], LANES, _chunk_rows(LANES))
    pks = lambda src: _pack([src["meta_tokens"]] + [src[n] for n in small], LANES, _chunk_rows(LANES))
    g_s, d_s, m_s, v_s = _adamw(sp, pks(W), pks(Mo), pks(Vo), "adamw_small")
    shapes_s = [W["meta_tokens"].shape] + small_shapes
    for key, packed in (("g", g_s), ("d", d_s), ("m", m_s), ("v", v_s)):
        for n, arr in zip(["meta_tokens"] + small, _unpack(packed, shapes_s)):
            res[key, n] = arr

    outs = [loss, grad_x]
    for key in ("g", "d", "m", "v"):
        outs += [res[key, n] for n in names]
    return tuple(outs)
```

```python
import numpy as np
import jax
import jax.numpy as jnp
from jax import lax
from jax.experimental import pallas as pl
from jax.experimental.pallas import tpu as pltpu

F32 = jnp.float32
BF16 = jnp.bfloat16

N_META = 16
EPS = 1e-6
HEADS = 8
FOX_HD = 64
FOX_W = HEADS * FOX_HD
MLA_Q_RANK = 256
MLA_KV_RANK = 128
MLA_NOPE = 64
MLA_ROPE = 32
MLA_QK = MLA_NOPE + MLA_ROPE
MLA_V = 64
ROPE_THETA = 10000.0
LANES = 128
HALF = LANES // 2
META_BLK = 128
NEG = -1e30

ADAM_LR = 0.001
ADAM_B1 = 0.9
ADAM_B2 = 0.999
ADAM_EPS = 1e-08
ADAM_WD = 0.01
ADAM_STEP = 10

N_DEV = 8
AXES = ("x", "y", "c")
VMEM_LIMIT_BYTES = 56 * 1024 * 1024


def _tile(n, cap, mult):
    best = None
    for d in range(mult, min(n, cap) + 1, mult):
        if n % d == 0:
            best = d
    return n if best is None else best


VREG_ELEMS = 8 * LANES


def _row_tile(rows, width):
    return _tile(rows, max(16, (1 << 19) // width), 16)


def _chunk_rows(width):
    rows = 16
    while 2 * rows * width <= 8 * VREG_ELEMS:
        rows *= 2
    return rows


def _by_chunks(rows, width, step, init=()):
    return step(pl.ds(0, rows), init)


def _params(sem=None):
    return pltpu.CompilerParams(dimension_semantics=sem, vmem_limit_bytes=VMEM_LIMIT_BYTES)


def _mm(a, b, mode, name, out_dtype=F32, scale=1.0, res=None, exch=None):
    if mode == "nn":
        (M, K), (K2, N) = a.shape, b.shape
    elif mode == "nt":
        (M, K), (N, K2) = a.shape, b.shape
    else:
        (K, M), (K2, N) = a.shape, b.shape
    assert K == K2, (a.shape, b.shape, mode)
    if mode == "tn":
        tm, tk = _tile(M, 1408, 128), _tile(K, 2080, 16)
    else:
        tm, tk = _tile(M, 832, 16), _tile(K, 4224, 128)
    tn = _tile(N, 1408, 128)
    nk = K // tk
    ni, nj = M // tm, N // tn
    bytes_a, bytes_b = a.size * a.dtype.itemsize, b.size * b.dtype.itemsize
    j_outer = nk == 1 and bytes_a * nj + bytes_b < bytes_a + bytes_b * ni
    ij = (lambda g0, g1: (g1, g0)) if j_outer else (lambda g0, g1: (g0, g1))

    def spec(shape, at):
        return pl.BlockSpec(shape, lambda g0, g1, k: at(*ij(g0, g1), k))

    a_spec = {"nn": spec((tm, tk), lambda i, j, k: (i, k)),
              "nt": spec((tm, tk), lambda i, j, k: (i, k)),
              "tn": spec((tk, tm), lambda i, j, k: (k, i))}[mode]
    b_spec = {"nn": spec((tk, tn), lambda i, j, k: (k, j)),
              "nt": spec((tn, tk), lambda i, j, k: (j, k)),
              "tn": spec((tk, tn), lambda i, j, k: (k, j))}[mode]
    dims = {"nn": (((1,), (0,)), ((), ())), "nt": (((1,), (1,)), ((), ())),
            "tn": (((0,), (0,)), ((), ()))}[mode]
    o_spec = spec((tm, tn), lambda i, j, k: (i, j))
    has_res = res is not None

    def body(*refs):
        a_ref, b_ref = refs[:2]
        r_ref = refs[2] if has_res else None
        o_ref = refs[2 + has_res]

        def finish(acc):
            o = acc * scale
            if has_res:
                o = o + r_ref[...]
            o_ref[...] = o.astype(out_dtype)

        prod = lax.dot_general(a_ref[...].astype(BF16), b_ref[...].astype(BF16), dims,
                               preferred_element_type=F32)
        if nk == 1:
            finish(prod)
            return
        acc_ref = refs[3 + has_res]
        k = pl.program_id(2)

        @pl.when(k == 0)
        def _():
            acc_ref[...] = prod

        @pl.when((k > 0) & (k < nk - 1))
        def _():
            acc_ref[...] += prod

        @pl.when(k == nk - 1)
        def _():
            finish(acc_ref[...] + prod)

    ins = [a, b] + ([res] if has_res else [])
    specs = [a_spec, b_spec] + ([o_spec] if has_res else [])
    (out,), got = _call(
        body, name, (nj, ni, nk) if j_outer else (ni, nj, nk), specs, [o_spec],
        [jax.ShapeDtypeStruct((M, N), out_dtype)],
        ins, scratch_shapes=[pltpu.VMEM((tm, tn), F32)] if nk > 1 else [],
        sem=("parallel", "parallel", "arbitrary"), exch=exch)
    return out if exch is None else (out, got)


def _rope_fwd(y, c, s1, s2):
    return y * c + pltpu.roll(y, LANES - 16, 1) * s1 + pltpu.roll(y, 16, 1) * s2


def _rope_bwd(dy, c, s1, s2):
    return dy * c + pltpu.roll(dy * s1, 16, 1) + pltpu.roll(dy * s2, LANES - 16, 1)


def _group_sum(v, seg):
    if seg == v.shape[-1]:
        return jnp.sum(v, axis=-1, keepdims=True)
    lo = lax.broadcasted_iota(jnp.int32, v.shape, 1) < seg
    s_lo = jnp.sum(jnp.where(lo, v, 0.0), axis=-1, keepdims=True)
    s_hi = jnp.sum(jnp.where(lo, 0.0, v), axis=-1, keepdims=True)
    return jnp.where(lo, s_lo, s_hi)


def _norm_fwd(src, col0, width, bw, gain, seg, d_true, name, tabs=None, out_dtype=BF16):
    T = src.shape[0]
    tr = _row_tile(T, bw)
    inv_d = 1.0 / d_true
    c0 = col0 // bw
    assert col0 % bw == 0 and width % bw == 0

    def body(*refs):
        if tabs is None:
            x_ref, g_ref, o_ref = refs
        else:
            x_ref, g_ref, c_ref, s1_ref, s2_ref, o_ref = refs
        gain_v = g_ref[...]

        def step(rows, carry):
            xv = x_ref[rows, :]
            r = lax.rsqrt(_group_sum(xv * xv, seg) * inv_d + EPS)
            y = xv * r * gain_v
            if tabs is not None:
                y = _rope_fwd(y, c_ref[rows, :], s1_ref[rows, :], s2_ref[rows, :])
            o_ref[rows, :] = y.astype(out_dtype)
            return carry

        _by_chunks(tr, bw, step)

    specs = [pl.BlockSpec((tr, bw), lambda i, j: (i, c0 + j)), pl.BlockSpec((1, bw), lambda i, j: (0, 0))]
    ins = [src, gain]
    if tabs is not None:
        tab = pl.BlockSpec((tr, LANES), lambda i, j: (i, 0))
        specs += [tab, tab, tab]
        ins += list(tabs)
    return pl.pallas_call(
        body, name=name, grid=(T // tr, width // bw), in_specs=specs,
        out_specs=pl.BlockSpec((tr, bw), lambda i, j: (i, j)),
        out_shape=jax.ShapeDtypeStruct((T, width), out_dtype),
        compiler_params=_params(("parallel", "parallel")),
    )(*ins)


def _norm_bwd_math(xv, gain, dyv, seg, inv_d):
    r = lax.rsqrt(_group_sum(xv * xv, seg) * inv_d + EPS)
    gy = dyv * gain
    dot = _group_sum(gy * xv, seg)
    dx = r * gy - xv * (r * r * r * inv_d) * dot
    return dx, jnp.sum(dyv * xv * r, axis=0, keepdims=True)


def _norm_bwd(src, col0, width, bw, gain, seg, d_true, dy, name, tabs=None, res=None, out_dtype=F32,
              bf16_copy=False):
    T = src.shape[0]
    tr = _row_tile(T, bw)
    inv_d = 1.0 / d_true
    c0 = col0 // bw
    has_res = res is not None

    def body(*refs):
        refs = list(refs)
        x_ref, g_ref, dy_ref = refs[:3]
        pos = 3
        if tabs is not None:
            c_ref, s1_ref, s2_ref = refs[3:6]
            pos = 6
        if has_res:
            r_ref = refs[pos]
            pos += 1
        dx_ref = refs[pos]
        dxb_ref = refs[pos + 1] if bf16_copy else None
        dg_ref = refs[pos + 1 + bf16_copy]
        gain_v = g_ref[...]

        def step(rows, dg_sum):
            dyv = dy_ref[rows, :].astype(F32)
            if tabs is not None:
                dyv = _rope_bwd(dyv, c_ref[rows, :], s1_ref[rows, :], s2_ref[rows, :])
            dx, dg = _norm_bwd_math(x_ref[rows, :], gain_v, dyv, seg, inv_d)
            if has_res:
                dx = dx + r_ref[rows, :]
            dx_ref[rows, :] = dx.astype(out_dtype)
            if bf16_copy:
                dxb_ref[rows, :] = dx.astype(BF16)
            return dg_sum + dg

        dg = _by_chunks(tr, bw, step, jnp.zeros((1, bw), F32))

        @pl.when((pl.program_id(0) == 0) & (pl.program_id(1) == 0))
        def _():
            dg_ref[...] = jnp.zeros_like(dg_ref)

        dg_ref[...] += dg

    blk = pl.BlockSpec((tr, bw), lambda i, j: (i, j))
    one = pl.BlockSpec((1, bw), lambda i, j: (0, 0))
    specs = [pl.BlockSpec((tr, bw), lambda i, j: (i, c0 + j)), one, blk]
    ins = [src, gain, dy]
    if tabs is not None:
        tab = pl.BlockSpec((tr, LANES), lambda i, j: (i, 0))
        specs += [tab, tab, tab]
        ins += list(tabs)
    if has_res:
        specs.append(blk)
        ins.append(res)
    extra = bf16_copy * [blk]
    extra_shape = bf16_copy * [jax.ShapeDtypeStruct((T, width), BF16)]
    return pl.pallas_call(
        body, name=name, grid=(T // tr, width // bw), in_specs=specs, out_specs=(blk, *extra, one),
        out_shape=(jax.ShapeDtypeStruct((T, width), out_dtype), *extra_shape,
                   jax.ShapeDtypeStruct((1, bw), F32)),
        compiler_params=_params(("arbitrary", "arbitrary")),
    )(*ins)


def _mla_k_raw(kv, kr):
    lane = lax.broadcasted_iota(jnp.int32, kv.shape, 1)
    return jnp.where(lane < MLA_NOPE, kv, jnp.where(lane < MLA_QK, pltpu.roll(kr, MLA_NOPE, 1), 0.0))


def _mla_k_fwd(kv_lin, z, kr_col, gain, tabs, name):
    T, W = kv_lin.shape
    tr = _row_tile(T, LANES)
    krb = kr_col // LANES
    inv_d = 1.0 / MLA_QK

    def body(kv_ref, kr_ref, g_ref, c_ref, s1_ref, s2_ref, o_ref):
        gain_v = g_ref[...]

        def step(rows, carry):
            xv = _mla_k_raw(kv_ref[rows, :], kr_ref[rows, :])
            r = lax.rsqrt(jnp.sum(xv * xv, axis=-1, keepdims=True) * inv_d + EPS)
            o_ref[rows, :] = _rope_fwd(xv * r * gain_v, c_ref[rows, :], s1_ref[rows, :],
                                       s2_ref[rows, :]).astype(BF16)
            return carry

        _by_chunks(tr, LANES, step)

    blk = pl.BlockSpec((tr, LANES), lambda i, h: (i, h))
    tab = pl.BlockSpec((tr, LANES), lambda i, h: (i, 0))
    return pl.pallas_call(
        body, name=name, grid=(T // tr, W // LANES),
        in_specs=[blk, pl.BlockSpec((tr, LANES), lambda i, h: (i, krb)),
                  pl.BlockSpec((1, LANES), lambda i, h: (0, 0)), tab, tab, tab],
        out_specs=blk, out_shape=jax.ShapeDtypeStruct((T, W), BF16),
        compiler_params=_params(("parallel", "parallel")),
    )(kv_lin, z, gain, *tabs)


def _mla_k_bwd(kv_lin, z, kr_col, gain, tabs, dk, dvk, name):
    T, W = kv_lin.shape
    tr = _row_tile(T, LANES)
    krb = kr_col // LANES
    inv_d = 1.0 / MLA_QK

    def body(kv_ref, kr_ref, g_ref, c_ref, s1_ref, s2_ref, dk_ref, dvk_ref, dkv_ref, dkr_ref, dg_ref):
        h = pl.program_id(1)
        gain_v = g_ref[...]

        @pl.when(h == 0)
        def _():
            dkr_ref[...] = jnp.zeros_like(dkr_ref)

        def step(rows, dg_sum):
            xv = _mla_k_raw(kv_ref[rows, :], kr_ref[rows, :])
            dyv = _rope_bwd(dk_ref[rows, :], c_ref[rows, :], s1_ref[rows, :], s2_ref[rows, :])
            dx, dg = _norm_bwd_math(xv, gain_v, dyv, LANES, inv_d)
            lane = lax.broadcasted_iota(jnp.int32, dx.shape, 1)
            dkv_ref[rows, :] = jnp.where(lane < MLA_NOPE, dx, dvk_ref[rows, :]).astype(BF16)
            dkr_ref[rows, :] += pltpu.roll(jnp.where((lane >= MLA_NOPE) & (lane < MLA_QK), dx, 0.0),
                                           LANES - MLA_NOPE, 1)
            return dg_sum + dg

        dg = _by_chunks(tr, LANES, step, jnp.zeros((1, LANES), F32))

        @pl.when((pl.program_id(0) == 0) & (h == 0))
        def _():
            dg_ref[...] = jnp.zeros_like(dg_ref)

        dg_ref[...] += dg

    blk = pl.BlockSpec((tr, LANES), lambda i, h: (i, h))
    tab = pl.BlockSpec((tr, LANES), lambda i, h: (i, 0))
    one = pl.BlockSpec((1, LANES), lambda i, h: (0, 0))
    return pl.pallas_call(
        body, name=name, grid=(T // tr, W // LANES),
        in_specs=[blk, pl.BlockSpec((tr, LANES), lambda i, h: (i, krb)), one, tab, tab, tab, blk, blk],
        out_specs=(blk, tab, one),
        out_shape=(jax.ShapeDtypeStruct((T, W), BF16), jax.ShapeDtypeStruct((T, LANES), F32),
                   jax.ShapeDtypeStruct((1, LANES), F32)),
        compiler_params=_params(("arbitrary", "arbitrary")),
    )(kv_lin, z, gain, *tabs, dk, dvk)


def _ffn_up(u, w_gu, name, exch=None):
    T, D = u.shape
    F = w_gu.shape[1] // 2
    tm, tn = _tile(T, 832, 16), _tile(F, 1408, 128)
    nj = F // tn

    def body(u_ref, wg_ref, wu_ref, sa_ref, sb_ref, a_ref):
        uv = u_ref[...]
        g = jnp.dot(uv, wg_ref[...], preferred_element_type=F32)
        up = jnp.dot(uv, wu_ref[...], preferred_element_type=F32)
        sg = jax.nn.sigmoid(g)
        silu = g * sg
        sa_ref[...] = silu.astype(BF16)
        sb_ref[...] = (up * (sg + silu * (1.0 - sg))).astype(BF16)
        a_ref[...] = (silu * up).astype(BF16)

    o_spec = pl.BlockSpec((tm, tn), lambda j, i: (i, j))
    sh = jax.ShapeDtypeStruct((T, F), BF16)
    return _call(
        body, name, (nj, T // tm),
        [pl.BlockSpec((tm, D), lambda j, i: (i, 0)),
         pl.BlockSpec((D, tn), lambda j, i: (0, j)),
         pl.BlockSpec((D, tn), lambda j, i: (0, j + nj))],
        [o_spec, o_spec, o_spec], [sh, sh, sh], [u, w_gu, w_gu],
        sem=("parallel", "parallel"), exch=exch)


def _ffn_down_bwd(dh, w_down, sa, sb, name, exch=None):
    T, D = dh.shape
    F = w_down.shape[0]
    tm, tn = _tile(T, 832, 16), _tile(F, 1408, 128)

    def body(dh_ref, w_ref, sa_ref, sb_ref, dg_ref, dup_ref):
        da = 0.5 * lax.dot_general(dh_ref[...].astype(BF16), w_ref[...], (((1,), (1,)), ((), ())),
                                   preferred_element_type=F32)
        dup_ref[...] = (da * sa_ref[...].astype(F32)).astype(BF16)
        dg_ref[...] = (da * sb_ref[...].astype(F32)).astype(BF16)

    t_spec = pl.BlockSpec((tm, tn), lambda j, i: (i, j))
    sh = jax.ShapeDtypeStruct((T, F), BF16)
    return _call(
        body, name, (F // tn, T // tm),
        [pl.BlockSpec((tm, D), lambda j, i: (i, 0)),
         pl.BlockSpec((tn, D), lambda j, i: (j, 0)), t_spec, t_spec],
        [t_spec, t_spec], [sh, sh], [dh, w_down, sa, sb],
        sem=("parallel", "parallel"), exch=exch)


def _ffn_up_bwd_dx(dg, dup, w_gu, name, exch=None):
    T, F = dg.shape
    D = w_gu.shape[0]
    tm, tk = _tile(T, 640, 16), _tile(F, 2816, 128)
    nk = F // tk
    nt = (((1,), (1,)), ((), ()))

    def body(dg_ref, dup_ref, wg_ref, wu_ref, o_ref, acc_ref):
        k = pl.program_id(1)
        prod = (lax.dot_general(dg_ref[...], wg_ref[...], nt, preferred_element_type=F32)
                + lax.dot_general(dup_ref[...], wu_ref[...], nt, preferred_element_type=F32))
        if nk == 1:
            o_ref[...] = prod
            return

        @pl.when(k == 0)
        def _():
            acc_ref[...] = prod

        @pl.when((k > 0) & (k < nk - 1))
        def _():
            acc_ref[...] += prod

        @pl.when(k == nk - 1)
        def _():
            o_ref[...] = acc_ref[...] + prod

    return _call(
        body, name, (T // tm, nk),
        [pl.BlockSpec((tm, tk), lambda i, k: (i, k)),
         pl.BlockSpec((tm, tk), lambda i, k: (i, k)),
         pl.BlockSpec((D, tk), lambda i, k: (0, k)),
         pl.BlockSpec((D, tk), lambda i, k: (0, k + nk))],
        [pl.BlockSpec((tm, D), lambda i, k: (i, 0))], [jax.ShapeDtypeStruct((T, D), F32)],
        [dg, dup, w_gu, w_gu], scratch_shapes=[pltpu.VMEM((tm, D), F32)],
        sem=("parallel", "arbitrary"), exch=exch)


def _logsig(x):
    return jnp.minimum(x, 0.0) - jnp.log(1.0 + jnp.exp(-jnp.abs(x)))


def _cum_fwd(fl, flm, bf, name):
    B, S, _ = fl.shape
    nb = S // LANES

    def body(fl_ref, flm_ref, bf_ref, cum_ref, cumm_ref):
        rows = lax.broadcasted_iota(jnp.int32, (LANES, LANES), 0)
        cols = lax.broadcasted_iota(jnp.int32, (LANES, LANES), 1)
        tri = (rows >= cols).astype(F32)
        bias = bf_ref[...]
        lfm = jnp.where(rows < N_META, _logsig(flm_ref[...] + bias), 0.0)
        cm = jnp.dot(tri, lfm, precision=lax.Precision.HIGHEST, preferred_element_type=F32)
        cumm_ref[...] = cm * LOG2E
        base = cm[LANES - 1:LANES, :]
        for b in range(B):
            def blk(i, carry):
                r0 = pl.multiple_of(i * LANES, LANES)
                lf = _logsig(fl_ref[b, pl.ds(r0, LANES), :] + bias)
                c = jnp.dot(tri, lf, precision=lax.Precision.HIGHEST,
                            preferred_element_type=F32) + carry
                cum_ref[b, pl.ds(r0, LANES), :] = c * LOG2E
                return c[LANES - 1:LANES, :]

            lax.fori_loop(0, nb, blk, base)

    return pl.pallas_call(
        body, name=name,
        out_shape=(jax.ShapeDtypeStruct((B, S, LANES), F32),
                   jax.ShapeDtypeStruct((LANES, LANES), F32)),
        compiler_params=_params(),
    )(fl, flm, bf)


def _cum_bwd(dc, dcm, fl, flm, bf, name):
    B, S, _ = fl.shape
    nb = S // LANES

    def body(dc_ref, dcm_ref, fl_ref, flm_ref, bf_ref, dfl_ref, dflm_ref, dbf_ref):
        rows = lax.broadcasted_iota(jnp.int32, (LANES, LANES), 0)
        cols = lax.broadcasted_iota(jnp.int32, (LANES, LANES), 1)
        triu = (rows <= cols).astype(F32)
        bias = bf_ref[...]
        total = jnp.zeros((1, LANES), F32)
        dbf = jnp.zeros((1, LANES), F32)
        for b in range(B):
            tail = jnp.zeros((1, LANES), F32)
            for t in range(nb):
                r0 = (nb - 1 - t) * LANES
                rc = jnp.dot(triu, dc_ref[b, r0:r0 + LANES, :], precision=lax.Precision.HIGHEST,
                             preferred_element_type=F32) + tail
                xv = fl_ref[b, r0:r0 + LANES, :] + bias
                d = rc / (1.0 + jnp.exp(xv))
                dfl_ref[b, r0:r0 + LANES, :] = d
                tail = rc[0:1, :]
                dbf = dbf + jnp.sum(d, axis=0, keepdims=True)
            total = total + tail
        rcm = jnp.dot(triu, dcm_ref[...], precision=lax.Precision.HIGHEST,
                      preferred_element_type=F32) + total
        dm = jnp.where(rows < N_META, rcm / (1.0 + jnp.exp(flm_ref[...] + bias)), 0.0)
        dflm_ref[...] = dm
        dbf_ref[...] = dbf + jnp.sum(dm, axis=0, keepdims=True)

    return pl.pallas_call(
        body, name=name,
        out_shape=(jax.ShapeDtypeStruct((B, S, LANES), F32),
                   jax.ShapeDtypeStruct((LANES, LANES), F32),
                   jax.ShapeDtypeStruct((1, LANES), F32)),
        compiler_params=_params(),
    )(dc, dcm, fl, flm, bf)


_NT = (((1,), (1,)), ((), ()))


def _token_rows_buffer(T, NX, width, dtype=F32):
    return lax.dynamic_update_slice(lax.empty((T, width), dtype), jnp.zeros((T - NX, width), dtype), (NX, 0))


def _attn_specs(S, NX, qw, v_col0):
    mb = NX // META_BLK
    vb = v_col0 // qw
    return (pl.BlockSpec((S, qw), lambda b, p: (b, p)),
            pl.BlockSpec((META_BLK, qw), lambda b, p: (mb, p)),
            pl.BlockSpec((S, qw), lambda b, p: (b, vb + p)),
            pl.BlockSpec((META_BLK, qw), lambda b, p: (mb, vb + p)),
            pl.BlockSpec((S, LANES), lambda b, p: (b, p)))


def _cum_specs(S, TK):
    return [pl.BlockSpec((1, 2, S // TK, 1, TK), lambda b, p: (b, p, 0, 0, 0)),
            pl.BlockSpec((1, 2, 1, META_BLK), lambda b, p: (b, p, 0, 0))]


LOG2E = 1.4426950408889634


def _attn_fwd(qn, kn, vsrc, v_col0, mla, scale, S, NX, name, ck=None, cmk=None, exch=None):
    T = qn.shape[0]
    B = NX // S
    qw = 2 * LANES if mla else LANES
    npair = qn.shape[1] // qw
    TQ = min(512, S)
    TK = TQ
    forget = ck is not None
    a = scale * LOG2E

    def body(*refs):
        if forget:
            q_ref, k_ref, km_ref, v_ref, vm_ref, ck_ref, cmk_ref, _, o_ref, lse_ref = refs
        else:
            q_ref, k_ref, km_ref, v_ref, vm_ref, _, o_ref, lse_ref = refs
        lo = lax.broadcasted_iota(jnp.int32, (1, LANES), 1) < HALF
        mcol = lax.broadcasted_iota(jnp.int32, (TQ, META_BLK), 1)
        causal = (lax.broadcasted_iota(jnp.int32, (TQ, TK), 0)
                  >= lax.broadcasted_iota(jnp.int32, (TQ, TK), 1))
        two = lax.broadcasted_iota(jnp.int32, (TQ, 2), 1)
        for qi in range(S // TQ):
            q0 = qi * TQ
            sls = [slice(e * LANES, (e + 1) * LANES) if mla else slice(None) for e in range(2)]
            if mla:
                qts = [q_ref[q0:q0 + TQ, sl] for sl in sls]
            else:
                qts = [jnp.where(lo if e == 0 else ~lo, q_ref[q0:q0 + TQ, :], 0.0).astype(BF16)
                       for e in range(2)]

            def step(e, kt, vt, c2, mask, carry):
                m, l, acc = carry
                s = lax.dot_general(qts[e], kt, _NT, preferred_element_type=F32) * a
                if forget:
                    s = s - c2
                if mask is not None:
                    s = jnp.where(mask, s, NEG)
                m2 = jnp.max(s, axis=1, keepdims=True)
                if m is not None:
                    m2 = jnp.maximum(m, m2)
                p = jnp.exp2(s - m2)
                l2 = jnp.sum(p, axis=1, keepdims=True)
                acc2 = jnp.dot(p.astype(BF16), vt.astype(BF16), preferred_element_type=F32)
                if m is not None:
                    alpha = jnp.exp2(m - m2)
                    l2, acc2 = alpha * l + l2, alpha * acc + acc2
                return m2, l2, acc2

            def both(rows, kj, mask, carry):
                return tuple(step(e, k_ref[rows, sls[e]], v_ref[rows, sls[e]],
                                  ck_ref[0, e, kj] if forget else None, mask, carry[e]) for e in range(2))

            def below(kj, carry):
                return both(pl.ds(pl.multiple_of(kj * TK, TK), TK), kj, None, carry)

            carry = tuple(step(e, km_ref[:, sls[e]], vm_ref[:, sls[e]], cmk_ref[0, e] if forget else None,
                               mcol < N_META, (None, None, None)) for e in range(2))
            if qi:
                carry = lax.fori_loop(0, qi, below, carry)
            carry = both(slice(q0, q0 + TK), qi, causal, carry)
            outs = [acc / l for _, l, acc in carry]
            lses = [m + jnp.log2(l) for m, l, _ in carry]
            first = pltpu.roll(outs[0], HALF, 1) if mla else outs[0]
            o_ref[q0:q0 + TQ, :] = jnp.where(lo, first, outs[1])
            lse_ref[0, 0, q0:q0 + TQ, :] = jnp.where(two == 0, lses[0], lses[1])

    qk, kmeta, vv, vmeta, pair = _attn_specs(S, NX, qw, v_col0)
    specs = [qk, qk, kmeta, vv, vmeta]
    ins = [qn, kn, kn, vsrc, vsrc]
    if forget:
        specs += _cum_specs(S, TK)
        ins += [ck, cmk]
    specs.append(pl.BlockSpec(memory_space=pl.ANY))
    ins.append(_token_rows_buffer(T, NX, npair * LANES))
    lse_spec = pl.BlockSpec((1, 1, S, 2), lambda b, p: (b, p, 0, 0))
    return _call(
        body, name, (B, npair), specs, [pair, lse_spec],
        [jax.ShapeDtypeStruct((T, npair * LANES), F32), jax.ShapeDtypeStruct((B, npair, S, 2), F32)],
        ins, sem=("parallel", "parallel"), aliases={len(ins) - 1: 0}, exch=exch)


def _attn_bwd(qn, kn, vsrc, v_col0, o, lse, do, mla, scale, S, NX, name, ck=None, cmk=None, exch=None):
    T, W = qn.shape
    B = NX // S
    qw = 2 * LANES if mla else LANES
    npair = W // qw
    TQ = min(512, S)
    TK = TQ
    forget = ck is not None
    a = scale * LOG2E
    _TN = (((0,), (0,)), ((), ()))

    def body(*refs):
        refs = list(refs)
        q_ref, k_ref, km_ref, v_ref, vm_ref, o_ref, do_ref, lse_ref = refs[:8]
        pos = 8
        if forget:
            ck_ref, cmk_ref = refs[8:10]
            pos = 10
        pos += 3
        dq_ref, dk_ref, dv_ref, dkm_ref, dvm_ref = refs[pos:pos + 5]
        if forget:
            dck_ref, dcm_ref, dcq_ref = refs[pos + 5:pos + 8]
            dck_ref[...] = jnp.zeros_like(dck_ref)
            dcm_ref[...] = jnp.zeros_like(dcm_ref)
        dk_ref[...] = jnp.zeros_like(dk_ref)
        dv_ref[...] = jnp.zeros_like(dv_ref)
        dkm_ref[...] = jnp.zeros_like(dkm_ref)
        dvm_ref[...] = jnp.zeros_like(dvm_ref)
        lo = lax.broadcasted_iota(jnp.int32, (1, LANES), 1) < HALF
        mcol = lax.broadcasted_iota(jnp.int32, (TQ, META_BLK), 1)
        causal = (lax.broadcasted_iota(jnp.int32, (TQ, TK), 0)
                  >= lax.broadcasted_iota(jnp.int32, (TQ, TK), 1))
        two = lax.broadcasted_iota(jnp.int32, (TQ, 2), 1)
        for qi in range(S // TQ):
            q0 = qi * TQ
            dof = do_ref[q0:q0 + TQ, :]
            prod = dof * o_ref[q0:q0 + TQ, :]
            lse2 = lse_ref[0, 0, q0:q0 + TQ, :]
            sls = [slice(e * LANES, (e + 1) * LANES) if mla else slice(None) for e in range(2)]
            mine = [lo, ~lo]
            if mla:
                qts = [q_ref[q0:q0 + TQ, sl] for sl in sls]
                dots = [jnp.where(lo, 0.0, pltpu.roll(dof, HALF, 1) if e == 0 else dof).astype(BF16)
                        for e in range(2)]
            else:
                qts = [jnp.where(mine[e], q_ref[q0:q0 + TQ, :], 0.0).astype(BF16) for e in range(2)]
                dots = [jnp.where(mine[e], dof, 0.0).astype(BF16) for e in range(2)]
            deltas = [jnp.sum(jnp.where(mine[e], prod, 0.0), axis=1, keepdims=True) for e in range(2)]
            lse_ts = [jnp.sum(jnp.where(two == e, lse2, 0.0), axis=1, keepdims=True) for e in range(2)]

            def grads(e, kt, vt, c2, mask):
                s = lax.dot_general(qts[e], kt, _NT, preferred_element_type=F32) * a
                if forget:
                    s = s - c2
                p = jnp.exp2(s - lse_ts[e])
                if mask is not None:
                    p = jnp.where(mask, p, 0.0)
                dp = lax.dot_general(dots[e], vt, _NT, preferred_element_type=F32)
                ds = p * (dp - deltas[e])
                dsb = ds.astype(BF16)
                return (jnp.dot(dsb, kt, preferred_element_type=F32),
                        lax.dot_general(dsb, qts[e], _TN, preferred_element_type=F32) * scale,
                        lax.dot_general(p.astype(BF16), dots[e], _TN, preferred_element_type=F32),
                        -jnp.sum(ds, axis=0, keepdims=True) if forget else None,
                        jnp.sum(ds, axis=1, keepdims=True) if forget else None)

            def block(k_at, v_at, dk_at, dv_at, c_at, dc_at, mask, dqs):
                got = [grads(e, k_at(sls[e]), v_at(sls[e]).astype(BF16), c_at(e) if forget else None, mask)
                       for e in range(2)]
                if mla:
                    for e in range(2):
                        dk_at(sls[e], got[e][1])
                        dv_at(sls[e], got[e][2])
                else:
                    dk_at(sls[0], got[0][1] + got[1][1])
                    dv_at(sls[0], got[0][2] + got[1][2])
                if forget:
                    for e in range(2):
                        dc_at(e, got[e][3])
                picks = (0, 0, 4, 4) if forget else (0, 0)
                new = tuple(got[i % 2][k] for i, k in enumerate(picks))
                return new if dqs is None else tuple(x + y for x, y in zip(dqs, new))

            def add_to(ref, *lead):
                def add(*idx_and_val):
                    *idx, val = idx_and_val
                    ref[(*lead, *idx)] += val
                return add

            def token_block(rows, kj, mask, dqs):
                return block(lambda sl: k_ref[rows, sl], lambda sl: v_ref[rows, sl],
                             lambda sl, val: add_to(dk_ref)(rows, sl, val),
                             lambda sl, val: add_to(dv_ref)(rows, sl, val),
                             lambda e: ck_ref[0, e, kj], lambda e, val: add_to(dck_ref, 0)(e, kj, val),
                             mask, dqs)

            dqs = block(lambda sl: km_ref[:, sl], lambda sl: vm_ref[:, sl],
                        lambda sl, val: add_to(dkm_ref, 0)(slice(None), sl, val),
                        lambda sl, val: add_to(dvm_ref, 0)(slice(None), sl, val),
                        lambda e: cmk_ref[0, e], lambda e, val: add_to(dcm_ref, 0)(e, val),
                        mcol < N_META, None)

            def below(kj, dqs):
                return token_block(pl.ds(pl.multiple_of(kj * TK, TK), TK), kj, None, dqs)

            if qi:
                dqs = lax.fori_loop(0, qi, below, dqs)
            dqs = token_block(slice(q0, q0 + TK), qi, causal, dqs)
            if forget:
                dcq_ref[0, 0, q0:q0 + TQ, :] = jnp.where(two == 0, dqs[2], dqs[3])
            if mla:
                for e in range(2):
                    dq_ref[q0:q0 + TQ, sls[e]] = dqs[e] * scale
            else:
                dq_ref[q0:q0 + TQ, :] = jnp.where(lo, dqs[0], dqs[1]) * scale

    qk, kmeta, vv, vmeta, pair = _attn_specs(S, NX, qw, v_col0)
    lse_spec = pl.BlockSpec((1, 1, S, 2), lambda b, p: (b, p, 0, 0))
    specs = [qk, qk, kmeta, vv, vmeta, pair, pair, lse_spec]
    ins = [qn, kn, kn, vsrc, vsrc, o, do, lse]
    if forget:
        specs += _cum_specs(S, TK)
        ins += [ck, cmk]
    first_alias = len(ins)
    specs += [pl.BlockSpec(memory_space=pl.ANY)] * 3
    ins += [_token_rows_buffer(T, NX, W) for _ in range(3)]
    mspec = pl.BlockSpec((1, META_BLK, qw), lambda b, p: (b, 0, p))
    out_specs = [qk, qk, qk, mspec, mspec]
    tok = jax.ShapeDtypeStruct((T, W), F32)
    met = jax.ShapeDtypeStruct((B, META_BLK, W), F32)
    out_shape = [tok, tok, tok, met, met]
    if forget:
        out_specs += _cum_specs(S, TK) + [lse_spec]
        out_shape += [jax.ShapeDtypeStruct((B, HEADS, S // TK, 1, TK), F32),
                      jax.ShapeDtypeStruct((B, HEADS, 1, META_BLK), F32),
                      jax.ShapeDtypeStruct((B, npair, S, 2), F32)]
    return _call(
        body, name, (B, npair), specs, out_specs, out_shape, ins, sem=("parallel", "parallel"),
        aliases={first_alias: 0, first_alias + 1: 1, first_alias + 2: 2}, exch=exch)


def _gate_fwd(z, bg, of, om, name):
    T, D = of.shape
    tm = _tile(T, 640, 16)

    def body(z_ref, bg_ref, of_ref, om_ref, o_ref):
        bias = bg_ref[...]

        def step(rows, carry):
            gt = jax.nn.sigmoid(z_ref[rows, :] + bias)
            o_ref[rows, :] = (gt[:, :D] * of_ref[rows, :] + gt[:, D:] * om_ref[rows, :]).astype(BF16)
            return carry

        _by_chunks(tm, D, step)

    row = pl.BlockSpec((tm, D), lambda i: (i, 0))
    return pl.pallas_call(
        body, name=name, grid=(T // tm,),
        in_specs=[pl.BlockSpec((tm, 2 * D), lambda i: (i, 0)),
                  pl.BlockSpec((1, 2 * D), lambda i: (0, 0)), row, row],
        out_specs=row, out_shape=jax.ShapeDtypeStruct((T, D), BF16),
        compiler_params=_params(("parallel",)),
    )(z, bg, of, om)


def _gate_bwd(dmix, z, bg, of, om, name):
    T, D = of.shape
    tm = _tile(T, 640, 16)

    def body(dm_ref, z_ref, bg_ref, of_ref, om_ref, dgl_ref, dof_ref, dom_ref, dbg_ref):
        bias = bg_ref[...]

        def step(rows, dbg_sum):
            gt = jax.nn.sigmoid(z_ref[rows, :] + bias)
            dm = dm_ref[rows, :]
            dof_ref[rows, :] = (dm * gt[:, :D]).astype(BF16)
            dom_ref[rows, :] = (dm * gt[:, D:]).astype(BF16)
            dgl = jnp.concatenate([dm * of_ref[rows, :], dm * om_ref[rows, :]], axis=1) * gt * (1.0 - gt)
            dgl_ref[rows, :] = dgl.astype(BF16)
            return dbg_sum + jnp.sum(dgl, axis=0, keepdims=True)

        dbg = _by_chunks(tm, D, step, jnp.zeros((1, 2 * D), F32))

        @pl.when(pl.program_id(0) == 0)
        def _():
            dbg_ref[...] = jnp.zeros_like(dbg_ref)

        dbg_ref[...] += dbg

    row = pl.BlockSpec((tm, D), lambda i: (i, 0))
    wide = pl.BlockSpec((tm, 2 * D), lambda i: (i, 0))
    one = pl.BlockSpec((1, 2 * D), lambda i: (0, 0))
    return pl.pallas_call(
        body, name=name, grid=(T // tm,),
        in_specs=[row, wide, one, row, row], out_specs=(wide, row, row, one),
        out_shape=(jax.ShapeDtypeStruct((T, 2 * D), BF16), jax.ShapeDtypeStruct((T, D), BF16),
                   jax.ShapeDtypeStruct((T, D), BF16), jax.ShapeDtypeStruct((1, 2 * D), F32)),
        compiler_params=_params(("arbitrary",)),
    )(dmix, z, bg, of, om)


def _loss(h, tgt, name):
    T, D = h.shape
    NX = tgt.shape[0]
    tm = _tile(NX, 640, 16)

    def body(h_ref, t_ref, _, __, dh_ref, dhb_ref, l_ref):
        i = pl.program_id(0)

        def step(rows, part):
            err = h_ref[rows, :] - t_ref[rows, :]
            dh = err * (1.0 / D)
            dh_ref[rows, :] = dh
            dhb_ref[rows, :] = dh.astype(BF16)
            return part + jnp.sum(err * err, axis=0, keepdims=True)

        part = _by_chunks(tm, D, step, jnp.zeros((1, D), F32))

        @pl.when(i == 0)
        def _():
            l_ref[...] = jnp.zeros_like(l_ref)

        l_ref[...] += 0.5 * jnp.sum(part) * (1.0 / D)

    row = pl.BlockSpec((tm, D), lambda i: (i, 0))
    acc = pl.BlockSpec((8, LANES), lambda i: (0, 0))
    hbm = pl.BlockSpec(memory_space=pl.ANY)
    return pl.pallas_call(
        body, name=name, grid=(NX // tm,), in_specs=[row, row, hbm, hbm], out_specs=(row, row, acc),
        out_shape=(jax.ShapeDtypeStruct((T, D), F32), jax.ShapeDtypeStruct((T, D), BF16),
                   jax.ShapeDtypeStruct((8, LANES), F32)),
        input_output_aliases={2: 0, 3: 1},
        compiler_params=_params(("arbitrary",)),
    )(h, tgt, _token_rows_buffer(T, NX, D), _token_rows_buffer(T, NX, D, BF16))


def _adamw(parts, w, m, v, name):
    P, R, C = parts.shape
    tr = _tile(R, max(8, (1 << 18) // C), 8)
    bc1 = 1.0 - ADAM_B1 ** ADAM_STEP
    bc2 = 1.0 - ADAM_B2 ** ADAM_STEP

    def body(p_ref, w_ref, m_ref, v_ref, g_ref, d_ref, m2_ref, v2_ref):
        def step(rows, carry):
            g = p_ref[0, rows, :].astype(F32)
            for j in range(1, P):
                g = g + p_ref[j, rows, :].astype(F32)
            m2 = ADAM_B1 * m_ref[rows, :] + (1.0 - ADAM_B1) * g
            v2 = ADAM_B2 * v_ref[rows, :] + (1.0 - ADAM_B2) * (g * g)
            m_hat = m2 / bc1
            v_hat = v2 / bc2
            g_ref[rows, :] = g
            d_ref[rows, :] = -ADAM_LR * (m_hat / (jnp.sqrt(v_hat) + ADAM_EPS) + ADAM_WD * w_ref[rows, :])
            m2_ref[rows, :] = m2
            v2_ref[rows, :] = v2
            return carry

        _by_chunks(tr, C, step)

    row = pl.BlockSpec((tr, C), lambda i: (i, 0))
    sh = jax.ShapeDtypeStruct((R, C), F32)
    return pl.pallas_call(
        body, name=name, grid=(R // tr,),
        in_specs=[pl.BlockSpec((P, tr, C), lambda i: (0, i, 0)), row, row, row],
        out_specs=(row, row, row, row), out_shape=(sh, sh, sh, sh),
        compiler_params=_params(("parallel",)),
    )(parts, w, m, v)


def _peer(d):
    x, y, c = lax.axis_index("x"), lax.axis_index("y"), lax.axis_index("c")
    px = 1 - x if d & 4 else x
    py = 1 - y if d & 2 else y
    pc = 1 - c if d & 1 else c
    return (px, py, pc), 4 * px + 2 * py + pc


class _Exchange:
    def __init__(self, srcs, gather):
        self.srcs, self.gather, self.n = list(srcs), gather, len(srcs)
        n = self.n
        hbm = pl.BlockSpec(memory_space=pl.ANY)
        self.in_specs = [hbm] * n
        self.out_specs = [hbm] * n
        self.out_shape = [jax.ShapeDtypeStruct((N_DEV,) + s.shape[-2:], s.dtype) for s in srcs]
        self.scratch = [pltpu.SemaphoreType.DMA((N_DEV - 1, n)), pltpu.SemaphoreType.DMA((N_DEV - 1, n)),
                        pltpu.SemaphoreType.DMA((n,))]

    def _copies(self, src_refs, out_refs, sems):
        send_sems, recv_sems, local_sems = sems
        _, me = _peer(0)

        def remote(w, d, landing):
            dev, lin = _peer(d)
            return pltpu.make_async_remote_copy(
                src_ref=src_refs[w] if self.gather else src_refs[w].at[lin],
                dst_ref=out_refs[w].at[lin if landing else me],
                send_sem=send_sems.at[d - 1, w], recv_sem=recv_sems.at[d - 1, w],
                device_id=dev, device_id_type=pl.DeviceIdType.MESH)

        pairs = [(w, d) for d in range(1, N_DEV) for w in range(self.n)]
        own = [pltpu.make_async_copy(src_refs[w] if self.gather else src_refs[w].at[me],
                                     out_refs[w].at[me], local_sems.at[w]) for w in range(self.n)]
        return own, [remote(w, d, False) for w, d in pairs], [remote(w, d, True) for w, d in pairs]

    def _gather_copies(self, src_refs, out_refs, sems):
        send_sems, recv_sems, local_sems = sems
        x, y, c = lax.axis_index("x"), lax.axis_index("y"), lax.axis_index("c")
        me, sibling = (x, y, c), (x, y, 1 - c)
        chips = [(1 - x, y), (x, 1 - y), (1 - x, 1 - y)]

        def copy(w, k, block, to, src=None):
            rows = out_refs[w].at[4 * block[0] + 2 * block[1] + block[2]]
            return pltpu.make_async_remote_copy(
                src_ref=rows if src is None else src, dst_ref=rows,
                send_sem=send_sems.at[k, w], recv_sem=recv_sems.at[k, w],
                device_id=to, device_id_type=pl.DeviceIdType.MESH)

        ws = range(self.n)
        own = [pltpu.make_async_copy(src_refs[w], out_refs[w].at[4 * x + 2 * y + c], local_sems.at[w])
               for w in ws]
        first = [copy(w, 0, me, sibling, src_refs[w]) for w in ws]
        first += [copy(w, 1 + j, me, (*chip, c), src_refs[w]) for j, chip in enumerate(chips) for w in ws]
        landed = [[copy(w, 1 + j, (*chip, c), me) for w in ws] for j, chip in enumerate(chips)]
        passed = [[copy(w, 4 + j, (*chip, c), sibling) for w in ws] for j, chip in enumerate(chips)]
        from_sibling = [copy(w, 0, sibling, me) for w in ws]
        from_sibling += [copy(w, 4 + j, (*chip, 1 - c), me) for j, chip in enumerate(chips) for w in ws]
        return own, first, landed, passed, from_sibling

    def start(self, src_refs, out_refs, sems):
        if self.gather:
            own, first = self._gather_copies(src_refs, out_refs, sems)[:2]
            sent = first
        else:
            own, sent, _ = self._copies(src_refs, out_refs, sems)
        for cp in own + sent:
            cp.start()

    def wait(self, src_refs, out_refs, sems):
        if self.gather:
            own, first, landed, passed, from_sibling = self._gather_copies(src_refs, out_refs, sems)
            for arrived, onward in zip(landed, passed):
                for cp in arrived:
                    cp.wait_recv()
                for cp in onward:
                    cp.start()
            for cp in from_sibling:
                cp.wait_recv()
            for cp in first + [cp for group in passed for cp in group]:
                cp.wait_send()
        else:
            own, sent, landing = self._copies(src_refs, out_refs, sems)
            for cp in landing:
                cp.wait_recv()
            for cp in sent:
                cp.wait_send()
        for cp in own:
            cp.wait()


def _exchange(srcs, name, gather):
    ex = _Exchange(srcs, gather)
    n = ex.n

    def body(*refs):
        ex.start(refs[:n], refs[n:2 * n], refs[2 * n:])
        ex.wait(refs[:n], refs[n:2 * n], refs[2 * n:])

    outs = pl.pallas_call(
        body, name=name, in_specs=ex.in_specs, out_specs=tuple(ex.out_specs),
        out_shape=tuple(ex.out_shape), scratch_shapes=ex.scratch,
    )(*srcs)
    return list(outs)


def _call(body, name, grid, in_specs, out_specs, out_shape, ins, scratch_shapes=(), sem=None,
          aliases=None, exch=None):
    aliases = aliases or {}
    if exch is None:
        outs = pl.pallas_call(
            body, name=name, grid=grid, in_specs=list(in_specs), out_specs=tuple(out_specs),
            out_shape=tuple(out_shape), scratch_shapes=list(scratch_shapes),
            input_output_aliases=aliases, compiler_params=_params(sem),
        )(*ins)
        return list(outs), []
    ni, no, ns, n = len(in_specs), len(out_specs), len(scratch_shapes), exch.n
    last_ids = [g - 1 for g in grid]

    def hosted(*refs):
        cin, xin = refs[:ni], refs[ni:ni + n]
        cout, xout = refs[ni + n:ni + n + no], refs[ni + n + no:ni + 2 * n + no]
        cscr, xsem = refs[ni + 2 * n + no:ni + 2 * n + no + ns], refs[ni + 2 * n + no + ns:]
        ids = [pl.program_id(a) for a in range(len(grid))]
        first, last = ids[0] == 0, ids[0] == last_ids[0]
        for a in range(1, len(grid)):
            first, last = first & (ids[a] == 0), last & (ids[a] == last_ids[a])

        @pl.when(first)
        def _():
            exch.start(xin, xout, xsem)

        body(*cin, *cout, *cscr)

        @pl.when(last)
        def _():
            exch.wait(xin, xout, xsem)

    outs = pl.pallas_call(
        hosted, name=name, grid=grid, in_specs=list(in_specs) + exch.in_specs,
        out_specs=tuple(list(out_specs) + exch.out_specs),
        out_shape=tuple(list(out_shape) + exch.out_shape),
        scratch_shapes=list(scratch_shapes) + exch.scratch, input_output_aliases=aliases,
        compiler_params=_params(("arbitrary",) * len(grid)),
    )(*ins, *exch.srcs)
    return list(outs[:no]), list(outs[no:])


def _pack(arrs, cols, row_mult):
    flat = jnp.concatenate([a.reshape(-1) for a in arrs])
    n = flat.shape[0]
    quantum = cols * row_mult
    total = -(-n // quantum) * quantum
    return jnp.pad(flat, (0, total - n)).reshape(total // cols, cols)


def _pack_rows(arrs, cols, row_mult):
    flat = jnp.concatenate(arrs, axis=1)
    n = flat.shape[1]
    quantum = cols * row_mult
    total = -(-n // quantum) * quantum
    return jnp.pad(flat, ((0, 0), (0, total - n))).reshape(N_DEV, total // cols, cols)


def _unpack(packed, shapes):
    flat = packed.reshape(-1)
    out, off = [], 0
    for s in shapes:
        n = int(np.prod(s))
        out.append(flat[off:off + n].reshape(s))
        off += n
    return out


def _rope_tables(positions):
    inv_freq = ROPE_THETA ** (-jnp.arange(0, MLA_ROPE, 2, dtype=F32) / MLA_ROPE)
    ang = positions.astype(F32)[:, None] * inv_freq[None, :]
    cos, sin = jnp.cos(ang), jnp.sin(ang)
    n = positions.shape[0]
    ones, zeros = jnp.ones((n, MLA_NOPE), F32), jnp.zeros((n, MLA_NOPE), F32)
    tail1, tail0 = jnp.ones((n, LANES - MLA_QK), F32), jnp.zeros((n, LANES - MLA_QK), F32)
    z16 = jnp.zeros((n, 16), F32)
    c = jnp.concatenate([ones, cos, cos, tail1], axis=1)
    s1 = jnp.concatenate([zeros, -sin, z16, tail0], axis=1)
    s2 = jnp.concatenate([zeros, z16, sin, tail0], axis=1)
    return c, s1, s2


def kernel(x, meta_tokens, ffn1_norm, ffn1_w_gu, ffn1_w_down, mix_norm, w_in, b_forget, b_gate, fox_q_norm, fox_k_norm, mla_cq_norm, mla_w_uq, mla_ckv_norm, mla_w_ukv, mla_q_norm, mla_k_norm, w_branch_fox, w_branch_mla, w_out, ffn2_norm, ffn2_w_gu, ffn2_w_down, loss_target, m_meta_tokens, m_ffn1_norm, m_ffn1_w_gu, m_ffn1_w_down, m_mix_norm, m_w_in, m_b_forget, m_b_gate, m_fox_q_norm, m_fox_k_norm, m_mla_cq_norm, m_mla_w_uq, m_mla_ckv_norm, m_mla_w_ukv, m_mla_q_norm, m_mla_k_norm, m_w_branch_fox, m_w_branch_mla, m_w_out, m_ffn2_norm, m_ffn2_w_gu, m_ffn2_w_down, v_meta_tokens, v_ffn1_norm, v_ffn1_w_gu, v_ffn1_w_down, v_mix_norm, v_w_in, v_b_forget, v_b_gate, v_fox_q_norm, v_fox_k_norm, v_mla_cq_norm, v_mla_w_uq, v_mla_ckv_norm, v_mla_w_ukv, v_mla_q_norm, v_mla_k_norm, v_w_branch_fox, v_w_branch_mla, v_w_out, v_ffn2_norm, v_ffn2_w_gu, v_ffn2_w_down):
    names = ["meta_tokens", "ffn1_norm", "ffn1_w_gu", "ffn1_w_down", "mix_norm", "w_in", "b_forget",
             "b_gate", "fox_q_norm", "fox_k_norm", "mla_cq_norm", "mla_w_uq", "mla_ckv_norm",
             "mla_w_ukv", "mla_q_norm", "mla_k_norm", "w_branch_fox", "w_branch_mla", "w_out",
             "ffn2_norm", "ffn2_w_gu", "ffn2_w_down"]
    W = dict(zip(names, [meta_tokens, ffn1_norm, ffn1_w_gu, ffn1_w_down, mix_norm, w_in, b_forget,
                         b_gate, fox_q_norm, fox_k_norm, mla_cq_norm, mla_w_uq, mla_ckv_norm,
                         mla_w_ukv, mla_q_norm, mla_k_norm, w_branch_fox, w_branch_mla, w_out,
                         ffn2_norm, ffn2_w_gu, ffn2_w_down]))
    Mo = dict(zip(names, [m_meta_tokens, m_ffn1_norm, m_ffn1_w_gu, m_ffn1_w_down, m_mix_norm, m_w_in,
                          m_b_forget, m_b_gate, m_fox_q_norm, m_fox_k_norm, m_mla_cq_norm,
                          m_mla_w_uq, m_mla_ckv_norm, m_mla_w_ukv, m_mla_q_norm, m_mla_k_norm,
                          m_w_branch_fox, m_w_branch_mla, m_w_out, m_ffn2_norm, m_ffn2_w_gu,
                          m_ffn2_w_down]))
    Vo = dict(zip(names, [v_meta_tokens, v_ffn1_norm, v_ffn1_w_gu, v_ffn1_w_down, v_mix_norm, v_w_in,
                          v_b_forget, v_b_gate, v_fox_q_norm, v_fox_k_norm, v_mla_cq_norm,
                          v_mla_w_uq, v_mla_ckv_norm, v_mla_w_ukv, v_mla_q_norm, v_mla_k_norm,
                          v_w_branch_fox, v_w_branch_mla, v_w_out, v_ffn2_norm, v_ffn2_w_gu,
                          v_ffn2_w_down]))

    B, S, D = x.shape
    NX = B * S
    T = NX + META_BLK
    H = HEADS
    assert NX % META_BLK == 0 and S % LANES == 0
    me = 4 * lax.axis_index("x") + 2 * lax.axis_index("y") + lax.axis_index("c")

    big = [("ffn1_w_gu", 1), ("ffn1_w_down", 0), ("w_in", 1), ("mla_w_uq", 1), ("mla_w_ukv", 1),
           ("w_branch_fox", 1), ("w_branch_mla", 1), ("w_out", 0), ("ffn2_w_gu", 1), ("ffn2_w_down", 0)]
    mix_small = ["mla_w_uq", "mla_w_ukv", "w_branch_fox", "w_branch_mla", "w_out"]
    last_group = ["ffn2_w_gu", "ffn2_w_down"]
    axis_of = dict(big)
    full = {}

    def shards(group):
        return [W[n][0].astype(BF16) for n in group]

    def assemble(group, blks):
        for n, blk in zip(group, blks):
            _, r, c = blk.shape
            full[n] = (blk.transpose(1, 0, 2).reshape(r, N_DEV * c) if axis_of[n] == 1
                       else blk.reshape(N_DEV * r, c))

    got = _exchange(shards(["ffn1_w_gu"]) + [meta_tokens], "gather_first", gather=True)
    assemble(["ffn1_w_gu"], got[:1])
    meta_full = got[1].transpose(1, 0, 2).reshape(N_META, D)

    Z_G, Z_FQ = 0, 2 * D
    Z_FK, Z_FV = Z_FQ + FOX_W, Z_FQ + 2 * FOX_W
    Z_CQ = Z_FQ + 3 * FOX_W
    Z_CKV = Z_CQ + MLA_Q_RANK
    Z_F = Z_CKV + MLA_KV_RANK
    Z_KR = Z_F + LANES

    def pad_lanes(a, w=LANES):
        return jnp.pad(a, [(0, 0)] * (a.ndim - 1) + [(0, w - a.shape[-1])])

    def rows_T(real, meta=None):
        n = real.shape[1]
        parts = [real]
        used = 0
        if meta is not None:
            parts.append(meta)
            used = meta.shape[0]
        if T - NX - used:
            parts.append(jnp.zeros((T - NX - used, n), real.dtype))
        return jnp.concatenate(parts, axis=0)

    def put_meta(tok, meta_per_seq):
        return lax.dynamic_update_slice(tok, meta_per_seq.sum(0), (NX, 0))

    h0 = rows_T(x.reshape(NX, D), meta_full)
    tgt = loss_target.reshape(NX, D)

    def ffn_fwd(h, norm, w_gu, tag, behind_up=None, behind_down=None):
        u = _norm_fwd(h, 0, D, D, norm, D, D, tag + "_norm")
        (sa, sb, a), got = _ffn_up(u, w_gu, tag + "_up",
                                  exch=_Exchange(shards(behind_up), True) if behind_up else None)
        assemble(behind_up or [], got)
        h_out = _mm(a, full[tag + "_w_down"], "nn", tag + "_down", scale=0.5, res=h,
                    exch=_Exchange(shards(behind_down), True) if behind_down else None)
        if behind_down:
            h_out, got = h_out
            assemble(behind_down, got)
        return h_out, (u, sa, sb, a)

    h1, ffn1_saved = ffn_fwd(h0, W["ffn1_norm"], full["ffn1_w_gu"], "ffn1",
                             behind_up=["ffn1_w_down"], behind_down=["w_in"])
    wi = full["w_in"]
    o_fq = 0
    o_f = 3 * FOX_W
    o_cq = o_f + HEADS
    o_kr = o_cq + MLA_Q_RANK + MLA_KV_RANK
    o_g = o_kr + MLA_ROPE
    w_in_p = jnp.concatenate([
        wi[:, o_g:o_g + 2 * D], wi[:, o_fq:o_f], wi[:, o_cq:o_kr],
        jnp.pad(wi[:, o_f:o_cq], ((0, 0), (0, LANES - HEADS))),
        jnp.pad(wi[:, o_kr:o_g], ((0, 0), (0, LANES - MLA_ROPE)))], axis=1)

    u2 = _norm_fwd(h1, 0, D, D, W["mix_norm"], D, D, "mix_norm")
    z, got = _mm(u2, w_in_p, "nn", "w_in", exch=_Exchange(shards(mix_small), True))
    assemble(mix_small, got)
    w_uq_p = jnp.pad(full["mla_w_uq"].reshape(MLA_Q_RANK, H, MLA_QK),
                     ((0, 0), (0, 0), (0, LANES - MLA_QK))).reshape(MLA_Q_RANK, H * LANES)

    gq_f = jnp.tile(W["fox_q_norm"], (1, 2))
    gk_f = jnp.tile(W["fox_k_norm"], (1, 2))
    fqn = _norm_fwd(z, Z_FQ, FOX_W, LANES, gq_f, FOX_HD, FOX_HD, "fox_q_norm")
    fkn = _norm_fwd(z, Z_FK, FOX_W, LANES, gk_f, FOX_HD, FOX_HD, "fox_k_norm")
    fl = z[:NX, Z_F:Z_F + LANES].reshape(B, S, LANES)
    flm = z[NX:, Z_F:Z_F + LANES]
    bf = pad_lanes(W["b_forget"])
    cum, cumm = _cum_fwd(fl, flm, bf, "forget_cum")
    TK = min(512, S)
    ck = cum[:, :, :H].transpose(0, 2, 1).reshape(B, H, S // TK, 1, TK)
    cmk = jnp.broadcast_to(cumm[:, :H].T[None, :, None, :], (B, H, 1, META_BLK))
    (o_fox, lse_fox), got = _attn_fwd(fqn, fkn, z, Z_FV, False, FOX_HD ** -0.5, S, NX, "fox_attn", ck, cmk,
                                      exch=_Exchange(shards(last_group), True))
    assemble(last_group, got)
    of = _mm(o_fox, full["w_branch_fox"], "nn", "branch_fox")

    pos = jnp.concatenate([jnp.tile(jnp.arange(S) + N_META, B), jnp.arange(META_BLK)])
    tabs = _rope_tables(pos)
    cqn = _norm_fwd(z, Z_CQ, MLA_Q_RANK, MLA_Q_RANK, W["mla_cq_norm"], MLA_Q_RANK, MLA_Q_RANK, "mla_cq_norm")
    q_lin = _mm(cqn, w_uq_p, "nn", "mla_uq")
    ckvn = _norm_fwd(z, Z_CKV, MLA_KV_RANK, MLA_KV_RANK, W["mla_ckv_norm"], MLA_KV_RANK, MLA_KV_RANK,
                     "mla_ckv_norm")
    kv_lin = _mm(ckvn, full["mla_w_ukv"], "nn", "mla_ukv")
    gq_m, gk_m = pad_lanes(W["mla_q_norm"]), pad_lanes(W["mla_k_norm"])
    mqn = _norm_fwd(q_lin, 0, H * LANES, LANES, gq_m, LANES, MLA_QK, "mla_q_norm", tabs=tabs)
    mkn = _mla_k_fwd(kv_lin, z, Z_KR, gk_m, tabs, "mla_k_norm")
    (o_mla, lse_mla), _ = _attn_fwd(mqn, mkn, kv_lin, 0, True, MLA_QK ** -0.5, S, NX, "mla_attn")
    om = _mm(o_mla, full["w_branch_mla"], "nn", "branch_mla")

    mix = _gate_fwd(z, W["b_gate"], of, om, "gate_mix")
    h2 = _mm(mix, full["w_out"], "nn", "w_out", res=h1)

    h3, ffn2_saved = ffn_fwd(h2, W["ffn2_norm"], full["ffn2_w_gu"], "ffn2")

    dh3, dh3_b, loss_acc = _loss(h3, tgt, "loss")
    loss = lax.psum(loss_acc[0, 0], AXES)

    G = {}
    parts = {}

    def scatter_of(group):
        per_dest = []
        for n in group:
            r, c = W[n].shape[1:]
            per_dest.append((G[n].reshape(r, N_DEV, c).transpose(1, 0, 2) if axis_of[n] == 1
                             else G[n].reshape(N_DEV, r, c)).astype(BF16))
        return _Exchange(per_dest, False)

    def ffn_bwd(dh, dh_b, h, norm, w_gu, w_down, saved, tag, behind_down=None, spread=False):
        u, sa, sb, a = saved
        G[tag + "_w_down"] = _mm(a, dh_b, "tn", tag + "_dw_down", scale=0.5)
        (dg, dup), got = _ffn_down_bwd(dh_b, w_down, sa, sb, tag + "_down_bwd",
                                       exch=scatter_of(behind_down) if behind_down else None)
        parts.update(zip(behind_down or [], got))
        dw_g = _mm(u, dg, "tn", tag + "_dw_g", exch=scatter_of([tag + "_w_down"]) if spread else None)
        if spread:
            dw_g, got = dw_g
            parts[tag + "_w_down"] = got[0]
        G[tag + "_w_gu"] = jnp.concatenate([dw_g, _mm(u, dup, "tn", tag + "_dw_u")], axis=1)
        (du,), got = _ffn_up_bwd_dx(dg, dup, w_gu, tag + "_up_bwd",
                                    exch=scatter_of([tag + "_w_gu"]) if spread else None)
        if spread:
            parts[tag + "_w_gu"] = got[0]
        *dh_in, G[tag + "_norm"] = _norm_bwd(h, 0, D, D, norm, D, D, du, tag + "_norm_bwd", res=dh,
                                             bf16_copy=not spread)
        return dh_in

    dh2, dh2_b = ffn_bwd(dh3, dh3_b, h2, W["ffn2_norm"], full["ffn2_w_gu"], full["ffn2_w_down"], ffn2_saved,
                         "ffn2")

    G["w_out"] = _mm(mix, dh2_b, "tn", "dw_out")
    dmix = _mm(dh2_b, full["w_out"], "nt", "w_out_bwd")
    dgl, dof, dom, G["b_gate"] = _gate_bwd(dmix, z, W["b_gate"], of, om, "gate_bwd")

    G["w_branch_fox"] = _mm(o_fox, dof, "tn", "dw_branch_fox")
    do_fox = _mm(dof, full["w_branch_fox"], "nt", "branch_fox_bwd")
    (dq_f, dk_f, dv_f, dkm_f, dvm_f, dck, dcmk, dcq), got = _attn_bwd(
        fqn, fkn, z, Z_FV, o_fox, lse_fox, do_fox, False, FOX_HD ** -0.5, S, NX, "fox_attn_bwd", ck, cmk,
        exch=scatter_of(last_group))
    parts.update(zip(last_group, got))
    dk_f, dv_f = put_meta(dk_f, dkm_f), put_meta(dv_f, dvm_f)
    dfq, gq = _norm_bwd(z, Z_FQ, FOX_W, LANES, gq_f, FOX_HD, FOX_HD, dq_f, "fox_q_norm_bwd", out_dtype=BF16)
    dfk, gk = _norm_bwd(z, Z_FK, FOX_W, LANES, gk_f, FOX_HD, FOX_HD, dk_f, "fox_k_norm_bwd", out_dtype=BF16)
    G["fox_q_norm"] = gq[:, :FOX_HD] + gq[:, FOX_HD:]
    G["fox_k_norm"] = gk[:, :FOX_HD] + gk[:, FOX_HD:]
    dc = pad_lanes(dck.reshape(B, H, S).transpose(0, 2, 1)
                   + dcq.transpose(0, 2, 1, 3).reshape(B, S, H))
    dcm = pad_lanes(dcmk.sum(0)[:, 0, :].T)
    dcm = jnp.where(jnp.arange(LANES)[:, None] < N_META, dcm, 0.0)
    dfl, dflm, dbf = _cum_bwd(dc, dcm, fl, flm, bf, "forget_cum_bwd")
    G["b_forget"] = dbf[:, :HEADS]
    dfl_t = rows_T(dfl.reshape(NX, LANES), dflm)

    G["w_branch_mla"] = _mm(o_mla, dom, "tn", "dw_branch_mla")
    do_mla = _mm(dom, full["w_branch_mla"], "nt", "branch_mla_bwd")
    (dq_m, dk_m, dvk, dkm_m, dvkm), _ = _attn_bwd(
        mqn, mkn, kv_lin, 0, o_mla, lse_mla, do_mla, True, MLA_QK ** -0.5, S, NX, "mla_attn_bwd")
    dk_m, dvk = put_meta(dk_m, dkm_m), put_meta(dvk, dvkm)
    dq_lin, gq = _norm_bwd(q_lin, 0, H * LANES, LANES, gq_m, LANES, MLA_QK, dq_m, "mla_q_norm_bwd", tabs=tabs,
                           out_dtype=BF16)
    G["mla_q_norm"] = gq[:, :MLA_QK]
    G["mla_w_uq"] = _mm(cqn, dq_lin, "tn", "dw_uq").reshape(MLA_Q_RANK, H, LANES)[:, :, :MLA_QK].reshape(
        MLA_Q_RANK, H * MLA_QK)
    dcqn = _mm(dq_lin, w_uq_p, "nt", "mla_uq_bwd")
    dcq, G["mla_cq_norm"] = _norm_bwd(z, Z_CQ, MLA_Q_RANK, MLA_Q_RANK, W["mla_cq_norm"], MLA_Q_RANK,
                                      MLA_Q_RANK, dcqn, "mla_cq_norm_bwd", out_dtype=BF16)
    dkv_lin, dkr, gk = _mla_k_bwd(kv_lin, z, Z_KR, gk_m, tabs, dk_m, dvk, "mla_k_norm_bwd")
    G["mla_k_norm"] = gk[:, :MLA_QK]
    G["mla_w_ukv"] = _mm(ckvn, dkv_lin, "tn", "dw_ukv")
    dckvn = _mm(dkv_lin, full["mla_w_ukv"], "nt", "mla_ukv_bwd")
    dckv, G["mla_ckv_norm"] = _norm_bwd(z, Z_CKV, MLA_KV_RANK, MLA_KV_RANK, W["mla_ckv_norm"], MLA_KV_RANK,
                                        MLA_KV_RANK, dckvn, "mla_ckv_norm_bwd", out_dtype=BF16)

    dz = jnp.concatenate([dgl, dfq, dfk, dv_f.astype(BF16), dcq, dckv, dfl_t.astype(BF16),
                          dkr.astype(BF16)], axis=1)
    dw_in_p = _mm(u2, dz, "tn", "dw_in")
    G["w_in"] = jnp.concatenate([
        dw_in_p[:, Z_FQ:Z_CQ], dw_in_p[:, Z_F:Z_F + HEADS], dw_in_p[:, Z_CQ:Z_F],
        dw_in_p[:, Z_KR:Z_KR + MLA_ROPE], dw_in_p[:, Z_G:Z_G + 2 * D]], axis=1)
    du2, got = _mm(dz, w_in_p, "nt", "w_in_bwd", exch=scatter_of(mix_small))
    parts.update(zip(mix_small, got))
    dh1, dh1_b, G["mix_norm"] = _norm_bwd(h1, 0, D, D, W["mix_norm"], D, D, du2, "mix_norm_bwd", res=dh2,
                                          bf16_copy=True)

    (dh0,) = ffn_bwd(dh1, dh1_b, h0, W["ffn1_norm"], full["ffn1_w_gu"], full["ffn1_w_down"], ffn1_saved, "ffn1",
                     behind_down=["w_in"], spread=True)
    grad_x = dh0[:NX].reshape(B, S, D)
    G["meta_tokens"] = dh0[NX:NX + N_META]

    res = {}
    for n, _ in big:
        outs4 = _adamw(parts[n], W[n][0], Mo[n][0], Vo[n][0], "adamw_" + n)
        for key, arr in zip(("g", "d", "m", "v"), outs4):
            res[key, n] = arr[None]

    small = [n for n in names if n not in dict(big) and n != "meta_tokens"]
    small_shapes = [W[n].shape for n in small]
    spack = _pack([G["meta_tokens"]] + [G[n] for n in small], 1024, 8)
    (sparts,) = _exchange([spack], "gather_small_grads", gather=True)
    sflat = sparts.reshape(N_DEV, -1)
    dsh = D // N_DEV
    meta_part = lax.dynamic_slice(sflat[:, :N_META * D].reshape(N_DEV, N_META, D),
                                  (0, 0, me * dsh), (N_DEV, N_META, dsh)).reshape(N_DEV, -1)
    rep_len = sum(int(np.prod(s)) for s in small_shapes)
    rep_part = sflat[:, N_META * D:N_META * D + rep_len]
    sp = _pack_rows([meta_part, rep_part], LANES, _chunk_rows(LANES))
    pks = lambda src: _pack([src["meta_tokens"]] + [src[n] for n in small], LANES, _chunk_rows(LANES))
    g_s, d_s, m_s, v_s = _adamw(sp, pks(W), pks(Mo), pks(Vo), "adamw_small")
    shapes_s = [W["meta_tokens"].shape] + small_shapes
    for key, packed in (("g", g_s), ("d", d_s), ("m", m_s), ("v", v_s)):
        for n, arr in zip(["meta_tokens"] + small, _unpack(packed, shapes_s)):
            res[key, n] = arr

    outs = [loss, grad_x]
    for key in ("g", "d", "m", "v"):
        outs += [res[key, n] for n in names]
    return tuple(outs)
```

```python
import numpy as np
import jax
import jax.numpy as jnp
from jax import lax
from jax.experimental import pallas as pl
from jax.experimental.pallas import tpu as pltpu

F32 = jnp.float32
BF16 = jnp.bfloat16

N_META = 16
EPS = 1e-6
HEADS = 8
FOX_HD = 64
FOX_W = HEADS * FOX_HD
MLA_Q_RANK = 256
MLA_KV_RANK = 128
MLA_NOPE = 64
MLA_ROPE = 32
MLA_QK = MLA_NOPE + MLA_ROPE
MLA_V = 64
ROPE_THETA = 10000.0
LANES = 128
HALF = LANES // 2
META_BLK = 128
NEG = -1e30

ADAM_LR = 0.001
ADAM_B1 = 0.9
ADAM_B2 = 0.999
ADAM_EPS = 1e-08
ADAM_WD = 0.01
ADAM_STEP = 10

N_DEV = 8
AXES = ("x", "y", "c")
VMEM_LIMIT_BYTES = 56 * 1024 * 1024


def _tile(n, cap, mult):
    best = None
    for d in range(mult, min(n, cap) + 1, mult):
        if n % d == 0:
            best = d
    return n if best is None else best


VREG_ELEMS = 8 * LANES


def _row_tile(rows, width):
    return _tile(rows, max(16, (1 << 19) // width), 16)


def _chunk_rows(width):
    rows = 16
    while 2 * rows * width <= 8 * VREG_ELEMS:
        rows *= 2
    return rows


def _by_chunks(rows, width, step, init=()):
    return step(pl.ds(0, rows), init)


def _params(sem=None):
    return pltpu.CompilerParams(dimension_semantics=sem, vmem_limit_bytes=VMEM_LIMIT_BYTES)


def _mm(a, b, mode, name, out_dtype=F32, scale=1.0, res=None, exch=None):
    if mode == "nn":
        (M, K), (K2, N) = a.shape, b.shape
    elif mode == "nt":
        (M, K), (N, K2) = a.shape, b.shape
    else:
        (K, M), (K2, N) = a.shape, b.shape
    assert K == K2, (a.shape, b.shape, mode)
    if mode == "tn":
        tm, tk = _tile(M, 1408, 128), _tile(K, 2080, 16)
    else:
        tm, tk = _tile(M, 832, 16), _tile(K, 4224, 128)
    tn = _tile(N, 1408, 128)
    nk = K // tk
    ni, nj = M // tm, N // tn
    bytes_a, bytes_b = a.size * a.dtype.itemsize, b.size * b.dtype.itemsize
    j_outer = nk == 1 and bytes_a * nj + bytes_b < bytes_a + bytes_b * ni
    ij = (lambda g0, g1: (g1, g0)) if j_outer else (lambda g0, g1: (g0, g1))

    def spec(shape, at):
        return pl.BlockSpec(shape, lambda g0, g1, k: at(*ij(g0, g1), k))

    a_spec = {"nn": spec((tm, tk), lambda i, j, k: (i, k)),
              "nt": spec((tm, tk), lambda i, j, k: (i, k)),
              "tn": spec((tk, tm), lambda i, j, k: (k, i))}[mode]
    b_spec = {"nn": spec((tk, tn), lambda i, j, k: (k, j)),
              "nt": spec((tn, tk), lambda i, j, k: (j, k)),
              "tn": spec((tk, tn), lambda i, j, k: (k, j))}[mode]
    dims = {"nn": (((1,), (0,)), ((), ())), "nt": (((1,), (1,)), ((), ())),
            "tn": (((0,), (0,)), ((), ()))}[mode]
    o_spec = spec((tm, tn), lambda i, j, k: (i, j))
    has_res = res is not None

    def body(*refs):
        a_ref, b_ref = refs[:2]
        r_ref = refs[2] if has_res else None
        o_ref = refs[2 + has_res]

        def finish(acc):
            o = acc * scale
            if has_res:
                o = o + r_ref[...]
            o_ref[...] = o.astype(out_dtype)

        prod = lax.dot_general(a_ref[...].astype(BF16), b_ref[...].astype(BF16), dims,
                               preferred_element_type=F32)
        if nk == 1:
            finish(prod)
            return
        acc_ref = refs[3 + has_res]
        k = pl.program_id(2)

        @pl.when(k == 0)
        def _():
            acc_ref[...] = prod

        @pl.when((k > 0) & (k < nk - 1))
        def _():
            acc_ref[...] += prod

        @pl.when(k == nk - 1)
        def _():
            finish(acc_ref[...] + prod)

    ins = [a, b] + ([res] if has_res else [])
    specs = [a_spec, b_spec] + ([o_spec] if has_res else [])
    (out,), got = _call(
        body, name, (nj, ni, nk) if j_outer else (ni, nj, nk), specs, [o_spec],
        [jax.ShapeDtypeStruct((M, N), out_dtype)],
        ins, scratch_shapes=[pltpu.VMEM((tm, tn), F32)] if nk > 1 else [],
        sem=("parallel", "parallel", "arbitrary"), exch=exch)
    return out if exch is None else (out, got)


def _rope_fwd(y, c, s1, s2):
    return y * c + pltpu.roll(y, LANES - 16, 1) * s1 + pltpu.roll(y, 16, 1) * s2


def _rope_bwd(dy, c, s1, s2):
    return dy * c + pltpu.roll(dy * s1, 16, 1) + pltpu.roll(dy * s2, LANES - 16, 1)


def _group_sum(v, seg):
    if seg == v.shape[-1]:
        return jnp.sum(v, axis=-1, keepdims=True)
    lo = lax.broadcasted_iota(jnp.int32, v.shape, 1) < seg
    s_lo = jnp.sum(jnp.where(lo, v, 0.0), axis=-1, keepdims=True)
    s_hi = jnp.sum(jnp.where(lo, 0.0, v), axis=-1, keepdims=True)
    return jnp.where(lo, s_lo, s_hi)


def _norm_fwd(src, col0, width, bw, gain, seg, d_true, name, tabs=None, out_dtype=BF16):
    T = src.shape[0]
    tr = _row_tile(T, bw)
    inv_d = 1.0 / d_true
    c0 = col0 // bw
    assert col0 % bw == 0 and width % bw == 0

    def body(*refs):
        if tabs is None:
            x_ref, g_ref, o_ref = refs
        else:
            x_ref, g_ref, c_ref, s1_ref, s2_ref, o_ref = refs
        gain_v = g_ref[...]

        def step(rows, carry):
            xv = x_ref[rows, :]
            r = lax.rsqrt(_group_sum(xv * xv, seg) * inv_d + EPS)
            y = xv * r * gain_v
            if tabs is not None:
                y = _rope_fwd(y, c_ref[rows, :], s1_ref[rows, :], s2_ref[rows, :])
            o_ref[rows, :] = y.astype(out_dtype)
            return carry

        _by_chunks(tr, bw, step)

    specs = [pl.BlockSpec((tr, bw), lambda i, j: (i, c0 + j)), pl.BlockSpec((1, bw), lambda i, j: (0, 0))]
    ins = [src, gain]
    if tabs is not None:
        tab = pl.BlockSpec((tr, LANES), lambda i, j: (i, 0))
        specs += [tab, tab, tab]
        ins += list(tabs)
    return pl.pallas_call(
        body, name=name, grid=(T // tr, width // bw), in_specs=specs,
        out_specs=pl.BlockSpec((tr, bw), lambda i, j: (i, j)),
        out_shape=jax.ShapeDtypeStruct((T, width), out_dtype),
        compiler_params=_params(("parallel", "parallel")),
    )(*ins)


def _norm_bwd_math(xv, gain, dyv, seg, inv_d):
    r = lax.rsqrt(_group_sum(xv * xv, seg) * inv_d + EPS)
    gy = dyv * gain
    dot = _group_sum(gy * xv, seg)
    dx = r * gy - xv * (r * r * r * inv_d) * dot
    return dx, jnp.sum(dyv * xv * r, axis=0, keepdims=True)


def _norm_bwd(src, col0, width, bw, gain, seg, d_true, dy, name, tabs=None, res=None, out_dtype=F32,
              bf16_copy=False):
    T = src.shape[0]
    tr = _row_tile(T, bw)
    inv_d = 1.0 / d_true
    c0 = col0 // bw
    has_res = res is not None

    def body(*refs):
        refs = list(refs)
        x_ref, g_ref, dy_ref = refs[:3]
        pos = 3
        if tabs is not None:
            c_ref, s1_ref, s2_ref = refs[3:6]
            pos = 6
        if has_res:
            r_ref = refs[pos]
            pos += 1
        dx_ref = refs[pos]
        dxb_ref = refs[pos + 1] if bf16_copy else None
        dg_ref = refs[pos + 1 + bf16_copy]
        gain_v = g_ref[...]

        def step(rows, dg_sum):
            dyv = dy_ref[rows, :].astype(F32)
            if tabs is not None:
                dyv = _rope_bwd(dyv, c_ref[rows, :], s1_ref[rows, :], s2_ref[rows, :])
            dx, dg = _norm_bwd_math(x_ref[rows, :], gain_v, dyv, seg, inv_d)
            if has_res:
                dx = dx + r_ref[rows, :]
            dx_ref[rows, :] = dx.astype(out_dtype)
            if bf16_copy:
                dxb_ref[rows, :] = dx.astype(BF16)
            return dg_sum + dg

        dg = _by_chunks(tr, bw, step, jnp.zeros((1, bw), F32))

        @pl.when((pl.program_id(0) == 0) & (pl.program_id(1) == 0))
        def _():
            dg_ref[...] = jnp.zeros_like(dg_ref)

        dg_ref[...] += dg

    blk = pl.BlockSpec((tr, bw), lambda i, j: (i, j))
    one = pl.BlockSpec((1, bw), lambda i, j: (0, 0))
    specs = [pl.BlockSpec((tr, bw), lambda i, j: (i, c0 + j)), one, blk]
    ins = [src, gain, dy]
    if tabs is not None:
        tab = pl.BlockSpec((tr, LANES), lambda i, j: (i, 0))
        specs += [tab, tab, tab]
        ins += list(tabs)
    if has_res:
        specs.append(blk)
        ins.append(res)
    extra = bf16_copy * [blk]
    extra_shape = bf16_copy * [jax.ShapeDtypeStruct((T, width), BF16)]
    return pl.pallas_call(
        body, name=name, grid=(T // tr, width // bw), in_specs=specs, out_specs=(blk, *extra, one),
        out_shape=(jax.ShapeDtypeStruct((T, width), out_dtype), *extra_shape,
                   jax.ShapeDtypeStruct((1, bw), F32)),
        compiler_params=_params(("arbitrary", "arbitrary")),
    )(*ins)


def _mla_k_raw(kv, kr):
    lane = lax.broadcasted_iota(jnp.int32, kv.shape, 1)
    return jnp.where(lane < MLA_NOPE, kv, jnp.where(lane < MLA_QK, pltpu.roll(kr, MLA_NOPE, 1), 0.0))


def _mla_k_fwd(kv_lin, z, kr_col, gain, tabs, name):
    T, W = kv_lin.shape
    tr = _row_tile(T, LANES)
    krb = kr_col // LANES
    inv_d = 1.0 / MLA_QK

    def body(kv_ref, kr_ref, g_ref, c_ref, s1_ref, s2_ref, o_ref):
        gain_v = g_ref[...]

        def step(rows, carry):
            xv = _mla_k_raw(kv_ref[rows, :], kr_ref[rows, :])
            r = lax.rsqrt(jnp.sum(xv * xv, axis=-1, keepdims=True) * inv_d + EPS)
            o_ref[rows, :] = _rope_fwd(xv * r * gain_v, c_ref[rows, :], s1_ref[rows, :],
                                       s2_ref[rows, :]).astype(BF16)
            return carry

        _by_chunks(tr, LANES, step)

    blk = pl.BlockSpec((tr, LANES), lambda i, h: (i, h))
    tab = pl.BlockSpec((tr, LANES), lambda i, h: (i, 0))
    return pl.pallas_call(
        body, name=name, grid=(T // tr, W // LANES),
        in_specs=[blk, pl.BlockSpec((tr, LANES), lambda i, h: (i, krb)),
                  pl.BlockSpec((1, LANES), lambda i, h: (0, 0)), tab, tab, tab],
        out_specs=blk, out_shape=jax.ShapeDtypeStruct((T, W), BF16),
        compiler_params=_params(("parallel", "parallel")),
    )(kv_lin, z, gain, *tabs)


def _mla_k_bwd(kv_lin, z, kr_col, gain, tabs, dk, dvk, name):
    T, W = kv_lin.shape
    tr = _row_tile(T, LANES)
    krb = kr_col // LANES
    inv_d = 1.0 / MLA_QK

    def body(kv_ref, kr_ref, g_ref, c_ref, s1_ref, s2_ref, dk_ref, dvk_ref, dkv_ref, dkr_ref, dg_ref):
        h = pl.program_id(1)
        gain_v = g_ref[...]

        @pl.when(h == 0)
        def _():
            dkr_ref[...] = jnp.zeros_like(dkr_ref)

        def step(rows, dg_sum):
            xv = _mla_k_raw(kv_ref[rows, :], kr_ref[rows, :])
            dyv = _rope_bwd(dk_ref[rows, :], c_ref[rows, :], s1_ref[rows, :], s2_ref[rows, :])
            dx, dg = _norm_bwd_math(xv, gain_v, dyv, LANES, inv_d)
            lane = lax.broadcasted_iota(jnp.int32, dx.shape, 1)
            dkv_ref[rows, :] = jnp.where(lane < MLA_NOPE, dx, dvk_ref[rows, :]).astype(BF16)
            dkr_ref[rows, :] += pltpu.roll(jnp.where((lane >= MLA_NOPE) & (lane < MLA_QK), dx, 0.0),
                                           LANES - MLA_NOPE, 1)
            return dg_sum + dg

        dg = _by_chunks(tr, LANES, step, jnp.zeros((1, LANES), F32))

        @pl.when((pl.program_id(0) == 0) & (h == 0))
        def _():
            dg_ref[...] = jnp.zeros_like(dg_ref)

        dg_ref[...] += dg

    blk = pl.BlockSpec((tr, LANES), lambda i, h: (i, h))
    tab = pl.BlockSpec((tr, LANES), lambda i, h: (i, 0))
    one = pl.BlockSpec((1, LANES), lambda i, h: (0, 0))
    return pl.pallas_call(
        body, name=name, grid=(T // tr, W // LANES),
        in_specs=[blk, pl.BlockSpec((tr, LANES), lambda i, h: (i, krb)), one, tab, tab, tab, blk, blk],
        out_specs=(blk, tab, one),
        out_shape=(jax.ShapeDtypeStruct((T, W), BF16), jax.ShapeDtypeStruct((T, LANES), F32),
                   jax.ShapeDtypeStruct((1, LANES), F32)),
        compiler_params=_params(("arbitrary", "arbitrary")),
    )(kv_lin, z, gain, *tabs, dk, dvk)


def _ffn_up(u, w_gu, name, exch=None):
    T, D = u.shape
    F = w_gu.shape[1] // 2
    tm, tn = _tile(T, 832, 16), _tile(F, 1408, 128)
    nj = F // tn

    def body(u_ref, wg_ref, wu_ref, sa_ref, sb_ref, a_ref):
        uv = u_ref[...]
        g = jnp.dot(uv, wg_ref[...], preferred_element_type=F32)
        up = jnp.dot(uv, wu_ref[...], preferred_element_type=F32)
        sg = jax.nn.sigmoid(g)
        silu = g * sg
        sa_ref[...] = silu.astype(BF16)
        sb_ref[...] = (up * (sg + silu * (1.0 - sg))).astype(BF16)
        a_ref[...] = (silu * up).astype(BF16)

    o_spec = pl.BlockSpec((tm, tn), lambda j, i: (i, j))
    sh = jax.ShapeDtypeStruct((T, F), BF16)
    return _call(
        body, name, (nj, T // tm),
        [pl.BlockSpec((tm, D), lambda j, i: (i, 0)),
         pl.BlockSpec((D, tn), lambda j, i: (0, j)),
         pl.BlockSpec((D, tn), lambda j, i: (0, j + nj))],
        [o_spec, o_spec, o_spec], [sh, sh, sh], [u, w_gu, w_gu],
        sem=("parallel", "parallel"), exch=exch)


def _ffn_down_bwd(dh, w_down, sa, sb, name, exch=None):
    T, D = dh.shape
    F = w_down.shape[0]
    tm, tn = _tile(T, 832, 16), _tile(F, 1408, 128)

    def body(dh_ref, w_ref, sa_ref, sb_ref, dg_ref, dup_ref):
        da = 0.5 * lax.dot_general(dh_ref[...].astype(BF16), w_ref[...], (((1,), (1,)), ((), ())),
                                   preferred_element_type=F32)
        dup_ref[...] = (da * sa_ref[...].astype(F32)).astype(BF16)
        dg_ref[...] = (da * sb_ref[...].astype(F32)).astype(BF16)

    t_spec = pl.BlockSpec((tm, tn), lambda j, i: (i, j))
    sh = jax.ShapeDtypeStruct((T, F), BF16)
    return _call(
        body, name, (F // tn, T // tm),
        [pl.BlockSpec((tm, D), lambda j, i: (i, 0)),
         pl.BlockSpec((tn, D), lambda j, i: (j, 0)), t_spec, t_spec],
        [t_spec, t_spec], [sh, sh], [dh, w_down, sa, sb],
        sem=("parallel", "parallel"), exch=exch)


def _ffn_up_bwd_dx(dg, dup, w_gu, name, exch=None):
    T, F = dg.shape
    D = w_gu.shape[0]
    tm, tk = _tile(T, 832, 16), _tile(F, 2816, 128)
    nk = F // tk
    nt = (((1,), (1,)), ((), ()))

    def body(dg_ref, dup_ref, wg_ref, wu_ref, o_ref, acc_ref):
        k = pl.program_id(1)
        prod = (lax.dot_general(dg_ref[...], wg_ref[...], nt, preferred_element_type=F32)
                + lax.dot_general(dup_ref[...], wu_ref[...], nt, preferred_element_type=F32))
        if nk == 1:
            o_ref[...] = prod
            return

        @pl.when(k == 0)
        def _():
            acc_ref[...] = prod

        @pl.when((k > 0) & (k < nk - 1))
        def _():
            acc_ref[...] += prod

        @pl.when(k == nk - 1)
        def _():
            o_ref[...] = acc_ref[...] + prod

    return _call(
        body, name, (T // tm, nk),
        [pl.BlockSpec((tm, tk), lambda i, k: (i, k)),
         pl.BlockSpec((tm, tk), lambda i, k: (i, k)),
         pl.BlockSpec((D, tk), lambda i, k: (0, k)),
         pl.BlockSpec((D, tk), lambda i, k: (0, k + nk))],
        [pl.BlockSpec((tm, D), lambda i, k: (i, 0))], [jax.ShapeDtypeStruct((T, D), F32)],
        [dg, dup, w_gu, w_gu], scratch_shapes=[pltpu.VMEM((tm, D), F32)],
        sem=("parallel", "arbitrary"), exch=exch)


def _logsig(x):
    return jnp.minimum(x, 0.0) - jnp.log(1.0 + jnp.exp(-jnp.abs(x)))


def _cum_fwd(fl, flm, bf, name):
    B, S, _ = fl.shape
    nb = S // LANES

    def body(fl_ref, flm_ref, bf_ref, cum_ref, cumm_ref):
        rows = lax.broadcasted_iota(jnp.int32, (LANES, LANES), 0)
        cols = lax.broadcasted_iota(jnp.int32, (LANES, LANES), 1)
        tri = (rows >= cols).astype(F32)
        bias = bf_ref[...]
        lfm = jnp.where(rows < N_META, _logsig(flm_ref[...] + bias), 0.0)
        cm = jnp.dot(tri, lfm, precision=lax.Precision.HIGHEST, preferred_element_type=F32)
        cumm_ref[...] = cm * LOG2E
        base = cm[LANES - 1:LANES, :]
        for b in range(B):
            def blk(i, carry):
                r0 = pl.multiple_of(i * LANES, LANES)
                lf = _logsig(fl_ref[b, pl.ds(r0, LANES), :] + bias)
                c = jnp.dot(tri, lf, precision=lax.Precision.HIGHEST,
                            preferred_element_type=F32) + carry
                cum_ref[b, pl.ds(r0, LANES), :] = c * LOG2E
                return c[LANES - 1:LANES, :]

            lax.fori_loop(0, nb, blk, base)

    return pl.pallas_call(
        body, name=name,
        out_shape=(jax.ShapeDtypeStruct((B, S, LANES), F32),
                   jax.ShapeDtypeStruct((LANES, LANES), F32)),
        compiler_params=_params(),
    )(fl, flm, bf)


def _cum_bwd(dc, dcm, fl, flm, bf, name):
    B, S, _ = fl.shape
    nb = S // LANES

    def body(dc_ref, dcm_ref, fl_ref, flm_ref, bf_ref, dfl_ref, dflm_ref, dbf_ref):
        rows = lax.broadcasted_iota(jnp.int32, (LANES, LANES), 0)
        cols = lax.broadcasted_iota(jnp.int32, (LANES, LANES), 1)
        triu = (rows <= cols).astype(F32)
        bias = bf_ref[...]
        total = jnp.zeros((1, LANES), F32)
        dbf = jnp.zeros((1, LANES), F32)
        for b in range(B):
            tail = jnp.zeros((1, LANES), F32)
            for t in range(nb):
                r0 = (nb - 1 - t) * LANES
                rc = jnp.dot(triu, dc_ref[b, r0:r0 + LANES, :], precision=lax.Precision.HIGHEST,
                             preferred_element_type=F32) + tail
                xv = fl_ref[b, r0:r0 + LANES, :] + bias
                d = rc / (1.0 + jnp.exp(xv))
                dfl_ref[b, r0:r0 + LANES, :] = d
                tail = rc[0:1, :]
                dbf = dbf + jnp.sum(d, axis=0, keepdims=True)
            total = total + tail
        rcm = jnp.dot(triu, dcm_ref[...], precision=lax.Precision.HIGHEST,
                      preferred_element_type=F32) + total
        dm = jnp.where(rows < N_META, rcm / (1.0 + jnp.exp(flm_ref[...] + bias)), 0.0)
        dflm_ref[...] = dm
        dbf_ref[...] = dbf + jnp.sum(dm, axis=0, keepdims=True)

    return pl.pallas_call(
        body, name=name,
        out_shape=(jax.ShapeDtypeStruct((B, S, LANES), F32),
                   jax.ShapeDtypeStruct((LANES, LANES), F32),
                   jax.ShapeDtypeStruct((1, LANES), F32)),
        compiler_params=_params(),
    )(dc, dcm, fl, flm, bf)


_NT = (((1,), (1,)), ((), ()))


def _token_rows_buffer(T, NX, width, dtype=F32):
    return lax.dynamic_update_slice(lax.empty((T, width), dtype), jnp.zeros((T - NX, width), dtype), (NX, 0))


def _attn_specs(S, NX, qw, v_col0):
    mb = NX // META_BLK
    vb = v_col0 // qw
    return (pl.BlockSpec((S, qw), lambda b, p: (b, p)),
            pl.BlockSpec((META_BLK, qw), lambda b, p: (mb, p)),
            pl.BlockSpec((S, qw), lambda b, p: (b, vb + p)),
            pl.BlockSpec((META_BLK, qw), lambda b, p: (mb, vb + p)),
            pl.BlockSpec((S, LANES), lambda b, p: (b, p)))


def _cum_specs(S, TK):
    return [pl.BlockSpec((1, 2, S // TK, 1, TK), lambda b, p: (b, p, 0, 0, 0)),
            pl.BlockSpec((1, 2, 1, META_BLK), lambda b, p: (b, p, 0, 0))]


LOG2E = 1.4426950408889634


def _attn_fwd(qn, kn, vsrc, v_col0, mla, scale, S, NX, name, ck=None, cmk=None, exch=None):
    T = qn.shape[0]
    B = NX // S
    qw = 2 * LANES if mla else LANES
    npair = qn.shape[1] // qw
    TQ = min(512, S)
    TK = TQ
    forget = ck is not None
    a = scale * LOG2E

    def body(*refs):
        if forget:
            q_ref, k_ref, km_ref, v_ref, vm_ref, ck_ref, cmk_ref, _, o_ref, lse_ref = refs
        else:
            q_ref, k_ref, km_ref, v_ref, vm_ref, _, o_ref, lse_ref = refs
        lo = lax.broadcasted_iota(jnp.int32, (1, LANES), 1) < HALF
        mcol = lax.broadcasted_iota(jnp.int32, (TQ, META_BLK), 1)
        causal = (lax.broadcasted_iota(jnp.int32, (TQ, TK), 0)
                  >= lax.broadcasted_iota(jnp.int32, (TQ, TK), 1))
        two = lax.broadcasted_iota(jnp.int32, (TQ, 2), 1)
        for qi in range(S // TQ):
            q0 = qi * TQ
            sls = [slice(e * LANES, (e + 1) * LANES) if mla else slice(None) for e in range(2)]
            if mla:
                qts = [q_ref[q0:q0 + TQ, sl] for sl in sls]
            else:
                qts = [jnp.where(lo if e == 0 else ~lo, q_ref[q0:q0 + TQ, :], 0.0).astype(BF16)
                       for e in range(2)]

            def step(e, kt, vt, c2, mask, carry):
                m, l, acc = carry
                s = lax.dot_general(qts[e], kt, _NT, preferred_element_type=F32) * a
                if forget:
                    s = s - c2
                if mask is not None:
                    s = jnp.where(mask, s, NEG)
                m2 = jnp.max(s, axis=1, keepdims=True)
                if m is not None:
                    m2 = jnp.maximum(m, m2)
                p = jnp.exp2(s - m2)
                l2 = jnp.sum(p, axis=1, keepdims=True)
                acc2 = jnp.dot(p.astype(BF16), vt.astype(BF16), preferred_element_type=F32)
                if m is not None:
                    alpha = jnp.exp2(m - m2)
                    l2, acc2 = alpha * l + l2, alpha * acc + acc2
                return m2, l2, acc2

            def both(rows, kj, mask, carry):
                return tuple(step(e, k_ref[rows, sls[e]], v_ref[rows, sls[e]],
                                  ck_ref[0, e, kj] if forget else None, mask, carry[e]) for e in range(2))

            def below(kj, carry):
                return both(pl.ds(pl.multiple_of(kj * TK, TK), TK), kj, None, carry)

            carry = tuple(step(e, km_ref[:, sls[e]], vm_ref[:, sls[e]], cmk_ref[0, e] if forget else None,
                               mcol < N_META, (None, None, None)) for e in range(2))
            if qi:
                carry = lax.fori_loop(0, qi, below, carry)
            carry = both(slice(q0, q0 + TK), qi, causal, carry)
            outs = [acc / l for _, l, acc in carry]
            lses = [m + jnp.log2(l) for m, l, _ in carry]
            first = pltpu.roll(outs[0], HALF, 1) if mla else outs[0]
            o_ref[q0:q0 + TQ, :] = jnp.where(lo, first, outs[1])
            lse_ref[0, 0, q0:q0 + TQ, :] = jnp.where(two == 0, lses[0], lses[1])

    qk, kmeta, vv, vmeta, pair = _attn_specs(S, NX, qw, v_col0)
    specs = [qk, qk, kmeta, vv, vmeta]
    ins = [qn, kn, kn, vsrc, vsrc]
    if forget:
        specs += _cum_specs(S, TK)
        ins += [ck, cmk]
    specs.append(pl.BlockSpec(memory_space=pl.ANY))
    ins.append(_token_rows_buffer(T, NX, npair * LANES))
    lse_spec = pl.BlockSpec((1, 1, S, 2), lambda b, p: (b, p, 0, 0))
    return _call(
        body, name, (B, npair), specs, [pair, lse_spec],
        [jax.ShapeDtypeStruct((T, npair * LANES), F32), jax.ShapeDtypeStruct((B, npair, S, 2), F32)],
        ins, sem=("parallel", "parallel"), aliases={len(ins) - 1: 0}, exch=exch)


def _attn_bwd(qn, kn, vsrc, v_col0, o, lse, do, mla, scale, S, NX, name, ck=None, cmk=None, exch=None):
    T, W = qn.shape
    B = NX // S
    qw = 2 * LANES if mla else LANES
    npair = W // qw
    TQ = min(512, S)
    TK = TQ
    forget = ck is not None
    a = scale * LOG2E
    _TN = (((0,), (0,)), ((), ()))

    def body(*refs):
        refs = list(refs)
        q_ref, k_ref, km_ref, v_ref, vm_ref, o_ref, do_ref, lse_ref = refs[:8]
        pos = 8
        if forget:
            ck_ref, cmk_ref = refs[8:10]
            pos = 10
        pos += 3
        dq_ref, dk_ref, dv_ref, dkm_ref, dvm_ref = refs[pos:pos + 5]
        if forget:
            dck_ref, dcm_ref, dcq_ref = refs[pos + 5:pos + 8]
            dck_ref[...] = jnp.zeros_like(dck_ref)
            dcm_ref[...] = jnp.zeros_like(dcm_ref)
        dk_ref[...] = jnp.zeros_like(dk_ref)
        dv_ref[...] = jnp.zeros_like(dv_ref)
        dkm_ref[...] = jnp.zeros_like(dkm_ref)
        dvm_ref[...] = jnp.zeros_like(dvm_ref)
        lo = lax.broadcasted_iota(jnp.int32, (1, LANES), 1) < HALF
        mcol = lax.broadcasted_iota(jnp.int32, (TQ, META_BLK), 1)
        causal = (lax.broadcasted_iota(jnp.int32, (TQ, TK), 0)
                  >= lax.broadcasted_iota(jnp.int32, (TQ, TK), 1))
        two = lax.broadcasted_iota(jnp.int32, (TQ, 2), 1)
        for qi in range(S // TQ):
            q0 = qi * TQ
            dof = do_ref[q0:q0 + TQ, :]
            prod = dof * o_ref[q0:q0 + TQ, :]
            lse2 = lse_ref[0, 0, q0:q0 + TQ, :]
            sls = [slice(e * LANES, (e + 1) * LANES) if mla else slice(None) for e in range(2)]
            mine = [lo, ~lo]
            if mla:
                qts = [q_ref[q0:q0 + TQ, sl] for sl in sls]
                dots = [jnp.where(lo, 0.0, pltpu.roll(dof, HALF, 1) if e == 0 else dof).astype(BF16)
                        for e in range(2)]
            else:
                qts = [jnp.where(mine[e], q_ref[q0:q0 + TQ, :], 0.0).astype(BF16) for e in range(2)]
                dots = [jnp.where(mine[e], dof, 0.0).astype(BF16) for e in range(2)]
            deltas = [jnp.sum(jnp.where(mine[e], prod, 0.0), axis=1, keepdims=True) for e in range(2)]
            lse_ts = [jnp.sum(jnp.where(two == e, lse2, 0.0), axis=1, keepdims=True) for e in range(2)]

            def grads(e, kt, vt, c2, mask):
                s = lax.dot_general(qts[e], kt, _NT, preferred_element_type=F32) * a
                if forget:
                    s = s - c2
                p = jnp.exp2(s - lse_ts[e])
                if mask is not None:
                    p = jnp.where(mask, p, 0.0)
                dp = lax.dot_general(dots[e], vt, _NT, preferred_element_type=F32)
                ds = p * (dp - deltas[e])
                dsb = ds.astype(BF16)
                return (jnp.dot(dsb, kt, preferred_element_type=F32),
                        lax.dot_general(dsb, qts[e], _TN, preferred_element_type=F32) * scale,
                        lax.dot_general(p.astype(BF16), dots[e], _TN, preferred_element_type=F32),
                        -jnp.sum(ds, axis=0, keepdims=True) if forget else None,
                        jnp.sum(ds, axis=1, keepdims=True) if forget else None)

            def block(k_at, v_at, dk_at, dv_at, c_at, dc_at, mask, dqs):
                got = [grads(e, k_at(sls[e]), v_at(sls[e]).astype(BF16), c_at(e) if forget else None, mask)
                       for e in range(2)]
                if mla:
                    for e in range(2):
                        dk_at(sls[e], got[e][1])
                        dv_at(sls[e], got[e][2])
                else:
                    dk_at(sls[0], got[0][1] + got[1][1])
                    dv_at(sls[0], got[0][2] + got[1][2])
                if forget:
                    for e in range(2):
                        dc_at(e, got[e][3])
                picks = (0, 0, 4, 4) if forget else (0, 0)
                new = tuple(got[i % 2][k] for i, k in enumerate(picks))
                return new if dqs is None else tuple(x + y for x, y in zip(dqs, new))

            def add_to(ref, *lead):
                def add(*idx_and_val):
                    *idx, val = idx_and_val
                    ref[(*lead, *idx)] += val
                return add

            def token_block(rows, kj, mask, dqs):
                return block(lambda sl: k_ref[rows, sl], lambda sl: v_ref[rows, sl],
                             lambda sl, val: add_to(dk_ref)(rows, sl, val),
                             lambda sl, val: add_to(dv_ref)(rows, sl, val),
                             lambda e: ck_ref[0, e, kj], lambda e, val: add_to(dck_ref, 0)(e, kj, val),
                             mask, dqs)

            dqs = block(lambda sl: km_ref[:, sl], lambda sl: vm_ref[:, sl],
                        lambda sl, val: add_to(dkm_ref, 0)(slice(None), sl, val),
                        lambda sl, val: add_to(dvm_ref, 0)(slice(None), sl, val),
                        lambda e: cmk_ref[0, e], lambda e, val: add_to(dcm_ref, 0)(e, val),
                        mcol < N_META, None)

            def below(kj, dqs):
                return token_block(pl.ds(pl.multiple_of(kj * TK, TK), TK), kj, None, dqs)

            if qi:
                dqs = lax.fori_loop(0, qi, below, dqs)
            dqs = token_block(slice(q0, q0 + TK), qi, causal, dqs)
            if forget:
                dcq_ref[0, 0, q0:q0 + TQ, :] = jnp.where(two == 0, dqs[2], dqs[3])
            if mla:
                for e in range(2):
                    dq_ref[q0:q0 + TQ, sls[e]] = dqs[e] * scale
            else:
                dq_ref[q0:q0 + TQ, :] = jnp.where(lo, dqs[0], dqs[1]) * scale

    qk, kmeta, vv, vmeta, pair = _attn_specs(S, NX, qw, v_col0)
    lse_spec = pl.BlockSpec((1, 1, S, 2), lambda b, p: (b, p, 0, 0))
    specs = [qk, qk, kmeta, vv, vmeta, pair, pair, lse_spec]
    ins = [qn, kn, kn, vsrc, vsrc, o, do, lse]
    if forget:
        specs += _cum_specs(S, TK)
        ins += [ck, cmk]
    first_alias = len(ins)
    specs += [pl.BlockSpec(memory_space=pl.ANY)] * 3
    ins += [_token_rows_buffer(T, NX, W) for _ in range(3)]
    mspec = pl.BlockSpec((1, META_BLK, qw), lambda b, p: (b, 0, p))
    out_specs = [qk, qk, qk, mspec, mspec]
    tok = jax.ShapeDtypeStruct((T, W), F32)
    met = jax.ShapeDtypeStruct((B, META_BLK, W), F32)
    out_shape = [tok, tok, tok, met, met]
    if forget:
        out_specs += _cum_specs(S, TK) + [lse_spec]
        out_shape += [jax.ShapeDtypeStruct((B, HEADS, S // TK, 1, TK), F32),
                      jax.ShapeDtypeStruct((B, HEADS, 1, META_BLK), F32),
                      jax.ShapeDtypeStruct((B, npair, S, 2), F32)]
    return _call(
        body, name, (B, npair), specs, out_specs, out_shape, ins, sem=("parallel", "parallel"),
        aliases={first_alias: 0, first_alias + 1: 1, first_alias + 2: 2}, exch=exch)


def _gate_fwd(z, bg, of, om, name):
    T, D = of.shape
    tm = _tile(T, 640, 16)

    def body(z_ref, bg_ref, of_ref, om_ref, o_ref):
        bias = bg_ref[...]

        def step(rows, carry):
            gt = jax.nn.sigmoid(z_ref[rows, :] + bias)
            o_ref[rows, :] = (gt[:, :D] * of_ref[rows, :] + gt[:, D:] * om_ref[rows, :]).astype(BF16)
            return carry

        _by_chunks(tm, D, step)

    row = pl.BlockSpec((tm, D), lambda i: (i, 0))
    return pl.pallas_call(
        body, name=name, grid=(T // tm,),
        in_specs=[pl.BlockSpec((tm, 2 * D), lambda i: (i, 0)),
                  pl.BlockSpec((1, 2 * D), lambda i: (0, 0)), row, row],
        out_specs=row, out_shape=jax.ShapeDtypeStruct((T, D), BF16),
        compiler_params=_params(("parallel",)),
    )(z, bg, of, om)


def _gate_bwd(dmix, z, bg, of, om, name):
    T, D = of.shape
    tm = _tile(T, 640, 16)

    def body(dm_ref, z_ref, bg_ref, of_ref, om_ref, dgl_ref, dof_ref, dom_ref, dbg_ref):
        bias = bg_ref[...]

        def step(rows, dbg_sum):
            gt = jax.nn.sigmoid(z_ref[rows, :] + bias)
            dm = dm_ref[rows, :]
            dof_ref[rows, :] = (dm * gt[:, :D]).astype(BF16)
            dom_ref[rows, :] = (dm * gt[:, D:]).astype(BF16)
            dgl = jnp.concatenate([dm * of_ref[rows, :], dm * om_ref[rows, :]], axis=1) * gt * (1.0 - gt)
            dgl_ref[rows, :] = dgl.astype(BF16)
            return dbg_sum + jnp.sum(dgl, axis=0, keepdims=True)

        dbg = _by_chunks(tm, D, step, jnp.zeros((1, 2 * D), F32))

        @pl.when(pl.program_id(0) == 0)
        def _():
            dbg_ref[...] = jnp.zeros_like(dbg_ref)

        dbg_ref[...] += dbg

    row = pl.BlockSpec((tm, D), lambda i: (i, 0))
    wide = pl.BlockSpec((tm, 2 * D), lambda i: (i, 0))
    one = pl.BlockSpec((1, 2 * D), lambda i: (0, 0))
    return pl.pallas_call(
        body, name=name, grid=(T // tm,),
        in_specs=[row, wide, one, row, row], out_specs=(wide, row, row, one),
        out_shape=(jax.ShapeDtypeStruct((T, 2 * D), BF16), jax.ShapeDtypeStruct((T, D), BF16),
                   jax.ShapeDtypeStruct((T, D), BF16), jax.ShapeDtypeStruct((1, 2 * D), F32)),
        compiler_params=_params(("arbitrary",)),
    )(dmix, z, bg, of, om)


def _loss(h, tgt, name):
    T, D = h.shape
    NX = tgt.shape[0]
    tm = _tile(NX, 640, 16)

    def body(h_ref, t_ref, _, __, dh_ref, dhb_ref, l_ref):
        i = pl.program_id(0)

        def step(rows, part):
            err = h_ref[rows, :] - t_ref[rows, :]
            dh = err * (1.0 / D)
            dh_ref[rows, :] = dh
            dhb_ref[rows, :] = dh.astype(BF16)
            return part + jnp.sum(err * err, axis=0, keepdims=True)

        part = _by_chunks(tm, D, step, jnp.zeros((1, D), F32))

        @pl.when(i == 0)
        def _():
            l_ref[...] = jnp.zeros_like(l_ref)

        l_ref[...] += 0.5 * jnp.sum(part) * (1.0 / D)

    row = pl.BlockSpec((tm, D), lambda i: (i, 0))
    acc = pl.BlockSpec((8, LANES), lambda i: (0, 0))
    hbm = pl.BlockSpec(memory_space=pl.ANY)
    return pl.pallas_call(
        body, name=name, grid=(NX // tm,), in_specs=[row, row, hbm, hbm], out_specs=(row, row, acc),
        out_shape=(jax.ShapeDtypeStruct((T, D), F32), jax.ShapeDtypeStruct((T, D), BF16),
                   jax.ShapeDtypeStruct((8, LANES), F32)),
        input_output_aliases={2: 0, 3: 1},
        compiler_params=_params(("arbitrary",)),
    )(h, tgt, _token_rows_buffer(T, NX, D), _token_rows_buffer(T, NX, D, BF16))


def _adamw(parts, w, m, v, name):
    P, R, C = parts.shape
    tr = _tile(R, max(8, (1 << 18) // C), 8)
    bc1 = 1.0 - ADAM_B1 ** ADAM_STEP
    bc2 = 1.0 - ADAM_B2 ** ADAM_STEP

    def body(p_ref, w_ref, m_ref, v_ref, g_ref, d_ref, m2_ref, v2_ref):
        def step(rows, carry):
            g = p_ref[0, rows, :].astype(F32)
            for j in range(1, P):
                g = g + p_ref[j, rows, :].astype(F32)
            m2 = ADAM_B1 * m_ref[rows, :] + (1.0 - ADAM_B1) * g
            v2 = ADAM_B2 * v_ref[rows, :] + (1.0 - ADAM_B2) * (g * g)
            m_hat = m2 / bc1
            v_hat = v2 / bc2
            g_ref[rows, :] = g
            d_ref[rows, :] = -ADAM_LR * (m_hat / (jnp.sqrt(v_hat) + ADAM_EPS) + ADAM_WD * w_ref[rows, :])
            m2_ref[rows, :] = m2
            v2_ref[rows, :] = v2
            return carry

        _by_chunks(tr, C, step)

    row = pl.BlockSpec((tr, C), lambda i: (i, 0))
    sh = jax.ShapeDtypeStruct((R, C), F32)
    return pl.pallas_call(
        body, name=name, grid=(R // tr,),
        in_specs=[pl.BlockSpec((P, tr, C), lambda i: (0, i, 0)), row, row, row],
        out_specs=(row, row, row, row), out_shape=(sh, sh, sh, sh),
        compiler_params=_params(("parallel",)),
    )(parts, w, m, v)


def _peer(d):
    x, y, c = lax.axis_index("x"), lax.axis_index("y"), lax.axis_index("c")
    px = 1 - x if d & 4 else x
    py = 1 - y if d & 2 else y
    pc = 1 - c if d & 1 else c
    return (px, py, pc), 4 * px + 2 * py + pc


class _Exchange:
    def __init__(self, srcs, gather):
        self.srcs, self.gather, self.n = list(srcs), gather, len(srcs)
        n = self.n
        hbm = pl.BlockSpec(memory_space=pl.ANY)
        self.in_specs = [hbm] * n
        self.out_specs = [hbm] * n
        self.out_shape = [jax.ShapeDtypeStruct((N_DEV,) + s.shape[-2:], s.dtype) for s in srcs]
        self.scratch = [pltpu.SemaphoreType.DMA((N_DEV - 1, n)), pltpu.SemaphoreType.DMA((N_DEV - 1, n)),
                        pltpu.SemaphoreType.DMA((n,))]

    def _copies(self, src_refs, out_refs, sems):
        send_sems, recv_sems, local_sems = sems
        _, me = _peer(0)

        def remote(w, d, landing):
            dev, lin = _peer(d)
            return pltpu.make_async_remote_copy(
                src_ref=src_refs[w] if self.gather else src_refs[w].at[lin],
                dst_ref=out_refs[w].at[lin if landing else me],
                send_sem=send_sems.at[d - 1, w], recv_sem=recv_sems.at[d - 1, w],
                device_id=dev, device_id_type=pl.DeviceIdType.MESH)

        pairs = [(w, d) for d in range(1, N_DEV) for w in range(self.n)]
        own = [pltpu.make_async_copy(src_refs[w] if self.gather else src_refs[w].at[me],
                                     out_refs[w].at[me], local_sems.at[w]) for w in range(self.n)]
        return own, [remote(w, d, False) for w, d in pairs], [remote(w, d, True) for w, d in pairs]

    def _gather_copies(self, src_refs, out_refs, sems):
        send_sems, recv_sems, local_sems = sems
        x, y, c = lax.axis_index("x"), lax.axis_index("y"), lax.axis_index("c")
        me, sibling = (x, y, c), (x, y, 1 - c)
        chips = [(1 - x, y), (x, 1 - y), (1 - x, 1 - y)]

        def copy(w, k, block, to, src=None):
            rows = out_refs[w].at[4 * block[0] + 2 * block[1] + block[2]]
            return pltpu.make_async_remote_copy(
                src_ref=rows if src is None else src, dst_ref=rows,
                send_sem=send_sems.at[k, w], recv_sem=recv_sems.at[k, w],
                device_id=to, device_id_type=pl.DeviceIdType.MESH)

        ws = range(self.n)
        own = [pltpu.make_async_copy(src_refs[w], out_refs[w].at[4 * x + 2 * y + c], local_sems.at[w])
               for w in ws]
        first = [copy(w, 0, me, sibling, src_refs[w]) for w in ws]
        first += [copy(w, 1 + j, me, (*chip, c), src_refs[w]) for j, chip in enumerate(chips) for w in ws]
        landed = [[copy(w, 1 + j, (*chip, c), me) for w in ws] for j, chip in enumerate(chips)]
        passed = [[copy(w, 4 + j, (*chip, c), sibling) for w in ws] for j, chip in enumerate(chips)]
        from_sibling = [copy(w, 0, sibling, me) for w in ws]
        from_sibling += [copy(w, 4 + j, (*chip, 1 - c), me) for j, chip in enumerate(chips) for w in ws]
        return own, first, landed, passed, from_sibling

    def start(self, src_refs, out_refs, sems):
        if self.gather:
            own, first = self._gather_copies(src_refs, out_refs, sems)[:2]
            sent = first
        else:
            own, sent, _ = self._copies(src_refs, out_refs, sems)
        for cp in own + sent:
            cp.start()

    def wait(self, src_refs, out_refs, sems):
        if self.gather:
            own, first, landed, passed, from_sibling = self._gather_copies(src_refs, out_refs, sems)
            for arrived, onward in zip(landed, passed):
                for cp in arrived:
                    cp.wait_recv()
                for cp in onward:
                    cp.start()
            for cp in from_sibling:
                cp.wait_recv()
            for cp in first + [cp for group in passed for cp in group]:
                cp.wait_send()
        else:
            own, sent, landing = self._copies(src_refs, out_refs, sems)
            for cp in landing:
                cp.wait_recv()
            for cp in sent:
                cp.wait_send()
        for cp in own:
            cp.wait()


def _exchange(srcs, name, gather):
    ex = _Exchange(srcs, gather)
    n = ex.n

    def body(*refs):
        ex.start(refs[:n], refs[n:2 * n], refs[2 * n:])
        ex.wait(refs[:n], refs[n:2 * n], refs[2 * n:])

    outs = pl.pallas_call(
        body, name=name, in_specs=ex.in_specs, out_specs=tuple(ex.out_specs),
        out_shape=tuple(ex.out_shape), scratch_shapes=ex.scratch,
    )(*srcs)
    return list(outs)


def _call(body, name, grid, in_specs, out_specs, out_shape, ins, scratch_shapes=(), sem=None,
          aliases=None, exch=None):
    aliases = aliases or {}
    if exch is None:
        outs = pl.pallas_call(
            body, name=name, grid=grid, in_specs=list(in_specs), out_specs=tuple(out_specs),
            out_shape=tuple(out_shape), scratch_shapes=list(scratch_shapes),
            input_output_aliases=aliases, compiler_params=_params(sem),
        )(*ins)
        return list(outs), []
    ni, no, ns, n = len(in_specs), len(out_specs), len(scratch_shapes), exch.n
    last_ids = [g - 1 for g in grid]

    def hosted(*refs):
        cin, xin = refs[:ni], refs[ni:ni + n]
        cout, xout = refs[ni + n:ni + n + no], refs[ni + n + no:ni + 2 * n + no]
        cscr, xsem = refs[ni + 2 * n + no:ni + 2 * n + no + ns], refs[ni + 2 * n + no + ns:]
        ids = [pl.program_id(a) for a in range(len(grid))]
        first, last = ids[0] == 0, ids[0] == last_ids[0]
        for a in range(1, len(grid)):
            first, last = first & (ids[a] == 0), last & (ids[a] == last_ids[a])

        @pl.when(first)
        def _():
            exch.start(xin, xout, xsem)

        body(*cin, *cout, *cscr)

        @pl.when(last)
        def _():
            exch.wait(xin, xout, xsem)

    outs = pl.pallas_call(
        hosted, name=name, grid=grid, in_specs=list(in_specs) + exch.in_specs,
        out_specs=tuple(list(out_specs) + exch.out_specs),
        out_shape=tuple(list(out_shape) + exch.out_shape),
        scratch_shapes=list(scratch_shapes) + exch.scratch, input_output_aliases=aliases,
        compiler_params=_params(("arbitrary",) * len(grid)),
    )(*ins, *exch.srcs)
    return list(outs[:no]), list(outs[no:])


def _pack(arrs, cols, row_mult):
    flat = jnp.concatenate([a.reshape(-1) for a in arrs])
    n = flat.shape[0]
    quantum = cols * row_mult
    total = -(-n // quantum) * quantum
    return jnp.pad(flat, (0, total - n)).reshape(total // cols, cols)


def _pack_rows(arrs, cols, row_mult):
    flat = jnp.concatenate(arrs, axis=1)
    n = flat.shape[1]
    quantum = cols * row_mult
    total = -(-n // quantum) * quantum
    return jnp.pad(flat, ((0, 0), (0, total - n))).reshape(N_DEV, total // cols, cols)


def _unpack(packed, shapes):
    flat = packed.reshape(-1)
    out, off = [], 0
    for s in shapes:
        n = int(np.prod(s))
        out.append(flat[off:off + n].reshape(s))
        off += n
    return out


def _rope_tables(positions):
    inv_freq = ROPE_THETA ** (-jnp.arange(0, MLA_ROPE, 2, dtype=F32) / MLA_ROPE)
    ang = positions.astype(F32)[:, None] * inv_freq[None, :]
    cos, sin = jnp.cos(ang), jnp.sin(ang)
    n = positions.shape[0]
    ones, zeros = jnp.ones((n, MLA_NOPE), F32), jnp.zeros((n, MLA_NOPE), F32)
    tail1, tail0 = jnp.ones((n, LANES - MLA_QK), F32), jnp.zeros((n, LANES - MLA_QK), F32)
    z16 = jnp.zeros((n, 16), F32)
    c = jnp.concatenate([ones, cos, cos, tail1], axis=1)
    s1 = jnp.concatenate([zeros, -sin, z16, tail0], axis=1)
    s2 = jnp.concatenate([zeros, z16, sin, tail0], axis=1)
    return c, s1, s2


def kernel(x, meta_tokens, ffn1_norm, ffn1_w_gu, ffn1_w_down, mix_norm, w_in, b_forget, b_gate, fox_q_norm, fox_k_norm, mla_cq_norm, mla_w_uq, mla_ckv_norm, mla_w_ukv, mla_q_norm, mla_k_norm, w_branch_fox, w_branch_mla, w_out, ffn2_norm, ffn2_w_gu, ffn2_w_down, loss_target, m_meta_tokens, m_ffn1_norm, m_ffn1_w_gu, m_ffn1_w_down, m_mix_norm, m_w_in, m_b_forget, m_b_gate, m_fox_q_norm, m_fox_k_norm, m_mla_cq_norm, m_mla_w_uq, m_mla_ckv_norm, m_mla_w_ukv, m_mla_q_norm, m_mla_k_norm, m_w_branch_fox, m_w_branch_mla, m_w_out, m_ffn2_norm, m_ffn2_w_gu, m_ffn2_w_down, v_meta_tokens, v_ffn1_norm, v_ffn1_w_gu, v_ffn1_w_down, v_mix_norm, v_w_in, v_b_forget, v_b_gate, v_fox_q_norm, v_fox_k_norm, v_mla_cq_norm, v_mla_w_uq, v_mla_ckv_norm, v_mla_w_ukv, v_mla_q_norm, v_mla_k_norm, v_w_branch_fox, v_w_branch_mla, v_w_out, v_ffn2_norm, v_ffn2_w_gu, v_ffn2_w_down):
    names = ["meta_tokens", "ffn1_norm", "ffn1_w_gu", "ffn1_w_down", "mix_norm", "w_in", "b_forget",
             "b_gate", "fox_q_norm", "fox_k_norm", "mla_cq_norm", "mla_w_uq", "mla_ckv_norm",
             "mla_w_ukv", "mla_q_norm", "mla_k_norm", "w_branch_fox", "w_branch_mla", "w_out",
             "ffn2_norm", "ffn2_w_gu", "ffn2_w_down"]
    W = dict(zip(names, [meta_tokens, ffn1_norm, ffn1_w_gu, ffn1_w_down, mix_norm, w_in, b_forget,
                         b_gate, fox_q_norm, fox_k_norm, mla_cq_norm, mla_w_uq, mla_ckv_norm,
                         mla_w_ukv, mla_q_norm, mla_k_norm, w_branch_fox, w_branch_mla, w_out,
                         ffn2_norm, ffn2_w_gu, ffn2_w_down]))
    Mo = dict(zip(names, [m_meta_tokens, m_ffn1_norm, m_ffn1_w_gu, m_ffn1_w_down, m_mix_norm, m_w_in,
                          m_b_forget, m_b_gate, m_fox_q_norm, m_fox_k_norm, m_mla_cq_norm,
                          m_mla_w_uq, m_mla_ckv_norm, m_mla_w_ukv, m_mla_q_norm, m_mla_k_norm,
                          m_w_branch_fox, m_w_branch_mla, m_w_out, m_ffn2_norm, m_ffn2_w_gu,
                          m_ffn2_w_down]))
    Vo = dict(zip(names, [v_meta_tokens, v_ffn1_norm, v_ffn1_w_gu, v_ffn1_w_down, v_mix_norm, v_w_in,
                          v_b_forget, v_b_gate, v_fox_q_norm, v_fox_k_norm, v_mla_cq_norm,
                          v_mla_w_uq, v_mla_ckv_norm, v_mla_w_ukv, v_mla_q_norm, v_mla_k_norm,
                          v_w_branch_fox, v_w_branch_mla, v_w_out, v_ffn2_norm, v_ffn2_w_gu,
                          v_ffn2_w_down]))

    B, S, D = x.shape
    NX = B * S
    T = NX + META_BLK
    H = HEADS
    assert NX % META_BLK == 0 and S % LANES == 0
    me = 4 * lax.axis_index("x") + 2 * lax.axis_index("y") + lax.axis_index("c")

    big = [("ffn1_w_gu", 1), ("ffn1_w_down", 0), ("w_in", 1), ("mla_w_uq", 1), ("mla_w_ukv", 1),
           ("w_branch_fox", 1), ("w_branch_mla", 1), ("w_out", 0), ("ffn2_w_gu", 1), ("ffn2_w_down", 0)]
    mix_small = ["mla_w_uq", "mla_w_ukv", "w_branch_fox", "w_branch_mla", "w_out"]
    last_group = ["ffn2_w_gu", "ffn2_w_down"]
    axis_of = dict(big)
    full = {}

    def shards(group):
        return [W[n][0].astype(BF16) for n in group]

    def assemble(group, blks):
        for n, blk in zip(group, blks):
            _, r, c = blk.shape
            full[n] = (blk.transpose(1, 0, 2).reshape(r, N_DEV * c) if axis_of[n] == 1
                       else blk.reshape(N_DEV * r, c))

    got = _exchange(shards(["ffn1_w_gu"]) + [meta_tokens], "gather_first", gather=True)
    assemble(["ffn1_w_gu"], got[:1])
    meta_full = got[1].transpose(1, 0, 2).reshape(N_META, D)

    Z_G, Z_FQ = 0, 2 * D
    Z_FK, Z_FV = Z_FQ + FOX_W, Z_FQ + 2 * FOX_W
    Z_CQ = Z_FQ + 3 * FOX_W
    Z_CKV = Z_CQ + MLA_Q_RANK
    Z_F = Z_CKV + MLA_KV_RANK
    Z_KR = Z_F + LANES

    def pad_lanes(a, w=LANES):
        return jnp.pad(a, [(0, 0)] * (a.ndim - 1) + [(0, w - a.shape[-1])])

    def rows_T(real, meta=None):
        n = real.shape[1]
        parts = [real]
        used = 0
        if meta is not None:
            parts.append(meta)
            used = meta.shape[0]
        if T - NX - used:
            parts.append(jnp.zeros((T - NX - used, n), real.dtype))
        return jnp.concatenate(parts, axis=0)

    def put_meta(tok, meta_per_seq):
        return lax.dynamic_update_slice(tok, meta_per_seq.sum(0), (NX, 0))

    h0 = rows_T(x.reshape(NX, D), meta_full)
    tgt = loss_target.reshape(NX, D)

    def ffn_fwd(h, norm, w_gu, tag, behind_up=None, behind_down=None):
        u = _norm_fwd(h, 0, D, D, norm, D, D, tag + "_norm")
        (sa, sb, a), got = _ffn_up(u, w_gu, tag + "_up",
                                  exch=_Exchange(shards(behind_up), True) if behind_up else None)
        assemble(behind_up or [], got)
        h_out = _mm(a, full[tag + "_w_down"], "nn", tag + "_down", scale=0.5, res=h,
                    exch=_Exchange(shards(behind_down), True) if behind_down else None)
        if behind_down:
            h_out, got = h_out
            assemble(behind_down, got)
        return h_out, (u, sa, sb, a)

    h1, ffn1_saved = ffn_fwd(h0, W["ffn1_norm"], full["ffn1_w_gu"], "ffn1",
                             behind_up=["ffn1_w_down"], behind_down=["w_in"])
    wi = full["w_in"]
    o_fq = 0
    o_f = 3 * FOX_W
    o_cq = o_f + HEADS
    o_kr = o_cq + MLA_Q_RANK + MLA_KV_RANK
    o_g = o_kr + MLA_ROPE
    w_in_p = jnp.concatenate([
        wi[:, o_g:o_g + 2 * D], wi[:, o_fq:o_f], wi[:, o_cq:o_kr],
        jnp.pad(wi[:, o_f:o_cq], ((0, 0), (0, LANES - HEADS))),
        jnp.pad(wi[:, o_kr:o_g], ((0, 0), (0, LANES - MLA_ROPE)))], axis=1)

    u2 = _norm_fwd(h1, 0, D, D, W["mix_norm"], D, D, "mix_norm")
    z, got = _mm(u2, w_in_p, "nn", "w_in", exch=_Exchange(shards(mix_small), True))
    assemble(mix_small, got)
    w_uq_p = jnp.pad(full["mla_w_uq"].reshape(MLA_Q_RANK, H, MLA_QK),
                     ((0, 0), (0, 0), (0, LANES - MLA_QK))).reshape(MLA_Q_RANK, H * LANES)

    gq_f = jnp.tile(W["fox_q_norm"], (1, 2))
    gk_f = jnp.tile(W["fox_k_norm"], (1, 2))
    fqn = _norm_fwd(z, Z_FQ, FOX_W, LANES, gq_f, FOX_HD, FOX_HD, "fox_q_norm")
    fkn = _norm_fwd(z, Z_FK, FOX_W, LANES, gk_f, FOX_HD, FOX_HD, "fox_k_norm")
    fl = z[:NX, Z_F:Z_F + LANES].reshape(B, S, LANES)
    flm = z[NX:, Z_F:Z_F + LANES]
    bf = pad_lanes(W["b_forget"])
    cum, cumm = _cum_fwd(fl, flm, bf, "forget_cum")
    TK = min(512, S)
    ck = cum[:, :, :H].transpose(0, 2, 1).reshape(B, H, S // TK, 1, TK)
    cmk = jnp.broadcast_to(cumm[:, :H].T[None, :, None, :], (B, H, 1, META_BLK))
    (o_fox, lse_fox), got = _attn_fwd(fqn, fkn, z, Z_FV, False, FOX_HD ** -0.5, S, NX, "fox_attn", ck, cmk,
                                      exch=_Exchange(shards(last_group), True))
    assemble(last_group, got)
    of = _mm(o_fox, full["w_branch_fox"], "nn", "branch_fox")

    pos = jnp.concatenate([jnp.tile(jnp.arange(S) + N_META, B), jnp.arange(META_BLK)])
    tabs = _rope_tables(pos)
    cqn = _norm_fwd(z, Z_CQ, MLA_Q_RANK, MLA_Q_RANK, W["mla_cq_norm"], MLA_Q_RANK, MLA_Q_RANK, "mla_cq_norm")
    q_lin = _mm(cqn, w_uq_p, "nn", "mla_uq")
    ckvn = _norm_fwd(z, Z_CKV, MLA_KV_RANK, MLA_KV_RANK, W["mla_ckv_norm"], MLA_KV_RANK, MLA_KV_RANK,
                     "mla_ckv_norm")
    kv_lin = _mm(ckvn, full["mla_w_ukv"], "nn", "mla_ukv")
    gq_m, gk_m = pad_lanes(W["mla_q_norm"]), pad_lanes(W["mla_k_norm"])
    mqn = _norm_fwd(q_lin, 0, H * LANES, LANES, gq_m, LANES, MLA_QK, "mla_q_norm", tabs=tabs)
    mkn = _mla_k_fwd(kv_lin, z, Z_KR, gk_m, tabs, "mla_k_norm")
    (o_mla, lse_mla), _ = _attn_fwd(mqn, mkn, kv_lin, 0, True, MLA_QK ** -0.5, S, NX, "mla_attn")
    om = _mm(o_mla, full["w_branch_mla"], "nn", "branch_mla")

    mix = _gate_fwd(z, W["b_gate"], of, om, "gate_mix")
    h2 = _mm(mix, full["w_out"], "nn", "w_out", res=h1)

    h3, ffn2_saved = ffn_fwd(h2, W["ffn2_norm"], full["ffn2_w_gu"], "ffn2")

    dh3, dh3_b, loss_acc = _loss(h3, tgt, "loss")
    loss = lax.psum(loss_acc[0, 0], AXES)

    G = {}
    parts = {}

    def scatter_of(group):
        per_dest = []
        for n in group:
            r, c = W[n].shape[1:]
            per_dest.append((G[n].reshape(r, N_DEV, c).transpose(1, 0, 2) if axis_of[n] == 1
                             else G[n].reshape(N_DEV, r, c)).astype(BF16))
        return _Exchange(per_dest, False)

    def ffn_bwd(dh, dh_b, h, norm, w_gu, w_down, saved, tag, behind_down=None, spread=False):
        u, sa, sb, a = saved
        G[tag + "_w_down"] = _mm(a, dh_b, "tn", tag + "_dw_down", scale=0.5)
        (dg, dup), got = _ffn_down_bwd(dh_b, w_down, sa, sb, tag + "_down_bwd",
                                       exch=scatter_of(behind_down) if behind_down else None)
        parts.update(zip(behind_down or [], got))
        dw_g = _mm(u, dg, "tn", tag + "_dw_g", exch=scatter_of([tag + "_w_down"]) if spread else None)
        if spread:
            dw_g, got = dw_g
            parts[tag + "_w_down"] = got[0]
        G[tag + "_w_gu"] = jnp.concatenate([dw_g, _mm(u, dup, "tn", tag + "_dw_u")], axis=1)
        (du,), got = _ffn_up_bwd_dx(dg, dup, w_gu, tag + "_up_bwd",
                                    exch=scatter_of([tag + "_w_gu"]) if spread else None)
        if spread:
            parts[tag + "_w_gu"] = got[0]
        *dh_in, G[tag + "_norm"] = _norm_bwd(h, 0, D, D, norm, D, D, du, tag + "_norm_bwd", res=dh,
                                             bf16_copy=not spread)
        return dh_in

    dh2, dh2_b = ffn_bwd(dh3, dh3_b, h2, W["ffn2_norm"], full["ffn2_w_gu"], full["ffn2_w_down"], ffn2_saved,
                         "ffn2")

    G["w_out"] = _mm(mix, dh2_b, "tn", "dw_out")
    dmix = _mm(dh2_b, full["w_out"], "nt", "w_out_bwd")
    dgl, dof, dom, G["b_gate"] = _gate_bwd(dmix, z, W["b_gate"], of, om, "gate_bwd")

    G["w_branch_fox"] = _mm(o_fox, dof, "tn", "dw_branch_fox")
    do_fox = _mm(dof, full["w_branch_fox"], "nt", "branch_fox_bwd")
    (dq_f, dk_f, dv_f, dkm_f, dvm_f, dck, dcmk, dcq), got = _attn_bwd(
        fqn, fkn, z, Z_FV, o_fox, lse_fox, do_fox, False, FOX_HD ** -0.5, S, NX, "fox_attn_bwd", ck, cmk,
        exch=scatter_of(last_group))
    parts.update(zip(last_group, got))
    dk_f, dv_f = put_meta(dk_f, dkm_f), put_meta(dv_f, dvm_f)
    dfq, gq = _norm_bwd(z, Z_FQ, FOX_W, LANES, gq_f, FOX_HD, FOX_HD, dq_f, "fox_q_norm_bwd", out_dtype=BF16)
    dfk, gk = _norm_bwd(z, Z_FK, FOX_W, LANES, gk_f, FOX_HD, FOX_HD, dk_f, "fox_k_norm_bwd", out_dtype=BF16)
    G["fox_q_norm"] = gq[:, :FOX_HD] + gq[:, FOX_HD:]
    G["fox_k_norm"] = gk[:, :FOX_HD] + gk[:, FOX_HD:]
    dc = pad_lanes(dck.reshape(B, H, S).transpose(0, 2, 1)
                   + dcq.transpose(0, 2, 1, 3).reshape(B, S, H))
    dcm = pad_lanes(dcmk.sum(0)[:, 0, :].T)
    dcm = jnp.where(jnp.arange(LANES)[:, None] < N_META, dcm, 0.0)
    dfl, dflm, dbf = _cum_bwd(dc, dcm, fl, flm, bf, "forget_cum_bwd")
    G["b_forget"] = dbf[:, :HEADS]
    dfl_t = rows_T(dfl.reshape(NX, LANES), dflm)

    G["w_branch_mla"] = _mm(o_mla, dom, "tn", "dw_branch_mla")
    do_mla = _mm(dom, full["w_branch_mla"], "nt", "branch_mla_bwd")
    (dq_m, dk_m, dvk, dkm_m, dvkm), _ = _attn_bwd(
        mqn, mkn, kv_lin, 0, o_mla, lse_mla, do_mla, True, MLA_QK ** -0.5, S, NX, "mla_attn_bwd")
    dk_m, dvk = put_meta(dk_m, dkm_m), put_meta(dvk, dvkm)
    dq_lin, gq = _norm_bwd(q_lin, 0, H * LANES, LANES, gq_m, LANES, MLA_QK, dq_m, "mla_q_norm_bwd", tabs=tabs,
                           out_dtype=BF16)
    G["mla_q_norm"] = gq[:, :MLA_QK]
    G["mla_w_uq"] = _mm(cqn, dq_lin, "tn", "dw_uq").reshape(MLA_Q_RANK, H, LANES)[:, :, :MLA_QK].reshape(
        MLA_Q_RANK, H * MLA_QK)
    dcqn = _mm(dq_lin, w_uq_p, "nt", "mla_uq_bwd")
    dcq, G["mla_cq_norm"] = _norm_bwd(z, Z_CQ, MLA_Q_RANK, MLA_Q_RANK, W["mla_cq_norm"], MLA_Q_RANK,
                                      MLA_Q_RANK, dcqn, "mla_cq_norm_bwd", out_dtype=BF16)
    dkv_lin, dkr, gk = _mla_k_bwd(kv_lin, z, Z_KR, gk_m, tabs, dk_m, dvk, "mla_k_norm_bwd")
    G["mla_k_norm"] = gk[:, :MLA_QK]
    G["mla_w_ukv"] = _mm(ckvn, dkv_lin, "tn", "dw_ukv")
    dckvn = _mm(dkv_lin, full["mla_w_ukv"], "nt", "mla_ukv_bwd")
    dckv, G["mla_ckv_norm"] = _norm_bwd(z, Z_CKV, MLA_KV_RANK, MLA_KV_RANK, W["mla_ckv_norm"], MLA_KV_RANK,
                                        MLA_KV_RANK, dckvn, "mla_ckv_norm_bwd", out_dtype=BF16)

    dz = jnp.concatenate([dgl, dfq, dfk, dv_f.astype(BF16), dcq, dckv, dfl_t.astype(BF16),
                          dkr.astype(BF16)], axis=1)
    dw_in_p = _mm(u2, dz, "tn", "dw_in")
    G["w_in"] = jnp.concatenate([
        dw_in_p[:, Z_FQ:Z_CQ], dw_in_p[:, Z_F:Z_F + HEADS], dw_in_p[:, Z_CQ:Z_F],
        dw_in_p[:, Z_KR:Z_KR + MLA_ROPE], dw_in_p[:, Z_G:Z_G + 2 * D]], axis=1)
    du2, got = _mm(dz, w_in_p, "nt", "w_in_bwd", exch=scatter_of(mix_small))
    parts.update(zip(mix_small, got))
    dh1, dh1_b, G["mix_norm"] = _norm_bwd(h1, 0, D, D, W["mix_norm"], D, D, du2, "mix_norm_bwd", res=dh2,
                                          bf16_copy=True)

    (dh0,) = ffn_bwd(dh1, dh1_b, h0, W["ffn1_norm"], full["ffn1_w_gu"], full["ffn1_w_down"], ffn1_saved, "ffn1",
                     behind_down=["w_in"], spread=True)
    grad_x = dh0[:NX].reshape(B, S, D)
    G["meta_tokens"] = dh0[NX:NX + N_META]

    res = {}
    for n, _ in big:
        outs4 = _adamw(parts[n], W[n][0], Mo[n][0], Vo[n][0], "adamw_" + n)
        for key, arr in zip(("g", "d", "m", "v"), outs4):
            res[key, n] = arr[None]

    small = [n for n in names if n not in dict(big) and n != "meta_tokens"]
    small_shapes = [W[n].shape for n in small]
    spack = _pack([G["meta_tokens"]] + [G[n] for n in small], 1024, 8)
    (sparts,) = _exchange([spack], "gather_small_grads", gather=True)
    sflat = sparts.reshape(N_DEV, -1)
    dsh = D // N_DEV
    meta_part = lax.dynamic_slice(sflat[:, :N_META * D].reshape(N_DEV, N_META, D),
                                  (0, 0, me * dsh), (N_DEV, N_META, dsh)).reshape(N_DEV, -1)
    rep_len = sum(int(np.prod(s)) for s in small_shapes)
    rep_part = sflat[:, N_META * D:N_META * D + rep_len]
    sp = _pack_rows([meta_part, rep_part], LANES, _chunk_rows(LANES))
    pks = lambda src: _pack([src["meta_tokens"]] + [src[n] for n in small], LANES, _chunk_rows(LANES))
    g_s, d_s, m_s, v_s = _adamw(sp, pks(W), pks(Mo), pks(Vo), "adamw_small")
    shapes_s = [W["meta_tokens"].shape] + small_shapes
    for key, packed in (("g", g_s), ("d", d_s), ("m", m_s), ("v", v_s)):
        for n, arr in zip(["meta_tokens"] + small, _unpack(packed, shapes_s)):
            res[key, n] = arr

    outs = [loss, grad_x]
    for key in ("g", "d", "m", "v"):
        outs += [res[key, n] for n in names]
    return tuple(outs)
```
